```python
import math
import jax, jax.numpy as jnp
from jax import lax
import numpy as np

D_MODEL = 1024
BATCH = 4
SEQ = 8192
DEPTH = 4

GRID_W = 64
N_FGROUPS = 4
FGROUP_DIM = 128
F_WIDTH = N_FGROUPS * FGROUP_DIM
N_HEADS = 8
N_KV_HEADS = 2
HEAD_DIM = 64
Q_GROUP = N_HEADS // N_KV_HEADS
Q_WIDTH = N_HEADS * HEAD_DIM
KV_WIDTH = N_KV_HEADS * HEAD_DIM
Q_BLOCK = 128
ROPE_THETA = 10000.0
ROPE_PAIRS = HEAD_DIM // 4
EVEN_IN_WIDTH = F_WIDTH + Q_WIDTH + 2 * KV_WIDTH
EVEN_MIX_WIDTH = F_WIDTH + Q_WIDTH
CHUNK = 128
N_CGROUPS = 8
CGROUP_DIM = D_MODEL // N_CGROUPS
C_WIDTH = N_CGROUPS * CGROUP_DIM
N_EXPERTS = 64
EXPERT_DIM = 256
TOP_K = 8
N_EXPERT_GROUPS = 8
TOPK_GROUPS = 4
ROUTE_SCALE = 2.5
EXPERT_BLOCK = 128
LN_EPS = 1e-5
QK_EPS = 1e-6
DEEPNORM_ALPHA = (2 * DEPTH) ** 0.25
DEEPNORM_BETA = (8 * DEPTH) ** -0.25

kernel_name = "hybrid_fourier_gqa_gmlp_moe_encoder"


def layer_norm(x, g, b):
    xf = x.astype(jnp.float32)
    mu = jnp.mean(xf, axis=-1, keepdims=True)
    var = jnp.mean(jnp.square(xf - mu), axis=-1, keepdims=True)
    y = (xf - mu) * lax.rsqrt(var + LN_EPS)
    return (y * g.astype(jnp.float32) + b.astype(jnp.float32)).astype(x.dtype)


def axial_rope_tables(seq_len):
    rows = seq_len // GRID_W
    row = jnp.repeat(jnp.arange(rows, dtype=jnp.float32), GRID_W)
    col = jnp.tile(jnp.arange(GRID_W, dtype=jnp.float32), rows)
    inv = ROPE_THETA ** (-jnp.arange(ROPE_PAIRS, dtype=jnp.float32) / ROPE_PAIRS)
    ang = jnp.stack([row, col], axis=-1)[:, :, None] * inv
    return jnp.cos(ang), jnp.sin(ang)


def qk_norm_rope(x, gain, cos, sin):
    B, S, H, D = x.shape
    xf = x.astype(jnp.float32)
    xf = xf * lax.rsqrt(jnp.mean(jnp.square(xf), axis=-1, keepdims=True) + QK_EPS)
    xf = (xf * gain.astype(jnp.float32)).reshape(B, S, H, 2, 2, ROPE_PAIRS)
    x1, x2 = xf[..., 0, :], xf[..., 1, :]
    c, s = cos[None, :, None], sin[None, :, None]
    out = jnp.stack([x1 * c - x2 * s, x2 * c + x1 * s], axis=-2)
    return out.reshape(B, S, H, D).astype(x.dtype)


def block_gqa_attention(q, k, v):
    B, S, H, D = q.shape
    n_blocks = S // Q_BLOCK
    qb = q.reshape(B, n_blocks, Q_BLOCK, N_KV_HEADS, Q_GROUP, D).transpose(1, 0, 3, 4, 2, 5)
    scale = 1.0 / math.sqrt(HEAD_DIM)

    def one_block(qblk):
        s = jnp.einsum('bkgqd,bskd->bkgqs', qblk, k, preferred_element_type=jnp.float32) * scale
        p = jax.nn.softmax(s, axis=-1).astype(v.dtype)
        return jnp.einsum('bkgqs,bskd->bqkgd', p, v)

    o = lax.map(one_block, qb)
    return o.transpose(1, 0, 2, 3, 4, 5).reshape(B, S, H * D)


def fourier_attention_mixer(x, w_in, w_fourier, q_gain, k_gain, w_out, cos, sin):
    B, S, _ = x.shape
    h = x @ w_in
    a, q, k, v = jnp.split(h, [F_WIDTH, F_WIDTH + Q_WIDTH, F_WIDTH + Q_WIDTH + KV_WIDTH], axis=-1)
    a = a.reshape(B, S, N_FGROUPS, FGROUP_DIM).astype(jnp.float32)
    af = jnp.fft.fft2(a, axes=(1, 3), norm='ortho').real.astype(x.dtype)
    a_out = jnp.einsum('bsgc,gcd->bsgd', af, w_fourier).reshape(B, S, F_WIDTH)
    q = qk_norm_rope(q.reshape(B, S, N_HEADS, HEAD_DIM), q_gain, cos, sin)
    k = qk_norm_rope(k.reshape(B, S, N_KV_HEADS, HEAD_DIM), k_gain, cos, sin)
    v = v.reshape(B, S, N_KV_HEADS, HEAD_DIM)
    attn = block_gqa_attention(q, k, v)
    return jnp.concatenate([a_out, attn], axis=-1) @ w_out


def spatial_gating_mixer(x, w_in, b_in, v_g, v_b, w_s, b_s, w_out):
    B, S, _ = x.shape
    h = jax.nn.gelu(x @ w_in + b_in, approximate=False)
    u, v = jnp.split(h, 2, axis=-1)
    v = layer_norm(v, v_g, v_b).reshape(B, S // CHUNK, CHUNK, N_CGROUPS, CGROUP_DIM)
    v = jnp.einsum('gpq,bnqgc->bnpgc', w_s, v) + b_s.T[:, :, None]
    return (u * v.reshape(B, S, C_WIDTH)) @ w_out


def moe_ffn(x, router_w, router_b, w_gate, w_up, w_down, sh_gate, sh_up, sh_down):
    B, S, D = x.shape
    T = B * S
    xt = x.reshape(T, D)
    scores = jax.nn.sigmoid((xt @ router_w).astype(jnp.float32))
    sel = scores + router_b.astype(jnp.float32)
    gsc = lax.top_k(sel.reshape(T, N_EXPERT_GROUPS, N_EXPERTS // N_EXPERT_GROUPS), 2)[0].sum(-1)
    _, gidx = lax.top_k(gsc, TOPK_GROUPS)
    gmask = jnp.any(gidx[..., None] == jnp.arange(N_EXPERT_GROUPS), axis=1)
    emask = jnp.repeat(gmask, N_EXPERTS // N_EXPERT_GROUPS, axis=1)
    _, eidx = lax.top_k(jnp.where(emask, sel, -jnp.inf), TOP_K)
    wts = jnp.take_along_axis(scores, eidx, axis=1)
    wts = wts / jnp.sum(wts, axis=-1, keepdims=True) * ROUTE_SCALE

    A = T * TOP_K
    e_flat = eidx.reshape(A)
    tok_flat = jnp.arange(A, dtype=jnp.int32) // TOP_K
    w_flat = wts.reshape(A)
    order = jnp.argsort(e_flat)
    e_sorted = e_flat[order]
    counts = jnp.bincount(e_flat, length=N_EXPERTS)
    padded = (counts + EXPERT_BLOCK - 1) // EXPERT_BLOCK * EXPERT_BLOCK
    pad_end = jnp.cumsum(padded)
    pad_start = pad_end - padded
    start = jnp.cumsum(counts) - counts
    dest = pad_start[e_sorted] + (jnp.arange(A) - start[e_sorted])
    n_blocks = -(-A // EXPERT_BLOCK) + N_EXPERTS
    P = n_blocks * EXPERT_BLOCK
    row_tok = jnp.full((P,), T, jnp.int32).at[dest].set(tok_flat[order])
    row_w = jnp.zeros((P,), jnp.float32).at[dest].set(w_flat[order])
    blk_e = jnp.minimum(jnp.searchsorted(pad_end, jnp.arange(n_blocks) * EXPERT_BLOCK, side='right'),
                        N_EXPERTS - 1)
    x_pad = jnp.concatenate([xt, jnp.zeros((1, D), xt.dtype)], axis=0)

    def expert_block(args):
        toks, rw, e = args
        xb = x_pad[toks]
        hb = jax.nn.silu(xb @ w_gate[e]) * (xb @ w_up[e])
        return (hb @ w_down[e]) * rw[:, None].astype(xb.dtype)

    yb = lax.map(expert_block, (row_tok.reshape(n_blocks, EXPERT_BLOCK),
                                row_w.reshape(n_blocks, EXPERT_BLOCK), blk_e))
    routed = jax.ops.segment_sum(yb.reshape(P, D), row_tok, num_segments=T + 1)[:T]
    shared = (jax.nn.silu(xt @ sh_gate) * (xt @ sh_up)) @ sh_down
    return (routed + shared).reshape(B, S, D)


def setup_inputs(seed: int = 0) -> dict:
    key = jax.random.key(seed)
    ks = jax.random.split(key, 32)
    f32 = jnp.float32
    ne, no = (DEPTH + 1) // 2, DEPTH // 2

    def nrm(k, shape, scale):
        return jax.random.normal(k, shape, f32) * scale

    return {
        "x": nrm(ks[0], (BATCH, SEQ, D_MODEL), 1.0),
        "ln_in_g": 1.0 + nrm(ks[1], (D_MODEL,), 0.05),
        "ln_in_b": nrm(ks[2], (D_MODEL,), 0.02),
        "e_w_in": nrm(ks[3], (ne, D_MODEL, EVEN_IN_WIDTH), D_MODEL ** -0.5),
        "e_w_fourier": nrm(ks[4], (ne, N_FGROUPS, FGROUP_DIM, FGROUP_DIM), FGROUP_DIM ** -0.5),
        "e_q_gain": 1.0 + nrm(ks[5], (ne, HEAD_DIM), 0.1),
        "e_k_gain": 1.0 + nrm(ks[6], (ne, HEAD_DIM), 0.1),
        "e_w_out": nrm(ks[7], (ne, EVEN_MIX_WIDTH, D_MODEL), EVEN_MIX_WIDTH ** -0.5 * DEEPNORM_BETA),
        "o_w_in": nrm(ks[8], (no, D_MODEL, 2 * C_WIDTH), D_MODEL ** -0.5),
        "o_b_in": nrm(ks[9], (no, 2 * C_WIDTH), 0.02),
        "o_v_ln_g": 1.0 + nrm(ks[10], (no, C_WIDTH), 0.05),
        "o_v_ln_b": nrm(ks[11], (no, C_WIDTH), 0.02),
        "o_w_spatial": nrm(ks[12], (no, N_CGROUPS, CHUNK, CHUNK), CHUNK ** -0.5),
        "o_b_spatial": 1.0 + nrm(ks[13], (no, N_CGROUPS, CHUNK), 0.1),
        "o_w_out": nrm(ks[14], (no, C_WIDTH, D_MODEL), C_WIDTH ** -0.5 * DEEPNORM_BETA),
        "ln_mix_g": 1.0 + nrm(ks[15], (DEPTH, D_MODEL), 0.05),
        "ln_mix_b": nrm(ks[16], (DEPTH, D_MODEL), 0.02),
        "ln_ffn_g": 1.0 + nrm(ks[17], (DEPTH, D_MODEL), 0.05),
        "ln_ffn_b": nrm(ks[18], (DEPTH, D_MODEL), 0.02),
        "router_w": nrm(ks[19], (DEPTH, D_MODEL, N_EXPERTS), D_MODEL ** -0.5),
        "router_b": nrm(ks[20], (DEPTH, N_EXPERTS), 0.01),
        "exp_w_gate": nrm(ks[21], (DEPTH, N_EXPERTS, D_MODEL, EXPERT_DIM), D_MODEL ** -0.5),
        "exp_w_up": nrm(ks[22], (DEPTH, N_EXPERTS, D_MODEL, EXPERT_DIM), D_MODEL ** -0.5),
        "exp_w_down": nrm(ks[23], (DEPTH, N_EXPERTS, EXPERT_DIM, D_MODEL), EXPERT_DIM ** -0.5 * DEEPNORM_BETA),
        "sh_w_gate": nrm(ks[24], (DEPTH, D_MODEL, EXPERT_DIM), D_MODEL ** -0.5),
        "sh_w_up": nrm(ks[25], (DEPTH, D_MODEL, EXPERT_DIM), D_MODEL ** -0.5),
        "sh_w_down": nrm(ks[26], (DEPTH, EXPERT_DIM, D_MODEL), EXPERT_DIM ** -0.5 * DEEPNORM_BETA),
    }


def reference(x, ln_in_g, ln_in_b, e_w_in, e_w_fourier, e_q_gain, e_k_gain, e_w_out,
              o_w_in, o_b_in, o_v_ln_g, o_v_ln_b, o_w_spatial, o_b_spatial, o_w_out,
              ln_mix_g, ln_mix_b, ln_ffn_g, ln_ffn_b, router_w, router_b,
              exp_w_gate, exp_w_up, exp_w_down, sh_w_gate, sh_w_up, sh_w_down):
    x = layer_norm(x, ln_in_g, ln_in_b)
    cos, sin = axial_rope_tables(x.shape[1])
    for i in range(DEPTH):
        j = i // 2
        if i % 2 == 0:
            mix = fourier_attention_mixer(x, e_w_in[j], e_w_fourier[j], e_q_gain[j], e_k_gain[j],
                                          e_w_out[j], cos, sin)
        else:
            mix = spatial_gating_mixer(x, o_w_in[j], o_b_in[j], o_v_ln_g[j], o_v_ln_b[j],
                                       o_w_spatial[j], o_b_spatial[j], o_w_out[j])
        x = layer_norm(DEEPNORM_ALPHA * x + mix, ln_mix_g[i], ln_mix_b[i])
        ffn = moe_ffn(x, router_w[i], router_b[i], exp_w_gate[i], exp_w_up[i], exp_w_down[i],
                      sh_w_gate[i], sh_w_up[i], sh_w_down[i])
        x = layer_norm(DEEPNORM_ALPHA * x + ffn, ln_ffn_g[i], ln_ffn_b[i])
    return x
```

```python
import functools
import math

import numpy as np
import jax
import jax.numpy as jnp
from jax import lax
from jax.experimental import pallas as pl
from jax.experimental.pallas import tpu as pltpu
from jax.experimental.pallas import tpu_sc as plsc

F32 = jnp.float32
BF16 = jnp.bfloat16
I32 = jnp.int32

D_MODEL = 1024
DEPTH = 4
GRID_W = 64
N_FGROUPS = 4
FGROUP_DIM = 128
F_WIDTH = N_FGROUPS * FGROUP_DIM
N_HEADS = 8
N_KV_HEADS = 2
HEAD_DIM = 64
Q_GROUP = N_HEADS // N_KV_HEADS
Q_WIDTH = N_HEADS * HEAD_DIM
KV_WIDTH = N_KV_HEADS * HEAD_DIM
ROPE_THETA = 10000.0
ROPE_PAIRS = HEAD_DIM // 4
EVEN_IN_WIDTH = F_WIDTH + Q_WIDTH + 2 * KV_WIDTH
CHUNK = 128
N_CGROUPS = 8
CGROUP_DIM = D_MODEL // N_CGROUPS
C_WIDTH = N_CGROUPS * CGROUP_DIM
N_EXPERTS = 64
EXPERT_DIM = 256
TOP_K = 8
N_EXPERT_GROUPS = 8
GROUP_SIZE = N_EXPERTS // N_EXPERT_GROUPS
TOPK_GROUPS = 4
ROUTE_SCALE = 2.5
LN_EPS = 1e-5
QK_EPS = 1e-6
DEEPNORM_ALPHA = (2 * DEPTH) ** 0.25

VMEM_LIMIT_BYTES = 56 * 1024 * 1024
ROW_TILE = 256
DFT_N1 = 64
DFT_KRON = 4
EXPERT_ROWS = 256
HALF = D_MODEL // 2
SC_WORKERS = 32
SC_CHUNK = 128
ATT_TQ = 128
ATT_TK = 512
NEG_INF = float("-inf")


def _cparams(*sem):
    return pltpu.CompilerParams(dimension_semantics=sem, vmem_limit_bytes=VMEM_LIMIT_BYTES)


def _ln(x, g, b):
    mu = jnp.mean(x, axis=-1, keepdims=True)
    xc = x - mu
    var = jnp.mean(xc * xc, axis=-1, keepdims=True)
    return xc * lax.rsqrt(var + LN_EPS) * g + b


def _dot(a, b):
    return jnp.dot(a, b, preferred_element_type=F32)


def _pack_halves(y):
    lo = lax.bitcast_convert_type(y[:, :HALF].astype(BF16).astype(F32), I32)
    hi = lax.bitcast_convert_type(y[:, HALF:].astype(BF16).astype(F32), I32)
    return lax.shift_right_logical(lo, 16) | (hi & jnp.int32(-65536))


def _unpack_lo(w):
    return lax.bitcast_convert_type(lax.shift_left(w, 16), F32)


def _unpack_hi(w):
    return lax.bitcast_convert_type(w & jnp.int32(-65536), F32)


def _ln_kernel(x_ref, g_ref, b_ref, o_ref):
    o_ref[...] = _ln(x_ref[...], g_ref[...], b_ref[...])


def _layer_norm(x, g, b):
    T, D = x.shape
    row = pl.BlockSpec((ROW_TILE, D), lambda i: (i, 0))
    vec = pl.BlockSpec((1, D), lambda i: (0, 0))
    return pl.pallas_call(
        _ln_kernel, grid=(T // ROW_TILE,), in_specs=[row, vec, vec], out_specs=row,
        out_shape=jax.ShapeDtypeStruct((T, D), F32), compiler_params=_cparams("parallel"),
        name="ln_in")(x, g.reshape(1, D), b.reshape(1, D))


def _even_in_kernel(x_ref, w_ref, qm_ref, km_ref, qg_ref, kg_ref, cos_ref, sin_ref,
                    a_ref, q_ref, kt_ref, v_ref):
    tm = x_ref.shape[0]
    h = _dot(x_ref[...].astype(BF16), w_ref[...])
    a_ref[...] = h[:, :F_WIDTH].astype(BF16)
    q = h[:, F_WIDTH:F_WIDTH + Q_WIDTH]
    k = h[:, F_WIDTH + Q_WIDTH:F_WIDTH + Q_WIDTH + KV_WIDTH]
    v = h[:, F_WIDTH + Q_WIDTH + KV_WIDTH:]
    cos = cos_ref[...]
    sin = sin_ref[...]
    lane = lax.broadcasted_iota(I32, (tm, 128), 1)
    first_of_pair = (lane & ROPE_PAIRS) == 0

    def mean_sq(xf, m_ref):
        sq = xf * xf
        hi = sq.astype(BF16)
        lo = (sq - hi.astype(F32)).astype(BF16)
        return _dot(hi, m_ref[...]) + _dot(lo, m_ref[...])

    def rope(xn):
        sw = jnp.where(first_of_pair, pltpu.roll(xn, 128 - ROPE_PAIRS, 1), pltpu.roll(xn, ROPE_PAIRS, 1))
        return xn * cos + sw * sin

    qn = q * lax.rsqrt(mean_sq(q, qm_ref) + QK_EPS) * qg_ref[...]
    scale = 1.0 / math.sqrt(HEAD_DIM)
    for c in range(Q_WIDTH // 128):
        q_ref[:, c * 128:(c + 1) * 128] = (rope(qn[:, c * 128:(c + 1) * 128]) * scale).astype(BF16)
    kn = k * lax.rsqrt(mean_sq(k, km_ref) + QK_EPS) * kg_ref[...]
    kt_ref[...] = rope(kn).T.astype(BF16)
    ones_col = jnp.where(lane == HEAD_DIM, 1.0, 0.0)
    low = lane < HEAD_DIM
    v_ref[0] = jnp.where(low, v, ones_col).astype(BF16)
    v_ref[1] = jnp.where(low, pltpu.roll(v, HEAD_DIM, 1), ones_col).astype(BF16)


def _rope_tables(S):
    rows = S // GRID_W
    t = np.arange(S)
    inv = ROPE_THETA ** (-np.arange(ROPE_PAIRS, dtype=np.float64) / ROPE_PAIRS)
    ang_r = (t // GRID_W)[:, None] * inv
    ang_c = (t % GRID_W)[:, None] * inv
    del rows
    cos = np.concatenate([np.cos(ang_r), np.cos(ang_r), np.cos(ang_c), np.cos(ang_c)], axis=1)
    sin = np.concatenate([-np.sin(ang_r), np.sin(ang_r), -np.sin(ang_c), np.sin(ang_c)], axis=1)
    return (jnp.asarray(np.tile(cos, (1, 2)), F32), jnp.asarray(np.tile(sin, (1, 2)), F32))


def _head_mean_matrix(width):
    m = np.kron(np.eye(width // HEAD_DIM), np.full((HEAD_DIM, HEAD_DIM), 1.0 / HEAD_DIM))
    return jnp.asarray(m, BF16)


def _even_in(x, w_in, q_gain, k_gain, B, S):
    T, D = x.shape
    tm = ROW_TILE
    ns = S // tm
    cos, sin = _rope_tables(S)
    row = lambda w: pl.BlockSpec((tm, w), lambda i: (i, 0))
    full = lambda a: pl.BlockSpec(a.shape, lambda i: (0,) * a.ndim)
    tab = pl.BlockSpec((tm, 128), lambda i: (i % ns, 0))
    w = w_in.astype(BF16)
    qm = _head_mean_matrix(Q_WIDTH)
    km = _head_mean_matrix(KV_WIDTH)
    qg = jnp.tile(q_gain.astype(F32), N_HEADS).reshape(1, Q_WIDTH)
    kg = jnp.tile(k_gain.astype(F32), N_KV_HEADS).reshape(1, KV_WIDTH)
    return pl.pallas_call(
        _even_in_kernel, grid=(T // tm,),
        in_specs=[row(D), full(w), full(qm), full(km), full(qg), full(kg), tab, tab],
        out_specs=[row(F_WIDTH), row(Q_WIDTH),
                   pl.BlockSpec((None, KV_WIDTH, tm), lambda i: (i // ns, 0, i % ns)),
                   pl.BlockSpec((N_KV_HEADS, tm, 128), lambda i: (0, i, 0))],
        out_shape=[jax.ShapeDtypeStruct((T, F_WIDTH), BF16),
                   jax.ShapeDtypeStruct((T, Q_WIDTH), BF16),
                   jax.ShapeDtypeStruct((B, KV_WIDTH, S), BF16),
                   jax.ShapeDtypeStruct((N_KV_HEADS, T, 128), BF16)],
        compiler_params=_cparams("parallel"), name="even_in")(x, w, qm, km, qg, kg, cos, sin)


def _fourier_kernel(a_ref, dftc_ref, taba_ref, kc_ref, ks_ref, wf_ref, o_ref,
                    zr_ref, zi_ref, ur_ref, ui_ref, y_ref):
    S = a_ref.shape[0]
    n1_count = DFT_N1
    n2_count = S // DFT_N1
    blk = DFT_KRON * DFT_N1
    scale = 1.0 / math.sqrt(S * FGROUP_DIM)

    def channel_dft(j, carry):
        rows = pl.ds(pl.multiple_of(j * blk, blk), blk)
        zz = _dot(a_ref[rows, :], dftc_ref[...])
        zr_ref[rows, :] = zz[:, :FGROUP_DIM]
        zi_ref[rows, :] = zz[:, FGROUP_DIM:]
        return carry

    lax.fori_loop(0, S // blk, channel_dft, 0)

    def stage_a(n1, carry):
        rows = pl.ds(n1, n2_count, stride=n1_count)
        zn = jnp.concatenate([zr_ref[rows, :], zi_ref[rows, :]], axis=1).astype(BF16)
        r = _dot(taba_ref[n1], zn)
        ur_ref[rows, :] = r[:n2_count, :FGROUP_DIM] + r[n2_count:, FGROUP_DIM:]
        ui_ref[rows, :] = r[:n2_count, FGROUP_DIM:] - r[n2_count:, :FGROUP_DIM]
        return carry

    lax.fori_loop(0, n1_count, stage_a, 0)

    def stage_b(j, carry):
        rows = pl.ds(pl.multiple_of(j * blk, blk), blk)
        re = (_dot(kc_ref[...], ur_ref[rows, :].astype(BF16))
              + _dot(ks_ref[...], ui_ref[rows, :].astype(BF16)))
        out = _dot((re * scale).astype(BF16), wf_ref[...])
        for k2l in range(DFT_KRON):
            y_ref[pl.ds(j * DFT_KRON + k2l, n1_count, stride=n2_count), :] = (
                out[k2l * n1_count:(k2l + 1) * n1_count])
        return carry

    lax.fori_loop(0, S // blk, stage_b, 0)
    o_ref[...] = y_ref[...].astype(BF16)


def _dft_tables(S):
    n1c, n2c = DFT_N1, S // DFT_N1
    c = np.arange(FGROUP_DIM)
    ang = 2 * np.pi * np.outer(c, c) / FGROUP_DIM
    dftc = np.concatenate([np.cos(ang), -np.sin(ang)], axis=1)
    n1 = np.arange(n1c)[:, None, None]
    k2 = np.arange(n2c)[None, :, None]
    n2 = np.arange(n2c)[None, None, :]
    th = 2 * np.pi * (n2 * k2 / n2c + n1 * k2 / S)
    taba = np.concatenate([np.cos(th), np.sin(th)], axis=1)
    k1 = np.arange(n1c)
    g = 2 * np.pi * np.outer(k1, k1) / n1c
    eye = np.eye(DFT_KRON)
    kc = np.kron(eye, np.cos(g))
    ks = np.kron(eye, np.sin(g))
    return tuple(jnp.asarray(t, BF16) for t in (dftc, taba, kc, ks))


def _fourier(a, w_fourier, B, S):
    T = a.shape[0]
    dftc, taba, kc, ks = _dft_tables(S)
    full = lambda t: pl.BlockSpec(t.shape, lambda b, g: (0,) * t.ndim)
    blk = pl.BlockSpec((S, FGROUP_DIM), lambda b, g: (b, g))
    return pl.pallas_call(
        _fourier_kernel, grid=(B, N_FGROUPS),
        in_specs=[blk, full(dftc), full(taba), full(kc), full(ks),
                  pl.BlockSpec((None, FGROUP_DIM, FGROUP_DIM), lambda b, g: (g, 0, 0))],
        out_specs=blk,
        out_shape=jax.ShapeDtypeStruct((T, F_WIDTH), BF16),
        scratch_shapes=[pltpu.VMEM((S, FGROUP_DIM), F32) for _ in range(5)],
        compiler_params=_cparams("parallel", "parallel"), name="fourier")(
            a, dftc, taba, kc, ks, w_fourier.astype(BF16))


def _attn_kernel(q_ref, kt_ref, v_ref, o_ref, qs_ref, m_ref, acc_ref):
    tq = q_ref.shape[0]
    ki = pl.program_id(3)

    @pl.when(ki == 0)
    def _():
        for g in range(Q_GROUP):
            qs_ref[g * tq:(g + 1) * tq, :] = q_ref[:, g * HEAD_DIM:(g + 1) * HEAD_DIM]
        m_ref[...] = jnp.full(m_ref.shape, NEG_INF, F32)
        acc_ref[...] = jnp.zeros(acc_ref.shape, F32)

    s = _dot(qs_ref[...], kt_ref[...])
    m_old = m_ref[...]
    m_new = jnp.maximum(m_old, jnp.max(s, axis=1, keepdims=True))
    p = jnp.exp(s - m_new).astype(BF16)
    acc_ref[...] = jnp.exp(m_old - m_new) * acc_ref[...] + _dot(p, v_ref[...])
    m_ref[...] = m_new

    @pl.when(ki == pl.num_programs(3) - 1)
    def _():
        acc = acc_ref[...]
        o = acc[:, :HEAD_DIM] / acc[:, HEAD_DIM:HEAD_DIM + 1]
        o_ref[...] = jnp.concatenate([o[g * tq:(g + 1) * tq] for g in range(Q_GROUP)],
                                     axis=1).astype(BF16)


def _attention(q, kt, v2, B, S):
    T = q.shape[0]
    tq, tk = ATT_TQ, min(ATT_TK, S)
    nq, nk = S // tq, S // tk
    gw = Q_GROUP * HEAD_DIM
    return pl.pallas_call(
        _attn_kernel, grid=(B, N_KV_HEADS, nq, nk),
        in_specs=[pl.BlockSpec((tq, gw), lambda b, h, i, j: (b * nq + i, h)),
                  pl.BlockSpec((None, HEAD_DIM, tk), lambda b, h, i, j: (b, h, j)),
                  pl.BlockSpec((None, tk, 128), lambda b, h, i, j: (h, b * nk + j, 0))],
        out_specs=pl.BlockSpec((tq, gw), lambda b, h, i, j: (b * nq + i, h)),
        out_shape=jax.ShapeDtypeStruct((T, Q_WIDTH), BF16),
        scratch_shapes=[pltpu.VMEM((Q_GROUP * tq, HEAD_DIM), BF16),
                        pltpu.VMEM((Q_GROUP * tq, 1), F32),
                        pltpu.VMEM((Q_GROUP * tq, 128), F32)],
        compiler_params=_cparams("parallel", "parallel", "parallel", "arbitrary"),
        name="attention")(q, kt, v2)


def _even_out_kernel(a_ref, t_ref, wa_ref, wt_ref, x_ref, g_ref, b_ref, o_ref):
    mix = _dot(a_ref[...], wa_ref[...]) + _dot(t_ref[...], wt_ref[...])
    o_ref[...] = _ln(DEEPNORM_ALPHA * x_ref[...] + mix, g_ref[...], b_ref[...])


def _even_out(a_out, attn, w_out, x, g, b):
    T, D = x.shape
    tm = ROW_TILE
    row = lambda w: pl.BlockSpec((tm, w), lambda i: (i, 0))
    full = lambda a: pl.BlockSpec(a.shape, lambda i: (0,) * a.ndim)
    wa = w_out[:F_WIDTH].astype(BF16)
    wt = w_out[F_WIDTH:].astype(BF16)
    g = g.reshape(1, D)
    b = b.reshape(1, D)
    return pl.pallas_call(
        _even_out_kernel, grid=(T // tm,),
        in_specs=[row(F_WIDTH), row(Q_WIDTH), full(wa), full(wt), row(D), full(g), full(b)],
        out_specs=row(D), out_shape=jax.ShapeDtypeStruct((T, D), F32),
        compiler_params=_cparams("parallel"), name="even_out")(a_out, attn, wa, wt, x, g, b)


def _odd_kernel(x_ref, wi_ref, bi_ref, vg_ref, vb_ref, ws_ref, bs_ref, wo_ref, g_ref, b_ref, o_ref,
                gate_ref):
    tm = x_ref.shape[0]
    x = x_ref[...]
    h = _dot(x.astype(BF16), wi_ref[...]) + bi_ref[...]
    h = 0.5 * h * (1.0 + lax.erf(h * (1.0 / math.sqrt(2.0))))
    u = h[:, :C_WIDTH]
    v = _ln(h[:, C_WIDTH:], vg_ref[...], vb_ref[...]).astype(BF16)
    for c in range(tm // CHUNK):
        r0 = c * CHUNK
        for gi in range(N_CGROUPS):
            l0 = gi * CGROUP_DIM
            sv = _dot(ws_ref[gi], v[r0:r0 + CHUNK, l0:l0 + CGROUP_DIM]) + bs_ref[gi]
            gate_ref[r0:r0 + CHUNK, l0:l0 + CGROUP_DIM] = (
                u[r0:r0 + CHUNK, l0:l0 + CGROUP_DIM] * sv).astype(BF16)
    mix = _dot(gate_ref[...], wo_ref[...])
    o_ref[...] = _ln(DEEPNORM_ALPHA * x + mix, g_ref[...], b_ref[...])


def _odd_mixer(x, w_in, b_in, v_g, v_b, w_s, b_s, w_out, g, b):
    T, D = x.shape
    tm = ROW_TILE
    row = pl.BlockSpec((tm, D), lambda i: (i, 0))
    full = lambda a: pl.BlockSpec(a.shape, lambda i: (0,) * a.ndim)
    args = [w_in.astype(BF16), b_in.reshape(1, 2 * C_WIDTH), v_g.reshape(1, C_WIDTH),
            v_b.reshape(1, C_WIDTH), w_s.astype(BF16),
            jnp.broadcast_to(b_s[:, :, None], (N_CGROUPS, CHUNK, CGROUP_DIM)).astype(F32),
            w_out.astype(BF16), g.reshape(1, D), b.reshape(1, D)]
    return pl.pallas_call(
        _odd_kernel, grid=(T // tm,),
        in_specs=[row] + [full(a) for a in args],
        out_specs=row, out_shape=jax.ShapeDtypeStruct((T, D), F32),
        scratch_shapes=[pltpu.VMEM((tm, C_WIDTH), BF16)],
        compiler_params=_cparams("parallel"), name="odd_mixer")(x, *args)


def _router_kernel(x_ref, w_ref, rb_ref, tri_ref, eidx_ref, wts_ref, pos_ref, cnt_ref, xp_ref, run_ref):
    tm = x_ref.shape[0]
    i = pl.program_id(0)

    @pl.when(i == 0)
    def _():
        run_ref[...] = jnp.zeros(run_ref.shape, F32)

    x = x_ref[...]
    xp_ref[...] = _pack_halves(x)
    xh = x.astype(BF16)
    xl = (x - xh.astype(F32)).astype(BF16)
    nt = (((1,), (1,)), ((), ()))
    dg = lambda a, c: lax.dot_general(a, c, nt, preferred_element_type=F32)
    logits = dg(w_ref[0], xh) + dg(w_ref[0], xl) + dg(w_ref[1], xh)
    scores = jax.nn.sigmoid(logits)
    sel = scores + rb_ref[...]

    i8 = lax.broadcasted_iota(I32, (GROUP_SIZE, tm), 0)
    gsc_rows = []
    for gidx in range(N_EXPERT_GROUPS):
        sg = sel[gidx * GROUP_SIZE:(gidx + 1) * GROUP_SIZE, :]
        m1 = jnp.max(sg, axis=0, keepdims=True)
        f1 = jnp.min(jnp.where(sg == m1, i8, GROUP_SIZE), axis=0, keepdims=True)
        m2 = jnp.max(jnp.where(i8 == f1, NEG_INF, sg), axis=0, keepdims=True)
        gsc_rows.append(m1 + m2)
    gsc = jnp.concatenate(gsc_rows, axis=0)

    gsel = jnp.zeros(gsc.shape, F32)
    for _ in range(TOPK_GROUPS):
        m = jnp.max(gsc, axis=0, keepdims=True)
        f = jnp.min(jnp.where(gsc == m, i8, N_EXPERT_GROUPS), axis=0, keepdims=True)
        pick = i8 == f
        gsel = jnp.where(pick, 1.0, gsel)
        gsc = jnp.where(pick, NEG_INF, gsc)
    esel = jnp.concatenate(
        [jnp.broadcast_to(gsel[gidx:gidx + 1, :], (GROUP_SIZE, tm)) for gidx in range(N_EXPERT_GROUPS)],
        axis=0)

    cur = jnp.where(esel > 0.0, sel, NEG_INF)
    ei = lax.broadcasted_iota(I32, cur.shape, 0)
    idx_rows, sc_rows = [], []
    chosen = jnp.zeros(cur.shape, F32)
    for _ in range(TOP_K):
        m = jnp.max(cur, axis=0, keepdims=True)
        f = jnp.min(jnp.where(cur == m, ei, N_EXPERTS), axis=0, keepdims=True)
        pick = ei == f
        idx_rows.append(f)
        sc_rows.append(jnp.sum(jnp.where(pick, scores, 0.0), axis=0, keepdims=True))
        chosen = jnp.where(pick, 1.0, chosen)
        cur = jnp.where(pick, NEG_INF, cur)
    eidx = jnp.concatenate(idx_rows, axis=0)
    sc = jnp.concatenate(sc_rows, axis=0)
    eidx_ref[...] = eidx
    wts_ref[...] = sc / jnp.sum(sc, axis=0, keepdims=True) * ROUTE_SCALE

    before = _dot(chosen.astype(BF16), tri_ref[...]) + run_ref[...]
    pos_rows = [jnp.sum(jnp.where(ei == idx_rows[k], before, 0.0), axis=0, keepdims=True)
                for k in range(TOP_K)]
    pos_ref[...] = jnp.concatenate(pos_rows, axis=0).astype(I32)
    run_new = run_ref[...] + jnp.sum(chosen, axis=1, keepdims=True)
    run_ref[...] = run_new
    cnt_ref[...] = jnp.broadcast_to(run_new, cnt_ref.shape).astype(I32)


def _router(x, router_w, router_b):
    T, D = x.shape
    tm = ROW_TILE
    wt = router_w.T.astype(F32)
    wh = wt.astype(BF16)
    wl = (wt - wh.astype(F32)).astype(BF16)
    w2 = jnp.stack([wh, wl])
    rb = router_b.astype(F32).reshape(N_EXPERTS, 1)
    tri = jnp.asarray(np.triu(np.ones((tm, tm)), 1), BF16)
    full = lambda a: pl.BlockSpec(a.shape, lambda i: (0,) * a.ndim)
    col = pl.BlockSpec((TOP_K, tm), lambda i: (0, i))
    return pl.pallas_call(
        _router_kernel, grid=(T // tm,),
        in_specs=[pl.BlockSpec((tm, D), lambda i: (i, 0)), full(w2), full(rb), full(tri)],
        out_specs=[col, col, col, pl.BlockSpec((N_EXPERTS, 128), lambda i: (0, 0)),
                   pl.BlockSpec((tm, HALF), lambda i: (i, 0))],
        out_shape=[jax.ShapeDtypeStruct((TOP_K, T), I32), jax.ShapeDtypeStruct((TOP_K, T), F32),
                   jax.ShapeDtypeStruct((TOP_K, T), I32), jax.ShapeDtypeStruct((N_EXPERTS, 128), I32),
                   jax.ShapeDtypeStruct((T, HALF), I32)],
        scratch_shapes=[pltpu.VMEM((N_EXPERTS, 1), F32)],
        compiler_params=_cparams("arbitrary"), name="router")(x, w2, rb, tri)


def _gather_rows(table, idx):
    n_rows = idx.shape[0]
    width = table.shape[1]
    per_worker = n_rows // SC_WORKERS
    n_chunks = per_worker // SC_CHUNK
    assert per_worker * SC_WORKERS == n_rows and n_chunks * SC_CHUNK == per_worker
    mesh = plsc.VectorSubcoreMesh(core_axis_name="c", subcore_axis_name="s")

    @functools.partial(
        pl.kernel, mesh=mesh,
        out_type=jax.ShapeDtypeStruct((n_rows, width), table.dtype),
        scratch_types=[pltpu.VMEM((SC_CHUNK,), I32), pltpu.VMEM((SC_CHUNK, width), table.dtype),
                       pltpu.SemaphoreType.DMA])
    def gather(table_hbm, idx_hbm, out_hbm, idx_v, rows_v, sem):
        wid = lax.axis_index("s") * 2 + lax.axis_index("c")
        base = wid * per_worker

        @pl.loop(0, n_chunks)
        def _(j):
            off = base + j * SC_CHUNK
            pltpu.sync_copy(idx_hbm.at[pl.ds(off, SC_CHUNK)], idx_v)
            pltpu.async_copy(table_hbm.at[idx_v], rows_v, sem).wait()
            pltpu.sync_copy(rows_v, out_hbm.at[pl.ds(off, SC_CHUNK)])

    return gather(table, idx)


def _expert_kernel(be_ref, xs_ref, wg_ref, wu_ref, wd_ref, ys_ref, wgu_s, wd_s):
    i = pl.program_id(0)
    prev = be_ref[jnp.maximum(i - 1, 0)]

    @pl.when(jnp.logical_or(i == 0, be_ref[i] != prev))
    def _():
        wgu_s[:, :EXPERT_DIM] = wg_ref[...].astype(BF16)
        wgu_s[:, EXPERT_DIM:] = wu_ref[...].astype(BF16)
        wd_s[...] = wd_ref[...].astype(BF16)

    w = xs_ref[...]
    xlo = _unpack_lo(w).astype(BF16)
    xhi = _unpack_hi(w).astype(BF16)
    gu = _dot(xlo, wgu_s[:HALF, :]) + _dot(xhi, wgu_s[HALF:, :])
    g = gu[:, :EXPERT_DIM]
    hb = (g * jax.nn.sigmoid(g) * gu[:, EXPERT_DIM:]).astype(BF16)
    ys_ref[...] = _pack_halves(_dot(hb, wd_s[...]))


def _experts(xs, blk_e, w_gate, w_up, w_down):
    P = xs.shape[0]
    n_blocks = P // EXPERT_ROWS
    grid_spec = pltpu.PrefetchScalarGridSpec(
        num_scalar_prefetch=1, grid=(n_blocks,),
        in_specs=[pl.BlockSpec((EXPERT_ROWS, HALF), lambda i, be: (i, 0)),
                  pl.BlockSpec((None, D_MODEL, EXPERT_DIM), lambda i, be: (be[i], 0, 0)),
                  pl.BlockSpec((None, D_MODEL, EXPERT_DIM), lambda i, be: (be[i], 0, 0)),
                  pl.BlockSpec((None, EXPERT_DIM, D_MODEL), lambda i, be: (be[i], 0, 0))],
        out_specs=pl.BlockSpec((EXPERT_ROWS, HALF), lambda i, be: (i, 0)),
        scratch_shapes=[pltpu.VMEM((D_MODEL, 2 * EXPERT_DIM), BF16),
                        pltpu.VMEM((EXPERT_DIM, D_MODEL), BF16)])
    return pl.pallas_call(
        _expert_kernel, grid_spec=grid_spec,
        out_shape=jax.ShapeDtypeStruct((P, HALF), I32),
        compiler_params=_cparams("arbitrary"), name="experts")(blk_e, xs, w_gate, w_up, w_down)


def _moe_out_kernel(x_ref, yg_ref, wt_ref, sgu_ref, sd_ref, g_ref, b_ref, o_ref):
    x = x_ref[...]
    wt = wt_ref[...]
    lo = jnp.zeros((x.shape[0], HALF), F32)
    hi = jnp.zeros((x.shape[0], HALF), F32)
    for k in range(TOP_K):
        w = yg_ref[:, k * HALF:(k + 1) * HALF]
        wk = wt[:, k:k + 1]
        lo = lo + wk * _unpack_lo(w)
        hi = hi + wk * _unpack_hi(w)
    gu = _dot(x.astype(BF16), sgu_ref[...])
    g = gu[:, :EXPERT_DIM]
    hs = (g * jax.nn.sigmoid(g) * gu[:, EXPERT_DIM:]).astype(BF16)
    ffn = jnp.concatenate([lo, hi], axis=1) + _dot(hs, sd_ref[...])
    o_ref[...] = _ln(DEEPNORM_ALPHA * x + ffn, g_ref[...], b_ref[...])


def _moe_out(x, yg, wts, sh_gate, sh_up, sh_down, g, b):
    T, D = x.shape
    tm = ROW_TILE
    row = lambda w: pl.BlockSpec((tm, w), lambda i: (i, 0))
    full = lambda a: pl.BlockSpec(a.shape, lambda i: (0,) * a.ndim)
    sgu = jnp.concatenate([sh_gate, sh_up], axis=1).astype(BF16)
    sd = sh_down.astype(BF16)
    g = g.reshape(1, D)
    b = b.reshape(1, D)
    return pl.pallas_call(
        _moe_out_kernel, grid=(T // tm,),
        in_specs=[row(D), row(TOP_K * HALF), row(TOP_K), full(sgu), full(sd), full(g), full(b)],
        out_specs=row(D), out_shape=jax.ShapeDtypeStruct((T, D), F32),
        compiler_params=_cparams("parallel"), name="moe_out")(x, yg, wts, sgu, sd, g, b)


def _moe(x, router_w, router_b, w_gate, w_up, w_down, sh_gate, sh_up, sh_down, g, b):
    T, D = x.shape
    A = T * TOP_K
    eidx, wts, pos, cnt, xp = _router(x, router_w, router_b)
    counts = cnt[:, 0]
    padded = (counts + EXPERT_ROWS - 1) // EXPERT_ROWS * EXPERT_ROWS
    pad_end = jnp.cumsum(padded)
    pad_start = pad_end - padded
    n_blocks = A // EXPERT_ROWS + N_EXPERTS
    P = n_blocks * EXPERT_ROWS
    dest = pad_start[eidx] + pos
    tok = jnp.broadcast_to(jnp.arange(T, dtype=I32)[None, :], (TOP_K, T))
    row_tok = (jnp.arange(P, dtype=I32) % T).at[dest.reshape(A)].set(
        tok.reshape(A), unique_indices=True)
    blk_e = jnp.minimum(
        jnp.searchsorted(pad_end, jnp.arange(n_blocks, dtype=I32) * EXPERT_ROWS, side="right"),
        N_EXPERTS - 1).astype(I32)
    xs = _gather_rows(xp, row_tok)
    ys = _experts(xs, blk_e, w_gate, w_up, w_down)
    yg = _gather_rows(ys, dest.T.reshape(A)).reshape(T, TOP_K * HALF)
    return _moe_out(x, yg, wts.T, sh_gate, sh_up, sh_down, g, b)


def kernel(x, ln_in_g, ln_in_b, e_w_in, e_w_fourier, e_q_gain, e_k_gain, e_w_out, o_w_in, o_b_in, o_v_ln_g, o_v_ln_b, o_w_spatial, o_b_spatial, o_w_out, ln_mix_g, ln_mix_b, ln_ffn_g, ln_ffn_b, router_w, router_b, exp_w_gate, exp_w_up, exp_w_down, sh_w_gate, sh_w_up, sh_w_down):
    B, S, D = x.shape
    T = B * S
    h = _layer_norm(x.reshape(T, D), ln_in_g, ln_in_b)
    for i in range(DEPTH):
        j = i // 2
        if i % 2 == 0:
            a, q, kt, v2 = _even_in(h, e_w_in[j], e_q_gain[j], e_k_gain[j], B, S)
            a_out = _fourier(a, e_w_fourier[j], B, S)
            attn = _attention(q, kt, v2, B, S)
            h = _even_out(a_out, attn, e_w_out[j], h, ln_mix_g[i], ln_mix_b[i])
        else:
            h = _odd_mixer(h, o_w_in[j], o_b_in[j], o_v_ln_g[j], o_v_ln_b[j], o_w_spatial[j],
                           o_b_spatial[j], o_w_out[j], ln_mix_g[i], ln_mix_b[i])
        h = _moe(h, router_w[i], router_b[i], exp_w_gate[i], exp_w_up[i], exp_w_down[i],
                 sh_w_gate[i], sh_w_up[i], sh_w_down[i], ln_ffn_g[i], ln_ffn_b[i])
    return h.reshape(B, S, D)
```

```python
import functools
import math

import numpy as np
import jax
import jax.numpy as jnp
from jax import lax
from jax.experimental import pallas as pl
from jax.experimental.pallas import tpu as pltpu
from jax.experimental.pallas import tpu_sc as plsc

F32 = jnp.float32
BF16 = jnp.bfloat16
I32 = jnp.int32

D_MODEL = 1024
DEPTH = 4
GRID_W = 64
N_FGROUPS = 4
FGROUP_DIM = 128
F_WIDTH = N_FGROUPS * FGROUP_DIM
N_HEADS = 8
N_KV_HEADS = 2
HEAD_DIM = 64
Q_GROUP = N_HEADS // N_KV_HEADS
Q_WIDTH = N_HEADS * HEAD_DIM
KV_WIDTH = N_KV_HEADS * HEAD_DIM
ROPE_THETA = 10000.0
ROPE_PAIRS = HEAD_DIM // 4
EVEN_IN_WIDTH = F_WIDTH + Q_WIDTH + 2 * KV_WIDTH
CHUNK = 128
N_CGROUPS = 8
CGROUP_DIM = D_MODEL // N_CGROUPS
C_WIDTH = N_CGROUPS * CGROUP_DIM
N_EXPERTS = 64
EXPERT_DIM = 256
TOP_K = 8
N_EXPERT_GROUPS = 8
GROUP_SIZE = N_EXPERTS // N_EXPERT_GROUPS
TOPK_GROUPS = 4
ROUTE_SCALE = 2.5
LN_EPS = 1e-5
QK_EPS = 1e-6
DEEPNORM_ALPHA = (2 * DEPTH) ** 0.25

VMEM_LIMIT_BYTES = 56 * 1024 * 1024
ROW_TILE = 256
DFT_N1 = 64
DFT_KRON = 4
EXPERT_ROWS = 256
HALF = D_MODEL // 2
SC_WORKERS = 32
SC_CHUNK = 128
ATT_TQ = 128
ATT_TK = 512
ATT_BOUND_SLACK = 1.0 + 2.0 ** -7
ATT_MIN_ROW_SUM = 2.0 ** -80
NEG_INF = float("-inf")


def _cparams(*sem):
    return pltpu.CompilerParams(dimension_semantics=sem, vmem_limit_bytes=VMEM_LIMIT_BYTES)


def _ln(x, g, b):
    mu = jnp.mean(x, axis=-1, keepdims=True)
    xc = x - mu
    var = jnp.mean(xc * xc, axis=-1, keepdims=True)
    return xc * lax.rsqrt(var + LN_EPS) * g + b


def _dot(a, b):
    return jnp.dot(a, b, preferred_element_type=F32)


def _pack_halves(y):
    lo = lax.bitcast_convert_type(y[:, :HALF].astype(BF16).astype(F32), I32)
    hi = lax.bitcast_convert_type(y[:, HALF:].astype(BF16).astype(F32), I32)
    return lax.shift_right_logical(lo, 16) | (hi & jnp.int32(-65536))


def _unpack_lo(w):
    return lax.bitcast_convert_type(lax.shift_left(w, 16), F32)


def _unpack_hi(w):
    return lax.bitcast_convert_type(w & jnp.int32(-65536), F32)


def _ln_kernel(x_ref, g_ref, b_ref, o_ref):
    o_ref[...] = _ln(x_ref[...], g_ref[...], b_ref[...])


def _layer_norm(x, g, b):
    T, D = x.shape
    row = pl.BlockSpec((ROW_TILE, D), lambda i: (i, 0))
    vec = pl.BlockSpec((1, D), lambda i: (0, 0))
    return pl.pallas_call(
        _ln_kernel, grid=(T // ROW_TILE,), in_specs=[row, vec, vec], out_specs=row,
        out_shape=jax.ShapeDtypeStruct((T, D), F32), compiler_params=_cparams("parallel"),
        name="ln_in")(x, g.reshape(1, D), b.reshape(1, D))


def _even_in_kernel(x_ref, w_ref, qm_ref, km_ref, qg_ref, kg_ref, cos_ref, sin_ref,
                    a_ref, q_ref, kt_ref, v_ref):
    tm = x_ref.shape[0]
    h = _dot(x_ref[...].astype(BF16), w_ref[...])
    a_ref[...] = h[:, :F_WIDTH].astype(BF16)
    q = h[:, F_WIDTH:F_WIDTH + Q_WIDTH]
    k = h[:, F_WIDTH + Q_WIDTH:F_WIDTH + Q_WIDTH + KV_WIDTH]
    v = h[:, F_WIDTH + Q_WIDTH + KV_WIDTH:]
    cos = cos_ref[...]
    sin = sin_ref[...]
    lane = lax.broadcasted_iota(I32, (tm, 128), 1)
    first_of_pair = (lane & ROPE_PAIRS) == 0

    def mean_sq(xf, m_ref):
        sq = xf * xf
        hi = sq.astype(BF16)
        lo = (sq - hi.astype(F32)).astype(BF16)
        return _dot(hi, m_ref[...]) + _dot(lo, m_ref[...])

    def rope(xn):
        sw = jnp.where(first_of_pair, pltpu.roll(xn, 128 - ROPE_PAIRS, 1), pltpu.roll(xn, ROPE_PAIRS, 1))
        return xn * cos + sw * sin

    qn = q * lax.rsqrt(mean_sq(q, qm_ref) + QK_EPS) * qg_ref[...]
    scale = math.log2(math.e) / math.sqrt(HEAD_DIM)
    for c in range(Q_WIDTH // 128):
        q_ref[:, c * 128:(c + 1) * 128] = (rope(qn[:, c * 128:(c + 1) * 128]) * scale).astype(BF16)
    kn = k * lax.rsqrt(mean_sq(k, km_ref) + QK_EPS) * kg_ref[...]
    kt_ref[...] = rope(kn).T.astype(BF16)
    ones_col = jnp.where(lane == HEAD_DIM, 1.0, 0.0)
    low = lane < HEAD_DIM
    v_ref[0] = jnp.where(low, v, ones_col).astype(BF16)
    v_ref[1] = jnp.where(low, pltpu.roll(v, HEAD_DIM, 1), ones_col).astype(BF16)


def _rope_tables(S):
    rows = S // GRID_W
    t = np.arange(S)
    inv = ROPE_THETA ** (-np.arange(ROPE_PAIRS, dtype=np.float64) / ROPE_PAIRS)
    ang_r = (t // GRID_W)[:, None] * inv
    ang_c = (t % GRID_W)[:, None] * inv
    del rows
    cos = np.concatenate([np.cos(ang_r), np.cos(ang_r), np.cos(ang_c), np.cos(ang_c)], axis=1)
    sin = np.concatenate([-np.sin(ang_r), np.sin(ang_r), -np.sin(ang_c), np.sin(ang_c)], axis=1)
    return (jnp.asarray(np.tile(cos, (1, 2)), F32), jnp.asarray(np.tile(sin, (1, 2)), F32))


def _head_mean_matrix(width):
    m = np.kron(np.eye(width // HEAD_DIM), np.full((HEAD_DIM, HEAD_DIM), 1.0 / HEAD_DIM))
    return jnp.asarray(m, BF16)


def _even_in(x, w_in, q_gain, k_gain, B, S):
    T, D = x.shape
    tm = ROW_TILE
    ns = S // tm
    cos, sin = _rope_tables(S)
    row = lambda w: pl.BlockSpec((tm, w), lambda i: (i, 0))
    full = lambda a: pl.BlockSpec(a.shape, lambda i: (0,) * a.ndim)
    tab = pl.BlockSpec((tm, 128), lambda i: (i % ns, 0))
    w = w_in.astype(BF16)
    qm = _head_mean_matrix(Q_WIDTH)
    km = _head_mean_matrix(KV_WIDTH)
    qg = jnp.tile(q_gain.astype(F32), N_HEADS).reshape(1, Q_WIDTH)
    kg = jnp.tile(k_gain.astype(F32), N_KV_HEADS).reshape(1, KV_WIDTH)
    return pl.pallas_call(
        _even_in_kernel, grid=(T // tm,),
        in_specs=[row(D), full(w), full(qm), full(km), full(qg), full(kg), tab, tab],
        out_specs=[row(F_WIDTH), row(Q_WIDTH),
                   pl.BlockSpec((None, KV_WIDTH, tm), lambda i: (i // ns, 0, i % ns)),
                   pl.BlockSpec((N_KV_HEADS, tm, 128), lambda i: (0, i, 0))],
        out_shape=[jax.ShapeDtypeStruct((T, F_WIDTH), BF16),
                   jax.ShapeDtypeStruct((T, Q_WIDTH), BF16),
                   jax.ShapeDtypeStruct((B, KV_WIDTH, S), BF16),
                   jax.ShapeDtypeStruct((N_KV_HEADS, T, 128), BF16)],
        compiler_params=_cparams("parallel"), name="even_in")(x, w, qm, km, qg, kg, cos, sin)


def _fourier_kernel(a_ref, dftc_ref, taba_ref, kc_ref, ks_ref, wf_ref, o_ref,
                    zr_ref, zi_ref, ur_ref, ui_ref, y_ref):
    S = a_ref.shape[0]
    n1_count = DFT_N1
    n2_count = S // DFT_N1
    blk = DFT_KRON * DFT_N1
    scale = 1.0 / math.sqrt(S * FGROUP_DIM)

    def channel_dft(j, carry):
        rows = pl.ds(pl.multiple_of(j * blk, blk), blk)
        zz = _dot(a_ref[rows, :], dftc_ref[...])
        zr_ref[rows, :] = zz[:, :FGROUP_DIM]
        zi_ref[rows, :] = zz[:, FGROUP_DIM:]
        return carry

    lax.fori_loop(0, S // blk, channel_dft, 0)

    def stage_a(n1, carry):
        rows = pl.ds(n1, n2_count, stride=n1_count)
        zn = jnp.concatenate([zr_ref[rows, :], zi_ref[rows, :]], axis=1).astype(BF16)
        r = _dot(taba_ref[n1], zn)
        ur_ref[rows, :] = r[:n2_count, :FGROUP_DIM] + r[n2_count:, FGROUP_DIM:]
        ui_ref[rows, :] = r[:n2_count, FGROUP_DIM:] - r[n2_count:, :FGROUP_DIM]
        return carry

    lax.fori_loop(0, n1_count, stage_a, 0)

    def stage_b(j, carry):
        rows = pl.ds(pl.multiple_of(j * blk, blk), blk)
        re = (_dot(kc_ref[...], ur_ref[rows, :].astype(BF16))
              + _dot(ks_ref[...], ui_ref[rows, :].astype(BF16)))
        out = _dot((re * scale).astype(BF16), wf_ref[...])
        for k2l in range(DFT_KRON):
            y_ref[pl.ds(j * DFT_KRON + k2l, n1_count, stride=n2_count), :] = (
                out[k2l * n1_count:(k2l + 1) * n1_count])
        return carry

    lax.fori_loop(0, S // blk, stage_b, 0)
    o_ref[...] = y_ref[...].astype(BF16)


def _dft_tables(S):
    n1c, n2c = DFT_N1, S // DFT_N1
    c = np.arange(FGROUP_DIM)
    ang = 2 * np.pi * np.outer(c, c) / FGROUP_DIM
    dftc = np.concatenate([np.cos(ang), -np.sin(ang)], axis=1)
    n1 = np.arange(n1c)[:, None, None]
    k2 = np.arange(n2c)[None, :, None]
    n2 = np.arange(n2c)[None, None, :]
    th = 2 * np.pi * (n2 * k2 / n2c + n1 * k2 / S)
    taba = np.concatenate([np.cos(th), np.sin(th)], axis=1)
    k1 = np.arange(n1c)
    g = 2 * np.pi * np.outer(k1, k1) / n1c
    eye = np.eye(DFT_KRON)
    kc = np.kron(eye, np.cos(g))
    ks = np.kron(eye, np.sin(g))
    return tuple(jnp.asarray(t, BF16) for t in (dftc, taba, kc, ks))


def _fourier(a, w_fourier, B, S):
    T = a.shape[0]
    dftc, taba, kc, ks = _dft_tables(S)
    full = lambda t: pl.BlockSpec(t.shape, lambda b, g: (0,) * t.ndim)
    blk = pl.BlockSpec((S, FGROUP_DIM), lambda b, g: (b, g))
    return pl.pallas_call(
        _fourier_kernel, grid=(B, N_FGROUPS),
        in_specs=[blk, full(dftc), full(taba), full(kc), full(ks),
                  pl.BlockSpec((None, FGROUP_DIM, FGROUP_DIM), lambda b, g: (g, 0, 0))],
        out_specs=blk,
        out_shape=jax.ShapeDtypeStruct((T, F_WIDTH), BF16),
        scratch_shapes=[pltpu.VMEM((S, FGROUP_DIM), F32) for _ in range(5)],
        compiler_params=_cparams("parallel", "parallel"), name="fourier")(
            a, dftc, taba, kc, ks, w_fourier.astype(BF16))


def _attn_kernel(q_ref, kt_ref, v_ref, o_ref, qs_ref, kmax_ref, acc_ref, m_ref):
    tq = q_ref.shape[0]
    n_keys = kt_ref.shape[1]
    tk = min(ATT_TK, n_keys)
    n_chunks = n_keys // tk

    @pl.when(pl.program_id(2) == 0)
    def _():
        def body(c, best):
            k = kt_ref[:, pl.ds(pl.multiple_of(c * tk, tk), tk)].astype(F32)
            return jnp.maximum(best, jnp.sum(k * k, axis=0, keepdims=True))
        best = lax.fori_loop(0, n_chunks, body, jnp.zeros((1, tk), F32))
        kmax_ref[...] = jnp.broadcast_to(jnp.sqrt(jnp.max(best, axis=1, keepdims=True)), kmax_ref.shape)

    for g in range(Q_GROUP):
        qs_ref[g * tq:(g + 1) * tq, :] = q_ref[:, g * HEAD_DIM:(g + 1) * HEAD_DIM]
    qf = qs_ref[...].astype(F32)
    bound = jnp.sqrt(jnp.sum(qf * qf, axis=1, keepdims=True)) * kmax_ref[0:1, 0:1] * ATT_BOUND_SLACK

    def chunk(c):
        cols = pl.ds(pl.multiple_of(c * tk, tk), tk)
        return _dot(qs_ref[...], kt_ref[:, cols]), v_ref[cols, :]

    def fast(c, carry):
        s, v = chunk(c)
        acc_ref[...] += _dot(jnp.exp2(s - bound).astype(BF16), v)
        return carry

    acc_ref[...] = jnp.zeros(acc_ref.shape, F32)
    lax.fori_loop(0, n_chunks, fast, 0)
    underflow = jnp.min(acc_ref[:, HEAD_DIM:HEAD_DIM + 1]) < ATT_MIN_ROW_SUM

    @pl.when(underflow)
    def _():
        def safe(c, carry):
            s, v = chunk(c)
            m_old = m_ref[...]
            m_new = jnp.maximum(m_old, jnp.max(s, axis=1, keepdims=True))
            acc_ref[...] = jnp.exp2(m_old - m_new) * acc_ref[...] + _dot(jnp.exp2(s - m_new).astype(BF16), v)
            m_ref[...] = m_new
            return carry

        m_ref[...] = jnp.full(m_ref.shape, NEG_INF, F32)
        acc_ref[...] = jnp.zeros(acc_ref.shape, F32)
        lax.fori_loop(0, n_chunks, safe, 0)

    acc = acc_ref[...]
    o = acc[:, :HEAD_DIM] / acc[:, HEAD_DIM:HEAD_DIM + 1]
    o_ref[...] = jnp.concatenate([o[g * tq:(g + 1) * tq] for g in range(Q_GROUP)], axis=1).astype(BF16)


def _attention(q, kt, v2, B, S):
    T = q.shape[0]
    tq = ATT_TQ
    nq = S // tq
    gw = Q_GROUP * HEAD_DIM
    rows = Q_GROUP * tq
    return pl.pallas_call(
        _attn_kernel, grid=(B, N_KV_HEADS, nq),
        in_specs=[pl.BlockSpec((tq, gw), lambda b, h, i: (b * nq + i, h)),
                  pl.BlockSpec((None, HEAD_DIM, S), lambda b, h, i: (b, h, 0)),
                  pl.BlockSpec((None, None, S, 128), lambda b, h, i: (h, b, 0, 0))],
        out_specs=pl.BlockSpec((tq, gw), lambda b, h, i: (b * nq + i, h)),
        out_shape=jax.ShapeDtypeStruct((T, Q_WIDTH), BF16),
        scratch_shapes=[pltpu.VMEM((rows, HEAD_DIM), BF16), pltpu.VMEM((8, 128), F32),
                        pltpu.VMEM((rows, 128), F32), pltpu.VMEM((rows, 1), F32)],
        compiler_params=_cparams("parallel", "parallel", "arbitrary"),
        name="attention")(q, kt, v2.reshape(N_KV_HEADS, B, S, 128))


def _even_out_kernel(a_ref, t_ref, wa_ref, wt_ref, x_ref, g_ref, b_ref, o_ref):
    mix = _dot(a_ref[...], wa_ref[...]) + _dot(t_ref[...], wt_ref[...])
    o_ref[...] = _ln(DEEPNORM_ALPHA * x_ref[...] + mix, g_ref[...], b_ref[...])


def _even_out(a_out, attn, w_out, x, g, b):
    T, D = x.shape
    tm = ROW_TILE
    row = lambda w: pl.BlockSpec((tm, w), lambda i: (i, 0))
    full = lambda a: pl.BlockSpec(a.shape, lambda i: (0,) * a.ndim)
    wa = w_out[:F_WIDTH].astype(BF16)
    wt = w_out[F_WIDTH:].astype(BF16)
    g = g.reshape(1, D)
    b = b.reshape(1, D)
    return pl.pallas_call(
        _even_out_kernel, grid=(T // tm,),
        in_specs=[row(F_WIDTH), row(Q_WIDTH), full(wa), full(wt), row(D), full(g), full(b)],
        out_specs=row(D), out_shape=jax.ShapeDtypeStruct((T, D), F32),
        compiler_params=_cparams("parallel"), name="even_out")(a_out, attn, wa, wt, x, g, b)


def _odd_kernel(x_ref, wi_ref, bi_ref, vg_ref, vb_ref, ws_ref, bs_ref, wo_ref, g_ref, b_ref, o_ref,
                gate_ref):
    tm = x_ref.shape[0]
    x = x_ref[...]
    h = _dot(x.astype(BF16), wi_ref[...]) + bi_ref[...]
    h = 0.5 * h * (1.0 + lax.erf(h * (1.0 / math.sqrt(2.0))))
    u = h[:, :C_WIDTH]
    v = _ln(h[:, C_WIDTH:], vg_ref[...], vb_ref[...]).astype(BF16)
    for c in range(tm // CHUNK):
        r0 = c * CHUNK
        for gi in range(N_CGROUPS):
            l0 = gi * CGROUP_DIM
            sv = _dot(ws_ref[gi], v[r0:r0 + CHUNK, l0:l0 + CGROUP_DIM]) + bs_ref[gi]
            gate_ref[r0:r0 + CHUNK, l0:l0 + CGROUP_DIM] = (
                u[r0:r0 + CHUNK, l0:l0 + CGROUP_DIM] * sv).astype(BF16)
    mix = _dot(gate_ref[...], wo_ref[...])
    o_ref[...] = _ln(DEEPNORM_ALPHA * x + mix, g_ref[...], b_ref[...])


def _odd_mixer(x, w_in, b_in, v_g, v_b, w_s, b_s, w_out, g, b):
    T, D = x.shape
    tm = ROW_TILE
    row = pl.BlockSpec((tm, D), lambda i: (i, 0))
    full = lambda a: pl.BlockSpec(a.shape, lambda i: (0,) * a.ndim)
    args = [w_in.astype(BF16), b_in.reshape(1, 2 * C_WIDTH), v_g.reshape(1, C_WIDTH),
            v_b.reshape(1, C_WIDTH), w_s.astype(BF16),
            jnp.broadcast_to(b_s[:, :, None], (N_CGROUPS, CHUNK, CGROUP_DIM)).astype(F32),
            w_out.astype(BF16), g.reshape(1, D), b.reshape(1, D)]
    return pl.pallas_call(
        _odd_kernel, grid=(T // tm,),
        in_specs=[row] + [full(a) for a in args],
        out_specs=row, out_shape=jax.ShapeDtypeStruct((T, D), F32),
        scratch_shapes=[pltpu.VMEM((tm, C_WIDTH), BF16)],
        compiler_params=_cparams("parallel"), name="odd_mixer")(x, *args)


def _router_kernel(x_ref, w_ref, rb_ref, tri_ref, eidx_ref, wts_ref, pos_ref, cnt_ref, xp_ref, run_ref):
    tm = x_ref.shape[0]
    i = pl.program_id(0)

    @pl.when(i == 0)
    def _():
        run_ref[...] = jnp.zeros(run_ref.shape, F32)

    x = x_ref[...]
    xp_ref[...] = _pack_halves(x)
    xh = x.astype(BF16)
    xl = (x - xh.astype(F32)).astype(BF16)
    nt = (((1,), (1,)), ((), ()))
    dg = lambda a, c: lax.dot_general(a, c, nt, preferred_element_type=F32)
    logits = dg(w_ref[0], xh) + dg(w_ref[0], xl) + dg(w_ref[1], xh)
    scores = jax.nn.sigmoid(logits)
    sel = scores + rb_ref[...]

    i8 = lax.broadcasted_iota(I32, (GROUP_SIZE, tm), 0)
    gsc_rows = []
    for gidx in range(N_EXPERT_GROUPS):
        sg = sel[gidx * GROUP_SIZE:(gidx + 1) * GROUP_SIZE, :]
        m1 = jnp.max(sg, axis=0, keepdims=True)
        f1 = jnp.min(jnp.where(sg == m1, i8, GROUP_SIZE), axis=0, keepdims=True)
        m2 = jnp.max(jnp.where(i8 == f1, NEG_INF, sg), axis=0, keepdims=True)
        gsc_rows.append(m1 + m2)
    gsc = jnp.concatenate(gsc_rows, axis=0)

    gsel = jnp.zeros(gsc.shape, F32)
    for _ in range(TOPK_GROUPS):
        m = jnp.max(gsc, axis=0, keepdims=True)
        f = jnp.min(jnp.where(gsc == m, i8, N_EXPERT_GROUPS), axis=0, keepdims=True)
        pick = i8 == f
        gsel = jnp.where(pick, 1.0, gsel)
        gsc = jnp.where(pick, NEG_INF, gsc)
    esel = jnp.concatenate(
        [jnp.broadcast_to(gsel[gidx:gidx + 1, :], (GROUP_SIZE, tm)) for gidx in range(N_EXPERT_GROUPS)],
        axis=0)

    cur = jnp.where(esel > 0.0, sel, NEG_INF)
    ei = lax.broadcasted_iota(I32, cur.shape, 0)
    idx_rows, sc_rows = [], []
    chosen = jnp.zeros(cur.shape, F32)
    for _ in range(TOP_K):
        m = jnp.max(cur, axis=0, keepdims=True)
        f = jnp.min(jnp.where(cur == m, ei, N_EXPERTS), axis=0, keepdims=True)
        pick = ei == f
        idx_rows.append(f)
        sc_rows.append(jnp.sum(jnp.where(pick, scores, 0.0), axis=0, keepdims=True))
        chosen = jnp.where(pick, 1.0, chosen)
        cur = jnp.where(pick, NEG_INF, cur)
    eidx = jnp.concatenate(idx_rows, axis=0)
    sc = jnp.concatenate(sc_rows, axis=0)
    eidx_ref[...] = eidx
    wts_ref[...] = sc / jnp.sum(sc, axis=0, keepdims=True) * ROUTE_SCALE

    before = _dot(chosen.astype(BF16), tri_ref[...]) + run_ref[...]
    pos_rows = [jnp.sum(jnp.where(ei == idx_rows[k], before, 0.0), axis=0, keepdims=True)
                for k in range(TOP_K)]
    pos_ref[...] = jnp.concatenate(pos_rows, axis=0).astype(I32)
    run_new = run_ref[...] + jnp.sum(chosen, axis=1, keepdims=True)
    run_ref[...] = run_new
    cnt_ref[...] = jnp.broadcast_to(run_new, cnt_ref.shape).astype(I32)


def _router(x, router_w, router_b):
    T, D = x.shape
    tm = ROW_TILE
    wt = router_w.T.astype(F32)
    wh = wt.astype(BF16)
    wl = (wt - wh.astype(F32)).astype(BF16)
    w2 = jnp.stack([wh, wl])
    rb = router_b.astype(F32).reshape(N_EXPERTS, 1)
    tri = jnp.asarray(np.triu(np.ones((tm, tm)), 1), BF16)
    full = lambda a: pl.BlockSpec(a.shape, lambda i: (0,) * a.ndim)
    col = pl.BlockSpec((TOP_K, tm), lambda i: (0, i))
    return pl.pallas_call(
        _router_kernel, grid=(T // tm,),
        in_specs=[pl.BlockSpec((tm, D), lambda i: (i, 0)), full(w2), full(rb), full(tri)],
        out_specs=[col, col, col, pl.BlockSpec((N_EXPERTS, 128), lambda i: (0, 0)),
                   pl.BlockSpec((tm, HALF), lambda i: (i, 0))],
        out_shape=[jax.ShapeDtypeStruct((TOP_K, T), I32), jax.ShapeDtypeStruct((TOP_K, T), F32),
                   jax.ShapeDtypeStruct((TOP_K, T), I32), jax.ShapeDtypeStruct((N_EXPERTS, 128), I32),
                   jax.ShapeDtypeStruct((T, HALF), I32)],
        scratch_shapes=[pltpu.VMEM((N_EXPERTS, 1), F32)],
        compiler_params=_cparams("arbitrary"), name="router")(x, w2, rb, tri)


def _dest_kernel(start_ref, eidx_ref, pos_ref, o_ref):
    e = eidx_ref[...]
    acc = pos_ref[...]
    for j in range(N_EXPERTS):
        acc = acc + jnp.where(e == j, start_ref[j], 0)
    o_ref[...] = acc


def _dest_rows(eidx, pos, seg_start):
    K, T = eidx.shape
    tl = min(T, 2048)
    blk = pl.BlockSpec((K, tl), lambda i, s: (0, i))
    grid_spec = pltpu.PrefetchScalarGridSpec(
        num_scalar_prefetch=1, grid=(T // tl,), in_specs=[blk, blk], out_specs=blk)
    return pl.pallas_call(
        _dest_kernel, grid_spec=grid_spec, out_shape=jax.ShapeDtypeStruct((K, T), I32),
        compiler_params=_cparams("parallel"), name="dest_rows")(seg_start, eidx, pos)


def _gather_rows(table, idx):
    n_rows = idx.shape[0]
    width = table.shape[1]
    per_worker = n_rows // SC_WORKERS
    n_chunks = per_worker // SC_CHUNK
    assert per_worker * SC_WORKERS == n_rows and n_chunks * SC_CHUNK == per_worker
    mesh = plsc.VectorSubcoreMesh(core_axis_name="c", subcore_axis_name="s")

    @functools.partial(
        pl.kernel, mesh=mesh,
        out_type=jax.ShapeDtypeStruct((n_rows, width), table.dtype),
        scratch_types=[pltpu.VMEM((SC_CHUNK,), I32), pltpu.VMEM((SC_CHUNK, width), table.dtype),
                       pltpu.SemaphoreType.DMA])
    def gather(table_hbm, idx_hbm, out_hbm, idx_v, rows_v, sem):
        wid = lax.axis_index("s") * 2 + lax.axis_index("c")
        base = wid * per_worker

        @pl.loop(0, n_chunks)
        def _(j):
            off = base + j * SC_CHUNK
            pltpu.sync_copy(idx_hbm.at[pl.ds(off, SC_CHUNK)], idx_v)
            pltpu.async_copy(table_hbm.at[idx_v], rows_v, sem).wait()
            pltpu.sync_copy(rows_v, out_hbm.at[pl.ds(off, SC_CHUNK)])

    return gather(table, idx)


def _scatter_rows(rows, dest):
    n_tok, width = rows.shape
    n_dst = dest.shape[0]
    per_worker = n_tok // SC_WORKERS
    n_chunks = per_worker // SC_CHUNK
    assert per_worker * SC_WORKERS == n_tok and n_chunks * SC_CHUNK == per_worker
    mesh = plsc.VectorSubcoreMesh(core_axis_name="c", subcore_axis_name="s")

    @functools.partial(
        pl.kernel, mesh=mesh,
        out_type=jax.ShapeDtypeStruct((n_dst * n_tok, width), rows.dtype),
        scratch_types=[pltpu.VMEM((n_dst, SC_CHUNK), I32), pltpu.VMEM((SC_CHUNK, width), rows.dtype),
                       pltpu.SemaphoreType.DMA])
    def scatter(rows_hbm, dest_hbm, out_hbm, idx_v, rows_v, sem):
        wid = lax.axis_index("s") * 2 + lax.axis_index("c")
        base = wid * per_worker

        @pl.loop(0, n_chunks)
        def _(j):
            off = base + j * SC_CHUNK
            pltpu.sync_copy(dest_hbm.at[:, pl.ds(off, SC_CHUNK)], idx_v)
            pltpu.sync_copy(rows_hbm.at[pl.ds(off, SC_CHUNK)], rows_v)
            copies = [pltpu.async_copy(rows_v, out_hbm.at[idx_v.at[k]], sem) for k in range(n_dst)]
            for c in copies:
                c.wait()

    return scatter(rows, dest)


def _expert_kernel(blk_ref, exp_ref, lo_ref, hi_ref, xs_ref, wg_ref, wu_ref, wd_ref, ys_ref,
                   wgu_s, wd_s, acc_s):
    i = pl.program_id(0)
    prev = jnp.maximum(i - 1, 0)

    @pl.when(jnp.logical_or(i == 0, exp_ref[i] != exp_ref[prev]))
    def _():
        wgu_s[:, :EXPERT_DIM] = wg_ref[...].astype(BF16)
        wgu_s[:, EXPERT_DIM:] = wu_ref[...].astype(BF16)
        wd_s[...] = wd_ref[...].astype(BF16)

    w = xs_ref[...]
    xlo = _unpack_lo(w).astype(BF16)
    xhi = _unpack_hi(w).astype(BF16)
    gu = _dot(xlo, wgu_s[:HALF, :]) + _dot(xhi, wgu_s[HALF:, :])
    g = gu[:, :EXPERT_DIM]
    hb = (g * jax.nn.sigmoid(g) * gu[:, EXPERT_DIM:]).astype(BF16)
    row = lax.broadcasted_iota(I32, (EXPERT_ROWS, 1), 0)
    mine = jnp.logical_and(row >= lo_ref[i], row < hi_ref[i])
    y = jnp.where(mine, _dot(hb, wd_s[...]), 0.0)
    first = jnp.logical_or(i == 0, blk_ref[i] != blk_ref[prev])

    @pl.when(first)
    def _():
        acc_s[...] = y

    @pl.when(jnp.logical_not(first))
    def _():
        acc_s[...] += y

    ys_ref[...] = _pack_halves(acc_s[...])


def _expert_items(counts, n_rows):
    n_blocks = n_rows // EXPERT_ROWS
    n_items = n_blocks + N_EXPERTS - 1
    end = jnp.cumsum(counts)
    start = end - counts
    first_blk = start // EXPERT_ROWS
    n_blk = jnp.where(counts > 0, (end - 1) // EXPERT_ROWS - first_blk + 1, 0)
    item_end = jnp.cumsum(n_blk)
    item_start = item_end - n_blk
    slot = jnp.arange(n_items, dtype=I32)
    e = jnp.minimum(jnp.sum((item_end[None, :] <= slot[:, None]).astype(I32), axis=1), N_EXPERTS - 1)
    onehot = (e[:, None] == jnp.arange(N_EXPERTS, dtype=I32)[None, :]).astype(I32)
    pick = lambda v: jnp.sum(onehot * v[None, :], axis=1)
    valid = slot < item_end[-1]
    blk = jnp.where(valid, pick(first_blk) + slot - pick(item_start), n_blocks - 1)
    lo = jnp.clip(pick(start) - blk * EXPERT_ROWS, 0, EXPERT_ROWS)
    hi = jnp.clip(pick(end) - blk * EXPERT_ROWS, 0, EXPERT_ROWS)
    last_e = jnp.max(jnp.where(counts > 0, jnp.arange(N_EXPERTS, dtype=I32), 0))
    e = jnp.where(valid, e, last_e)
    hi = jnp.where(valid, hi, 0)
    lo = jnp.where(valid, lo, 0)
    return blk.astype(I32), e.astype(I32), lo.astype(I32), hi.astype(I32)


def _experts(xs, items, w_gate, w_up, w_down):
    n_rows = xs.shape[0]
    n_items = items[0].shape[0]
    grid_spec = pltpu.PrefetchScalarGridSpec(
        num_scalar_prefetch=4, grid=(n_items,),
        in_specs=[pl.BlockSpec((EXPERT_ROWS, HALF), lambda i, b, e, lo, hi: (b[i], 0)),
                  pl.BlockSpec((None, D_MODEL, EXPERT_DIM), lambda i, b, e, lo, hi: (e[i], 0, 0)),
                  pl.BlockSpec((None, D_MODEL, EXPERT_DIM), lambda i, b, e, lo, hi: (e[i], 0, 0)),
                  pl.BlockSpec((None, EXPERT_DIM, D_MODEL), lambda i, b, e, lo, hi: (e[i], 0, 0))],
        out_specs=pl.BlockSpec((EXPERT_ROWS, HALF), lambda i, b, e, lo, hi: (b[i], 0)),
        scratch_shapes=[pltpu.VMEM((D_MODEL, 2 * EXPERT_DIM), BF16),
                        pltpu.VMEM((EXPERT_DIM, D_MODEL), BF16),
                        pltpu.VMEM((EXPERT_ROWS, D_MODEL), F32)])
    return pl.pallas_call(
        _expert_kernel, grid_spec=grid_spec,
        out_shape=jax.ShapeDtypeStruct((n_rows, HALF), I32),
        compiler_params=_cparams("arbitrary"), name="experts")(*items, xs, w_gate, w_up, w_down)


def _moe_out_kernel(x_ref, yg_ref, wt_ref, sgu_ref, sd_ref, g_ref, b_ref, o_ref):
    x = x_ref[...]
    wt = wt_ref[...]
    lo = jnp.zeros((x.shape[0], HALF), F32)
    hi = jnp.zeros((x.shape[0], HALF), F32)
    for k in range(TOP_K):
        w = yg_ref[k]
        wk = wt[:, k:k + 1]
        lo = lo + wk * _unpack_lo(w)
        hi = hi + wk * _unpack_hi(w)
    gu = _dot(x.astype(BF16), sgu_ref[...])
    g = gu[:, :EXPERT_DIM]
    hs = (g * jax.nn.sigmoid(g) * gu[:, EXPERT_DIM:]).astype(BF16)
    ffn = jnp.concatenate([lo, hi], axis=1) + _dot(hs, sd_ref[...])
    o_ref[...] = _ln(DEEPNORM_ALPHA * x + ffn, g_ref[...], b_ref[...])


def _moe_out(x, yg, wts, sh_gate, sh_up, sh_down, g, b):
    T, D = x.shape
    tm = ROW_TILE
    row = lambda w: pl.BlockSpec((tm, w), lambda i: (i, 0))
    full = lambda a: pl.BlockSpec(a.shape, lambda i: (0,) * a.ndim)
    sgu = jnp.concatenate([sh_gate, sh_up], axis=1).astype(BF16)
    sd = sh_down.astype(BF16)
    g = g.reshape(1, D)
    b = b.reshape(1, D)
    return pl.pallas_call(
        _moe_out_kernel, grid=(T // tm,),
        in_specs=[row(D), pl.BlockSpec((TOP_K, tm, HALF), lambda i: (0, i, 0)), row(TOP_K),
                  full(sgu), full(sd), full(g), full(b)],
        out_specs=row(D), out_shape=jax.ShapeDtypeStruct((T, D), F32),
        compiler_params=_cparams("parallel"), name="moe_out")(x, yg, wts, sgu, sd, g, b)


def _moe(x, router_w, router_b, w_gate, w_up, w_down, sh_gate, sh_up, sh_down, g, b):
    T, D = x.shape
    A = T * TOP_K
    eidx, wts, pos, cnt, xp = _router(x, router_w, router_b)
    counts = cnt[:, 0]
    seg_start = (jnp.cumsum(counts) - counts).astype(I32)
    dest = _dest_rows(eidx, pos, seg_start)
    xs = _scatter_rows(xp, dest)
    ys = _experts(xs, _expert_items(counts, A), w_gate, w_up, w_down)
    yg = _gather_rows(ys, dest.reshape(A)).reshape(TOP_K, T, HALF)
    return _moe_out(x, yg, wts.T, sh_gate, sh_up, sh_down, g, b)


def kernel(x, ln_in_g, ln_in_b, e_w_in, e_w_fourier, e_q_gain, e_k_gain, e_w_out, o_w_in, o_b_in, o_v_ln_g, o_v_ln_b, o_w_spatial, o_b_spatial, o_w_out, ln_mix_g, ln_mix_b, ln_ffn_g, ln_ffn_b, router_w, router_b, exp_w_gate, exp_w_up, exp_w_down, sh_w_gate, sh_w_up, sh_w_down):
    B, S, D = x.shape
    T = B * S
    h = _layer_norm(x.reshape(T, D), ln_in_g, ln_in_b)
    for i in range(DEPTH):
        j = i // 2
        if i % 2 == 0:
            a, q, kt, v2 = _even_in(h, e_w_in[j], e_q_gain[j], e_k_gain[j], B, S)
            a_out = _fourier(a, e_w_fourier[j], B, S)
            attn = _attention(q, kt, v2, B, S)
            h = _even_out(a_out, attn, e_w_out[j], h, ln_mix_g[i], ln_mix_b[i])
        else:
            h = _odd_mixer(h, o_w_in[j], o_b_in[j], o_v_ln_g[j], o_v_ln_b[j], o_w_spatial[j],
                           o_b_spatial[j], o_w_out[j], ln_mix_g[i], ln_mix_b[i])
        h = _moe(h, router_w[i], router_b[i], exp_w_gate[i], exp_w_up[i], exp_w_down[i],
                 sh_w_gate[i], sh_w_up[i], sh_w_down[i], ln_ffn_g[i], ln_ffn_b[i])
    return h.reshape(B, S, D)
```

```python
import functools
import math

import numpy as np
import jax
import jax.numpy as jnp
from jax import lax
from jax.experimental import pallas as pl
from jax.experimental.pallas import tpu as pltpu
from jax.experimental.pallas import tpu_sc as plsc

F32 = jnp.float32
BF16 = jnp.bfloat16
I32 = jnp.int32

D_MODEL = 1024
DEPTH = 4
GRID_W = 64
N_FGROUPS = 4
FGROUP_DIM = 128
F_WIDTH = N_FGROUPS * FGROUP_DIM
N_HEADS = 8
N_KV_HEADS = 2
HEAD_DIM = 64
Q_GROUP = N_HEADS // N_KV_HEADS
Q_WIDTH = N_HEADS * HEAD_DIM
KV_WIDTH = N_KV_HEADS * HEAD_DIM
ROPE_THETA = 10000.0
ROPE_PAIRS = HEAD_DIM // 4
EVEN_IN_WIDTH = F_WIDTH + Q_WIDTH + 2 * KV_WIDTH
CHUNK = 128
N_CGROUPS = 8
CGROUP_DIM = D_MODEL // N_CGROUPS
C_WIDTH = N_CGROUPS * CGROUP_DIM
N_EXPERTS = 64
EXPERT_DIM = 256
TOP_K = 8
N_EXPERT_GROUPS = 8
GROUP_SIZE = N_EXPERTS // N_EXPERT_GROUPS
TOPK_GROUPS = 4
ROUTE_SCALE = 2.5
LN_EPS = 1e-5
QK_EPS = 1e-6
DEEPNORM_ALPHA = (2 * DEPTH) ** 0.25

VMEM_LIMIT_BYTES = 56 * 1024 * 1024
ROW_TILE = 256
DFT_N1 = 64
DFT_KRON = 4
EXPERT_ROWS = 512
EXPERT_SUB_ROWS = 512
HALF = D_MODEL // 2
SC_WORKERS = 32
SC_CHUNK = 128
MOE_TOKEN_GROUPS = 2
ATT_TQ = 128
ATT_TK = 512
ATT_BOUND_SLACK = 1.0 + 2.0 ** -7
ATT_MIN_ROW_SUM = 2.0 ** -80
NEG_INF = float("-inf")


def _cparams(*sem):
    return pltpu.CompilerParams(dimension_semantics=sem, vmem_limit_bytes=VMEM_LIMIT_BYTES)


def _ln(x, g, b):
    mu = jnp.mean(x, axis=-1, keepdims=True)
    xc = x - mu
    var = jnp.mean(xc * xc, axis=-1, keepdims=True)
    return xc * lax.rsqrt(var + LN_EPS) * g + b


def _dot(a, b):
    return jnp.dot(a, b, preferred_element_type=F32)


def _pack_halves(y):
    lo = lax.bitcast_convert_type(y[:, :HALF].astype(BF16).astype(F32), I32)
    hi = lax.bitcast_convert_type(y[:, HALF:].astype(BF16).astype(F32), I32)
    return lax.shift_right_logical(lo, 16) | (hi & jnp.int32(-65536))


def _unpack_lo(w):
    return lax.bitcast_convert_type(lax.shift_left(w, 16), F32)


def _unpack_hi(w):
    return lax.bitcast_convert_type(w & jnp.int32(-65536), F32)


def _ln_kernel(x_ref, g_ref, b_ref, o_ref):
    o_ref[...] = _ln(x_ref[...], g_ref[...], b_ref[...])


def _layer_norm(x, g, b):
    T, D = x.shape
    row = pl.BlockSpec((ROW_TILE, D), lambda i: (i, 0))
    vec = pl.BlockSpec((1, D), lambda i: (0, 0))
    return pl.pallas_call(
        _ln_kernel, grid=(T // ROW_TILE,), in_specs=[row, vec, vec], out_specs=row,
        out_shape=jax.ShapeDtypeStruct((T, D), F32), compiler_params=_cparams("parallel"),
        name="ln_in")(x, g.reshape(1, D), b.reshape(1, D))


def _even_in_kernel(x_ref, w_ref, qm_ref, km_ref, qg_ref, kg_ref, cos_ref, sin_ref,
                    a_ref, q_ref, kt_ref, v_ref):
    tm = x_ref.shape[0]
    h = _dot(x_ref[...].astype(BF16), w_ref[...])
    a_ref[...] = h[:, :F_WIDTH].astype(BF16)
    q = h[:, F_WIDTH:F_WIDTH + Q_WIDTH]
    k = h[:, F_WIDTH + Q_WIDTH:F_WIDTH + Q_WIDTH + KV_WIDTH]
    v = h[:, F_WIDTH + Q_WIDTH + KV_WIDTH:]
    cos = cos_ref[...]
    sin = sin_ref[...]
    lane = lax.broadcasted_iota(I32, (tm, 128), 1)
    first_of_pair = (lane & ROPE_PAIRS) == 0

    def mean_sq(xf, m_ref):
        sq = xf * xf
        hi = sq.astype(BF16)
        lo = (sq - hi.astype(F32)).astype(BF16)
        return _dot(hi, m_ref[...]) + _dot(lo, m_ref[...])

    def rope(xn):
        sw = jnp.where(first_of_pair, pltpu.roll(xn, 128 - ROPE_PAIRS, 1), pltpu.roll(xn, ROPE_PAIRS, 1))
        return xn * cos + sw * sin

    qn = q * lax.rsqrt(mean_sq(q, qm_ref) + QK_EPS) * qg_ref[...]
    scale = math.log2(math.e) / math.sqrt(HEAD_DIM)
    for c in range(Q_WIDTH // 128):
        q_ref[:, c * 128:(c + 1) * 128] = (rope(qn[:, c * 128:(c + 1) * 128]) * scale).astype(BF16)
    kn = k * lax.rsqrt(mean_sq(k, km_ref) + QK_EPS) * kg_ref[...]
    kt_ref[...] = rope(kn).T.astype(BF16)
    ones_col = jnp.where(lane == HEAD_DIM, 1.0, 0.0)
    low = lane < HEAD_DIM
    v_ref[0] = jnp.where(low, v, ones_col).astype(BF16)
    v_ref[1] = jnp.where(low, pltpu.roll(v, HEAD_DIM, 1), ones_col).astype(BF16)


def _rope_tables(S):
    rows = S // GRID_W
    t = np.arange(S)
    inv = ROPE_THETA ** (-np.arange(ROPE_PAIRS, dtype=np.float64) / ROPE_PAIRS)
    ang_r = (t // GRID_W)[:, None] * inv
    ang_c = (t % GRID_W)[:, None] * inv
    del rows
    cos = np.concatenate([np.cos(ang_r), np.cos(ang_r), np.cos(ang_c), np.cos(ang_c)], axis=1)
    sin = np.concatenate([-np.sin(ang_r), np.sin(ang_r), -np.sin(ang_c), np.sin(ang_c)], axis=1)
    return (jnp.asarray(np.tile(cos, (1, 2)), F32), jnp.asarray(np.tile(sin, (1, 2)), F32))


def _head_mean_matrix(width):
    m = np.kron(np.eye(width // HEAD_DIM), np.full((HEAD_DIM, HEAD_DIM), 1.0 / HEAD_DIM))
    return jnp.asarray(m, BF16)


def _even_in(x, w_in, q_gain, k_gain, B, S):
    T, D = x.shape
    tm = ROW_TILE
    ns = S // tm
    cos, sin = _rope_tables(S)
    row = lambda w: pl.BlockSpec((tm, w), lambda i: (i, 0))
    full = lambda a: pl.BlockSpec(a.shape, lambda i: (0,) * a.ndim)
    tab = pl.BlockSpec((tm, 128), lambda i: (i % ns, 0))
    w = w_in.astype(BF16)
    qm = _head_mean_matrix(Q_WIDTH)
    km = _head_mean_matrix(KV_WIDTH)
    qg = jnp.tile(q_gain.astype(F32), N_HEADS).reshape(1, Q_WIDTH)
    kg = jnp.tile(k_gain.astype(F32), N_KV_HEADS).reshape(1, KV_WIDTH)
    return pl.pallas_call(
        _even_in_kernel, grid=(T // tm,),
        in_specs=[row(D), full(w), full(qm), full(km), full(qg), full(kg), tab, tab],
        out_specs=[row(F_WIDTH), row(Q_WIDTH),
                   pl.BlockSpec((None, KV_WIDTH, tm), lambda i: (i // ns, 0, i % ns)),
                   pl.BlockSpec((N_KV_HEADS, tm, 128), lambda i: (0, i, 0))],
        out_shape=[jax.ShapeDtypeStruct((T, F_WIDTH), BF16),
                   jax.ShapeDtypeStruct((T, Q_WIDTH), BF16),
                   jax.ShapeDtypeStruct((B, KV_WIDTH, S), BF16),
                   jax.ShapeDtypeStruct((N_KV_HEADS, T, 128), BF16)],
        compiler_params=_cparams("parallel"), name="even_in")(x, w, qm, km, qg, kg, cos, sin)


def _fourier_kernel(a_ref, dftc_ref, taba_ref, kc_ref, ks_ref, wf_ref, o_ref,
                    zr_ref, zi_ref, ur_ref, ui_ref, y_ref):
    S = a_ref.shape[0]
    n1_count = DFT_N1
    n2_count = S // DFT_N1
    blk = DFT_KRON * DFT_N1
    scale = 1.0 / math.sqrt(S * FGROUP_DIM)

    def channel_dft(j, carry):
        rows = pl.ds(pl.multiple_of(j * blk, blk), blk)
        zz = _dot(a_ref[rows, :], dftc_ref[...])
        zr_ref[rows, :] = zz[:, :FGROUP_DIM]
        zi_ref[rows, :] = zz[:, FGROUP_DIM:]
        return carry

    lax.fori_loop(0, S // blk, channel_dft, 0)

    def stage_a(n1, carry):
        rows = pl.ds(n1, n2_count, stride=n1_count)
        zn = jnp.concatenate([zr_ref[rows, :], zi_ref[rows, :]], axis=1).astype(BF16)
        r = _dot(taba_ref[n1], zn)
        ur_ref[rows, :] = r[:n2_count, :FGROUP_DIM] + r[n2_count:, FGROUP_DIM:]
        ui_ref[rows, :] = r[:n2_count, FGROUP_DIM:] - r[n2_count:, :FGROUP_DIM]
        return carry

    lax.fori_loop(0, n1_count, stage_a, 0)

    def stage_b(j, carry):
        rows = pl.ds(pl.multiple_of(j * blk, blk), blk)
        re = (_dot(kc_ref[...], ur_ref[rows, :].astype(BF16))
              + _dot(ks_ref[...], ui_ref[rows, :].astype(BF16)))
        out = _dot((re * scale).astype(BF16), wf_ref[...])
        for k2l in range(DFT_KRON):
            y_ref[pl.ds(j * DFT_KRON + k2l, n1_count, stride=n2_count), :] = (
                out[k2l * n1_count:(k2l + 1) * n1_count])
        return carry

    lax.fori_loop(0, S // blk, stage_b, 0)
    o_ref[...] = y_ref[...].astype(BF16)


def _dft_tables(S):
    n1c, n2c = DFT_N1, S // DFT_N1
    c = np.arange(FGROUP_DIM)
    ang = 2 * np.pi * np.outer(c, c) / FGROUP_DIM
    dftc = np.concatenate([np.cos(ang), -np.sin(ang)], axis=1)
    n1 = np.arange(n1c)[:, None, None]
    k2 = np.arange(n2c)[None, :, None]
    n2 = np.arange(n2c)[None, None, :]
    th = 2 * np.pi * (n2 * k2 / n2c + n1 * k2 / S)
    taba = np.concatenate([np.cos(th), np.sin(th)], axis=1)
    k1 = np.arange(n1c)
    g = 2 * np.pi * np.outer(k1, k1) / n1c
    eye = np.eye(DFT_KRON)
    kc = np.kron(eye, np.cos(g))
    ks = np.kron(eye, np.sin(g))
    return tuple(jnp.asarray(t, BF16) for t in (dftc, taba, kc, ks))


def _fourier(a, w_fourier, B, S):
    T = a.shape[0]
    dftc, taba, kc, ks = _dft_tables(S)
    full = lambda t: pl.BlockSpec(t.shape, lambda b, g: (0,) * t.ndim)
    blk = pl.BlockSpec((S, FGROUP_DIM), lambda b, g: (b, g))
    return pl.pallas_call(
        _fourier_kernel, grid=(B, N_FGROUPS),
        in_specs=[blk, full(dftc), full(taba), full(kc), full(ks),
                  pl.BlockSpec((None, FGROUP_DIM, FGROUP_DIM), lambda b, g: (g, 0, 0))],
        out_specs=blk,
        out_shape=jax.ShapeDtypeStruct((T, F_WIDTH), BF16),
        scratch_shapes=[pltpu.VMEM((S, FGROUP_DIM), F32) for _ in range(5)],
        compiler_params=_cparams("parallel", "parallel"), name="fourier")(
            a, dftc, taba, kc, ks, w_fourier.astype(BF16))


def _attn_kernel(q_ref, kt_ref, v_ref, o_ref, qs_ref, kmax_ref, acc_ref, m_ref):
    tq = q_ref.shape[0]
    n_keys = kt_ref.shape[1]
    tk = min(ATT_TK, n_keys)
    n_chunks = n_keys // tk

    @pl.when(pl.program_id(2) == 0)
    def _():
        def body(c, best):
            k = kt_ref[:, pl.ds(pl.multiple_of(c * tk, tk), tk)].astype(F32)
            return jnp.maximum(best, jnp.sum(k * k, axis=0, keepdims=True))
        best = lax.fori_loop(0, n_chunks, body, jnp.zeros((1, tk), F32))
        kmax_ref[...] = jnp.broadcast_to(jnp.sqrt(jnp.max(best, axis=1, keepdims=True)), kmax_ref.shape)

    for g in range(Q_GROUP):
        qs_ref[g * tq:(g + 1) * tq, :] = q_ref[:, g * HEAD_DIM:(g + 1) * HEAD_DIM]
    qf = qs_ref[...].astype(F32)
    bound = jnp.sqrt(jnp.sum(qf * qf, axis=1, keepdims=True)) * kmax_ref[0:1, 0:1] * ATT_BOUND_SLACK

    def chunk(c):
        cols = pl.ds(pl.multiple_of(c * tk, tk), tk)
        return _dot(qs_ref[...], kt_ref[:, cols]), v_ref[cols, :]

    def fast(c, carry):
        s, v = chunk(c)
        acc_ref[...] += _dot(jnp.exp2(s - bound).astype(BF16), v)
        return carry

    acc_ref[...] = jnp.zeros(acc_ref.shape, F32)
    lax.fori_loop(0, n_chunks, fast, 0)
    underflow = jnp.min(acc_ref[:, HEAD_DIM:HEAD_DIM + 1]) < ATT_MIN_ROW_SUM

    @pl.when(underflow)
    def _():
        def safe(c, carry):
            s, v = chunk(c)
            m_old = m_ref[...]
            m_new = jnp.maximum(m_old, jnp.max(s, axis=1, keepdims=True))
            acc_ref[...] = jnp.exp2(m_old - m_new) * acc_ref[...] + _dot(jnp.exp2(s - m_new).astype(BF16), v)
            m_ref[...] = m_new
            return carry

        m_ref[...] = jnp.full(m_ref.shape, NEG_INF, F32)
        acc_ref[...] = jnp.zeros(acc_ref.shape, F32)
        lax.fori_loop(0, n_chunks, safe, 0)

    acc = acc_ref[...]
    o = acc[:, :HEAD_DIM] / acc[:, HEAD_DIM:HEAD_DIM + 1]
    o_ref[...] = jnp.concatenate([o[g * tq:(g + 1) * tq] for g in range(Q_GROUP)], axis=1).astype(BF16)


def _attention(q, kt, v2, B, S):
    T = q.shape[0]
    tq = ATT_TQ
    nq = S // tq
    gw = Q_GROUP * HEAD_DIM
    rows = Q_GROUP * tq
    return pl.pallas_call(
        _attn_kernel, grid=(B, N_KV_HEADS, nq),
        in_specs=[pl.BlockSpec((tq, gw), lambda b, h, i: (b * nq + i, h)),
                  pl.BlockSpec((None, HEAD_DIM, S), lambda b, h, i: (b, h, 0)),
                  pl.BlockSpec((None, None, S, 128), lambda b, h, i: (h, b, 0, 0))],
        out_specs=pl.BlockSpec((tq, gw), lambda b, h, i: (b * nq + i, h)),
        out_shape=jax.ShapeDtypeStruct((T, Q_WIDTH), BF16),
        scratch_shapes=[pltpu.VMEM((rows, HEAD_DIM), BF16), pltpu.VMEM((8, 128), F32),
                        pltpu.VMEM((rows, 128), F32), pltpu.VMEM((rows, 1), F32)],
        compiler_params=_cparams("parallel", "parallel", "arbitrary"),
        name="attention")(q, kt, v2.reshape(N_KV_HEADS, B, S, 128))


def _even_out_kernel(a_ref, t_ref, wa_ref, wt_ref, x_ref, g_ref, b_ref, o_ref):
    mix = _dot(a_ref[...], wa_ref[...]) + _dot(t_ref[...], wt_ref[...])
    o_ref[...] = _ln(DEEPNORM_ALPHA * x_ref[...] + mix, g_ref[...], b_ref[...])


def _even_out(a_out, attn, w_out, x, g, b):
    T, D = x.shape
    tm = ROW_TILE
    row = lambda w: pl.BlockSpec((tm, w), lambda i: (i, 0))
    full = lambda a: pl.BlockSpec(a.shape, lambda i: (0,) * a.ndim)
    wa = w_out[:F_WIDTH].astype(BF16)
    wt = w_out[F_WIDTH:].astype(BF16)
    g = g.reshape(1, D)
    b = b.reshape(1, D)
    return pl.pallas_call(
        _even_out_kernel, grid=(T // tm,),
        in_specs=[row(F_WIDTH), row(Q_WIDTH), full(wa), full(wt), row(D), full(g), full(b)],
        out_specs=row(D), out_shape=jax.ShapeDtypeStruct((T, D), F32),
        compiler_params=_cparams("parallel"), name="even_out")(a_out, attn, wa, wt, x, g, b)


def _odd_kernel(x_ref, wi_ref, bi_ref, vg_ref, vb_ref, ws_ref, bs_ref, wo_ref, g_ref, b_ref, o_ref,
                gate_ref):
    tm = x_ref.shape[0]
    x = x_ref[...]
    h = _dot(x.astype(BF16), wi_ref[...]) + bi_ref[...]
    h = 0.5 * h * (1.0 + lax.erf(h * (1.0 / math.sqrt(2.0))))
    u = h[:, :C_WIDTH]
    v = _ln(h[:, C_WIDTH:], vg_ref[...], vb_ref[...]).astype(BF16)
    for c in range(tm // CHUNK):
        r0 = c * CHUNK
        for gi in range(N_CGROUPS):
            l0 = gi * CGROUP_DIM
            sv = _dot(ws_ref[gi], v[r0:r0 + CHUNK, l0:l0 + CGROUP_DIM]) + bs_ref[gi]
            gate_ref[r0:r0 + CHUNK, l0:l0 + CGROUP_DIM] = (
                u[r0:r0 + CHUNK, l0:l0 + CGROUP_DIM] * sv).astype(BF16)
    mix = _dot(gate_ref[...], wo_ref[...])
    o_ref[...] = _ln(DEEPNORM_ALPHA * x + mix, g_ref[...], b_ref[...])


def _odd_mixer(x, w_in, b_in, v_g, v_b, w_s, b_s, w_out, g, b):
    T, D = x.shape
    tm = ROW_TILE
    row = pl.BlockSpec((tm, D), lambda i: (i, 0))
    full = lambda a: pl.BlockSpec(a.shape, lambda i: (0,) * a.ndim)
    args = [w_in.astype(BF16), b_in.reshape(1, 2 * C_WIDTH), v_g.reshape(1, C_WIDTH),
            v_b.reshape(1, C_WIDTH), w_s.astype(BF16),
            jnp.broadcast_to(b_s[:, :, None], (N_CGROUPS, CHUNK, CGROUP_DIM)).astype(F32),
            w_out.astype(BF16), g.reshape(1, D), b.reshape(1, D)]
    return pl.pallas_call(
        _odd_kernel, grid=(T // tm,),
        in_specs=[row] + [full(a) for a in args],
        out_specs=row, out_shape=jax.ShapeDtypeStruct((T, D), F32),
        scratch_shapes=[pltpu.VMEM((tm, C_WIDTH), BF16)],
        compiler_params=_cparams("parallel"), name="odd_mixer")(x, *args)


def _router_kernel(x_ref, w_ref, rb_ref, tri_ref, eidx_ref, wts_ref, pos_ref, cnt_ref, xp_ref, run_ref):
    tm = x_ref.shape[0]
    i = pl.program_id(0)

    @pl.when(i == 0)
    def _():
        run_ref[...] = jnp.zeros(run_ref.shape, F32)

    x = x_ref[...]
    xp_ref[...] = _pack_halves(x)
    xh = x.astype(BF16)
    xl = (x - xh.astype(F32)).astype(BF16)
    nt = (((1,), (1,)), ((), ()))
    dg = lambda a, c: lax.dot_general(a, c, nt, preferred_element_type=F32)
    logits = dg(w_ref[0], xh) + dg(w_ref[0], xl) + dg(w_ref[1], xh)
    scores = jax.nn.sigmoid(logits)
    sel = scores + rb_ref[...]

    i8 = lax.broadcasted_iota(I32, (GROUP_SIZE, tm), 0)
    gsc_rows = []
    for gidx in range(N_EXPERT_GROUPS):
        sg = sel[gidx * GROUP_SIZE:(gidx + 1) * GROUP_SIZE, :]
        m1 = jnp.max(sg, axis=0, keepdims=True)
        f1 = jnp.min(jnp.where(sg == m1, i8, GROUP_SIZE), axis=0, keepdims=True)
        m2 = jnp.max(jnp.where(i8 == f1, NEG_INF, sg), axis=0, keepdims=True)
        gsc_rows.append(m1 + m2)
    gsc = jnp.concatenate(gsc_rows, axis=0)

    gsel = jnp.zeros(gsc.shape, F32)
    for _ in range(TOPK_GROUPS):
        m = jnp.max(gsc, axis=0, keepdims=True)
        f = jnp.min(jnp.where(gsc == m, i8, N_EXPERT_GROUPS), axis=0, keepdims=True)
        pick = i8 == f
        gsel = jnp.where(pick, 1.0, gsel)
        gsc = jnp.where(pick, NEG_INF, gsc)
    esel = jnp.concatenate(
        [jnp.broadcast_to(gsel[gidx:gidx + 1, :], (GROUP_SIZE, tm)) for gidx in range(N_EXPERT_GROUPS)],
        axis=0)

    cur = jnp.where(esel > 0.0, sel, NEG_INF)
    ei = lax.broadcasted_iota(I32, cur.shape, 0)
    idx_rows, sc_rows = [], []
    chosen = jnp.zeros(cur.shape, F32)
    for _ in range(TOP_K):
        m = jnp.max(cur, axis=0, keepdims=True)
        f = jnp.min(jnp.where(cur == m, ei, N_EXPERTS), axis=0, keepdims=True)
        pick = ei == f
        idx_rows.append(f)
        sc_rows.append(jnp.sum(jnp.where(pick, scores, 0.0), axis=0, keepdims=True))
        chosen = jnp.where(pick, 1.0, chosen)
        cur = jnp.where(pick, NEG_INF, cur)
    eidx = jnp.concatenate(idx_rows, axis=0)
    sc = jnp.concatenate(sc_rows, axis=0)
    eidx_ref[...] = eidx
    wts_ref[...] = sc / jnp.sum(sc, axis=0, keepdims=True) * ROUTE_SCALE

    before = _dot(chosen.astype(BF16), tri_ref[...]) + run_ref[...]
    pos_rows = [jnp.sum(jnp.where(ei == idx_rows[k], before, 0.0), axis=0, keepdims=True)
                for k in range(TOP_K)]
    pos_ref[...] = jnp.concatenate(pos_rows, axis=0).astype(I32)
    run_new = run_ref[...] + jnp.sum(chosen, axis=1, keepdims=True)
    run_ref[...] = run_new
    cnt_ref[...] = jnp.broadcast_to(run_new, cnt_ref.shape).astype(I32)


def _router(x, router_w, router_b, first_tile, n_tiles):
    D = x.shape[1]
    tm = ROW_TILE
    T = n_tiles * tm
    wt = router_w.T.astype(F32)
    wh = wt.astype(BF16)
    wl = (wt - wh.astype(F32)).astype(BF16)
    w2 = jnp.stack([wh, wl])
    rb = router_b.astype(F32).reshape(N_EXPERTS, 1)
    tri = jnp.asarray(np.triu(np.ones((tm, tm)), 1), BF16)
    full = lambda a: pl.BlockSpec(a.shape, lambda i: (0,) * a.ndim)
    col = pl.BlockSpec((TOP_K, tm), lambda i: (0, i))
    return pl.pallas_call(
        _router_kernel, grid=(n_tiles,),
        in_specs=[pl.BlockSpec((tm, D), lambda i: (i + first_tile, 0)), full(w2), full(rb), full(tri)],
        out_specs=[col, col, col, pl.BlockSpec((N_EXPERTS, 128), lambda i: (0, 0)),
                   pl.BlockSpec((tm, HALF), lambda i: (i, 0))],
        out_shape=[jax.ShapeDtypeStruct((TOP_K, T), I32), jax.ShapeDtypeStruct((TOP_K, T), F32),
                   jax.ShapeDtypeStruct((TOP_K, T), I32), jax.ShapeDtypeStruct((N_EXPERTS, 128), I32),
                   jax.ShapeDtypeStruct((T, HALF), I32)],
        scratch_shapes=[pltpu.VMEM((N_EXPERTS, 1), F32)],
        compiler_params=_cparams("arbitrary"), name="router")(x, w2, rb, tri)


def _dest_kernel(start_ref, eidx_ref, pos_ref, o_ref):
    e = eidx_ref[...]
    acc = pos_ref[...]
    for j in range(N_EXPERTS):
        acc = acc + jnp.where(e == j, start_ref[j], 0)
    o_ref[...] = acc


def _dest_rows(eidx, pos, seg_start):
    K, T = eidx.shape
    tl = min(T, 2048)
    blk = pl.BlockSpec((K, tl), lambda i, s: (0, i))
    grid_spec = pltpu.PrefetchScalarGridSpec(
        num_scalar_prefetch=1, grid=(T // tl,), in_specs=[blk, blk], out_specs=blk)
    return pl.pallas_call(
        _dest_kernel, grid_spec=grid_spec, out_shape=jax.ShapeDtypeStruct((K, T), I32),
        compiler_params=_cparams("parallel"), name="dest_rows")(seg_start, eidx, pos)


def _gather_rows(table, idx):
    n_rows = idx.shape[0]
    width = table.shape[1]
    per_worker = n_rows // SC_WORKERS
    n_chunks = per_worker // SC_CHUNK
    assert per_worker * SC_WORKERS == n_rows and n_chunks * SC_CHUNK == per_worker
    mesh = plsc.VectorSubcoreMesh(core_axis_name="c", subcore_axis_name="s")

    @functools.partial(
        pl.kernel, mesh=mesh,
        out_type=jax.ShapeDtypeStruct((n_rows, width), table.dtype),
        scratch_types=[pltpu.VMEM((SC_CHUNK,), I32), pltpu.VMEM((SC_CHUNK, width), table.dtype),
                       pltpu.SemaphoreType.DMA])
    def gather(table_hbm, idx_hbm, out_hbm, idx_v, rows_v, sem):
        wid = lax.axis_index("s") * 2 + lax.axis_index("c")
        base = wid * per_worker

        @pl.loop(0, n_chunks)
        def _(j):
            off = base + j * SC_CHUNK
            pltpu.sync_copy(idx_hbm.at[pl.ds(off, SC_CHUNK)], idx_v)
            pltpu.async_copy(table_hbm.at[idx_v], rows_v, sem).wait()
            pltpu.sync_copy(rows_v, out_hbm.at[pl.ds(off, SC_CHUNK)])

    return gather(table, idx)


def _scatter_rows(rows, dest):
    n_tok, width = rows.shape
    n_dst = dest.shape[0]
    per_worker = n_tok // SC_WORKERS
    n_chunks = per_worker // SC_CHUNK
    assert per_worker * SC_WORKERS == n_tok and n_chunks * SC_CHUNK == per_worker
    mesh = plsc.VectorSubcoreMesh(core_axis_name="c", subcore_axis_name="s")

    @functools.partial(
        pl.kernel, mesh=mesh,
        out_type=jax.ShapeDtypeStruct((n_dst * n_tok, width), rows.dtype),
        scratch_types=[pltpu.VMEM((n_dst, SC_CHUNK), I32), pltpu.VMEM((SC_CHUNK, width), rows.dtype),
                       pltpu.SemaphoreType.DMA])
    def scatter(rows_hbm, dest_hbm, out_hbm, idx_v, rows_v, sem):
        wid = lax.axis_index("s") * 2 + lax.axis_index("c")
        base = wid * per_worker

        @pl.loop(0, n_chunks)
        def _(j):
            off = base + j * SC_CHUNK
            pltpu.sync_copy(dest_hbm.at[:, pl.ds(off, SC_CHUNK)], idx_v)
            pltpu.sync_copy(rows_hbm.at[pl.ds(off, SC_CHUNK)], rows_v)
            copies = [pltpu.async_copy(rows_v, out_hbm.at[idx_v.at[k]], sem) for k in range(n_dst)]
            for c in copies:
                c.wait()

    return scatter(rows, dest)


def _expert_kernel(blk_ref, exp_ref, lo_ref, hi_ref, xs_ref, wg_ref, wu_ref, wd_ref, ys_ref,
                   wgu_s, wd_s, acc_s):
    i = pl.program_id(0)
    prev = jnp.maximum(i - 1, 0)

    @pl.when(jnp.logical_or(i == 0, exp_ref[i] != exp_ref[prev]))
    def _():
        wgu_s[:, :EXPERT_DIM] = wg_ref[...].astype(BF16)
        wgu_s[:, EXPERT_DIM:] = wu_ref[...].astype(BF16)
        wd_s[...] = wd_ref[...].astype(BF16)

    @pl.when(i == 0)
    def _():
        acc_s[...] = jnp.zeros(acc_s.shape, F32)

    first = jnp.logical_or(i == 0, blk_ref[i] != blk_ref[prev])
    for r0 in range(0, EXPERT_ROWS, EXPERT_SUB_ROWS):
        rows = slice(r0, r0 + EXPERT_SUB_ROWS)
        w = xs_ref[rows, :]
        xlo = _unpack_lo(w).astype(BF16)
        xhi = _unpack_hi(w).astype(BF16)
        gu = _dot(xlo, wgu_s[:HALF, :]) + _dot(xhi, wgu_s[HALF:, :])
        g = gu[:, :EXPERT_DIM]
        hb = (g * jax.nn.sigmoid(g) * gu[:, EXPERT_DIM:]).astype(BF16)
        row = r0 + lax.broadcasted_iota(I32, (EXPERT_SUB_ROWS, 1), 0)
        mine = jnp.logical_and(row >= lo_ref[i], row < hi_ref[i])
        y = jnp.where(mine, _dot(hb, wd_s[...]), 0.0)
        acc = jnp.where(first, y, acc_s[rows, :] + y)
        acc_s[rows, :] = acc
        ys_ref[rows, :] = _pack_halves(acc)


def _expert_items(counts, n_rows):
    n_blocks = n_rows // EXPERT_ROWS
    n_items = n_blocks + N_EXPERTS - 1
    end = jnp.cumsum(counts)
    start = end - counts
    first_blk = start // EXPERT_ROWS
    n_blk = jnp.where(counts > 0, (end - 1) // EXPERT_ROWS - first_blk + 1, 0)
    item_end = jnp.cumsum(n_blk)
    item_start = item_end - n_blk
    slot = jnp.arange(n_items, dtype=I32)
    e = jnp.minimum(jnp.sum((item_end[None, :] <= slot[:, None]).astype(I32), axis=1), N_EXPERTS - 1)
    onehot = (e[:, None] == jnp.arange(N_EXPERTS, dtype=I32)[None, :]).astype(I32)
    pick = lambda v: jnp.sum(onehot * v[None, :], axis=1)
    valid = slot < item_end[-1]
    blk = jnp.where(valid, pick(first_blk) + slot - pick(item_start), n_blocks - 1)
    lo = jnp.clip(pick(start) - blk * EXPERT_ROWS, 0, EXPERT_ROWS)
    hi = jnp.clip(pick(end) - blk * EXPERT_ROWS, 0, EXPERT_ROWS)
    last_e = jnp.max(jnp.where(counts > 0, jnp.arange(N_EXPERTS, dtype=I32), 0))
    e = jnp.where(valid, e, last_e)
    hi = jnp.where(valid, hi, 0)
    lo = jnp.where(valid, lo, 0)
    return blk.astype(I32), e.astype(I32), lo.astype(I32), hi.astype(I32)


def _experts(xs, items, w_gate, w_up, w_down, layer):
    n_rows = xs.shape[0]
    n_items = items[0].shape[0]
    wmap = lambda i, b, e, lo, hi: (layer, e[i], 0, 0)
    grid_spec = pltpu.PrefetchScalarGridSpec(
        num_scalar_prefetch=4, grid=(n_items,),
        in_specs=[pl.BlockSpec((EXPERT_ROWS, HALF), lambda i, b, e, lo, hi: (b[i], 0)),
                  pl.BlockSpec((None, None, D_MODEL, EXPERT_DIM), wmap),
                  pl.BlockSpec((None, None, D_MODEL, EXPERT_DIM), wmap),
                  pl.BlockSpec((None, None, EXPERT_DIM, D_MODEL), wmap)],
        out_specs=pl.BlockSpec((EXPERT_ROWS, HALF), lambda i, b, e, lo, hi: (b[i], 0)),
        scratch_shapes=[pltpu.VMEM((D_MODEL, 2 * EXPERT_DIM), BF16),
                        pltpu.VMEM((EXPERT_DIM, D_MODEL), BF16),
                        pltpu.VMEM((EXPERT_ROWS, D_MODEL), F32)])
    return pl.pallas_call(
        _expert_kernel, grid_spec=grid_spec,
        out_shape=jax.ShapeDtypeStruct((n_rows, HALF), I32),
        compiler_params=_cparams("arbitrary"), name="experts")(*items, xs, w_gate, w_up, w_down)


def _moe_out_kernel(x_ref, yg_ref, wt_ref, sgu_ref, sd_ref, g_ref, b_ref, *rest):
    o_ref = rest[-1]
    x = x_ref[...]
    wt = wt_ref[...]
    lo = jnp.zeros((x.shape[0], HALF), F32)
    hi = jnp.zeros((x.shape[0], HALF), F32)
    for k in range(TOP_K):
        w = yg_ref[k]
        wk = wt[:, k:k + 1]
        lo = lo + wk * _unpack_lo(w)
        hi = hi + wk * _unpack_hi(w)
    gu = _dot(x.astype(BF16), sgu_ref[...])
    g = gu[:, :EXPERT_DIM]
    hs = (g * jax.nn.sigmoid(g) * gu[:, EXPERT_DIM:]).astype(BF16)
    ffn = jnp.concatenate([lo, hi], axis=1) + _dot(hs, sd_ref[...])
    o_ref[...] = _ln(DEEPNORM_ALPHA * x + ffn, g_ref[...], b_ref[...])


def _moe_out(x, yg, wts, sh_gate, sh_up, sh_down, g, b, first_tile, partial_out):
    T, D = x.shape
    tm = ROW_TILE
    n_tiles = yg.shape[1] // tm
    xrow = pl.BlockSpec((tm, D), lambda i: (i + first_tile, 0))
    full = lambda a: pl.BlockSpec(a.shape, lambda i: (0,) * a.ndim)
    sgu = jnp.concatenate([sh_gate, sh_up], axis=1).astype(BF16)
    sd = sh_down.astype(BF16)
    g = g.reshape(1, D)
    b = b.reshape(1, D)
    args = [x, yg, wts, sgu, sd, g, b]
    in_specs = [xrow, pl.BlockSpec((TOP_K, tm, HALF), lambda i: (0, i, 0)),
                pl.BlockSpec((tm, TOP_K), lambda i: (i, 0)), full(sgu), full(sd), full(g), full(b)]
    aliases = {}
    if partial_out is not None:
        args.append(partial_out)
        in_specs.append(pl.BlockSpec(memory_space=pl.ANY))
        aliases = {len(args) - 1: 0}
    return pl.pallas_call(
        _moe_out_kernel, grid=(n_tiles,), in_specs=in_specs,
        out_specs=xrow, out_shape=jax.ShapeDtypeStruct((T, D), F32),
        input_output_aliases=aliases,
        compiler_params=_cparams("parallel"), name="moe_out")(*args)


def _moe(x, router_w, router_b, w_gate, w_up, w_down, layer, sh_gate, sh_up, sh_down, g, b):
    T = x.shape[0]
    tiles = T // ROW_TILE // MOE_TOKEN_GROUPS
    tg = tiles * ROW_TILE
    out = None
    for grp in range(MOE_TOKEN_GROUPS):
        eidx, wts, pos, cnt, xp = _router(x, router_w, router_b, grp * tiles, tiles)
        counts = cnt[:, 0]
        seg_start = (jnp.cumsum(counts) - counts).astype(I32)
        dest = _dest_rows(eidx, pos, seg_start)
        xs = _scatter_rows(xp, dest)
        ys = _experts(xs, _expert_items(counts, tg * TOP_K), w_gate, w_up, w_down, layer)
        yg = _gather_rows(ys, dest.reshape(tg * TOP_K)).reshape(TOP_K, tg, HALF)
        out = _moe_out(x, yg, wts.T, sh_gate, sh_up, sh_down, g, b, grp * tiles, out)
    return out


def kernel(x, ln_in_g, ln_in_b, e_w_in, e_w_fourier, e_q_gain, e_k_gain, e_w_out, o_w_in, o_b_in, o_v_ln_g, o_v_ln_b, o_w_spatial, o_b_spatial, o_w_out, ln_mix_g, ln_mix_b, ln_ffn_g, ln_ffn_b, router_w, router_b, exp_w_gate, exp_w_up, exp_w_down, sh_w_gate, sh_w_up, sh_w_down):
    B, S, D = x.shape
    T = B * S
    h = _layer_norm(x.reshape(T, D), ln_in_g, ln_in_b)
    for i in range(DEPTH):
        j = i // 2
        if i % 2 == 0:
            a, q, kt, v2 = _even_in(h, e_w_in[j], e_q_gain[j], e_k_gain[j], B, S)
            a_out = _fourier(a, e_w_fourier[j], B, S)
            attn = _attention(q, kt, v2, B, S)
            h = _even_out(a_out, attn, e_w_out[j], h, ln_mix_g[i], ln_mix_b[i])
        else:
            h = _odd_mixer(h, o_w_in[j], o_b_in[j], o_v_ln_g[j], o_v_ln_b[j], o_w_spatial[j],
                           o_b_spatial[j], o_w_out[j], ln_mix_g[i], ln_mix_b[i])
        h = _moe(h, router_w[i], router_b[i], exp_w_gate, exp_w_up, exp_w_down, i,
                 sh_w_gate[i], sh_w_up[i], sh_w_down[i], ln_ffn_g[i], ln_ffn_b[i])
    return h.reshape(B, S, D)
```

```python
import functools
import math

import numpy as np
import jax
import jax.numpy as jnp
from jax import lax
from jax.experimental import pallas as pl
from jax.experimental.pallas import tpu as pltpu
from jax.experimental.pallas import tpu_sc as plsc

F32 = jnp.float32
BF16 = jnp.bfloat16
I32 = jnp.int32

D_MODEL = 1024
DEPTH = 4
GRID_W = 64
N_FGROUPS = 4
FGROUP_DIM = 128
F_WIDTH = N_FGROUPS * FGROUP_DIM
N_HEADS = 8
N_KV_HEADS = 2
HEAD_DIM = 64
Q_GROUP = N_HEADS // N_KV_HEADS
Q_WIDTH = N_HEADS * HEAD_DIM
KV_WIDTH = N_KV_HEADS * HEAD_DIM
ROPE_THETA = 10000.0
ROPE_PAIRS = HEAD_DIM // 4
EVEN_IN_WIDTH = F_WIDTH + Q_WIDTH + 2 * KV_WIDTH
CHUNK = 128
N_CGROUPS = 8
CGROUP_DIM = D_MODEL // N_CGROUPS
C_WIDTH = N_CGROUPS * CGROUP_DIM
N_EXPERTS = 64
EXPERT_DIM = 256
TOP_K = 8
N_EXPERT_GROUPS = 8
GROUP_SIZE = N_EXPERTS // N_EXPERT_GROUPS
TOPK_GROUPS = 4
ROUTE_SCALE = 2.5
LN_EPS = 1e-5
QK_EPS = 1e-6
DEEPNORM_ALPHA = (2 * DEPTH) ** 0.25

VMEM_LIMIT_BYTES = 56 * 1024 * 1024
ROW_TILE = 256
DFT_N1 = 64
DFT_KRON = 4
ROUTER_TILE = 1024
EXPERT_ROWS = 1024
EXPERT_SUB_ROWS = 512
HALF = D_MODEL // 2
SC_WORKERS = 32
SC_CHUNK = 128
MOE_TOKEN_GROUPS = 2
ATT_TQ = 128
ATT_TK = 512
ATT_BOUND_SLACK = 1.0 + 2.0 ** -7
ATT_MIN_ROW_SUM = 2.0 ** -80
NEG_INF = float("-inf")


def _cparams(*sem):
    return pltpu.CompilerParams(dimension_semantics=sem, vmem_limit_bytes=VMEM_LIMIT_BYTES)


def _ln(x, g, b):
    mu = jnp.mean(x, axis=-1, keepdims=True)
    xc = x - mu
    var = jnp.mean(xc * xc, axis=-1, keepdims=True)
    return xc * lax.rsqrt(var + LN_EPS) * g + b


def _dot(a, b):
    return jnp.dot(a, b, preferred_element_type=F32)


def _pack_halves(y):
    lo = lax.bitcast_convert_type(y[:, :HALF].astype(BF16).astype(F32), I32)
    hi = lax.bitcast_convert_type(y[:, HALF:].astype(BF16).astype(F32), I32)
    return lax.shift_right_logical(lo, 16) | (hi & jnp.int32(-65536))


def _unpack_lo(w):
    return lax.bitcast_convert_type(lax.shift_left(w, 16), F32)


def _unpack_hi(w):
    return lax.bitcast_convert_type(w & jnp.int32(-65536), F32)


def _ln_kernel(x_ref, g_ref, b_ref, o_ref):
    o_ref[...] = _ln(x_ref[...], g_ref[...], b_ref[...])


def _layer_norm(x, g, b):
    T, D = x.shape
    row = pl.BlockSpec((ROW_TILE, D), lambda i: (i, 0))
    vec = pl.BlockSpec((1, D), lambda i: (0, 0))
    return pl.pallas_call(
        _ln_kernel, grid=(T // ROW_TILE,), in_specs=[row, vec, vec], out_specs=row,
        out_shape=jax.ShapeDtypeStruct((T, D), F32), compiler_params=_cparams("parallel"),
        name="ln_in")(x, g.reshape(1, D), b.reshape(1, D))


def _even_in_kernel(x_ref, w_ref, qm_ref, km_ref, qg_ref, kg_ref, cos_ref, sin_ref,
                    a_ref, q_ref, kt_ref, v_ref):
    tm = x_ref.shape[0]
    h = _dot(x_ref[...].astype(BF16), w_ref[...])
    a_ref[...] = h[:, :F_WIDTH].astype(BF16)
    q = h[:, F_WIDTH:F_WIDTH + Q_WIDTH]
    k = h[:, F_WIDTH + Q_WIDTH:F_WIDTH + Q_WIDTH + KV_WIDTH]
    v = h[:, F_WIDTH + Q_WIDTH + KV_WIDTH:]
    cos = cos_ref[...]
    sin = sin_ref[...]
    lane = lax.broadcasted_iota(I32, (tm, 128), 1)
    first_of_pair = (lane & ROPE_PAIRS) == 0

    def mean_sq(xf, m_ref):
        sq = xf * xf
        hi = sq.astype(BF16)
        lo = (sq - hi.astype(F32)).astype(BF16)
        return _dot(hi, m_ref[...]) + _dot(lo, m_ref[...])

    def rope(xn):
        sw = jnp.where(first_of_pair, pltpu.roll(xn, 128 - ROPE_PAIRS, 1), pltpu.roll(xn, ROPE_PAIRS, 1))
        return xn * cos + sw * sin

    qn = q * lax.rsqrt(mean_sq(q, qm_ref) + QK_EPS) * qg_ref[...]
    scale = math.log2(math.e) / math.sqrt(HEAD_DIM)
    for c in range(Q_WIDTH // 128):
        q_ref[:, c * 128:(c + 1) * 128] = (rope(qn[:, c * 128:(c + 1) * 128]) * scale).astype(BF16)
    kn = k * lax.rsqrt(mean_sq(k, km_ref) + QK_EPS) * kg_ref[...]
    kt_ref[...] = rope(kn).T.astype(BF16)
    ones_col = jnp.where(lane == HEAD_DIM, 1.0, 0.0)
    low = lane < HEAD_DIM
    v_ref[0] = jnp.where(low, v, ones_col).astype(BF16)
    v_ref[1] = jnp.where(low, pltpu.roll(v, HEAD_DIM, 1), ones_col).astype(BF16)


def _rope_tables(S):
    rows = S // GRID_W
    t = np.arange(S)
    inv = ROPE_THETA ** (-np.arange(ROPE_PAIRS, dtype=np.float64) / ROPE_PAIRS)
    ang_r = (t // GRID_W)[:, None] * inv
    ang_c = (t % GRID_W)[:, None] * inv
    del rows
    cos = np.concatenate([np.cos(ang_r), np.cos(ang_r), np.cos(ang_c), np.cos(ang_c)], axis=1)
    sin = np.concatenate([-np.sin(ang_r), np.sin(ang_r), -np.sin(ang_c), np.sin(ang_c)], axis=1)
    return (jnp.asarray(np.tile(cos, (1, 2)), F32), jnp.asarray(np.tile(sin, (1, 2)), F32))


def _head_mean_matrix(width):
    m = np.kron(np.eye(width // HEAD_DIM), np.full((HEAD_DIM, HEAD_DIM), 1.0 / HEAD_DIM))
    return jnp.asarray(m, BF16)


def _even_in(x, w_in, q_gain, k_gain, B, S):
    T, D = x.shape
    tm = ROW_TILE
    ns = S // tm
    cos, sin = _rope_tables(S)
    row = lambda w: pl.BlockSpec((tm, w), lambda i: (i, 0))
    full = lambda a: pl.BlockSpec(a.shape, lambda i: (0,) * a.ndim)
    tab = pl.BlockSpec((tm, 128), lambda i: (i % ns, 0))
    w = w_in.astype(BF16)
    qm = _head_mean_matrix(Q_WIDTH)
    km = _head_mean_matrix(KV_WIDTH)
    qg = jnp.tile(q_gain.astype(F32), N_HEADS).reshape(1, Q_WIDTH)
    kg = jnp.tile(k_gain.astype(F32), N_KV_HEADS).reshape(1, KV_WIDTH)
    return pl.pallas_call(
        _even_in_kernel, grid=(T // tm,),
        in_specs=[row(D), full(w), full(qm), full(km), full(qg), full(kg), tab, tab],
        out_specs=[row(F_WIDTH), row(Q_WIDTH),
                   pl.BlockSpec((None, KV_WIDTH, tm), lambda i: (i // ns, 0, i % ns)),
                   pl.BlockSpec((N_KV_HEADS, tm, 128), lambda i: (0, i, 0))],
        out_shape=[jax.ShapeDtypeStruct((T, F_WIDTH), BF16),
                   jax.ShapeDtypeStruct((T, Q_WIDTH), BF16),
                   jax.ShapeDtypeStruct((B, KV_WIDTH, S), BF16),
                   jax.ShapeDtypeStruct((N_KV_HEADS, T, 128), BF16)],
        compiler_params=_cparams("parallel"), name="even_in")(x, w, qm, km, qg, kg, cos, sin)


def _fourier_kernel(a_ref, dftc_ref, taba_ref, kc_ref, ks_ref, wf_ref, o_ref,
                    zr_ref, zi_ref, ur_ref, ui_ref, y_ref):
    S = a_ref.shape[0]
    n1_count = DFT_N1
    n2_count = S // DFT_N1
    blk = DFT_KRON * DFT_N1
    scale = 1.0 / math.sqrt(S * FGROUP_DIM)

    def channel_dft(j, carry):
        rows = pl.ds(pl.multiple_of(j * blk, blk), blk)
        zz = _dot(a_ref[rows, :], dftc_ref[...])
        zr_ref[rows, :] = zz[:, :FGROUP_DIM]
        zi_ref[rows, :] = zz[:, FGROUP_DIM:]
        return carry

    lax.fori_loop(0, S // blk, channel_dft, 0)

    def stage_a(n1, carry):
        rows = pl.ds(n1, n2_count, stride=n1_count)
        zn = jnp.concatenate([zr_ref[rows, :], zi_ref[rows, :]], axis=1).astype(BF16)
        r = _dot(taba_ref[n1], zn)
        ur_ref[rows, :] = r[:n2_count, :FGROUP_DIM] + r[n2_count:, FGROUP_DIM:]
        ui_ref[rows, :] = r[:n2_count, FGROUP_DIM:] - r[n2_count:, :FGROUP_DIM]
        return carry

    lax.fori_loop(0, n1_count, stage_a, 0)

    def stage_b(j, carry):
        rows = pl.ds(pl.multiple_of(j * blk, blk), blk)
        re = (_dot(kc_ref[...], ur_ref[rows, :].astype(BF16))
              + _dot(ks_ref[...], ui_ref[rows, :].astype(BF16)))
        out = _dot((re * scale).astype(BF16), wf_ref[...])
        for k2l in range(DFT_KRON):
            y_ref[pl.ds(j * DFT_KRON + k2l, n1_count, stride=n2_count), :] = (
                out[k2l * n1_count:(k2l + 1) * n1_count])
        return carry

    lax.fori_loop(0, S // blk, stage_b, 0)
    o_ref[...] = y_ref[...].astype(BF16)


def _dft_tables(S):
    n1c, n2c = DFT_N1, S // DFT_N1
    c = np.arange(FGROUP_DIM)
    ang = 2 * np.pi * np.outer(c, c) / FGROUP_DIM
    dftc = np.concatenate([np.cos(ang), -np.sin(ang)], axis=1)
    n1 = np.arange(n1c)[:, None, None]
    k2 = np.arange(n2c)[None, :, None]
    n2 = np.arange(n2c)[None, None, :]
    th = 2 * np.pi * (n2 * k2 / n2c + n1 * k2 / S)
    taba = np.concatenate([np.cos(th), np.sin(th)], axis=1)
    k1 = np.arange(n1c)
    g = 2 * np.pi * np.outer(k1, k1) / n1c
    eye = np.eye(DFT_KRON)
    kc = np.kron(eye, np.cos(g))
    ks = np.kron(eye, np.sin(g))
    return tuple(jnp.asarray(t, BF16) for t in (dftc, taba, kc, ks))


def _fourier(a, w_fourier, B, S):
    T = a.shape[0]
    dftc, taba, kc, ks = _dft_tables(S)
    full = lambda t: pl.BlockSpec(t.shape, lambda b, g: (0,) * t.ndim)
    blk = pl.BlockSpec((S, FGROUP_DIM), lambda b, g: (b, g))
    return pl.pallas_call(
        _fourier_kernel, grid=(B, N_FGROUPS),
        in_specs=[blk, full(dftc), full(taba), full(kc), full(ks),
                  pl.BlockSpec((None, FGROUP_DIM, FGROUP_DIM), lambda b, g: (g, 0, 0))],
        out_specs=blk,
        out_shape=jax.ShapeDtypeStruct((T, F_WIDTH), BF16),
        scratch_shapes=[pltpu.VMEM((S, FGROUP_DIM), F32) for _ in range(5)],
        compiler_params=_cparams("parallel", "parallel"), name="fourier")(
            a, dftc, taba, kc, ks, w_fourier.astype(BF16))


def _attn_kernel(q_ref, kt_ref, v_ref, o_ref, qs_ref, kmax_ref, acc_ref, m_ref, s0_ref, s1_ref):
    tq = q_ref.shape[0]
    n_keys = kt_ref.shape[1]
    tk = min(ATT_TK, n_keys)
    n_chunks = n_keys // tk
    assert n_chunks % 2 == 0 and n_chunks * tk == n_keys

    @pl.when(pl.program_id(2) == 0)
    def _():
        def body(c, best):
            k = kt_ref[:, pl.ds(pl.multiple_of(c * tk, tk), tk)].astype(F32)
            return jnp.maximum(best, jnp.sum(k * k, axis=0, keepdims=True))
        best = lax.fori_loop(0, n_chunks, body, jnp.zeros((1, tk), F32))
        kmax_ref[...] = jnp.broadcast_to(jnp.sqrt(jnp.max(best, axis=1, keepdims=True)), kmax_ref.shape)

    for g in range(Q_GROUP):
        qs_ref[g * tq:(g + 1) * tq, :] = q_ref[:, g * HEAD_DIM:(g + 1) * HEAD_DIM]
    qf = qs_ref[...].astype(F32)
    bound = jnp.sqrt(jnp.sum(qf * qf, axis=1, keepdims=True)) * kmax_ref[0:1, 0:1] * ATT_BOUND_SLACK

    def chunk(c):
        cols = pl.ds(pl.multiple_of(c * tk, tk), tk)
        return _dot(qs_ref[...], kt_ref[:, cols]), v_ref[cols, :]

    def scores(c):
        return _dot(qs_ref[...], kt_ref[:, pl.ds(pl.multiple_of(c * tk, tk), tk)])

    def weighted(s_buf, c):
        v = v_ref[pl.ds(pl.multiple_of(c * tk, tk), tk), :]
        acc_ref[...] += _dot(jnp.exp2(s_buf[...] - bound).astype(BF16), v)

    def fast(c2, carry):
        c = 2 * c2
        s1_ref[...] = scores(c + 1)
        weighted(s0_ref, c)
        s0_ref[...] = scores(jnp.minimum(c + 2, n_chunks - 1))
        weighted(s1_ref, c + 1)
        return carry

    acc_ref[...] = jnp.zeros(acc_ref.shape, F32)
    s0_ref[...] = scores(0)
    lax.fori_loop(0, n_chunks // 2, fast, 0)
    underflow = jnp.min(acc_ref[:, HEAD_DIM:HEAD_DIM + 1]) < ATT_MIN_ROW_SUM

    @pl.when(underflow)
    def _():
        def safe(c, carry):
            s, v = chunk(c)
            m_old = m_ref[...]
            m_new = jnp.maximum(m_old, jnp.max(s, axis=1, keepdims=True))
            acc_ref[...] = jnp.exp2(m_old - m_new) * acc_ref[...] + _dot(jnp.exp2(s - m_new).astype(BF16), v)
            m_ref[...] = m_new
            return carry

        m_ref[...] = jnp.full(m_ref.shape, NEG_INF, F32)
        acc_ref[...] = jnp.zeros(acc_ref.shape, F32)
        lax.fori_loop(0, n_chunks, safe, 0)

    acc = acc_ref[...]
    o = acc[:, :HEAD_DIM] / acc[:, HEAD_DIM:HEAD_DIM + 1]
    o_ref[...] = jnp.concatenate([o[g * tq:(g + 1) * tq] for g in range(Q_GROUP)], axis=1).astype(BF16)


def _attention(q, kt, v2, B, S):
    T = q.shape[0]
    tq = ATT_TQ
    nq = S // tq
    gw = Q_GROUP * HEAD_DIM
    rows = Q_GROUP * tq
    return pl.pallas_call(
        _attn_kernel, grid=(B, N_KV_HEADS, nq),
        in_specs=[pl.BlockSpec((tq, gw), lambda b, h, i: (b * nq + i, h)),
                  pl.BlockSpec((None, HEAD_DIM, S), lambda b, h, i: (b, h, 0)),
                  pl.BlockSpec((None, None, S, 128), lambda b, h, i: (h, b, 0, 0))],
        out_specs=pl.BlockSpec((tq, gw), lambda b, h, i: (b * nq + i, h)),
        out_shape=jax.ShapeDtypeStruct((T, Q_WIDTH), BF16),
        scratch_shapes=[pltpu.VMEM((rows, HEAD_DIM), BF16), pltpu.VMEM((8, 128), F32),
                        pltpu.VMEM((rows, 128), F32), pltpu.VMEM((rows, 1), F32),
                        pltpu.VMEM((rows, min(ATT_TK, S)), F32), pltpu.VMEM((rows, min(ATT_TK, S)), F32)],
        compiler_params=_cparams("parallel", "parallel", "arbitrary"),
        name="attention")(q, kt, v2.reshape(N_KV_HEADS, B, S, 128))


def _even_out_kernel(a_ref, t_ref, wa_ref, wt_ref, x_ref, g_ref, b_ref, o_ref):
    mix = _dot(a_ref[...], wa_ref[...]) + _dot(t_ref[...], wt_ref[...])
    o_ref[...] = _ln(DEEPNORM_ALPHA * x_ref[...] + mix, g_ref[...], b_ref[...])


def _even_out(a_out, attn, w_out, x, g, b):
    T, D = x.shape
    tm = ROW_TILE
    row = lambda w: pl.BlockSpec((tm, w), lambda i: (i, 0))
    full = lambda a: pl.BlockSpec(a.shape, lambda i: (0,) * a.ndim)
    wa = w_out[:F_WIDTH].astype(BF16)
    wt = w_out[F_WIDTH:].astype(BF16)
    g = g.reshape(1, D)
    b = b.reshape(1, D)
    return pl.pallas_call(
        _even_out_kernel, grid=(T // tm,),
        in_specs=[row(F_WIDTH), row(Q_WIDTH), full(wa), full(wt), row(D), full(g), full(b)],
        out_specs=row(D), out_shape=jax.ShapeDtypeStruct((T, D), F32),
        compiler_params=_cparams("parallel"), name="even_out")(a_out, attn, wa, wt, x, g, b)


def _odd_kernel(x_ref, wi_ref, bi_ref, vg_ref, vb_ref, ws_ref, bs_ref, wo_ref, g_ref, b_ref, o_ref,
                gate_ref):
    tm = x_ref.shape[0]
    x = x_ref[...]
    h = _dot(x.astype(BF16), wi_ref[...]) + bi_ref[...]
    h = 0.5 * h * (1.0 + lax.erf(h * (1.0 / math.sqrt(2.0))))
    u = h[:, :C_WIDTH]
    v = _ln(h[:, C_WIDTH:], vg_ref[...], vb_ref[...]).astype(BF16)
    for c in range(tm // CHUNK):
        r0 = c * CHUNK
        for gi in range(N_CGROUPS):
            l0 = gi * CGROUP_DIM
            sv = _dot(ws_ref[gi], v[r0:r0 + CHUNK, l0:l0 + CGROUP_DIM]) + bs_ref[gi]
            gate_ref[r0:r0 + CHUNK, l0:l0 + CGROUP_DIM] = (
                u[r0:r0 + CHUNK, l0:l0 + CGROUP_DIM] * sv).astype(BF16)
    mix = _dot(gate_ref[...], wo_ref[...])
    o_ref[...] = _ln(DEEPNORM_ALPHA * x + mix, g_ref[...], b_ref[...])


def _odd_mixer(x, w_in, b_in, v_g, v_b, w_s, b_s, w_out, g, b):
    T, D = x.shape
    tm = ROW_TILE
    row = pl.BlockSpec((tm, D), lambda i: (i, 0))
    full = lambda a: pl.BlockSpec(a.shape, lambda i: (0,) * a.ndim)
    args = [w_in.astype(BF16), b_in.reshape(1, 2 * C_WIDTH), v_g.reshape(1, C_WIDTH),
            v_b.reshape(1, C_WIDTH), w_s.astype(BF16),
            jnp.broadcast_to(b_s[:, :, None], (N_CGROUPS, CHUNK, CGROUP_DIM)).astype(F32),
            w_out.astype(BF16), g.reshape(1, D), b.reshape(1, D)]
    return pl.pallas_call(
        _odd_kernel, grid=(T // tm,),
        in_specs=[row] + [full(a) for a in args],
        out_specs=row, out_shape=jax.ShapeDtypeStruct((T, D), F32),
        scratch_shapes=[pltpu.VMEM((tm, C_WIDTH), BF16)],
        compiler_params=_cparams("parallel"), name="odd_mixer")(x, *args)


def _router_kernel(x_ref, w_ref, rb_ref, tri_ref, eidx_ref, wts_ref, pos_ref, cnt_ref, xp_ref, run_ref):
    tm = x_ref.shape[0]
    i = pl.program_id(0)

    @pl.when(i == 0)
    def _():
        run_ref[...] = jnp.zeros(run_ref.shape, F32)

    x = x_ref[...]
    xp_ref[...] = _pack_halves(x)
    xh = x.astype(BF16)
    xl = (x - xh.astype(F32)).astype(BF16)
    nt = (((1,), (1,)), ((), ()))
    dg = lambda a, c: lax.dot_general(a, c, nt, preferred_element_type=F32)
    logits = dg(w_ref[0], xh) + dg(w_ref[0], xl) + dg(w_ref[1], xh)
    scores = jax.nn.sigmoid(logits)
    sel = scores + rb_ref[...]

    i8 = lax.broadcasted_iota(I32, (GROUP_SIZE, tm), 0)
    gsc_rows = []
    for gidx in range(N_EXPERT_GROUPS):
        sg = sel[gidx * GROUP_SIZE:(gidx + 1) * GROUP_SIZE, :]
        m1 = jnp.max(sg, axis=0, keepdims=True)
        f1 = jnp.min(jnp.where(sg == m1, i8, GROUP_SIZE), axis=0, keepdims=True)
        m2 = jnp.max(jnp.where(i8 == f1, NEG_INF, sg), axis=0, keepdims=True)
        gsc_rows.append(m1 + m2)
    gsc = jnp.concatenate(gsc_rows, axis=0)

    gsel = jnp.zeros(gsc.shape, F32)
    for _ in range(TOPK_GROUPS):
        m = jnp.max(gsc, axis=0, keepdims=True)
        f = jnp.min(jnp.where(gsc == m, i8, N_EXPERT_GROUPS), axis=0, keepdims=True)
        pick = i8 == f
        gsel = jnp.where(pick, 1.0, gsel)
        gsc = jnp.where(pick, NEG_INF, gsc)
    esel = jnp.concatenate(
        [jnp.broadcast_to(gsel[gidx:gidx + 1, :], (GROUP_SIZE, tm)) for gidx in range(N_EXPERT_GROUPS)],
        axis=0)

    cur = jnp.where(esel > 0.0, sel, NEG_INF)
    ei = lax.broadcasted_iota(I32, cur.shape, 0)
    idx_rows, sc_rows = [], []
    chosen = jnp.zeros(cur.shape, F32)
    for _ in range(TOP_K):
        m = jnp.max(cur, axis=0, keepdims=True)
        f = jnp.min(jnp.where(cur == m, ei, N_EXPERTS), axis=0, keepdims=True)
        pick = ei == f
        idx_rows.append(f)
        sc_rows.append(jnp.sum(jnp.where(pick, scores, 0.0), axis=0, keepdims=True))
        chosen = jnp.where(pick, 1.0, chosen)
        cur = jnp.where(pick, NEG_INF, cur)
    eidx = jnp.concatenate(idx_rows, axis=0)
    sc = jnp.concatenate(sc_rows, axis=0)
    eidx_ref[...] = eidx
    wts_ref[...] = sc / jnp.sum(sc, axis=0, keepdims=True) * ROUTE_SCALE

    before = _dot(chosen.astype(BF16), tri_ref[...]) + run_ref[...]
    pos_rows = [jnp.sum(jnp.where(ei == idx_rows[k], before, 0.0), axis=0, keepdims=True)
                for k in range(TOP_K)]
    pos_ref[...] = jnp.concatenate(pos_rows, axis=0).astype(I32)
    run_new = run_ref[...] + jnp.sum(chosen, axis=1, keepdims=True)
    run_ref[...] = run_new
    cnt_ref[...] = jnp.broadcast_to(run_new, cnt_ref.shape).astype(I32)


def _router(x, router_w, router_b, first_row, T):
    D = x.shape[1]
    tm = min(ROUTER_TILE, T)
    n_tiles = T // tm
    first_tile = first_row // tm
    assert n_tiles * tm == T and first_tile * tm == first_row
    wt = router_w.T.astype(F32)
    wh = wt.astype(BF16)
    wl = (wt - wh.astype(F32)).astype(BF16)
    w2 = jnp.stack([wh, wl])
    rb = router_b.astype(F32).reshape(N_EXPERTS, 1)
    tri = jnp.asarray(np.triu(np.ones((tm, tm)), 1), BF16)
    full = lambda a: pl.BlockSpec(a.shape, lambda i: (0,) * a.ndim)
    col = pl.BlockSpec((TOP_K, tm), lambda i: (0, i))
    return pl.pallas_call(
        _router_kernel, grid=(n_tiles,),
        in_specs=[pl.BlockSpec((tm, D), lambda i: (i + first_tile, 0)), full(w2), full(rb), full(tri)],
        out_specs=[col, col, col, pl.BlockSpec((N_EXPERTS, 128), lambda i: (0, 0)),
                   pl.BlockSpec((tm, HALF), lambda i: (i, 0))],
        out_shape=[jax.ShapeDtypeStruct((TOP_K, T), I32), jax.ShapeDtypeStruct((TOP_K, T), F32),
                   jax.ShapeDtypeStruct((TOP_K, T), I32), jax.ShapeDtypeStruct((N_EXPERTS, 128), I32),
                   jax.ShapeDtypeStruct((T, HALF), I32)],
        scratch_shapes=[pltpu.VMEM((N_EXPERTS, 1), F32)],
        compiler_params=_cparams("arbitrary"), name="router")(x, w2, rb, tri)


def _dest_kernel(start_ref, eidx_ref, pos_ref, o_ref):
    e = eidx_ref[...]
    acc = pos_ref[...]
    for j in range(N_EXPERTS):
        acc = acc + jnp.where(e == j, start_ref[j], 0)
    o_ref[...] = acc


def _dest_rows(eidx, pos, seg_start):
    K, T = eidx.shape
    tl = min(T, 2048)
    blk = pl.BlockSpec((K, tl), lambda i, s: (0, i))
    grid_spec = pltpu.PrefetchScalarGridSpec(
        num_scalar_prefetch=1, grid=(T // tl,), in_specs=[blk, blk], out_specs=blk)
    return pl.pallas_call(
        _dest_kernel, grid_spec=grid_spec, out_shape=jax.ShapeDtypeStruct((K, T), I32),
        compiler_params=_cparams("parallel"), name="dest_rows")(seg_start, eidx, pos)


def _gather_rows(table, idx):
    n_rows = idx.shape[0]
    width = table.shape[1]
    per_worker = n_rows // SC_WORKERS
    n_chunks = per_worker // SC_CHUNK
    assert per_worker * SC_WORKERS == n_rows and n_chunks * SC_CHUNK == per_worker
    mesh = plsc.VectorSubcoreMesh(core_axis_name="c", subcore_axis_name="s")

    @functools.partial(
        pl.kernel, mesh=mesh,
        out_type=jax.ShapeDtypeStruct((n_rows, width), table.dtype),
        scratch_types=[pltpu.VMEM((SC_CHUNK,), I32), pltpu.VMEM((SC_CHUNK, width), table.dtype),
                       pltpu.SemaphoreType.DMA])
    def gather(table_hbm, idx_hbm, out_hbm, idx_v, rows_v, sem):
        wid = lax.axis_index("s") * 2 + lax.axis_index("c")
        base = wid * per_worker

        @pl.loop(0, n_chunks)
        def _(j):
            off = base + j * SC_CHUNK
            pltpu.sync_copy(idx_hbm.at[pl.ds(off, SC_CHUNK)], idx_v)
            pltpu.async_copy(table_hbm.at[idx_v], rows_v, sem).wait()
            pltpu.sync_copy(rows_v, out_hbm.at[pl.ds(off, SC_CHUNK)])

    return gather(table, idx)


def _scatter_rows(rows, dest):
    n_tok, width = rows.shape
    n_dst = dest.shape[0]
    per_worker = n_tok // SC_WORKERS
    n_chunks = per_worker // SC_CHUNK
    assert per_worker * SC_WORKERS == n_tok and n_chunks * SC_CHUNK == per_worker
    mesh = plsc.VectorSubcoreMesh(core_axis_name="c", subcore_axis_name="s")

    @functools.partial(
        pl.kernel, mesh=mesh,
        out_type=jax.ShapeDtypeStruct((n_dst * n_tok, width), rows.dtype),
        scratch_types=[pltpu.VMEM((n_dst, SC_CHUNK), I32), pltpu.VMEM((SC_CHUNK, width), rows.dtype),
                       pltpu.SemaphoreType.DMA])
    def scatter(rows_hbm, dest_hbm, out_hbm, idx_v, rows_v, sem):
        wid = lax.axis_index("s") * 2 + lax.axis_index("c")
        base = wid * per_worker

        @pl.loop(0, n_chunks)
        def _(j):
            off = base + j * SC_CHUNK
            pltpu.sync_copy(dest_hbm.at[:, pl.ds(off, SC_CHUNK)], idx_v)
            pltpu.sync_copy(rows_hbm.at[pl.ds(off, SC_CHUNK)], rows_v)
            copies = [pltpu.async_copy(rows_v, out_hbm.at[idx_v.at[k]], sem) for k in range(n_dst)]
            for c in copies:
                c.wait()

    return scatter(rows, dest)


def _expert_kernel(blk_ref, exp_ref, lo_ref, hi_ref, xs_ref, wg_ref, wu_ref, wd_ref, ys_ref,
                   wgu_s, wd_s):
    i = pl.program_id(0)
    prev = jnp.maximum(i - 1, 0)

    @pl.when(jnp.logical_or(i == 0, exp_ref[i] != exp_ref[prev]))
    def _():
        wgu_s[:, :EXPERT_DIM] = wg_ref[...].astype(BF16)
        wgu_s[:, EXPERT_DIM:] = wu_ref[...].astype(BF16)
        wd_s[...] = wd_ref[...].astype(BF16)

    first = jnp.logical_or(i == 0, blk_ref[i] != blk_ref[prev])
    for r0 in range(0, EXPERT_ROWS, EXPERT_SUB_ROWS):
        rows = slice(r0, r0 + EXPERT_SUB_ROWS)
        w = xs_ref[rows, :]
        xlo = lax.bitcast_convert_type(w.astype(jnp.int16), BF16)
        xhi = lax.bitcast_convert_type(lax.shift_right_logical(w, 16).astype(jnp.int16), BF16)
        gu = _dot(xlo, wgu_s[:HALF, :]) + _dot(xhi, wgu_s[HALF:, :])
        g = gu[:, :EXPERT_DIM]
        hb = (g * jax.nn.sigmoid(g) * gu[:, EXPERT_DIM:]).astype(BF16)
        y = _dot(hb, wd_s[...])
        packed = pltpu.pack_elementwise([y[:, :HALF], y[:, HALF:]], packed_dtype=BF16)
        row = r0 + lax.broadcasted_iota(I32, (EXPERT_SUB_ROWS, 1), 0)
        mine = jnp.logical_and(row >= lo_ref[i], row < hi_ref[i])
        kept = jnp.where(first, 0, ys_ref[rows, :])
        ys_ref[rows, :] = jnp.where(mine, packed, kept)


def _expert_items(counts, n_rows):
    n_blocks = n_rows // EXPERT_ROWS
    n_items = n_blocks + N_EXPERTS - 1
    end = jnp.cumsum(counts)
    start = end - counts
    first_blk = start // EXPERT_ROWS
    n_blk = jnp.where(counts > 0, (end - 1) // EXPERT_ROWS - first_blk + 1, 0)
    item_end = jnp.cumsum(n_blk)
    item_start = item_end - n_blk
    slot = jnp.arange(n_items, dtype=I32)
    e = jnp.minimum(jnp.sum((item_end[None, :] <= slot[:, None]).astype(I32), axis=1), N_EXPERTS - 1)
    onehot = (e[:, None] == jnp.arange(N_EXPERTS, dtype=I32)[None, :]).astype(I32)
    pick = lambda v: jnp.sum(onehot * v[None, :], axis=1)
    valid = slot < item_end[-1]
    blk = jnp.where(valid, pick(first_blk) + slot - pick(item_start), n_blocks - 1)
    lo = jnp.clip(pick(start) - blk * EXPERT_ROWS, 0, EXPERT_ROWS)
    hi = jnp.clip(pick(end) - blk * EXPERT_ROWS, 0, EXPERT_ROWS)
    last_e = jnp.max(jnp.where(counts > 0, jnp.arange(N_EXPERTS, dtype=I32), 0))
    e = jnp.where(valid, e, last_e)
    hi = jnp.where(valid, hi, 0)
    lo = jnp.where(valid, lo, 0)
    return blk.astype(I32), e.astype(I32), lo.astype(I32), hi.astype(I32)


def _experts(xs, items, w_gate, w_up, w_down, layer):
    n_rows = xs.shape[0]
    n_items = items[0].shape[0]
    wmap = lambda i, b, e, lo, hi: (layer, e[i], 0, 0)
    grid_spec = pltpu.PrefetchScalarGridSpec(
        num_scalar_prefetch=4, grid=(n_items,),
        in_specs=[pl.BlockSpec((EXPERT_ROWS, HALF), lambda i, b, e, lo, hi: (b[i], 0)),
                  pl.BlockSpec((None, None, D_MODEL, EXPERT_DIM), wmap),
                  pl.BlockSpec((None, None, D_MODEL, EXPERT_DIM), wmap),
                  pl.BlockSpec((None, None, EXPERT_DIM, D_MODEL), wmap)],
        out_specs=pl.BlockSpec((EXPERT_ROWS, HALF), lambda i, b, e, lo, hi: (b[i], 0)),
        scratch_shapes=[pltpu.VMEM((D_MODEL, 2 * EXPERT_DIM), BF16),
                        pltpu.VMEM((EXPERT_DIM, D_MODEL), BF16)])
    return pl.pallas_call(
        _expert_kernel, grid_spec=grid_spec,
        out_shape=jax.ShapeDtypeStruct((n_rows, HALF), I32),
        compiler_params=_cparams("arbitrary"), name="experts")(*items, xs, w_gate, w_up, w_down)


def _moe_out_kernel(x_ref, yg_ref, wt_ref, sgu_ref, sd_ref, g_ref, b_ref, *rest):
    o_ref = rest[-1]
    x = x_ref[...]
    wt = wt_ref[...]
    lo = jnp.zeros((x.shape[0], HALF), F32)
    hi = jnp.zeros((x.shape[0], HALF), F32)
    for k in range(TOP_K):
        w = yg_ref[k]
        wk = wt[:, k:k + 1]
        lo = lo + wk * _unpack_lo(w)
        hi = hi + wk * _unpack_hi(w)
    gu = _dot(x.astype(BF16), sgu_ref[...])
    g = gu[:, :EXPERT_DIM]
    hs = (g * jax.nn.sigmoid(g) * gu[:, EXPERT_DIM:]).astype(BF16)
    ffn = jnp.concatenate([lo, hi], axis=1) + _dot(hs, sd_ref[...])
    o_ref[...] = _ln(DEEPNORM_ALPHA * x + ffn, g_ref[...], b_ref[...])


def _moe_out(x, yg, wts, sh_gate, sh_up, sh_down, g, b, first_tile, partial_out):
    T, D = x.shape
    tm = ROW_TILE
    n_tiles = yg.shape[1] // tm
    xrow = pl.BlockSpec((tm, D), lambda i: (i + first_tile, 0))
    full = lambda a: pl.BlockSpec(a.shape, lambda i: (0,) * a.ndim)
    sgu = jnp.concatenate([sh_gate, sh_up], axis=1).astype(BF16)
    sd = sh_down.astype(BF16)
    g = g.reshape(1, D)
    b = b.reshape(1, D)
    args = [x, yg, wts, sgu, sd, g, b]
    in_specs = [xrow, pl.BlockSpec((TOP_K, tm, HALF), lambda i: (0, i, 0)),
                pl.BlockSpec((tm, TOP_K), lambda i: (i, 0)), full(sgu), full(sd), full(g), full(b)]
    aliases = {}
    if partial_out is not None:
        args.append(partial_out)
        in_specs.append(pl.BlockSpec(memory_space=pl.ANY))
        aliases = {len(args) - 1: 0}
    return pl.pallas_call(
        _moe_out_kernel, grid=(n_tiles,), in_specs=in_specs,
        out_specs=xrow, out_shape=jax.ShapeDtypeStruct((T, D), F32),
        input_output_aliases=aliases,
        compiler_params=_cparams("parallel"), name="moe_out")(*args)


def _moe(x, router_w, router_b, w_gate, w_up, w_down, layer, sh_gate, sh_up, sh_down, g, b):
    T = x.shape[0]
    tiles = T // ROW_TILE // MOE_TOKEN_GROUPS
    tg = tiles * ROW_TILE
    out = None
    for grp in range(MOE_TOKEN_GROUPS):
        eidx, wts, pos, cnt, xp = _router(x, router_w, router_b, grp * tg, tg)
        counts = cnt[:, 0]
        seg_start = (jnp.cumsum(counts) - counts).astype(I32)
        dest = _dest_rows(eidx, pos, seg_start)
        xs = _scatter_rows(xp, dest)
        ys = _experts(xs, _expert_items(counts, tg * TOP_K), w_gate, w_up, w_down, layer)
        yg = _gather_rows(ys, dest.reshape(tg * TOP_K)).reshape(TOP_K, tg, HALF)
        out = _moe_out(x, yg, wts.T, sh_gate, sh_up, sh_down, g, b, grp * tiles, out)
    return out


def kernel(x, ln_in_g, ln_in_b, e_w_in, e_w_fourier, e_q_gain, e_k_gain, e_w_out, o_w_in, o_b_in, o_v_ln_g, o_v_ln_b, o_w_spatial, o_b_spatial, o_w_out, ln_mix_g, ln_mix_b, ln_ffn_g, ln_ffn_b, router_w, router_b, exp_w_gate, exp_w_up, exp_w_down, sh_w_gate, sh_w_up, sh_w_down):
    B, S, D = x.shape
    T = B * S
    h = _layer_norm(x.reshape(T, D), ln_in_g, ln_in_b)
    for i in range(DEPTH):
        j = i // 2
        if i % 2 == 0:
            a, q, kt, v2 = _even_in(h, e_w_in[j], e_q_gain[j], e_k_gain[j], B, S)
            a_out = _fourier(a, e_w_fourier[j], B, S)
            attn = _attention(q, kt, v2, B, S)
            h = _even_out(a_out, attn, e_w_out[j], h, ln_mix_g[i], ln_mix_b[i])
        else:
            h = _odd_mixer(h, o_w_in[j], o_b_in[j], o_v_ln_g[j], o_v_ln_b[j], o_w_spatial[j],
                           o_b_spatial[j], o_w_out[j], ln_mix_g[i], ln_mix_b[i])
        h = _moe(h, router_w[i], router_b[i], exp_w_gate, exp_w_up, exp_w_down, i,
                 sh_w_gate[i], sh_w_up[i], sh_w_down[i], ln_ffn_g[i], ln_ffn_b[i])
    return h.reshape(B, S, D)
```

```python
import functools
import math

import numpy as np
import jax
import jax.numpy as jnp
from jax import lax
from jax.experimental import pallas as pl
from jax.experimental.pallas import tpu as pltpu
from jax.experimental.pallas import tpu_sc as plsc

F32 = jnp.float32
BF16 = jnp.bfloat16
I32 = jnp.int32

D_MODEL = 1024
DEPTH = 4
GRID_W = 64
N_FGROUPS = 4
FGROUP_DIM = 128
F_WIDTH = N_FGROUPS * FGROUP_DIM
N_HEADS = 8
N_KV_HEADS = 2
HEAD_DIM = 64
Q_GROUP = N_HEADS // N_KV_HEADS
Q_WIDTH = N_HEADS * HEAD_DIM
KV_WIDTH = N_KV_HEADS * HEAD_DIM
ROPE_THETA = 10000.0
ROPE_PAIRS = HEAD_DIM // 4
EVEN_IN_WIDTH = F_WIDTH + Q_WIDTH + 2 * KV_WIDTH
CHUNK = 128
N_CGROUPS = 8
CGROUP_DIM = D_MODEL // N_CGROUPS
C_WIDTH = N_CGROUPS * CGROUP_DIM
N_EXPERTS = 64
EXPERT_DIM = 256
TOP_K = 8
N_EXPERT_GROUPS = 8
GROUP_SIZE = N_EXPERTS // N_EXPERT_GROUPS
TOPK_GROUPS = 4
ROUTE_SCALE = 2.5
LN_EPS = 1e-5
QK_EPS = 1e-6
DEEPNORM_ALPHA = (2 * DEPTH) ** 0.25

VMEM_LIMIT_BYTES = 56 * 1024 * 1024
ROW_TILE = 512
DFT_N1 = 64
DFT_KRON = 4
DFT_PITCH_PAD = 8
ROUTER_TILE = 1024
EXPERT_ROWS = 1024
EXPERT_SUB_ROWS = 512
HALF = D_MODEL // 2
SC_WORKERS = 32
SC_CHUNK = 128
MOE_TOKEN_GROUPS = 2
ATT_TQ = 256
ATT_TK = 512
ATT_BOUND_SLACK = 1.0 + 2.0 ** -7
ATT_MIN_ROW_SUM = 2.0 ** -80
NEG_INF = float("-inf")


def _cparams(*sem):
    return pltpu.CompilerParams(dimension_semantics=sem, vmem_limit_bytes=VMEM_LIMIT_BYTES)


def _ln(x, g, b):
    mu = jnp.mean(x, axis=-1, keepdims=True)
    xc = x - mu
    var = jnp.mean(xc * xc, axis=-1, keepdims=True)
    return xc * lax.rsqrt(var + LN_EPS) * g + b


def _dot(a, b):
    return jnp.dot(a, b, preferred_element_type=F32)


def _pack_halves(y):
    lo = lax.bitcast_convert_type(y[:, :HALF].astype(BF16).astype(F32), I32)
    hi = lax.bitcast_convert_type(y[:, HALF:].astype(BF16).astype(F32), I32)
    return lax.shift_right_logical(lo, 16) | (hi & jnp.int32(-65536))


def _unpack_lo(w):
    return lax.bitcast_convert_type(lax.shift_left(w, 16), F32)


def _unpack_hi(w):
    return lax.bitcast_convert_type(w & jnp.int32(-65536), F32)


def _ln_kernel(x_ref, g_ref, b_ref, o_ref):
    o_ref[...] = _ln(x_ref[...], g_ref[...], b_ref[...])


def _layer_norm(x, g, b):
    T, D = x.shape
    row = pl.BlockSpec((ROW_TILE, D), lambda i: (i, 0))
    vec = pl.BlockSpec((1, D), lambda i: (0, 0))
    return pl.pallas_call(
        _ln_kernel, grid=(T // ROW_TILE,), in_specs=[row, vec, vec], out_specs=row,
        out_shape=jax.ShapeDtypeStruct((T, D), F32), compiler_params=_cparams("parallel"),
        name="ln_in")(x, g.reshape(1, D), b.reshape(1, D))


def _even_in_kernel(x_ref, w_ref, qm_ref, km_ref, qg_ref, kg_ref, cos_ref, sin_ref,
                    a_ref, q_ref, kt_ref, v_ref):
    tm = x_ref.shape[0]
    h = _dot(x_ref[...].astype(BF16), w_ref[...])
    a_ref[...] = h[:, :F_WIDTH].astype(BF16)
    q = h[:, F_WIDTH:F_WIDTH + Q_WIDTH]
    k = h[:, F_WIDTH + Q_WIDTH:F_WIDTH + Q_WIDTH + KV_WIDTH]
    v = h[:, F_WIDTH + Q_WIDTH + KV_WIDTH:]
    cos = cos_ref[...]
    sin = sin_ref[...]
    lane = lax.broadcasted_iota(I32, (tm, 128), 1)
    first_of_pair = (lane & ROPE_PAIRS) == 0

    def mean_sq(xf, m_ref):
        sq = xf * xf
        hi = sq.astype(BF16)
        lo = (sq - hi.astype(F32)).astype(BF16)
        return _dot(hi, m_ref[...]) + _dot(lo, m_ref[...])

    def rope(xn):
        sw = jnp.where(first_of_pair, pltpu.roll(xn, 128 - ROPE_PAIRS, 1), pltpu.roll(xn, ROPE_PAIRS, 1))
        return xn * cos + sw * sin

    qn = q * lax.rsqrt(mean_sq(q, qm_ref) + QK_EPS) * qg_ref[...]
    scale = math.log2(math.e) / math.sqrt(HEAD_DIM)
    for c in range(Q_WIDTH // 128):
        q_ref[:, c * 128:(c + 1) * 128] = (rope(qn[:, c * 128:(c + 1) * 128]) * scale).astype(BF16)
    kn = k * lax.rsqrt(mean_sq(k, km_ref) + QK_EPS) * kg_ref[...]
    kt_ref[...] = rope(kn).T.astype(BF16)
    ones_col = jnp.where(lane == HEAD_DIM, 1.0, 0.0)
    low = lane < HEAD_DIM
    v_ref[0] = jnp.where(low, v, ones_col).astype(BF16)
    v_ref[1] = jnp.where(low, pltpu.roll(v, HEAD_DIM, 1), ones_col).astype(BF16)


def _rope_tables(S):
    rows = S // GRID_W
    t = np.arange(S)
    inv = ROPE_THETA ** (-np.arange(ROPE_PAIRS, dtype=np.float64) / ROPE_PAIRS)
    ang_r = (t // GRID_W)[:, None] * inv
    ang_c = (t % GRID_W)[:, None] * inv
    del rows
    cos = np.concatenate([np.cos(ang_r), np.cos(ang_r), np.cos(ang_c), np.cos(ang_c)], axis=1)
    sin = np.concatenate([-np.sin(ang_r), np.sin(ang_r), -np.sin(ang_c), np.sin(ang_c)], axis=1)
    return (jnp.asarray(np.tile(cos, (1, 2)), F32), jnp.asarray(np.tile(sin, (1, 2)), F32))


def _head_mean_matrix(width):
    m = np.kron(np.eye(width // HEAD_DIM), np.full((HEAD_DIM, HEAD_DIM), 1.0 / HEAD_DIM))
    return jnp.asarray(m, BF16)


def _even_in(x, w_in, q_gain, k_gain, B, S):
    T, D = x.shape
    tm = ROW_TILE
    ns = S // tm
    cos, sin = _rope_tables(S)
    row = lambda w: pl.BlockSpec((tm, w), lambda i: (i, 0))
    full = lambda a: pl.BlockSpec(a.shape, lambda i: (0,) * a.ndim)
    tab = pl.BlockSpec((tm, 128), lambda i: (i % ns, 0))
    w = w_in.astype(BF16)
    qm = _head_mean_matrix(Q_WIDTH)
    km = _head_mean_matrix(KV_WIDTH)
    qg = jnp.tile(q_gain.astype(F32), N_HEADS).reshape(1, Q_WIDTH)
    kg = jnp.tile(k_gain.astype(F32), N_KV_HEADS).reshape(1, KV_WIDTH)
    return pl.pallas_call(
        _even_in_kernel, grid=(T // tm,),
        in_specs=[row(D), full(w), full(qm), full(km), full(qg), full(kg), tab, tab],
        out_specs=[row(F_WIDTH), row(Q_WIDTH),
                   pl.BlockSpec((None, KV_WIDTH, tm), lambda i: (i // ns, 0, i % ns)),
                   pl.BlockSpec((N_KV_HEADS, tm, 128), lambda i: (0, i, 0))],
        out_shape=[jax.ShapeDtypeStruct((T, F_WIDTH), BF16),
                   jax.ShapeDtypeStruct((T, Q_WIDTH), BF16),
                   jax.ShapeDtypeStruct((B, KV_WIDTH, S), BF16),
                   jax.ShapeDtypeStruct((N_KV_HEADS, T, 128), BF16)],
        compiler_params=_cparams("parallel"), name="even_in")(x, w, qm, km, qg, kg, cos, sin)


def _fourier_kernel(a_ref, dftc_ref, taba_ref, kc_ref, ks_ref, wf_ref, o_ref,
                    zr_ref, zi_ref, ur_ref, ui_ref, y_ref):
    S = a_ref.shape[0]
    n1_count = DFT_N1
    n2_count = S // DFT_N1
    pz = n1_count + DFT_PITCH_PAD
    pu = n2_count + DFT_PITCH_PAD
    blk = DFT_KRON * DFT_N1
    scale = 1.0 / math.sqrt(S * FGROUP_DIM)

    def channel_dft(j, carry):
        zz = _dot(a_ref[pl.ds(pl.multiple_of(j * blk, blk), blk), :], dftc_ref[...])
        for q in range(DFT_KRON):
            dst = pl.ds(pl.multiple_of((j * DFT_KRON + q) * pz, 8), n1_count)
            zr_ref[dst, :] = zz[q * n1_count:(q + 1) * n1_count, :FGROUP_DIM]
            zi_ref[dst, :] = zz[q * n1_count:(q + 1) * n1_count, FGROUP_DIM:]
        return carry

    lax.fori_loop(0, S // blk, channel_dft, 0, unroll=2)

    def stage_a(n1, carry):
        src = pl.ds(n1, n2_count, stride=pz)
        zn = jnp.concatenate([zr_ref[src, :], zi_ref[src, :]], axis=1).astype(BF16)
        r = _dot(taba_ref[n1], zn)
        dst = pl.ds(pl.multiple_of(n1 * pu, 8), n2_count)
        ur_ref[dst, :] = r[:n2_count, :FGROUP_DIM] + r[n2_count:, FGROUP_DIM:]
        ui_ref[dst, :] = r[:n2_count, FGROUP_DIM:] - r[n2_count:, :FGROUP_DIM]
        return carry

    lax.fori_loop(0, n1_count, stage_a, 0, unroll=4)

    def stage_b(j, carry):
        srcs = [pl.ds(j * DFT_KRON + q, n1_count, stride=pu) for q in range(DFT_KRON)]
        ur = jnp.concatenate([ur_ref[s, :] for s in srcs], axis=0).astype(BF16)
        ui = jnp.concatenate([ui_ref[s, :] for s in srcs], axis=0).astype(BF16)
        re = _dot(kc_ref[...], ur) + _dot(ks_ref[...], ui)
        out = _dot((re * scale).astype(BF16), wf_ref[...])
        for q in range(DFT_KRON):
            y_ref[srcs[q], :] = out[q * n1_count:(q + 1) * n1_count]
        return carry

    lax.fori_loop(0, S // blk, stage_b, 0, unroll=4)

    def compact(k1, carry):
        o_ref[pl.ds(pl.multiple_of(k1 * n2_count, n2_count), n2_count), :] = (
            y_ref[pl.ds(pl.multiple_of(k1 * pu, 8), n2_count), :].astype(BF16))
        return carry

    lax.fori_loop(0, n1_count, compact, 0)


def _dft_tables(S):
    n1c, n2c = DFT_N1, S // DFT_N1
    c = np.arange(FGROUP_DIM)
    ang = 2 * np.pi * np.outer(c, c) / FGROUP_DIM
    dftc = np.concatenate([np.cos(ang), -np.sin(ang)], axis=1)
    n1 = np.arange(n1c)[:, None, None]
    k2 = np.arange(n2c)[None, :, None]
    n2 = np.arange(n2c)[None, None, :]
    th = 2 * np.pi * (n2 * k2 / n2c + n1 * k2 / S)
    taba = np.concatenate([np.cos(th), np.sin(th)], axis=1)
    k1 = np.arange(n1c)
    g = 2 * np.pi * np.outer(k1, k1) / n1c
    eye = np.eye(DFT_KRON)
    kc = np.kron(eye, np.cos(g))
    ks = np.kron(eye, np.sin(g))
    return tuple(jnp.asarray(t, BF16) for t in (dftc, taba, kc, ks))


def _fourier(a, w_fourier, B, S):
    T = a.shape[0]
    dftc, taba, kc, ks = _dft_tables(S)
    full = lambda t: pl.BlockSpec(t.shape, lambda b, g: (0,) * t.ndim)
    blk = pl.BlockSpec((S, FGROUP_DIM), lambda b, g: (b, g))
    return pl.pallas_call(
        _fourier_kernel, grid=(B, N_FGROUPS),
        in_specs=[blk, full(dftc), full(taba), full(kc), full(ks),
                  pl.BlockSpec((None, FGROUP_DIM, FGROUP_DIM), lambda b, g: (g, 0, 0))],
        out_specs=blk,
        out_shape=jax.ShapeDtypeStruct((T, F_WIDTH), BF16),
        scratch_shapes=(
            [pltpu.VMEM((S // DFT_N1 * (DFT_N1 + DFT_PITCH_PAD), FGROUP_DIM), F32)] * 2
            + [pltpu.VMEM((DFT_N1 * (S // DFT_N1 + DFT_PITCH_PAD), FGROUP_DIM), F32)] * 3),
        compiler_params=_cparams("parallel", "parallel"), name="fourier")(
            a, dftc, taba, kc, ks, w_fourier.astype(BF16))


def _attn_kernel(q_ref, kt_ref, v_ref, o_ref, qs_ref, kmax_ref, acc_ref, m_ref, s0_ref, s1_ref):
    tq = q_ref.shape[0]
    n_keys = kt_ref.shape[1]
    tk = min(ATT_TK, n_keys)
    n_chunks = n_keys // tk
    assert n_chunks % 2 == 0 and n_chunks * tk == n_keys

    @pl.when(pl.program_id(2) == 0)
    def _():
        def body(c, best):
            k = kt_ref[:, pl.ds(pl.multiple_of(c * tk, tk), tk)].astype(F32)
            return jnp.maximum(best, jnp.sum(k * k, axis=0, keepdims=True))
        best = lax.fori_loop(0, n_chunks, body, jnp.zeros((1, tk), F32))
        kmax_ref[...] = jnp.broadcast_to(jnp.sqrt(jnp.max(best, axis=1, keepdims=True)), kmax_ref.shape)

    for g in range(Q_GROUP):
        qs_ref[g * tq:(g + 1) * tq, :] = q_ref[:, g * HEAD_DIM:(g + 1) * HEAD_DIM]
    qf = qs_ref[...].astype(F32)
    bound = jnp.sqrt(jnp.sum(qf * qf, axis=1, keepdims=True)) * kmax_ref[0:1, 0:1] * ATT_BOUND_SLACK

    def chunk(c):
        cols = pl.ds(pl.multiple_of(c * tk, tk), tk)
        return _dot(qs_ref[...], kt_ref[:, cols]), v_ref[cols, :]

    def scores(c):
        return _dot(qs_ref[...], kt_ref[:, pl.ds(pl.multiple_of(c * tk, tk), tk)])

    def weighted(s_buf, c):
        v = v_ref[pl.ds(pl.multiple_of(c * tk, tk), tk), :]
        acc_ref[...] += _dot(jnp.exp2(s_buf[...] - bound).astype(BF16), v)

    def fast(c2, carry):
        c = 2 * c2
        s1_ref[...] = scores(c + 1)
        weighted(s0_ref, c)
        s0_ref[...] = scores(jnp.minimum(c + 2, n_chunks - 1))
        weighted(s1_ref, c + 1)
        return carry

    acc_ref[...] = jnp.zeros(acc_ref.shape, F32)
    s0_ref[...] = scores(0)
    lax.fori_loop(0, n_chunks // 2, fast, 0)
    underflow = jnp.min(acc_ref[:, HEAD_DIM:HEAD_DIM + 1]) < ATT_MIN_ROW_SUM

    @pl.when(underflow)
    def _():
        def safe(c, carry):
            s, v = chunk(c)
            m_old = m_ref[...]
            m_new = jnp.maximum(m_old, jnp.max(s, axis=1, keepdims=True))
            acc_ref[...] = jnp.exp2(m_old - m_new) * acc_ref[...] + _dot(jnp.exp2(s - m_new).astype(BF16), v)
            m_ref[...] = m_new
            return carry

        m_ref[...] = jnp.full(m_ref.shape, NEG_INF, F32)
        acc_ref[...] = jnp.zeros(acc_ref.shape, F32)
        lax.fori_loop(0, n_chunks, safe, 0)

    acc = acc_ref[...]
    o = acc[:, :HEAD_DIM] / acc[:, HEAD_DIM:HEAD_DIM + 1]
    o_ref[...] = jnp.concatenate([o[g * tq:(g + 1) * tq] for g in range(Q_GROUP)], axis=1).astype(BF16)


def _attention(q, kt, v2, B, S):
    T = q.shape[0]
    tq = ATT_TQ
    nq = S // tq
    gw = Q_GROUP * HEAD_DIM
    rows = Q_GROUP * tq
    return pl.pallas_call(
        _attn_kernel, grid=(B, N_KV_HEADS, nq),
        in_specs=[pl.BlockSpec((tq, gw), lambda b, h, i: (b * nq + i, h)),
                  pl.BlockSpec((None, HEAD_DIM, S), lambda b, h, i: (b, h, 0)),
                  pl.BlockSpec((None, None, S, 128), lambda b, h, i: (h, b, 0, 0))],
        out_specs=pl.BlockSpec((tq, gw), lambda b, h, i: (b * nq + i, h)),
        out_shape=jax.ShapeDtypeStruct((T, Q_WIDTH), BF16),
        scratch_shapes=[pltpu.VMEM((rows, HEAD_DIM), BF16), pltpu.VMEM((8, 128), F32),
                        pltpu.VMEM((rows, 128), F32), pltpu.VMEM((rows, 1), F32),
                        pltpu.VMEM((rows, min(ATT_TK, S)), F32), pltpu.VMEM((rows, min(ATT_TK, S)), F32)],
        compiler_params=_cparams("parallel", "parallel", "arbitrary"),
        name="attention")(q, kt, v2.reshape(N_KV_HEADS, B, S, 128))


def _even_out_kernel(a_ref, t_ref, wa_ref, wt_ref, x_ref, g_ref, b_ref, o_ref):
    mix = _dot(a_ref[...], wa_ref[...]) + _dot(t_ref[...], wt_ref[...])
    o_ref[...] = _ln(DEEPNORM_ALPHA * x_ref[...] + mix, g_ref[...], b_ref[...])


def _even_out(a_out, attn, w_out, x, g, b):
    T, D = x.shape
    tm = ROW_TILE
    row = lambda w: pl.BlockSpec((tm, w), lambda i: (i, 0))
    full = lambda a: pl.BlockSpec(a.shape, lambda i: (0,) * a.ndim)
    wa = w_out[:F_WIDTH].astype(BF16)
    wt = w_out[F_WIDTH:].astype(BF16)
    g = g.reshape(1, D)
    b = b.reshape(1, D)
    return pl.pallas_call(
        _even_out_kernel, grid=(T // tm,),
        in_specs=[row(F_WIDTH), row(Q_WIDTH), full(wa), full(wt), row(D), full(g), full(b)],
        out_specs=row(D), out_shape=jax.ShapeDtypeStruct((T, D), F32),
        compiler_params=_cparams("parallel"), name="even_out")(a_out, attn, wa, wt, x, g, b)


def _odd_kernel(x_ref, wi_ref, bi_ref, vg_ref, vb_ref, ws_ref, bs_ref, wo_ref, g_ref, b_ref, o_ref,
                gate_ref):
    tm = x_ref.shape[0]
    x = x_ref[...]
    h = _dot(x.astype(BF16), wi_ref[...]) + bi_ref[...]
    h = 0.5 * h * (1.0 + lax.erf(h * (1.0 / math.sqrt(2.0))))
    u = h[:, :C_WIDTH]
    v = _ln(h[:, C_WIDTH:], vg_ref[...], vb_ref[...]).astype(BF16)
    for c in range(tm // CHUNK):
        r0 = c * CHUNK
        for gi in range(N_CGROUPS):
            l0 = gi * CGROUP_DIM
            sv = _dot(ws_ref[gi], v[r0:r0 + CHUNK, l0:l0 + CGROUP_DIM]) + bs_ref[gi]
            gate_ref[r0:r0 + CHUNK, l0:l0 + CGROUP_DIM] = (
                u[r0:r0 + CHUNK, l0:l0 + CGROUP_DIM] * sv).astype(BF16)
    mix = _dot(gate_ref[...], wo_ref[...])
    o_ref[...] = _ln(DEEPNORM_ALPHA * x + mix, g_ref[...], b_ref[...])


def _odd_mixer(x, w_in, b_in, v_g, v_b, w_s, b_s, w_out, g, b):
    T, D = x.shape
    tm = ROW_TILE
    row = pl.BlockSpec((tm, D), lambda i: (i, 0))
    full = lambda a: pl.BlockSpec(a.shape, lambda i: (0,) * a.ndim)
    args = [w_in.astype(BF16), b_in.reshape(1, 2 * C_WIDTH), v_g.reshape(1, C_WIDTH),
            v_b.reshape(1, C_WIDTH), w_s.astype(BF16),
            jnp.broadcast_to(b_s[:, :, None], (N_CGROUPS, CHUNK, CGROUP_DIM)).astype(F32),
            w_out.astype(BF16), g.reshape(1, D), b.reshape(1, D)]
    return pl.pallas_call(
        _odd_kernel, grid=(T // tm,),
        in_specs=[row] + [full(a) for a in args],
        out_specs=row, out_shape=jax.ShapeDtypeStruct((T, D), F32),
        scratch_shapes=[pltpu.VMEM((tm, C_WIDTH), BF16)],
        compiler_params=_cparams("parallel"), name="odd_mixer")(x, *args)


def _router_kernel(x_ref, w_ref, rb_ref, tri_ref, eidx_ref, wts_ref, pos_ref, cnt_ref, xp_ref, run_ref):
    tm = x_ref.shape[0]
    i = pl.program_id(0)

    @pl.when(i == 0)
    def _():
        run_ref[...] = jnp.zeros(run_ref.shape, F32)

    x = x_ref[...]
    xp_ref[...] = _pack_halves(x)
    xh = x.astype(BF16)
    xl = (x - xh.astype(F32)).astype(BF16)
    nt = (((1,), (1,)), ((), ()))
    dg = lambda a, c: lax.dot_general(a, c, nt, preferred_element_type=F32)
    logits = dg(w_ref[0], xh) + dg(w_ref[0], xl) + dg(w_ref[1], xh)
    scores = jax.nn.sigmoid(logits)
    sel = scores + rb_ref[...]

    i8 = lax.broadcasted_iota(I32, (GROUP_SIZE, tm), 0)
    gsc_rows = []
    for gidx in range(N_EXPERT_GROUPS):
        sg = sel[gidx * GROUP_SIZE:(gidx + 1) * GROUP_SIZE, :]
        m1 = jnp.max(sg, axis=0, keepdims=True)
        f1 = jnp.min(jnp.where(sg == m1, i8, GROUP_SIZE), axis=0, keepdims=True)
        m2 = jnp.max(jnp.where(i8 == f1, NEG_INF, sg), axis=0, keepdims=True)
        gsc_rows.append(m1 + m2)
    gsc = jnp.concatenate(gsc_rows, axis=0)

    gsel = jnp.zeros(gsc.shape, F32)
    for _ in range(TOPK_GROUPS):
        m = jnp.max(gsc, axis=0, keepdims=True)
        f = jnp.min(jnp.where(gsc == m, i8, N_EXPERT_GROUPS), axis=0, keepdims=True)
        pick = i8 == f
        gsel = jnp.where(pick, 1.0, gsel)
        gsc = jnp.where(pick, NEG_INF, gsc)
    esel = jnp.concatenate(
        [jnp.broadcast_to(gsel[gidx:gidx + 1, :], (GROUP_SIZE, tm)) for gidx in range(N_EXPERT_GROUPS)],
        axis=0)

    cur = jnp.where(esel > 0.0, sel, NEG_INF)
    ei = lax.broadcasted_iota(I32, cur.shape, 0)
    idx_rows, sc_rows = [], []
    chosen = jnp.zeros(cur.shape, F32)
    for _ in range(TOP_K):
        m = jnp.max(cur, axis=0, keepdims=True)
        f = jnp.min(jnp.where(cur == m, ei, N_EXPERTS), axis=0, keepdims=True)
        pick = ei == f
        idx_rows.append(f)
        sc_rows.append(jnp.sum(jnp.where(pick, scores, 0.0), axis=0, keepdims=True))
        chosen = jnp.where(pick, 1.0, chosen)
        cur = jnp.where(pick, NEG_INF, cur)
    eidx = jnp.concatenate(idx_rows, axis=0)
    sc = jnp.concatenate(sc_rows, axis=0)
    eidx_ref[...] = eidx
    wts_ref[...] = sc / jnp.sum(sc, axis=0, keepdims=True) * ROUTE_SCALE

    before = _dot(chosen.astype(BF16), tri_ref[...]) + run_ref[...]
    pos_rows = [jnp.sum(jnp.where(ei == idx_rows[k], before, 0.0), axis=0, keepdims=True)
                for k in range(TOP_K)]
    pos_ref[...] = jnp.concatenate(pos_rows, axis=0).astype(I32)
    run_new = run_ref[...] + jnp.sum(chosen, axis=1, keepdims=True)
    run_ref[...] = run_new
    cnt_ref[...] = jnp.broadcast_to(run_new, cnt_ref.shape).astype(I32)


def _router(x, router_w, router_b, first_row, T):
    D = x.shape[1]
    tm = min(ROUTER_TILE, T)
    n_tiles = T // tm
    first_tile = first_row // tm
    assert n_tiles * tm == T and first_tile * tm == first_row
    wt = router_w.T.astype(F32)
    wh = wt.astype(BF16)
    wl = (wt - wh.astype(F32)).astype(BF16)
    w2 = jnp.stack([wh, wl])
    rb = router_b.astype(F32).reshape(N_EXPERTS, 1)
    tri = jnp.asarray(np.triu(np.ones((tm, tm)), 1), BF16)
    full = lambda a: pl.BlockSpec(a.shape, lambda i: (0,) * a.ndim)
    col = pl.BlockSpec((TOP_K, tm), lambda i: (0, i))
    return pl.pallas_call(
        _router_kernel, grid=(n_tiles,),
        in_specs=[pl.BlockSpec((tm, D), lambda i: (i + first_tile, 0)), full(w2), full(rb), full(tri)],
        out_specs=[col, col, col, pl.BlockSpec((N_EXPERTS, 128), lambda i: (0, 0)),
                   pl.BlockSpec((tm, HALF), lambda i: (i, 0))],
        out_shape=[jax.ShapeDtypeStruct((TOP_K, T), I32), jax.ShapeDtypeStruct((TOP_K, T), F32),
                   jax.ShapeDtypeStruct((TOP_K, T), I32), jax.ShapeDtypeStruct((N_EXPERTS, 128), I32),
                   jax.ShapeDtypeStruct((T, HALF), I32)],
        scratch_shapes=[pltpu.VMEM((N_EXPERTS, 1), F32)],
        compiler_params=_cparams("arbitrary"), name="router")(x, w2, rb, tri)


def _dest_kernel(start_ref, eidx_ref, pos_ref, o_ref):
    e = eidx_ref[...]
    acc = pos_ref[...]
    for j in range(N_EXPERTS):
        acc = acc + jnp.where(e == j, start_ref[j], 0)
    o_ref[...] = acc


def _dest_rows(eidx, pos, seg_start):
    K, T = eidx.shape
    tl = min(T, 2048)
    blk = pl.BlockSpec((K, tl), lambda i, s: (0, i))
    grid_spec = pltpu.PrefetchScalarGridSpec(
        num_scalar_prefetch=1, grid=(T // tl,), in_specs=[blk, blk], out_specs=blk)
    return pl.pallas_call(
        _dest_kernel, grid_spec=grid_spec, out_shape=jax.ShapeDtypeStruct((K, T), I32),
        compiler_params=_cparams("parallel"), name="dest_rows")(seg_start, eidx, pos)


def _gather_rows(table, idx):
    n_rows = idx.shape[0]
    width = table.shape[1]
    per_worker = n_rows // SC_WORKERS
    n_chunks = per_worker // SC_CHUNK
    assert per_worker * SC_WORKERS == n_rows and n_chunks * SC_CHUNK == per_worker
    mesh = plsc.VectorSubcoreMesh(core_axis_name="c", subcore_axis_name="s")

    @functools.partial(
        pl.kernel, mesh=mesh,
        out_type=jax.ShapeDtypeStruct((n_rows, width), table.dtype),
        scratch_types=[pltpu.VMEM((SC_CHUNK,), I32), pltpu.VMEM((SC_CHUNK, width), table.dtype),
                       pltpu.SemaphoreType.DMA])
    def gather(table_hbm, idx_hbm, out_hbm, idx_v, rows_v, sem):
        wid = lax.axis_index("s") * 2 + lax.axis_index("c")
        base = wid * per_worker

        @pl.loop(0, n_chunks)
        def _(j):
            off = base + j * SC_CHUNK
            pltpu.sync_copy(idx_hbm.at[pl.ds(off, SC_CHUNK)], idx_v)
            pltpu.async_copy(table_hbm.at[idx_v], rows_v, sem).wait()
            pltpu.sync_copy(rows_v, out_hbm.at[pl.ds(off, SC_CHUNK)])

    return gather(table, idx)


def _scatter_rows(rows, dest):
    n_tok, width = rows.shape
    n_dst = dest.shape[0]
    per_worker = n_tok // SC_WORKERS
    n_chunks = per_worker // SC_CHUNK
    assert per_worker * SC_WORKERS == n_tok and n_chunks * SC_CHUNK == per_worker
    mesh = plsc.VectorSubcoreMesh(core_axis_name="c", subcore_axis_name="s")

    @functools.partial(
        pl.kernel, mesh=mesh,
        out_type=jax.ShapeDtypeStruct((n_dst * n_tok, width), rows.dtype),
        scratch_types=[pltpu.VMEM((n_dst, SC_CHUNK), I32), pltpu.VMEM((SC_CHUNK, width), rows.dtype),
                       pltpu.SemaphoreType.DMA])
    def scatter(rows_hbm, dest_hbm, out_hbm, idx_v, rows_v, sem):
        wid = lax.axis_index("s") * 2 + lax.axis_index("c")
        base = wid * per_worker

        @pl.loop(0, n_chunks)
        def _(j):
            off = base + j * SC_CHUNK
            pltpu.sync_copy(dest_hbm.at[:, pl.ds(off, SC_CHUNK)], idx_v)
            pltpu.sync_copy(rows_hbm.at[pl.ds(off, SC_CHUNK)], rows_v)
            copies = [pltpu.async_copy(rows_v, out_hbm.at[idx_v.at[k]], sem) for k in range(n_dst)]
            for c in copies:
                c.wait()

    return scatter(rows, dest)


def _expert_kernel(blk_ref, exp_ref, lo_ref, hi_ref, xs_ref, wg_ref, wu_ref, wd_ref, ys_ref,
                   wgu_s, wd_s):
    i = pl.program_id(0)
    prev = jnp.maximum(i - 1, 0)

    @pl.when(jnp.logical_or(i == 0, exp_ref[i] != exp_ref[prev]))
    def _():
        wgu_s[:, :EXPERT_DIM] = wg_ref[...].astype(BF16)
        wgu_s[:, EXPERT_DIM:] = wu_ref[...].astype(BF16)
        wd_s[...] = wd_ref[...].astype(BF16)

    first = jnp.logical_or(i == 0, blk_ref[i] != blk_ref[prev])
    for r0 in range(0, EXPERT_ROWS, EXPERT_SUB_ROWS):
        rows = slice(r0, r0 + EXPERT_SUB_ROWS)
        w = xs_ref[rows, :]
        xlo = lax.bitcast_convert_type(w.astype(jnp.int16), BF16)
        xhi = lax.bitcast_convert_type(lax.shift_right_logical(w, 16).astype(jnp.int16), BF16)
        gu = _dot(xlo, wgu_s[:HALF, :]) + _dot(xhi, wgu_s[HALF:, :])
        g = gu[:, :EXPERT_DIM]
        hb = (g * jax.nn.sigmoid(g) * gu[:, EXPERT_DIM:]).astype(BF16)
        y = _dot(hb, wd_s[...])
        packed = pltpu.pack_elementwise([y[:, :HALF], y[:, HALF:]], packed_dtype=BF16)
        row = r0 + lax.broadcasted_iota(I32, (EXPERT_SUB_ROWS, 1), 0)
        mine = jnp.logical_and(row >= lo_ref[i], row < hi_ref[i])
        kept = jnp.where(first, 0, ys_ref[rows, :])
        ys_ref[rows, :] = jnp.where(mine, packed, kept)


def _expert_items(counts, n_rows):
    n_blocks = n_rows // EXPERT_ROWS
    n_items = n_blocks + N_EXPERTS - 1
    end = jnp.cumsum(counts)
    start = end - counts
    first_blk = start // EXPERT_ROWS
    n_blk = jnp.where(counts > 0, (end - 1) // EXPERT_ROWS - first_blk + 1, 0)
    item_end = jnp.cumsum(n_blk)
    item_start = item_end - n_blk
    slot = jnp.arange(n_items, dtype=I32)
    e = jnp.minimum(jnp.sum((item_end[None, :] <= slot[:, None]).astype(I32), axis=1), N_EXPERTS - 1)
    onehot = (e[:, None] == jnp.arange(N_EXPERTS, dtype=I32)[None, :]).astype(I32)
    pick = lambda v: jnp.sum(onehot * v[None, :], axis=1)
    valid = slot < item_end[-1]
    blk = jnp.where(valid, pick(first_blk) + slot - pick(item_start), n_blocks - 1)
    lo = jnp.clip(pick(start) - blk * EXPERT_ROWS, 0, EXPERT_ROWS)
    hi = jnp.clip(pick(end) - blk * EXPERT_ROWS, 0, EXPERT_ROWS)
    last_e = jnp.max(jnp.where(counts > 0, jnp.arange(N_EXPERTS, dtype=I32), 0))
    e = jnp.where(valid, e, last_e)
    hi = jnp.where(valid, hi, 0)
    lo = jnp.where(valid, lo, 0)
    return blk.astype(I32), e.astype(I32), lo.astype(I32), hi.astype(I32)


def _experts(xs, items, w_gate, w_up, w_down, layer):
    n_rows = xs.shape[0]
    n_items = items[0].shape[0]
    wmap = lambda i, b, e, lo, hi: (layer, e[i], 0, 0)
    grid_spec = pltpu.PrefetchScalarGridSpec(
        num_scalar_prefetch=4, grid=(n_items,),
        in_specs=[pl.BlockSpec((EXPERT_ROWS, HALF), lambda i, b, e, lo, hi: (b[i], 0)),
                  pl.BlockSpec((None, None, D_MODEL, EXPERT_DIM), wmap),
                  pl.BlockSpec((None, None, D_MODEL, EXPERT_DIM), wmap),
                  pl.BlockSpec((None, None, EXPERT_DIM, D_MODEL), wmap)],
        out_specs=pl.BlockSpec((EXPERT_ROWS, HALF), lambda i, b, e, lo, hi: (b[i], 0)),
        scratch_shapes=[pltpu.VMEM((D_MODEL, 2 * EXPERT_DIM), BF16),
                        pltpu.VMEM((EXPERT_DIM, D_MODEL), BF16)])
    return pl.pallas_call(
        _expert_kernel, grid_spec=grid_spec,
        out_shape=jax.ShapeDtypeStruct((n_rows, HALF), I32),
        compiler_params=_cparams("arbitrary"), name="experts")(*items, xs, w_gate, w_up, w_down)


def _moe_out_kernel(x_ref, yg_ref, wt_ref, sgu_ref, sd_ref, g_ref, b_ref, *rest):
    o_ref = rest[-1]
    x = x_ref[...]
    wt = wt_ref[...]
    lo = jnp.zeros((x.shape[0], HALF), F32)
    hi = jnp.zeros((x.shape[0], HALF), F32)
    for k in range(TOP_K):
        w = yg_ref[k]
        wk = wt[:, k:k + 1]
        lo = lo + wk * _unpack_lo(w)
        hi = hi + wk * _unpack_hi(w)
    gu = _dot(x.astype(BF16), sgu_ref[...])
    g = gu[:, :EXPERT_DIM]
    hs = (g * jax.nn.sigmoid(g) * gu[:, EXPERT_DIM:]).astype(BF16)
    ffn = jnp.concatenate([lo, hi], axis=1) + _dot(hs, sd_ref[...])
    o_ref[...] = _ln(DEEPNORM_ALPHA * x + ffn, g_ref[...], b_ref[...])


def _moe_out(x, yg, wts, sh_gate, sh_up, sh_down, g, b, first_tile, partial_out):
    T, D = x.shape
    tm = ROW_TILE
    n_tiles = yg.shape[1] // tm
    xrow = pl.BlockSpec((tm, D), lambda i: (i + first_tile, 0))
    full = lambda a: pl.BlockSpec(a.shape, lambda i: (0,) * a.ndim)
    sgu = jnp.concatenate([sh_gate, sh_up], axis=1).astype(BF16)
    sd = sh_down.astype(BF16)
    g = g.reshape(1, D)
    b = b.reshape(1, D)
    args = [x, yg, wts, sgu, sd, g, b]
    in_specs = [xrow, pl.BlockSpec((TOP_K, tm, HALF), lambda i: (0, i, 0)),
                pl.BlockSpec((tm, TOP_K), lambda i: (i, 0)), full(sgu), full(sd), full(g), full(b)]
    aliases = {}
    if partial_out is not None:
        args.append(partial_out)
        in_specs.append(pl.BlockSpec(memory_space=pl.ANY))
        aliases = {len(args) - 1: 0}
    return pl.pallas_call(
        _moe_out_kernel, grid=(n_tiles,), in_specs=in_specs,
        out_specs=xrow, out_shape=jax.ShapeDtypeStruct((T, D), F32),
        input_output_aliases=aliases,
        compiler_params=_cparams("parallel"), name="moe_out")(*args)


def _moe(x, router_w, router_b, w_gate, w_up, w_down, layer, sh_gate, sh_up, sh_down, g, b):
    T = x.shape[0]
    tiles = T // ROW_TILE // MOE_TOKEN_GROUPS
    tg = tiles * ROW_TILE
    out = None
    for grp in range(MOE_TOKEN_GROUPS):
        eidx, wts, pos, cnt, xp = _router(x, router_w, router_b, grp * tg, tg)
        counts = cnt[:, 0]
        seg_start = (jnp.cumsum(counts) - counts).astype(I32)
        dest = _dest_rows(eidx, pos, seg_start)
        xs = _scatter_rows(xp, dest)
        ys = _experts(xs, _expert_items(counts, tg * TOP_K), w_gate, w_up, w_down, layer)
        yg = _gather_rows(ys, dest.reshape(tg * TOP_K)).reshape(TOP_K, tg, HALF)
        out = _moe_out(x, yg, wts.T, sh_gate, sh_up, sh_down, g, b, grp * tiles, out)
    return out


def kernel(x, ln_in_g, ln_in_b, e_w_in, e_w_fourier, e_q_gain, e_k_gain, e_w_out, o_w_in, o_b_in, o_v_ln_g, o_v_ln_b, o_w_spatial, o_b_spatial, o_w_out, ln_mix_g, ln_mix_b, ln_ffn_g, ln_ffn_b, router_w, router_b, exp_w_gate, exp_w_up, exp_w_down, sh_w_gate, sh_w_up, sh_w_down):
    B, S, D = x.shape
    T = B * S
    h = _layer_norm(x.reshape(T, D), ln_in_g, ln_in_b)
    for i in range(DEPTH):
        j = i // 2
        if i % 2 == 0:
            a, q, kt, v2 = _even_in(h, e_w_in[j], e_q_gain[j], e_k_gain[j], B, S)
            a_out = _fourier(a, e_w_fourier[j], B, S)
            attn = _attention(q, kt, v2, B, S)
            h = _even_out(a_out, attn, e_w_out[j], h, ln_mix_g[i], ln_mix_b[i])
        else:
            h = _odd_mixer(h, o_w_in[j], o_b_in[j], o_v_ln_g[j], o_v_ln_b[j], o_w_spatial[j],
                           o_b_spatial[j], o_w_out[j], ln_mix_g[i], ln_mix_b[i])
        h = _moe(h, router_w[i], router_b[i], exp_w_gate, exp_w_up, exp_w_down, i,
                 sh_w_gate[i], sh_w_up[i], sh_w_down[i], ln_ffn_g[i], ln_ffn_b[i])
    return h.reshape(B, S, D)
```

```python
import functools
import math

import numpy as np
import jax
import jax.numpy as jnp
from jax import lax
from jax.experimental import pallas as pl
from jax.experimental.pallas import tpu as pltpu
from jax.experimental.pallas import tpu_sc as plsc

F32 = jnp.float32
BF16 = jnp.bfloat16
I32 = jnp.int32

D_MODEL = 1024
DEPTH = 4
GRID_W = 64
N_FGROUPS = 4
FGROUP_DIM = 128
F_WIDTH = N_FGROUPS * FGROUP_DIM
N_HEADS = 8
N_KV_HEADS = 2
HEAD_DIM = 64
Q_GROUP = N_HEADS // N_KV_HEADS
Q_WIDTH = N_HEADS * HEAD_DIM
KV_WIDTH = N_KV_HEADS * HEAD_DIM
ROPE_THETA = 10000.0
ROPE_PAIRS = HEAD_DIM // 4
EVEN_IN_WIDTH = F_WIDTH + Q_WIDTH + 2 * KV_WIDTH
CHUNK = 128
N_CGROUPS = 8
CGROUP_DIM = D_MODEL // N_CGROUPS
C_WIDTH = N_CGROUPS * CGROUP_DIM
N_EXPERTS = 64
EXPERT_DIM = 256
TOP_K = 8
N_EXPERT_GROUPS = 8
GROUP_SIZE = N_EXPERTS // N_EXPERT_GROUPS
TOPK_GROUPS = 4
ROUTE_SCALE = 2.5
LN_EPS = 1e-5
QK_EPS = 1e-6
DEEPNORM_ALPHA = (2 * DEPTH) ** 0.25

VMEM_LIMIT_BYTES = 56 * 1024 * 1024
ROW_TILE = 512
DFT_N1 = 64
DFT_KRON = 4
DFT_PITCH_PAD = 8
ROUTER_TILE = 1024
EXPERT_ROWS = 2048
EXPERT_SUB_ROWS = 512
HALF = D_MODEL // 2
SC_WORKERS = 32
SC_CHUNK = 128
MOE_TOKEN_GROUPS = 2
ATT_TQ = 256
ATT_TK = 512
ATT_BOUND_SLACK = 1.0 + 2.0 ** -7
ATT_MIN_ROW_SUM = 2.0 ** -80
NEG_INF = float("-inf")


def _cparams(*sem):
    return pltpu.CompilerParams(dimension_semantics=sem, vmem_limit_bytes=VMEM_LIMIT_BYTES)


def _ln(x, g, b):
    mu = jnp.mean(x, axis=-1, keepdims=True)
    xc = x - mu
    var = jnp.mean(xc * xc, axis=-1, keepdims=True)
    return xc * lax.rsqrt(var + LN_EPS) * g + b


def _dot(a, b):
    return jnp.dot(a, b, preferred_element_type=F32)


def _pack_halves(y):
    lo = lax.bitcast_convert_type(y[:, :HALF].astype(BF16).astype(F32), I32)
    hi = lax.bitcast_convert_type(y[:, HALF:].astype(BF16).astype(F32), I32)
    return lax.shift_right_logical(lo, 16) | (hi & jnp.int32(-65536))


def _unpack_lo(w):
    return lax.bitcast_convert_type(lax.shift_left(w, 16), F32)


def _unpack_hi(w):
    return lax.bitcast_convert_type(w & jnp.int32(-65536), F32)


def _ln_kernel(x_ref, g_ref, b_ref, o_ref):
    o_ref[...] = _ln(x_ref[...], g_ref[...], b_ref[...])


def _layer_norm(x, g, b):
    T, D = x.shape
    row = pl.BlockSpec((ROW_TILE, D), lambda i: (i, 0))
    vec = pl.BlockSpec((1, D), lambda i: (0, 0))
    return pl.pallas_call(
        _ln_kernel, grid=(T // ROW_TILE,), in_specs=[row, vec, vec], out_specs=row,
        out_shape=jax.ShapeDtypeStruct((T, D), F32), compiler_params=_cparams("parallel"),
        name="ln_in")(x, g.reshape(1, D), b.reshape(1, D))


def _even_in_kernel(x_ref, w_ref, qm_ref, km_ref, qg_ref, kg_ref, cos_ref, sin_ref,
                    a_ref, q_ref, kt_ref, v_ref):
    tm = x_ref.shape[0]
    h = _dot(x_ref[...].astype(BF16), w_ref[...])
    a_ref[...] = h[:, :F_WIDTH].astype(BF16)
    q = h[:, F_WIDTH:F_WIDTH + Q_WIDTH]
    k = h[:, F_WIDTH + Q_WIDTH:F_WIDTH + Q_WIDTH + KV_WIDTH]
    v = h[:, F_WIDTH + Q_WIDTH + KV_WIDTH:]
    cos = cos_ref[...]
    sin = sin_ref[...]
    lane = lax.broadcasted_iota(I32, (tm, 128), 1)
    first_of_pair = (lane & ROPE_PAIRS) == 0

    def mean_sq(xf, m_ref):
        sq = xf * xf
        hi = sq.astype(BF16)
        lo = (sq - hi.astype(F32)).astype(BF16)
        return _dot(hi, m_ref[...]) + _dot(lo, m_ref[...])

    def rope(xn):
        sw = jnp.where(first_of_pair, pltpu.roll(xn, 128 - ROPE_PAIRS, 1), pltpu.roll(xn, ROPE_PAIRS, 1))
        return xn * cos + sw * sin

    qn = q * lax.rsqrt(mean_sq(q, qm_ref) + QK_EPS) * qg_ref[...]
    scale = math.log2(math.e) / math.sqrt(HEAD_DIM)
    for c in range(Q_WIDTH // 128):
        q_ref[:, c * 128:(c + 1) * 128] = (rope(qn[:, c * 128:(c + 1) * 128]) * scale).astype(BF16)
    kn = k * lax.rsqrt(mean_sq(k, km_ref) + QK_EPS) * kg_ref[...]
    kt_ref[...] = rope(kn).T.astype(BF16)
    ones_col = jnp.where(lane == HEAD_DIM, 1.0, 0.0)
    low = lane < HEAD_DIM
    v_ref[0] = jnp.where(low, v, ones_col).astype(BF16)
    v_ref[1] = jnp.where(low, pltpu.roll(v, HEAD_DIM, 1), ones_col).astype(BF16)


def _rope_tables(S):
    rows = S // GRID_W
    t = np.arange(S)
    inv = ROPE_THETA ** (-np.arange(ROPE_PAIRS, dtype=np.float64) / ROPE_PAIRS)
    ang_r = (t // GRID_W)[:, None] * inv
    ang_c = (t % GRID_W)[:, None] * inv
    del rows
    cos = np.concatenate([np.cos(ang_r), np.cos(ang_r), np.cos(ang_c), np.cos(ang_c)], axis=1)
    sin = np.concatenate([-np.sin(ang_r), np.sin(ang_r), -np.sin(ang_c), np.sin(ang_c)], axis=1)
    return (jnp.asarray(np.tile(cos, (1, 2)), F32), jnp.asarray(np.tile(sin, (1, 2)), F32))


def _head_mean_matrix(width):
    m = np.kron(np.eye(width // HEAD_DIM), np.full((HEAD_DIM, HEAD_DIM), 1.0 / HEAD_DIM))
    return jnp.asarray(m, BF16)


def _even_in(x, w_in, q_gain, k_gain, B, S):
    T, D = x.shape
    tm = ROW_TILE
    ns = S // tm
    cos, sin = _rope_tables(S)
    row = lambda w: pl.BlockSpec((tm, w), lambda i: (i, 0))
    full = lambda a: pl.BlockSpec(a.shape, lambda i: (0,) * a.ndim)
    tab = pl.BlockSpec((tm, 128), lambda i: (i % ns, 0))
    w = w_in.astype(BF16)
    qm = _head_mean_matrix(Q_WIDTH)
    km = _head_mean_matrix(KV_WIDTH)
    qg = jnp.tile(q_gain.astype(F32), N_HEADS).reshape(1, Q_WIDTH)
    kg = jnp.tile(k_gain.astype(F32), N_KV_HEADS).reshape(1, KV_WIDTH)
    return pl.pallas_call(
        _even_in_kernel, grid=(T // tm,),
        in_specs=[row(D), full(w), full(qm), full(km), full(qg), full(kg), tab, tab],
        out_specs=[row(F_WIDTH), row(Q_WIDTH),
                   pl.BlockSpec((None, KV_WIDTH, tm), lambda i: (i // ns, 0, i % ns)),
                   pl.BlockSpec((N_KV_HEADS, tm, 128), lambda i: (0, i, 0))],
        out_shape=[jax.ShapeDtypeStruct((T, F_WIDTH), BF16),
                   jax.ShapeDtypeStruct((T, Q_WIDTH), BF16),
                   jax.ShapeDtypeStruct((B, KV_WIDTH, S), BF16),
                   jax.ShapeDtypeStruct((N_KV_HEADS, T, 128), BF16)],
        compiler_params=_cparams("parallel"), name="even_in")(x, w, qm, km, qg, kg, cos, sin)


def _fourier_kernel(a_ref, dftc_ref, taba_ref, kc_ref, ks_ref, wf_ref, o_ref,
                    zr_ref, zi_ref, ur_ref, ui_ref, y_ref):
    S = a_ref.shape[0]
    n1_count = DFT_N1
    n2_count = S // DFT_N1
    pz = n1_count + DFT_PITCH_PAD
    pu = n2_count + DFT_PITCH_PAD
    blk = DFT_KRON * DFT_N1
    scale = 1.0 / math.sqrt(S * FGROUP_DIM)

    def channel_dft(j, carry):
        zz = _dot(a_ref[pl.ds(pl.multiple_of(j * blk, blk), blk), :], dftc_ref[...])
        for q in range(DFT_KRON):
            dst = pl.ds(pl.multiple_of((j * DFT_KRON + q) * pz, 8), n1_count)
            zr_ref[dst, :] = zz[q * n1_count:(q + 1) * n1_count, :FGROUP_DIM]
            zi_ref[dst, :] = zz[q * n1_count:(q + 1) * n1_count, FGROUP_DIM:]
        return carry

    lax.fori_loop(0, S // blk, channel_dft, 0, unroll=2)

    def stage_a(n1, carry):
        src = pl.ds(n1, n2_count, stride=pz)
        zn = jnp.concatenate([zr_ref[src, :], zi_ref[src, :]], axis=1).astype(BF16)
        r = _dot(taba_ref[n1], zn)
        dst = pl.ds(pl.multiple_of(n1 * pu, 8), n2_count)
        ur_ref[dst, :] = r[:n2_count, :FGROUP_DIM] + r[n2_count:, FGROUP_DIM:]
        ui_ref[dst, :] = r[:n2_count, FGROUP_DIM:] - r[n2_count:, :FGROUP_DIM]
        return carry

    lax.fori_loop(0, n1_count, stage_a, 0, unroll=4)

    def stage_b(j, carry):
        srcs = [pl.ds(j * DFT_KRON + q, n1_count, stride=pu) for q in range(DFT_KRON)]
        ur = jnp.concatenate([ur_ref[s, :] for s in srcs], axis=0).astype(BF16)
        ui = jnp.concatenate([ui_ref[s, :] for s in srcs], axis=0).astype(BF16)
        re = _dot(kc_ref[...], ur) + _dot(ks_ref[...], ui)
        out = _dot((re * scale).astype(BF16), wf_ref[...])
        for q in range(DFT_KRON):
            y_ref[srcs[q], :] = out[q * n1_count:(q + 1) * n1_count]
        return carry

    lax.fori_loop(0, S // blk, stage_b, 0, unroll=4)

    def compact(k1, carry):
        o_ref[pl.ds(pl.multiple_of(k1 * n2_count, n2_count), n2_count), :] = (
            y_ref[pl.ds(pl.multiple_of(k1 * pu, 8), n2_count), :].astype(BF16))
        return carry

    lax.fori_loop(0, n1_count, compact, 0)


def _dft_tables(S):
    n1c, n2c = DFT_N1, S // DFT_N1
    c = np.arange(FGROUP_DIM)
    ang = 2 * np.pi * np.outer(c, c) / FGROUP_DIM
    dftc = np.concatenate([np.cos(ang), -np.sin(ang)], axis=1)
    n1 = np.arange(n1c)[:, None, None]
    k2 = np.arange(n2c)[None, :, None]
    n2 = np.arange(n2c)[None, None, :]
    th = 2 * np.pi * (n2 * k2 / n2c + n1 * k2 / S)
    taba = np.concatenate([np.cos(th), np.sin(th)], axis=1)
    k1 = np.arange(n1c)
    g = 2 * np.pi * np.outer(k1, k1) / n1c
    eye = np.eye(DFT_KRON)
    kc = np.kron(eye, np.cos(g))
    ks = np.kron(eye, np.sin(g))
    return tuple(jnp.asarray(t, BF16) for t in (dftc, taba, kc, ks))


def _fourier(a, w_fourier, B, S):
    T = a.shape[0]
    dftc, taba, kc, ks = _dft_tables(S)
    full = lambda t: pl.BlockSpec(t.shape, lambda b, g: (0,) * t.ndim)
    blk = pl.BlockSpec((S, FGROUP_DIM), lambda b, g: (b, g))
    return pl.pallas_call(
        _fourier_kernel, grid=(B, N_FGROUPS),
        in_specs=[blk, full(dftc), full(taba), full(kc), full(ks),
                  pl.BlockSpec((None, FGROUP_DIM, FGROUP_DIM), lambda b, g: (g, 0, 0))],
        out_specs=blk,
        out_shape=jax.ShapeDtypeStruct((T, F_WIDTH), BF16),
        scratch_shapes=(
            [pltpu.VMEM((S // DFT_N1 * (DFT_N1 + DFT_PITCH_PAD), FGROUP_DIM), F32)] * 2
            + [pltpu.VMEM((DFT_N1 * (S // DFT_N1 + DFT_PITCH_PAD), FGROUP_DIM), F32)] * 3),
        compiler_params=_cparams("parallel", "parallel"), name="fourier")(
            a, dftc, taba, kc, ks, w_fourier.astype(BF16))


def _attn_kernel(q_ref, kt_ref, v_ref, o_ref, qs_ref, kmax_ref, acc_ref, m_ref, s0_ref, s1_ref):
    tq = q_ref.shape[0]
    n_keys = kt_ref.shape[1]
    tk = min(ATT_TK, n_keys)
    n_chunks = n_keys // tk
    assert n_chunks % 2 == 0 and n_chunks * tk == n_keys

    @pl.when(pl.program_id(2) == 0)
    def _():
        def body(c, best):
            k = kt_ref[:, pl.ds(pl.multiple_of(c * tk, tk), tk)].astype(F32)
            return jnp.maximum(best, jnp.sum(k * k, axis=0, keepdims=True))
        best = lax.fori_loop(0, n_chunks, body, jnp.zeros((1, tk), F32))
        kmax_ref[...] = jnp.broadcast_to(jnp.sqrt(jnp.max(best, axis=1, keepdims=True)), kmax_ref.shape)

    for g in range(Q_GROUP):
        qs_ref[g * tq:(g + 1) * tq, :] = q_ref[:, g * HEAD_DIM:(g + 1) * HEAD_DIM]
    qf = qs_ref[...].astype(F32)
    bound = jnp.sqrt(jnp.sum(qf * qf, axis=1, keepdims=True)) * kmax_ref[0:1, 0:1] * ATT_BOUND_SLACK

    def chunk(c):
        cols = pl.ds(pl.multiple_of(c * tk, tk), tk)
        return _dot(qs_ref[...], kt_ref[:, cols]), v_ref[cols, :]

    def scores(c):
        return _dot(qs_ref[...], kt_ref[:, pl.ds(pl.multiple_of(c * tk, tk), tk)])

    def weighted(s_buf, c):
        v = v_ref[pl.ds(pl.multiple_of(c * tk, tk), tk), :]
        acc_ref[...] += _dot(jnp.exp2(s_buf[...] - bound).astype(BF16), v)

    def fast(c2, carry):
        c = 2 * c2
        s1_ref[...] = scores(c + 1)
        weighted(s0_ref, c)
        s0_ref[...] = scores(jnp.minimum(c + 2, n_chunks - 1))
        weighted(s1_ref, c + 1)
        return carry

    acc_ref[...] = jnp.zeros(acc_ref.shape, F32)
    s0_ref[...] = scores(0)
    lax.fori_loop(0, n_chunks // 2, fast, 0)
    underflow = jnp.min(acc_ref[:, HEAD_DIM:HEAD_DIM + 1]) < ATT_MIN_ROW_SUM

    @pl.when(underflow)
    def _():
        def safe(c, carry):
            s, v = chunk(c)
            m_old = m_ref[...]
            m_new = jnp.maximum(m_old, jnp.max(s, axis=1, keepdims=True))
            acc_ref[...] = jnp.exp2(m_old - m_new) * acc_ref[...] + _dot(jnp.exp2(s - m_new).astype(BF16), v)
            m_ref[...] = m_new
            return carry

        m_ref[...] = jnp.full(m_ref.shape, NEG_INF, F32)
        acc_ref[...] = jnp.zeros(acc_ref.shape, F32)
        lax.fori_loop(0, n_chunks, safe, 0)

    acc = acc_ref[...]
    o = acc[:, :HEAD_DIM] / acc[:, HEAD_DIM:HEAD_DIM + 1]
    o_ref[...] = jnp.concatenate([o[g * tq:(g + 1) * tq] for g in range(Q_GROUP)], axis=1).astype(BF16)


def _attention(q, kt, v2, B, S):
    T = q.shape[0]
    tq = ATT_TQ
    nq = S // tq
    gw = Q_GROUP * HEAD_DIM
    rows = Q_GROUP * tq
    return pl.pallas_call(
        _attn_kernel, grid=(B, N_KV_HEADS, nq),
        in_specs=[pl.BlockSpec((tq, gw), lambda b, h, i: (b * nq + i, h)),
                  pl.BlockSpec((None, HEAD_DIM, S), lambda b, h, i: (b, h, 0)),
                  pl.BlockSpec((None, None, S, 128), lambda b, h, i: (h, b, 0, 0))],
        out_specs=pl.BlockSpec((tq, gw), lambda b, h, i: (b * nq + i, h)),
        out_shape=jax.ShapeDtypeStruct((T, Q_WIDTH), BF16),
        scratch_shapes=[pltpu.VMEM((rows, HEAD_DIM), BF16), pltpu.VMEM((8, 128), F32),
                        pltpu.VMEM((rows, 128), F32), pltpu.VMEM((rows, 1), F32),
                        pltpu.VMEM((rows, min(ATT_TK, S)), F32), pltpu.VMEM((rows, min(ATT_TK, S)), F32)],
        compiler_params=_cparams("parallel", "parallel", "arbitrary"),
        name="attention")(q, kt, v2.reshape(N_KV_HEADS, B, S, 128))


def _even_out_kernel(a_ref, t_ref, wa_ref, wt_ref, x_ref, g_ref, b_ref, o_ref):
    mix = _dot(a_ref[...], wa_ref[...]) + _dot(t_ref[...], wt_ref[...])
    o_ref[...] = _ln(DEEPNORM_ALPHA * x_ref[...] + mix, g_ref[...], b_ref[...])


def _even_out(a_out, attn, w_out, x, g, b):
    T, D = x.shape
    tm = ROW_TILE
    row = lambda w: pl.BlockSpec((tm, w), lambda i: (i, 0))
    full = lambda a: pl.BlockSpec(a.shape, lambda i: (0,) * a.ndim)
    wa = w_out[:F_WIDTH].astype(BF16)
    wt = w_out[F_WIDTH:].astype(BF16)
    g = g.reshape(1, D)
    b = b.reshape(1, D)
    return pl.pallas_call(
        _even_out_kernel, grid=(T // tm,),
        in_specs=[row(F_WIDTH), row(Q_WIDTH), full(wa), full(wt), row(D), full(g), full(b)],
        out_specs=row(D), out_shape=jax.ShapeDtypeStruct((T, D), F32),
        compiler_params=_cparams("parallel"), name="even_out")(a_out, attn, wa, wt, x, g, b)


def _odd_kernel(x_ref, wi_ref, bi_ref, vg_ref, vb_ref, ws_ref, bs_ref, wo_ref, g_ref, b_ref, o_ref,
                gate_ref):
    tm = x_ref.shape[0]
    x = x_ref[...]
    h = _dot(x.astype(BF16), wi_ref[...]) + bi_ref[...]
    h = 0.5 * h * (1.0 + lax.erf(h * (1.0 / math.sqrt(2.0))))
    u = h[:, :C_WIDTH]
    v = _ln(h[:, C_WIDTH:], vg_ref[...], vb_ref[...]).astype(BF16)
    for c in range(tm // CHUNK):
        r0 = c * CHUNK
        for gi in range(N_CGROUPS):
            l0 = gi * CGROUP_DIM
            sv = _dot(ws_ref[gi], v[r0:r0 + CHUNK, l0:l0 + CGROUP_DIM]) + bs_ref[gi]
            gate_ref[r0:r0 + CHUNK, l0:l0 + CGROUP_DIM] = (
                u[r0:r0 + CHUNK, l0:l0 + CGROUP_DIM] * sv).astype(BF16)
    mix = _dot(gate_ref[...], wo_ref[...])
    o_ref[...] = _ln(DEEPNORM_ALPHA * x + mix, g_ref[...], b_ref[...])


def _odd_mixer(x, w_in, b_in, v_g, v_b, w_s, b_s, w_out, g, b):
    T, D = x.shape
    tm = ROW_TILE
    row = pl.BlockSpec((tm, D), lambda i: (i, 0))
    full = lambda a: pl.BlockSpec(a.shape, lambda i: (0,) * a.ndim)
    args = [w_in.astype(BF16), b_in.reshape(1, 2 * C_WIDTH), v_g.reshape(1, C_WIDTH),
            v_b.reshape(1, C_WIDTH), w_s.astype(BF16),
            jnp.broadcast_to(b_s[:, :, None], (N_CGROUPS, CHUNK, CGROUP_DIM)).astype(F32),
            w_out.astype(BF16), g.reshape(1, D), b.reshape(1, D)]
    return pl.pallas_call(
        _odd_kernel, grid=(T // tm,),
        in_specs=[row] + [full(a) for a in args],
        out_specs=row, out_shape=jax.ShapeDtypeStruct((T, D), F32),
        scratch_shapes=[pltpu.VMEM((tm, C_WIDTH), BF16)],
        compiler_params=_cparams("parallel"), name="odd_mixer")(x, *args)


def _router_kernel(x_ref, w_ref, rb_ref, tri_ref, eidx_ref, wts_ref, pos_ref, cnt_ref, xp_ref, run_ref):
    tm = x_ref.shape[0]
    i = pl.program_id(0)

    @pl.when(i == 0)
    def _():
        run_ref[...] = jnp.zeros(run_ref.shape, F32)

    x = x_ref[...]
    xp_ref[...] = _pack_halves(x)
    xh = x.astype(BF16)
    xl = (x - xh.astype(F32)).astype(BF16)
    nt = (((1,), (1,)), ((), ()))
    dg = lambda a, c: lax.dot_general(a, c, nt, preferred_element_type=F32)
    logits = dg(w_ref[0], xh) + dg(w_ref[0], xl) + dg(w_ref[1], xh)
    scores = jax.nn.sigmoid(logits)
    sel = scores + rb_ref[...]

    i8 = lax.broadcasted_iota(I32, (GROUP_SIZE, tm), 0)
    gsc_rows = []
    for gidx in range(N_EXPERT_GROUPS):
        sg = sel[gidx * GROUP_SIZE:(gidx + 1) * GROUP_SIZE, :]
        m1 = jnp.max(sg, axis=0, keepdims=True)
        f1 = jnp.min(jnp.where(sg == m1, i8, GROUP_SIZE), axis=0, keepdims=True)
        m2 = jnp.max(jnp.where(i8 == f1, NEG_INF, sg), axis=0, keepdims=True)
        gsc_rows.append(m1 + m2)
    gsc = jnp.concatenate(gsc_rows, axis=0)

    gsel = jnp.zeros(gsc.shape, F32)
    for _ in range(TOPK_GROUPS):
        m = jnp.max(gsc, axis=0, keepdims=True)
        f = jnp.min(jnp.where(gsc == m, i8, N_EXPERT_GROUPS), axis=0, keepdims=True)
        pick = i8 == f
        gsel = jnp.where(pick, 1.0, gsel)
        gsc = jnp.where(pick, NEG_INF, gsc)
    esel = jnp.concatenate(
        [jnp.broadcast_to(gsel[gidx:gidx + 1, :], (GROUP_SIZE, tm)) for gidx in range(N_EXPERT_GROUPS)],
        axis=0)

    cur = jnp.where(esel > 0.0, sel, NEG_INF)
    ei = lax.broadcasted_iota(I32, cur.shape, 0)
    idx_rows, sc_rows = [], []
    chosen = jnp.zeros(cur.shape, F32)
    for _ in range(TOP_K):
        m = jnp.max(cur, axis=0, keepdims=True)
        f = jnp.min(jnp.where(cur == m, ei, N_EXPERTS), axis=0, keepdims=True)
        pick = ei == f
        idx_rows.append(f)
        sc_rows.append(jnp.sum(jnp.where(pick, scores, 0.0), axis=0, keepdims=True))
        chosen = jnp.where(pick, 1.0, chosen)
        cur = jnp.where(pick, NEG_INF, cur)
    eidx = jnp.concatenate(idx_rows, axis=0)
    sc = jnp.concatenate(sc_rows, axis=0)
    eidx_ref[...] = eidx
    wts_ref[...] = sc / jnp.sum(sc, axis=0, keepdims=True) * ROUTE_SCALE

    before = _dot(chosen.astype(BF16), tri_ref[...]) + run_ref[...]
    pos_rows = [jnp.sum(jnp.where(ei == idx_rows[k], before, 0.0), axis=0, keepdims=True)
                for k in range(TOP_K)]
    pos_ref[...] = jnp.concatenate(pos_rows, axis=0).astype(I32)
    run_new = run_ref[...] + jnp.sum(chosen, axis=1, keepdims=True)
    run_ref[...] = run_new
    cnt_ref[...] = jnp.broadcast_to(run_new, cnt_ref.shape).astype(I32)


def _router(x, router_w, router_b, first_row, T):
    D = x.shape[1]
    tm = min(ROUTER_TILE, T)
    n_tiles = T // tm
    first_tile = first_row // tm
    assert n_tiles * tm == T and first_tile * tm == first_row
    wt = router_w.T.astype(F32)
    wh = wt.astype(BF16)
    wl = (wt - wh.astype(F32)).astype(BF16)
    w2 = jnp.stack([wh, wl])
    rb = router_b.astype(F32).reshape(N_EXPERTS, 1)
    tri = jnp.asarray(np.triu(np.ones((tm, tm)), 1), BF16)
    full = lambda a: pl.BlockSpec(a.shape, lambda i: (0,) * a.ndim)
    col = pl.BlockSpec((TOP_K, tm), lambda i: (0, i))
    return pl.pallas_call(
        _router_kernel, grid=(n_tiles,),
        in_specs=[pl.BlockSpec((tm, D), lambda i: (i + first_tile, 0)), full(w2), full(rb), full(tri)],
        out_specs=[col, col, col, pl.BlockSpec((N_EXPERTS, 128), lambda i: (0, 0)),
                   pl.BlockSpec((tm, HALF), lambda i: (i, 0))],
        out_shape=[jax.ShapeDtypeStruct((TOP_K, T), I32), jax.ShapeDtypeStruct((TOP_K, T), F32),
                   jax.ShapeDtypeStruct((TOP_K, T), I32), jax.ShapeDtypeStruct((N_EXPERTS, 128), I32),
                   jax.ShapeDtypeStruct((T, HALF), I32)],
        scratch_shapes=[pltpu.VMEM((N_EXPERTS, 1), F32)],
        compiler_params=_cparams("arbitrary"), name="router")(x, w2, rb, tri)


def _dest_kernel(start_ref, eidx_ref, pos_ref, o_ref):
    e = eidx_ref[...]
    acc = pos_ref[...]
    for j in range(N_EXPERTS):
        acc = acc + jnp.where(e == j, start_ref[j], 0)
    o_ref[...] = acc


def _dest_rows(eidx, pos, seg_start):
    K, T = eidx.shape
    tl = min(T, 2048)
    blk = pl.BlockSpec((K, tl), lambda i, s: (0, i))
    grid_spec = pltpu.PrefetchScalarGridSpec(
        num_scalar_prefetch=1, grid=(T // tl,), in_specs=[blk, blk], out_specs=blk)
    return pl.pallas_call(
        _dest_kernel, grid_spec=grid_spec, out_shape=jax.ShapeDtypeStruct((K, T), I32),
        compiler_params=_cparams("parallel"), name="dest_rows")(seg_start, eidx, pos)


def _gather_rows(table, idx):
    n_rows = idx.shape[0]
    width = table.shape[1]
    per_worker = n_rows // SC_WORKERS
    n_chunks = per_worker // SC_CHUNK
    assert per_worker * SC_WORKERS == n_rows and n_chunks * SC_CHUNK == per_worker
    mesh = plsc.VectorSubcoreMesh(core_axis_name="c", subcore_axis_name="s")

    @functools.partial(
        pl.kernel, mesh=mesh,
        out_type=jax.ShapeDtypeStruct((n_rows, width), table.dtype),
        scratch_types=[pltpu.VMEM((SC_CHUNK,), I32), pltpu.VMEM((SC_CHUNK, width), table.dtype),
                       pltpu.SemaphoreType.DMA])
    def gather(table_hbm, idx_hbm, out_hbm, idx_v, rows_v, sem):
        wid = lax.axis_index("s") * 2 + lax.axis_index("c")
        base = wid * per_worker

        @pl.loop(0, n_chunks)
        def _(j):
            off = base + j * SC_CHUNK
            pltpu.sync_copy(idx_hbm.at[pl.ds(off, SC_CHUNK)], idx_v)
            pltpu.async_copy(table_hbm.at[idx_v], rows_v, sem).wait()
            pltpu.sync_copy(rows_v, out_hbm.at[pl.ds(off, SC_CHUNK)])

    return gather(table, idx)


def _scatter_rows(rows, dest):
    n_tok, width = rows.shape
    n_dst = dest.shape[0]
    per_worker = n_tok // SC_WORKERS
    n_chunks = per_worker // SC_CHUNK
    assert per_worker * SC_WORKERS == n_tok and n_chunks * SC_CHUNK == per_worker
    mesh = plsc.VectorSubcoreMesh(core_axis_name="c", subcore_axis_name="s")

    @functools.partial(
        pl.kernel, mesh=mesh,
        out_type=jax.ShapeDtypeStruct((n_dst * n_tok, width), rows.dtype),
        scratch_types=[pltpu.VMEM((n_dst, SC_CHUNK), I32), pltpu.VMEM((SC_CHUNK, width), rows.dtype),
                       pltpu.SemaphoreType.DMA])
    def scatter(rows_hbm, dest_hbm, out_hbm, idx_v, rows_v, sem):
        wid = lax.axis_index("s") * 2 + lax.axis_index("c")
        base = wid * per_worker

        @pl.loop(0, n_chunks)
        def _(j):
            off = base + j * SC_CHUNK
            pltpu.sync_copy(dest_hbm.at[:, pl.ds(off, SC_CHUNK)], idx_v)
            pltpu.sync_copy(rows_hbm.at[pl.ds(off, SC_CHUNK)], rows_v)
            copies = [pltpu.async_copy(rows_v, out_hbm.at[idx_v.at[k]], sem) for k in range(n_dst)]
            for c in copies:
                c.wait()

    return scatter(rows, dest)


def _expert_kernel(blk_ref, exp_ref, lo_ref, hi_ref, xs_ref, wg_ref, wu_ref, wd_ref, ys_ref,
                   wgu_s, wd_s):
    i = pl.program_id(0)
    prev = jnp.maximum(i - 1, 0)

    @pl.when(jnp.logical_or(i == 0, exp_ref[i] != exp_ref[prev]))
    def _():
        wgu_s[:, :EXPERT_DIM] = wg_ref[...].astype(BF16)
        wgu_s[:, EXPERT_DIM:] = wu_ref[...].astype(BF16)
        wd_s[...] = wd_ref[...].astype(BF16)

    first = jnp.logical_or(i == 0, blk_ref[i] != blk_ref[prev])
    lo = lo_ref[i]
    hi = hi_ref[i]

    def sub_block(r0):
        rows = slice(r0, r0 + EXPERT_SUB_ROWS)
        w = xs_ref[rows, :]
        xlo = lax.bitcast_convert_type(w.astype(jnp.int16), BF16)
        xhi = lax.bitcast_convert_type(lax.shift_right_logical(w, 16).astype(jnp.int16), BF16)
        gu = _dot(xlo, wgu_s[:HALF, :]) + _dot(xhi, wgu_s[HALF:, :])
        g = gu[:, :EXPERT_DIM]
        hb = (g * jax.nn.sigmoid(g) * gu[:, EXPERT_DIM:]).astype(BF16)
        y = _dot(hb, wd_s[...])
        packed = pltpu.pack_elementwise([y[:, :HALF], y[:, HALF:]], packed_dtype=BF16)
        row = r0 + lax.broadcasted_iota(I32, (EXPERT_SUB_ROWS, 1), 0)
        mine = jnp.logical_and(row >= lo, row < hi)
        kept = jnp.where(first, 0, ys_ref[rows, :])
        ys_ref[rows, :] = jnp.where(mine, packed, kept)

    for r0 in range(0, EXPERT_ROWS, EXPERT_SUB_ROWS):
        pl.when(jnp.logical_and(lo < r0 + EXPERT_SUB_ROWS, hi > r0))(functools.partial(sub_block, r0))


def _expert_items(counts, n_rows):
    n_blocks = n_rows // EXPERT_ROWS
    n_items = n_blocks + N_EXPERTS - 1
    end = jnp.cumsum(counts)
    start = end - counts
    first_blk = start // EXPERT_ROWS
    n_blk = jnp.where(counts > 0, (end - 1) // EXPERT_ROWS - first_blk + 1, 0)
    item_end = jnp.cumsum(n_blk)
    item_start = item_end - n_blk
    slot = jnp.arange(n_items, dtype=I32)
    e = jnp.minimum(jnp.sum((item_end[None, :] <= slot[:, None]).astype(I32), axis=1), N_EXPERTS - 1)
    onehot = (e[:, None] == jnp.arange(N_EXPERTS, dtype=I32)[None, :]).astype(I32)
    pick = lambda v: jnp.sum(onehot * v[None, :], axis=1)
    valid = slot < item_end[-1]
    blk = jnp.where(valid, pick(first_blk) + slot - pick(item_start), n_blocks - 1)
    lo = jnp.clip(pick(start) - blk * EXPERT_ROWS, 0, EXPERT_ROWS)
    hi = jnp.clip(pick(end) - blk * EXPERT_ROWS, 0, EXPERT_ROWS)
    last_e = jnp.max(jnp.where(counts > 0, jnp.arange(N_EXPERTS, dtype=I32), 0))
    e = jnp.where(valid, e, last_e)
    hi = jnp.where(valid, hi, 0)
    lo = jnp.where(valid, lo, 0)
    return blk.astype(I32), e.astype(I32), lo.astype(I32), hi.astype(I32)


def _experts(xs, items, w_gate, w_up, w_down, layer):
    n_rows = xs.shape[0]
    n_items = items[0].shape[0]
    wmap = lambda i, b, e, lo, hi: (layer, e[i], 0, 0)
    grid_spec = pltpu.PrefetchScalarGridSpec(
        num_scalar_prefetch=4, grid=(n_items,),
        in_specs=[pl.BlockSpec((EXPERT_ROWS, HALF), lambda i, b, e, lo, hi: (b[i], 0)),
                  pl.BlockSpec((None, None, D_MODEL, EXPERT_DIM), wmap),
                  pl.BlockSpec((None, None, D_MODEL, EXPERT_DIM), wmap),
                  pl.BlockSpec((None, None, EXPERT_DIM, D_MODEL), wmap)],
        out_specs=pl.BlockSpec((EXPERT_ROWS, HALF), lambda i, b, e, lo, hi: (b[i], 0)),
        scratch_shapes=[pltpu.VMEM((D_MODEL, 2 * EXPERT_DIM), BF16),
                        pltpu.VMEM((EXPERT_DIM, D_MODEL), BF16)])
    return pl.pallas_call(
        _expert_kernel, grid_spec=grid_spec,
        out_shape=jax.ShapeDtypeStruct((n_rows, HALF), I32),
        compiler_params=_cparams("arbitrary"), name="experts")(*items, xs, w_gate, w_up, w_down)


def _moe_out_kernel(x_ref, yg_ref, wt_ref, sgu_ref, sd_ref, g_ref, b_ref, *rest):
    o_ref = rest[-1]
    x = x_ref[...]
    wt = wt_ref[...]
    lo = jnp.zeros((x.shape[0], HALF), F32)
    hi = jnp.zeros((x.shape[0], HALF), F32)
    for k in range(TOP_K):
        w = yg_ref[k]
        wk = wt[:, k:k + 1]
        lo = lo + wk * _unpack_lo(w)
        hi = hi + wk * _unpack_hi(w)
    gu = _dot(x.astype(BF16), sgu_ref[...])
    g = gu[:, :EXPERT_DIM]
    hs = (g * jax.nn.sigmoid(g) * gu[:, EXPERT_DIM:]).astype(BF16)
    ffn = jnp.concatenate([lo, hi], axis=1) + _dot(hs, sd_ref[...])
    o_ref[...] = _ln(DEEPNORM_ALPHA * x + ffn, g_ref[...], b_ref[...])


def _moe_out(x, yg, wts, sh_gate, sh_up, sh_down, g, b, first_tile, partial_out):
    T, D = x.shape
    tm = ROW_TILE
    n_tiles = yg.shape[1] // tm
    xrow = pl.BlockSpec((tm, D), lambda i: (i + first_tile, 0))
    full = lambda a: pl.BlockSpec(a.shape, lambda i: (0,) * a.ndim)
    sgu = jnp.concatenate([sh_gate, sh_up], axis=1).astype(BF16)
    sd = sh_down.astype(BF16)
    g = g.reshape(1, D)
    b = b.reshape(1, D)
    args = [x, yg, wts, sgu, sd, g, b]
    in_specs = [xrow, pl.BlockSpec((TOP_K, tm, HALF), lambda i: (0, i, 0)),
                pl.BlockSpec((tm, TOP_K), lambda i: (i, 0)), full(sgu), full(sd), full(g), full(b)]
    aliases = {}
    if partial_out is not None:
        args.append(partial_out)
        in_specs.append(pl.BlockSpec(memory_space=pl.ANY))
        aliases = {len(args) - 1: 0}
    return pl.pallas_call(
        _moe_out_kernel, grid=(n_tiles,), in_specs=in_specs,
        out_specs=xrow, out_shape=jax.ShapeDtypeStruct((T, D), F32),
        input_output_aliases=aliases,
        compiler_params=_cparams("parallel"), name="moe_out")(*args)


def _moe(x, router_w, router_b, w_gate, w_up, w_down, layer, sh_gate, sh_up, sh_down, g, b):
    T = x.shape[0]
    tiles = T // ROW_TILE // MOE_TOKEN_GROUPS
    tg = tiles * ROW_TILE
    out = None
    for grp in range(MOE_TOKEN_GROUPS):
        eidx, wts, pos, cnt, xp = _router(x, router_w, router_b, grp * tg, tg)
        counts = cnt[:, 0]
        seg_start = (jnp.cumsum(counts) - counts).astype(I32)
        dest = _dest_rows(eidx, pos, seg_start)
        xs = _scatter_rows(xp, dest)
        ys = _experts(xs, _expert_items(counts, tg * TOP_K), w_gate, w_up, w_down, layer)
        yg = _gather_rows(ys, dest.reshape(tg * TOP_K)).reshape(TOP_K, tg, HALF)
        out = _moe_out(x, yg, wts.T, sh_gate, sh_up, sh_down, g, b, grp * tiles, out)
    return out


def kernel(x, ln_in_g, ln_in_b, e_w_in, e_w_fourier, e_q_gain, e_k_gain, e_w_out, o_w_in, o_b_in, o_v_ln_g, o_v_ln_b, o_w_spatial, o_b_spatial, o_w_out, ln_mix_g, ln_mix_b, ln_ffn_g, ln_ffn_b, router_w, router_b, exp_w_gate, exp_w_up, exp_w_down, sh_w_gate, sh_w_up, sh_w_down):
    B, S, D = x.shape
    T = B * S
    h = _layer_norm(x.reshape(T, D), ln_in_g, ln_in_b)
    for i in range(DEPTH):
        j = i // 2
        if i % 2 == 0:
            a, q, kt, v2 = _even_in(h, e_w_in[j], e_q_gain[j], e_k_gain[j], B, S)
            a_out = _fourier(a, e_w_fourier[j], B, S)
            attn = _attention(q, kt, v2, B, S)
            h = _even_out(a_out, attn, e_w_out[j], h, ln_mix_g[i], ln_mix_b[i])
        else:
            h = _odd_mixer(h, o_w_in[j], o_b_in[j], o_v_ln_g[j], o_v_ln_b[j], o_w_spatial[j],
                           o_b_spatial[j], o_w_out[j], ln_mix_g[i], ln_mix_b[i])
        h = _moe(h, router_w[i], router_b[i], exp_w_gate, exp_w_up, exp_w_down, i,
                 sh_w_gate[i], sh_w_up[i], sh_w_down[i], ln_ffn_g[i], ln_ffn_b[i])
    return h.reshape(B, S, D)
```

```python
import functools
import math

import numpy as np
import jax
import jax.numpy as jnp
from jax import lax
from jax.experimental import pallas as pl
from jax.experimental.pallas import tpu as pltpu
from jax.experimental.pallas import tpu_sc as plsc

F32 = jnp.float32
BF16 = jnp.bfloat16
I32 = jnp.int32

D_MODEL = 1024
DEPTH = 4
GRID_W = 64
N_FGROUPS = 4
FGROUP_DIM = 128
F_WIDTH = N_FGROUPS * FGROUP_DIM
N_HEADS = 8
N_KV_HEADS = 2
HEAD_DIM = 64
Q_GROUP = N_HEADS // N_KV_HEADS
Q_WIDTH = N_HEADS * HEAD_DIM
KV_WIDTH = N_KV_HEADS * HEAD_DIM
ROPE_THETA = 10000.0
ROPE_PAIRS = HEAD_DIM // 4
EVEN_IN_WIDTH = F_WIDTH + Q_WIDTH + 2 * KV_WIDTH
CHUNK = 128
N_CGROUPS = 8
CGROUP_DIM = D_MODEL // N_CGROUPS
C_WIDTH = N_CGROUPS * CGROUP_DIM
N_EXPERTS = 64
EXPERT_DIM = 256
TOP_K = 8
N_EXPERT_GROUPS = 8
GROUP_SIZE = N_EXPERTS // N_EXPERT_GROUPS
TOPK_GROUPS = 4
ROUTE_SCALE = 2.5
LN_EPS = 1e-5
QK_EPS = 1e-6
DEEPNORM_ALPHA = (2 * DEPTH) ** 0.25

VMEM_LIMIT_BYTES = 56 * 1024 * 1024
ROW_TILE = 512
DFT_N1 = 64
DFT_KRON = 4
DFT_PITCH_PAD = 8
ROUTER_TILE = 1024
EXPERT_ROWS = 2048
EXPERT_SUB_ROWS = 512
HALF = D_MODEL // 2
SC_WORKERS = 32
SC_CHUNK = 128
BATCH_CHAINS = 2
MOE_TOKEN_GROUPS = 1
ATT_TQ = 256
ATT_TK = 512
ATT_BOUND_SLACK = 1.0 + 2.0 ** -7
ATT_MIN_ROW_SUM = 2.0 ** -80
NEG_INF = float("-inf")


def _cparams(*sem):
    return pltpu.CompilerParams(dimension_semantics=sem, vmem_limit_bytes=VMEM_LIMIT_BYTES)


def _ln(x, g, b):
    mu = jnp.mean(x, axis=-1, keepdims=True)
    xc = x - mu
    var = jnp.mean(xc * xc, axis=-1, keepdims=True)
    return xc * lax.rsqrt(var + LN_EPS) * g + b


def _dot(a, b):
    return jnp.dot(a, b, preferred_element_type=F32)


def _pack_halves(y):
    lo = lax.bitcast_convert_type(y[:, :HALF].astype(BF16).astype(F32), I32)
    hi = lax.bitcast_convert_type(y[:, HALF:].astype(BF16).astype(F32), I32)
    return lax.shift_right_logical(lo, 16) | (hi & jnp.int32(-65536))


def _unpack_lo(w):
    return lax.bitcast_convert_type(lax.shift_left(w, 16), F32)


def _unpack_hi(w):
    return lax.bitcast_convert_type(w & jnp.int32(-65536), F32)


def _ln_kernel(x_ref, g_ref, b_ref, o_ref):
    o_ref[...] = _ln(x_ref[...], g_ref[...], b_ref[...])


def _layer_norm(x, g, b):
    T, D = x.shape
    row = pl.BlockSpec((ROW_TILE, D), lambda i: (i, 0))
    vec = pl.BlockSpec((1, D), lambda i: (0, 0))
    return pl.pallas_call(
        _ln_kernel, grid=(T // ROW_TILE,), in_specs=[row, vec, vec], out_specs=row,
        out_shape=jax.ShapeDtypeStruct((T, D), F32), compiler_params=_cparams("parallel"),
        name="ln_in")(x, g.reshape(1, D), b.reshape(1, D))


def _even_in_kernel(x_ref, w_ref, qm_ref, km_ref, qg_ref, kg_ref, cos_ref, sin_ref,
                    a_ref, q_ref, kt_ref, v_ref):
    tm = x_ref.shape[0]
    h = _dot(x_ref[...].astype(BF16), w_ref[...])
    a_ref[...] = h[:, :F_WIDTH].astype(BF16)
    q = h[:, F_WIDTH:F_WIDTH + Q_WIDTH]
    k = h[:, F_WIDTH + Q_WIDTH:F_WIDTH + Q_WIDTH + KV_WIDTH]
    v = h[:, F_WIDTH + Q_WIDTH + KV_WIDTH:]
    cos = cos_ref[...]
    sin = sin_ref[...]
    lane = lax.broadcasted_iota(I32, (tm, 128), 1)
    first_of_pair = (lane & ROPE_PAIRS) == 0

    def mean_sq(xf, m_ref):
        sq = xf * xf
        hi = sq.astype(BF16)
        lo = (sq - hi.astype(F32)).astype(BF16)
        return _dot(hi, m_ref[...]) + _dot(lo, m_ref[...])

    def rope(xn):
        sw = jnp.where(first_of_pair, pltpu.roll(xn, 128 - ROPE_PAIRS, 1), pltpu.roll(xn, ROPE_PAIRS, 1))
        return xn * cos + sw * sin

    qn = q * lax.rsqrt(mean_sq(q, qm_ref) + QK_EPS) * qg_ref[...]
    scale = math.log2(math.e) / math.sqrt(HEAD_DIM)
    for c in range(Q_WIDTH // 128):
        q_ref[:, c * 128:(c + 1) * 128] = (rope(qn[:, c * 128:(c + 1) * 128]) * scale).astype(BF16)
    kn = k * lax.rsqrt(mean_sq(k, km_ref) + QK_EPS) * kg_ref[...]
    kt_ref[...] = rope(kn).T.astype(BF16)
    ones_col = jnp.where(lane == HEAD_DIM, 1.0, 0.0)
    low = lane < HEAD_DIM
    v_ref[0] = jnp.where(low, v, ones_col).astype(BF16)
    v_ref[1] = jnp.where(low, pltpu.roll(v, HEAD_DIM, 1), ones_col).astype(BF16)


def _rope_tables(S):
    rows = S // GRID_W
    t = np.arange(S)
    inv = ROPE_THETA ** (-np.arange(ROPE_PAIRS, dtype=np.float64) / ROPE_PAIRS)
    ang_r = (t // GRID_W)[:, None] * inv
    ang_c = (t % GRID_W)[:, None] * inv
    del rows
    cos = np.concatenate([np.cos(ang_r), np.cos(ang_r), np.cos(ang_c), np.cos(ang_c)], axis=1)
    sin = np.concatenate([-np.sin(ang_r), np.sin(ang_r), -np.sin(ang_c), np.sin(ang_c)], axis=1)
    return (jnp.asarray(np.tile(cos, (1, 2)), F32), jnp.asarray(np.tile(sin, (1, 2)), F32))


def _head_mean_matrix(width):
    m = np.kron(np.eye(width // HEAD_DIM), np.full((HEAD_DIM, HEAD_DIM), 1.0 / HEAD_DIM))
    return jnp.asarray(m, BF16)


def _even_in(x, w_in, q_gain, k_gain, B, S):
    T, D = x.shape
    tm = ROW_TILE
    ns = S // tm
    cos, sin = _rope_tables(S)
    row = lambda w: pl.BlockSpec((tm, w), lambda i: (i, 0))
    full = lambda a: pl.BlockSpec(a.shape, lambda i: (0,) * a.ndim)
    tab = pl.BlockSpec((tm, 128), lambda i: (i % ns, 0))
    w = w_in.astype(BF16)
    qm = _head_mean_matrix(Q_WIDTH)
    km = _head_mean_matrix(KV_WIDTH)
    qg = jnp.tile(q_gain.astype(F32), N_HEADS).reshape(1, Q_WIDTH)
    kg = jnp.tile(k_gain.astype(F32), N_KV_HEADS).reshape(1, KV_WIDTH)
    return pl.pallas_call(
        _even_in_kernel, grid=(T // tm,),
        in_specs=[row(D), full(w), full(qm), full(km), full(qg), full(kg), tab, tab],
        out_specs=[row(F_WIDTH), row(Q_WIDTH),
                   pl.BlockSpec((None, KV_WIDTH, tm), lambda i: (i // ns, 0, i % ns)),
                   pl.BlockSpec((N_KV_HEADS, tm, 128), lambda i: (0, i, 0))],
        out_shape=[jax.ShapeDtypeStruct((T, F_WIDTH), BF16),
                   jax.ShapeDtypeStruct((T, Q_WIDTH), BF16),
                   jax.ShapeDtypeStruct((B, KV_WIDTH, S), BF16),
                   jax.ShapeDtypeStruct((N_KV_HEADS, T, 128), BF16)],
        compiler_params=_cparams("parallel"), name="even_in")(x, w, qm, km, qg, kg, cos, sin)


def _fourier_kernel(a_ref, dftc_ref, taba_ref, kc_ref, ks_ref, wf_ref, o_ref,
                    zr_ref, zi_ref, ur_ref, ui_ref, y_ref):
    S = a_ref.shape[0]
    n1_count = DFT_N1
    n2_count = S // DFT_N1
    pz = n1_count + DFT_PITCH_PAD
    pu = n2_count + DFT_PITCH_PAD
    blk = DFT_KRON * DFT_N1
    scale = 1.0 / math.sqrt(S * FGROUP_DIM)

    def channel_dft(j, carry):
        zz = _dot(a_ref[pl.ds(pl.multiple_of(j * blk, blk), blk), :], dftc_ref[...])
        for q in range(DFT_KRON):
            dst = pl.ds(pl.multiple_of((j * DFT_KRON + q) * pz, 8), n1_count)
            zr_ref[dst, :] = zz[q * n1_count:(q + 1) * n1_count, :FGROUP_DIM]
            zi_ref[dst, :] = zz[q * n1_count:(q + 1) * n1_count, FGROUP_DIM:]
        return carry

    lax.fori_loop(0, S // blk, channel_dft, 0, unroll=2)

    def stage_a(n1, carry):
        src = pl.ds(n1, n2_count, stride=pz)
        zn = jnp.concatenate([zr_ref[src, :], zi_ref[src, :]], axis=1).astype(BF16)
        r = _dot(taba_ref[n1], zn)
        dst = pl.ds(pl.multiple_of(n1 * pu, 8), n2_count)
        ur_ref[dst, :] = r[:n2_count, :FGROUP_DIM] + r[n2_count:, FGROUP_DIM:]
        ui_ref[dst, :] = r[:n2_count, FGROUP_DIM:] - r[n2_count:, :FGROUP_DIM]
        return carry

    lax.fori_loop(0, n1_count, stage_a, 0, unroll=4)

    def stage_b(j, carry):
        srcs = [pl.ds(j * DFT_KRON + q, n1_count, stride=pu) for q in range(DFT_KRON)]
        ur = jnp.concatenate([ur_ref[s, :] for s in srcs], axis=0).astype(BF16)
        ui = jnp.concatenate([ui_ref[s, :] for s in srcs], axis=0).astype(BF16)
        re = _dot(kc_ref[...], ur) + _dot(ks_ref[...], ui)
        out = _dot((re * scale).astype(BF16), wf_ref[...])
        for q in range(DFT_KRON):
            y_ref[srcs[q], :] = out[q * n1_count:(q + 1) * n1_count]
        return carry

    lax.fori_loop(0, S // blk, stage_b, 0, unroll=4)

    def compact(k1, carry):
        o_ref[pl.ds(pl.multiple_of(k1 * n2_count, n2_count), n2_count), :] = (
            y_ref[pl.ds(pl.multiple_of(k1 * pu, 8), n2_count), :].astype(BF16))
        return carry

    lax.fori_loop(0, n1_count, compact, 0)


def _dft_tables(S):
    n1c, n2c = DFT_N1, S // DFT_N1
    c = np.arange(FGROUP_DIM)
    ang = 2 * np.pi * np.outer(c, c) / FGROUP_DIM
    dftc = np.concatenate([np.cos(ang), -np.sin(ang)], axis=1)
    n1 = np.arange(n1c)[:, None, None]
    k2 = np.arange(n2c)[None, :, None]
    n2 = np.arange(n2c)[None, None, :]
    th = 2 * np.pi * (n2 * k2 / n2c + n1 * k2 / S)
    taba = np.concatenate([np.cos(th), np.sin(th)], axis=1)
    k1 = np.arange(n1c)
    g = 2 * np.pi * np.outer(k1, k1) / n1c
    eye = np.eye(DFT_KRON)
    kc = np.kron(eye, np.cos(g))
    ks = np.kron(eye, np.sin(g))
    return tuple(jnp.asarray(t, BF16) for t in (dftc, taba, kc, ks))


def _fourier(a, w_fourier, B, S):
    T = a.shape[0]
    dftc, taba, kc, ks = _dft_tables(S)
    full = lambda t: pl.BlockSpec(t.shape, lambda b, g: (0,) * t.ndim)
    blk = pl.BlockSpec((S, FGROUP_DIM), lambda b, g: (b, g))
    return pl.pallas_call(
        _fourier_kernel, grid=(B, N_FGROUPS),
        in_specs=[blk, full(dftc), full(taba), full(kc), full(ks),
                  pl.BlockSpec((None, FGROUP_DIM, FGROUP_DIM), lambda b, g: (g, 0, 0))],
        out_specs=blk,
        out_shape=jax.ShapeDtypeStruct((T, F_WIDTH), BF16),
        scratch_shapes=(
            [pltpu.VMEM((S // DFT_N1 * (DFT_N1 + DFT_PITCH_PAD), FGROUP_DIM), F32)] * 2
            + [pltpu.VMEM((DFT_N1 * (S // DFT_N1 + DFT_PITCH_PAD), FGROUP_DIM), F32)] * 3),
        compiler_params=_cparams("parallel", "parallel"), name="fourier")(
            a, dftc, taba, kc, ks, w_fourier.astype(BF16))


def _attn_kernel(q_ref, kt_ref, v_ref, o_ref, qs_ref, kmax_ref, acc_ref, m_ref, s0_ref, s1_ref):
    tq = q_ref.shape[0]
    n_keys = kt_ref.shape[1]
    tk = min(ATT_TK, n_keys)
    n_chunks = n_keys // tk
    assert n_chunks % 2 == 0 and n_chunks * tk == n_keys

    @pl.when(pl.program_id(2) == 0)
    def _():
        def body(c, best):
            k = kt_ref[:, pl.ds(pl.multiple_of(c * tk, tk), tk)].astype(F32)
            return jnp.maximum(best, jnp.sum(k * k, axis=0, keepdims=True))
        best = lax.fori_loop(0, n_chunks, body, jnp.zeros((1, tk), F32))
        kmax_ref[...] = jnp.broadcast_to(jnp.sqrt(jnp.max(best, axis=1, keepdims=True)), kmax_ref.shape)

    for g in range(Q_GROUP):
        qs_ref[g * tq:(g + 1) * tq, :] = q_ref[:, g * HEAD_DIM:(g + 1) * HEAD_DIM]
    qf = qs_ref[...].astype(F32)
    bound = jnp.sqrt(jnp.sum(qf * qf, axis=1, keepdims=True)) * kmax_ref[0:1, 0:1] * ATT_BOUND_SLACK

    def chunk(c):
        cols = pl.ds(pl.multiple_of(c * tk, tk), tk)
        return _dot(qs_ref[...], kt_ref[:, cols]), v_ref[cols, :]

    def scores(c):
        return _dot(qs_ref[...], kt_ref[:, pl.ds(pl.multiple_of(c * tk, tk), tk)])

    def weighted(s_buf, c):
        v = v_ref[pl.ds(pl.multiple_of(c * tk, tk), tk), :]
        acc_ref[...] += _dot(jnp.exp2(s_buf[...] - bound).astype(BF16), v)

    def fast(c2, carry):
        c = 2 * c2
        s1_ref[...] = scores(c + 1)
        weighted(s0_ref, c)
        s0_ref[...] = scores(jnp.minimum(c + 2, n_chunks - 1))
        weighted(s1_ref, c + 1)
        return carry

    acc_ref[...] = jnp.zeros(acc_ref.shape, F32)
    s0_ref[...] = scores(0)
    lax.fori_loop(0, n_chunks // 2, fast, 0)
    underflow = jnp.min(acc_ref[:, HEAD_DIM:HEAD_DIM + 1]) < ATT_MIN_ROW_SUM

    @pl.when(underflow)
    def _():
        def safe(c, carry):
            s, v = chunk(c)
            m_old = m_ref[...]
            m_new = jnp.maximum(m_old, jnp.max(s, axis=1, keepdims=True))
            acc_ref[...] = jnp.exp2(m_old - m_new) * acc_ref[...] + _dot(jnp.exp2(s - m_new).astype(BF16), v)
            m_ref[...] = m_new
            return carry

        m_ref[...] = jnp.full(m_ref.shape, NEG_INF, F32)
        acc_ref[...] = jnp.zeros(acc_ref.shape, F32)
        lax.fori_loop(0, n_chunks, safe, 0)

    acc = acc_ref[...]
    o = acc[:, :HEAD_DIM] / acc[:, HEAD_DIM:HEAD_DIM + 1]
    o_ref[...] = jnp.concatenate([o[g * tq:(g + 1) * tq] for g in range(Q_GROUP)], axis=1).astype(BF16)


def _attention(q, kt, v2, B, S):
    T = q.shape[0]
    tq = ATT_TQ
    nq = S // tq
    gw = Q_GROUP * HEAD_DIM
    rows = Q_GROUP * tq
    return pl.pallas_call(
        _attn_kernel, grid=(B, N_KV_HEADS, nq),
        in_specs=[pl.BlockSpec((tq, gw), lambda b, h, i: (b * nq + i, h)),
                  pl.BlockSpec((None, HEAD_DIM, S), lambda b, h, i: (b, h, 0)),
                  pl.BlockSpec((None, None, S, 128), lambda b, h, i: (h, b, 0, 0))],
        out_specs=pl.BlockSpec((tq, gw), lambda b, h, i: (b * nq + i, h)),
        out_shape=jax.ShapeDtypeStruct((T, Q_WIDTH), BF16),
        scratch_shapes=[pltpu.VMEM((rows, HEAD_DIM), BF16), pltpu.VMEM((8, 128), F32),
                        pltpu.VMEM((rows, 128), F32), pltpu.VMEM((rows, 1), F32),
                        pltpu.VMEM((rows, min(ATT_TK, S)), F32), pltpu.VMEM((rows, min(ATT_TK, S)), F32)],
        compiler_params=_cparams("parallel", "parallel", "arbitrary"),
        name="attention")(q, kt, v2.reshape(N_KV_HEADS, B, S, 128))


def _even_out_kernel(a_ref, t_ref, wa_ref, wt_ref, x_ref, g_ref, b_ref, o_ref):
    mix = _dot(a_ref[...], wa_ref[...]) + _dot(t_ref[...], wt_ref[...])
    o_ref[...] = _ln(DEEPNORM_ALPHA * x_ref[...] + mix, g_ref[...], b_ref[...])


def _even_out(a_out, attn, w_out, x, g, b):
    T, D = x.shape
    tm = ROW_TILE
    row = lambda w: pl.BlockSpec((tm, w), lambda i: (i, 0))
    full = lambda a: pl.BlockSpec(a.shape, lambda i: (0,) * a.ndim)
    wa = w_out[:F_WIDTH].astype(BF16)
    wt = w_out[F_WIDTH:].astype(BF16)
    g = g.reshape(1, D)
    b = b.reshape(1, D)
    return pl.pallas_call(
        _even_out_kernel, grid=(T // tm,),
        in_specs=[row(F_WIDTH), row(Q_WIDTH), full(wa), full(wt), row(D), full(g), full(b)],
        out_specs=row(D), out_shape=jax.ShapeDtypeStruct((T, D), F32),
        compiler_params=_cparams("parallel"), name="even_out")(a_out, attn, wa, wt, x, g, b)


def _odd_kernel(x_ref, wi_ref, bi_ref, vg_ref, vb_ref, ws_ref, bs_ref, wo_ref, g_ref, b_ref, o_ref,
                gate_ref):
    tm = x_ref.shape[0]
    x = x_ref[...]
    h = _dot(x.astype(BF16), wi_ref[...]) + bi_ref[...]
    h = 0.5 * h * (1.0 + lax.erf(h * (1.0 / math.sqrt(2.0))))
    u = h[:, :C_WIDTH]
    v = _ln(h[:, C_WIDTH:], vg_ref[...], vb_ref[...]).astype(BF16)
    for c in range(tm // CHUNK):
        r0 = c * CHUNK
        for gi in range(N_CGROUPS):
            l0 = gi * CGROUP_DIM
            sv = _dot(ws_ref[gi], v[r0:r0 + CHUNK, l0:l0 + CGROUP_DIM]) + bs_ref[gi]
            gate_ref[r0:r0 + CHUNK, l0:l0 + CGROUP_DIM] = (
                u[r0:r0 + CHUNK, l0:l0 + CGROUP_DIM] * sv).astype(BF16)
    mix = _dot(gate_ref[...], wo_ref[...])
    o_ref[...] = _ln(DEEPNORM_ALPHA * x + mix, g_ref[...], b_ref[...])


def _odd_mixer(x, w_in, b_in, v_g, v_b, w_s, b_s, w_out, g, b):
    T, D = x.shape
    tm = ROW_TILE
    row = pl.BlockSpec((tm, D), lambda i: (i, 0))
    full = lambda a: pl.BlockSpec(a.shape, lambda i: (0,) * a.ndim)
    args = [w_in.astype(BF16), b_in.reshape(1, 2 * C_WIDTH), v_g.reshape(1, C_WIDTH),
            v_b.reshape(1, C_WIDTH), w_s.astype(BF16),
            jnp.broadcast_to(b_s[:, :, None], (N_CGROUPS, CHUNK, CGROUP_DIM)).astype(F32),
            w_out.astype(BF16), g.reshape(1, D), b.reshape(1, D)]
    return pl.pallas_call(
        _odd_kernel, grid=(T // tm,),
        in_specs=[row] + [full(a) for a in args],
        out_specs=row, out_shape=jax.ShapeDtypeStruct((T, D), F32),
        scratch_shapes=[pltpu.VMEM((tm, C_WIDTH), BF16)],
        compiler_params=_cparams("parallel"), name="odd_mixer")(x, *args)


def _router_kernel(x_ref, w_ref, rb_ref, tri_ref, eidx_ref, wts_ref, pos_ref, cnt_ref, xp_ref, run_ref):
    tm = x_ref.shape[0]
    i = pl.program_id(0)

    @pl.when(i == 0)
    def _():
        run_ref[...] = jnp.zeros(run_ref.shape, F32)

    x = x_ref[...]
    xp_ref[...] = _pack_halves(x)
    xh = x.astype(BF16)
    xl = (x - xh.astype(F32)).astype(BF16)
    nt = (((1,), (1,)), ((), ()))
    dg = lambda a, c: lax.dot_general(a, c, nt, preferred_element_type=F32)
    logits = dg(w_ref[0], xh) + dg(w_ref[0], xl) + dg(w_ref[1], xh)
    scores = jax.nn.sigmoid(logits)
    sel = scores + rb_ref[...]

    i8 = lax.broadcasted_iota(I32, (GROUP_SIZE, tm), 0)
    gsc_rows = []
    for gidx in range(N_EXPERT_GROUPS):
        sg = sel[gidx * GROUP_SIZE:(gidx + 1) * GROUP_SIZE, :]
        m1 = jnp.max(sg, axis=0, keepdims=True)
        f1 = jnp.min(jnp.where(sg == m1, i8, GROUP_SIZE), axis=0, keepdims=True)
        m2 = jnp.max(jnp.where(i8 == f1, NEG_INF, sg), axis=0, keepdims=True)
        gsc_rows.append(m1 + m2)
    gsc = jnp.concatenate(gsc_rows, axis=0)

    gsel = jnp.zeros(gsc.shape, F32)
    for _ in range(TOPK_GROUPS):
        m = jnp.max(gsc, axis=0, keepdims=True)
        f = jnp.min(jnp.where(gsc == m, i8, N_EXPERT_GROUPS), axis=0, keepdims=True)
        pick = i8 == f
        gsel = jnp.where(pick, 1.0, gsel)
        gsc = jnp.where(pick, NEG_INF, gsc)
    esel = jnp.concatenate(
        [jnp.broadcast_to(gsel[gidx:gidx + 1, :], (GROUP_SIZE, tm)) for gidx in range(N_EXPERT_GROUPS)],
        axis=0)

    cur = jnp.where(esel > 0.0, sel, NEG_INF)
    ei = lax.broadcasted_iota(I32, cur.shape, 0)
    idx_rows, sc_rows = [], []
    chosen = jnp.zeros(cur.shape, F32)
    for _ in range(TOP_K):
        m = jnp.max(cur, axis=0, keepdims=True)
        f = jnp.min(jnp.where(cur == m, ei, N_EXPERTS), axis=0, keepdims=True)
        pick = ei == f
        idx_rows.append(f)
        sc_rows.append(jnp.sum(jnp.where(pick, scores, 0.0), axis=0, keepdims=True))
        chosen = jnp.where(pick, 1.0, chosen)
        cur = jnp.where(pick, NEG_INF, cur)
    eidx = jnp.concatenate(idx_rows, axis=0)
    sc = jnp.concatenate(sc_rows, axis=0)
    eidx_ref[...] = eidx
    wts_ref[...] = sc / jnp.sum(sc, axis=0, keepdims=True) * ROUTE_SCALE

    before = _dot(chosen.astype(BF16), tri_ref[...]) + run_ref[...]
    pos_rows = [jnp.sum(jnp.where(ei == idx_rows[k], before, 0.0), axis=0, keepdims=True)
                for k in range(TOP_K)]
    pos_ref[...] = jnp.concatenate(pos_rows, axis=0).astype(I32)
    run_new = run_ref[...] + jnp.sum(chosen, axis=1, keepdims=True)
    run_ref[...] = run_new
    cnt_ref[...] = jnp.broadcast_to(run_new, cnt_ref.shape).astype(I32)


def _router(x, router_w, router_b, first_row, T):
    D = x.shape[1]
    tm = min(ROUTER_TILE, T)
    n_tiles = T // tm
    first_tile = first_row // tm
    assert n_tiles * tm == T and first_tile * tm == first_row
    wt = router_w.T.astype(F32)
    wh = wt.astype(BF16)
    wl = (wt - wh.astype(F32)).astype(BF16)
    w2 = jnp.stack([wh, wl])
    rb = router_b.astype(F32).reshape(N_EXPERTS, 1)
    tri = jnp.asarray(np.triu(np.ones((tm, tm)), 1), BF16)
    full = lambda a: pl.BlockSpec(a.shape, lambda i: (0,) * a.ndim)
    col = pl.BlockSpec((TOP_K, tm), lambda i: (0, i))
    return pl.pallas_call(
        _router_kernel, grid=(n_tiles,),
        in_specs=[pl.BlockSpec((tm, D), lambda i: (i + first_tile, 0)), full(w2), full(rb), full(tri)],
        out_specs=[col, col, col, pl.BlockSpec((N_EXPERTS, 128), lambda i: (0, 0)),
                   pl.BlockSpec((tm, HALF), lambda i: (i, 0))],
        out_shape=[jax.ShapeDtypeStruct((TOP_K, T), I32), jax.ShapeDtypeStruct((TOP_K, T), F32),
                   jax.ShapeDtypeStruct((TOP_K, T), I32), jax.ShapeDtypeStruct((N_EXPERTS, 128), I32),
                   jax.ShapeDtypeStruct((T, HALF), I32)],
        scratch_shapes=[pltpu.VMEM((N_EXPERTS, 1), F32)],
        compiler_params=_cparams("arbitrary"), name="router")(x, w2, rb, tri)


def _dest_kernel(start_ref, eidx_ref, pos_ref, o_ref):
    e = eidx_ref[...]
    acc = pos_ref[...]
    for j in range(N_EXPERTS):
        acc = acc + jnp.where(e == j, start_ref[j], 0)
    o_ref[...] = acc


def _dest_rows(eidx, pos, seg_start):
    K, T = eidx.shape
    tl = min(T, 2048)
    blk = pl.BlockSpec((K, tl), lambda i, s: (0, i))
    grid_spec = pltpu.PrefetchScalarGridSpec(
        num_scalar_prefetch=1, grid=(T // tl,), in_specs=[blk, blk], out_specs=blk)
    return pl.pallas_call(
        _dest_kernel, grid_spec=grid_spec, out_shape=jax.ShapeDtypeStruct((K, T), I32),
        compiler_params=_cparams("parallel"), name="dest_rows")(seg_start, eidx, pos)


def _gather_rows(table, idx):
    n_rows = idx.shape[0]
    width = table.shape[1]
    per_worker = n_rows // SC_WORKERS
    n_chunks = per_worker // SC_CHUNK
    assert per_worker * SC_WORKERS == n_rows and n_chunks * SC_CHUNK == per_worker
    mesh = plsc.VectorSubcoreMesh(core_axis_name="c", subcore_axis_name="s")

    @functools.partial(
        pl.kernel, mesh=mesh,
        out_type=jax.ShapeDtypeStruct((n_rows, width), table.dtype),
        scratch_types=[pltpu.VMEM((SC_CHUNK,), I32), pltpu.VMEM((SC_CHUNK, width), table.dtype),
                       pltpu.SemaphoreType.DMA])
    def gather(table_hbm, idx_hbm, out_hbm, idx_v, rows_v, sem):
        wid = lax.axis_index("s") * 2 + lax.axis_index("c")
        base = wid * per_worker

        @pl.loop(0, n_chunks)
        def _(j):
            off = base + j * SC_CHUNK
            pltpu.sync_copy(idx_hbm.at[pl.ds(off, SC_CHUNK)], idx_v)
            pltpu.async_copy(table_hbm.at[idx_v], rows_v, sem).wait()
            pltpu.sync_copy(rows_v, out_hbm.at[pl.ds(off, SC_CHUNK)])

    return gather(table, idx)


def _scatter_rows(rows, dest):
    n_tok, width = rows.shape
    n_dst = dest.shape[0]
    per_worker = n_tok // SC_WORKERS
    n_chunks = per_worker // SC_CHUNK
    assert per_worker * SC_WORKERS == n_tok and n_chunks * SC_CHUNK == per_worker
    mesh = plsc.VectorSubcoreMesh(core_axis_name="c", subcore_axis_name="s")

    @functools.partial(
        pl.kernel, mesh=mesh,
        out_type=jax.ShapeDtypeStruct((n_dst * n_tok, width), rows.dtype),
        scratch_types=[pltpu.VMEM((n_dst, SC_CHUNK), I32), pltpu.VMEM((SC_CHUNK, width), rows.dtype),
                       pltpu.SemaphoreType.DMA])
    def scatter(rows_hbm, dest_hbm, out_hbm, idx_v, rows_v, sem):
        wid = lax.axis_index("s") * 2 + lax.axis_index("c")
        base = wid * per_worker

        @pl.loop(0, n_chunks)
        def _(j):
            off = base + j * SC_CHUNK
            pltpu.sync_copy(dest_hbm.at[:, pl.ds(off, SC_CHUNK)], idx_v)
            pltpu.sync_copy(rows_hbm.at[pl.ds(off, SC_CHUNK)], rows_v)
            copies = [pltpu.async_copy(rows_v, out_hbm.at[idx_v.at[k]], sem) for k in range(n_dst)]
            for c in copies:
                c.wait()

    return scatter(rows, dest)


def _expert_kernel(blk_ref, exp_ref, lo_ref, hi_ref, xs_ref, wg_ref, wu_ref, wd_ref, ys_ref,
                   wgu_s, wd_s):
    i = pl.program_id(0)
    prev = jnp.maximum(i - 1, 0)

    @pl.when(jnp.logical_or(i == 0, exp_ref[i] != exp_ref[prev]))
    def _():
        wgu_s[:, :EXPERT_DIM] = wg_ref[...].astype(BF16)
        wgu_s[:, EXPERT_DIM:] = wu_ref[...].astype(BF16)
        wd_s[...] = wd_ref[...].astype(BF16)

    first = jnp.logical_or(i == 0, blk_ref[i] != blk_ref[prev])
    lo = lo_ref[i]
    hi = hi_ref[i]

    def sub_block(r0):
        rows = slice(r0, r0 + EXPERT_SUB_ROWS)
        w = xs_ref[rows, :]
        xlo = lax.bitcast_convert_type(w.astype(jnp.int16), BF16)
        xhi = lax.bitcast_convert_type(lax.shift_right_logical(w, 16).astype(jnp.int16), BF16)
        gu = _dot(xlo, wgu_s[:HALF, :]) + _dot(xhi, wgu_s[HALF:, :])
        g = gu[:, :EXPERT_DIM]
        hb = (g * jax.nn.sigmoid(g) * gu[:, EXPERT_DIM:]).astype(BF16)
        y = _dot(hb, wd_s[...])
        packed = pltpu.pack_elementwise([y[:, :HALF], y[:, HALF:]], packed_dtype=BF16)
        row = r0 + lax.broadcasted_iota(I32, (EXPERT_SUB_ROWS, 1), 0)
        mine = jnp.logical_and(row >= lo, row < hi)
        kept = jnp.where(first, 0, ys_ref[rows, :])
        ys_ref[rows, :] = jnp.where(mine, packed, kept)

    for r0 in range(0, EXPERT_ROWS, EXPERT_SUB_ROWS):
        pl.when(jnp.logical_and(lo < r0 + EXPERT_SUB_ROWS, hi > r0))(functools.partial(sub_block, r0))


def _expert_items(counts, n_rows):
    n_blocks = n_rows // EXPERT_ROWS
    n_items = n_blocks + N_EXPERTS - 1
    end = jnp.cumsum(counts)
    start = end - counts
    first_blk = start // EXPERT_ROWS
    n_blk = jnp.where(counts > 0, (end - 1) // EXPERT_ROWS - first_blk + 1, 0)
    item_end = jnp.cumsum(n_blk)
    item_start = item_end - n_blk
    slot = jnp.arange(n_items, dtype=I32)
    e = jnp.minimum(jnp.sum((item_end[None, :] <= slot[:, None]).astype(I32), axis=1), N_EXPERTS - 1)
    onehot = (e[:, None] == jnp.arange(N_EXPERTS, dtype=I32)[None, :]).astype(I32)
    pick = lambda v: jnp.sum(onehot * v[None, :], axis=1)
    valid = slot < item_end[-1]
    blk = jnp.where(valid, pick(first_blk) + slot - pick(item_start), n_blocks - 1)
    lo = jnp.clip(pick(start) - blk * EXPERT_ROWS, 0, EXPERT_ROWS)
    hi = jnp.clip(pick(end) - blk * EXPERT_ROWS, 0, EXPERT_ROWS)
    last_e = jnp.max(jnp.where(counts > 0, jnp.arange(N_EXPERTS, dtype=I32), 0))
    e = jnp.where(valid, e, last_e)
    hi = jnp.where(valid, hi, 0)
    lo = jnp.where(valid, lo, 0)
    return blk.astype(I32), e.astype(I32), lo.astype(I32), hi.astype(I32)


def _experts(xs, items, w_gate, w_up, w_down, layer):
    n_rows = xs.shape[0]
    n_items = items[0].shape[0]
    wmap = lambda i, b, e, lo, hi: (layer, e[i], 0, 0)
    grid_spec = pltpu.PrefetchScalarGridSpec(
        num_scalar_prefetch=4, grid=(n_items,),
        in_specs=[pl.BlockSpec((EXPERT_ROWS, HALF), lambda i, b, e, lo, hi: (b[i], 0)),
                  pl.BlockSpec((None, None, D_MODEL, EXPERT_DIM), wmap),
                  pl.BlockSpec((None, None, D_MODEL, EXPERT_DIM), wmap),
                  pl.BlockSpec((None, None, EXPERT_DIM, D_MODEL), wmap)],
        out_specs=pl.BlockSpec((EXPERT_ROWS, HALF), lambda i, b, e, lo, hi: (b[i], 0)),
        scratch_shapes=[pltpu.VMEM((D_MODEL, 2 * EXPERT_DIM), BF16),
                        pltpu.VMEM((EXPERT_DIM, D_MODEL), BF16)])
    return pl.pallas_call(
        _expert_kernel, grid_spec=grid_spec,
        out_shape=jax.ShapeDtypeStruct((n_rows, HALF), I32),
        compiler_params=_cparams("arbitrary"), name="experts")(*items, xs, w_gate, w_up, w_down)


def _moe_out_kernel(x_ref, yg_ref, wt_ref, sgu_ref, sd_ref, g_ref, b_ref, *rest):
    o_ref = rest[-1]
    x = x_ref[...]
    wt = wt_ref[...]
    lo = jnp.zeros((x.shape[0], HALF), F32)
    hi = jnp.zeros((x.shape[0], HALF), F32)
    for k in range(TOP_K):
        w = yg_ref[k]
        wk = wt[:, k:k + 1]
        lo = lo + wk * _unpack_lo(w)
        hi = hi + wk * _unpack_hi(w)
    gu = _dot(x.astype(BF16), sgu_ref[...])
    g = gu[:, :EXPERT_DIM]
    hs = (g * jax.nn.sigmoid(g) * gu[:, EXPERT_DIM:]).astype(BF16)
    ffn = jnp.concatenate([lo, hi], axis=1) + _dot(hs, sd_ref[...])
    o_ref[...] = _ln(DEEPNORM_ALPHA * x + ffn, g_ref[...], b_ref[...])


def _moe_out(x, yg, wts, sh_gate, sh_up, sh_down, g, b, first_tile, partial_out):
    T, D = x.shape
    tm = ROW_TILE
    n_tiles = yg.shape[1] // tm
    xrow = pl.BlockSpec((tm, D), lambda i: (i + first_tile, 0))
    full = lambda a: pl.BlockSpec(a.shape, lambda i: (0,) * a.ndim)
    sgu = jnp.concatenate([sh_gate, sh_up], axis=1).astype(BF16)
    sd = sh_down.astype(BF16)
    g = g.reshape(1, D)
    b = b.reshape(1, D)
    args = [x, yg, wts, sgu, sd, g, b]
    in_specs = [xrow, pl.BlockSpec((TOP_K, tm, HALF), lambda i: (0, i, 0)),
                pl.BlockSpec((tm, TOP_K), lambda i: (i, 0)), full(sgu), full(sd), full(g), full(b)]
    aliases = {}
    if partial_out is not None:
        args.append(partial_out)
        in_specs.append(pl.BlockSpec(memory_space=pl.ANY))
        aliases = {len(args) - 1: 0}
    return pl.pallas_call(
        _moe_out_kernel, grid=(n_tiles,), in_specs=in_specs,
        out_specs=xrow, out_shape=jax.ShapeDtypeStruct((T, D), F32),
        input_output_aliases=aliases,
        compiler_params=_cparams("parallel"), name="moe_out")(*args)


def _moe(x, router_w, router_b, w_gate, w_up, w_down, layer, sh_gate, sh_up, sh_down, g, b):
    T = x.shape[0]
    tiles = T // ROW_TILE // MOE_TOKEN_GROUPS
    tg = tiles * ROW_TILE
    out = None
    for grp in range(MOE_TOKEN_GROUPS):
        eidx, wts, pos, cnt, xp = _router(x, router_w, router_b, grp * tg, tg)
        counts = cnt[:, 0]
        seg_start = (jnp.cumsum(counts) - counts).astype(I32)
        dest = _dest_rows(eidx, pos, seg_start)
        xs = _scatter_rows(xp, dest)
        ys = _experts(xs, _expert_items(counts, tg * TOP_K), w_gate, w_up, w_down, layer)
        yg = _gather_rows(ys, dest.reshape(tg * TOP_K)).reshape(TOP_K, tg, HALF)
        out = _moe_out(x, yg, wts.T, sh_gate, sh_up, sh_down, g, b, grp * tiles, out)
    return out


def kernel(x, ln_in_g, ln_in_b, e_w_in, e_w_fourier, e_q_gain, e_k_gain, e_w_out, o_w_in, o_b_in, o_v_ln_g, o_v_ln_b, o_w_spatial, o_b_spatial, o_w_out, ln_mix_g, ln_mix_b, ln_ffn_g, ln_ffn_b, router_w, router_b, exp_w_gate, exp_w_up, exp_w_down, sh_w_gate, sh_w_up, sh_w_down):
    B, S, D = x.shape
    Bc = B // BATCH_CHAINS
    T = Bc * S
    hs = [_layer_norm(xc.reshape(T, D), ln_in_g, ln_in_b) for xc in jnp.split(x, BATCH_CHAINS, axis=0)]
    for i in range(DEPTH):
        j = i // 2
        for c, h in enumerate(hs):
            if i % 2 == 0:
                a, q, kt, v2 = _even_in(h, e_w_in[j], e_q_gain[j], e_k_gain[j], Bc, S)
                a_out = _fourier(a, e_w_fourier[j], Bc, S)
                attn = _attention(q, kt, v2, Bc, S)
                h = _even_out(a_out, attn, e_w_out[j], h, ln_mix_g[i], ln_mix_b[i])
            else:
                h = _odd_mixer(h, o_w_in[j], o_b_in[j], o_v_ln_g[j], o_v_ln_b[j], o_w_spatial[j],
                               o_b_spatial[j], o_w_out[j], ln_mix_g[i], ln_mix_b[i])
            hs[c] = _moe(h, router_w[i], router_b[i], exp_w_gate, exp_w_up, exp_w_down, i,
                         sh_w_gate[i], sh_w_up[i], sh_w_down[i], ln_ffn_g[i], ln_ffn_b[i])
    return jnp.concatenate(hs, axis=0).reshape(B, S, D)
```

```python
import functools
import math

import numpy as np
import jax
import jax.numpy as jnp
from jax import lax
from jax.experimental import pallas as pl
from jax.experimental.pallas import tpu as pltpu
from jax.experimental.pallas import tpu_sc as plsc

F32 = jnp.float32
BF16 = jnp.bfloat16
I32 = jnp.int32

D_MODEL = 1024
DEPTH = 4
GRID_W = 64
N_FGROUPS = 4
FGROUP_DIM = 128
F_WIDTH = N_FGROUPS * FGROUP_DIM
N_HEADS = 8
N_KV_HEADS = 2
HEAD_DIM = 64
Q_GROUP = N_HEADS // N_KV_HEADS
Q_WIDTH = N_HEADS * HEAD_DIM
KV_WIDTH = N_KV_HEADS * HEAD_DIM
ROPE_THETA = 10000.0
ROPE_PAIRS = HEAD_DIM // 4
EVEN_IN_WIDTH = F_WIDTH + Q_WIDTH + 2 * KV_WIDTH
CHUNK = 128
N_CGROUPS = 8
CGROUP_DIM = D_MODEL // N_CGROUPS
C_WIDTH = N_CGROUPS * CGROUP_DIM
N_EXPERTS = 64
EXPERT_DIM = 256
TOP_K = 8
N_EXPERT_GROUPS = 8
GROUP_SIZE = N_EXPERTS // N_EXPERT_GROUPS
TOPK_GROUPS = 4
ROUTE_SCALE = 2.5
LN_EPS = 1e-5
QK_EPS = 1e-6
DEEPNORM_ALPHA = (2 * DEPTH) ** 0.25

VMEM_LIMIT_BYTES = 56 * 1024 * 1024
ROW_TILE = 512
DFT_N1 = 64
DFT_KRON = 4
DFT_PITCH_PAD = 8
ROUTER_TILE = 1024
EXPERT_ROWS = 2048
EXPERT_SUB_ROWS = 512
HALF = D_MODEL // 2
SC_WORKERS = 32
SC_CHUNK = 128
BATCH_CHAINS = 1
MOE_TOKEN_GROUPS = 2
ATT_TQ = 256
ATT_TK = 512
ATT_BOUND_SLACK = 1.0 + 2.0 ** -7
ATT_MIN_ROW_SUM = 2.0 ** -80
NEG_INF = float("-inf")


def _cparams(*sem):
    return pltpu.CompilerParams(dimension_semantics=sem, vmem_limit_bytes=VMEM_LIMIT_BYTES)


def _ln(x, g, b):
    mu = jnp.mean(x, axis=-1, keepdims=True)
    xc = x - mu
    var = jnp.mean(xc * xc, axis=-1, keepdims=True)
    return xc * lax.rsqrt(var + LN_EPS) * g + b


def _dot(a, b):
    return jnp.dot(a, b, preferred_element_type=F32)


def _pack_halves(y):
    lo = lax.bitcast_convert_type(y[:, :HALF].astype(BF16).astype(F32), I32)
    hi = lax.bitcast_convert_type(y[:, HALF:].astype(BF16).astype(F32), I32)
    return lax.shift_right_logical(lo, 16) | (hi & jnp.int32(-65536))


def _unpack_lo(w):
    return lax.bitcast_convert_type(lax.shift_left(w, 16), F32)


def _unpack_hi(w):
    return lax.bitcast_convert_type(w & jnp.int32(-65536), F32)


def _ln_kernel(x_ref, g_ref, b_ref, o_ref):
    o_ref[...] = _ln(x_ref[...], g_ref[...], b_ref[...])


def _layer_norm(x, g, b):
    T, D = x.shape
    row = pl.BlockSpec((ROW_TILE, D), lambda i: (i, 0))
    vec = pl.BlockSpec((1, D), lambda i: (0, 0))
    return pl.pallas_call(
        _ln_kernel, grid=(T // ROW_TILE,), in_specs=[row, vec, vec], out_specs=row,
        out_shape=jax.ShapeDtypeStruct((T, D), F32), compiler_params=_cparams("parallel"),
        name="ln_in")(x, g.reshape(1, D), b.reshape(1, D))


def _even_in_kernel(x_ref, w_ref, qm_ref, km_ref, qg_ref, kg_ref, cos_ref, sin_ref,
                    a_ref, q_ref, kt_ref, v_ref):
    tm = x_ref.shape[0]
    h = _dot(x_ref[...].astype(BF16), w_ref[...])
    a_ref[...] = h[:, :F_WIDTH].astype(BF16)
    q = h[:, F_WIDTH:F_WIDTH + Q_WIDTH]
    k = h[:, F_WIDTH + Q_WIDTH:F_WIDTH + Q_WIDTH + KV_WIDTH]
    v = h[:, F_WIDTH + Q_WIDTH + KV_WIDTH:]
    cos = cos_ref[...]
    sin = sin_ref[...]
    lane = lax.broadcasted_iota(I32, (tm, 128), 1)
    first_of_pair = (lane & ROPE_PAIRS) == 0

    def mean_sq(xf, m_ref):
        sq = xf * xf
        hi = sq.astype(BF16)
        lo = (sq - hi.astype(F32)).astype(BF16)
        return _dot(hi, m_ref[...]) + _dot(lo, m_ref[...])

    def rope(xn):
        sw = jnp.where(first_of_pair, pltpu.roll(xn, 128 - ROPE_PAIRS, 1), pltpu.roll(xn, ROPE_PAIRS, 1))
        return xn * cos + sw * sin

    qn = q * lax.rsqrt(mean_sq(q, qm_ref) + QK_EPS) * qg_ref[...]
    scale = math.log2(math.e) / math.sqrt(HEAD_DIM)
    for c in range(Q_WIDTH // 128):
        q_ref[:, c * 128:(c + 1) * 128] = (rope(qn[:, c * 128:(c + 1) * 128]) * scale).astype(BF16)
    kn = k * lax.rsqrt(mean_sq(k, km_ref) + QK_EPS) * kg_ref[...]
    kt_ref[...] = rope(kn).T.astype(BF16)
    ones_col = jnp.where(lane == HEAD_DIM, 1.0, 0.0)
    low = lane < HEAD_DIM
    v_ref[0] = jnp.where(low, v, ones_col).astype(BF16)
    v_ref[1] = jnp.where(low, pltpu.roll(v, HEAD_DIM, 1), ones_col).astype(BF16)


def _rope_tables(S):
    rows = S // GRID_W
    t = np.arange(S)
    inv = ROPE_THETA ** (-np.arange(ROPE_PAIRS, dtype=np.float64) / ROPE_PAIRS)
    ang_r = (t // GRID_W)[:, None] * inv
    ang_c = (t % GRID_W)[:, None] * inv
    del rows
    cos = np.concatenate([np.cos(ang_r), np.cos(ang_r), np.cos(ang_c), np.cos(ang_c)], axis=1)
    sin = np.concatenate([-np.sin(ang_r), np.sin(ang_r), -np.sin(ang_c), np.sin(ang_c)], axis=1)
    return (jnp.asarray(np.tile(cos, (1, 2)), F32), jnp.asarray(np.tile(sin, (1, 2)), F32))


def _head_mean_matrix(width):
    m = np.kron(np.eye(width // HEAD_DIM), np.full((HEAD_DIM, HEAD_DIM), 1.0 / HEAD_DIM))
    return jnp.asarray(m, BF16)


def _even_in(x, w_in, q_gain, k_gain, B, S):
    T, D = x.shape
    tm = ROW_TILE
    ns = S // tm
    cos, sin = _rope_tables(S)
    row = lambda w: pl.BlockSpec((tm, w), lambda i: (i, 0))
    full = lambda a: pl.BlockSpec(a.shape, lambda i: (0,) * a.ndim)
    tab = pl.BlockSpec((tm, 128), lambda i: (i % ns, 0))
    w = w_in.astype(BF16)
    qm = _head_mean_matrix(Q_WIDTH)
    km = _head_mean_matrix(KV_WIDTH)
    qg = jnp.tile(q_gain.astype(F32), N_HEADS).reshape(1, Q_WIDTH)
    kg = jnp.tile(k_gain.astype(F32), N_KV_HEADS).reshape(1, KV_WIDTH)
    return pl.pallas_call(
        _even_in_kernel, grid=(T // tm,),
        in_specs=[row(D), full(w), full(qm), full(km), full(qg), full(kg), tab, tab],
        out_specs=[row(F_WIDTH), row(Q_WIDTH),
                   pl.BlockSpec((None, KV_WIDTH, tm), lambda i: (i // ns, 0, i % ns)),
                   pl.BlockSpec((N_KV_HEADS, tm, 128), lambda i: (0, i, 0))],
        out_shape=[jax.ShapeDtypeStruct((T, F_WIDTH), BF16),
                   jax.ShapeDtypeStruct((T, Q_WIDTH), BF16),
                   jax.ShapeDtypeStruct((B, KV_WIDTH, S), BF16),
                   jax.ShapeDtypeStruct((N_KV_HEADS, T, 128), BF16)],
        compiler_params=_cparams("parallel"), name="even_in")(x, w, qm, km, qg, kg, cos, sin)


def _fourier_kernel(a_ref, dftc_ref, taba_ref, kc_ref, ks_ref, wf_ref, o_ref,
                    zr_ref, zi_ref, ur_ref, ui_ref, y_ref):
    S = a_ref.shape[0]
    n1_count = DFT_N1
    n2_count = S // DFT_N1
    pz = n1_count + DFT_PITCH_PAD
    pu = n2_count + DFT_PITCH_PAD
    blk = DFT_KRON * DFT_N1
    scale = 1.0 / math.sqrt(S * FGROUP_DIM)

    def channel_dft(j, carry):
        zz = _dot(a_ref[pl.ds(pl.multiple_of(j * blk, blk), blk), :], dftc_ref[...])
        for q in range(DFT_KRON):
            dst = pl.ds(pl.multiple_of((j * DFT_KRON + q) * pz, 8), n1_count)
            zr_ref[dst, :] = zz[q * n1_count:(q + 1) * n1_count, :FGROUP_DIM]
            zi_ref[dst, :] = zz[q * n1_count:(q + 1) * n1_count, FGROUP_DIM:]
        return carry

    lax.fori_loop(0, S // blk, channel_dft, 0, unroll=2)

    def stage_a(n1, carry):
        src = pl.ds(n1, n2_count, stride=pz)
        zn = jnp.concatenate([zr_ref[src, :], zi_ref[src, :]], axis=1).astype(BF16)
        r = _dot(taba_ref[n1], zn)
        dst = pl.ds(pl.multiple_of(n1 * pu, 8), n2_count)
        ur_ref[dst, :] = r[:n2_count, :FGROUP_DIM] + r[n2_count:, FGROUP_DIM:]
        ui_ref[dst, :] = r[:n2_count, FGROUP_DIM:] - r[n2_count:, :FGROUP_DIM]
        return carry

    lax.fori_loop(0, n1_count, stage_a, 0, unroll=4)

    def stage_b(j, carry):
        srcs = [pl.ds(j * DFT_KRON + q, n1_count, stride=pu) for q in range(DFT_KRON)]
        ur = jnp.concatenate([ur_ref[s, :] for s in srcs], axis=0).astype(BF16)
        ui = jnp.concatenate([ui_ref[s, :] for s in srcs], axis=0).astype(BF16)
        re = _dot(kc_ref[...], ur) + _dot(ks_ref[...], ui)
        out = _dot((re * scale).astype(BF16), wf_ref[...])
        for q in range(DFT_KRON):
            y_ref[srcs[q], :] = out[q * n1_count:(q + 1) * n1_count]
        return carry

    lax.fori_loop(0, S // blk, stage_b, 0, unroll=4)

    def compact(k1, carry):
        o_ref[pl.ds(pl.multiple_of(k1 * n2_count, n2_count), n2_count), :] = (
            y_ref[pl.ds(pl.multiple_of(k1 * pu, 8), n2_count), :].astype(BF16))
        return carry

    lax.fori_loop(0, n1_count, compact, 0)


def _dft_tables(S):
    n1c, n2c = DFT_N1, S // DFT_N1
    c = np.arange(FGROUP_DIM)
    ang = 2 * np.pi * np.outer(c, c) / FGROUP_DIM
    dftc = np.concatenate([np.cos(ang), -np.sin(ang)], axis=1)
    n1 = np.arange(n1c)[:, None, None]
    k2 = np.arange(n2c)[None, :, None]
    n2 = np.arange(n2c)[None, None, :]
    th = 2 * np.pi * (n2 * k2 / n2c + n1 * k2 / S)
    taba = np.concatenate([np.cos(th), np.sin(th)], axis=1)
    k1 = np.arange(n1c)
    g = 2 * np.pi * np.outer(k1, k1) / n1c
    eye = np.eye(DFT_KRON)
    kc = np.kron(eye, np.cos(g))
    ks = np.kron(eye, np.sin(g))
    return tuple(jnp.asarray(t, BF16) for t in (dftc, taba, kc, ks))


def _fourier(a, w_fourier, B, S):
    T = a.shape[0]
    dftc, taba, kc, ks = _dft_tables(S)
    full = lambda t: pl.BlockSpec(t.shape, lambda b, g: (0,) * t.ndim)
    blk = pl.BlockSpec((S, FGROUP_DIM), lambda b, g: (b, g))
    return pl.pallas_call(
        _fourier_kernel, grid=(B, N_FGROUPS),
        in_specs=[blk, full(dftc), full(taba), full(kc), full(ks),
                  pl.BlockSpec((None, FGROUP_DIM, FGROUP_DIM), lambda b, g: (g, 0, 0))],
        out_specs=blk,
        out_shape=jax.ShapeDtypeStruct((T, F_WIDTH), BF16),
        scratch_shapes=(
            [pltpu.VMEM((S // DFT_N1 * (DFT_N1 + DFT_PITCH_PAD), FGROUP_DIM), F32)] * 2
            + [pltpu.VMEM((DFT_N1 * (S // DFT_N1 + DFT_PITCH_PAD), FGROUP_DIM), F32)] * 3),
        compiler_params=_cparams("parallel", "parallel"), name="fourier")(
            a, dftc, taba, kc, ks, w_fourier.astype(BF16))


def _attn_kernel(q_ref, kt_ref, v_ref, o_ref, qs_ref, kmax_ref, acc_ref, m_ref, s0_ref, s1_ref):
    tq = q_ref.shape[0]
    n_keys = kt_ref.shape[1]
    tk = min(ATT_TK, n_keys)
    n_chunks = n_keys // tk
    assert n_chunks % 2 == 0 and n_chunks * tk == n_keys

    @pl.when(pl.program_id(2) == 0)
    def _():
        def body(c, best):
            k = kt_ref[:, pl.ds(pl.multiple_of(c * tk, tk), tk)].astype(F32)
            return jnp.maximum(best, jnp.sum(k * k, axis=0, keepdims=True))
        best = lax.fori_loop(0, n_chunks, body, jnp.zeros((1, tk), F32))
        kmax_ref[...] = jnp.broadcast_to(jnp.sqrt(jnp.max(best, axis=1, keepdims=True)), kmax_ref.shape)

    for g in range(Q_GROUP):
        qs_ref[g * tq:(g + 1) * tq, :] = q_ref[:, g * HEAD_DIM:(g + 1) * HEAD_DIM]
    qf = qs_ref[...].astype(F32)
    bound = jnp.sqrt(jnp.sum(qf * qf, axis=1, keepdims=True)) * kmax_ref[0:1, 0:1] * ATT_BOUND_SLACK

    def chunk(c):
        cols = pl.ds(pl.multiple_of(c * tk, tk), tk)
        return _dot(qs_ref[...], kt_ref[:, cols]), v_ref[cols, :]

    def scores(c):
        return _dot(qs_ref[...], kt_ref[:, pl.ds(pl.multiple_of(c * tk, tk), tk)])

    def weighted(s_buf, c):
        v = v_ref[pl.ds(pl.multiple_of(c * tk, tk), tk), :]
        acc_ref[...] += _dot(jnp.exp2(s_buf[...] - bound).astype(BF16), v)

    def fast(c2, carry):
        c = 2 * c2
        s1_ref[...] = scores(c + 1)
        weighted(s0_ref, c)
        s0_ref[...] = scores(jnp.minimum(c + 2, n_chunks - 1))
        weighted(s1_ref, c + 1)
        return carry

    acc_ref[...] = jnp.zeros(acc_ref.shape, F32)
    s0_ref[...] = scores(0)
    lax.fori_loop(0, n_chunks // 2, fast, 0)
    underflow = jnp.min(acc_ref[:, HEAD_DIM:HEAD_DIM + 1]) < ATT_MIN_ROW_SUM

    @pl.when(underflow)
    def _():
        def safe(c, carry):
            s, v = chunk(c)
            m_old = m_ref[...]
            m_new = jnp.maximum(m_old, jnp.max(s, axis=1, keepdims=True))
            acc_ref[...] = jnp.exp2(m_old - m_new) * acc_ref[...] + _dot(jnp.exp2(s - m_new).astype(BF16), v)
            m_ref[...] = m_new
            return carry

        m_ref[...] = jnp.full(m_ref.shape, NEG_INF, F32)
        acc_ref[...] = jnp.zeros(acc_ref.shape, F32)
        lax.fori_loop(0, n_chunks, safe, 0)

    acc = acc_ref[...]
    o = acc[:, :HEAD_DIM] / acc[:, HEAD_DIM:HEAD_DIM + 1]
    o_ref[...] = jnp.concatenate([o[g * tq:(g + 1) * tq] for g in range(Q_GROUP)], axis=1).astype(BF16)


def _attention(q, kt, v2, B, S):
    T = q.shape[0]
    tq = ATT_TQ
    nq = S // tq
    gw = Q_GROUP * HEAD_DIM
    rows = Q_GROUP * tq
    return pl.pallas_call(
        _attn_kernel, grid=(B, N_KV_HEADS, nq),
        in_specs=[pl.BlockSpec((tq, gw), lambda b, h, i: (b * nq + i, h)),
                  pl.BlockSpec((None, HEAD_DIM, S), lambda b, h, i: (b, h, 0)),
                  pl.BlockSpec((None, None, S, 128), lambda b, h, i: (h, b, 0, 0))],
        out_specs=pl.BlockSpec((tq, gw), lambda b, h, i: (b * nq + i, h)),
        out_shape=jax.ShapeDtypeStruct((T, Q_WIDTH), BF16),
        scratch_shapes=[pltpu.VMEM((rows, HEAD_DIM), BF16), pltpu.VMEM((8, 128), F32),
                        pltpu.VMEM((rows, 128), F32), pltpu.VMEM((rows, 1), F32),
                        pltpu.VMEM((rows, min(ATT_TK, S)), F32), pltpu.VMEM((rows, min(ATT_TK, S)), F32)],
        compiler_params=_cparams("parallel", "parallel", "arbitrary"),
        name="attention")(q, kt, v2.reshape(N_KV_HEADS, B, S, 128))


def _even_out_kernel(a_ref, t_ref, wa_ref, wt_ref, x_ref, g_ref, b_ref, o_ref):
    mix = _dot(a_ref[...], wa_ref[...]) + _dot(t_ref[...], wt_ref[...])
    o_ref[...] = _ln(DEEPNORM_ALPHA * x_ref[...] + mix, g_ref[...], b_ref[...])


def _even_out(a_out, attn, w_out, x, g, b):
    T, D = x.shape
    tm = ROW_TILE
    row = lambda w: pl.BlockSpec((tm, w), lambda i: (i, 0))
    full = lambda a: pl.BlockSpec(a.shape, lambda i: (0,) * a.ndim)
    wa = w_out[:F_WIDTH].astype(BF16)
    wt = w_out[F_WIDTH:].astype(BF16)
    g = g.reshape(1, D)
    b = b.reshape(1, D)
    return pl.pallas_call(
        _even_out_kernel, grid=(T // tm,),
        in_specs=[row(F_WIDTH), row(Q_WIDTH), full(wa), full(wt), row(D), full(g), full(b)],
        out_specs=row(D), out_shape=jax.ShapeDtypeStruct((T, D), F32),
        compiler_params=_cparams("parallel"), name="even_out")(a_out, attn, wa, wt, x, g, b)


def _odd_kernel(x_ref, wi_ref, bi_ref, vg_ref, vb_ref, ws_ref, bs_ref, wo_ref, g_ref, b_ref, o_ref,
                gate_ref):
    tm = x_ref.shape[0]
    x = x_ref[...]
    h = _dot(x.astype(BF16), wi_ref[...]) + bi_ref[...]
    h = 0.5 * h * (1.0 + lax.erf(h * (1.0 / math.sqrt(2.0))))
    u = h[:, :C_WIDTH]
    v = _ln(h[:, C_WIDTH:], vg_ref[...], vb_ref[...]).astype(BF16)
    for c in range(tm // CHUNK):
        r0 = c * CHUNK
        for gi in range(N_CGROUPS):
            l0 = gi * CGROUP_DIM
            sv = _dot(ws_ref[gi], v[r0:r0 + CHUNK, l0:l0 + CGROUP_DIM]) + bs_ref[gi]
            gate_ref[r0:r0 + CHUNK, l0:l0 + CGROUP_DIM] = (
                u[r0:r0 + CHUNK, l0:l0 + CGROUP_DIM] * sv).astype(BF16)
    mix = _dot(gate_ref[...], wo_ref[...])
    o_ref[...] = _ln(DEEPNORM_ALPHA * x + mix, g_ref[...], b_ref[...])


def _odd_mixer(x, w_in, b_in, v_g, v_b, w_s, b_s, w_out, g, b):
    T, D = x.shape
    tm = ROW_TILE
    row = pl.BlockSpec((tm, D), lambda i: (i, 0))
    full = lambda a: pl.BlockSpec(a.shape, lambda i: (0,) * a.ndim)
    args = [w_in.astype(BF16), b_in.reshape(1, 2 * C_WIDTH), v_g.reshape(1, C_WIDTH),
            v_b.reshape(1, C_WIDTH), w_s.astype(BF16),
            jnp.broadcast_to(b_s[:, :, None], (N_CGROUPS, CHUNK, CGROUP_DIM)).astype(F32),
            w_out.astype(BF16), g.reshape(1, D), b.reshape(1, D)]
    return pl.pallas_call(
        _odd_kernel, grid=(T // tm,),
        in_specs=[row] + [full(a) for a in args],
        out_specs=row, out_shape=jax.ShapeDtypeStruct((T, D), F32),
        scratch_shapes=[pltpu.VMEM((tm, C_WIDTH), BF16)],
        compiler_params=_cparams("parallel"), name="odd_mixer")(x, *args)


def _router_kernel(x_ref, w_ref, rb_ref, tri_ref, eidx_ref, wts_ref, pos_ref, cnt_ref, xp_ref, run_ref):
    tm = x_ref.shape[0]
    i = pl.program_id(0)

    @pl.when(i == 0)
    def _():
        run_ref[...] = jnp.zeros(run_ref.shape, F32)

    x = x_ref[...]
    xp_ref[...] = _pack_halves(x)
    xh = x.astype(BF16)
    xl = (x - xh.astype(F32)).astype(BF16)
    nt = (((1,), (1,)), ((), ()))
    dg = lambda a, c: lax.dot_general(a, c, nt, preferred_element_type=F32)
    logits = dg(w_ref[0], xh) + dg(w_ref[0], xl) + dg(w_ref[1], xh)
    scores = jax.nn.sigmoid(logits)
    sel = scores + rb_ref[...]

    i8 = lax.broadcasted_iota(I32, (GROUP_SIZE, tm), 0)
    gsc_rows = []
    for gidx in range(N_EXPERT_GROUPS):
        sg = sel[gidx * GROUP_SIZE:(gidx + 1) * GROUP_SIZE, :]
        m1 = jnp.max(sg, axis=0, keepdims=True)
        f1 = jnp.min(jnp.where(sg == m1, i8, GROUP_SIZE), axis=0, keepdims=True)
        m2 = jnp.max(jnp.where(i8 == f1, NEG_INF, sg), axis=0, keepdims=True)
        gsc_rows.append(m1 + m2)
    gsc = jnp.concatenate(gsc_rows, axis=0)

    gsel = jnp.zeros(gsc.shape, F32)
    for _ in range(TOPK_GROUPS):
        m = jnp.max(gsc, axis=0, keepdims=True)
        f = jnp.min(jnp.where(gsc == m, i8, N_EXPERT_GROUPS), axis=0, keepdims=True)
        pick = i8 == f
        gsel = jnp.where(pick, 1.0, gsel)
        gsc = jnp.where(pick, NEG_INF, gsc)
    esel = jnp.concatenate(
        [jnp.broadcast_to(gsel[gidx:gidx + 1, :], (GROUP_SIZE, tm)) for gidx in range(N_EXPERT_GROUPS)],
        axis=0)

    cur = jnp.where(esel > 0.0, sel, NEG_INF)
    ei = lax.broadcasted_iota(I32, cur.shape, 0)
    idx_rows, sc_rows = [], []
    chosen = jnp.zeros(cur.shape, F32)
    for _ in range(TOP_K):
        m = jnp.max(cur, axis=0, keepdims=True)
        f = jnp.min(jnp.where(cur == m, ei, N_EXPERTS), axis=0, keepdims=True)
        pick = ei == f
        idx_rows.append(f)
        sc_rows.append(jnp.sum(jnp.where(pick, scores, 0.0), axis=0, keepdims=True))
        chosen = jnp.where(pick, 1.0, chosen)
        cur = jnp.where(pick, NEG_INF, cur)
    eidx = jnp.concatenate(idx_rows, axis=0)
    sc = jnp.concatenate(sc_rows, axis=0)
    eidx_ref[...] = eidx
    wts_ref[...] = sc / jnp.sum(sc, axis=0, keepdims=True) * ROUTE_SCALE

    before = _dot(chosen.astype(BF16), tri_ref[...]) + run_ref[...]
    pos_rows = [jnp.sum(jnp.where(ei == idx_rows[k], before, 0.0), axis=0, keepdims=True)
                for k in range(TOP_K)]
    pos_ref[...] = jnp.concatenate(pos_rows, axis=0).astype(I32)
    run_new = run_ref[...] + jnp.sum(chosen, axis=1, keepdims=True)
    run_ref[...] = run_new
    cnt_ref[...] = jnp.broadcast_to(run_new, cnt_ref.shape).astype(I32)


def _router(x, router_w, router_b, first_row, T):
    D = x.shape[1]
    tm = min(ROUTER_TILE, T)
    n_tiles = T // tm
    first_tile = first_row // tm
    assert n_tiles * tm == T and first_tile * tm == first_row
    wt = router_w.T.astype(F32)
    wh = wt.astype(BF16)
    wl = (wt - wh.astype(F32)).astype(BF16)
    w2 = jnp.stack([wh, wl])
    rb = router_b.astype(F32).reshape(N_EXPERTS, 1)
    tri = jnp.asarray(np.triu(np.ones((tm, tm)), 1), BF16)
    full = lambda a: pl.BlockSpec(a.shape, lambda i: (0,) * a.ndim)
    col = pl.BlockSpec((TOP_K, tm), lambda i: (0, i))
    return pl.pallas_call(
        _router_kernel, grid=(n_tiles,),
        in_specs=[pl.BlockSpec((tm, D), lambda i: (i + first_tile, 0)), full(w2), full(rb), full(tri)],
        out_specs=[col, col, col, pl.BlockSpec((N_EXPERTS, 128), lambda i: (0, 0)),
                   pl.BlockSpec((tm, HALF), lambda i: (i, 0))],
        out_shape=[jax.ShapeDtypeStruct((TOP_K, T), I32), jax.ShapeDtypeStruct((TOP_K, T), F32),
                   jax.ShapeDtypeStruct((TOP_K, T), I32), jax.ShapeDtypeStruct((N_EXPERTS, 128), I32),
                   jax.ShapeDtypeStruct((T, HALF), I32)],
        scratch_shapes=[pltpu.VMEM((N_EXPERTS, 1), F32)],
        compiler_params=_cparams("arbitrary"), name="router")(x, w2, rb, tri)


def _dest_kernel(start_ref, eidx_ref, pos_ref, o_ref):
    e = eidx_ref[...]
    acc = pos_ref[...]
    for j in range(N_EXPERTS):
        acc = acc + jnp.where(e == j, start_ref[j], 0)
    o_ref[...] = acc


def _dest_rows(eidx, pos, seg_start):
    K, T = eidx.shape
    tl = min(T, 2048)
    blk = pl.BlockSpec((K, tl), lambda i, s: (0, i))
    grid_spec = pltpu.PrefetchScalarGridSpec(
        num_scalar_prefetch=1, grid=(T // tl,), in_specs=[blk, blk], out_specs=blk)
    return pl.pallas_call(
        _dest_kernel, grid_spec=grid_spec, out_shape=jax.ShapeDtypeStruct((K, T), I32),
        compiler_params=_cparams("parallel"), name="dest_rows")(seg_start, eidx, pos)


def _gather_rows(table, idx):
    n_rows = idx.shape[0]
    width = table.shape[1]
    per_worker = n_rows // SC_WORKERS
    n_chunks = per_worker // SC_CHUNK
    assert per_worker * SC_WORKERS == n_rows and n_chunks * SC_CHUNK == per_worker
    mesh = plsc.VectorSubcoreMesh(core_axis_name="c", subcore_axis_name="s")

    @functools.partial(
        pl.kernel, mesh=mesh,
        out_type=jax.ShapeDtypeStruct((n_rows, width), table.dtype),
        scratch_types=[pltpu.VMEM((SC_CHUNK,), I32), pltpu.VMEM((SC_CHUNK, width), table.dtype),
                       pltpu.SemaphoreType.DMA])
    def gather(table_hbm, idx_hbm, out_hbm, idx_v, rows_v, sem):
        wid = lax.axis_index("s") * 2 + lax.axis_index("c")
        base = wid * per_worker

        @pl.loop(0, n_chunks)
        def _(j):
            off = base + j * SC_CHUNK
            pltpu.sync_copy(idx_hbm.at[pl.ds(off, SC_CHUNK)], idx_v)
            pltpu.async_copy(table_hbm.at[idx_v], rows_v, sem).wait()
            pltpu.sync_copy(rows_v, out_hbm.at[pl.ds(off, SC_CHUNK)])

    return gather(table, idx)


def _scatter_rows(rows, dest):
    n_tok, width = rows.shape
    n_dst = dest.shape[0]
    per_worker = n_tok // SC_WORKERS
    n_chunks = per_worker // SC_CHUNK
    assert per_worker * SC_WORKERS == n_tok and n_chunks * SC_CHUNK == per_worker
    mesh = plsc.VectorSubcoreMesh(core_axis_name="c", subcore_axis_name="s")

    @functools.partial(
        pl.kernel, mesh=mesh,
        out_type=jax.ShapeDtypeStruct((n_dst * n_tok, width), rows.dtype),
        scratch_types=[pltpu.VMEM((n_dst, SC_CHUNK), I32), pltpu.VMEM((SC_CHUNK, width), rows.dtype),
                       pltpu.SemaphoreType.DMA])
    def scatter(rows_hbm, dest_hbm, out_hbm, idx_v, rows_v, sem):
        wid = lax.axis_index("s") * 2 + lax.axis_index("c")
        base = wid * per_worker

        @pl.loop(0, n_chunks)
        def _(j):
            off = base + j * SC_CHUNK
            pltpu.sync_copy(dest_hbm.at[:, pl.ds(off, SC_CHUNK)], idx_v)
            pltpu.sync_copy(rows_hbm.at[pl.ds(off, SC_CHUNK)], rows_v)
            copies = [pltpu.async_copy(rows_v, out_hbm.at[idx_v.at[k]], sem) for k in range(n_dst)]
            for c in copies:
                c.wait()

    return scatter(rows, dest)


def _expert_kernel(blk_ref, exp_ref, lo_ref, hi_ref, slot_ref, nxt_ref, xs_ref, wg_hbm, wu_hbm, wd_hbm,
                   ys_ref, wg_buf, wu_buf, wd_buf, wgu_s, wd_s, sem, *, layer):
    i = pl.program_id(0)
    prev = jnp.maximum(i - 1, 0)

    def weight_copies(expert, slot):
        return [pltpu.make_async_copy(src.at[layer, expert], dst.at[slot], sem.at[slot])
                for src, dst in ((wg_hbm, wg_buf), (wu_hbm, wu_buf), (wd_hbm, wd_buf))]

    @pl.when(i == 0)
    def _():
        for c in weight_copies(exp_ref[0], slot_ref[0]):
            c.start()

    @pl.when(jnp.logical_or(i == 0, exp_ref[i] != exp_ref[prev]))
    def _():
        slot = slot_ref[i]
        for c in weight_copies(exp_ref[i], slot):
            c.wait()
        wgu_s[:, :EXPERT_DIM] = wg_buf[slot].astype(BF16)
        wgu_s[:, EXPERT_DIM:] = wu_buf[slot].astype(BF16)
        wd_s[...] = wd_buf[slot].astype(BF16)

        @pl.when(nxt_ref[i] >= 0)
        def _():
            for c in weight_copies(nxt_ref[i], 1 - slot):
                c.start()

    first = jnp.logical_or(i == 0, blk_ref[i] != blk_ref[prev])
    lo = lo_ref[i]
    hi = hi_ref[i]

    def sub_block(r0):
        rows = slice(r0, r0 + EXPERT_SUB_ROWS)
        w = xs_ref[rows, :]
        xlo = lax.bitcast_convert_type(w.astype(jnp.int16), BF16)
        xhi = lax.bitcast_convert_type(lax.shift_right_logical(w, 16).astype(jnp.int16), BF16)
        gu = _dot(xlo, wgu_s[:HALF, :]) + _dot(xhi, wgu_s[HALF:, :])
        g = gu[:, :EXPERT_DIM]
        hb = (g * jax.nn.sigmoid(g) * gu[:, EXPERT_DIM:]).astype(BF16)
        y = _dot(hb, wd_s[...])
        packed = pltpu.pack_elementwise([y[:, :HALF], y[:, HALF:]], packed_dtype=BF16)
        row = r0 + lax.broadcasted_iota(I32, (EXPERT_SUB_ROWS, 1), 0)
        mine = jnp.logical_and(row >= lo, row < hi)
        kept = jnp.where(first, 0, ys_ref[rows, :])
        ys_ref[rows, :] = jnp.where(mine, packed, kept)

    for r0 in range(0, EXPERT_ROWS, EXPERT_SUB_ROWS):
        pl.when(jnp.logical_and(lo < r0 + EXPERT_SUB_ROWS, hi > r0))(functools.partial(sub_block, r0))


def _expert_items(counts, n_rows):
    n_blocks = n_rows // EXPERT_ROWS
    n_items = n_blocks + N_EXPERTS - 1
    end = jnp.cumsum(counts)
    start = end - counts
    first_blk = start // EXPERT_ROWS
    n_blk = jnp.where(counts > 0, (end - 1) // EXPERT_ROWS - first_blk + 1, 0)
    item_end = jnp.cumsum(n_blk)
    item_start = item_end - n_blk
    slot = jnp.arange(n_items, dtype=I32)
    e = jnp.minimum(jnp.sum((item_end[None, :] <= slot[:, None]).astype(I32), axis=1), N_EXPERTS - 1)
    onehot = (e[:, None] == jnp.arange(N_EXPERTS, dtype=I32)[None, :]).astype(I32)
    pick = lambda v: jnp.sum(onehot * v[None, :], axis=1)
    valid = slot < item_end[-1]
    blk = jnp.where(valid, pick(first_blk) + slot - pick(item_start), n_blocks - 1)
    lo = jnp.clip(pick(start) - blk * EXPERT_ROWS, 0, EXPERT_ROWS)
    hi = jnp.clip(pick(end) - blk * EXPERT_ROWS, 0, EXPERT_ROWS)
    last_e = jnp.max(jnp.where(counts > 0, jnp.arange(N_EXPERTS, dtype=I32), 0))
    e = jnp.where(valid, e, last_e)
    hi = jnp.where(valid, hi, 0)
    lo = jnp.where(valid, lo, 0)
    change = jnp.concatenate([jnp.ones((1,), I32), (e[1:] != e[:-1]).astype(I32)])
    slot = (jnp.cumsum(change) - 1) % 2
    later = jnp.where(jnp.arange(N_EXPERTS, dtype=I32)[None, :] > e[:, None], counts[None, :] > 0, False)
    nxt = jnp.where(jnp.any(later, axis=1), jnp.argmax(later, axis=1), -1)
    return tuple(a.astype(I32) for a in (blk, e, lo, hi, slot, nxt))


def _experts(xs, items, w_gate, w_up, w_down, layer):
    n_rows = xs.shape[0]
    n_items = items[0].shape[0]
    blk_map = lambda i, b, *_: (b[i], 0)
    hbm = pl.BlockSpec(memory_space=pl.ANY)
    grid_spec = pltpu.PrefetchScalarGridSpec(
        num_scalar_prefetch=len(items), grid=(n_items,),
        in_specs=[pl.BlockSpec((EXPERT_ROWS, HALF), blk_map), hbm, hbm, hbm],
        out_specs=pl.BlockSpec((EXPERT_ROWS, HALF), blk_map),
        scratch_shapes=[pltpu.VMEM((2, D_MODEL, EXPERT_DIM), F32),
                        pltpu.VMEM((2, D_MODEL, EXPERT_DIM), F32),
                        pltpu.VMEM((2, EXPERT_DIM, D_MODEL), F32),
                        pltpu.VMEM((D_MODEL, 2 * EXPERT_DIM), BF16),
                        pltpu.VMEM((EXPERT_DIM, D_MODEL), BF16),
                        pltpu.SemaphoreType.DMA((2,))])
    return pl.pallas_call(
        functools.partial(_expert_kernel, layer=layer), grid_spec=grid_spec,
        out_shape=jax.ShapeDtypeStruct((n_rows, HALF), I32),
        compiler_params=_cparams("arbitrary"), name="experts")(*items, xs, w_gate, w_up, w_down)


def _moe_out_kernel(x_ref, yg_ref, wt_ref, sgu_ref, sd_ref, g_ref, b_ref, *rest):
    o_ref = rest[-1]
    x = x_ref[...]
    wt = wt_ref[...]
    lo = jnp.zeros((x.shape[0], HALF), F32)
    hi = jnp.zeros((x.shape[0], HALF), F32)
    for k in range(TOP_K):
        w = yg_ref[k]
        wk = wt[:, k:k + 1]
        lo = lo + wk * _unpack_lo(w)
        hi = hi + wk * _unpack_hi(w)
    gu = _dot(x.astype(BF16), sgu_ref[...])
    g = gu[:, :EXPERT_DIM]
    hs = (g * jax.nn.sigmoid(g) * gu[:, EXPERT_DIM:]).astype(BF16)
    ffn = jnp.concatenate([lo, hi], axis=1) + _dot(hs, sd_ref[...])
    o_ref[...] = _ln(DEEPNORM_ALPHA * x + ffn, g_ref[...], b_ref[...])


def _moe_out(x, yg, wts, sh_gate, sh_up, sh_down, g, b, first_tile, partial_out):
    T, D = x.shape
    tm = ROW_TILE
    n_tiles = yg.shape[1] // tm
    xrow = pl.BlockSpec((tm, D), lambda i: (i + first_tile, 0))
    full = lambda a: pl.BlockSpec(a.shape, lambda i: (0,) * a.ndim)
    sgu = jnp.concatenate([sh_gate, sh_up], axis=1).astype(BF16)
    sd = sh_down.astype(BF16)
    g = g.reshape(1, D)
    b = b.reshape(1, D)
    args = [x, yg, wts, sgu, sd, g, b]
    in_specs = [xrow, pl.BlockSpec((TOP_K, tm, HALF), lambda i: (0, i, 0)),
                pl.BlockSpec((tm, TOP_K), lambda i: (i, 0)), full(sgu), full(sd), full(g), full(b)]
    aliases = {}
    if partial_out is not None:
        args.append(partial_out)
        in_specs.append(pl.BlockSpec(memory_space=pl.ANY))
        aliases = {len(args) - 1: 0}
    return pl.pallas_call(
        _moe_out_kernel, grid=(n_tiles,), in_specs=in_specs,
        out_specs=xrow, out_shape=jax.ShapeDtypeStruct((T, D), F32),
        input_output_aliases=aliases,
        compiler_params=_cparams("parallel"), name="moe_out")(*args)


def _moe(x, router_w, router_b, w_gate, w_up, w_down, layer, sh_gate, sh_up, sh_down, g, b):
    T = x.shape[0]
    tiles = T // ROW_TILE // MOE_TOKEN_GROUPS
    tg = tiles * ROW_TILE
    out = None
    for grp in range(MOE_TOKEN_GROUPS):
        eidx, wts, pos, cnt, xp = _router(x, router_w, router_b, grp * tg, tg)
        counts = cnt[:, 0]
        seg_start = (jnp.cumsum(counts) - counts).astype(I32)
        dest = _dest_rows(eidx, pos, seg_start)
        xs = _scatter_rows(xp, dest)
        ys = _experts(xs, _expert_items(counts, tg * TOP_K), w_gate, w_up, w_down, layer)
        yg = _gather_rows(ys, dest.reshape(tg * TOP_K)).reshape(TOP_K, tg, HALF)
        out = _moe_out(x, yg, wts.T, sh_gate, sh_up, sh_down, g, b, grp * tiles, out)
    return out


def kernel(x, ln_in_g, ln_in_b, e_w_in, e_w_fourier, e_q_gain, e_k_gain, e_w_out, o_w_in, o_b_in, o_v_ln_g, o_v_ln_b, o_w_spatial, o_b_spatial, o_w_out, ln_mix_g, ln_mix_b, ln_ffn_g, ln_ffn_b, router_w, router_b, exp_w_gate, exp_w_up, exp_w_down, sh_w_gate, sh_w_up, sh_w_down):
    B, S, D = x.shape
    Bc = B // BATCH_CHAINS
    T = Bc * S
    hs = [_layer_norm(xc.reshape(T, D), ln_in_g, ln_in_b) for xc in jnp.split(x, BATCH_CHAINS, axis=0)]
    for i in range(DEPTH):
        j = i // 2
        for c, h in enumerate(hs):
            if i % 2 == 0:
                a, q, kt, v2 = _even_in(h, e_w_in[j], e_q_gain[j], e_k_gain[j], Bc, S)
                a_out = _fourier(a, e_w_fourier[j], Bc, S)
                attn = _attention(q, kt, v2, Bc, S)
                h = _even_out(a_out, attn, e_w_out[j], h, ln_mix_g[i], ln_mix_b[i])
            else:
                h = _odd_mixer(h, o_w_in[j], o_b_in[j], o_v_ln_g[j], o_v_ln_b[j], o_w_spatial[j],
                               o_b_spatial[j], o_w_out[j], ln_mix_g[i], ln_mix_b[i])
            hs[c] = _moe(h, router_w[i], router_b[i], exp_w_gate, exp_w_up, exp_w_down, i,
                         sh_w_gate[i], sh_w_up[i], sh_w_down[i], ln_ffn_g[i], ln_ffn_b[i])
    return jnp.concatenate(hs, axis=0).reshape(B, S, D)
```

```python
import functools
import math

import numpy as np
import jax
import jax.numpy as jnp
from jax import lax
from jax.experimental import pallas as pl
from jax.experimental.pallas import tpu as pltpu
from jax.experimental.pallas import tpu_sc as plsc

F32 = jnp.float32
BF16 = jnp.bfloat16
I32 = jnp.int32

D_MODEL = 1024
DEPTH = 4
GRID_W = 64
N_FGROUPS = 4
FGROUP_DIM = 128
F_WIDTH = N_FGROUPS * FGROUP_DIM
N_HEADS = 8
N_KV_HEADS = 2
HEAD_DIM = 64
Q_GROUP = N_HEADS // N_KV_HEADS
Q_WIDTH = N_HEADS * HEAD_DIM
KV_WIDTH = N_KV_HEADS * HEAD_DIM
ROPE_THETA = 10000.0
ROPE_PAIRS = HEAD_DIM // 4
EVEN_IN_WIDTH = F_WIDTH + Q_WIDTH + 2 * KV_WIDTH
CHUNK = 128
N_CGROUPS = 8
CGROUP_DIM = D_MODEL // N_CGROUPS
C_WIDTH = N_CGROUPS * CGROUP_DIM
N_EXPERTS = 64
EXPERT_DIM = 256
TOP_K = 8
N_EXPERT_GROUPS = 8
GROUP_SIZE = N_EXPERTS // N_EXPERT_GROUPS
TOPK_GROUPS = 4
ROUTE_SCALE = 2.5
LN_EPS = 1e-5
QK_EPS = 1e-6
DEEPNORM_ALPHA = (2 * DEPTH) ** 0.25

VMEM_LIMIT_BYTES = 56 * 1024 * 1024
ROW_TILE = 512
DFT_N1 = 64
DFT_KRON = 4
DFT_PITCH_PAD = 8
ROUTER_TILE = 1024
EXPERT_ROWS = 2048
EXPERT_XS_SLOTS = 3
EXPERT_SUB_ROWS = 512
HALF = D_MODEL // 2
SC_WORKERS = 32
SC_CHUNK = 128
BATCH_CHAINS = 1
MOE_TOKEN_GROUPS = 2
ATT_TQ = 256
ATT_TK = 512
ATT_BOUND_SLACK = 1.0 + 2.0 ** -7
ATT_MIN_ROW_SUM = 2.0 ** -80
NEG_INF = float("-inf")


def _cparams(*sem):
    return pltpu.CompilerParams(dimension_semantics=sem, vmem_limit_bytes=VMEM_LIMIT_BYTES)


def _ln(x, g, b):
    mu = jnp.mean(x, axis=-1, keepdims=True)
    xc = x - mu
    var = jnp.mean(xc * xc, axis=-1, keepdims=True)
    return xc * lax.rsqrt(var + LN_EPS) * g + b


def _dot(a, b):
    return jnp.dot(a, b, preferred_element_type=F32)


def _pack_halves(y):
    lo = lax.bitcast_convert_type(y[:, :HALF].astype(BF16).astype(F32), I32)
    hi = lax.bitcast_convert_type(y[:, HALF:].astype(BF16).astype(F32), I32)
    return lax.shift_right_logical(lo, 16) | (hi & jnp.int32(-65536))


def _unpack_lo(w):
    return lax.bitcast_convert_type(lax.shift_left(w, 16), F32)


def _unpack_hi(w):
    return lax.bitcast_convert_type(w & jnp.int32(-65536), F32)


def _ln_kernel(x_ref, g_ref, b_ref, o_ref):
    o_ref[...] = _ln(x_ref[...], g_ref[...], b_ref[...])


def _layer_norm(x, g, b):
    T, D = x.shape
    row = pl.BlockSpec((ROW_TILE, D), lambda i: (i, 0))
    vec = pl.BlockSpec((1, D), lambda i: (0, 0))
    return pl.pallas_call(
        _ln_kernel, grid=(T // ROW_TILE,), in_specs=[row, vec, vec], out_specs=row,
        out_shape=jax.ShapeDtypeStruct((T, D), F32), compiler_params=_cparams("parallel"),
        name="ln_in")(x, g.reshape(1, D), b.reshape(1, D))


def _even_in_kernel(x_ref, w_ref, qm_ref, km_ref, qg_ref, kg_ref, cos_ref, sin_ref,
                    a_ref, q_ref, kt_ref, v_ref):
    tm = x_ref.shape[0]
    h = _dot(x_ref[...].astype(BF16), w_ref[...])
    a_ref[...] = h[:, :F_WIDTH].astype(BF16)
    q = h[:, F_WIDTH:F_WIDTH + Q_WIDTH]
    k = h[:, F_WIDTH + Q_WIDTH:F_WIDTH + Q_WIDTH + KV_WIDTH]
    v = h[:, F_WIDTH + Q_WIDTH + KV_WIDTH:]
    cos = cos_ref[...]
    sin = sin_ref[...]
    lane = lax.broadcasted_iota(I32, (tm, 128), 1)
    first_of_pair = (lane & ROPE_PAIRS) == 0

    def mean_sq(xf, m_ref):
        sq = xf * xf
        hi = sq.astype(BF16)
        lo = (sq - hi.astype(F32)).astype(BF16)
        return _dot(hi, m_ref[...]) + _dot(lo, m_ref[...])

    def rope(xn):
        sw = jnp.where(first_of_pair, pltpu.roll(xn, 128 - ROPE_PAIRS, 1), pltpu.roll(xn, ROPE_PAIRS, 1))
        return xn * cos + sw * sin

    qn = q * lax.rsqrt(mean_sq(q, qm_ref) + QK_EPS) * qg_ref[...]
    scale = math.log2(math.e) / math.sqrt(HEAD_DIM)
    for c in range(Q_WIDTH // 128):
        q_ref[:, c * 128:(c + 1) * 128] = (rope(qn[:, c * 128:(c + 1) * 128]) * scale).astype(BF16)
    kn = k * lax.rsqrt(mean_sq(k, km_ref) + QK_EPS) * kg_ref[...]
    kt_ref[...] = rope(kn).T.astype(BF16)
    ones_col = jnp.where(lane == HEAD_DIM, 1.0, 0.0)
    low = lane < HEAD_DIM
    v_ref[0] = jnp.where(low, v, ones_col).astype(BF16)
    v_ref[1] = jnp.where(low, pltpu.roll(v, HEAD_DIM, 1), ones_col).astype(BF16)


def _rope_tables(S):
    rows = S // GRID_W
    t = np.arange(S)
    inv = ROPE_THETA ** (-np.arange(ROPE_PAIRS, dtype=np.float64) / ROPE_PAIRS)
    ang_r = (t // GRID_W)[:, None] * inv
    ang_c = (t % GRID_W)[:, None] * inv
    del rows
    cos = np.concatenate([np.cos(ang_r), np.cos(ang_r), np.cos(ang_c), np.cos(ang_c)], axis=1)
    sin = np.concatenate([-np.sin(ang_r), np.sin(ang_r), -np.sin(ang_c), np.sin(ang_c)], axis=1)
    return (jnp.asarray(np.tile(cos, (1, 2)), F32), jnp.asarray(np.tile(sin, (1, 2)), F32))


def _head_mean_matrix(width):
    m = np.kron(np.eye(width // HEAD_DIM), np.full((HEAD_DIM, HEAD_DIM), 1.0 / HEAD_DIM))
    return jnp.asarray(m, BF16)


def _even_in(x, w_in, q_gain, k_gain, B, S):
    T, D = x.shape
    tm = ROW_TILE
    ns = S // tm
    cos, sin = _rope_tables(S)
    row = lambda w: pl.BlockSpec((tm, w), lambda i: (i, 0))
    full = lambda a: pl.BlockSpec(a.shape, lambda i: (0,) * a.ndim)
    tab = pl.BlockSpec((tm, 128), lambda i: (i % ns, 0))
    w = w_in.astype(BF16)
    qm = _head_mean_matrix(Q_WIDTH)
    km = _head_mean_matrix(KV_WIDTH)
    qg = jnp.tile(q_gain.astype(F32), N_HEADS).reshape(1, Q_WIDTH)
    kg = jnp.tile(k_gain.astype(F32), N_KV_HEADS).reshape(1, KV_WIDTH)
    return pl.pallas_call(
        _even_in_kernel, grid=(T // tm,),
        in_specs=[row(D), full(w), full(qm), full(km), full(qg), full(kg), tab, tab],
        out_specs=[row(F_WIDTH), row(Q_WIDTH),
                   pl.BlockSpec((None, KV_WIDTH, tm), lambda i: (i // ns, 0, i % ns)),
                   pl.BlockSpec((N_KV_HEADS, tm, 128), lambda i: (0, i, 0))],
        out_shape=[jax.ShapeDtypeStruct((T, F_WIDTH), BF16),
                   jax.ShapeDtypeStruct((T, Q_WIDTH), BF16),
                   jax.ShapeDtypeStruct((B, KV_WIDTH, S), BF16),
                   jax.ShapeDtypeStruct((N_KV_HEADS, T, 128), BF16)],
        compiler_params=_cparams("parallel"), name="even_in")(x, w, qm, km, qg, kg, cos, sin)


def _fourier_kernel(a_ref, dftc_ref, taba_ref, kc_ref, ks_ref, wf_ref, o_ref,
                    zr_ref, zi_ref, ur_ref, ui_ref, y_ref):
    S = a_ref.shape[0]
    n1_count = DFT_N1
    n2_count = S // DFT_N1
    pz = n1_count + DFT_PITCH_PAD
    pu = n2_count + DFT_PITCH_PAD
    blk = DFT_KRON * DFT_N1
    scale = 1.0 / math.sqrt(S * FGROUP_DIM)

    def channel_dft(j, carry):
        zz = _dot(a_ref[pl.ds(pl.multiple_of(j * blk, blk), blk), :], dftc_ref[...])
        for q in range(DFT_KRON):
            dst = pl.ds(pl.multiple_of((j * DFT_KRON + q) * pz, 8), n1_count)
            zr_ref[dst, :] = zz[q * n1_count:(q + 1) * n1_count, :FGROUP_DIM]
            zi_ref[dst, :] = zz[q * n1_count:(q + 1) * n1_count, FGROUP_DIM:]
        return carry

    lax.fori_loop(0, S // blk, channel_dft, 0, unroll=2)

    def stage_a(n1, carry):
        src = pl.ds(n1, n2_count, stride=pz)
        zn = jnp.concatenate([zr_ref[src, :], zi_ref[src, :]], axis=1).astype(BF16)
        r = _dot(taba_ref[n1], zn)
        dst = pl.ds(pl.multiple_of(n1 * pu, 8), n2_count)
        ur_ref[dst, :] = r[:n2_count, :FGROUP_DIM] + r[n2_count:, FGROUP_DIM:]
        ui_ref[dst, :] = r[:n2_count, FGROUP_DIM:] - r[n2_count:, :FGROUP_DIM]
        return carry

    lax.fori_loop(0, n1_count, stage_a, 0, unroll=4)

    def stage_b(j, carry):
        srcs = [pl.ds(j * DFT_KRON + q, n1_count, stride=pu) for q in range(DFT_KRON)]
        ur = jnp.concatenate([ur_ref[s, :] for s in srcs], axis=0).astype(BF16)
        ui = jnp.concatenate([ui_ref[s, :] for s in srcs], axis=0).astype(BF16)
        re = _dot(kc_ref[...], ur) + _dot(ks_ref[...], ui)
        out = _dot((re * scale).astype(BF16), wf_ref[...])
        for q in range(DFT_KRON):
            y_ref[srcs[q], :] = out[q * n1_count:(q + 1) * n1_count]
        return carry

    lax.fori_loop(0, S // blk, stage_b, 0, unroll=4)

    def compact(k1, carry):
        o_ref[pl.ds(pl.multiple_of(k1 * n2_count, n2_count), n2_count), :] = (
            y_ref[pl.ds(pl.multiple_of(k1 * pu, 8), n2_count), :].astype(BF16))
        return carry

    lax.fori_loop(0, n1_count, compact, 0)


def _dft_tables(S):
    n1c, n2c = DFT_N1, S // DFT_N1
    c = np.arange(FGROUP_DIM)
    ang = 2 * np.pi * np.outer(c, c) / FGROUP_DIM
    dftc = np.concatenate([np.cos(ang), -np.sin(ang)], axis=1)
    n1 = np.arange(n1c)[:, None, None]
    k2 = np.arange(n2c)[None, :, None]
    n2 = np.arange(n2c)[None, None, :]
    th = 2 * np.pi * (n2 * k2 / n2c + n1 * k2 / S)
    taba = np.concatenate([np.cos(th), np.sin(th)], axis=1)
    k1 = np.arange(n1c)
    g = 2 * np.pi * np.outer(k1, k1) / n1c
    eye = np.eye(DFT_KRON)
    kc = np.kron(eye, np.cos(g))
    ks = np.kron(eye, np.sin(g))
    return tuple(jnp.asarray(t, BF16) for t in (dftc, taba, kc, ks))


def _fourier(a, w_fourier, B, S):
    T = a.shape[0]
    dftc, taba, kc, ks = _dft_tables(S)
    full = lambda t: pl.BlockSpec(t.shape, lambda b, g: (0,) * t.ndim)
    blk = pl.BlockSpec((S, FGROUP_DIM), lambda b, g: (b, g))
    return pl.pallas_call(
        _fourier_kernel, grid=(B, N_FGROUPS),
        in_specs=[blk, full(dftc), full(taba), full(kc), full(ks),
                  pl.BlockSpec((None, FGROUP_DIM, FGROUP_DIM), lambda b, g: (g, 0, 0))],
        out_specs=blk,
        out_shape=jax.ShapeDtypeStruct((T, F_WIDTH), BF16),
        scratch_shapes=(
            [pltpu.VMEM((S // DFT_N1 * (DFT_N1 + DFT_PITCH_PAD), FGROUP_DIM), F32)] * 2
            + [pltpu.VMEM((DFT_N1 * (S // DFT_N1 + DFT_PITCH_PAD), FGROUP_DIM), F32)] * 3),
        compiler_params=_cparams("parallel", "parallel"), name="fourier")(
            a, dftc, taba, kc, ks, w_fourier.astype(BF16))


def _attn_kernel(q_ref, kt_ref, v_ref, o_ref, qs_ref, kmax_ref, acc_ref, m_ref, s0_ref, s1_ref):
    tq = q_ref.shape[0]
    n_keys = kt_ref.shape[1]
    tk = min(ATT_TK, n_keys)
    n_chunks = n_keys // tk
    assert n_chunks % 2 == 0 and n_chunks * tk == n_keys

    @pl.when(pl.program_id(2) == 0)
    def _():
        def body(c, best):
            k = kt_ref[:, pl.ds(pl.multiple_of(c * tk, tk), tk)].astype(F32)
            return jnp.maximum(best, jnp.sum(k * k, axis=0, keepdims=True))
        best = lax.fori_loop(0, n_chunks, body, jnp.zeros((1, tk), F32))
        kmax_ref[...] = jnp.broadcast_to(jnp.sqrt(jnp.max(best, axis=1, keepdims=True)), kmax_ref.shape)

    for g in range(Q_GROUP):
        qs_ref[g * tq:(g + 1) * tq, :] = q_ref[:, g * HEAD_DIM:(g + 1) * HEAD_DIM]
    qf = qs_ref[...].astype(F32)
    bound = jnp.sqrt(jnp.sum(qf * qf, axis=1, keepdims=True)) * kmax_ref[0:1, 0:1] * ATT_BOUND_SLACK

    def chunk(c):
        cols = pl.ds(pl.multiple_of(c * tk, tk), tk)
        return _dot(qs_ref[...], kt_ref[:, cols]), v_ref[cols, :]

    def scores(c):
        return _dot(qs_ref[...], kt_ref[:, pl.ds(pl.multiple_of(c * tk, tk), tk)])

    def weighted(s_buf, c):
        v = v_ref[pl.ds(pl.multiple_of(c * tk, tk), tk), :]
        acc_ref[...] += _dot(jnp.exp2(s_buf[...] - bound).astype(BF16), v)

    def fast(c2, carry):
        c = 2 * c2
        s1_ref[...] = scores(c + 1)
        weighted(s0_ref, c)
        s0_ref[...] = scores(jnp.minimum(c + 2, n_chunks - 1))
        weighted(s1_ref, c + 1)
        return carry

    acc_ref[...] = jnp.zeros(acc_ref.shape, F32)
    s0_ref[...] = scores(0)
    lax.fori_loop(0, n_chunks // 2, fast, 0)
    underflow = jnp.min(acc_ref[:, HEAD_DIM:HEAD_DIM + 1]) < ATT_MIN_ROW_SUM

    @pl.when(underflow)
    def _():
        def safe(c, carry):
            s, v = chunk(c)
            m_old = m_ref[...]
            m_new = jnp.maximum(m_old, jnp.max(s, axis=1, keepdims=True))
            acc_ref[...] = jnp.exp2(m_old - m_new) * acc_ref[...] + _dot(jnp.exp2(s - m_new).astype(BF16), v)
            m_ref[...] = m_new
            return carry

        m_ref[...] = jnp.full(m_ref.shape, NEG_INF, F32)
        acc_ref[...] = jnp.zeros(acc_ref.shape, F32)
        lax.fori_loop(0, n_chunks, safe, 0)

    acc = acc_ref[...]
    o = acc[:, :HEAD_DIM] / acc[:, HEAD_DIM:HEAD_DIM + 1]
    o_ref[...] = jnp.concatenate([o[g * tq:(g + 1) * tq] for g in range(Q_GROUP)], axis=1).astype(BF16)


def _attention(q, kt, v2, B, S):
    T = q.shape[0]
    tq = ATT_TQ
    nq = S // tq
    gw = Q_GROUP * HEAD_DIM
    rows = Q_GROUP * tq
    return pl.pallas_call(
        _attn_kernel, grid=(B, N_KV_HEADS, nq),
        in_specs=[pl.BlockSpec((tq, gw), lambda b, h, i: (b * nq + i, h)),
                  pl.BlockSpec((None, HEAD_DIM, S), lambda b, h, i: (b, h, 0)),
                  pl.BlockSpec((None, None, S, 128), lambda b, h, i: (h, b, 0, 0))],
        out_specs=pl.BlockSpec((tq, gw), lambda b, h, i: (b * nq + i, h)),
        out_shape=jax.ShapeDtypeStruct((T, Q_WIDTH), BF16),
        scratch_shapes=[pltpu.VMEM((rows, HEAD_DIM), BF16), pltpu.VMEM((8, 128), F32),
                        pltpu.VMEM((rows, 128), F32), pltpu.VMEM((rows, 1), F32),
                        pltpu.VMEM((rows, min(ATT_TK, S)), F32), pltpu.VMEM((rows, min(ATT_TK, S)), F32)],
        compiler_params=_cparams("parallel", "parallel", "arbitrary"),
        name="attention")(q, kt, v2.reshape(N_KV_HEADS, B, S, 128))


def _even_out_kernel(a_ref, t_ref, wa_ref, wt_ref, x_ref, g_ref, b_ref, o_ref):
    mix = _dot(a_ref[...], wa_ref[...]) + _dot(t_ref[...], wt_ref[...])
    o_ref[...] = _ln(DEEPNORM_ALPHA * x_ref[...] + mix, g_ref[...], b_ref[...])


def _even_out(a_out, attn, w_out, x, g, b):
    T, D = x.shape
    tm = ROW_TILE
    row = lambda w: pl.BlockSpec((tm, w), lambda i: (i, 0))
    full = lambda a: pl.BlockSpec(a.shape, lambda i: (0,) * a.ndim)
    wa = w_out[:F_WIDTH].astype(BF16)
    wt = w_out[F_WIDTH:].astype(BF16)
    g = g.reshape(1, D)
    b = b.reshape(1, D)
    return pl.pallas_call(
        _even_out_kernel, grid=(T // tm,),
        in_specs=[row(F_WIDTH), row(Q_WIDTH), full(wa), full(wt), row(D), full(g), full(b)],
        out_specs=row(D), out_shape=jax.ShapeDtypeStruct((T, D), F32),
        compiler_params=_cparams("parallel"), name="even_out")(a_out, attn, wa, wt, x, g, b)


def _odd_kernel(x_ref, wi_ref, bi_ref, vg_ref, vb_ref, ws_ref, bs_ref, wo_ref, g_ref, b_ref, o_ref,
                gate_ref):
    tm = x_ref.shape[0]
    x = x_ref[...]
    h = _dot(x.astype(BF16), wi_ref[...]) + bi_ref[...]
    h = 0.5 * h * (1.0 + lax.erf(h * (1.0 / math.sqrt(2.0))))
    u = h[:, :C_WIDTH]
    v = _ln(h[:, C_WIDTH:], vg_ref[...], vb_ref[...]).astype(BF16)
    for c in range(tm // CHUNK):
        r0 = c * CHUNK
        for gi in range(N_CGROUPS):
            l0 = gi * CGROUP_DIM
            sv = _dot(ws_ref[gi], v[r0:r0 + CHUNK, l0:l0 + CGROUP_DIM]) + bs_ref[gi]
            gate_ref[r0:r0 + CHUNK, l0:l0 + CGROUP_DIM] = (
                u[r0:r0 + CHUNK, l0:l0 + CGROUP_DIM] * sv).astype(BF16)
    mix = _dot(gate_ref[...], wo_ref[...])
    o_ref[...] = _ln(DEEPNORM_ALPHA * x + mix, g_ref[...], b_ref[...])


def _odd_mixer(x, w_in, b_in, v_g, v_b, w_s, b_s, w_out, g, b):
    T, D = x.shape
    tm = ROW_TILE
    row = pl.BlockSpec((tm, D), lambda i: (i, 0))
    full = lambda a: pl.BlockSpec(a.shape, lambda i: (0,) * a.ndim)
    args = [w_in.astype(BF16), b_in.reshape(1, 2 * C_WIDTH), v_g.reshape(1, C_WIDTH),
            v_b.reshape(1, C_WIDTH), w_s.astype(BF16),
            jnp.broadcast_to(b_s[:, :, None], (N_CGROUPS, CHUNK, CGROUP_DIM)).astype(F32),
            w_out.astype(BF16), g.reshape(1, D), b.reshape(1, D)]
    return pl.pallas_call(
        _odd_kernel, grid=(T // tm,),
        in_specs=[row] + [full(a) for a in args],
        out_specs=row, out_shape=jax.ShapeDtypeStruct((T, D), F32),
        scratch_shapes=[pltpu.VMEM((tm, C_WIDTH), BF16)],
        compiler_params=_cparams("parallel"), name="odd_mixer")(x, *args)


def _router_kernel(x_ref, w_ref, rb_ref, tri_ref, eidx_ref, wts_ref, pos_ref, cnt_ref, xp_ref, run_ref):
    tm = x_ref.shape[0]
    i = pl.program_id(0)

    @pl.when(i == 0)
    def _():
        run_ref[...] = jnp.zeros(run_ref.shape, F32)

    x = x_ref[...]
    xp_ref[...] = _pack_halves(x)
    xh = x.astype(BF16)
    xl = (x - xh.astype(F32)).astype(BF16)
    nt = (((1,), (1,)), ((), ()))
    dg = lambda a, c: lax.dot_general(a, c, nt, preferred_element_type=F32)
    logits = dg(w_ref[0], xh) + dg(w_ref[0], xl) + dg(w_ref[1], xh)
    scores = jax.nn.sigmoid(logits)
    sel = scores + rb_ref[...]

    i8 = lax.broadcasted_iota(I32, (GROUP_SIZE, tm), 0)
    gsc_rows = []
    for gidx in range(N_EXPERT_GROUPS):
        sg = sel[gidx * GROUP_SIZE:(gidx + 1) * GROUP_SIZE, :]
        m1 = jnp.max(sg, axis=0, keepdims=True)
        f1 = jnp.min(jnp.where(sg == m1, i8, GROUP_SIZE), axis=0, keepdims=True)
        m2 = jnp.max(jnp.where(i8 == f1, NEG_INF, sg), axis=0, keepdims=True)
        gsc_rows.append(m1 + m2)
    gsc = jnp.concatenate(gsc_rows, axis=0)

    gsel = jnp.zeros(gsc.shape, F32)
    for _ in range(TOPK_GROUPS):
        m = jnp.max(gsc, axis=0, keepdims=True)
        f = jnp.min(jnp.where(gsc == m, i8, N_EXPERT_GROUPS), axis=0, keepdims=True)
        pick = i8 == f
        gsel = jnp.where(pick, 1.0, gsel)
        gsc = jnp.where(pick, NEG_INF, gsc)
    esel = jnp.concatenate(
        [jnp.broadcast_to(gsel[gidx:gidx + 1, :], (GROUP_SIZE, tm)) for gidx in range(N_EXPERT_GROUPS)],
        axis=0)

    cur = jnp.where(esel > 0.0, sel, NEG_INF)
    ei = lax.broadcasted_iota(I32, cur.shape, 0)
    idx_rows, sc_rows = [], []
    chosen = jnp.zeros(cur.shape, F32)
    for _ in range(TOP_K):
        m = jnp.max(cur, axis=0, keepdims=True)
        f = jnp.min(jnp.where(cur == m, ei, N_EXPERTS), axis=0, keepdims=True)
        pick = ei == f
        idx_rows.append(f)
        sc_rows.append(jnp.sum(jnp.where(pick, scores, 0.0), axis=0, keepdims=True))
        chosen = jnp.where(pick, 1.0, chosen)
        cur = jnp.where(pick, NEG_INF, cur)
    eidx = jnp.concatenate(idx_rows, axis=0)
    sc = jnp.concatenate(sc_rows, axis=0)
    eidx_ref[...] = eidx
    wts_ref[...] = sc / jnp.sum(sc, axis=0, keepdims=True) * ROUTE_SCALE

    before = _dot(chosen.astype(BF16), tri_ref[...]) + run_ref[...]
    pos_rows = [jnp.sum(jnp.where(ei == idx_rows[k], before, 0.0), axis=0, keepdims=True)
                for k in range(TOP_K)]
    pos_ref[...] = jnp.concatenate(pos_rows, axis=0).astype(I32)
    run_new = run_ref[...] + jnp.sum(chosen, axis=1, keepdims=True)
    run_ref[...] = run_new
    cnt_ref[...] = jnp.broadcast_to(run_new, cnt_ref.shape).astype(I32)


def _router(x, router_w, router_b, first_row, T):
    D = x.shape[1]
    tm = min(ROUTER_TILE, T)
    n_tiles = T // tm
    first_tile = first_row // tm
    assert n_tiles * tm == T and first_tile * tm == first_row
    wt = router_w.T.astype(F32)
    wh = wt.astype(BF16)
    wl = (wt - wh.astype(F32)).astype(BF16)
    w2 = jnp.stack([wh, wl])
    rb = router_b.astype(F32).reshape(N_EXPERTS, 1)
    tri = jnp.asarray(np.triu(np.ones((tm, tm)), 1), BF16)
    full = lambda a: pl.BlockSpec(a.shape, lambda i: (0,) * a.ndim)
    col = pl.BlockSpec((TOP_K, tm), lambda i: (0, i))
    return pl.pallas_call(
        _router_kernel, grid=(n_tiles,),
        in_specs=[pl.BlockSpec((tm, D), lambda i: (i + first_tile, 0)), full(w2), full(rb), full(tri)],
        out_specs=[col, col, col, pl.BlockSpec((N_EXPERTS, 128), lambda i: (0, 0)),
                   pl.BlockSpec((tm, HALF), lambda i: (i, 0))],
        out_shape=[jax.ShapeDtypeStruct((TOP_K, T), I32), jax.ShapeDtypeStruct((TOP_K, T), F32),
                   jax.ShapeDtypeStruct((TOP_K, T), I32), jax.ShapeDtypeStruct((N_EXPERTS, 128), I32),
                   jax.ShapeDtypeStruct((T, HALF), I32)],
        scratch_shapes=[pltpu.VMEM((N_EXPERTS, 1), F32)],
        compiler_params=_cparams("arbitrary"), name="router")(x, w2, rb, tri)


def _dest_kernel(start_ref, eidx_ref, pos_ref, o_ref):
    e = eidx_ref[...]
    acc = pos_ref[...]
    for j in range(N_EXPERTS):
        acc = acc + jnp.where(e == j, start_ref[j], 0)
    o_ref[...] = acc


def _dest_rows(eidx, pos, seg_start):
    K, T = eidx.shape
    tl = min(T, 2048)
    blk = pl.BlockSpec((K, tl), lambda i, s: (0, i))
    grid_spec = pltpu.PrefetchScalarGridSpec(
        num_scalar_prefetch=1, grid=(T // tl,), in_specs=[blk, blk], out_specs=blk)
    return pl.pallas_call(
        _dest_kernel, grid_spec=grid_spec, out_shape=jax.ShapeDtypeStruct((K, T), I32),
        compiler_params=_cparams("parallel"), name="dest_rows")(seg_start, eidx, pos)


def _gather_rows(table, idx):
    n_rows = idx.shape[0]
    width = table.shape[1]
    per_worker = n_rows // SC_WORKERS
    n_chunks = per_worker // SC_CHUNK
    assert per_worker * SC_WORKERS == n_rows and n_chunks * SC_CHUNK == per_worker
    mesh = plsc.VectorSubcoreMesh(core_axis_name="c", subcore_axis_name="s")

    @functools.partial(
        pl.kernel, mesh=mesh,
        out_type=jax.ShapeDtypeStruct((n_rows, width), table.dtype),
        scratch_types=[pltpu.VMEM((SC_CHUNK,), I32), pltpu.VMEM((SC_CHUNK, width), table.dtype),
                       pltpu.SemaphoreType.DMA])
    def gather(table_hbm, idx_hbm, out_hbm, idx_v, rows_v, sem):
        wid = lax.axis_index("s") * 2 + lax.axis_index("c")
        base = wid * per_worker

        @pl.loop(0, n_chunks)
        def _(j):
            off = base + j * SC_CHUNK
            pltpu.sync_copy(idx_hbm.at[pl.ds(off, SC_CHUNK)], idx_v)
            pltpu.async_copy(table_hbm.at[idx_v], rows_v, sem).wait()
            pltpu.sync_copy(rows_v, out_hbm.at[pl.ds(off, SC_CHUNK)])

    return gather(table, idx)


def _scatter_rows(rows, dest):
    n_tok, width = rows.shape
    n_dst = dest.shape[0]
    per_worker = n_tok // SC_WORKERS
    n_chunks = per_worker // SC_CHUNK
    assert per_worker * SC_WORKERS == n_tok and n_chunks * SC_CHUNK == per_worker
    mesh = plsc.VectorSubcoreMesh(core_axis_name="c", subcore_axis_name="s")

    @functools.partial(
        pl.kernel, mesh=mesh,
        out_type=jax.ShapeDtypeStruct((n_dst * n_tok, width), rows.dtype),
        scratch_types=[pltpu.VMEM((n_dst, SC_CHUNK), I32), pltpu.VMEM((SC_CHUNK, width), rows.dtype),
                       pltpu.SemaphoreType.DMA])
    def scatter(rows_hbm, dest_hbm, out_hbm, idx_v, rows_v, sem):
        wid = lax.axis_index("s") * 2 + lax.axis_index("c")
        base = wid * per_worker

        @pl.loop(0, n_chunks)
        def _(j):
            off = base + j * SC_CHUNK
            pltpu.sync_copy(dest_hbm.at[:, pl.ds(off, SC_CHUNK)], idx_v)
            pltpu.sync_copy(rows_hbm.at[pl.ds(off, SC_CHUNK)], rows_v)
            copies = [pltpu.async_copy(rows_v, out_hbm.at[idx_v.at[k]], sem) for k in range(n_dst)]
            for c in copies:
                c.wait()

    return scatter(rows, dest)


def _expert_kernel(blk_ref, exp_ref, lo_ref, hi_ref, slot_ref, nxt_ref, xs_hbm, wg_hbm, wu_hbm, wd_hbm,
                   ys_ref, wg_buf, wu_buf, wd_buf, wgu_s, wd_s, sem, xs_buf, xs_sem, *, layer):
    i = pl.program_id(0)
    prev = jnp.maximum(i - 1, 0)

    def weight_copies(expert, slot):
        return [pltpu.make_async_copy(src.at[layer, expert], dst.at[slot], sem.at[slot])
                for src, dst in ((wg_hbm, wg_buf), (wu_hbm, wu_buf), (wd_hbm, wd_buf))]

    @pl.when(i == 0)
    def _():
        for c in weight_copies(exp_ref[0], slot_ref[0]):
            c.start()

    @pl.when(jnp.logical_or(i == 0, exp_ref[i] != exp_ref[prev]))
    def _():
        slot = slot_ref[i]
        for c in weight_copies(exp_ref[i], slot):
            c.wait()
        wgu_s[:, :EXPERT_DIM] = wg_buf[slot].astype(BF16)
        wgu_s[:, EXPERT_DIM:] = wu_buf[slot].astype(BF16)
        wd_s[...] = wd_buf[slot].astype(BF16)

        @pl.when(nxt_ref[i] >= 0)
        def _():
            for c in weight_copies(nxt_ref[i], 1 - slot):
                c.start()

    n_blocks = xs_hbm.shape[0] // EXPERT_ROWS
    blk = blk_ref[i]
    first = jnp.logical_or(i == 0, blk != blk_ref[prev])

    def rows_copy(block):
        slot = block % EXPERT_XS_SLOTS
        src = xs_hbm.at[pl.ds(pl.multiple_of(block * EXPERT_ROWS, EXPERT_ROWS), EXPERT_ROWS)]
        return pltpu.make_async_copy(src, xs_buf.at[slot], xs_sem.at[slot])

    @pl.when(i == 0)
    def _():
        for b0 in range(min(EXPERT_XS_SLOTS - 1, n_blocks)):
            rows_copy(b0).start()

    @pl.when(first)
    def _():
        rows_copy(blk).wait()

        @pl.when(blk + EXPERT_XS_SLOTS - 1 < n_blocks)
        def _():
            rows_copy(blk + EXPERT_XS_SLOTS - 1).start()

    xs_ref = xs_buf.at[blk % EXPERT_XS_SLOTS]
    lo = lo_ref[i]
    hi = hi_ref[i]

    def sub_block(r0):
        rows = slice(r0, r0 + EXPERT_SUB_ROWS)
        w = xs_ref[rows, :]
        xlo = lax.bitcast_convert_type(w.astype(jnp.int16), BF16)
        xhi = lax.bitcast_convert_type(lax.shift_right_logical(w, 16).astype(jnp.int16), BF16)
        gu = _dot(xlo, wgu_s[:HALF, :]) + _dot(xhi, wgu_s[HALF:, :])
        g = gu[:, :EXPERT_DIM]
        hb = (g * jax.nn.sigmoid(g) * gu[:, EXPERT_DIM:]).astype(BF16)
        y = _dot(hb, wd_s[...])
        packed = pltpu.pack_elementwise([y[:, :HALF], y[:, HALF:]], packed_dtype=BF16)
        row = r0 + lax.broadcasted_iota(I32, (EXPERT_SUB_ROWS, 1), 0)
        mine = jnp.logical_and(row >= lo, row < hi)
        kept = jnp.where(first, 0, ys_ref[rows, :])
        ys_ref[rows, :] = jnp.where(mine, packed, kept)

    for r0 in range(0, EXPERT_ROWS, EXPERT_SUB_ROWS):
        pl.when(jnp.logical_and(lo < r0 + EXPERT_SUB_ROWS, hi > r0))(functools.partial(sub_block, r0))


def _expert_items(counts, n_rows):
    n_blocks = n_rows // EXPERT_ROWS
    n_items = n_blocks + N_EXPERTS - 1
    end = jnp.cumsum(counts)
    start = end - counts
    first_blk = start // EXPERT_ROWS
    n_blk = jnp.where(counts > 0, (end - 1) // EXPERT_ROWS - first_blk + 1, 0)
    item_end = jnp.cumsum(n_blk)
    item_start = item_end - n_blk
    slot = jnp.arange(n_items, dtype=I32)
    e = jnp.minimum(jnp.sum((item_end[None, :] <= slot[:, None]).astype(I32), axis=1), N_EXPERTS - 1)
    onehot = (e[:, None] == jnp.arange(N_EXPERTS, dtype=I32)[None, :]).astype(I32)
    pick = lambda v: jnp.sum(onehot * v[None, :], axis=1)
    valid = slot < item_end[-1]
    blk = jnp.where(valid, pick(first_blk) + slot - pick(item_start), n_blocks - 1)
    lo = jnp.clip(pick(start) - blk * EXPERT_ROWS, 0, EXPERT_ROWS)
    hi = jnp.clip(pick(end) - blk * EXPERT_ROWS, 0, EXPERT_ROWS)
    last_e = jnp.max(jnp.where(counts > 0, jnp.arange(N_EXPERTS, dtype=I32), 0))
    e = jnp.where(valid, e, last_e)
    hi = jnp.where(valid, hi, 0)
    lo = jnp.where(valid, lo, 0)
    change = jnp.concatenate([jnp.ones((1,), I32), (e[1:] != e[:-1]).astype(I32)])
    slot = (jnp.cumsum(change) - 1) % 2
    later = jnp.where(jnp.arange(N_EXPERTS, dtype=I32)[None, :] > e[:, None], counts[None, :] > 0, False)
    nxt = jnp.where(jnp.any(later, axis=1), jnp.argmax(later, axis=1), -1)
    return tuple(a.astype(I32) for a in (blk, e, lo, hi, slot, nxt))


def _experts(xs, items, w_gate, w_up, w_down, layer):
    n_rows = xs.shape[0]
    n_items = items[0].shape[0]
    blk_map = lambda i, b, *_: (b[i], 0)
    hbm = pl.BlockSpec(memory_space=pl.ANY)
    grid_spec = pltpu.PrefetchScalarGridSpec(
        num_scalar_prefetch=len(items), grid=(n_items,),
        in_specs=[hbm, hbm, hbm, hbm],
        out_specs=pl.BlockSpec((EXPERT_ROWS, HALF), blk_map),
        scratch_shapes=[pltpu.VMEM((2, D_MODEL, EXPERT_DIM), F32),
                        pltpu.VMEM((2, D_MODEL, EXPERT_DIM), F32),
                        pltpu.VMEM((2, EXPERT_DIM, D_MODEL), F32),
                        pltpu.VMEM((D_MODEL, 2 * EXPERT_DIM), BF16),
                        pltpu.VMEM((EXPERT_DIM, D_MODEL), BF16),
                        pltpu.SemaphoreType.DMA((2,)),
                        pltpu.VMEM((EXPERT_XS_SLOTS, EXPERT_ROWS, HALF), I32),
                        pltpu.SemaphoreType.DMA((EXPERT_XS_SLOTS,))])
    return pl.pallas_call(
        functools.partial(_expert_kernel, layer=layer), grid_spec=grid_spec,
        out_shape=jax.ShapeDtypeStruct((n_rows, HALF), I32),
        compiler_params=_cparams("arbitrary"), name="experts")(*items, xs, w_gate, w_up, w_down)


def _moe_out_kernel(x_ref, yg_ref, wt_ref, sgu_ref, sd_ref, g_ref, b_ref, *rest):
    o_ref = rest[-1]
    x = x_ref[...]
    wt = wt_ref[...]
    lo = jnp.zeros((x.shape[0], HALF), F32)
    hi = jnp.zeros((x.shape[0], HALF), F32)
    for k in range(TOP_K):
        w = yg_ref[k]
        wk = wt[:, k:k + 1]
        lo = lo + wk * _unpack_lo(w)
        hi = hi + wk * _unpack_hi(w)
    gu = _dot(x.astype(BF16), sgu_ref[...])
    g = gu[:, :EXPERT_DIM]
    hs = (g * jax.nn.sigmoid(g) * gu[:, EXPERT_DIM:]).astype(BF16)
    ffn = jnp.concatenate([lo, hi], axis=1) + _dot(hs, sd_ref[...])
    o_ref[...] = _ln(DEEPNORM_ALPHA * x + ffn, g_ref[...], b_ref[...])


def _moe_out(x, yg, wts, sh_gate, sh_up, sh_down, g, b, first_tile, partial_out):
    T, D = x.shape
    tm = ROW_TILE
    n_tiles = yg.shape[1] // tm
    xrow = pl.BlockSpec((tm, D), lambda i: (i + first_tile, 0))
    full = lambda a: pl.BlockSpec(a.shape, lambda i: (0,) * a.ndim)
    sgu = jnp.concatenate([sh_gate, sh_up], axis=1).astype(BF16)
    sd = sh_down.astype(BF16)
    g = g.reshape(1, D)
    b = b.reshape(1, D)
    args = [x, yg, wts, sgu, sd, g, b]
    in_specs = [xrow, pl.BlockSpec((TOP_K, tm, HALF), lambda i: (0, i, 0)),
                pl.BlockSpec((tm, TOP_K), lambda i: (i, 0)), full(sgu), full(sd), full(g), full(b)]
    aliases = {}
    if partial_out is not None:
        args.append(partial_out)
        in_specs.append(pl.BlockSpec(memory_space=pl.ANY))
        aliases = {len(args) - 1: 0}
    return pl.pallas_call(
        _moe_out_kernel, grid=(n_tiles,), in_specs=in_specs,
        out_specs=xrow, out_shape=jax.ShapeDtypeStruct((T, D), F32),
        input_output_aliases=aliases,
        compiler_params=_cparams("parallel"), name="moe_out")(*args)


def _moe(x, router_w, router_b, w_gate, w_up, w_down, layer, sh_gate, sh_up, sh_down, g, b):
    T = x.shape[0]
    tiles = T // ROW_TILE // MOE_TOKEN_GROUPS
    tg = tiles * ROW_TILE
    out = None
    for grp in range(MOE_TOKEN_GROUPS):
        eidx, wts, pos, cnt, xp = _router(x, router_w, router_b, grp * tg, tg)
        counts = cnt[:, 0]
        seg_start = (jnp.cumsum(counts) - counts).astype(I32)
        dest = _dest_rows(eidx, pos, seg_start)
        xs = _scatter_rows(xp, dest)
        ys = _experts(xs, _expert_items(counts, tg * TOP_K), w_gate, w_up, w_down, layer)
        yg = _gather_rows(ys, dest.reshape(tg * TOP_K)).reshape(TOP_K, tg, HALF)
        out = _moe_out(x, yg, wts.T, sh_gate, sh_up, sh_down, g, b, grp * tiles, out)
    return out


def kernel(x, ln_in_g, ln_in_b, e_w_in, e_w_fourier, e_q_gain, e_k_gain, e_w_out, o_w_in, o_b_in, o_v_ln_g, o_v_ln_b, o_w_spatial, o_b_spatial, o_w_out, ln_mix_g, ln_mix_b, ln_ffn_g, ln_ffn_b, router_w, router_b, exp_w_gate, exp_w_up, exp_w_down, sh_w_gate, sh_w_up, sh_w_down):
    B, S, D = x.shape
    Bc = B // BATCH_CHAINS
    T = Bc * S
    hs = [_layer_norm(xc.reshape(T, D), ln_in_g, ln_in_b) for xc in jnp.split(x, BATCH_CHAINS, axis=0)]
    for i in range(DEPTH):
        j = i // 2
        for c, h in enumerate(hs):
            if i % 2 == 0:
                a, q, kt, v2 = _even_in(h, e_w_in[j], e_q_gain[j], e_k_gain[j], Bc, S)
                a_out = _fourier(a, e_w_fourier[j], Bc, S)
                attn = _attention(q, kt, v2, Bc, S)
                h = _even_out(a_out, attn, e_w_out[j], h, ln_mix_g[i], ln_mix_b[i])
            else:
                h = _odd_mixer(h, o_w_in[j], o_b_in[j], o_v_ln_g[j], o_v_ln_b[j], o_w_spatial[j],
                               o_b_spatial[j], o_w_out[j], ln_mix_g[i], ln_mix_b[i])
            hs[c] = _moe(h, router_w[i], router_b[i], exp_w_gate, exp_w_up, exp_w_down, i,
                         sh_w_gate[i], sh_w_up[i], sh_w_down[i], ln_ffn_g[i], ln_ffn_b[i])
    return jnp.concatenate(hs, axis=0).reshape(B, S, D)
```

```python
import functools
import math

import numpy as np
import jax
import jax.numpy as jnp
from jax import lax
from jax.experimental import pallas as pl
from jax.experimental.pallas import tpu as pltpu
from jax.experimental.pallas import tpu_sc as plsc

F32 = jnp.float32
BF16 = jnp.bfloat16
I32 = jnp.int32

D_MODEL = 1024
DEPTH = 4
GRID_W = 64
N_FGROUPS = 4
FGROUP_DIM = 128
F_WIDTH = N_FGROUPS * FGROUP_DIM
N_HEADS = 8
N_KV_HEADS = 2
HEAD_DIM = 64
Q_GROUP = N_HEADS // N_KV_HEADS
Q_WIDTH = N_HEADS * HEAD_DIM
KV_WIDTH = N_KV_HEADS * HEAD_DIM
ROPE_THETA = 10000.0
ROPE_PAIRS = HEAD_DIM // 4
EVEN_IN_WIDTH = F_WIDTH + Q_WIDTH + 2 * KV_WIDTH
CHUNK = 128
N_CGROUPS = 8
CGROUP_DIM = D_MODEL // N_CGROUPS
C_WIDTH = N_CGROUPS * CGROUP_DIM
N_EXPERTS = 64
EXPERT_DIM = 256
TOP_K = 8
N_EXPERT_GROUPS = 8
GROUP_SIZE = N_EXPERTS // N_EXPERT_GROUPS
TOPK_GROUPS = 4
ROUTE_SCALE = 2.5
LN_EPS = 1e-5
QK_EPS = 1e-6
DEEPNORM_ALPHA = (2 * DEPTH) ** 0.25

VMEM_LIMIT_BYTES = 56 * 1024 * 1024
ROW_TILE = 512
DFT_N1 = 64
DFT_KRON = 4
DFT_PITCH_PAD = 8
ROUTER_TILE = 1024
EXPERT_ROWS = 2048
EXPERT_XS_SLOTS = 3
EXPERT_SUB_ROWS = 512
HALF = D_MODEL // 2
SC_WORKERS = 32
SC_CHUNK = 128
BATCH_CHAINS = 1
MOE_TOKEN_GROUPS = 2
ATT_TQ = 256
ATT_TK = 512
ATT_V_ROWS = 80
ATT_BOUND_SLACK = 1.0 + 2.0 ** -7
ATT_MIN_ROW_SUM = 2.0 ** -80
NEG_INF = float("-inf")


def _cparams(*sem):
    return pltpu.CompilerParams(dimension_semantics=sem, vmem_limit_bytes=VMEM_LIMIT_BYTES)


def _ln(x, g, b):
    mu = jnp.mean(x, axis=-1, keepdims=True)
    xc = x - mu
    var = jnp.mean(xc * xc, axis=-1, keepdims=True)
    return xc * lax.rsqrt(var + LN_EPS) * g + b


def _dot(a, b):
    return jnp.dot(a, b, preferred_element_type=F32)


def _pack_halves(y):
    lo = lax.bitcast_convert_type(y[:, :HALF].astype(BF16).astype(F32), I32)
    hi = lax.bitcast_convert_type(y[:, HALF:].astype(BF16).astype(F32), I32)
    return lax.shift_right_logical(lo, 16) | (hi & jnp.int32(-65536))


def _unpack_lo(w):
    return lax.bitcast_convert_type(lax.shift_left(w, 16), F32)


def _unpack_hi(w):
    return lax.bitcast_convert_type(w & jnp.int32(-65536), F32)


def _ln_kernel(x_ref, g_ref, b_ref, o_ref):
    o_ref[...] = _ln(x_ref[...], g_ref[...], b_ref[...])


def _layer_norm(x, g, b):
    T, D = x.shape
    row = pl.BlockSpec((ROW_TILE, D), lambda i: (i, 0))
    vec = pl.BlockSpec((1, D), lambda i: (0, 0))
    return pl.pallas_call(
        _ln_kernel, grid=(T // ROW_TILE,), in_specs=[row, vec, vec], out_specs=row,
        out_shape=jax.ShapeDtypeStruct((T, D), F32), compiler_params=_cparams("parallel"),
        name="ln_in")(x, g.reshape(1, D), b.reshape(1, D))


def _even_in_kernel(x_ref, w_ref, qm_ref, km_ref, qg_ref, kg_ref, cos_ref, sin_ref,
                    a_ref, qt_ref, k_ref, vt_ref):
    tm = x_ref.shape[0]
    h = _dot(x_ref[...].astype(BF16), w_ref[...])
    a_ref[...] = h[:, :F_WIDTH].astype(BF16)
    q = h[:, F_WIDTH:F_WIDTH + Q_WIDTH]
    k = h[:, F_WIDTH + Q_WIDTH:F_WIDTH + Q_WIDTH + KV_WIDTH]
    v = h[:, F_WIDTH + Q_WIDTH + KV_WIDTH:]
    cos = cos_ref[...]
    sin = sin_ref[...]
    lane = lax.broadcasted_iota(I32, (tm, 128), 1)
    first_of_pair = (lane & ROPE_PAIRS) == 0

    def mean_sq(xf, m_ref):
        sq = xf * xf
        hi = sq.astype(BF16)
        lo = (sq - hi.astype(F32)).astype(BF16)
        return _dot(hi, m_ref[...]) + _dot(lo, m_ref[...])

    def rope(xn):
        sw = jnp.where(first_of_pair, pltpu.roll(xn, 128 - ROPE_PAIRS, 1), pltpu.roll(xn, ROPE_PAIRS, 1))
        return xn * cos + sw * sin

    qn = q * lax.rsqrt(mean_sq(q, qm_ref) + QK_EPS) * qg_ref[...]
    scale = math.log2(math.e) / math.sqrt(HEAD_DIM)
    for c in range(Q_WIDTH // 128):
        qt_ref[c * 128:(c + 1) * 128, :] = (rope(qn[:, c * 128:(c + 1) * 128]) * scale).T.astype(BF16)
    kn = rope(k * lax.rsqrt(mean_sq(k, km_ref) + QK_EPS) * kg_ref[...])
    low = lane < HEAD_DIM
    k_ref[0] = jnp.where(low, kn, 0.0).astype(BF16)
    k_ref[1] = jnp.where(low, pltpu.roll(kn, HEAD_DIM, 1), 0.0).astype(BF16)
    ones_col = jnp.where(lane == HEAD_DIM, 1.0, 0.0)
    vt_ref[0:128, :] = jnp.where(low, v, ones_col).T.astype(BF16)
    vt_ref[128:256, :] = jnp.where(low, pltpu.roll(v, HEAD_DIM, 1), ones_col).T.astype(BF16)


def _rope_tables(S):
    rows = S // GRID_W
    t = np.arange(S)
    inv = ROPE_THETA ** (-np.arange(ROPE_PAIRS, dtype=np.float64) / ROPE_PAIRS)
    ang_r = (t // GRID_W)[:, None] * inv
    ang_c = (t % GRID_W)[:, None] * inv
    del rows
    cos = np.concatenate([np.cos(ang_r), np.cos(ang_r), np.cos(ang_c), np.cos(ang_c)], axis=1)
    sin = np.concatenate([-np.sin(ang_r), np.sin(ang_r), -np.sin(ang_c), np.sin(ang_c)], axis=1)
    return (jnp.asarray(np.tile(cos, (1, 2)), F32), jnp.asarray(np.tile(sin, (1, 2)), F32))


def _head_mean_matrix(width):
    m = np.kron(np.eye(width // HEAD_DIM), np.full((HEAD_DIM, HEAD_DIM), 1.0 / HEAD_DIM))
    return jnp.asarray(m, BF16)


def _even_in(x, w_in, q_gain, k_gain, B, S):
    T, D = x.shape
    tm = ROW_TILE
    ns = S // tm
    cos, sin = _rope_tables(S)
    row = lambda w: pl.BlockSpec((tm, w), lambda i: (i, 0))
    full = lambda a: pl.BlockSpec(a.shape, lambda i: (0,) * a.ndim)
    tab = pl.BlockSpec((tm, 128), lambda i: (i % ns, 0))
    w = w_in.astype(BF16)
    qm = _head_mean_matrix(Q_WIDTH)
    km = _head_mean_matrix(KV_WIDTH)
    qg = jnp.tile(q_gain.astype(F32), N_HEADS).reshape(1, Q_WIDTH)
    kg = jnp.tile(k_gain.astype(F32), N_KV_HEADS).reshape(1, KV_WIDTH)
    return pl.pallas_call(
        _even_in_kernel, grid=(T // tm,),
        in_specs=[row(D), full(w), full(qm), full(km), full(qg), full(kg), tab, tab],
        out_specs=[row(F_WIDTH),
                   pl.BlockSpec((None, Q_WIDTH, tm), lambda i: (i // ns, 0, i % ns)),
                   pl.BlockSpec((N_KV_HEADS, tm, 128), lambda i: (0, i, 0)),
                   pl.BlockSpec((None, N_KV_HEADS * 128, tm), lambda i: (i // ns, 0, i % ns))],
        out_shape=[jax.ShapeDtypeStruct((T, F_WIDTH), BF16),
                   jax.ShapeDtypeStruct((B, Q_WIDTH, S), BF16),
                   jax.ShapeDtypeStruct((N_KV_HEADS, T, 128), BF16),
                   jax.ShapeDtypeStruct((B, N_KV_HEADS * 128, S), BF16)],
        compiler_params=_cparams("parallel"), name="even_in")(x, w, qm, km, qg, kg, cos, sin)


def _fourier_kernel(a_ref, dftc_ref, taba_ref, kc_ref, ks_ref, wf_ref, o_ref,
                    zr_ref, zi_ref, ur_ref, ui_ref, y_ref):
    S = a_ref.shape[0]
    n1_count = DFT_N1
    n2_count = S // DFT_N1
    pz = n1_count + DFT_PITCH_PAD
    pu = n2_count + DFT_PITCH_PAD
    blk = DFT_KRON * DFT_N1
    scale = 1.0 / math.sqrt(S * FGROUP_DIM)

    def channel_dft(j, carry):
        zz = _dot(a_ref[pl.ds(pl.multiple_of(j * blk, blk), blk), :], dftc_ref[...])
        for q in range(DFT_KRON):
            dst = pl.ds(pl.multiple_of((j * DFT_KRON + q) * pz, 8), n1_count)
            zr_ref[dst, :] = zz[q * n1_count:(q + 1) * n1_count, :FGROUP_DIM]
            zi_ref[dst, :] = zz[q * n1_count:(q + 1) * n1_count, FGROUP_DIM:]
        return carry

    lax.fori_loop(0, S // blk, channel_dft, 0, unroll=2)

    def stage_a(n1, carry):
        src = pl.ds(n1, n2_count, stride=pz)
        zn = jnp.concatenate([zr_ref[src, :], zi_ref[src, :]], axis=1).astype(BF16)
        r = _dot(taba_ref[n1], zn)
        dst = pl.ds(pl.multiple_of(n1 * pu, 8), n2_count)
        ur_ref[dst, :] = r[:n2_count, :FGROUP_DIM] + r[n2_count:, FGROUP_DIM:]
        ui_ref[dst, :] = r[:n2_count, FGROUP_DIM:] - r[n2_count:, :FGROUP_DIM]
        return carry

    lax.fori_loop(0, n1_count, stage_a, 0, unroll=4)

    def stage_b(j, carry):
        srcs = [pl.ds(j * DFT_KRON + q, n1_count, stride=pu) for q in range(DFT_KRON)]
        ur = jnp.concatenate([ur_ref[s, :] for s in srcs], axis=0).astype(BF16)
        ui = jnp.concatenate([ui_ref[s, :] for s in srcs], axis=0).astype(BF16)
        re = _dot(kc_ref[...], ur) + _dot(ks_ref[...], ui)
        out = _dot((re * scale).astype(BF16), wf_ref[...])
        for q in range(DFT_KRON):
            y_ref[srcs[q], :] = out[q * n1_count:(q + 1) * n1_count]
        return carry

    lax.fori_loop(0, S // blk, stage_b, 0, unroll=4)

    def compact(k1, carry):
        o_ref[pl.ds(pl.multiple_of(k1 * n2_count, n2_count), n2_count), :] = (
            y_ref[pl.ds(pl.multiple_of(k1 * pu, 8), n2_count), :].astype(BF16))
        return carry

    lax.fori_loop(0, n1_count, compact, 0)


def _dft_tables(S):
    n1c, n2c = DFT_N1, S // DFT_N1
    c = np.arange(FGROUP_DIM)
    ang = 2 * np.pi * np.outer(c, c) / FGROUP_DIM
    dftc = np.concatenate([np.cos(ang), -np.sin(ang)], axis=1)
    n1 = np.arange(n1c)[:, None, None]
    k2 = np.arange(n2c)[None, :, None]
    n2 = np.arange(n2c)[None, None, :]
    th = 2 * np.pi * (n2 * k2 / n2c + n1 * k2 / S)
    taba = np.concatenate([np.cos(th), np.sin(th)], axis=1)
    k1 = np.arange(n1c)
    g = 2 * np.pi * np.outer(k1, k1) / n1c
    eye = np.eye(DFT_KRON)
    kc = np.kron(eye, np.cos(g))
    ks = np.kron(eye, np.sin(g))
    return tuple(jnp.asarray(t, BF16) for t in (dftc, taba, kc, ks))


def _fourier(a, w_fourier, B, S):
    T = a.shape[0]
    dftc, taba, kc, ks = _dft_tables(S)
    full = lambda t: pl.BlockSpec(t.shape, lambda b, g: (0,) * t.ndim)
    blk = pl.BlockSpec((S, FGROUP_DIM), lambda b, g: (b, g))
    return pl.pallas_call(
        _fourier_kernel, grid=(B, N_FGROUPS),
        in_specs=[blk, full(dftc), full(taba), full(kc), full(ks),
                  pl.BlockSpec((None, FGROUP_DIM, FGROUP_DIM), lambda b, g: (g, 0, 0))],
        out_specs=blk,
        out_shape=jax.ShapeDtypeStruct((T, F_WIDTH), BF16),
        scratch_shapes=(
            [pltpu.VMEM((S // DFT_N1 * (DFT_N1 + DFT_PITCH_PAD), FGROUP_DIM), F32)] * 2
            + [pltpu.VMEM((DFT_N1 * (S // DFT_N1 + DFT_PITCH_PAD), FGROUP_DIM), F32)] * 3),
        compiler_params=_cparams("parallel", "parallel"), name="fourier")(
            a, dftc, taba, kc, ks, w_fourier.astype(BF16))


def _attn_kernel(qt_ref, k_ref, vt_ref, o_ref, qs_ref, kmax_ref, acc_ref, m_ref, s0_ref, s1_ref):
    tq = qt_ref.shape[1]
    n_keys = k_ref.shape[0]
    tk = min(ATT_TK, n_keys)
    n_chunks = n_keys // tk
    assert n_chunks % 2 == 0 and n_chunks * tk == n_keys

    def keys(c):
        return k_ref[pl.ds(pl.multiple_of(c * tk, tk), tk), :]

    @pl.when(pl.program_id(2) == 0)
    def _():
        def body(c, best):
            k = keys(c).astype(F32)
            return jnp.maximum(best, jnp.sum(k * k, axis=1, keepdims=True))
        best = lax.fori_loop(0, n_chunks, body, jnp.zeros((tk, 1), F32))
        kmax_ref[...] = jnp.broadcast_to(jnp.sqrt(jnp.max(best, axis=0, keepdims=True)), kmax_ref.shape)

    qs_ref[HEAD_DIM:, :] = jnp.zeros((128 - HEAD_DIM, Q_GROUP * tq), BF16)
    for g in range(Q_GROUP):
        qs_ref[:HEAD_DIM, g * tq:(g + 1) * tq] = qt_ref[g * HEAD_DIM:(g + 1) * HEAD_DIM, :]
    qf = qs_ref[...].astype(F32)
    bound = jnp.sqrt(jnp.sum(qf * qf, axis=0, keepdims=True)) * kmax_ref[0:1, 0:1] * ATT_BOUND_SLACK

    def scores(c):
        return _dot(keys(c), qs_ref[...])

    def values(c):
        return vt_ref[:ATT_V_ROWS, pl.ds(pl.multiple_of(c * tk, tk), tk)]

    def weighted(s_buf, c):
        acc_ref[...] += _dot(values(c), jnp.exp2(s_buf[...] - bound).astype(BF16))

    def fast(c2, carry):
        c = 2 * c2
        s1_ref[...] = scores(c + 1)
        weighted(s0_ref, c)
        s0_ref[...] = scores(jnp.minimum(c + 2, n_chunks - 1))
        weighted(s1_ref, c + 1)
        return carry

    acc_ref[...] = jnp.zeros(acc_ref.shape, F32)
    s0_ref[...] = scores(0)
    lax.fori_loop(0, n_chunks // 2, fast, 0)
    underflow = jnp.min(acc_ref[HEAD_DIM:HEAD_DIM + 1, :]) < ATT_MIN_ROW_SUM

    @pl.when(underflow)
    def _():
        def safe(c, carry):
            s = scores(c)
            m_old = m_ref[...]
            m_new = jnp.maximum(m_old, jnp.max(s, axis=0, keepdims=True))
            acc_ref[...] = (jnp.exp2(m_old - m_new) * acc_ref[...]
                            + _dot(values(c), jnp.exp2(s - m_new).astype(BF16)))
            m_ref[...] = m_new
            return carry

        m_ref[...] = jnp.full(m_ref.shape, NEG_INF, F32)
        acc_ref[...] = jnp.zeros(acc_ref.shape, F32)
        lax.fori_loop(0, n_chunks, safe, 0)

    acc = acc_ref[...]
    ot = acc[:HEAD_DIM, :] / acc[HEAD_DIM:HEAD_DIM + 1, :]
    ot = jnp.concatenate([ot, jnp.zeros((128 - HEAD_DIM, Q_GROUP * tq), F32)], axis=0)
    o = ot.T
    o_ref[...] = jnp.concatenate([o[g * tq:(g + 1) * tq, :HEAD_DIM] for g in range(Q_GROUP)],
                                 axis=1).astype(BF16)


def _attention(qt, k2, vt, B, S):
    T = B * S
    tq = ATT_TQ
    nq = S // tq
    gw = Q_GROUP * HEAD_DIM
    cols = Q_GROUP * tq
    tk = min(ATT_TK, S)
    return pl.pallas_call(
        _attn_kernel, grid=(B, N_KV_HEADS, nq),
        in_specs=[pl.BlockSpec((None, gw, tq), lambda b, h, i: (b, h, i)),
                  pl.BlockSpec((None, None, S, 128), lambda b, h, i: (h, b, 0, 0)),
                  pl.BlockSpec((None, 128, S), lambda b, h, i: (b, h, 0))],
        out_specs=pl.BlockSpec((tq, gw), lambda b, h, i: (b * nq + i, h)),
        out_shape=jax.ShapeDtypeStruct((T, Q_WIDTH), BF16),
        scratch_shapes=[pltpu.VMEM((128, cols), BF16), pltpu.VMEM((8, 128), F32),
                        pltpu.VMEM((ATT_V_ROWS, cols), F32), pltpu.VMEM((1, cols), F32),
                        pltpu.VMEM((tk, cols), F32), pltpu.VMEM((tk, cols), F32)],
        compiler_params=_cparams("parallel", "parallel", "arbitrary"),
        name="attention")(qt, k2.reshape(N_KV_HEADS, B, S, 128), vt)


def _even_out_kernel(a_ref, t_ref, wa_ref, wt_ref, x_ref, g_ref, b_ref, o_ref):
    mix = _dot(a_ref[...], wa_ref[...]) + _dot(t_ref[...], wt_ref[...])
    o_ref[...] = _ln(DEEPNORM_ALPHA * x_ref[...] + mix, g_ref[...], b_ref[...])


def _even_out(a_out, attn, w_out, x, g, b):
    T, D = x.shape
    tm = ROW_TILE
    row = lambda w: pl.BlockSpec((tm, w), lambda i: (i, 0))
    full = lambda a: pl.BlockSpec(a.shape, lambda i: (0,) * a.ndim)
    wa = w_out[:F_WIDTH].astype(BF16)
    wt = w_out[F_WIDTH:].astype(BF16)
    g = g.reshape(1, D)
    b = b.reshape(1, D)
    return pl.pallas_call(
        _even_out_kernel, grid=(T // tm,),
        in_specs=[row(F_WIDTH), row(Q_WIDTH), full(wa), full(wt), row(D), full(g), full(b)],
        out_specs=row(D), out_shape=jax.ShapeDtypeStruct((T, D), F32),
        compiler_params=_cparams("parallel"), name="even_out")(a_out, attn, wa, wt, x, g, b)


def _odd_kernel(x_ref, wi_ref, bi_ref, vg_ref, vb_ref, ws_ref, bs_ref, wo_ref, g_ref, b_ref, o_ref,
                gate_ref):
    tm = x_ref.shape[0]
    x = x_ref[...]
    h = _dot(x.astype(BF16), wi_ref[...]) + bi_ref[...]
    h = 0.5 * h * (1.0 + lax.erf(h * (1.0 / math.sqrt(2.0))))
    u = h[:, :C_WIDTH]
    v = _ln(h[:, C_WIDTH:], vg_ref[...], vb_ref[...]).astype(BF16)
    for c in range(tm // CHUNK):
        r0 = c * CHUNK
        for gi in range(N_CGROUPS):
            l0 = gi * CGROUP_DIM
            sv = _dot(ws_ref[gi], v[r0:r0 + CHUNK, l0:l0 + CGROUP_DIM]) + bs_ref[gi]
            gate_ref[r0:r0 + CHUNK, l0:l0 + CGROUP_DIM] = (
                u[r0:r0 + CHUNK, l0:l0 + CGROUP_DIM] * sv).astype(BF16)
    mix = _dot(gate_ref[...], wo_ref[...])
    o_ref[...] = _ln(DEEPNORM_ALPHA * x + mix, g_ref[...], b_ref[...])


def _odd_mixer(x, w_in, b_in, v_g, v_b, w_s, b_s, w_out, g, b):
    T, D = x.shape
    tm = ROW_TILE
    row = pl.BlockSpec((tm, D), lambda i: (i, 0))
    full = lambda a: pl.BlockSpec(a.shape, lambda i: (0,) * a.ndim)
    args = [w_in.astype(BF16), b_in.reshape(1, 2 * C_WIDTH), v_g.reshape(1, C_WIDTH),
            v_b.reshape(1, C_WIDTH), w_s.astype(BF16),
            jnp.broadcast_to(b_s[:, :, None], (N_CGROUPS, CHUNK, CGROUP_DIM)).astype(F32),
            w_out.astype(BF16), g.reshape(1, D), b.reshape(1, D)]
    return pl.pallas_call(
        _odd_kernel, grid=(T // tm,),
        in_specs=[row] + [full(a) for a in args],
        out_specs=row, out_shape=jax.ShapeDtypeStruct((T, D), F32),
        scratch_shapes=[pltpu.VMEM((tm, C_WIDTH), BF16)],
        compiler_params=_cparams("parallel"), name="odd_mixer")(x, *args)


def _router_kernel(x_ref, w_ref, rb_ref, tri_ref, eidx_ref, wts_ref, pos_ref, cnt_ref, xp_ref, run_ref):
    tm = x_ref.shape[0]
    i = pl.program_id(0)

    @pl.when(i == 0)
    def _():
        run_ref[...] = jnp.zeros(run_ref.shape, F32)

    x = x_ref[...]
    xp_ref[...] = _pack_halves(x)
    xh = x.astype(BF16)
    xl = (x - xh.astype(F32)).astype(BF16)
    nt = (((1,), (1,)), ((), ()))
    dg = lambda a, c: lax.dot_general(a, c, nt, preferred_element_type=F32)
    logits = dg(w_ref[0], xh) + dg(w_ref[0], xl) + dg(w_ref[1], xh)
    scores = jax.nn.sigmoid(logits)
    sel = scores + rb_ref[...]

    i8 = lax.broadcasted_iota(I32, (GROUP_SIZE, tm), 0)
    gsc_rows = []
    for gidx in range(N_EXPERT_GROUPS):
        sg = sel[gidx * GROUP_SIZE:(gidx + 1) * GROUP_SIZE, :]
        m1 = jnp.max(sg, axis=0, keepdims=True)
        f1 = jnp.min(jnp.where(sg == m1, i8, GROUP_SIZE), axis=0, keepdims=True)
        m2 = jnp.max(jnp.where(i8 == f1, NEG_INF, sg), axis=0, keepdims=True)
        gsc_rows.append(m1 + m2)
    gsc = jnp.concatenate(gsc_rows, axis=0)

    gsel = jnp.zeros(gsc.shape, F32)
    for _ in range(TOPK_GROUPS):
        m = jnp.max(gsc, axis=0, keepdims=True)
        f = jnp.min(jnp.where(gsc == m, i8, N_EXPERT_GROUPS), axis=0, keepdims=True)
        pick = i8 == f
        gsel = jnp.where(pick, 1.0, gsel)
        gsc = jnp.where(pick, NEG_INF, gsc)
    esel = jnp.concatenate(
        [jnp.broadcast_to(gsel[gidx:gidx + 1, :], (GROUP_SIZE, tm)) for gidx in range(N_EXPERT_GROUPS)],
        axis=0)

    cur = jnp.where(esel > 0.0, sel, NEG_INF)
    ei = lax.broadcasted_iota(I32, cur.shape, 0)
    idx_rows, sc_rows = [], []
    chosen = jnp.zeros(cur.shape, F32)
    for _ in range(TOP_K):
        m = jnp.max(cur, axis=0, keepdims=True)
        f = jnp.min(jnp.where(cur == m, ei, N_EXPERTS), axis=0, keepdims=True)
        pick = ei == f
        idx_rows.append(f)
        sc_rows.append(jnp.sum(jnp.where(pick, scores, 0.0), axis=0, keepdims=True))
        chosen = jnp.where(pick, 1.0, chosen)
        cur = jnp.where(pick, NEG_INF, cur)
    eidx = jnp.concatenate(idx_rows, axis=0)
    sc = jnp.concatenate(sc_rows, axis=0)
    eidx_ref[...] = eidx
    wts_ref[...] = sc / jnp.sum(sc, axis=0, keepdims=True) * ROUTE_SCALE

    before = _dot(chosen.astype(BF16), tri_ref[...]) + run_ref[...]
    pos_rows = [jnp.sum(jnp.where(ei == idx_rows[k], before, 0.0), axis=0, keepdims=True)
                for k in range(TOP_K)]
    pos_ref[...] = jnp.concatenate(pos_rows, axis=0).astype(I32)
    run_new = run_ref[...] + jnp.sum(chosen, axis=1, keepdims=True)
    run_ref[...] = run_new
    cnt_ref[...] = jnp.broadcast_to(run_new, cnt_ref.shape).astype(I32)


def _router(x, router_w, router_b, first_row, T):
    D = x.shape[1]
    tm = min(ROUTER_TILE, T)
    n_tiles = T // tm
    first_tile = first_row // tm
    assert n_tiles * tm == T and first_tile * tm == first_row
    wt = router_w.T.astype(F32)
    wh = wt.astype(BF16)
    wl = (wt - wh.astype(F32)).astype(BF16)
    w2 = jnp.stack([wh, wl])
    rb = router_b.astype(F32).reshape(N_EXPERTS, 1)
    tri = jnp.asarray(np.triu(np.ones((tm, tm)), 1), BF16)
    full = lambda a: pl.BlockSpec(a.shape, lambda i: (0,) * a.ndim)
    col = pl.BlockSpec((TOP_K, tm), lambda i: (0, i))
    return pl.pallas_call(
        _router_kernel, grid=(n_tiles,),
        in_specs=[pl.BlockSpec((tm, D), lambda i: (i + first_tile, 0)), full(w2), full(rb), full(tri)],
        out_specs=[col, col, col, pl.BlockSpec((N_EXPERTS, 128), lambda i: (0, 0)),
                   pl.BlockSpec((tm, HALF), lambda i: (i, 0))],
        out_shape=[jax.ShapeDtypeStruct((TOP_K, T), I32), jax.ShapeDtypeStruct((TOP_K, T), F32),
                   jax.ShapeDtypeStruct((TOP_K, T), I32), jax.ShapeDtypeStruct((N_EXPERTS, 128), I32),
                   jax.ShapeDtypeStruct((T, HALF), I32)],
        scratch_shapes=[pltpu.VMEM((N_EXPERTS, 1), F32)],
        compiler_params=_cparams("arbitrary"), name="router")(x, w2, rb, tri)


def _dest_kernel(start_ref, eidx_ref, pos_ref, o_ref):
    e = eidx_ref[...]
    acc = pos_ref[...]
    for j in range(N_EXPERTS):
        acc = acc + jnp.where(e == j, start_ref[j], 0)
    o_ref[...] = acc


def _dest_rows(eidx, pos, seg_start):
    K, T = eidx.shape
    tl = min(T, 2048)
    blk = pl.BlockSpec((K, tl), lambda i, s: (0, i))
    grid_spec = pltpu.PrefetchScalarGridSpec(
        num_scalar_prefetch=1, grid=(T // tl,), in_specs=[blk, blk], out_specs=blk)
    return pl.pallas_call(
        _dest_kernel, grid_spec=grid_spec, out_shape=jax.ShapeDtypeStruct((K, T), I32),
        compiler_params=_cparams("parallel"), name="dest_rows")(seg_start, eidx, pos)


def _gather_rows(table, idx):
    n_rows = idx.shape[0]
    width = table.shape[1]
    per_worker = n_rows // SC_WORKERS
    n_chunks = per_worker // SC_CHUNK
    assert per_worker * SC_WORKERS == n_rows and n_chunks * SC_CHUNK == per_worker
    mesh = plsc.VectorSubcoreMesh(core_axis_name="c", subcore_axis_name="s")

    @functools.partial(
        pl.kernel, mesh=mesh,
        out_type=jax.ShapeDtypeStruct((n_rows, width), table.dtype),
        scratch_types=[pltpu.VMEM((SC_CHUNK,), I32), pltpu.VMEM((SC_CHUNK, width), table.dtype),
                       pltpu.SemaphoreType.DMA])
    def gather(table_hbm, idx_hbm, out_hbm, idx_v, rows_v, sem):
        wid = lax.axis_index("s") * 2 + lax.axis_index("c")
        base = wid * per_worker

        @pl.loop(0, n_chunks)
        def _(j):
            off = base + j * SC_CHUNK
            pltpu.sync_copy(idx_hbm.at[pl.ds(off, SC_CHUNK)], idx_v)
            pltpu.async_copy(table_hbm.at[idx_v], rows_v, sem).wait()
            pltpu.sync_copy(rows_v, out_hbm.at[pl.ds(off, SC_CHUNK)])

    return gather(table, idx)


def _scatter_rows(rows, dest):
    n_tok, width = rows.shape
    n_dst = dest.shape[0]
    per_worker = n_tok // SC_WORKERS
    n_chunks = per_worker // SC_CHUNK
    assert per_worker * SC_WORKERS == n_tok and n_chunks * SC_CHUNK == per_worker
    mesh = plsc.VectorSubcoreMesh(core_axis_name="c", subcore_axis_name="s")

    @functools.partial(
        pl.kernel, mesh=mesh,
        out_type=jax.ShapeDtypeStruct((n_dst * n_tok, width), rows.dtype),
        scratch_types=[pltpu.VMEM((n_dst, SC_CHUNK), I32), pltpu.VMEM((SC_CHUNK, width), rows.dtype),
                       pltpu.SemaphoreType.DMA])
    def scatter(rows_hbm, dest_hbm, out_hbm, idx_v, rows_v, sem):
        wid = lax.axis_index("s") * 2 + lax.axis_index("c")
        base = wid * per_worker

        @pl.loop(0, n_chunks)
        def _(j):
            off = base + j * SC_CHUNK
            pltpu.sync_copy(dest_hbm.at[:, pl.ds(off, SC_CHUNK)], idx_v)
            pltpu.sync_copy(rows_hbm.at[pl.ds(off, SC_CHUNK)], rows_v)
            copies = [pltpu.async_copy(rows_v, out_hbm.at[idx_v.at[k]], sem) for k in range(n_dst)]
            for c in copies:
                c.wait()

    return scatter(rows, dest)


def _expert_kernel(blk_ref, exp_ref, lo_ref, hi_ref, slot_ref, nxt_ref, xs_hbm, wg_hbm, wu_hbm, wd_hbm,
                   ys_ref, wg_buf, wu_buf, wd_buf, wgu_s, wd_s, sem, xs_buf, xs_sem, *, layer):
    i = pl.program_id(0)
    prev = jnp.maximum(i - 1, 0)

    def weight_copies(expert, slot):
        return [pltpu.make_async_copy(src.at[layer, expert], dst.at[slot], sem.at[slot])
                for src, dst in ((wg_hbm, wg_buf), (wu_hbm, wu_buf), (wd_hbm, wd_buf))]

    @pl.when(i == 0)
    def _():
        for c in weight_copies(exp_ref[0], slot_ref[0]):
            c.start()

    @pl.when(jnp.logical_or(i == 0, exp_ref[i] != exp_ref[prev]))
    def _():
        slot = slot_ref[i]
        for c in weight_copies(exp_ref[i], slot):
            c.wait()
        wgu_s[:, :EXPERT_DIM] = wg_buf[slot].astype(BF16)
        wgu_s[:, EXPERT_DIM:] = wu_buf[slot].astype(BF16)
        wd_s[...] = wd_buf[slot].astype(BF16)

        @pl.when(nxt_ref[i] >= 0)
        def _():
            for c in weight_copies(nxt_ref[i], 1 - slot):
                c.start()

    n_blocks = xs_hbm.shape[0] // EXPERT_ROWS
    blk = blk_ref[i]
    first = jnp.logical_or(i == 0, blk != blk_ref[prev])

    def rows_copy(block):
        slot = block % EXPERT_XS_SLOTS
        src = xs_hbm.at[pl.ds(pl.multiple_of(block * EXPERT_ROWS, EXPERT_ROWS), EXPERT_ROWS)]
        return pltpu.make_async_copy(src, xs_buf.at[slot], xs_sem.at[slot])

    @pl.when(i == 0)
    def _():
        for b0 in range(min(EXPERT_XS_SLOTS - 1, n_blocks)):
            rows_copy(b0).start()

    @pl.when(first)
    def _():
        rows_copy(blk).wait()

        @pl.when(blk + EXPERT_XS_SLOTS - 1 < n_blocks)
        def _():
            rows_copy(blk + EXPERT_XS_SLOTS - 1).start()

    xs_ref = xs_buf.at[blk % EXPERT_XS_SLOTS]
    lo = lo_ref[i]
    hi = hi_ref[i]

    def sub_block(r0):
        rows = slice(r0, r0 + EXPERT_SUB_ROWS)
        w = xs_ref[rows, :]
        xlo = lax.bitcast_convert_type(w.astype(jnp.int16), BF16)
        xhi = lax.bitcast_convert_type(lax.shift_right_logical(w, 16).astype(jnp.int16), BF16)
        gu = _dot(xlo, wgu_s[:HALF, :]) + _dot(xhi, wgu_s[HALF:, :])
        g = gu[:, :EXPERT_DIM]
        hb = (g * jax.nn.sigmoid(g) * gu[:, EXPERT_DIM:]).astype(BF16)
        y = _dot(hb, wd_s[...])
        packed = pltpu.pack_elementwise([y[:, :HALF], y[:, HALF:]], packed_dtype=BF16)
        row = r0 + lax.broadcasted_iota(I32, (EXPERT_SUB_ROWS, 1), 0)
        mine = jnp.logical_and(row >= lo, row < hi)
        kept = jnp.where(first, 0, ys_ref[rows, :])
        ys_ref[rows, :] = jnp.where(mine, packed, kept)

    for r0 in range(0, EXPERT_ROWS, EXPERT_SUB_ROWS):
        pl.when(jnp.logical_and(lo < r0 + EXPERT_SUB_ROWS, hi > r0))(functools.partial(sub_block, r0))


def _expert_items(counts, n_rows):
    n_blocks = n_rows // EXPERT_ROWS
    n_items = n_blocks + N_EXPERTS - 1
    end = jnp.cumsum(counts)
    start = end - counts
    first_blk = start // EXPERT_ROWS
    n_blk = jnp.where(counts > 0, (end - 1) // EXPERT_ROWS - first_blk + 1, 0)
    item_end = jnp.cumsum(n_blk)
    item_start = item_end - n_blk
    slot = jnp.arange(n_items, dtype=I32)
    e = jnp.minimum(jnp.sum((item_end[None, :] <= slot[:, None]).astype(I32), axis=1), N_EXPERTS - 1)
    onehot = (e[:, None] == jnp.arange(N_EXPERTS, dtype=I32)[None, :]).astype(I32)
    pick = lambda v: jnp.sum(onehot * v[None, :], axis=1)
    valid = slot < item_end[-1]
    blk = jnp.where(valid, pick(first_blk) + slot - pick(item_start), n_blocks - 1)
    lo = jnp.clip(pick(start) - blk * EXPERT_ROWS, 0, EXPERT_ROWS)
    hi = jnp.clip(pick(end) - blk * EXPERT_ROWS, 0, EXPERT_ROWS)
    last_e = jnp.max(jnp.where(counts > 0, jnp.arange(N_EXPERTS, dtype=I32), 0))
    e = jnp.where(valid, e, last_e)
    hi = jnp.where(valid, hi, 0)
    lo = jnp.where(valid, lo, 0)
    change = jnp.concatenate([jnp.ones((1,), I32), (e[1:] != e[:-1]).astype(I32)])
    slot = (jnp.cumsum(change) - 1) % 2
    later = jnp.where(jnp.arange(N_EXPERTS, dtype=I32)[None, :] > e[:, None], counts[None, :] > 0, False)
    nxt = jnp.where(jnp.any(later, axis=1), jnp.argmax(later, axis=1), -1)
    return tuple(a.astype(I32) for a in (blk, e, lo, hi, slot, nxt))


def _experts(xs, items, w_gate, w_up, w_down, layer):
    n_rows = xs.shape[0]
    n_items = items[0].shape[0]
    blk_map = lambda i, b, *_: (b[i], 0)
    hbm = pl.BlockSpec(memory_space=pl.ANY)
    grid_spec = pltpu.PrefetchScalarGridSpec(
        num_scalar_prefetch=len(items), grid=(n_items,),
        in_specs=[hbm, hbm, hbm, hbm],
        out_specs=pl.BlockSpec((EXPERT_ROWS, HALF), blk_map),
        scratch_shapes=[pltpu.VMEM((2, D_MODEL, EXPERT_DIM), F32),
                        pltpu.VMEM((2, D_MODEL, EXPERT_DIM), F32),
                        pltpu.VMEM((2, EXPERT_DIM, D_MODEL), F32),
                        pltpu.VMEM((D_MODEL, 2 * EXPERT_DIM), BF16),
                        pltpu.VMEM((EXPERT_DIM, D_MODEL), BF16),
                        pltpu.SemaphoreType.DMA((2,)),
                        pltpu.VMEM((EXPERT_XS_SLOTS, EXPERT_ROWS, HALF), I32),
                        pltpu.SemaphoreType.DMA((EXPERT_XS_SLOTS,))])
    return pl.pallas_call(
        functools.partial(_expert_kernel, layer=layer), grid_spec=grid_spec,
        out_shape=jax.ShapeDtypeStruct((n_rows, HALF), I32),
        compiler_params=_cparams("arbitrary"), name="experts")(*items, xs, w_gate, w_up, w_down)


def _moe_out_kernel(x_ref, yg_ref, wt_ref, sgu_ref, sd_ref, g_ref, b_ref, *rest):
    o_ref = rest[-1]
    x = x_ref[...]
    wt = wt_ref[...]
    lo = jnp.zeros((x.shape[0], HALF), F32)
    hi = jnp.zeros((x.shape[0], HALF), F32)
    for k in range(TOP_K):
        w = yg_ref[k]
        wk = wt[:, k:k + 1]
        lo = lo + wk * _unpack_lo(w)
        hi = hi + wk * _unpack_hi(w)
    gu = _dot(x.astype(BF16), sgu_ref[...])
    g = gu[:, :EXPERT_DIM]
    hs = (g * jax.nn.sigmoid(g) * gu[:, EXPERT_DIM:]).astype(BF16)
    ffn = jnp.concatenate([lo, hi], axis=1) + _dot(hs, sd_ref[...])
    o_ref[...] = _ln(DEEPNORM_ALPHA * x + ffn, g_ref[...], b_ref[...])


def _moe_out(x, yg, wts, sh_gate, sh_up, sh_down, g, b, first_tile, partial_out):
    T, D = x.shape
    tm = ROW_TILE
    n_tiles = yg.shape[1] // tm
    xrow = pl.BlockSpec((tm, D), lambda i: (i + first_tile, 0))
    full = lambda a: pl.BlockSpec(a.shape, lambda i: (0,) * a.ndim)
    sgu = jnp.concatenate([sh_gate, sh_up], axis=1).astype(BF16)
    sd = sh_down.astype(BF16)
    g = g.reshape(1, D)
    b = b.reshape(1, D)
    args = [x, yg, wts, sgu, sd, g, b]
    in_specs = [xrow, pl.BlockSpec((TOP_K, tm, HALF), lambda i: (0, i, 0)),
                pl.BlockSpec((tm, TOP_K), lambda i: (i, 0)), full(sgu), full(sd), full(g), full(b)]
    aliases = {}
    if partial_out is not None:
        args.append(partial_out)
        in_specs.append(pl.BlockSpec(memory_space=pl.ANY))
        aliases = {len(args) - 1: 0}
    return pl.pallas_call(
        _moe_out_kernel, grid=(n_tiles,), in_specs=in_specs,
        out_specs=xrow, out_shape=jax.ShapeDtypeStruct((T, D), F32),
        input_output_aliases=aliases,
        compiler_params=_cparams("parallel"), name="moe_out")(*args)


def _moe(x, router_w, router_b, w_gate, w_up, w_down, layer, sh_gate, sh_up, sh_down, g, b):
    T = x.shape[0]
    tiles = T // ROW_TILE // MOE_TOKEN_GROUPS
    tg = tiles * ROW_TILE
    out = None
    for grp in range(MOE_TOKEN_GROUPS):
        eidx, wts, pos, cnt, xp = _router(x, router_w, router_b, grp * tg, tg)
        counts = cnt[:, 0]
        seg_start = (jnp.cumsum(counts) - counts).astype(I32)
        dest = _dest_rows(eidx, pos, seg_start)
        xs = _scatter_rows(xp, dest)
        ys = _experts(xs, _expert_items(counts, tg * TOP_K), w_gate, w_up, w_down, layer)
        yg = _gather_rows(ys, dest.reshape(tg * TOP_K)).reshape(TOP_K, tg, HALF)
        out = _moe_out(x, yg, wts.T, sh_gate, sh_up, sh_down, g, b, grp * tiles, out)
    return out


def kernel(x, ln_in_g, ln_in_b, e_w_in, e_w_fourier, e_q_gain, e_k_gain, e_w_out, o_w_in, o_b_in, o_v_ln_g, o_v_ln_b, o_w_spatial, o_b_spatial, o_w_out, ln_mix_g, ln_mix_b, ln_ffn_g, ln_ffn_b, router_w, router_b, exp_w_gate, exp_w_up, exp_w_down, sh_w_gate, sh_w_up, sh_w_down):
    B, S, D = x.shape
    Bc = B // BATCH_CHAINS
    T = Bc * S
    hs = [_layer_norm(xc.reshape(T, D), ln_in_g, ln_in_b) for xc in jnp.split(x, BATCH_CHAINS, axis=0)]
    for i in range(DEPTH):
        j = i // 2
        for c, h in enumerate(hs):
            if i % 2 == 0:
                a, qt, k2, vt = _even_in(h, e_w_in[j], e_q_gain[j], e_k_gain[j], Bc, S)
                a_out = _fourier(a, e_w_fourier[j], Bc, S)
                attn = _attention(qt, k2, vt, Bc, S)
                h = _even_out(a_out, attn, e_w_out[j], h, ln_mix_g[i], ln_mix_b[i])
            else:
                h = _odd_mixer(h, o_w_in[j], o_b_in[j], o_v_ln_g[j], o_v_ln_b[j], o_w_spatial[j],
                               o_b_spatial[j], o_w_out[j], ln_mix_g[i], ln_mix_b[i])
            hs[c] = _moe(h, router_w[i], router_b[i], exp_w_gate, exp_w_up, exp_w_down, i,
                         sh_w_gate[i], sh_w_up[i], sh_w_down[i], ln_ffn_g[i], ln_ffn_b[i])
    return jnp.concatenate(hs, axis=0).reshape(B, S, D)
```

```python
import functools
import math

import numpy as np
import jax
import jax.numpy as jnp
from jax import lax
from jax.experimental import pallas as pl
from jax.experimental.pallas import tpu as pltpu
from jax.experimental.pallas import tpu_sc as plsc

F32 = jnp.float32
BF16 = jnp.bfloat16
I32 = jnp.int32

D_MODEL = 1024
DEPTH = 4
GRID_W = 64
N_FGROUPS = 4
FGROUP_DIM = 128
F_WIDTH = N_FGROUPS * FGROUP_DIM
N_HEADS = 8
N_KV_HEADS = 2
HEAD_DIM = 64
Q_GROUP = N_HEADS // N_KV_HEADS
Q_WIDTH = N_HEADS * HEAD_DIM
KV_WIDTH = N_KV_HEADS * HEAD_DIM
ROPE_THETA = 10000.0
ROPE_PAIRS = HEAD_DIM // 4
EVEN_IN_WIDTH = F_WIDTH + Q_WIDTH + 2 * KV_WIDTH
CHUNK = 128
N_CGROUPS = 8
CGROUP_DIM = D_MODEL // N_CGROUPS
C_WIDTH = N_CGROUPS * CGROUP_DIM
N_EXPERTS = 64
EXPERT_DIM = 256
TOP_K = 8
N_EXPERT_GROUPS = 8
GROUP_SIZE = N_EXPERTS // N_EXPERT_GROUPS
TOPK_GROUPS = 4
ROUTE_SCALE = 2.5
LN_EPS = 1e-5
QK_EPS = 1e-6
DEEPNORM_ALPHA = (2 * DEPTH) ** 0.25

VMEM_LIMIT_BYTES = 56 * 1024 * 1024
ROW_TILE = 512
DFT_N1 = 64
DFT_KRON = 4
DFT_PITCH_PAD = 8
ROUTER_TILE = 1024
EXPERT_ROWS = 2048
EXPERT_XS_SLOTS = 3
EXPERT_SUB_ROWS = 512
HALF = D_MODEL // 2
SC_WORKERS = 32
SC_CHUNK = 128
BATCH_CHAINS = 1
MOE_TOKEN_GROUPS = 2
ATT_TQ = 256
ATT_TK = 512
ATT_V_ROWS = 80
ATT_BOUND_SLACK = 1.0 + 2.0 ** -7
ATT_MIN_ROW_SUM = 2.0 ** -80
NEG_INF = float("-inf")


def _cparams(*sem):
    return pltpu.CompilerParams(dimension_semantics=sem, vmem_limit_bytes=VMEM_LIMIT_BYTES)


def _ln(x, g, b):
    mu = jnp.mean(x, axis=-1, keepdims=True)
    xc = x - mu
    var = jnp.mean(xc * xc, axis=-1, keepdims=True)
    return xc * lax.rsqrt(var + LN_EPS) * g + b


def _dot(a, b):
    return jnp.dot(a, b, preferred_element_type=F32)


def _pack_halves(y):
    lo = lax.bitcast_convert_type(y[:, :HALF].astype(BF16).astype(F32), I32)
    hi = lax.bitcast_convert_type(y[:, HALF:].astype(BF16).astype(F32), I32)
    return lax.shift_right_logical(lo, 16) | (hi & jnp.int32(-65536))


def _unpack_lo(w):
    return lax.bitcast_convert_type(lax.shift_left(w, 16), F32)


def _unpack_hi(w):
    return lax.bitcast_convert_type(w & jnp.int32(-65536), F32)


def _ln_kernel(x_ref, g_ref, b_ref, o_ref):
    o_ref[...] = _ln(x_ref[...], g_ref[...], b_ref[...])


def _layer_norm(x, g, b):
    T, D = x.shape
    row = pl.BlockSpec((ROW_TILE, D), lambda i: (i, 0))
    vec = pl.BlockSpec((1, D), lambda i: (0, 0))
    return pl.pallas_call(
        _ln_kernel, grid=(T // ROW_TILE,), in_specs=[row, vec, vec], out_specs=row,
        out_shape=jax.ShapeDtypeStruct((T, D), F32), compiler_params=_cparams("parallel"),
        name="ln_in")(x, g.reshape(1, D), b.reshape(1, D))


def _even_in_kernel(x_ref, w_ref, qm_ref, km_ref, qg_ref, kg_ref, cos_ref, sin_ref,
                    a_ref, qt_ref, k_ref, vt_ref):
    tm = x_ref.shape[0]
    h = _dot(x_ref[...].astype(BF16), w_ref[...])
    a_ref[...] = h[:, :F_WIDTH].astype(BF16)
    q = h[:, F_WIDTH:F_WIDTH + Q_WIDTH]
    k = h[:, F_WIDTH + Q_WIDTH:F_WIDTH + Q_WIDTH + KV_WIDTH]
    v = h[:, F_WIDTH + Q_WIDTH + KV_WIDTH:]
    cos = cos_ref[...]
    sin = sin_ref[...]
    lane = lax.broadcasted_iota(I32, (tm, 128), 1)
    first_of_pair = (lane & ROPE_PAIRS) == 0

    def mean_sq(xf, m_ref):
        sq = xf * xf
        hi = sq.astype(BF16)
        lo = (sq - hi.astype(F32)).astype(BF16)
        return _dot(hi, m_ref[...]) + _dot(lo, m_ref[...])

    def rope(xn):
        sw = jnp.where(first_of_pair, pltpu.roll(xn, 128 - ROPE_PAIRS, 1), pltpu.roll(xn, ROPE_PAIRS, 1))
        return xn * cos + sw * sin

    qn = q * lax.rsqrt(mean_sq(q, qm_ref) + QK_EPS) * qg_ref[...]
    scale = math.log2(math.e) / math.sqrt(HEAD_DIM)
    for c in range(Q_WIDTH // 128):
        qt_ref[c * 128:(c + 1) * 128, :] = (rope(qn[:, c * 128:(c + 1) * 128]) * scale).T.astype(BF16)
    kn = rope(k * lax.rsqrt(mean_sq(k, km_ref) + QK_EPS) * kg_ref[...])
    low = lane < HEAD_DIM
    k_ref[0] = jnp.where(low, kn, 0.0).astype(BF16)
    k_ref[1] = jnp.where(low, pltpu.roll(kn, HEAD_DIM, 1), 0.0).astype(BF16)
    ones_col = jnp.where(lane == HEAD_DIM, 1.0, 0.0)
    vt_ref[0:128, :] = jnp.where(low, v, ones_col).T.astype(BF16)
    vt_ref[128:256, :] = jnp.where(low, pltpu.roll(v, HEAD_DIM, 1), ones_col).T.astype(BF16)


def _rope_tables(S):
    rows = S // GRID_W
    t = np.arange(S)
    inv = ROPE_THETA ** (-np.arange(ROPE_PAIRS, dtype=np.float64) / ROPE_PAIRS)
    ang_r = (t // GRID_W)[:, None] * inv
    ang_c = (t % GRID_W)[:, None] * inv
    del rows
    cos = np.concatenate([np.cos(ang_r), np.cos(ang_r), np.cos(ang_c), np.cos(ang_c)], axis=1)
    sin = np.concatenate([-np.sin(ang_r), np.sin(ang_r), -np.sin(ang_c), np.sin(ang_c)], axis=1)
    return (jnp.asarray(np.tile(cos, (1, 2)), F32), jnp.asarray(np.tile(sin, (1, 2)), F32))


def _head_mean_matrix(width):
    m = np.kron(np.eye(width // HEAD_DIM), np.full((HEAD_DIM, HEAD_DIM), 1.0 / HEAD_DIM))
    return jnp.asarray(m, BF16)


def _even_in(x, w_in, q_gain, k_gain, B, S):
    T, D = x.shape
    tm = ROW_TILE
    ns = S // tm
    cos, sin = _rope_tables(S)
    row = lambda w: pl.BlockSpec((tm, w), lambda i: (i, 0))
    full = lambda a: pl.BlockSpec(a.shape, lambda i: (0,) * a.ndim)
    tab = pl.BlockSpec((tm, 128), lambda i: (i % ns, 0))
    w = w_in.astype(BF16)
    qm = _head_mean_matrix(Q_WIDTH)
    km = _head_mean_matrix(KV_WIDTH)
    qg = jnp.tile(q_gain.astype(F32), N_HEADS).reshape(1, Q_WIDTH)
    kg = jnp.tile(k_gain.astype(F32), N_KV_HEADS).reshape(1, KV_WIDTH)
    return pl.pallas_call(
        _even_in_kernel, grid=(T // tm,),
        in_specs=[row(D), full(w), full(qm), full(km), full(qg), full(kg), tab, tab],
        out_specs=[row(F_WIDTH),
                   pl.BlockSpec((None, Q_WIDTH, tm), lambda i: (i // ns, 0, i % ns)),
                   pl.BlockSpec((N_KV_HEADS, tm, 128), lambda i: (0, i, 0)),
                   pl.BlockSpec((None, N_KV_HEADS * 128, tm), lambda i: (i // ns, 0, i % ns))],
        out_shape=[jax.ShapeDtypeStruct((T, F_WIDTH), BF16),
                   jax.ShapeDtypeStruct((B, Q_WIDTH, S), BF16),
                   jax.ShapeDtypeStruct((N_KV_HEADS, T, 128), BF16),
                   jax.ShapeDtypeStruct((B, N_KV_HEADS * 128, S), BF16)],
        compiler_params=_cparams("parallel"), name="even_in")(x, w, qm, km, qg, kg, cos, sin)


def _fourier_kernel(a_ref, dftc_ref, taba_ref, kc_ref, ks_ref, wf_ref, o_ref,
                    zr_ref, zi_ref, ur_ref, ui_ref, y_ref):
    S = a_ref.shape[0]
    n1_count = DFT_N1
    n2_count = S // DFT_N1
    pz = n1_count + DFT_PITCH_PAD
    pu = n2_count + DFT_PITCH_PAD
    blk = DFT_KRON * DFT_N1
    scale = 1.0 / math.sqrt(S * FGROUP_DIM)

    def channel_dft(j, carry):
        zz = _dot(a_ref[pl.ds(pl.multiple_of(j * blk, blk), blk), :], dftc_ref[...])
        for q in range(DFT_KRON):
            dst = pl.ds(pl.multiple_of((j * DFT_KRON + q) * pz, 8), n1_count)
            zr_ref[dst, :] = zz[q * n1_count:(q + 1) * n1_count, :FGROUP_DIM]
            zi_ref[dst, :] = zz[q * n1_count:(q + 1) * n1_count, FGROUP_DIM:]
        return carry

    lax.fori_loop(0, S // blk, channel_dft, 0, unroll=2)

    def stage_a(n1, carry):
        src = pl.ds(n1, n2_count, stride=pz)
        zn = jnp.concatenate([zr_ref[src, :], zi_ref[src, :]], axis=1).astype(BF16)
        r = _dot(taba_ref[n1], zn)
        dst = pl.ds(pl.multiple_of(n1 * pu, 8), n2_count)
        ur_ref[dst, :] = r[:n2_count, :FGROUP_DIM] + r[n2_count:, FGROUP_DIM:]
        ui_ref[dst, :] = r[:n2_count, FGROUP_DIM:] - r[n2_count:, :FGROUP_DIM]
        return carry

    lax.fori_loop(0, n1_count, stage_a, 0, unroll=4)

    def stage_b(j, carry):
        srcs = [pl.ds(j * DFT_KRON + q, n1_count, stride=pu) for q in range(DFT_KRON)]
        ur = jnp.concatenate([ur_ref[s, :] for s in srcs], axis=0).astype(BF16)
        ui = jnp.concatenate([ui_ref[s, :] for s in srcs], axis=0).astype(BF16)
        re = _dot(kc_ref[...], ur) + _dot(ks_ref[...], ui)
        out = _dot((re * scale).astype(BF16), wf_ref[...])
        for q in range(DFT_KRON):
            y_ref[srcs[q], :] = out[q * n1_count:(q + 1) * n1_count]
        return carry

    lax.fori_loop(0, S // blk, stage_b, 0, unroll=4)

    def compact(k1, carry):
        o_ref[pl.ds(pl.multiple_of(k1 * n2_count, n2_count), n2_count), :] = (
            y_ref[pl.ds(pl.multiple_of(k1 * pu, 8), n2_count), :].astype(BF16))
        return carry

    lax.fori_loop(0, n1_count, compact, 0)


def _dft_tables(S):
    n1c, n2c = DFT_N1, S // DFT_N1
    c = np.arange(FGROUP_DIM)
    ang = 2 * np.pi * np.outer(c, c) / FGROUP_DIM
    dftc = np.concatenate([np.cos(ang), -np.sin(ang)], axis=1)
    n1 = np.arange(n1c)[:, None, None]
    k2 = np.arange(n2c)[None, :, None]
    n2 = np.arange(n2c)[None, None, :]
    th = 2 * np.pi * (n2 * k2 / n2c + n1 * k2 / S)
    taba = np.concatenate([np.cos(th), np.sin(th)], axis=1)
    k1 = np.arange(n1c)
    g = 2 * np.pi * np.outer(k1, k1) / n1c
    eye = np.eye(DFT_KRON)
    kc = np.kron(eye, np.cos(g))
    ks = np.kron(eye, np.sin(g))
    return tuple(jnp.asarray(t, BF16) for t in (dftc, taba, kc, ks))


def _fourier(a, w_fourier, B, S):
    T = a.shape[0]
    dftc, taba, kc, ks = _dft_tables(S)
    full = lambda t: pl.BlockSpec(t.shape, lambda b, g: (0,) * t.ndim)
    blk = pl.BlockSpec((S, FGROUP_DIM), lambda b, g: (b, g))
    return pl.pallas_call(
        _fourier_kernel, grid=(B, N_FGROUPS),
        in_specs=[blk, full(dftc), full(taba), full(kc), full(ks),
                  pl.BlockSpec((None, FGROUP_DIM, FGROUP_DIM), lambda b, g: (g, 0, 0))],
        out_specs=blk,
        out_shape=jax.ShapeDtypeStruct((T, F_WIDTH), BF16),
        scratch_shapes=(
            [pltpu.VMEM((S // DFT_N1 * (DFT_N1 + DFT_PITCH_PAD), FGROUP_DIM), F32)] * 2
            + [pltpu.VMEM((DFT_N1 * (S // DFT_N1 + DFT_PITCH_PAD), FGROUP_DIM), F32)] * 3),
        compiler_params=_cparams("parallel", "parallel"), name="fourier")(
            a, dftc, taba, kc, ks, w_fourier.astype(BF16))


def _attn_kernel(qt_ref, k_ref, vt_ref, o_ref, qs_ref, kmax_ref, acc_ref, m_ref, s0_ref, s1_ref,
                 p0_ref, p1_ref):
    tq = qt_ref.shape[1]
    n_keys = k_ref.shape[0]
    tk = min(ATT_TK, n_keys)
    n_chunks = n_keys // tk
    assert n_chunks % 2 == 0 and n_chunks * tk == n_keys

    def keys(c):
        return k_ref[pl.ds(pl.multiple_of(c * tk, tk), tk), :]

    @pl.when(pl.program_id(2) == 0)
    def _():
        def body(c, best):
            k = keys(c).astype(F32)
            return jnp.maximum(best, jnp.sum(k * k, axis=1, keepdims=True))
        best = lax.fori_loop(0, n_chunks, body, jnp.zeros((tk, 1), F32))
        kmax_ref[...] = jnp.broadcast_to(jnp.sqrt(jnp.max(best, axis=0, keepdims=True)), kmax_ref.shape)

    qs_ref[HEAD_DIM:, :] = jnp.zeros((128 - HEAD_DIM, Q_GROUP * tq), BF16)
    for g in range(Q_GROUP):
        qs_ref[:HEAD_DIM, g * tq:(g + 1) * tq] = qt_ref[g * HEAD_DIM:(g + 1) * HEAD_DIM, :]
    qf = qs_ref[...].astype(F32)
    bound = jnp.sqrt(jnp.sum(qf * qf, axis=0, keepdims=True)) * kmax_ref[0:1, 0:1] * ATT_BOUND_SLACK

    def scores(c):
        return _dot(keys(c), qs_ref[...])

    def values(c):
        return vt_ref[:ATT_V_ROWS, pl.ds(pl.multiple_of(c * tk, tk), tk)]

    def weights(s_buf):
        return jnp.exp2(s_buf[...] - bound).astype(BF16)

    def accumulate(p_buf, c):
        acc_ref[...] += _dot(values(c), p_buf[...])

    last = n_chunks - 1

    def fast(c2, carry):
        c = 2 * c2
        s0_ref[...] = scores(jnp.minimum(c + 2, last))
        p1_ref[...] = weights(s1_ref)
        accumulate(p0_ref, c)
        s1_ref[...] = scores(jnp.minimum(c + 3, last))
        p0_ref[...] = weights(s0_ref)
        accumulate(p1_ref, c + 1)
        return carry

    acc_ref[...] = jnp.zeros(acc_ref.shape, F32)
    s0_ref[...] = scores(0)
    p0_ref[...] = weights(s0_ref)
    s1_ref[...] = scores(1)
    lax.fori_loop(0, n_chunks // 2, fast, 0)
    underflow = jnp.min(acc_ref[HEAD_DIM:HEAD_DIM + 1, :]) < ATT_MIN_ROW_SUM

    @pl.when(underflow)
    def _():
        def safe(c, carry):
            s = scores(c)
            m_old = m_ref[...]
            m_new = jnp.maximum(m_old, jnp.max(s, axis=0, keepdims=True))
            acc_ref[...] = (jnp.exp2(m_old - m_new) * acc_ref[...]
                            + _dot(values(c), jnp.exp2(s - m_new).astype(BF16)))
            m_ref[...] = m_new
            return carry

        m_ref[...] = jnp.full(m_ref.shape, NEG_INF, F32)
        acc_ref[...] = jnp.zeros(acc_ref.shape, F32)
        lax.fori_loop(0, n_chunks, safe, 0)

    acc = acc_ref[...]
    ot = acc[:HEAD_DIM, :] / acc[HEAD_DIM:HEAD_DIM + 1, :]
    ot = jnp.concatenate([ot, jnp.zeros((128 - HEAD_DIM, Q_GROUP * tq), F32)], axis=0)
    o = ot.T
    o_ref[...] = jnp.concatenate([o[g * tq:(g + 1) * tq, :HEAD_DIM] for g in range(Q_GROUP)],
                                 axis=1).astype(BF16)


def _attention(qt, k2, vt, B, S):
    T = B * S
    tq = ATT_TQ
    nq = S // tq
    gw = Q_GROUP * HEAD_DIM
    cols = Q_GROUP * tq
    tk = min(ATT_TK, S)
    return pl.pallas_call(
        _attn_kernel, grid=(B, N_KV_HEADS, nq),
        in_specs=[pl.BlockSpec((None, gw, tq), lambda b, h, i: (b, h, i)),
                  pl.BlockSpec((None, None, S, 128), lambda b, h, i: (h, b, 0, 0)),
                  pl.BlockSpec((None, 128, S), lambda b, h, i: (b, h, 0))],
        out_specs=pl.BlockSpec((tq, gw), lambda b, h, i: (b * nq + i, h)),
        out_shape=jax.ShapeDtypeStruct((T, Q_WIDTH), BF16),
        scratch_shapes=[pltpu.VMEM((128, cols), BF16), pltpu.VMEM((8, 128), F32),
                        pltpu.VMEM((ATT_V_ROWS, cols), F32), pltpu.VMEM((1, cols), F32),
                        pltpu.VMEM((tk, cols), F32), pltpu.VMEM((tk, cols), F32),
                        pltpu.VMEM((tk, cols), BF16), pltpu.VMEM((tk, cols), BF16)],
        compiler_params=_cparams("parallel", "parallel", "arbitrary"),
        name="attention")(qt, k2.reshape(N_KV_HEADS, B, S, 128), vt)


def _even_out_kernel(a_ref, t_ref, wa_ref, wt_ref, x_ref, g_ref, b_ref, o_ref):
    mix = _dot(a_ref[...], wa_ref[...]) + _dot(t_ref[...], wt_ref[...])
    o_ref[...] = _ln(DEEPNORM_ALPHA * x_ref[...] + mix, g_ref[...], b_ref[...])


def _even_out(a_out, attn, w_out, x, g, b):
    T, D = x.shape
    tm = ROW_TILE
    row = lambda w: pl.BlockSpec((tm, w), lambda i: (i, 0))
    full = lambda a: pl.BlockSpec(a.shape, lambda i: (0,) * a.ndim)
    wa = w_out[:F_WIDTH].astype(BF16)
    wt = w_out[F_WIDTH:].astype(BF16)
    g = g.reshape(1, D)
    b = b.reshape(1, D)
    return pl.pallas_call(
        _even_out_kernel, grid=(T // tm,),
        in_specs=[row(F_WIDTH), row(Q_WIDTH), full(wa), full(wt), row(D), full(g), full(b)],
        out_specs=row(D), out_shape=jax.ShapeDtypeStruct((T, D), F32),
        compiler_params=_cparams("parallel"), name="even_out")(a_out, attn, wa, wt, x, g, b)


def _odd_kernel(x_ref, wi_ref, bi_ref, vg_ref, vb_ref, ws_ref, bs_ref, wo_ref, g_ref, b_ref, o_ref,
                gate_ref):
    tm = x_ref.shape[0]
    x = x_ref[...]
    h = _dot(x.astype(BF16), wi_ref[...]) + bi_ref[...]
    h = 0.5 * h * (1.0 + lax.erf(h * (1.0 / math.sqrt(2.0))))
    u = h[:, :C_WIDTH]
    v = _ln(h[:, C_WIDTH:], vg_ref[...], vb_ref[...]).astype(BF16)
    for c in range(tm // CHUNK):
        r0 = c * CHUNK
        for gi in range(N_CGROUPS):
            l0 = gi * CGROUP_DIM
            sv = _dot(ws_ref[gi], v[r0:r0 + CHUNK, l0:l0 + CGROUP_DIM]) + bs_ref[gi]
            gate_ref[r0:r0 + CHUNK, l0:l0 + CGROUP_DIM] = (
                u[r0:r0 + CHUNK, l0:l0 + CGROUP_DIM] * sv).astype(BF16)
    mix = _dot(gate_ref[...], wo_ref[...])
    o_ref[...] = _ln(DEEPNORM_ALPHA * x + mix, g_ref[...], b_ref[...])


def _odd_mixer(x, w_in, b_in, v_g, v_b, w_s, b_s, w_out, g, b):
    T, D = x.shape
    tm = ROW_TILE
    row = pl.BlockSpec((tm, D), lambda i: (i, 0))
    full = lambda a: pl.BlockSpec(a.shape, lambda i: (0,) * a.ndim)
    args = [w_in.astype(BF16), b_in.reshape(1, 2 * C_WIDTH), v_g.reshape(1, C_WIDTH),
            v_b.reshape(1, C_WIDTH), w_s.astype(BF16),
            jnp.broadcast_to(b_s[:, :, None], (N_CGROUPS, CHUNK, CGROUP_DIM)).astype(F32),
            w_out.astype(BF16), g.reshape(1, D), b.reshape(1, D)]
    return pl.pallas_call(
        _odd_kernel, grid=(T // tm,),
        in_specs=[row] + [full(a) for a in args],
        out_specs=row, out_shape=jax.ShapeDtypeStruct((T, D), F32),
        scratch_shapes=[pltpu.VMEM((tm, C_WIDTH), BF16)],
        compiler_params=_cparams("parallel"), name="odd_mixer")(x, *args)


def _router_kernel(x_ref, w_ref, rb_ref, tri_ref, eidx_ref, wts_ref, pos_ref, cnt_ref, xp_ref, run_ref):
    tm = x_ref.shape[0]
    i = pl.program_id(0)

    @pl.when(i == 0)
    def _():
        run_ref[...] = jnp.zeros(run_ref.shape, F32)

    x = x_ref[...]
    xp_ref[...] = _pack_halves(x)
    xh = x.astype(BF16)
    xl = (x - xh.astype(F32)).astype(BF16)
    nt = (((1,), (1,)), ((), ()))
    dg = lambda a, c: lax.dot_general(a, c, nt, preferred_element_type=F32)
    logits = dg(w_ref[0], xh) + dg(w_ref[0], xl) + dg(w_ref[1], xh)
    scores = jax.nn.sigmoid(logits)
    sel = scores + rb_ref[...]

    i8 = lax.broadcasted_iota(I32, (GROUP_SIZE, tm), 0)
    gsc_rows = []
    for gidx in range(N_EXPERT_GROUPS):
        sg = sel[gidx * GROUP_SIZE:(gidx + 1) * GROUP_SIZE, :]
        m1 = jnp.max(sg, axis=0, keepdims=True)
        f1 = jnp.min(jnp.where(sg == m1, i8, GROUP_SIZE), axis=0, keepdims=True)
        m2 = jnp.max(jnp.where(i8 == f1, NEG_INF, sg), axis=0, keepdims=True)
        gsc_rows.append(m1 + m2)
    gsc = jnp.concatenate(gsc_rows, axis=0)

    gsel = jnp.zeros(gsc.shape, F32)
    for _ in range(TOPK_GROUPS):
        m = jnp.max(gsc, axis=0, keepdims=True)
        f = jnp.min(jnp.where(gsc == m, i8, N_EXPERT_GROUPS), axis=0, keepdims=True)
        pick = i8 == f
        gsel = jnp.where(pick, 1.0, gsel)
        gsc = jnp.where(pick, NEG_INF, gsc)
    esel = jnp.concatenate(
        [jnp.broadcast_to(gsel[gidx:gidx + 1, :], (GROUP_SIZE, tm)) for gidx in range(N_EXPERT_GROUPS)],
        axis=0)

    cur = jnp.where(esel > 0.0, sel, NEG_INF)
    ei = lax.broadcasted_iota(I32, cur.shape, 0)
    idx_rows, sc_rows = [], []
    chosen = jnp.zeros(cur.shape, F32)
    for _ in range(TOP_K):
        m = jnp.max(cur, axis=0, keepdims=True)
        f = jnp.min(jnp.where(cur == m, ei, N_EXPERTS), axis=0, keepdims=True)
        pick = ei == f
        idx_rows.append(f)
        sc_rows.append(jnp.sum(jnp.where(pick, scores, 0.0), axis=0, keepdims=True))
        chosen = jnp.where(pick, 1.0, chosen)
        cur = jnp.where(pick, NEG_INF, cur)
    eidx = jnp.concatenate(idx_rows, axis=0)
    sc = jnp.concatenate(sc_rows, axis=0)
    eidx_ref[...] = eidx
    wts_ref[...] = sc / jnp.sum(sc, axis=0, keepdims=True) * ROUTE_SCALE

    before = _dot(chosen.astype(BF16), tri_ref[...]) + run_ref[...]
    pos_rows = [jnp.sum(jnp.where(ei == idx_rows[k], before, 0.0), axis=0, keepdims=True)
                for k in range(TOP_K)]
    pos_ref[...] = jnp.concatenate(pos_rows, axis=0).astype(I32)
    run_new = run_ref[...] + jnp.sum(chosen, axis=1, keepdims=True)
    run_ref[...] = run_new
    cnt_ref[...] = jnp.broadcast_to(run_new, cnt_ref.shape).astype(I32)


def _router(x, router_w, router_b, first_row, T):
    D = x.shape[1]
    tm = min(ROUTER_TILE, T)
    n_tiles = T // tm
    first_tile = first_row // tm
    assert n_tiles * tm == T and first_tile * tm == first_row
    wt = router_w.T.astype(F32)
    wh = wt.astype(BF16)
    wl = (wt - wh.astype(F32)).astype(BF16)
    w2 = jnp.stack([wh, wl])
    rb = router_b.astype(F32).reshape(N_EXPERTS, 1)
    tri = jnp.asarray(np.triu(np.ones((tm, tm)), 1), BF16)
    full = lambda a: pl.BlockSpec(a.shape, lambda i: (0,) * a.ndim)
    col = pl.BlockSpec((TOP_K, tm), lambda i: (0, i))
    return pl.pallas_call(
        _router_kernel, grid=(n_tiles,),
        in_specs=[pl.BlockSpec((tm, D), lambda i: (i + first_tile, 0)), full(w2), full(rb), full(tri)],
        out_specs=[col, col, col, pl.BlockSpec((N_EXPERTS, 128), lambda i: (0, 0)),
                   pl.BlockSpec((tm, HALF), lambda i: (i, 0))],
        out_shape=[jax.ShapeDtypeStruct((TOP_K, T), I32), jax.ShapeDtypeStruct((TOP_K, T), F32),
                   jax.ShapeDtypeStruct((TOP_K, T), I32), jax.ShapeDtypeStruct((N_EXPERTS, 128), I32),
                   jax.ShapeDtypeStruct((T, HALF), I32)],
        scratch_shapes=[pltpu.VMEM((N_EXPERTS, 1), F32)],
        compiler_params=_cparams("arbitrary"), name="router")(x, w2, rb, tri)


def _dest_kernel(start_ref, eidx_ref, pos_ref, o_ref):
    e = eidx_ref[...]
    acc = pos_ref[...]
    for j in range(N_EXPERTS):
        acc = acc + jnp.where(e == j, start_ref[j], 0)
    o_ref[...] = acc


def _dest_rows(eidx, pos, seg_start):
    K, T = eidx.shape
    tl = min(T, 2048)
    blk = pl.BlockSpec((K, tl), lambda i, s: (0, i))
    grid_spec = pltpu.PrefetchScalarGridSpec(
        num_scalar_prefetch=1, grid=(T // tl,), in_specs=[blk, blk], out_specs=blk)
    return pl.pallas_call(
        _dest_kernel, grid_spec=grid_spec, out_shape=jax.ShapeDtypeStruct((K, T), I32),
        compiler_params=_cparams("parallel"), name="dest_rows")(seg_start, eidx, pos)


def _gather_rows(table, idx):
    n_rows = idx.shape[0]
    width = table.shape[1]
    per_worker = n_rows // SC_WORKERS
    n_chunks = per_worker // SC_CHUNK
    assert per_worker * SC_WORKERS == n_rows and n_chunks * SC_CHUNK == per_worker
    mesh = plsc.VectorSubcoreMesh(core_axis_name="c", subcore_axis_name="s")

    @functools.partial(
        pl.kernel, mesh=mesh,
        out_type=jax.ShapeDtypeStruct((n_rows, width), table.dtype),
        scratch_types=[pltpu.VMEM((SC_CHUNK,), I32), pltpu.VMEM((SC_CHUNK, width), table.dtype),
                       pltpu.SemaphoreType.DMA])
    def gather(table_hbm, idx_hbm, out_hbm, idx_v, rows_v, sem):
        wid = lax.axis_index("s") * 2 + lax.axis_index("c")
        base = wid * per_worker

        @pl.loop(0, n_chunks)
        def _(j):
            off = base + j * SC_CHUNK
            pltpu.sync_copy(idx_hbm.at[pl.ds(off, SC_CHUNK)], idx_v)
            pltpu.async_copy(table_hbm.at[idx_v], rows_v, sem).wait()
            pltpu.sync_copy(rows_v, out_hbm.at[pl.ds(off, SC_CHUNK)])

    return gather(table, idx)


def _scatter_rows(rows, dest):
    n_tok, width = rows.shape
    n_dst = dest.shape[0]
    per_worker = n_tok // SC_WORKERS
    n_chunks = per_worker // SC_CHUNK
    assert per_worker * SC_WORKERS == n_tok and n_chunks * SC_CHUNK == per_worker
    mesh = plsc.VectorSubcoreMesh(core_axis_name="c", subcore_axis_name="s")

    @functools.partial(
        pl.kernel, mesh=mesh,
        out_type=jax.ShapeDtypeStruct((n_dst * n_tok, width), rows.dtype),
        scratch_types=[pltpu.VMEM((n_dst, SC_CHUNK), I32), pltpu.VMEM((SC_CHUNK, width), rows.dtype),
                       pltpu.SemaphoreType.DMA])
    def scatter(rows_hbm, dest_hbm, out_hbm, idx_v, rows_v, sem):
        wid = lax.axis_index("s") * 2 + lax.axis_index("c")
        base = wid * per_worker

        @pl.loop(0, n_chunks)
        def _(j):
            off = base + j * SC_CHUNK
            pltpu.sync_copy(dest_hbm.at[:, pl.ds(off, SC_CHUNK)], idx_v)
            pltpu.sync_copy(rows_hbm.at[pl.ds(off, SC_CHUNK)], rows_v)
            copies = [pltpu.async_copy(rows_v, out_hbm.at[idx_v.at[k]], sem) for k in range(n_dst)]
            for c in copies:
                c.wait()

    return scatter(rows, dest)


def _expert_kernel(blk_ref, exp_ref, lo_ref, hi_ref, slot_ref, nxt_ref, xs_hbm, wg_hbm, wu_hbm, wd_hbm,
                   ys_ref, wg_buf, wu_buf, wd_buf, wgu_s, wd_s, sem, xs_buf, xs_sem, *, layer):
    i = pl.program_id(0)
    prev = jnp.maximum(i - 1, 0)

    def weight_copies(expert, slot):
        return [pltpu.make_async_copy(src.at[layer, expert], dst.at[slot], sem.at[slot])
                for src, dst in ((wg_hbm, wg_buf), (wu_hbm, wu_buf), (wd_hbm, wd_buf))]

    @pl.when(i == 0)
    def _():
        for c in weight_copies(exp_ref[0], slot_ref[0]):
            c.start()

    @pl.when(jnp.logical_or(i == 0, exp_ref[i] != exp_ref[prev]))
    def _():
        slot = slot_ref[i]
        for c in weight_copies(exp_ref[i], slot):
            c.wait()
        wgu_s[:, :EXPERT_DIM] = wg_buf[slot].astype(BF16)
        wgu_s[:, EXPERT_DIM:] = wu_buf[slot].astype(BF16)
        wd_s[...] = wd_buf[slot].astype(BF16)

        @pl.when(nxt_ref[i] >= 0)
        def _():
            for c in weight_copies(nxt_ref[i], 1 - slot):
                c.start()

    n_blocks = xs_hbm.shape[0] // EXPERT_ROWS
    blk = blk_ref[i]
    first = jnp.logical_or(i == 0, blk != blk_ref[prev])

    def rows_copy(block):
        slot = block % EXPERT_XS_SLOTS
        src = xs_hbm.at[pl.ds(pl.multiple_of(block * EXPERT_ROWS, EXPERT_ROWS), EXPERT_ROWS)]
        return pltpu.make_async_copy(src, xs_buf.at[slot], xs_sem.at[slot])

    @pl.when(i == 0)
    def _():
        for b0 in range(min(EXPERT_XS_SLOTS - 1, n_blocks)):
            rows_copy(b0).start()

    @pl.when(first)
    def _():
        rows_copy(blk).wait()

        @pl.when(blk + EXPERT_XS_SLOTS - 1 < n_blocks)
        def _():
            rows_copy(blk + EXPERT_XS_SLOTS - 1).start()

    xs_ref = xs_buf.at[blk % EXPERT_XS_SLOTS]
    lo = lo_ref[i]
    hi = hi_ref[i]

    def sub_block(r0):
        rows = slice(r0, r0 + EXPERT_SUB_ROWS)
        w = xs_ref[rows, :]
        xlo = lax.bitcast_convert_type(w.astype(jnp.int16), BF16)
        xhi = lax.bitcast_convert_type(lax.shift_right_logical(w, 16).astype(jnp.int16), BF16)
        gu = _dot(xlo, wgu_s[:HALF, :]) + _dot(xhi, wgu_s[HALF:, :])
        g = gu[:, :EXPERT_DIM]
        hb = (g * jax.nn.sigmoid(g) * gu[:, EXPERT_DIM:]).astype(BF16)
        y = _dot(hb, wd_s[...])
        packed = pltpu.pack_elementwise([y[:, :HALF], y[:, HALF:]], packed_dtype=BF16)
        row = r0 + lax.broadcasted_iota(I32, (EXPERT_SUB_ROWS, 1), 0)
        mine = jnp.logical_and(row >= lo, row < hi)
        kept = jnp.where(first, 0, ys_ref[rows, :])
        ys_ref[rows, :] = jnp.where(mine, packed, kept)

    for r0 in range(0, EXPERT_ROWS, EXPERT_SUB_ROWS):
        pl.when(jnp.logical_and(lo < r0 + EXPERT_SUB_ROWS, hi > r0))(functools.partial(sub_block, r0))


def _expert_items(counts, n_rows):
    n_blocks = n_rows // EXPERT_ROWS
    n_items = n_blocks + N_EXPERTS - 1
    end = jnp.cumsum(counts)
    start = end - counts
    first_blk = start // EXPERT_ROWS
    n_blk = jnp.where(counts > 0, (end - 1) // EXPERT_ROWS - first_blk + 1, 0)
    item_end = jnp.cumsum(n_blk)
    item_start = item_end - n_blk
    slot = jnp.arange(n_items, dtype=I32)
    e = jnp.minimum(jnp.sum((item_end[None, :] <= slot[:, None]).astype(I32), axis=1), N_EXPERTS - 1)
    onehot = (e[:, None] == jnp.arange(N_EXPERTS, dtype=I32)[None, :]).astype(I32)
    pick = lambda v: jnp.sum(onehot * v[None, :], axis=1)
    valid = slot < item_end[-1]
    blk = jnp.where(valid, pick(first_blk) + slot - pick(item_start), n_blocks - 1)
    lo = jnp.clip(pick(start) - blk * EXPERT_ROWS, 0, EXPERT_ROWS)
    hi = jnp.clip(pick(end) - blk * EXPERT_ROWS, 0, EXPERT_ROWS)
    last_e = jnp.max(jnp.where(counts > 0, jnp.arange(N_EXPERTS, dtype=I32), 0))
    e = jnp.where(valid, e, last_e)
    hi = jnp.where(valid, hi, 0)
    lo = jnp.where(valid, lo, 0)
    change = jnp.concatenate([jnp.ones((1,), I32), (e[1:] != e[:-1]).astype(I32)])
    slot = (jnp.cumsum(change) - 1) % 2
    later = jnp.where(jnp.arange(N_EXPERTS, dtype=I32)[None, :] > e[:, None], counts[None, :] > 0, False)
    nxt = jnp.where(jnp.any(later, axis=1), jnp.argmax(later, axis=1), -1)
    return tuple(a.astype(I32) for a in (blk, e, lo, hi, slot, nxt))


def _experts(xs, items, w_gate, w_up, w_down, layer):
    n_rows = xs.shape[0]
    n_items = items[0].shape[0]
    blk_map = lambda i, b, *_: (b[i], 0)
    hbm = pl.BlockSpec(memory_space=pl.ANY)
    grid_spec = pltpu.PrefetchScalarGridSpec(
        num_scalar_prefetch=len(items), grid=(n_items,),
        in_specs=[hbm, hbm, hbm, hbm],
        out_specs=pl.BlockSpec((EXPERT_ROWS, HALF), blk_map),
        scratch_shapes=[pltpu.VMEM((2, D_MODEL, EXPERT_DIM), F32),
                        pltpu.VMEM((2, D_MODEL, EXPERT_DIM), F32),
                        pltpu.VMEM((2, EXPERT_DIM, D_MODEL), F32),
                        pltpu.VMEM((D_MODEL, 2 * EXPERT_DIM), BF16),
                        pltpu.VMEM((EXPERT_DIM, D_MODEL), BF16),
                        pltpu.SemaphoreType.DMA((2,)),
                        pltpu.VMEM((EXPERT_XS_SLOTS, EXPERT_ROWS, HALF), I32),
                        pltpu.SemaphoreType.DMA((EXPERT_XS_SLOTS,))])
    return pl.pallas_call(
        functools.partial(_expert_kernel, layer=layer), grid_spec=grid_spec,
        out_shape=jax.ShapeDtypeStruct((n_rows, HALF), I32),
        compiler_params=_cparams("arbitrary"), name="experts")(*items, xs, w_gate, w_up, w_down)


def _moe_out_kernel(x_ref, yg_ref, wt_ref, sgu_ref, sd_ref, g_ref, b_ref, *rest):
    o_ref = rest[-1]
    x = x_ref[...]
    wt = wt_ref[...]
    lo = jnp.zeros((x.shape[0], HALF), F32)
    hi = jnp.zeros((x.shape[0], HALF), F32)
    for k in range(TOP_K):
        w = yg_ref[k]
        wk = wt[:, k:k + 1]
        lo = lo + wk * _unpack_lo(w)
        hi = hi + wk * _unpack_hi(w)
    gu = _dot(x.astype(BF16), sgu_ref[...])
    g = gu[:, :EXPERT_DIM]
    hs = (g * jax.nn.sigmoid(g) * gu[:, EXPERT_DIM:]).astype(BF16)
    ffn = jnp.concatenate([lo, hi], axis=1) + _dot(hs, sd_ref[...])
    o_ref[...] = _ln(DEEPNORM_ALPHA * x + ffn, g_ref[...], b_ref[...])


def _moe_out(x, yg, wts, sh_gate, sh_up, sh_down, g, b, first_tile, partial_out):
    T, D = x.shape
    tm = ROW_TILE
    n_tiles = yg.shape[1] // tm
    xrow = pl.BlockSpec((tm, D), lambda i: (i + first_tile, 0))
    full = lambda a: pl.BlockSpec(a.shape, lambda i: (0,) * a.ndim)
    sgu = jnp.concatenate([sh_gate, sh_up], axis=1).astype(BF16)
    sd = sh_down.astype(BF16)
    g = g.reshape(1, D)
    b = b.reshape(1, D)
    args = [x, yg, wts, sgu, sd, g, b]
    in_specs = [xrow, pl.BlockSpec((TOP_K, tm, HALF), lambda i: (0, i, 0)),
                pl.BlockSpec((tm, TOP_K), lambda i: (i, 0)), full(sgu), full(sd), full(g), full(b)]
    aliases = {}
    if partial_out is not None:
        args.append(partial_out)
        in_specs.append(pl.BlockSpec(memory_space=pl.ANY))
        aliases = {len(args) - 1: 0}
    return pl.pallas_call(
        _moe_out_kernel, grid=(n_tiles,), in_specs=in_specs,
        out_specs=xrow, out_shape=jax.ShapeDtypeStruct((T, D), F32),
        input_output_aliases=aliases,
        compiler_params=_cparams("parallel"), name="moe_out")(*args)


def _moe(x, router_w, router_b, w_gate, w_up, w_down, layer, sh_gate, sh_up, sh_down, g, b):
    T = x.shape[0]
    tiles = T // ROW_TILE // MOE_TOKEN_GROUPS
    tg = tiles * ROW_TILE
    out = None
    for grp in range(MOE_TOKEN_GROUPS):
        eidx, wts, pos, cnt, xp = _router(x, router_w, router_b, grp * tg, tg)
        counts = cnt[:, 0]
        seg_start = (jnp.cumsum(counts) - counts).astype(I32)
        dest = _dest_rows(eidx, pos, seg_start)
        xs = _scatter_rows(xp, dest)
        ys = _experts(xs, _expert_items(counts, tg * TOP_K), w_gate, w_up, w_down, layer)
        yg = _gather_rows(ys, dest.reshape(tg * TOP_K)).reshape(TOP_K, tg, HALF)
        out = _moe_out(x, yg, wts.T, sh_gate, sh_up, sh_down, g, b, grp * tiles, out)
    return out


def kernel(x, ln_in_g, ln_in_b, e_w_in, e_w_fourier, e_q_gain, e_k_gain, e_w_out, o_w_in, o_b_in, o_v_ln_g, o_v_ln_b, o_w_spatial, o_b_spatial, o_w_out, ln_mix_g, ln_mix_b, ln_ffn_g, ln_ffn_b, router_w, router_b, exp_w_gate, exp_w_up, exp_w_down, sh_w_gate, sh_w_up, sh_w_down):
    B, S, D = x.shape
    Bc = B // BATCH_CHAINS
    T = Bc * S
    hs = [_layer_norm(xc.reshape(T, D), ln_in_g, ln_in_b) for xc in jnp.split(x, BATCH_CHAINS, axis=0)]
    for i in range(DEPTH):
        j = i // 2
        for c, h in enumerate(hs):
            if i % 2 == 0:
                a, qt, k2, vt = _even_in(h, e_w_in[j], e_q_gain[j], e_k_gain[j], Bc, S)
                a_out = _fourier(a, e_w_fourier[j], Bc, S)
                attn = _attention(qt, k2, vt, Bc, S)
                h = _even_out(a_out, attn, e_w_out[j], h, ln_mix_g[i], ln_mix_b[i])
            else:
                h = _odd_mixer(h, o_w_in[j], o_b_in[j], o_v_ln_g[j], o_v_ln_b[j], o_w_spatial[j],
                               o_b_spatial[j], o_w_out[j], ln_mix_g[i], ln_mix_b[i])
            hs[c] = _moe(h, router_w[i], router_b[i], exp_w_gate, exp_w_up, exp_w_down, i,
                         sh_w_gate[i], sh_w_up[i], sh_w_down[i], ln_ffn_g[i], ln_ffn_b[i])
    return jnp.concatenate(hs, axis=0).reshape(B, S, D)
```

```python
import functools
import math

import numpy as np
import jax
import jax.numpy as jnp
from jax import lax
from jax.experimental import pallas as pl
from jax.experimental.pallas import tpu as pltpu
from jax.experimental.pallas import tpu_sc as plsc

F32 = jnp.float32
BF16 = jnp.bfloat16
I32 = jnp.int32

D_MODEL = 1024
DEPTH = 4
GRID_W = 64
N_FGROUPS = 4
FGROUP_DIM = 128
F_WIDTH = N_FGROUPS * FGROUP_DIM
N_HEADS = 8
N_KV_HEADS = 2
HEAD_DIM = 64
Q_GROUP = N_HEADS // N_KV_HEADS
Q_WIDTH = N_HEADS * HEAD_DIM
KV_WIDTH = N_KV_HEADS * HEAD_DIM
ROPE_THETA = 10000.0
ROPE_PAIRS = HEAD_DIM // 4
EVEN_IN_WIDTH = F_WIDTH + Q_WIDTH + 2 * KV_WIDTH
CHUNK = 128
N_CGROUPS = 8
CGROUP_DIM = D_MODEL // N_CGROUPS
C_WIDTH = N_CGROUPS * CGROUP_DIM
N_EXPERTS = 64
EXPERT_DIM = 256
TOP_K = 8
N_EXPERT_GROUPS = 8
GROUP_SIZE = N_EXPERTS // N_EXPERT_GROUPS
TOPK_GROUPS = 4
ROUTE_SCALE = 2.5
LN_EPS = 1e-5
QK_EPS = 1e-6
DEEPNORM_ALPHA = (2 * DEPTH) ** 0.25

VMEM_LIMIT_BYTES = 56 * 1024 * 1024
ROW_TILE = 512
DFT_N1 = 64
DFT_KRON = 4
DFT_PITCH_PAD = 8
ROUTER_TILE = 1024
EXPERT_ROWS = 2048
EXPERT_XS_SLOTS = 3
EXPERT_SUB_ROWS = 512
HALF = D_MODEL // 2
SC_WORKERS = 32
SC_CHUNK = 128
BATCH_CHAINS = 1
MOE_TOKEN_GROUPS = 2
ATT_TQ = 256
ATT_TK = 512
ATT_V_ROWS = 80
ATT_BOUND_SLACK = 1.0 + 2.0 ** -7
ATT_MIN_ROW_SUM = 2.0 ** -80
NEG_INF = float("-inf")


def _cparams(*sem):
    return pltpu.CompilerParams(dimension_semantics=sem, vmem_limit_bytes=VMEM_LIMIT_BYTES)


def _ln(x, g, b):
    mu = jnp.mean(x, axis=-1, keepdims=True)
    xc = x - mu
    var = jnp.mean(xc * xc, axis=-1, keepdims=True)
    return xc * lax.rsqrt(var + LN_EPS) * g + b


def _dot(a, b):
    return jnp.dot(a, b, preferred_element_type=F32)


def _pack_halves(y):
    lo = lax.bitcast_convert_type(y[:, :HALF].astype(BF16).astype(F32), I32)
    hi = lax.bitcast_convert_type(y[:, HALF:].astype(BF16).astype(F32), I32)
    return lax.shift_right_logical(lo, 16) | (hi & jnp.int32(-65536))


def _unpack_lo(w):
    return lax.bitcast_convert_type(lax.shift_left(w, 16), F32)


def _unpack_hi(w):
    return lax.bitcast_convert_type(w & jnp.int32(-65536), F32)


def _ln_kernel(x_ref, g_ref, b_ref, o_ref):
    o_ref[...] = _ln(x_ref[...], g_ref[...], b_ref[...])


def _layer_norm(x, g, b):
    T, D = x.shape
    row = pl.BlockSpec((ROW_TILE, D), lambda i: (i, 0))
    vec = pl.BlockSpec((1, D), lambda i: (0, 0))
    return pl.pallas_call(
        _ln_kernel, grid=(T // ROW_TILE,), in_specs=[row, vec, vec], out_specs=row,
        out_shape=jax.ShapeDtypeStruct((T, D), F32), compiler_params=_cparams("parallel"),
        name="ln_in")(x, g.reshape(1, D), b.reshape(1, D))


def _even_in_kernel(x_ref, w_ref, qm_ref, km_ref, qg_ref, kg_ref, cos_ref, sin_ref,
                    a_ref, qt_ref, k_ref, vt_ref):
    tm = x_ref.shape[0]
    h = _dot(x_ref[...].astype(BF16), w_ref[...])
    a_ref[...] = h[:, :F_WIDTH].astype(BF16)
    q = h[:, F_WIDTH:F_WIDTH + Q_WIDTH]
    k = h[:, F_WIDTH + Q_WIDTH:F_WIDTH + Q_WIDTH + KV_WIDTH]
    v = h[:, F_WIDTH + Q_WIDTH + KV_WIDTH:]
    cos = cos_ref[...]
    sin = sin_ref[...]
    lane = lax.broadcasted_iota(I32, (tm, 128), 1)
    first_of_pair = (lane & ROPE_PAIRS) == 0

    def mean_sq(xf, m_ref):
        sq = xf * xf
        hi = sq.astype(BF16)
        lo = (sq - hi.astype(F32)).astype(BF16)
        return _dot(hi, m_ref[...]) + _dot(lo, m_ref[...])

    def rope(xn):
        sw = jnp.where(first_of_pair, pltpu.roll(xn, 128 - ROPE_PAIRS, 1), pltpu.roll(xn, ROPE_PAIRS, 1))
        return xn * cos + sw * sin

    qn = q * lax.rsqrt(mean_sq(q, qm_ref) + QK_EPS) * qg_ref[...]
    scale = math.log2(math.e) / math.sqrt(HEAD_DIM)
    for c in range(Q_WIDTH // 128):
        qt_ref[c * 128:(c + 1) * 128, :] = (rope(qn[:, c * 128:(c + 1) * 128]) * scale).T.astype(BF16)
    kn = rope(k * lax.rsqrt(mean_sq(k, km_ref) + QK_EPS) * kg_ref[...])
    low = lane < HEAD_DIM
    k_ref[0] = jnp.where(low, kn, 0.0).astype(BF16)
    k_ref[1] = jnp.where(low, pltpu.roll(kn, HEAD_DIM, 1), 0.0).astype(BF16)
    ones_col = jnp.where(lane == HEAD_DIM, 1.0, 0.0)
    vt_ref[0:128, :] = jnp.where(low, v, ones_col).T.astype(BF16)
    vt_ref[128:256, :] = jnp.where(low, pltpu.roll(v, HEAD_DIM, 1), ones_col).T.astype(BF16)


def _rope_tables(S):
    rows = S // GRID_W
    t = np.arange(S)
    inv = ROPE_THETA ** (-np.arange(ROPE_PAIRS, dtype=np.float64) / ROPE_PAIRS)
    ang_r = (t // GRID_W)[:, None] * inv
    ang_c = (t % GRID_W)[:, None] * inv
    del rows
    cos = np.concatenate([np.cos(ang_r), np.cos(ang_r), np.cos(ang_c), np.cos(ang_c)], axis=1)
    sin = np.concatenate([-np.sin(ang_r), np.sin(ang_r), -np.sin(ang_c), np.sin(ang_c)], axis=1)
    return (jnp.asarray(np.tile(cos, (1, 2)), F32), jnp.asarray(np.tile(sin, (1, 2)), F32))


def _head_mean_matrix(width):
    m = np.kron(np.eye(width // HEAD_DIM), np.full((HEAD_DIM, HEAD_DIM), 1.0 / HEAD_DIM))
    return jnp.asarray(m, BF16)


def _even_in(x, w_in, q_gain, k_gain, B, S):
    T, D = x.shape
    tm = ROW_TILE
    ns = S // tm
    cos, sin = _rope_tables(S)
    row = lambda w: pl.BlockSpec((tm, w), lambda i: (i, 0))
    full = lambda a: pl.BlockSpec(a.shape, lambda i: (0,) * a.ndim)
    tab = pl.BlockSpec((tm, 128), lambda i: (i % ns, 0))
    w = w_in.astype(BF16)
    qm = _head_mean_matrix(Q_WIDTH)
    km = _head_mean_matrix(KV_WIDTH)
    qg = jnp.tile(q_gain.astype(F32), N_HEADS).reshape(1, Q_WIDTH)
    kg = jnp.tile(k_gain.astype(F32), N_KV_HEADS).reshape(1, KV_WIDTH)
    return pl.pallas_call(
        _even_in_kernel, grid=(T // tm,),
        in_specs=[row(D), full(w), full(qm), full(km), full(qg), full(kg), tab, tab],
        out_specs=[row(F_WIDTH),
                   pl.BlockSpec((None, Q_WIDTH, tm), lambda i: (i // ns, 0, i % ns)),
                   pl.BlockSpec((N_KV_HEADS, tm, 128), lambda i: (0, i, 0)),
                   pl.BlockSpec((None, N_KV_HEADS * 128, tm), lambda i: (i // ns, 0, i % ns))],
        out_shape=[jax.ShapeDtypeStruct((T, F_WIDTH), BF16),
                   jax.ShapeDtypeStruct((B, Q_WIDTH, S), BF16),
                   jax.ShapeDtypeStruct((N_KV_HEADS, T, 128), BF16),
                   jax.ShapeDtypeStruct((B, N_KV_HEADS * 128, S), BF16)],
        compiler_params=_cparams("parallel"), name="even_in")(x, w, qm, km, qg, kg, cos, sin)


def _fourier_kernel(a_ref, dftc_ref, taba_ref, kc_ref, ks_ref, wf_ref, o_ref,
                    zr_ref, zi_ref, ur_ref, ui_ref, y_ref):
    S = a_ref.shape[0]
    n1_count = DFT_N1
    n2_count = S // DFT_N1
    pz = n1_count + DFT_PITCH_PAD
    pu = n2_count + DFT_PITCH_PAD
    blk = DFT_KRON * DFT_N1
    scale = 1.0 / math.sqrt(S * FGROUP_DIM)

    def channel_dft(j, carry):
        zz = _dot(a_ref[pl.ds(pl.multiple_of(j * blk, blk), blk), :], dftc_ref[...])
        for q in range(DFT_KRON):
            dst = pl.ds(pl.multiple_of((j * DFT_KRON + q) * pz, 8), n1_count)
            zr_ref[dst, :] = zz[q * n1_count:(q + 1) * n1_count, :FGROUP_DIM]
            zi_ref[dst, :] = zz[q * n1_count:(q + 1) * n1_count, FGROUP_DIM:]
        return carry

    lax.fori_loop(0, S // blk, channel_dft, 0, unroll=2)

    def stage_a(n1, carry):
        src = pl.ds(n1, n2_count, stride=pz)
        zn = jnp.concatenate([zr_ref[src, :], zi_ref[src, :]], axis=1).astype(BF16)
        r = _dot(taba_ref[n1], zn)
        dst = pl.ds(pl.multiple_of(n1 * pu, 8), n2_count)
        ur_ref[dst, :] = r[:n2_count, :FGROUP_DIM] + r[n2_count:, FGROUP_DIM:]
        ui_ref[dst, :] = r[:n2_count, FGROUP_DIM:] - r[n2_count:, :FGROUP_DIM]
        return carry

    lax.fori_loop(0, n1_count, stage_a, 0, unroll=4)

    def stage_b(j, carry):
        srcs = [pl.ds(j * DFT_KRON + q, n1_count, stride=pu) for q in range(DFT_KRON)]
        ur = jnp.concatenate([ur_ref[s, :] for s in srcs], axis=0).astype(BF16)
        ui = jnp.concatenate([ui_ref[s, :] for s in srcs], axis=0).astype(BF16)
        re = _dot(kc_ref[...], ur) + _dot(ks_ref[...], ui)
        out = _dot((re * scale).astype(BF16), wf_ref[...])
        for q in range(DFT_KRON):
            y_ref[srcs[q], :] = out[q * n1_count:(q + 1) * n1_count]
        return carry

    lax.fori_loop(0, S // blk, stage_b, 0, unroll=4)

    def compact(k1, carry):
        o_ref[pl.ds(pl.multiple_of(k1 * n2_count, n2_count), n2_count), :] = (
            y_ref[pl.ds(pl.multiple_of(k1 * pu, 8), n2_count), :].astype(BF16))
        return carry

    lax.fori_loop(0, n1_count, compact, 0)


def _dft_tables(S):
    n1c, n2c = DFT_N1, S // DFT_N1
    c = np.arange(FGROUP_DIM)
    ang = 2 * np.pi * np.outer(c, c) / FGROUP_DIM
    dftc = np.concatenate([np.cos(ang), -np.sin(ang)], axis=1)
    n1 = np.arange(n1c)[:, None, None]
    k2 = np.arange(n2c)[None, :, None]
    n2 = np.arange(n2c)[None, None, :]
    th = 2 * np.pi * (n2 * k2 / n2c + n1 * k2 / S)
    taba = np.concatenate([np.cos(th), np.sin(th)], axis=1)
    k1 = np.arange(n1c)
    g = 2 * np.pi * np.outer(k1, k1) / n1c
    eye = np.eye(DFT_KRON)
    kc = np.kron(eye, np.cos(g))
    ks = np.kron(eye, np.sin(g))
    return tuple(jnp.asarray(t, BF16) for t in (dftc, taba, kc, ks))


def _fourier(a, w_fourier, B, S):
    T = a.shape[0]
    dftc, taba, kc, ks = _dft_tables(S)
    full = lambda t: pl.BlockSpec(t.shape, lambda b, g: (0,) * t.ndim)
    blk = pl.BlockSpec((S, FGROUP_DIM), lambda b, g: (b, g))
    return pl.pallas_call(
        _fourier_kernel, grid=(B, N_FGROUPS),
        in_specs=[blk, full(dftc), full(taba), full(kc), full(ks),
                  pl.BlockSpec((None, FGROUP_DIM, FGROUP_DIM), lambda b, g: (g, 0, 0))],
        out_specs=blk,
        out_shape=jax.ShapeDtypeStruct((T, F_WIDTH), BF16),
        scratch_shapes=(
            [pltpu.VMEM((S // DFT_N1 * (DFT_N1 + DFT_PITCH_PAD), FGROUP_DIM), F32)] * 2
            + [pltpu.VMEM((DFT_N1 * (S // DFT_N1 + DFT_PITCH_PAD), FGROUP_DIM), F32)] * 3),
        compiler_params=_cparams("parallel", "parallel"), name="fourier")(
            a, dftc, taba, kc, ks, w_fourier.astype(BF16))


def _attn_kernel(qt_ref, k_ref, vt_ref, o_ref, qs_ref, kmax_ref, acc_ref, m_ref, s0_ref, s1_ref,
                 p0_ref, p1_ref):
    tq = qt_ref.shape[1]
    n_keys = k_ref.shape[0]
    tk = min(ATT_TK, n_keys)
    n_chunks = n_keys // tk
    assert n_chunks % 2 == 0 and n_chunks * tk == n_keys

    def keys(c):
        return k_ref[pl.ds(pl.multiple_of(c * tk, tk), tk), :]

    @pl.when(pl.program_id(2) == 0)
    def _():
        def body(c, best):
            k = keys(c).astype(F32)
            return jnp.maximum(best, jnp.sum(k * k, axis=1, keepdims=True))
        best = lax.fori_loop(0, n_chunks, body, jnp.zeros((tk, 1), F32))
        kmax_ref[...] = jnp.broadcast_to(jnp.sqrt(jnp.max(best, axis=0, keepdims=True)), kmax_ref.shape)

    qs_ref[HEAD_DIM:, :] = jnp.zeros((128 - HEAD_DIM, Q_GROUP * tq), BF16)
    for g in range(Q_GROUP):
        qs_ref[:HEAD_DIM, g * tq:(g + 1) * tq] = qt_ref[g * HEAD_DIM:(g + 1) * HEAD_DIM, :]
    qf = qs_ref[...].astype(F32)
    bound = jnp.sqrt(jnp.sum(qf * qf, axis=0, keepdims=True)) * kmax_ref[0:1, 0:1] * ATT_BOUND_SLACK

    def scores(c):
        return _dot(keys(c), qs_ref[...])

    def values(c):
        return vt_ref[:ATT_V_ROWS, pl.ds(pl.multiple_of(c * tk, tk), tk)]

    def weights(s_buf):
        return jnp.exp2(s_buf[...] - bound).astype(BF16)

    def accumulate(p_buf, c):
        acc_ref[...] += _dot(values(c), p_buf[...])

    last = n_chunks - 1

    def fast(c2, carry):
        c = 2 * c2
        s0_ref[...] = scores(jnp.minimum(c + 2, last))
        p1_ref[...] = weights(s1_ref)
        accumulate(p0_ref, c)
        s1_ref[...] = scores(jnp.minimum(c + 3, last))
        p0_ref[...] = weights(s0_ref)
        accumulate(p1_ref, c + 1)
        return carry

    acc_ref[...] = jnp.zeros(acc_ref.shape, F32)
    s0_ref[...] = scores(0)
    p0_ref[...] = weights(s0_ref)
    s1_ref[...] = scores(1)
    lax.fori_loop(0, n_chunks // 2, fast, 0, unroll=2)
    underflow = jnp.min(acc_ref[HEAD_DIM:HEAD_DIM + 1, :]) < ATT_MIN_ROW_SUM

    @pl.when(underflow)
    def _():
        def safe(c, carry):
            s = scores(c)
            m_old = m_ref[...]
            m_new = jnp.maximum(m_old, jnp.max(s, axis=0, keepdims=True))
            acc_ref[...] = (jnp.exp2(m_old - m_new) * acc_ref[...]
                            + _dot(values(c), jnp.exp2(s - m_new).astype(BF16)))
            m_ref[...] = m_new
            return carry

        m_ref[...] = jnp.full(m_ref.shape, NEG_INF, F32)
        acc_ref[...] = jnp.zeros(acc_ref.shape, F32)
        lax.fori_loop(0, n_chunks, safe, 0)

    acc = acc_ref[...]
    ot = acc[:HEAD_DIM, :] / acc[HEAD_DIM:HEAD_DIM + 1, :]
    ot = jnp.concatenate([ot, jnp.zeros((128 - HEAD_DIM, Q_GROUP * tq), F32)], axis=0)
    o = ot.T
    o_ref[...] = jnp.concatenate([o[g * tq:(g + 1) * tq, :HEAD_DIM] for g in range(Q_GROUP)],
                                 axis=1).astype(BF16)


def _attention(qt, k2, vt, B, S):
    T = B * S
    tq = ATT_TQ
    nq = S // tq
    gw = Q_GROUP * HEAD_DIM
    cols = Q_GROUP * tq
    tk = min(ATT_TK, S)
    return pl.pallas_call(
        _attn_kernel, grid=(B, N_KV_HEADS, nq),
        in_specs=[pl.BlockSpec((None, gw, tq), lambda b, h, i: (b, h, i)),
                  pl.BlockSpec((None, None, S, 128), lambda b, h, i: (h, b, 0, 0)),
                  pl.BlockSpec((None, 128, S), lambda b, h, i: (b, h, 0))],
        out_specs=pl.BlockSpec((tq, gw), lambda b, h, i: (b * nq + i, h)),
        out_shape=jax.ShapeDtypeStruct((T, Q_WIDTH), BF16),
        scratch_shapes=[pltpu.VMEM((128, cols), BF16), pltpu.VMEM((8, 128), F32),
                        pltpu.VMEM((ATT_V_ROWS, cols), F32), pltpu.VMEM((1, cols), F32),
                        pltpu.VMEM((tk, cols), F32), pltpu.VMEM((tk, cols), F32),
                        pltpu.VMEM((tk, cols), BF16), pltpu.VMEM((tk, cols), BF16)],
        compiler_params=_cparams("parallel", "parallel", "arbitrary"),
        name="attention")(qt, k2.reshape(N_KV_HEADS, B, S, 128), vt)


def _even_out_kernel(a_ref, t_ref, wa_ref, wt_ref, x_ref, g_ref, b_ref, o_ref):
    mix = _dot(a_ref[...], wa_ref[...]) + _dot(t_ref[...], wt_ref[...])
    o_ref[...] = _ln(DEEPNORM_ALPHA * x_ref[...] + mix, g_ref[...], b_ref[...])


def _even_out(a_out, attn, w_out, x, g, b):
    T, D = x.shape
    tm = ROW_TILE
    row = lambda w: pl.BlockSpec((tm, w), lambda i: (i, 0))
    full = lambda a: pl.BlockSpec(a.shape, lambda i: (0,) * a.ndim)
    wa = w_out[:F_WIDTH].astype(BF16)
    wt = w_out[F_WIDTH:].astype(BF16)
    g = g.reshape(1, D)
    b = b.reshape(1, D)
    return pl.pallas_call(
        _even_out_kernel, grid=(T // tm,),
        in_specs=[row(F_WIDTH), row(Q_WIDTH), full(wa), full(wt), row(D), full(g), full(b)],
        out_specs=row(D), out_shape=jax.ShapeDtypeStruct((T, D), F32),
        compiler_params=_cparams("parallel"), name="even_out")(a_out, attn, wa, wt, x, g, b)


def _odd_kernel(x_ref, wi_ref, bi_ref, vg_ref, vb_ref, ws_ref, bs_ref, wo_ref, g_ref, b_ref, o_ref,
                gate_ref):
    tm = x_ref.shape[0]
    x = x_ref[...]
    h = _dot(x.astype(BF16), wi_ref[...]) + bi_ref[...]
    h = 0.5 * h * (1.0 + lax.erf(h * (1.0 / math.sqrt(2.0))))
    u = h[:, :C_WIDTH]
    v = _ln(h[:, C_WIDTH:], vg_ref[...], vb_ref[...]).astype(BF16)
    for c in range(tm // CHUNK):
        r0 = c * CHUNK
        for gi in range(N_CGROUPS):
            l0 = gi * CGROUP_DIM
            sv = _dot(ws_ref[gi], v[r0:r0 + CHUNK, l0:l0 + CGROUP_DIM]) + bs_ref[gi]
            gate_ref[r0:r0 + CHUNK, l0:l0 + CGROUP_DIM] = (
                u[r0:r0 + CHUNK, l0:l0 + CGROUP_DIM] * sv).astype(BF16)
    mix = _dot(gate_ref[...], wo_ref[...])
    o_ref[...] = _ln(DEEPNORM_ALPHA * x + mix, g_ref[...], b_ref[...])


def _odd_mixer(x, w_in, b_in, v_g, v_b, w_s, b_s, w_out, g, b):
    T, D = x.shape
    tm = ROW_TILE
    row = pl.BlockSpec((tm, D), lambda i: (i, 0))
    full = lambda a: pl.BlockSpec(a.shape, lambda i: (0,) * a.ndim)
    args = [w_in.astype(BF16), b_in.reshape(1, 2 * C_WIDTH), v_g.reshape(1, C_WIDTH),
            v_b.reshape(1, C_WIDTH), w_s.astype(BF16),
            jnp.broadcast_to(b_s[:, :, None], (N_CGROUPS, CHUNK, CGROUP_DIM)).astype(F32),
            w_out.astype(BF16), g.reshape(1, D), b.reshape(1, D)]
    return pl.pallas_call(
        _odd_kernel, grid=(T // tm,),
        in_specs=[row] + [full(a) for a in args],
        out_specs=row, out_shape=jax.ShapeDtypeStruct((T, D), F32),
        scratch_shapes=[pltpu.VMEM((tm, C_WIDTH), BF16)],
        compiler_params=_cparams("parallel"), name="odd_mixer")(x, *args)


def _router_kernel(x_ref, w_ref, rb_ref, tri_ref, eidx_ref, wts_ref, pos_ref, cnt_ref, xp_ref, run_ref):
    tm = x_ref.shape[0]
    i = pl.program_id(0)

    @pl.when(i == 0)
    def _():
        run_ref[...] = jnp.zeros(run_ref.shape, F32)

    x = x_ref[...]
    xp_ref[...] = _pack_halves(x)
    xh = x.astype(BF16)
    xl = (x - xh.astype(F32)).astype(BF16)
    nt = (((1,), (1,)), ((), ()))
    dg = lambda a, c: lax.dot_general(a, c, nt, preferred_element_type=F32)
    logits = dg(w_ref[0], xh) + dg(w_ref[0], xl) + dg(w_ref[1], xh)
    scores = jax.nn.sigmoid(logits)
    sel = scores + rb_ref[...]

    i8 = lax.broadcasted_iota(I32, (GROUP_SIZE, tm), 0)
    gsc_rows = []
    for gidx in range(N_EXPERT_GROUPS):
        sg = sel[gidx * GROUP_SIZE:(gidx + 1) * GROUP_SIZE, :]
        m1 = jnp.max(sg, axis=0, keepdims=True)
        f1 = jnp.min(jnp.where(sg == m1, i8, GROUP_SIZE), axis=0, keepdims=True)
        m2 = jnp.max(jnp.where(i8 == f1, NEG_INF, sg), axis=0, keepdims=True)
        gsc_rows.append(m1 + m2)
    gsc = jnp.concatenate(gsc_rows, axis=0)

    gsel = jnp.zeros(gsc.shape, F32)
    for _ in range(TOPK_GROUPS):
        m = jnp.max(gsc, axis=0, keepdims=True)
        f = jnp.min(jnp.where(gsc == m, i8, N_EXPERT_GROUPS), axis=0, keepdims=True)
        pick = i8 == f
        gsel = jnp.where(pick, 1.0, gsel)
        gsc = jnp.where(pick, NEG_INF, gsc)
    esel = jnp.concatenate(
        [jnp.broadcast_to(gsel[gidx:gidx + 1, :], (GROUP_SIZE, tm)) for gidx in range(N_EXPERT_GROUPS)],
        axis=0)

    cur = jnp.where(esel > 0.0, sel, NEG_INF)
    ei = lax.broadcasted_iota(I32, cur.shape, 0)
    idx_rows, sc_rows = [], []
    chosen = jnp.zeros(cur.shape, F32)
    for _ in range(TOP_K):
        m = jnp.max(cur, axis=0, keepdims=True)
        f = jnp.min(jnp.where(cur == m, ei, N_EXPERTS), axis=0, keepdims=True)
        pick = ei == f
        idx_rows.append(f)
        sc_rows.append(jnp.sum(jnp.where(pick, scores, 0.0), axis=0, keepdims=True))
        chosen = jnp.where(pick, 1.0, chosen)
        cur = jnp.where(pick, NEG_INF, cur)
    eidx = jnp.concatenate(idx_rows, axis=0)
    sc = jnp.concatenate(sc_rows, axis=0)
    eidx_ref[...] = eidx
    wts_ref[...] = sc / jnp.sum(sc, axis=0, keepdims=True) * ROUTE_SCALE

    before = _dot(chosen.astype(BF16), tri_ref[...]) + run_ref[...]
    pos_rows = [jnp.sum(jnp.where(ei == idx_rows[k], before, 0.0), axis=0, keepdims=True)
                for k in range(TOP_K)]
    pos_ref[...] = jnp.concatenate(pos_rows, axis=0).astype(I32)
    run_new = run_ref[...] + jnp.sum(chosen, axis=1, keepdims=True)
    run_ref[...] = run_new
    cnt_ref[...] = jnp.broadcast_to(run_new, cnt_ref.shape).astype(I32)


def _router(x, router_w, router_b, first_row, T):
    D = x.shape[1]
    tm = min(ROUTER_TILE, T)
    n_tiles = T // tm
    first_tile = first_row // tm
    assert n_tiles * tm == T and first_tile * tm == first_row
    wt = router_w.T.astype(F32)
    wh = wt.astype(BF16)
    wl = (wt - wh.astype(F32)).astype(BF16)
    w2 = jnp.stack([wh, wl])
    rb = router_b.astype(F32).reshape(N_EXPERTS, 1)
    tri = jnp.asarray(np.triu(np.ones((tm, tm)), 1), BF16)
    full = lambda a: pl.BlockSpec(a.shape, lambda i: (0,) * a.ndim)
    col = pl.BlockSpec((TOP_K, tm), lambda i: (0, i))
    return pl.pallas_call(
        _router_kernel, grid=(n_tiles,),
        in_specs=[pl.BlockSpec((tm, D), lambda i: (i + first_tile, 0)), full(w2), full(rb), full(tri)],
        out_specs=[col, col, col, pl.BlockSpec((N_EXPERTS, 128), lambda i: (0, 0)),
                   pl.BlockSpec((tm, HALF), lambda i: (i, 0))],
        out_shape=[jax.ShapeDtypeStruct((TOP_K, T), I32), jax.ShapeDtypeStruct((TOP_K, T), F32),
                   jax.ShapeDtypeStruct((TOP_K, T), I32), jax.ShapeDtypeStruct((N_EXPERTS, 128), I32),
                   jax.ShapeDtypeStruct((T, HALF), I32)],
        scratch_shapes=[pltpu.VMEM((N_EXPERTS, 1), F32)],
        compiler_params=_cparams("arbitrary"), name="router")(x, w2, rb, tri)


def _dest_kernel(start_ref, eidx_ref, pos_ref, o_ref):
    e = eidx_ref[...]
    acc = pos_ref[...]
    for j in range(N_EXPERTS):
        acc = acc + jnp.where(e == j, start_ref[j], 0)
    o_ref[...] = acc


def _dest_rows(eidx, pos, seg_start):
    K, T = eidx.shape
    tl = min(T, 2048)
    blk = pl.BlockSpec((K, tl), lambda i, s: (0, i))
    grid_spec = pltpu.PrefetchScalarGridSpec(
        num_scalar_prefetch=1, grid=(T // tl,), in_specs=[blk, blk], out_specs=blk)
    return pl.pallas_call(
        _dest_kernel, grid_spec=grid_spec, out_shape=jax.ShapeDtypeStruct((K, T), I32),
        compiler_params=_cparams("parallel"), name="dest_rows")(seg_start, eidx, pos)


def _gather_rows(table, idx):
    n_rows = idx.shape[0]
    width = table.shape[1]
    per_worker = n_rows // SC_WORKERS
    n_chunks = per_worker // SC_CHUNK
    assert per_worker * SC_WORKERS == n_rows and n_chunks * SC_CHUNK == per_worker
    mesh = plsc.VectorSubcoreMesh(core_axis_name="c", subcore_axis_name="s")

    @functools.partial(
        pl.kernel, mesh=mesh,
        out_type=jax.ShapeDtypeStruct((n_rows, width), table.dtype),
        scratch_types=[pltpu.VMEM((SC_CHUNK,), I32), pltpu.VMEM((SC_CHUNK, width), table.dtype),
                       pltpu.SemaphoreType.DMA])
    def gather(table_hbm, idx_hbm, out_hbm, idx_v, rows_v, sem):
        wid = lax.axis_index("s") * 2 + lax.axis_index("c")
        base = wid * per_worker

        @pl.loop(0, n_chunks)
        def _(j):
            off = base + j * SC_CHUNK
            pltpu.sync_copy(idx_hbm.at[pl.ds(off, SC_CHUNK)], idx_v)
            pltpu.async_copy(table_hbm.at[idx_v], rows_v, sem).wait()
            pltpu.sync_copy(rows_v, out_hbm.at[pl.ds(off, SC_CHUNK)])

    return gather(table, idx)


def _scatter_rows(rows, dest):
    n_tok, width = rows.shape
    n_dst = dest.shape[0]
    per_worker = n_tok // SC_WORKERS
    n_chunks = per_worker // SC_CHUNK
    assert per_worker * SC_WORKERS == n_tok and n_chunks * SC_CHUNK == per_worker
    mesh = plsc.VectorSubcoreMesh(core_axis_name="c", subcore_axis_name="s")

    @functools.partial(
        pl.kernel, mesh=mesh,
        out_type=jax.ShapeDtypeStruct((n_dst * n_tok, width), rows.dtype),
        scratch_types=[pltpu.VMEM((n_dst, SC_CHUNK), I32), pltpu.VMEM((SC_CHUNK, width), rows.dtype),
                       pltpu.SemaphoreType.DMA])
    def scatter(rows_hbm, dest_hbm, out_hbm, idx_v, rows_v, sem):
        wid = lax.axis_index("s") * 2 + lax.axis_index("c")
        base = wid * per_worker

        @pl.loop(0, n_chunks)
        def _(j):
            off = base + j * SC_CHUNK
            pltpu.sync_copy(dest_hbm.at[:, pl.ds(off, SC_CHUNK)], idx_v)
            pltpu.sync_copy(rows_hbm.at[pl.ds(off, SC_CHUNK)], rows_v)
            copies = [pltpu.async_copy(rows_v, out_hbm.at[idx_v.at[k]], sem) for k in range(n_dst)]
            for c in copies:
                c.wait()

    return scatter(rows, dest)


def _expert_kernel(blk_ref, exp_ref, lo_ref, hi_ref, slot_ref, nxt_ref, xs_hbm, wg_hbm, wu_hbm, wd_hbm,
                   ys_ref, wg_buf, wu_buf, wd_buf, wgu_s, wd_s, sem, xs_buf, xs_sem, *, layer):
    i = pl.program_id(0)
    prev = jnp.maximum(i - 1, 0)

    def weight_copies(expert, slot):
        return [pltpu.make_async_copy(src.at[layer, expert], dst.at[slot], sem.at[slot])
                for src, dst in ((wg_hbm, wg_buf), (wu_hbm, wu_buf), (wd_hbm, wd_buf))]

    @pl.when(i == 0)
    def _():
        for c in weight_copies(exp_ref[0], slot_ref[0]):
            c.start()

    @pl.when(jnp.logical_or(i == 0, exp_ref[i] != exp_ref[prev]))
    def _():
        slot = slot_ref[i]
        for c in weight_copies(exp_ref[i], slot):
            c.wait()
        wgu_s[:, :EXPERT_DIM] = wg_buf[slot].astype(BF16)
        wgu_s[:, EXPERT_DIM:] = wu_buf[slot].astype(BF16)
        wd_s[...] = wd_buf[slot].astype(BF16)

        @pl.when(nxt_ref[i] >= 0)
        def _():
            for c in weight_copies(nxt_ref[i], 1 - slot):
                c.start()

    n_blocks = xs_hbm.shape[0] // EXPERT_ROWS
    blk = blk_ref[i]
    first = jnp.logical_or(i == 0, blk != blk_ref[prev])

    def rows_copy(block):
        slot = block % EXPERT_XS_SLOTS
        src = xs_hbm.at[pl.ds(pl.multiple_of(block * EXPERT_ROWS, EXPERT_ROWS), EXPERT_ROWS)]
        return pltpu.make_async_copy(src, xs_buf.at[slot], xs_sem.at[slot])

    @pl.when(i == 0)
    def _():
        for b0 in range(min(EXPERT_XS_SLOTS - 1, n_blocks)):
            rows_copy(b0).start()

    @pl.when(first)
    def _():
        rows_copy(blk).wait()

        @pl.when(blk + EXPERT_XS_SLOTS - 1 < n_blocks)
        def _():
            rows_copy(blk + EXPERT_XS_SLOTS - 1).start()

    xs_ref = xs_buf.at[blk % EXPERT_XS_SLOTS]
    lo = lo_ref[i]
    hi = hi_ref[i]

    def sub_block(r0):
        rows = slice(r0, r0 + EXPERT_SUB_ROWS)
        w = xs_ref[rows, :]
        xlo = lax.bitcast_convert_type(w.astype(jnp.int16), BF16)
        xhi = lax.bitcast_convert_type(lax.shift_right_logical(w, 16).astype(jnp.int16), BF16)
        gu = _dot(xlo, wgu_s[:HALF, :]) + _dot(xhi, wgu_s[HALF:, :])
        g = gu[:, :EXPERT_DIM]
        hb = (g * jax.nn.sigmoid(g) * gu[:, EXPERT_DIM:]).astype(BF16)
        y = _dot(hb, wd_s[...])
        packed = pltpu.pack_elementwise([y[:, :HALF], y[:, HALF:]], packed_dtype=BF16)
        row = r0 + lax.broadcasted_iota(I32, (EXPERT_SUB_ROWS, 1), 0)
        mine = jnp.logical_and(row >= lo, row < hi)
        kept = jnp.where(first, 0, ys_ref[rows, :])
        ys_ref[rows, :] = jnp.where(mine, packed, kept)

    for r0 in range(0, EXPERT_ROWS, EXPERT_SUB_ROWS):
        pl.when(jnp.logical_and(lo < r0 + EXPERT_SUB_ROWS, hi > r0))(functools.partial(sub_block, r0))


def _expert_items(counts, n_rows):
    n_blocks = n_rows // EXPERT_ROWS
    n_items = n_blocks + N_EXPERTS - 1
    end = jnp.cumsum(counts)
    start = end - counts
    first_blk = start // EXPERT_ROWS
    n_blk = jnp.where(counts > 0, (end - 1) // EXPERT_ROWS - first_blk + 1, 0)
    item_end = jnp.cumsum(n_blk)
    item_start = item_end - n_blk
    slot = jnp.arange(n_items, dtype=I32)
    e = jnp.minimum(jnp.sum((item_end[None, :] <= slot[:, None]).astype(I32), axis=1), N_EXPERTS - 1)
    onehot = (e[:, None] == jnp.arange(N_EXPERTS, dtype=I32)[None, :]).astype(I32)
    pick = lambda v: jnp.sum(onehot * v[None, :], axis=1)
    valid = slot < item_end[-1]
    blk = jnp.where(valid, pick(first_blk) + slot - pick(item_start), n_blocks - 1)
    lo = jnp.clip(pick(start) - blk * EXPERT_ROWS, 0, EXPERT_ROWS)
    hi = jnp.clip(pick(end) - blk * EXPERT_ROWS, 0, EXPERT_ROWS)
    last_e = jnp.max(jnp.where(counts > 0, jnp.arange(N_EXPERTS, dtype=I32), 0))
    e = jnp.where(valid, e, last_e)
    hi = jnp.where(valid, hi, 0)
    lo = jnp.where(valid, lo, 0)
    change = jnp.concatenate([jnp.ones((1,), I32), (e[1:] != e[:-1]).astype(I32)])
    slot = (jnp.cumsum(change) - 1) % 2
    later = jnp.where(jnp.arange(N_EXPERTS, dtype=I32)[None, :] > e[:, None], counts[None, :] > 0, False)
    nxt = jnp.where(jnp.any(later, axis=1), jnp.argmax(later, axis=1), -1)
    return tuple(a.astype(I32) for a in (blk, e, lo, hi, slot, nxt))


def _experts(xs, items, w_gate, w_up, w_down, layer):
    n_rows = xs.shape[0]
    n_items = items[0].shape[0]
    blk_map = lambda i, b, *_: (b[i], 0)
    hbm = pl.BlockSpec(memory_space=pl.ANY)
    grid_spec = pltpu.PrefetchScalarGridSpec(
        num_scalar_prefetch=len(items), grid=(n_items,),
        in_specs=[hbm, hbm, hbm, hbm],
        out_specs=pl.BlockSpec((EXPERT_ROWS, HALF), blk_map),
        scratch_shapes=[pltpu.VMEM((2, D_MODEL, EXPERT_DIM), F32),
                        pltpu.VMEM((2, D_MODEL, EXPERT_DIM), F32),
                        pltpu.VMEM((2, EXPERT_DIM, D_MODEL), F32),
                        pltpu.VMEM((D_MODEL, 2 * EXPERT_DIM), BF16),
                        pltpu.VMEM((EXPERT_DIM, D_MODEL), BF16),
                        pltpu.SemaphoreType.DMA((2,)),
                        pltpu.VMEM((EXPERT_XS_SLOTS, EXPERT_ROWS, HALF), I32),
                        pltpu.SemaphoreType.DMA((EXPERT_XS_SLOTS,))])
    return pl.pallas_call(
        functools.partial(_expert_kernel, layer=layer), grid_spec=grid_spec,
        out_shape=jax.ShapeDtypeStruct((n_rows, HALF), I32),
        compiler_params=_cparams("arbitrary"), name="experts")(*items, xs, w_gate, w_up, w_down)


def _moe_out_kernel(x_ref, yg_ref, wt_ref, sgu_ref, sd_ref, g_ref, b_ref, *rest):
    o_ref = rest[-1]
    x = x_ref[...]
    wt = wt_ref[...]
    lo = jnp.zeros((x.shape[0], HALF), F32)
    hi = jnp.zeros((x.shape[0], HALF), F32)
    for k in range(TOP_K):
        w = yg_ref[k]
        wk = wt[:, k:k + 1]
        lo = lo + wk * _unpack_lo(w)
        hi = hi + wk * _unpack_hi(w)
    gu = _dot(x.astype(BF16), sgu_ref[...])
    g = gu[:, :EXPERT_DIM]
    hs = (g * jax.nn.sigmoid(g) * gu[:, EXPERT_DIM:]).astype(BF16)
    ffn = jnp.concatenate([lo, hi], axis=1) + _dot(hs, sd_ref[...])
    o_ref[...] = _ln(DEEPNORM_ALPHA * x + ffn, g_ref[...], b_ref[...])


def _moe_out(x, yg, wts, sh_gate, sh_up, sh_down, g, b, first_tile, partial_out):
    T, D = x.shape
    tm = ROW_TILE
    n_tiles = yg.shape[1] // tm
    xrow = pl.BlockSpec((tm, D), lambda i: (i + first_tile, 0))
    full = lambda a: pl.BlockSpec(a.shape, lambda i: (0,) * a.ndim)
    sgu = jnp.concatenate([sh_gate, sh_up], axis=1).astype(BF16)
    sd = sh_down.astype(BF16)
    g = g.reshape(1, D)
    b = b.reshape(1, D)
    args = [x, yg, wts, sgu, sd, g, b]
    in_specs = [xrow, pl.BlockSpec((TOP_K, tm, HALF), lambda i: (0, i, 0)),
                pl.BlockSpec((tm, TOP_K), lambda i: (i, 0)), full(sgu), full(sd), full(g), full(b)]
    aliases = {}
    if partial_out is not None:
        args.append(partial_out)
        in_specs.append(pl.BlockSpec(memory_space=pl.ANY))
        aliases = {len(args) - 1: 0}
    return pl.pallas_call(
        _moe_out_kernel, grid=(n_tiles,), in_specs=in_specs,
        out_specs=xrow, out_shape=jax.ShapeDtypeStruct((T, D), F32),
        input_output_aliases=aliases,
        compiler_params=_cparams("parallel"), name="moe_out")(*args)


def _moe(x, router_w, router_b, w_gate, w_up, w_down, layer, sh_gate, sh_up, sh_down, g, b):
    T = x.shape[0]
    tiles = T // ROW_TILE // MOE_TOKEN_GROUPS
    tg = tiles * ROW_TILE
    out = None
    for grp in range(MOE_TOKEN_GROUPS):
        eidx, wts, pos, cnt, xp = _router(x, router_w, router_b, grp * tg, tg)
        counts = cnt[:, 0]
        seg_start = (jnp.cumsum(counts) - counts).astype(I32)
        dest = _dest_rows(eidx, pos, seg_start)
        xs = _scatter_rows(xp, dest)
        ys = _experts(xs, _expert_items(counts, tg * TOP_K), w_gate, w_up, w_down, layer)
        yg = _gather_rows(ys, dest.reshape(tg * TOP_K)).reshape(TOP_K, tg, HALF)
        out = _moe_out(x, yg, wts.T, sh_gate, sh_up, sh_down, g, b, grp * tiles, out)
    return out


def kernel(x, ln_in_g, ln_in_b, e_w_in, e_w_fourier, e_q_gain, e_k_gain, e_w_out, o_w_in, o_b_in, o_v_ln_g, o_v_ln_b, o_w_spatial, o_b_spatial, o_w_out, ln_mix_g, ln_mix_b, ln_ffn_g, ln_ffn_b, router_w, router_b, exp_w_gate, exp_w_up, exp_w_down, sh_w_gate, sh_w_up, sh_w_down):
    B, S, D = x.shape
    Bc = B // BATCH_CHAINS
    T = Bc * S
    hs = [_layer_norm(xc.reshape(T, D), ln_in_g, ln_in_b) for xc in jnp.split(x, BATCH_CHAINS, axis=0)]
    for i in range(DEPTH):
        j = i // 2
        for c, h in enumerate(hs):
            if i % 2 == 0:
                a, qt, k2, vt = _even_in(h, e_w_in[j], e_q_gain[j], e_k_gain[j], Bc, S)
                a_out = _fourier(a, e_w_fourier[j], Bc, S)
                attn = _attention(qt, k2, vt, Bc, S)
                h = _even_out(a_out, attn, e_w_out[j], h, ln_mix_g[i], ln_mix_b[i])
            else:
                h = _odd_mixer(h, o_w_in[j], o_b_in[j], o_v_ln_g[j], o_v_ln_b[j], o_w_spatial[j],
                               o_b_spatial[j], o_w_out[j], ln_mix_g[i], ln_mix_b[i])
            hs[c] = _moe(h, router_w[i], router_b[i], exp_w_gate, exp_w_up, exp_w_down, i,
                         sh_w_gate[i], sh_w_up[i], sh_w_down[i], ln_ffn_g[i], ln_ffn_b[i])
    return jnp.concatenate(hs, axis=0).reshape(B, S, D)
```

```python
import functools
import math

import numpy as np
import jax
import jax.numpy as jnp
from jax import lax
from jax.experimental import pallas as pl
from jax.experimental.pallas import tpu as pltpu
from jax.experimental.pallas import tpu_sc as plsc

F32 = jnp.float32
BF16 = jnp.bfloat16
I32 = jnp.int32

D_MODEL = 1024
DEPTH = 4
GRID_W = 64
N_FGROUPS = 4
FGROUP_DIM = 128
F_WIDTH = N_FGROUPS * FGROUP_DIM
N_HEADS = 8
N_KV_HEADS = 2
HEAD_DIM = 64
Q_GROUP = N_HEADS // N_KV_HEADS
Q_WIDTH = N_HEADS * HEAD_DIM
KV_WIDTH = N_KV_HEADS * HEAD_DIM
ROPE_THETA = 10000.0
ROPE_PAIRS = HEAD_DIM // 4
EVEN_IN_WIDTH = F_WIDTH + Q_WIDTH + 2 * KV_WIDTH
CHUNK = 128
N_CGROUPS = 8
CGROUP_DIM = D_MODEL // N_CGROUPS
C_WIDTH = N_CGROUPS * CGROUP_DIM
N_EXPERTS = 64
EXPERT_DIM = 256
TOP_K = 8
N_EXPERT_GROUPS = 8
GROUP_SIZE = N_EXPERTS // N_EXPERT_GROUPS
TOPK_GROUPS = 4
ROUTE_SCALE = 2.5
LN_EPS = 1e-5
QK_EPS = 1e-6
DEEPNORM_ALPHA = (2 * DEPTH) ** 0.25

VMEM_LIMIT_BYTES = 56 * 1024 * 1024
ROW_TILE = 512
DFT_N1 = 64
DFT_KRON = 4
DFT_PITCH_PAD = 8
ROUTER_TILE = 1024
EXPERT_ROWS = 2048
EXPERT_XS_SLOTS = 3
EXPERT_SUB_ROWS = 512
HALF = D_MODEL // 2
SC_CORES = 2
SC_SUBCORES = 16
SC_WORKERS = SC_CORES * SC_SUBCORES
SC_CHUNK = 128
MOE_TOKEN_GROUPS = 2
ATT_TQ = 256
ATT_TK = 512
ATT_V_ROWS = 80
ATT_BOUND_SLACK = 1.0 + 2.0 ** -7
ATT_MIN_ROW_SUM = 2.0 ** -80
NEG_INF = float("-inf")


def _cparams(*sem):
    return pltpu.CompilerParams(dimension_semantics=sem, vmem_limit_bytes=VMEM_LIMIT_BYTES)


def _ln(x, g, b):
    mu = jnp.mean(x, axis=-1, keepdims=True)
    xc = x - mu
    var = jnp.mean(xc * xc, axis=-1, keepdims=True)
    return xc * lax.rsqrt(var + LN_EPS) * g + b


def _dot(a, b):
    return jnp.dot(a, b, preferred_element_type=F32)


def _pack_halves(y):
    lo = lax.bitcast_convert_type(y[:, :HALF].astype(BF16).astype(F32), I32)
    hi = lax.bitcast_convert_type(y[:, HALF:].astype(BF16).astype(F32), I32)
    return lax.shift_right_logical(lo, 16) | (hi & jnp.int32(-65536))


def _unpack_lo(w):
    return lax.bitcast_convert_type(lax.shift_left(w, 16), F32)


def _unpack_hi(w):
    return lax.bitcast_convert_type(w & jnp.int32(-65536), F32)


def _ln_kernel(x_ref, g_ref, b_ref, o_ref):
    o_ref[...] = _ln(x_ref[...], g_ref[...], b_ref[...])


def _layer_norm(x, g, b):
    T, D = x.shape
    row = pl.BlockSpec((ROW_TILE, D), lambda i: (i, 0))
    vec = pl.BlockSpec((1, D), lambda i: (0, 0))
    return pl.pallas_call(
        _ln_kernel, grid=(T // ROW_TILE,), in_specs=[row, vec, vec], out_specs=row,
        out_shape=jax.ShapeDtypeStruct((T, D), F32), compiler_params=_cparams("parallel"),
        name="ln_in")(x, g.reshape(1, D), b.reshape(1, D))


def _even_in_kernel(x_ref, w_ref, qm_ref, km_ref, qg_ref, kg_ref, cos_ref, sin_ref,
                    a_ref, qt_ref, k_ref, vt_ref):
    tm = x_ref.shape[0]
    h = _dot(x_ref[...].astype(BF16), w_ref[...])
    a_ref[...] = h[:, :F_WIDTH].astype(BF16)
    q = h[:, F_WIDTH:F_WIDTH + Q_WIDTH]
    k = h[:, F_WIDTH + Q_WIDTH:F_WIDTH + Q_WIDTH + KV_WIDTH]
    v = h[:, F_WIDTH + Q_WIDTH + KV_WIDTH:]
    cos = cos_ref[...]
    sin = sin_ref[...]
    lane = lax.broadcasted_iota(I32, (tm, 128), 1)
    first_of_pair = (lane & ROPE_PAIRS) == 0

    def mean_sq(xf, m_ref):
        sq = xf * xf
        hi = sq.astype(BF16)
        lo = (sq - hi.astype(F32)).astype(BF16)
        return _dot(hi, m_ref[...]) + _dot(lo, m_ref[...])

    def rope(xn):
        sw = jnp.where(first_of_pair, pltpu.roll(xn, 128 - ROPE_PAIRS, 1), pltpu.roll(xn, ROPE_PAIRS, 1))
        return xn * cos + sw * sin

    qn = q * lax.rsqrt(mean_sq(q, qm_ref) + QK_EPS) * qg_ref[...]
    scale = math.log2(math.e) / math.sqrt(HEAD_DIM)
    for c in range(Q_WIDTH // 128):
        qt_ref[c * 128:(c + 1) * 128, :] = (rope(qn[:, c * 128:(c + 1) * 128]) * scale).T.astype(BF16)
    kn = rope(k * lax.rsqrt(mean_sq(k, km_ref) + QK_EPS) * kg_ref[...])
    low = lane < HEAD_DIM
    k_ref[0] = jnp.where(low, kn, 0.0).astype(BF16)
    k_ref[1] = jnp.where(low, pltpu.roll(kn, HEAD_DIM, 1), 0.0).astype(BF16)
    ones_col = jnp.where(lane == HEAD_DIM, 1.0, 0.0)
    vt_ref[0:128, :] = jnp.where(low, v, ones_col).T.astype(BF16)
    vt_ref[128:256, :] = jnp.where(low, pltpu.roll(v, HEAD_DIM, 1), ones_col).T.astype(BF16)


def _rope_tables(S):
    t = np.arange(S)
    inv = ROPE_THETA ** (-np.arange(ROPE_PAIRS, dtype=np.float64) / ROPE_PAIRS)
    ang_r = (t // GRID_W)[:, None] * inv
    ang_c = (t % GRID_W)[:, None] * inv
    cos = np.concatenate([np.cos(ang_r), np.cos(ang_r), np.cos(ang_c), np.cos(ang_c)], axis=1)
    sin = np.concatenate([-np.sin(ang_r), np.sin(ang_r), -np.sin(ang_c), np.sin(ang_c)], axis=1)
    return (jnp.asarray(np.tile(cos, (1, 2)), F32), jnp.asarray(np.tile(sin, (1, 2)), F32))


def _head_mean_matrix(width):
    m = np.kron(np.eye(width // HEAD_DIM), np.full((HEAD_DIM, HEAD_DIM), 1.0 / HEAD_DIM))
    return jnp.asarray(m, BF16)


def _even_in(x, w_in, q_gain, k_gain, B, S):
    T, D = x.shape
    tm = ROW_TILE
    ns = S // tm
    cos, sin = _rope_tables(S)
    row = lambda w: pl.BlockSpec((tm, w), lambda i: (i, 0))
    full = lambda a: pl.BlockSpec(a.shape, lambda i: (0,) * a.ndim)
    tab = pl.BlockSpec((tm, 128), lambda i: (i % ns, 0))
    w = w_in.astype(BF16)
    qm = _head_mean_matrix(Q_WIDTH)
    km = _head_mean_matrix(KV_WIDTH)
    qg = jnp.tile(q_gain.astype(F32), N_HEADS).reshape(1, Q_WIDTH)
    kg = jnp.tile(k_gain.astype(F32), N_KV_HEADS).reshape(1, KV_WIDTH)
    return pl.pallas_call(
        _even_in_kernel, grid=(T // tm,),
        in_specs=[row(D), full(w), full(qm), full(km), full(qg), full(kg), tab, tab],
        out_specs=[row(F_WIDTH),
                   pl.BlockSpec((None, Q_WIDTH, tm), lambda i: (i // ns, 0, i % ns)),
                   pl.BlockSpec((N_KV_HEADS, tm, 128), lambda i: (0, i, 0)),
                   pl.BlockSpec((None, N_KV_HEADS * 128, tm), lambda i: (i // ns, 0, i % ns))],
        out_shape=[jax.ShapeDtypeStruct((T, F_WIDTH), BF16),
                   jax.ShapeDtypeStruct((B, Q_WIDTH, S), BF16),
                   jax.ShapeDtypeStruct((N_KV_HEADS, T, 128), BF16),
                   jax.ShapeDtypeStruct((B, N_KV_HEADS * 128, S), BF16)],
        compiler_params=_cparams("parallel"), name="even_in")(x, w, qm, km, qg, kg, cos, sin)


def _fourier_kernel(a_ref, dftc_ref, taba_ref, kc_ref, ks_ref, wf_ref, o_ref,
                    zr_ref, zi_ref, ur_ref, ui_ref, y_ref):
    S = a_ref.shape[0]
    n1_count = DFT_N1
    n2_count = S // DFT_N1
    pz = n1_count + DFT_PITCH_PAD
    pu = n2_count + DFT_PITCH_PAD
    blk = DFT_KRON * DFT_N1
    scale = 1.0 / math.sqrt(S * FGROUP_DIM)

    def channel_dft(j, carry):
        zz = _dot(a_ref[pl.ds(pl.multiple_of(j * blk, blk), blk), :], dftc_ref[...])
        for q in range(DFT_KRON):
            dst = pl.ds(pl.multiple_of((j * DFT_KRON + q) * pz, 8), n1_count)
            zr_ref[dst, :] = zz[q * n1_count:(q + 1) * n1_count, :FGROUP_DIM]
            zi_ref[dst, :] = zz[q * n1_count:(q + 1) * n1_count, FGROUP_DIM:]
        return carry

    lax.fori_loop(0, S // blk, channel_dft, 0, unroll=2)

    def stage_a(n1, carry):
        src = pl.ds(n1, n2_count, stride=pz)
        zn = jnp.concatenate([zr_ref[src, :], zi_ref[src, :]], axis=1).astype(BF16)
        r = _dot(taba_ref[n1], zn)
        dst = pl.ds(pl.multiple_of(n1 * pu, 8), n2_count)
        ur_ref[dst, :] = r[:n2_count, :FGROUP_DIM] + r[n2_count:, FGROUP_DIM:]
        ui_ref[dst, :] = r[:n2_count, FGROUP_DIM:] - r[n2_count:, :FGROUP_DIM]
        return carry

    lax.fori_loop(0, n1_count, stage_a, 0, unroll=4)

    def stage_b(j, carry):
        srcs = [pl.ds(j * DFT_KRON + q, n1_count, stride=pu) for q in range(DFT_KRON)]
        ur = jnp.concatenate([ur_ref[s, :] for s in srcs], axis=0).astype(BF16)
        ui = jnp.concatenate([ui_ref[s, :] for s in srcs], axis=0).astype(BF16)
        re = _dot(kc_ref[...], ur) + _dot(ks_ref[...], ui)
        out = _dot((re * scale).astype(BF16), wf_ref[...])
        for q in range(DFT_KRON):
            y_ref[srcs[q], :] = out[q * n1_count:(q + 1) * n1_count]
        return carry

    lax.fori_loop(0, S // blk, stage_b, 0, unroll=4)

    def compact(k1, carry):
        o_ref[pl.ds(pl.multiple_of(k1 * n2_count, n2_count), n2_count), :] = (
            y_ref[pl.ds(pl.multiple_of(k1 * pu, 8), n2_count), :].astype(BF16))
        return carry

    lax.fori_loop(0, n1_count, compact, 0)


def _dft_tables(S):
    n1c, n2c = DFT_N1, S // DFT_N1
    c = np.arange(FGROUP_DIM)
    ang = 2 * np.pi * np.outer(c, c) / FGROUP_DIM
    dftc = np.concatenate([np.cos(ang), -np.sin(ang)], axis=1)
    n1 = np.arange(n1c)[:, None, None]
    k2 = np.arange(n2c)[None, :, None]
    n2 = np.arange(n2c)[None, None, :]
    th = 2 * np.pi * (n2 * k2 / n2c + n1 * k2 / S)
    taba = np.concatenate([np.cos(th), np.sin(th)], axis=1)
    k1 = np.arange(n1c)
    g = 2 * np.pi * np.outer(k1, k1) / n1c
    eye = np.eye(DFT_KRON)
    kc = np.kron(eye, np.cos(g))
    ks = np.kron(eye, np.sin(g))
    return tuple(jnp.asarray(t, BF16) for t in (dftc, taba, kc, ks))


def _fourier(a, w_fourier, B, S):
    T = a.shape[0]
    dftc, taba, kc, ks = _dft_tables(S)
    full = lambda t: pl.BlockSpec(t.shape, lambda b, g: (0,) * t.ndim)
    blk = pl.BlockSpec((S, FGROUP_DIM), lambda b, g: (b, g))
    return pl.pallas_call(
        _fourier_kernel, grid=(B, N_FGROUPS),
        in_specs=[blk, full(dftc), full(taba), full(kc), full(ks),
                  pl.BlockSpec((None, FGROUP_DIM, FGROUP_DIM), lambda b, g: (g, 0, 0))],
        out_specs=blk,
        out_shape=jax.ShapeDtypeStruct((T, F_WIDTH), BF16),
        scratch_shapes=(
            [pltpu.VMEM((S // DFT_N1 * (DFT_N1 + DFT_PITCH_PAD), FGROUP_DIM), F32)] * 2
            + [pltpu.VMEM((DFT_N1 * (S // DFT_N1 + DFT_PITCH_PAD), FGROUP_DIM), F32)] * 3),
        compiler_params=_cparams("parallel", "parallel"), name="fourier")(
            a, dftc, taba, kc, ks, w_fourier.astype(BF16))


def _attn_kernel(qt_ref, k_ref, vt_ref, o_ref, qs_ref, kmax_ref, acc_ref, m_ref, s0_ref, s1_ref,
                 p0_ref, p1_ref):
    tq = qt_ref.shape[1]
    n_keys = k_ref.shape[0]
    tk = min(ATT_TK, n_keys)
    n_chunks = n_keys // tk
    assert n_chunks % 2 == 0 and n_chunks * tk == n_keys

    def keys(c):
        return k_ref[pl.ds(pl.multiple_of(c * tk, tk), tk), :]

    @pl.when(pl.program_id(2) == 0)
    def _():
        def body(c, best):
            k = keys(c).astype(F32)
            return jnp.maximum(best, jnp.sum(k * k, axis=1, keepdims=True))
        best = lax.fori_loop(0, n_chunks, body, jnp.zeros((tk, 1), F32))
        kmax_ref[...] = jnp.broadcast_to(jnp.sqrt(jnp.max(best, axis=0, keepdims=True)), kmax_ref.shape)

    qs_ref[HEAD_DIM:, :] = jnp.zeros((128 - HEAD_DIM, Q_GROUP * tq), BF16)
    for g in range(Q_GROUP):
        qs_ref[:HEAD_DIM, g * tq:(g + 1) * tq] = qt_ref[g * HEAD_DIM:(g + 1) * HEAD_DIM, :]
    qf = qs_ref[...].astype(F32)
    bound = jnp.sqrt(jnp.sum(qf * qf, axis=0, keepdims=True)) * kmax_ref[0:1, 0:1] * ATT_BOUND_SLACK

    def scores(c):
        return _dot(keys(c), qs_ref[...])

    def values(c):
        return vt_ref[:ATT_V_ROWS, pl.ds(pl.multiple_of(c * tk, tk), tk)]

    def weights(s_buf):
        return jnp.exp2(s_buf[...] - bound).astype(BF16)

    def accumulate(p_buf, c):
        acc_ref[...] += _dot(values(c), p_buf[...])

    last = n_chunks - 1

    def fast(c2, carry):
        c = 2 * c2
        s0_ref[...] = scores(jnp.minimum(c + 2, last))
        p1_ref[...] = weights(s1_ref)
        accumulate(p0_ref, c)
        s1_ref[...] = scores(jnp.minimum(c + 3, last))
        p0_ref[...] = weights(s0_ref)
        accumulate(p1_ref, c + 1)
        return carry

    acc_ref[...] = jnp.zeros(acc_ref.shape, F32)
    s0_ref[...] = scores(0)
    p0_ref[...] = weights(s0_ref)
    s1_ref[...] = scores(1)
    lax.fori_loop(0, n_chunks // 2, fast, 0, unroll=4)
    underflow = jnp.min(acc_ref[HEAD_DIM:HEAD_DIM + 1, :]) < ATT_MIN_ROW_SUM

    @pl.when(underflow)
    def _():
        def safe(c, carry):
            s = scores(c)
            m_old = m_ref[...]
            m_new = jnp.maximum(m_old, jnp.max(s, axis=0, keepdims=True))
            acc_ref[...] = (jnp.exp2(m_old - m_new) * acc_ref[...]
                            + _dot(values(c), jnp.exp2(s - m_new).astype(BF16)))
            m_ref[...] = m_new
            return carry

        m_ref[...] = jnp.full(m_ref.shape, NEG_INF, F32)
        acc_ref[...] = jnp.zeros(acc_ref.shape, F32)
        lax.fori_loop(0, n_chunks, safe, 0)

    acc = acc_ref[...]
    ot = acc[:HEAD_DIM, :] / acc[HEAD_DIM:HEAD_DIM + 1, :]
    ot = jnp.concatenate([ot, jnp.zeros((128 - HEAD_DIM, Q_GROUP * tq), F32)], axis=0)
    o = ot.T
    o_ref[...] = jnp.concatenate([o[g * tq:(g + 1) * tq, :HEAD_DIM] for g in range(Q_GROUP)],
                                 axis=1).astype(BF16)


def _attention(qt, k2, vt, B, S):
    T = B * S
    tq = ATT_TQ
    nq = S // tq
    gw = Q_GROUP * HEAD_DIM
    cols = Q_GROUP * tq
    tk = min(ATT_TK, S)
    return pl.pallas_call(
        _attn_kernel, grid=(B, N_KV_HEADS, nq),
        in_specs=[pl.BlockSpec((None, gw, tq), lambda b, h, i: (b, h, i)),
                  pl.BlockSpec((None, None, S, 128), lambda b, h, i: (h, b, 0, 0)),
                  pl.BlockSpec((None, 128, S), lambda b, h, i: (b, h, 0))],
        out_specs=pl.BlockSpec((tq, gw), lambda b, h, i: (b * nq + i, h)),
        out_shape=jax.ShapeDtypeStruct((T, Q_WIDTH), BF16),
        scratch_shapes=[pltpu.VMEM((128, cols), BF16), pltpu.VMEM((8, 128), F32),
                        pltpu.VMEM((ATT_V_ROWS, cols), F32), pltpu.VMEM((1, cols), F32),
                        pltpu.VMEM((tk, cols), F32), pltpu.VMEM((tk, cols), F32),
                        pltpu.VMEM((tk, cols), BF16), pltpu.VMEM((tk, cols), BF16)],
        compiler_params=_cparams("parallel", "parallel", "arbitrary"),
        name="attention")(qt, k2.reshape(N_KV_HEADS, B, S, 128), vt)


def _even_out_kernel(a_ref, t_ref, wa_ref, wt_ref, x_ref, g_ref, b_ref, o_ref):
    mix = _dot(a_ref[...], wa_ref[...]) + _dot(t_ref[...], wt_ref[...])
    o_ref[...] = _ln(DEEPNORM_ALPHA * x_ref[...] + mix, g_ref[...], b_ref[...])


def _even_out(a_out, attn, w_out, x, g, b):
    T, D = x.shape
    tm = ROW_TILE
    row = lambda w: pl.BlockSpec((tm, w), lambda i: (i, 0))
    full = lambda a: pl.BlockSpec(a.shape, lambda i: (0,) * a.ndim)
    wa = w_out[:F_WIDTH].astype(BF16)
    wt = w_out[F_WIDTH:].astype(BF16)
    g = g.reshape(1, D)
    b = b.reshape(1, D)
    return pl.pallas_call(
        _even_out_kernel, grid=(T // tm,),
        in_specs=[row(F_WIDTH), row(Q_WIDTH), full(wa), full(wt), row(D), full(g), full(b)],
        out_specs=row(D), out_shape=jax.ShapeDtypeStruct((T, D), F32),
        compiler_params=_cparams("parallel"), name="even_out")(a_out, attn, wa, wt, x, g, b)


def _odd_kernel(x_ref, wi_ref, bi_ref, vg_ref, vb_ref, ws_ref, bs_ref, wo_ref, g_ref, b_ref, o_ref,
                gate_ref):
    tm = x_ref.shape[0]
    x = x_ref[...]
    h = _dot(x.astype(BF16), wi_ref[...]) + bi_ref[...]
    h = 0.5 * h * (1.0 + lax.erf(h * (1.0 / math.sqrt(2.0))))
    u = h[:, :C_WIDTH]
    v = _ln(h[:, C_WIDTH:], vg_ref[...], vb_ref[...]).astype(BF16)
    for c in range(tm // CHUNK):
        r0 = c * CHUNK
        for gi in range(N_CGROUPS):
            l0 = gi * CGROUP_DIM
            sv = _dot(ws_ref[gi], v[r0:r0 + CHUNK, l0:l0 + CGROUP_DIM]) + bs_ref[gi]
            gate_ref[r0:r0 + CHUNK, l0:l0 + CGROUP_DIM] = (
                u[r0:r0 + CHUNK, l0:l0 + CGROUP_DIM] * sv).astype(BF16)
    mix = _dot(gate_ref[...], wo_ref[...])
    o_ref[...] = _ln(DEEPNORM_ALPHA * x + mix, g_ref[...], b_ref[...])


def _odd_mixer(x, w_in, b_in, v_g, v_b, w_s, b_s, w_out, g, b):
    T, D = x.shape
    tm = ROW_TILE
    row = pl.BlockSpec((tm, D), lambda i: (i, 0))
    full = lambda a: pl.BlockSpec(a.shape, lambda i: (0,) * a.ndim)
    args = [w_in.astype(BF16), b_in.reshape(1, 2 * C_WIDTH), v_g.reshape(1, C_WIDTH),
            v_b.reshape(1, C_WIDTH), w_s.astype(BF16),
            jnp.broadcast_to(b_s[:, :, None], (N_CGROUPS, CHUNK, CGROUP_DIM)).astype(F32),
            w_out.astype(BF16), g.reshape(1, D), b.reshape(1, D)]
    return pl.pallas_call(
        _odd_kernel, grid=(T // tm,),
        in_specs=[row] + [full(a) for a in args],
        out_specs=row, out_shape=jax.ShapeDtypeStruct((T, D), F32),
        scratch_shapes=[pltpu.VMEM((tm, C_WIDTH), BF16)],
        compiler_params=_cparams("parallel"), name="odd_mixer")(x, *args)


def _router_kernel(x_ref, w_ref, rb_ref, tri_ref, eidx_ref, wts_ref, pos_ref, cnt_ref, xp_ref, run_ref):
    tm = x_ref.shape[0]
    i = pl.program_id(0)

    @pl.when(i == 0)
    def _():
        run_ref[...] = jnp.zeros(run_ref.shape, F32)

    x = x_ref[...]
    xp_ref[...] = _pack_halves(x)
    xh = x.astype(BF16)
    xl = (x - xh.astype(F32)).astype(BF16)
    nt = (((1,), (1,)), ((), ()))
    dg = lambda a, c: lax.dot_general(a, c, nt, preferred_element_type=F32)
    logits = dg(w_ref[0], xh) + dg(w_ref[0], xl) + dg(w_ref[1], xh)
    scores = jax.nn.sigmoid(logits)
    sel = scores + rb_ref[...]

    i8 = lax.broadcasted_iota(I32, (GROUP_SIZE, tm), 0)
    gsc_rows = []
    for gidx in range(N_EXPERT_GROUPS):
        sg = sel[gidx * GROUP_SIZE:(gidx + 1) * GROUP_SIZE, :]
        m1 = jnp.max(sg, axis=0, keepdims=True)
        f1 = jnp.min(jnp.where(sg == m1, i8, GROUP_SIZE), axis=0, keepdims=True)
        m2 = jnp.max(jnp.where(i8 == f1, NEG_INF, sg), axis=0, keepdims=True)
        gsc_rows.append(m1 + m2)
    gsc = jnp.concatenate(gsc_rows, axis=0)

    gsel = jnp.zeros(gsc.shape, F32)
    for _ in range(TOPK_GROUPS):
        m = jnp.max(gsc, axis=0, keepdims=True)
        f = jnp.min(jnp.where(gsc == m, i8, N_EXPERT_GROUPS), axis=0, keepdims=True)
        pick = i8 == f
        gsel = jnp.where(pick, 1.0, gsel)
        gsc = jnp.where(pick, NEG_INF, gsc)
    esel = jnp.concatenate(
        [jnp.broadcast_to(gsel[gidx:gidx + 1, :], (GROUP_SIZE, tm)) for gidx in range(N_EXPERT_GROUPS)],
        axis=0)

    cur = jnp.where(esel > 0.0, sel, NEG_INF)
    ei = lax.broadcasted_iota(I32, cur.shape, 0)
    idx_rows, sc_rows = [], []
    chosen = jnp.zeros(cur.shape, F32)
    for _ in range(TOP_K):
        m = jnp.max(cur, axis=0, keepdims=True)
        f = jnp.min(jnp.where(cur == m, ei, N_EXPERTS), axis=0, keepdims=True)
        pick = ei == f
        idx_rows.append(f)
        sc_rows.append(jnp.sum(jnp.where(pick, scores, 0.0), axis=0, keepdims=True))
        chosen = jnp.where(pick, 1.0, chosen)
        cur = jnp.where(pick, NEG_INF, cur)
    eidx = jnp.concatenate(idx_rows, axis=0)
    sc = jnp.concatenate(sc_rows, axis=0)
    eidx_ref[...] = eidx
    wts_ref[...] = sc / jnp.sum(sc, axis=0, keepdims=True) * ROUTE_SCALE

    before = _dot(chosen.astype(BF16), tri_ref[...]) + run_ref[...]
    pos_rows = [jnp.sum(jnp.where(ei == idx_rows[k], before, 0.0), axis=0, keepdims=True)
                for k in range(TOP_K)]
    pos_ref[...] = jnp.concatenate(pos_rows, axis=0).astype(I32)
    run_new = run_ref[...] + jnp.sum(chosen, axis=1, keepdims=True)
    run_ref[...] = run_new
    cnt_ref[...] = jnp.broadcast_to(run_new, cnt_ref.shape).astype(I32)


def _router(x, router_w, router_b, first_row, T):
    D = x.shape[1]
    tm = min(ROUTER_TILE, T)
    n_tiles = T // tm
    first_tile = first_row // tm
    assert n_tiles * tm == T and first_tile * tm == first_row
    wt = router_w.T.astype(F32)
    wh = wt.astype(BF16)
    wl = (wt - wh.astype(F32)).astype(BF16)
    w2 = jnp.stack([wh, wl])
    rb = router_b.astype(F32).reshape(N_EXPERTS, 1)
    tri = jnp.asarray(np.triu(np.ones((tm, tm)), 1), BF16)
    full = lambda a: pl.BlockSpec(a.shape, lambda i: (0,) * a.ndim)
    col = pl.BlockSpec((TOP_K, tm), lambda i: (0, i))
    return pl.pallas_call(
        _router_kernel, grid=(n_tiles,),
        in_specs=[pl.BlockSpec((tm, D), lambda i: (i + first_tile, 0)), full(w2), full(rb), full(tri)],
        out_specs=[col, col, col, pl.BlockSpec((N_EXPERTS, 128), lambda i: (0, 0)),
                   pl.BlockSpec((tm, HALF), lambda i: (i, 0))],
        out_shape=[jax.ShapeDtypeStruct((TOP_K, T), I32), jax.ShapeDtypeStruct((TOP_K, T), F32),
                   jax.ShapeDtypeStruct((TOP_K, T), I32), jax.ShapeDtypeStruct((N_EXPERTS, 128), I32),
                   jax.ShapeDtypeStruct((T, HALF), I32)],
        scratch_shapes=[pltpu.VMEM((N_EXPERTS, 1), F32)],
        compiler_params=_cparams("arbitrary"), name="router")(x, w2, rb, tri)


def _dest_kernel(start_ref, eidx_ref, pos_ref, o_ref):
    e = eidx_ref[...]
    acc = pos_ref[...]
    for j in range(N_EXPERTS):
        acc = acc + jnp.where(e == j, start_ref[j], 0)
    o_ref[...] = acc


def _dest_rows(eidx, pos, seg_start):
    K, T = eidx.shape
    tl = min(T, 2048)
    blk = pl.BlockSpec((K, tl), lambda i, s: (0, i))
    grid_spec = pltpu.PrefetchScalarGridSpec(
        num_scalar_prefetch=1, grid=(T // tl,), in_specs=[blk, blk], out_specs=blk)
    return pl.pallas_call(
        _dest_kernel, grid_spec=grid_spec, out_shape=jax.ShapeDtypeStruct((K, T), I32),
        compiler_params=_cparams("parallel"), name="dest_rows")(seg_start, eidx, pos)


def _gather_rows(table, idx):
    n_rows = idx.shape[0]
    width = table.shape[1]
    per_worker = n_rows // SC_WORKERS
    n_chunks = per_worker // SC_CHUNK
    assert per_worker * SC_WORKERS == n_rows and n_chunks * SC_CHUNK == per_worker
    mesh = plsc.VectorSubcoreMesh(core_axis_name="c", subcore_axis_name="s")

    @functools.partial(
        pl.kernel, mesh=mesh,
        out_type=jax.ShapeDtypeStruct((n_rows, width), table.dtype),
        scratch_types=[pltpu.VMEM((SC_CHUNK,), I32), pltpu.VMEM((SC_CHUNK, width), table.dtype),
                       pltpu.SemaphoreType.DMA])
    def gather(table_hbm, idx_hbm, out_hbm, idx_v, rows_v, sem):
        wid = lax.axis_index("s") * SC_CORES + lax.axis_index("c")
        base = wid * per_worker

        @pl.loop(0, n_chunks)
        def _(j):
            off = base + j * SC_CHUNK
            pltpu.sync_copy(idx_hbm.at[pl.ds(off, SC_CHUNK)], idx_v)
            pltpu.async_copy(table_hbm.at[idx_v], rows_v, sem).wait()
            pltpu.sync_copy(rows_v, out_hbm.at[pl.ds(off, SC_CHUNK)])

    return gather(table, idx)


def _scatter_rows(rows, dest):
    n_tok, width = rows.shape
    n_dst = dest.shape[0]
    per_worker = n_tok // SC_WORKERS
    n_chunks = per_worker // SC_CHUNK
    assert per_worker * SC_WORKERS == n_tok and n_chunks * SC_CHUNK == per_worker
    mesh = plsc.VectorSubcoreMesh(core_axis_name="c", subcore_axis_name="s")

    @functools.partial(
        pl.kernel, mesh=mesh,
        out_type=jax.ShapeDtypeStruct((n_dst * n_tok, width), rows.dtype),
        scratch_types=[pltpu.VMEM((n_dst, SC_CHUNK), I32), pltpu.VMEM((SC_CHUNK, width), rows.dtype),
                       pltpu.SemaphoreType.DMA])
    def scatter(rows_hbm, dest_hbm, out_hbm, idx_v, rows_v, sem):
        wid = lax.axis_index("s") * SC_CORES + lax.axis_index("c")
        base = wid * per_worker

        @pl.loop(0, n_chunks)
        def _(j):
            off = base + j * SC_CHUNK
            pltpu.sync_copy(dest_hbm.at[:, pl.ds(off, SC_CHUNK)], idx_v)
            pltpu.sync_copy(rows_hbm.at[pl.ds(off, SC_CHUNK)], rows_v)
            copies = [pltpu.async_copy(rows_v, out_hbm.at[idx_v.at[k]], sem) for k in range(n_dst)]
            for c in copies:
                c.wait()

    return scatter(rows, dest)


def _expert_kernel(blk_ref, exp_ref, lo_ref, hi_ref, slot_ref, nxt_ref, xs_hbm, wg_hbm, wu_hbm, wd_hbm,
                   ys_ref, wg_buf, wu_buf, wd_buf, wgu_s, wd_s, sem, xs_buf, xs_sem, *, layer):
    i = pl.program_id(0)
    prev = jnp.maximum(i - 1, 0)

    def weight_copies(expert, slot):
        return [pltpu.make_async_copy(src.at[layer, expert], dst.at[slot], sem.at[slot])
                for src, dst in ((wg_hbm, wg_buf), (wu_hbm, wu_buf), (wd_hbm, wd_buf))]

    @pl.when(i == 0)
    def _():
        for c in weight_copies(exp_ref[0], slot_ref[0]):
            c.start()

    @pl.when(jnp.logical_or(i == 0, exp_ref[i] != exp_ref[prev]))
    def _():
        slot = slot_ref[i]
        for c in weight_copies(exp_ref[i], slot):
            c.wait()
        wgu_s[:, :EXPERT_DIM] = wg_buf[slot].astype(BF16)
        wgu_s[:, EXPERT_DIM:] = wu_buf[slot].astype(BF16)
        wd_s[...] = wd_buf[slot].astype(BF16)

        @pl.when(nxt_ref[i] >= 0)
        def _():
            for c in weight_copies(nxt_ref[i], 1 - slot):
                c.start()

    n_blocks = xs_hbm.shape[0] // EXPERT_ROWS
    blk = blk_ref[i]
    first = jnp.logical_or(i == 0, blk != blk_ref[prev])

    def rows_copy(block):
        slot = block % EXPERT_XS_SLOTS
        src = xs_hbm.at[pl.ds(pl.multiple_of(block * EXPERT_ROWS, EXPERT_ROWS), EXPERT_ROWS)]
        return pltpu.make_async_copy(src, xs_buf.at[slot], xs_sem.at[slot])

    @pl.when(i == 0)
    def _():
        for b0 in range(min(EXPERT_XS_SLOTS - 1, n_blocks)):
            rows_copy(b0).start()

    @pl.when(first)
    def _():
        rows_copy(blk).wait()

        @pl.when(blk + EXPERT_XS_SLOTS - 1 < n_blocks)
        def _():
            rows_copy(blk + EXPERT_XS_SLOTS - 1).start()

    xs_ref = xs_buf.at[blk % EXPERT_XS_SLOTS]
    lo = lo_ref[i]
    hi = hi_ref[i]

    def sub_block(r0):
        rows = slice(r0, r0 + EXPERT_SUB_ROWS)
        w = xs_ref[rows, :]
        xlo = lax.bitcast_convert_type(w.astype(jnp.int16), BF16)
        xhi = lax.bitcast_convert_type(lax.shift_right_logical(w, 16).astype(jnp.int16), BF16)
        gu = _dot(xlo, wgu_s[:HALF, :]) + _dot(xhi, wgu_s[HALF:, :])
        g = gu[:, :EXPERT_DIM]
        hb = (g * jax.nn.sigmoid(g) * gu[:, EXPERT_DIM:]).astype(BF16)
        y = _dot(hb, wd_s[...])
        packed = pltpu.pack_elementwise([y[:, :HALF], y[:, HALF:]], packed_dtype=BF16)
        row = r0 + lax.broadcasted_iota(I32, (EXPERT_SUB_ROWS, 1), 0)
        mine = jnp.logical_and(row >= lo, row < hi)
        kept = jnp.where(first, 0, ys_ref[rows, :])
        ys_ref[rows, :] = jnp.where(mine, packed, kept)

    def touched(r0):
        return jnp.logical_and(lo < r0 + EXPERT_SUB_ROWS, hi > r0)

    for r0 in range(0, EXPERT_ROWS, 2 * EXPERT_SUB_ROWS):
        r1 = r0 + EXPERT_SUB_ROWS
        t0, t1 = touched(r0), touched(r1)

        @pl.when(jnp.logical_and(t0, t1))
        def _():
            sub_block(r0)
            sub_block(r1)

        pl.when(jnp.logical_and(t0, jnp.logical_not(t1)))(functools.partial(sub_block, r0))
        pl.when(jnp.logical_and(jnp.logical_not(t0), t1))(functools.partial(sub_block, r1))


def _expert_items(counts, n_rows):
    n_blocks = n_rows // EXPERT_ROWS
    n_items = n_blocks + N_EXPERTS - 1
    end = jnp.cumsum(counts)
    start = end - counts
    first_blk = start // EXPERT_ROWS
    n_blk = jnp.where(counts > 0, (end - 1) // EXPERT_ROWS - first_blk + 1, 0)
    item_end = jnp.cumsum(n_blk)
    item_start = item_end - n_blk
    slot = jnp.arange(n_items, dtype=I32)
    e = jnp.minimum(jnp.sum((item_end[None, :] <= slot[:, None]).astype(I32), axis=1), N_EXPERTS - 1)
    onehot = (e[:, None] == jnp.arange(N_EXPERTS, dtype=I32)[None, :]).astype(I32)
    pick = lambda v: jnp.sum(onehot * v[None, :], axis=1)
    valid = slot < item_end[-1]
    blk = jnp.where(valid, pick(first_blk) + slot - pick(item_start), n_blocks - 1)
    lo = jnp.clip(pick(start) - blk * EXPERT_ROWS, 0, EXPERT_ROWS)
    hi = jnp.clip(pick(end) - blk * EXPERT_ROWS, 0, EXPERT_ROWS)
    last_e = jnp.max(jnp.where(counts > 0, jnp.arange(N_EXPERTS, dtype=I32), 0))
    e = jnp.where(valid, e, last_e)
    hi = jnp.where(valid, hi, 0)
    lo = jnp.where(valid, lo, 0)
    change = jnp.concatenate([jnp.ones((1,), I32), (e[1:] != e[:-1]).astype(I32)])
    slot = (jnp.cumsum(change) - 1) % 2
    later = jnp.where(jnp.arange(N_EXPERTS, dtype=I32)[None, :] > e[:, None], counts[None, :] > 0, False)
    nxt = jnp.where(jnp.any(later, axis=1), jnp.argmax(later, axis=1), -1)
    return tuple(a.astype(I32) for a in (blk, e, lo, hi, slot, nxt))


def _experts(xs, items, w_gate, w_up, w_down, layer):
    n_rows = xs.shape[0]
    n_items = items[0].shape[0]
    blk_map = lambda i, b, *_: (b[i], 0)
    hbm = pl.BlockSpec(memory_space=pl.ANY)
    grid_spec = pltpu.PrefetchScalarGridSpec(
        num_scalar_prefetch=len(items), grid=(n_items,),
        in_specs=[hbm, hbm, hbm, hbm],
        out_specs=pl.BlockSpec((EXPERT_ROWS, HALF), blk_map),
        scratch_shapes=[pltpu.VMEM((2, D_MODEL, EXPERT_DIM), F32),
                        pltpu.VMEM((2, D_MODEL, EXPERT_DIM), F32),
                        pltpu.VMEM((2, EXPERT_DIM, D_MODEL), F32),
                        pltpu.VMEM((D_MODEL, 2 * EXPERT_DIM), BF16),
                        pltpu.VMEM((EXPERT_DIM, D_MODEL), BF16),
                        pltpu.SemaphoreType.DMA((2,)),
                        pltpu.VMEM((EXPERT_XS_SLOTS, EXPERT_ROWS, HALF), I32),
                        pltpu.SemaphoreType.DMA((EXPERT_XS_SLOTS,))])
    return pl.pallas_call(
        functools.partial(_expert_kernel, layer=layer), grid_spec=grid_spec,
        out_shape=jax.ShapeDtypeStruct((n_rows, HALF), I32),
        compiler_params=_cparams("arbitrary"), name="experts")(*items, xs, w_gate, w_up, w_down)


def _moe_out_kernel(x_ref, yg_ref, wt_ref, sgu_ref, sd_ref, g_ref, b_ref, *rest):
    o_ref = rest[-1]
    x = x_ref[...]
    wt = wt_ref[...]
    lo = jnp.zeros((x.shape[0], HALF), F32)
    hi = jnp.zeros((x.shape[0], HALF), F32)
    for k in range(TOP_K):
        w = yg_ref[k]
        wk = wt[:, k:k + 1]
        lo = lo + wk * _unpack_lo(w)
        hi = hi + wk * _unpack_hi(w)
    gu = _dot(x.astype(BF16), sgu_ref[...])
    g = gu[:, :EXPERT_DIM]
    hs = (g * jax.nn.sigmoid(g) * gu[:, EXPERT_DIM:]).astype(BF16)
    ffn = jnp.concatenate([lo, hi], axis=1) + _dot(hs, sd_ref[...])
    o_ref[...] = _ln(DEEPNORM_ALPHA * x + ffn, g_ref[...], b_ref[...])


def _moe_out(x, yg, wts, sh_gate, sh_up, sh_down, g, b, first_tile, partial_out):
    T, D = x.shape
    tm = ROW_TILE
    n_tiles = yg.shape[1] // tm
    xrow = pl.BlockSpec((tm, D), lambda i: (i + first_tile, 0))
    full = lambda a: pl.BlockSpec(a.shape, lambda i: (0,) * a.ndim)
    sgu = jnp.concatenate([sh_gate, sh_up], axis=1).astype(BF16)
    sd = sh_down.astype(BF16)
    g = g.reshape(1, D)
    b = b.reshape(1, D)
    args = [x, yg, wts, sgu, sd, g, b]
    in_specs = [xrow, pl.BlockSpec((TOP_K, tm, HALF), lambda i: (0, i, 0)),
                pl.BlockSpec((tm, TOP_K), lambda i: (i, 0)), full(sgu), full(sd), full(g), full(b)]
    aliases = {}
    if partial_out is not None:
        args.append(partial_out)
        in_specs.append(pl.BlockSpec(memory_space=pl.ANY))
        aliases = {len(args) - 1: 0}
    return pl.pallas_call(
        _moe_out_kernel, grid=(n_tiles,), in_specs=in_specs,
        out_specs=xrow, out_shape=jax.ShapeDtypeStruct((T, D), F32),
        input_output_aliases=aliases,
        compiler_params=_cparams("parallel"), name="moe_out")(*args)


def _moe(x, router_w, router_b, w_gate, w_up, w_down, layer, sh_gate, sh_up, sh_down, g, b):
    T = x.shape[0]
    tiles = T // ROW_TILE // MOE_TOKEN_GROUPS
    tg = tiles * ROW_TILE
    out = None
    for grp in range(MOE_TOKEN_GROUPS):
        eidx, wts, pos, cnt, xp = _router(x, router_w, router_b, grp * tg, tg)
        counts = cnt[:, 0]
        seg_start = (jnp.cumsum(counts) - counts).astype(I32)
        dest = _dest_rows(eidx, pos, seg_start)
        xs = _scatter_rows(xp, dest)
        ys = _experts(xs, _expert_items(counts, tg * TOP_K), w_gate, w_up, w_down, layer)
        yg = _gather_rows(ys, dest.reshape(tg * TOP_K)).reshape(TOP_K, tg, HALF)
        out = _moe_out(x, yg, wts.T, sh_gate, sh_up, sh_down, g, b, grp * tiles, out)
    return out


def kernel(x, ln_in_g, ln_in_b, e_w_in, e_w_fourier, e_q_gain, e_k_gain, e_w_out, o_w_in, o_b_in, o_v_ln_g, o_v_ln_b, o_w_spatial, o_b_spatial, o_w_out, ln_mix_g, ln_mix_b, ln_ffn_g, ln_ffn_b, router_w, router_b, exp_w_gate, exp_w_up, exp_w_down, sh_w_gate, sh_w_up, sh_w_down):
    B, S, D = x.shape
    T = B * S
    h = _layer_norm(x.reshape(T, D), ln_in_g, ln_in_b)
    for i in range(DEPTH):
        j = i // 2
        if i % 2 == 0:
            a, qt, k2, vt = _even_in(h, e_w_in[j], e_q_gain[j], e_k_gain[j], B, S)
            a_out = _fourier(a, e_w_fourier[j], B, S)
            attn = _attention(qt, k2, vt, B, S)
            h = _even_out(a_out, attn, e_w_out[j], h, ln_mix_g[i], ln_mix_b[i])
        else:
            h = _odd_mixer(h, o_w_in[j], o_b_in[j], o_v_ln_g[j], o_v_ln_b[j], o_w_spatial[j],
                           o_b_spatial[j], o_w_out[j], ln_mix_g[i], ln_mix_b[i])
        h = _moe(h, router_w[i], router_b[i], exp_w_gate, exp_w_up, exp_w_down, i,
                 sh_w_gate[i], sh_w_up[i], sh_w_down[i], ln_ffn_g[i], ln_ffn_b[i])
    return h.reshape(B, S, D)
```

```python
import functools
import math

import numpy as np
import jax
import jax.numpy as jnp
from jax import lax
from jax.experimental import pallas as pl
from jax.experimental.pallas import tpu as pltpu
from jax.experimental.pallas import tpu_sc as plsc

F32 = jnp.float32
BF16 = jnp.bfloat16
I32 = jnp.int32

D_MODEL = 1024
DEPTH = 4
GRID_W = 64
N_FGROUPS = 4
FGROUP_DIM = 128
F_WIDTH = N_FGROUPS * FGROUP_DIM
N_HEADS = 8
N_KV_HEADS = 2
HEAD_DIM = 64
Q_GROUP = N_HEADS // N_KV_HEADS
Q_WIDTH = N_HEADS * HEAD_DIM
KV_WIDTH = N_KV_HEADS * HEAD_DIM
ROPE_THETA = 10000.0
ROPE_PAIRS = HEAD_DIM // 4
EVEN_IN_WIDTH = F_WIDTH + Q_WIDTH + 2 * KV_WIDTH
CHUNK = 128
N_CGROUPS = 8
CGROUP_DIM = D_MODEL // N_CGROUPS
C_WIDTH = N_CGROUPS * CGROUP_DIM
N_EXPERTS = 64
EXPERT_DIM = 256
TOP_K = 8
N_EXPERT_GROUPS = 8
GROUP_SIZE = N_EXPERTS // N_EXPERT_GROUPS
TOPK_GROUPS = 4
ROUTE_SCALE = 2.5
LN_EPS = 1e-5
QK_EPS = 1e-6
DEEPNORM_ALPHA = (2 * DEPTH) ** 0.25

VMEM_LIMIT_BYTES = 56 * 1024 * 1024
ROW_TILE = 512
DFT_N1 = 64
DFT_KRON = 4
DFT_PITCH_PAD = 8
ROUTER_TILE = 1024
EXPERT_ROWS = 2048
EXPERT_XS_SLOTS = 3
EXPERT_SUB_ROWS = 512
HALF = D_MODEL // 2
SC_CORES = 2
SC_SUBCORES = 16
SC_WORKERS = SC_CORES * SC_SUBCORES
SC_CHUNK = 128
MOE_TOKEN_GROUPS = 2
ATT_TQ = 256
ATT_TK = 512
ATT_V_ROWS = 80
ATT_BOUND_SLACK = 1.0 + 2.0 ** -7
ATT_MIN_ROW_SUM = 2.0 ** -80
NEG_INF = float("-inf")


def _cparams(*sem):
    return pltpu.CompilerParams(dimension_semantics=sem, vmem_limit_bytes=VMEM_LIMIT_BYTES)


def _ln(x, g, b):
    mu = jnp.mean(x, axis=-1, keepdims=True)
    xc = x - mu
    var = jnp.mean(xc * xc, axis=-1, keepdims=True)
    return xc * lax.rsqrt(var + LN_EPS) * g + b


def _dot(a, b):
    return jnp.dot(a, b, preferred_element_type=F32)


def _pack_halves(y):
    lo = lax.bitcast_convert_type(y[:, :HALF].astype(BF16).astype(F32), I32)
    hi = lax.bitcast_convert_type(y[:, HALF:].astype(BF16).astype(F32), I32)
    return lax.shift_right_logical(lo, 16) | (hi & jnp.int32(-65536))


def _unpack_lo(w):
    return lax.bitcast_convert_type(lax.shift_left(w, 16), F32)


def _unpack_hi(w):
    return lax.bitcast_convert_type(w & jnp.int32(-65536), F32)


def _ln_kernel(x_ref, g_ref, b_ref, o_ref):
    o_ref[...] = _ln(x_ref[...], g_ref[...], b_ref[...])


def _layer_norm(x, g, b):
    T, D = x.shape
    row = pl.BlockSpec((ROW_TILE, D), lambda i: (i, 0))
    vec = pl.BlockSpec((1, D), lambda i: (0, 0))
    return pl.pallas_call(
        _ln_kernel, grid=(T // ROW_TILE,), in_specs=[row, vec, vec], out_specs=row,
        out_shape=jax.ShapeDtypeStruct((T, D), F32), compiler_params=_cparams("parallel"),
        name="ln_in")(x, g.reshape(1, D), b.reshape(1, D))


def _even_in_kernel(x_ref, w_ref, qm_ref, km_ref, qg_ref, kg_ref, cos_ref, sin_ref,
                    a_ref, qt_ref, k_ref, vt_ref):
    tm = x_ref.shape[0]
    h = _dot(x_ref[...].astype(BF16), w_ref[...])
    a_ref[...] = h[:, :F_WIDTH].astype(BF16)
    q = h[:, F_WIDTH:F_WIDTH + Q_WIDTH]
    k = h[:, F_WIDTH + Q_WIDTH:F_WIDTH + Q_WIDTH + KV_WIDTH]
    v = h[:, F_WIDTH + Q_WIDTH + KV_WIDTH:]
    cos = cos_ref[...]
    sin = sin_ref[...]
    lane = lax.broadcasted_iota(I32, (tm, 128), 1)
    first_of_pair = (lane & ROPE_PAIRS) == 0

    def mean_sq(xf, m_ref):
        sq = xf * xf
        hi = sq.astype(BF16)
        lo = (sq - hi.astype(F32)).astype(BF16)
        return _dot(hi, m_ref[...]) + _dot(lo, m_ref[...])

    def rope(xn):
        sw = jnp.where(first_of_pair, pltpu.roll(xn, 128 - ROPE_PAIRS, 1), pltpu.roll(xn, ROPE_PAIRS, 1))
        return xn * cos + sw * sin

    qn = q * lax.rsqrt(mean_sq(q, qm_ref) + QK_EPS) * qg_ref[...]
    scale = math.log2(math.e) / math.sqrt(HEAD_DIM)
    for c in range(Q_WIDTH // 128):
        qt_ref[c * 128:(c + 1) * 128, :] = (rope(qn[:, c * 128:(c + 1) * 128]) * scale).T.astype(BF16)
    kn = rope(k * lax.rsqrt(mean_sq(k, km_ref) + QK_EPS) * kg_ref[...])
    low = lane < HEAD_DIM
    k_ref[0] = jnp.where(low, kn, 0.0).astype(BF16)
    k_ref[1] = jnp.where(low, pltpu.roll(kn, HEAD_DIM, 1), 0.0).astype(BF16)
    ones_col = jnp.where(lane == HEAD_DIM, 1.0, 0.0)
    vt_ref[0:128, :] = jnp.where(low, v, ones_col).T.astype(BF16)
    vt_ref[128:256, :] = jnp.where(low, pltpu.roll(v, HEAD_DIM, 1), ones_col).T.astype(BF16)


def _rope_tables(S):
    t = np.arange(S)
    inv = ROPE_THETA ** (-np.arange(ROPE_PAIRS, dtype=np.float64) / ROPE_PAIRS)
    ang_r = (t // GRID_W)[:, None] * inv
    ang_c = (t % GRID_W)[:, None] * inv
    cos = np.concatenate([np.cos(ang_r), np.cos(ang_r), np.cos(ang_c), np.cos(ang_c)], axis=1)
    sin = np.concatenate([-np.sin(ang_r), np.sin(ang_r), -np.sin(ang_c), np.sin(ang_c)], axis=1)
    return (jnp.asarray(np.tile(cos, (1, 2)), F32), jnp.asarray(np.tile(sin, (1, 2)), F32))


def _head_mean_matrix(width):
    m = np.kron(np.eye(width // HEAD_DIM), np.full((HEAD_DIM, HEAD_DIM), 1.0 / HEAD_DIM))
    return jnp.asarray(m, BF16)


def _even_in(x, w_in, q_gain, k_gain, B, S):
    T, D = x.shape
    tm = ROW_TILE
    ns = S // tm
    cos, sin = _rope_tables(S)
    row = lambda w: pl.BlockSpec((tm, w), lambda i: (i, 0))
    full = lambda a: pl.BlockSpec(a.shape, lambda i: (0,) * a.ndim)
    tab = pl.BlockSpec((tm, 128), lambda i: (i % ns, 0))
    w = w_in.astype(BF16)
    qm = _head_mean_matrix(Q_WIDTH)
    km = _head_mean_matrix(KV_WIDTH)
    qg = jnp.tile(q_gain.astype(F32), N_HEADS).reshape(1, Q_WIDTH)
    kg = jnp.tile(k_gain.astype(F32), N_KV_HEADS).reshape(1, KV_WIDTH)
    return pl.pallas_call(
        _even_in_kernel, grid=(T // tm,),
        in_specs=[row(D), full(w), full(qm), full(km), full(qg), full(kg), tab, tab],
        out_specs=[row(F_WIDTH),
                   pl.BlockSpec((None, Q_WIDTH, tm), lambda i: (i // ns, 0, i % ns)),
                   pl.BlockSpec((N_KV_HEADS, tm, 128), lambda i: (0, i, 0)),
                   pl.BlockSpec((None, N_KV_HEADS * 128, tm), lambda i: (i // ns, 0, i % ns))],
        out_shape=[jax.ShapeDtypeStruct((T, F_WIDTH), BF16),
                   jax.ShapeDtypeStruct((B, Q_WIDTH, S), BF16),
                   jax.ShapeDtypeStruct((N_KV_HEADS, T, 128), BF16),
                   jax.ShapeDtypeStruct((B, N_KV_HEADS * 128, S), BF16)],
        compiler_params=_cparams("parallel"), name="even_in")(x, w, qm, km, qg, kg, cos, sin)


def _fourier_kernel(a_ref, dftc_ref, taba_ref, kc_ref, ks_ref, wf_ref, o_ref,
                    zr_ref, zi_ref, ur_ref, ui_ref, y_ref):
    S = a_ref.shape[0]
    n1_count = DFT_N1
    n2_count = S // DFT_N1
    pz = n1_count + DFT_PITCH_PAD
    pu = n2_count + DFT_PITCH_PAD
    blk = DFT_KRON * DFT_N1
    scale = 1.0 / math.sqrt(S * FGROUP_DIM)

    def channel_dft(j, carry):
        zz = _dot(a_ref[pl.ds(pl.multiple_of(j * blk, blk), blk), :], dftc_ref[...])
        for q in range(DFT_KRON):
            dst = pl.ds(pl.multiple_of((j * DFT_KRON + q) * pz, 8), n1_count)
            zr_ref[dst, :] = zz[q * n1_count:(q + 1) * n1_count, :FGROUP_DIM]
            zi_ref[dst, :] = zz[q * n1_count:(q + 1) * n1_count, FGROUP_DIM:]
        return carry

    lax.fori_loop(0, S // blk, channel_dft, 0, unroll=2)

    def stage_a(n1, carry):
        src = pl.ds(n1, n2_count, stride=pz)
        zn = jnp.concatenate([zr_ref[src, :], zi_ref[src, :]], axis=1).astype(BF16)
        r = _dot(taba_ref[n1], zn)
        dst = pl.ds(pl.multiple_of(n1 * pu, 8), n2_count)
        ur_ref[dst, :] = r[:n2_count, :FGROUP_DIM] + r[n2_count:, FGROUP_DIM:]
        ui_ref[dst, :] = r[:n2_count, FGROUP_DIM:] - r[n2_count:, :FGROUP_DIM]
        return carry

    lax.fori_loop(0, n1_count, stage_a, 0, unroll=4)

    def stage_b(j, carry):
        srcs = [pl.ds(j * DFT_KRON + q, n1_count, stride=pu) for q in range(DFT_KRON)]
        ur = jnp.concatenate([ur_ref[s, :] for s in srcs], axis=0).astype(BF16)
        ui = jnp.concatenate([ui_ref[s, :] for s in srcs], axis=0).astype(BF16)
        re = _dot(kc_ref[...], ur) + _dot(ks_ref[...], ui)
        out = _dot((re * scale).astype(BF16), wf_ref[...])
        for q in range(DFT_KRON):
            y_ref[srcs[q], :] = out[q * n1_count:(q + 1) * n1_count]
        return carry

    lax.fori_loop(0, S // blk, stage_b, 0, unroll=4)

    def compact(k1, carry):
        o_ref[pl.ds(pl.multiple_of(k1 * n2_count, n2_count), n2_count), :] = (
            y_ref[pl.ds(pl.multiple_of(k1 * pu, 8), n2_count), :].astype(BF16))
        return carry

    lax.fori_loop(0, n1_count, compact, 0)


def _dft_tables(S):
    n1c, n2c = DFT_N1, S // DFT_N1
    c = np.arange(FGROUP_DIM)
    ang = 2 * np.pi * np.outer(c, c) / FGROUP_DIM
    dftc = np.concatenate([np.cos(ang), -np.sin(ang)], axis=1)
    n1 = np.arange(n1c)[:, None, None]
    k2 = np.arange(n2c)[None, :, None]
    n2 = np.arange(n2c)[None, None, :]
    th = 2 * np.pi * (n2 * k2 / n2c + n1 * k2 / S)
    taba = np.concatenate([np.cos(th), np.sin(th)], axis=1)
    k1 = np.arange(n1c)
    g = 2 * np.pi * np.outer(k1, k1) / n1c
    eye = np.eye(DFT_KRON)
    kc = np.kron(eye, np.cos(g))
    ks = np.kron(eye, np.sin(g))
    return tuple(jnp.asarray(t, BF16) for t in (dftc, taba, kc, ks))


def _fourier(a, w_fourier, B, S):
    T = a.shape[0]
    dftc, taba, kc, ks = _dft_tables(S)
    full = lambda t: pl.BlockSpec(t.shape, lambda b, g: (0,) * t.ndim)
    blk = pl.BlockSpec((S, FGROUP_DIM), lambda b, g: (b, g))
    return pl.pallas_call(
        _fourier_kernel, grid=(B, N_FGROUPS),
        in_specs=[blk, full(dftc), full(taba), full(kc), full(ks),
                  pl.BlockSpec((None, FGROUP_DIM, FGROUP_DIM), lambda b, g: (g, 0, 0))],
        out_specs=blk,
        out_shape=jax.ShapeDtypeStruct((T, F_WIDTH), BF16),
        scratch_shapes=(
            [pltpu.VMEM((S // DFT_N1 * (DFT_N1 + DFT_PITCH_PAD), FGROUP_DIM), F32)] * 2
            + [pltpu.VMEM((DFT_N1 * (S // DFT_N1 + DFT_PITCH_PAD), FGROUP_DIM), F32)] * 3),
        compiler_params=_cparams("parallel", "parallel"), name="fourier")(
            a, dftc, taba, kc, ks, w_fourier.astype(BF16))


def _attn_kernel(qt_ref, k_ref, vt_ref, o_ref, qs_ref, kmax_ref, acc_ref, m_ref, s0_ref, s1_ref,
                 p0_ref, p1_ref):
    tq = qt_ref.shape[1]
    n_keys = k_ref.shape[0]
    tk = min(ATT_TK, n_keys)
    n_chunks = n_keys // tk
    assert n_chunks % 2 == 0 and n_chunks * tk == n_keys

    def keys(c):
        return k_ref[pl.ds(pl.multiple_of(c * tk, tk), tk), :]

    @pl.when(pl.program_id(2) == 0)
    def _():
        def body(c, best):
            k = keys(c).astype(F32)
            return jnp.maximum(best, jnp.sum(k * k, axis=1, keepdims=True))
        best = lax.fori_loop(0, n_chunks, body, jnp.zeros((tk, 1), F32))
        kmax_ref[...] = jnp.broadcast_to(jnp.sqrt(jnp.max(best, axis=0, keepdims=True)), kmax_ref.shape)

    qs_ref[HEAD_DIM:, :] = jnp.zeros((128 - HEAD_DIM, Q_GROUP * tq), BF16)
    for g in range(Q_GROUP):
        qs_ref[:HEAD_DIM, g * tq:(g + 1) * tq] = qt_ref[g * HEAD_DIM:(g + 1) * HEAD_DIM, :]
    qf = qs_ref[...].astype(F32)
    bound = jnp.sqrt(jnp.sum(qf * qf, axis=0, keepdims=True)) * kmax_ref[0:1, 0:1] * ATT_BOUND_SLACK

    def scores(c):
        return _dot(keys(c), qs_ref[...])

    def values(c):
        return vt_ref[:ATT_V_ROWS, pl.ds(pl.multiple_of(c * tk, tk), tk)]

    def weights(s_buf):
        return jnp.exp2(s_buf[...] - bound).astype(BF16)

    def accumulate(p_buf, c):
        acc_ref[...] += _dot(values(c), p_buf[...])

    last = n_chunks - 1

    def fast(c2, carry):
        c = 2 * c2
        s0_ref[...] = scores(jnp.minimum(c + 2, last))
        p1_ref[...] = weights(s1_ref)
        accumulate(p0_ref, c)
        s1_ref[...] = scores(jnp.minimum(c + 3, last))
        p0_ref[...] = weights(s0_ref)
        accumulate(p1_ref, c + 1)
        return carry

    acc_ref[...] = jnp.zeros(acc_ref.shape, F32)
    s0_ref[...] = scores(0)
    p0_ref[...] = weights(s0_ref)
    s1_ref[...] = scores(1)
    lax.fori_loop(0, n_chunks // 2, fast, 0, unroll=8)
    underflow = jnp.min(acc_ref[HEAD_DIM:HEAD_DIM + 1, :]) < ATT_MIN_ROW_SUM

    @pl.when(underflow)
    def _():
        def safe(c, carry):
            s = scores(c)
            m_old = m_ref[...]
            m_new = jnp.maximum(m_old, jnp.max(s, axis=0, keepdims=True))
            acc_ref[...] = (jnp.exp2(m_old - m_new) * acc_ref[...]
                            + _dot(values(c), jnp.exp2(s - m_new).astype(BF16)))
            m_ref[...] = m_new
            return carry

        m_ref[...] = jnp.full(m_ref.shape, NEG_INF, F32)
        acc_ref[...] = jnp.zeros(acc_ref.shape, F32)
        lax.fori_loop(0, n_chunks, safe, 0)

    acc = acc_ref[...]
    ot = acc[:HEAD_DIM, :] / acc[HEAD_DIM:HEAD_DIM + 1, :]
    ot = jnp.concatenate([ot, jnp.zeros((128 - HEAD_DIM, Q_GROUP * tq), F32)], axis=0)
    o = ot.T
    o_ref[...] = jnp.concatenate([o[g * tq:(g + 1) * tq, :HEAD_DIM] for g in range(Q_GROUP)],
                                 axis=1).astype(BF16)


def _attention(qt, k2, vt, B, S):
    T = B * S
    tq = ATT_TQ
    nq = S // tq
    gw = Q_GROUP * HEAD_DIM
    cols = Q_GROUP * tq
    tk = min(ATT_TK, S)
    return pl.pallas_call(
        _attn_kernel, grid=(B, N_KV_HEADS, nq),
        in_specs=[pl.BlockSpec((None, gw, tq), lambda b, h, i: (b, h, i)),
                  pl.BlockSpec((None, None, S, 128), lambda b, h, i: (h, b, 0, 0)),
                  pl.BlockSpec((None, 128, S), lambda b, h, i: (b, h, 0))],
        out_specs=pl.BlockSpec((tq, gw), lambda b, h, i: (b * nq + i, h)),
        out_shape=jax.ShapeDtypeStruct((T, Q_WIDTH), BF16),
        scratch_shapes=[pltpu.VMEM((128, cols), BF16), pltpu.VMEM((8, 128), F32),
                        pltpu.VMEM((ATT_V_ROWS, cols), F32), pltpu.VMEM((1, cols), F32),
                        pltpu.VMEM((tk, cols), F32), pltpu.VMEM((tk, cols), F32),
                        pltpu.VMEM((tk, cols), BF16), pltpu.VMEM((tk, cols), BF16)],
        compiler_params=_cparams("parallel", "parallel", "arbitrary"),
        name="attention")(qt, k2.reshape(N_KV_HEADS, B, S, 128), vt)


def _even_out_kernel(a_ref, t_ref, wa_ref, wt_ref, x_ref, g_ref, b_ref, o_ref):
    mix = _dot(a_ref[...], wa_ref[...]) + _dot(t_ref[...], wt_ref[...])
    o_ref[...] = _ln(DEEPNORM_ALPHA * x_ref[...] + mix, g_ref[...], b_ref[...])


def _even_out(a_out, attn, w_out, x, g, b):
    T, D = x.shape
    tm = ROW_TILE
    row = lambda w: pl.BlockSpec((tm, w), lambda i: (i, 0))
    full = lambda a: pl.BlockSpec(a.shape, lambda i: (0,) * a.ndim)
    wa = w_out[:F_WIDTH].astype(BF16)
    wt = w_out[F_WIDTH:].astype(BF16)
    g = g.reshape(1, D)
    b = b.reshape(1, D)
    return pl.pallas_call(
        _even_out_kernel, grid=(T // tm,),
        in_specs=[row(F_WIDTH), row(Q_WIDTH), full(wa), full(wt), row(D), full(g), full(b)],
        out_specs=row(D), out_shape=jax.ShapeDtypeStruct((T, D), F32),
        compiler_params=_cparams("parallel"), name="even_out")(a_out, attn, wa, wt, x, g, b)


def _odd_kernel(x_ref, wi_ref, bi_ref, vg_ref, vb_ref, ws_ref, bs_ref, wo_ref, g_ref, b_ref, o_ref,
                gate_ref):
    tm = x_ref.shape[0]
    x = x_ref[...]
    h = _dot(x.astype(BF16), wi_ref[...]) + bi_ref[...]
    h = 0.5 * h * (1.0 + lax.erf(h * (1.0 / math.sqrt(2.0))))
    u = h[:, :C_WIDTH]
    v = _ln(h[:, C_WIDTH:], vg_ref[...], vb_ref[...]).astype(BF16)
    for c in range(tm // CHUNK):
        r0 = c * CHUNK
        for gi in range(N_CGROUPS):
            l0 = gi * CGROUP_DIM
            sv = _dot(ws_ref[gi], v[r0:r0 + CHUNK, l0:l0 + CGROUP_DIM]) + bs_ref[gi]
            gate_ref[r0:r0 + CHUNK, l0:l0 + CGROUP_DIM] = (
                u[r0:r0 + CHUNK, l0:l0 + CGROUP_DIM] * sv).astype(BF16)
    mix = _dot(gate_ref[...], wo_ref[...])
    o_ref[...] = _ln(DEEPNORM_ALPHA * x + mix, g_ref[...], b_ref[...])


def _odd_mixer(x, w_in, b_in, v_g, v_b, w_s, b_s, w_out, g, b):
    T, D = x.shape
    tm = ROW_TILE
    row = pl.BlockSpec((tm, D), lambda i: (i, 0))
    full = lambda a: pl.BlockSpec(a.shape, lambda i: (0,) * a.ndim)
    args = [w_in.astype(BF16), b_in.reshape(1, 2 * C_WIDTH), v_g.reshape(1, C_WIDTH),
            v_b.reshape(1, C_WIDTH), w_s.astype(BF16),
            jnp.broadcast_to(b_s[:, :, None], (N_CGROUPS, CHUNK, CGROUP_DIM)).astype(F32),
            w_out.astype(BF16), g.reshape(1, D), b.reshape(1, D)]
    return pl.pallas_call(
        _odd_kernel, grid=(T // tm,),
        in_specs=[row] + [full(a) for a in args],
        out_specs=row, out_shape=jax.ShapeDtypeStruct((T, D), F32),
        scratch_shapes=[pltpu.VMEM((tm, C_WIDTH), BF16)],
        compiler_params=_cparams("parallel"), name="odd_mixer")(x, *args)


def _router_kernel(x_ref, w_ref, rb_ref, tri_ref, eidx_ref, wts_ref, pos_ref, cnt_ref, xp_ref, run_ref):
    tm = x_ref.shape[0]
    i = pl.program_id(0)

    @pl.when(i == 0)
    def _():
        run_ref[...] = jnp.zeros(run_ref.shape, F32)

    x = x_ref[...]
    xp_ref[...] = _pack_halves(x)
    xh = x.astype(BF16)
    xl = (x - xh.astype(F32)).astype(BF16)
    nt = (((1,), (1,)), ((), ()))
    dg = lambda a, c: lax.dot_general(a, c, nt, preferred_element_type=F32)
    logits = dg(w_ref[0], xh) + dg(w_ref[0], xl) + dg(w_ref[1], xh)
    scores = jax.nn.sigmoid(logits)
    sel = scores + rb_ref[...]

    i8 = lax.broadcasted_iota(I32, (GROUP_SIZE, tm), 0)
    gsc_rows = []
    for gidx in range(N_EXPERT_GROUPS):
        sg = sel[gidx * GROUP_SIZE:(gidx + 1) * GROUP_SIZE, :]
        m1 = jnp.max(sg, axis=0, keepdims=True)
        f1 = jnp.min(jnp.where(sg == m1, i8, GROUP_SIZE), axis=0, keepdims=True)
        m2 = jnp.max(jnp.where(i8 == f1, NEG_INF, sg), axis=0, keepdims=True)
        gsc_rows.append(m1 + m2)
    gsc = jnp.concatenate(gsc_rows, axis=0)

    gsel = jnp.zeros(gsc.shape, F32)
    for _ in range(TOPK_GROUPS):
        m = jnp.max(gsc, axis=0, keepdims=True)
        f = jnp.min(jnp.where(gsc == m, i8, N_EXPERT_GROUPS), axis=0, keepdims=True)
        pick = i8 == f
        gsel = jnp.where(pick, 1.0, gsel)
        gsc = jnp.where(pick, NEG_INF, gsc)
    esel = jnp.concatenate(
        [jnp.broadcast_to(gsel[gidx:gidx + 1, :], (GROUP_SIZE, tm)) for gidx in range(N_EXPERT_GROUPS)],
        axis=0)

    cur = jnp.where(esel > 0.0, sel, NEG_INF)
    ei = lax.broadcasted_iota(I32, cur.shape, 0)
    idx_rows, sc_rows = [], []
    chosen = jnp.zeros(cur.shape, F32)
    for _ in range(TOP_K):
        m = jnp.max(cur, axis=0, keepdims=True)
        f = jnp.min(jnp.where(cur == m, ei, N_EXPERTS), axis=0, keepdims=True)
        pick = ei == f
        idx_rows.append(f)
        sc_rows.append(jnp.sum(jnp.where(pick, scores, 0.0), axis=0, keepdims=True))
        chosen = jnp.where(pick, 1.0, chosen)
        cur = jnp.where(pick, NEG_INF, cur)
    eidx = jnp.concatenate(idx_rows, axis=0)
    sc = jnp.concatenate(sc_rows, axis=0)
    eidx_ref[...] = eidx
    wts_ref[...] = sc / jnp.sum(sc, axis=0, keepdims=True) * ROUTE_SCALE

    before = _dot(chosen.astype(BF16), tri_ref[...]) + run_ref[...]
    pos_rows = [jnp.sum(jnp.where(ei == idx_rows[k], before, 0.0), axis=0, keepdims=True)
                for k in range(TOP_K)]
    pos_ref[...] = jnp.concatenate(pos_rows, axis=0).astype(I32)
    run_new = run_ref[...] + jnp.sum(chosen, axis=1, keepdims=True)
    run_ref[...] = run_new
    cnt_ref[...] = jnp.broadcast_to(run_new, cnt_ref.shape).astype(I32)


def _router(x, router_w, router_b, first_row, T):
    D = x.shape[1]
    tm = min(ROUTER_TILE, T)
    n_tiles = T // tm
    first_tile = first_row // tm
    assert n_tiles * tm == T and first_tile * tm == first_row
    wt = router_w.T.astype(F32)
    wh = wt.astype(BF16)
    wl = (wt - wh.astype(F32)).astype(BF16)
    w2 = jnp.stack([wh, wl])
    rb = router_b.astype(F32).reshape(N_EXPERTS, 1)
    tri = jnp.asarray(np.triu(np.ones((tm, tm)), 1), BF16)
    full = lambda a: pl.BlockSpec(a.shape, lambda i: (0,) * a.ndim)
    col = pl.BlockSpec((TOP_K, tm), lambda i: (0, i))
    return pl.pallas_call(
        _router_kernel, grid=(n_tiles,),
        in_specs=[pl.BlockSpec((tm, D), lambda i: (i + first_tile, 0)), full(w2), full(rb), full(tri)],
        out_specs=[col, col, col, pl.BlockSpec((N_EXPERTS, 128), lambda i: (0, 0)),
                   pl.BlockSpec((tm, HALF), lambda i: (i, 0))],
        out_shape=[jax.ShapeDtypeStruct((TOP_K, T), I32), jax.ShapeDtypeStruct((TOP_K, T), F32),
                   jax.ShapeDtypeStruct((TOP_K, T), I32), jax.ShapeDtypeStruct((N_EXPERTS, 128), I32),
                   jax.ShapeDtypeStruct((T, HALF), I32)],
        scratch_shapes=[pltpu.VMEM((N_EXPERTS, 1), F32)],
        compiler_params=_cparams("arbitrary"), name="router")(x, w2, rb, tri)


def _dest_kernel(start_ref, eidx_ref, pos_ref, o_ref):
    e = eidx_ref[...]
    acc = pos_ref[...]
    for j in range(N_EXPERTS):
        acc = acc + jnp.where(e == j, start_ref[j], 0)
    o_ref[...] = acc


def _dest_rows(eidx, pos, seg_start):
    K, T = eidx.shape
    tl = min(T, 2048)
    blk = pl.BlockSpec((K, tl), lambda i, s: (0, i))
    grid_spec = pltpu.PrefetchScalarGridSpec(
        num_scalar_prefetch=1, grid=(T // tl,), in_specs=[blk, blk], out_specs=blk)
    return pl.pallas_call(
        _dest_kernel, grid_spec=grid_spec, out_shape=jax.ShapeDtypeStruct((K, T), I32),
        compiler_params=_cparams("parallel"), name="dest_rows")(seg_start, eidx, pos)


def _gather_rows(table, idx):
    n_rows = idx.shape[0]
    width = table.shape[1]
    per_worker = n_rows // SC_WORKERS
    n_chunks = per_worker // SC_CHUNK
    assert per_worker * SC_WORKERS == n_rows and n_chunks * SC_CHUNK == per_worker
    mesh = plsc.VectorSubcoreMesh(core_axis_name="c", subcore_axis_name="s")

    @functools.partial(
        pl.kernel, mesh=mesh,
        out_type=jax.ShapeDtypeStruct((n_rows, width), table.dtype),
        scratch_types=[pltpu.VMEM((SC_CHUNK,), I32), pltpu.VMEM((SC_CHUNK, width), table.dtype),
                       pltpu.SemaphoreType.DMA])
    def gather(table_hbm, idx_hbm, out_hbm, idx_v, rows_v, sem):
        wid = lax.axis_index("s") * SC_CORES + lax.axis_index("c")
        base = wid * per_worker

        @pl.loop(0, n_chunks)
        def _(j):
            off = base + j * SC_CHUNK
            pltpu.sync_copy(idx_hbm.at[pl.ds(off, SC_CHUNK)], idx_v)
            pltpu.async_copy(table_hbm.at[idx_v], rows_v, sem).wait()
            pltpu.sync_copy(rows_v, out_hbm.at[pl.ds(off, SC_CHUNK)])

    return gather(table, idx)


def _scatter_rows(rows, dest):
    n_tok, width = rows.shape
    n_dst = dest.shape[0]
    per_worker = n_tok // SC_WORKERS
    n_chunks = per_worker // SC_CHUNK
    assert per_worker * SC_WORKERS == n_tok and n_chunks * SC_CHUNK == per_worker
    mesh = plsc.VectorSubcoreMesh(core_axis_name="c", subcore_axis_name="s")

    @functools.partial(
        pl.kernel, mesh=mesh,
        out_type=jax.ShapeDtypeStruct((n_dst * n_tok, width), rows.dtype),
        scratch_types=[pltpu.VMEM((n_dst, SC_CHUNK), I32), pltpu.VMEM((SC_CHUNK, width), rows.dtype),
                       pltpu.SemaphoreType.DMA])
    def scatter(rows_hbm, dest_hbm, out_hbm, idx_v, rows_v, sem):
        wid = lax.axis_index("s") * SC_CORES + lax.axis_index("c")
        base = wid * per_worker

        @pl.loop(0, n_chunks)
        def _(j):
            off = base + j * SC_CHUNK
            pltpu.sync_copy(dest_hbm.at[:, pl.ds(off, SC_CHUNK)], idx_v)
            pltpu.sync_copy(rows_hbm.at[pl.ds(off, SC_CHUNK)], rows_v)
            copies = [pltpu.async_copy(rows_v, out_hbm.at[idx_v.at[k]], sem) for k in range(n_dst)]
            for c in copies:
                c.wait()

    return scatter(rows, dest)


def _expert_kernel(blk_ref, exp_ref, lo_ref, hi_ref, slot_ref, nxt_ref, xs_hbm, wg_hbm, wu_hbm, wd_hbm,
                   ys_ref, wg_buf, wu_buf, wd_buf, wgu_s, wd_s, sem, xs_buf, xs_sem, *, layer):
    i = pl.program_id(0)
    prev = jnp.maximum(i - 1, 0)

    def weight_copies(expert, slot):
        return [pltpu.make_async_copy(src.at[layer, expert], dst.at[slot], sem.at[slot])
                for src, dst in ((wg_hbm, wg_buf), (wu_hbm, wu_buf), (wd_hbm, wd_buf))]

    @pl.when(i == 0)
    def _():
        for c in weight_copies(exp_ref[0], slot_ref[0]):
            c.start()

    @pl.when(jnp.logical_or(i == 0, exp_ref[i] != exp_ref[prev]))
    def _():
        slot = slot_ref[i]
        for c in weight_copies(exp_ref[i], slot):
            c.wait()
        wgu_s[:, :EXPERT_DIM] = wg_buf[slot].astype(BF16)
        wgu_s[:, EXPERT_DIM:] = wu_buf[slot].astype(BF16)
        wd_s[...] = wd_buf[slot].astype(BF16)

        @pl.when(nxt_ref[i] >= 0)
        def _():
            for c in weight_copies(nxt_ref[i], 1 - slot):
                c.start()

    n_blocks = xs_hbm.shape[0] // EXPERT_ROWS
    blk = blk_ref[i]
    first = jnp.logical_or(i == 0, blk != blk_ref[prev])

    def rows_copy(block):
        slot = block % EXPERT_XS_SLOTS
        src = xs_hbm.at[pl.ds(pl.multiple_of(block * EXPERT_ROWS, EXPERT_ROWS), EXPERT_ROWS)]
        return pltpu.make_async_copy(src, xs_buf.at[slot], xs_sem.at[slot])

    @pl.when(i == 0)
    def _():
        for b0 in range(min(EXPERT_XS_SLOTS - 1, n_blocks)):
            rows_copy(b0).start()

    @pl.when(first)
    def _():
        rows_copy(blk).wait()

        @pl.when(blk + EXPERT_XS_SLOTS - 1 < n_blocks)
        def _():
            rows_copy(blk + EXPERT_XS_SLOTS - 1).start()

    xs_ref = xs_buf.at[blk % EXPERT_XS_SLOTS]
    lo = lo_ref[i]
    hi = hi_ref[i]

    def sub_block(r0):
        rows = slice(r0, r0 + EXPERT_SUB_ROWS)
        w = xs_ref[rows, :]
        xlo = lax.bitcast_convert_type(w.astype(jnp.int16), BF16)
        xhi = lax.bitcast_convert_type(lax.shift_right_logical(w, 16).astype(jnp.int16), BF16)
        gu = _dot(xlo, wgu_s[:HALF, :]) + _dot(xhi, wgu_s[HALF:, :])
        g = gu[:, :EXPERT_DIM]
        hb = (g * jax.nn.sigmoid(g) * gu[:, EXPERT_DIM:]).astype(BF16)
        y = _dot(hb, wd_s[...])
        packed = pltpu.pack_elementwise([y[:, :HALF], y[:, HALF:]], packed_dtype=BF16)
        row = r0 + lax.broadcasted_iota(I32, (EXPERT_SUB_ROWS, 1), 0)
        mine = jnp.logical_and(row >= lo, row < hi)
        kept = jnp.where(first, 0, ys_ref[rows, :])
        ys_ref[rows, :] = jnp.where(mine, packed, kept)

    def touched(r0):
        return jnp.logical_and(lo < r0 + EXPERT_SUB_ROWS, hi > r0)

    for r0 in range(0, EXPERT_ROWS, 2 * EXPERT_SUB_ROWS):
        r1 = r0 + EXPERT_SUB_ROWS
        t0, t1 = touched(r0), touched(r1)

        @pl.when(jnp.logical_and(t0, t1))
        def _():
            sub_block(r0)
            sub_block(r1)

        pl.when(jnp.logical_and(t0, jnp.logical_not(t1)))(functools.partial(sub_block, r0))
        pl.when(jnp.logical_and(jnp.logical_not(t0), t1))(functools.partial(sub_block, r1))


def _expert_items(counts, n_rows):
    n_blocks = n_rows // EXPERT_ROWS
    n_items = n_blocks + N_EXPERTS - 1
    end = jnp.cumsum(counts)
    start = end - counts
    first_blk = start // EXPERT_ROWS
    n_blk = jnp.where(counts > 0, (end - 1) // EXPERT_ROWS - first_blk + 1, 0)
    item_end = jnp.cumsum(n_blk)
    item_start = item_end - n_blk
    slot = jnp.arange(n_items, dtype=I32)
    e = jnp.minimum(jnp.sum((item_end[None, :] <= slot[:, None]).astype(I32), axis=1), N_EXPERTS - 1)
    onehot = (e[:, None] == jnp.arange(N_EXPERTS, dtype=I32)[None, :]).astype(I32)
    pick = lambda v: jnp.sum(onehot * v[None, :], axis=1)
    valid = slot < item_end[-1]
    blk = jnp.where(valid, pick(first_blk) + slot - pick(item_start), n_blocks - 1)
    lo = jnp.clip(pick(start) - blk * EXPERT_ROWS, 0, EXPERT_ROWS)
    hi = jnp.clip(pick(end) - blk * EXPERT_ROWS, 0, EXPERT_ROWS)
    last_e = jnp.max(jnp.where(counts > 0, jnp.arange(N_EXPERTS, dtype=I32), 0))
    e = jnp.where(valid, e, last_e)
    hi = jnp.where(valid, hi, 0)
    lo = jnp.where(valid, lo, 0)
    change = jnp.concatenate([jnp.ones((1,), I32), (e[1:] != e[:-1]).astype(I32)])
    slot = (jnp.cumsum(change) - 1) % 2
    later = jnp.where(jnp.arange(N_EXPERTS, dtype=I32)[None, :] > e[:, None], counts[None, :] > 0, False)
    nxt = jnp.where(jnp.any(later, axis=1), jnp.argmax(later, axis=1), -1)
    return tuple(a.astype(I32) for a in (blk, e, lo, hi, slot, nxt))


def _experts(xs, items, w_gate, w_up, w_down, layer):
    n_rows = xs.shape[0]
    n_items = items[0].shape[0]
    blk_map = lambda i, b, *_: (b[i], 0)
    hbm = pl.BlockSpec(memory_space=pl.ANY)
    grid_spec = pltpu.PrefetchScalarGridSpec(
        num_scalar_prefetch=len(items), grid=(n_items,),
        in_specs=[hbm, hbm, hbm, hbm],
        out_specs=pl.BlockSpec((EXPERT_ROWS, HALF), blk_map),
        scratch_shapes=[pltpu.VMEM((2, D_MODEL, EXPERT_DIM), F32),
                        pltpu.VMEM((2, D_MODEL, EXPERT_DIM), F32),
                        pltpu.VMEM((2, EXPERT_DIM, D_MODEL), F32),
                        pltpu.VMEM((D_MODEL, 2 * EXPERT_DIM), BF16),
                        pltpu.VMEM((EXPERT_DIM, D_MODEL), BF16),
                        pltpu.SemaphoreType.DMA((2,)),
                        pltpu.VMEM((EXPERT_XS_SLOTS, EXPERT_ROWS, HALF), I32),
                        pltpu.SemaphoreType.DMA((EXPERT_XS_SLOTS,))])
    return pl.pallas_call(
        functools.partial(_expert_kernel, layer=layer), grid_spec=grid_spec,
        out_shape=jax.ShapeDtypeStruct((n_rows, HALF), I32),
        compiler_params=_cparams("arbitrary"), name="experts")(*items, xs, w_gate, w_up, w_down)


def _moe_out_kernel(x_ref, yg_ref, wt_ref, sgu_ref, sd_ref, g_ref, b_ref, *rest):
    o_ref = rest[-1]
    x = x_ref[...]
    wt = wt_ref[...]
    lo = jnp.zeros((x.shape[0], HALF), F32)
    hi = jnp.zeros((x.shape[0], HALF), F32)
    for k in range(TOP_K):
        w = yg_ref[k]
        wk = wt[:, k:k + 1]
        lo = lo + wk * _unpack_lo(w)
        hi = hi + wk * _unpack_hi(w)
    gu = _dot(x.astype(BF16), sgu_ref[...])
    g = gu[:, :EXPERT_DIM]
    hs = (g * jax.nn.sigmoid(g) * gu[:, EXPERT_DIM:]).astype(BF16)
    ffn = jnp.concatenate([lo, hi], axis=1) + _dot(hs, sd_ref[...])
    o_ref[...] = _ln(DEEPNORM_ALPHA * x + ffn, g_ref[...], b_ref[...])


def _moe_out(x, yg, wts, sh_gate, sh_up, sh_down, g, b, first_tile, partial_out):
    T, D = x.shape
    tm = ROW_TILE
    n_tiles = yg.shape[1] // tm
    xrow = pl.BlockSpec((tm, D), lambda i: (i + first_tile, 0))
    full = lambda a: pl.BlockSpec(a.shape, lambda i: (0,) * a.ndim)
    sgu = jnp.concatenate([sh_gate, sh_up], axis=1).astype(BF16)
    sd = sh_down.astype(BF16)
    g = g.reshape(1, D)
    b = b.reshape(1, D)
    args = [x, yg, wts, sgu, sd, g, b]
    in_specs = [xrow, pl.BlockSpec((TOP_K, tm, HALF), lambda i: (0, i, 0)),
                pl.BlockSpec((tm, TOP_K), lambda i: (i, 0)), full(sgu), full(sd), full(g), full(b)]
    aliases = {}
    if partial_out is not None:
        args.append(partial_out)
        in_specs.append(pl.BlockSpec(memory_space=pl.ANY))
        aliases = {len(args) - 1: 0}
    return pl.pallas_call(
        _moe_out_kernel, grid=(n_tiles,), in_specs=in_specs,
        out_specs=xrow, out_shape=jax.ShapeDtypeStruct((T, D), F32),
        input_output_aliases=aliases,
        compiler_params=_cparams("parallel"), name="moe_out")(*args)


def _moe(x, router_w, router_b, w_gate, w_up, w_down, layer, sh_gate, sh_up, sh_down, g, b):
    T = x.shape[0]
    tiles = T // ROW_TILE // MOE_TOKEN_GROUPS
    tg = tiles * ROW_TILE
    out = None
    for grp in range(MOE_TOKEN_GROUPS):
        eidx, wts, pos, cnt, xp = _router(x, router_w, router_b, grp * tg, tg)
        counts = cnt[:, 0]
        seg_start = (jnp.cumsum(counts) - counts).astype(I32)
        dest = _dest_rows(eidx, pos, seg_start)
        xs = _scatter_rows(xp, dest)
        ys = _experts(xs, _expert_items(counts, tg * TOP_K), w_gate, w_up, w_down, layer)
        yg = _gather_rows(ys, dest.reshape(tg * TOP_K)).reshape(TOP_K, tg, HALF)
        out = _moe_out(x, yg, wts.T, sh_gate, sh_up, sh_down, g, b, grp * tiles, out)
    return out


def kernel(x, ln_in_g, ln_in_b, e_w_in, e_w_fourier, e_q_gain, e_k_gain, e_w_out, o_w_in, o_b_in, o_v_ln_g, o_v_ln_b, o_w_spatial, o_b_spatial, o_w_out, ln_mix_g, ln_mix_b, ln_ffn_g, ln_ffn_b, router_w, router_b, exp_w_gate, exp_w_up, exp_w_down, sh_w_gate, sh_w_up, sh_w_down):
    B, S, D = x.shape
    T = B * S
    h = _layer_norm(x.reshape(T, D), ln_in_g, ln_in_b)
    for i in range(DEPTH):
        j = i // 2
        if i % 2 == 0:
            a, qt, k2, vt = _even_in(h, e_w_in[j], e_q_gain[j], e_k_gain[j], B, S)
            a_out = _fourier(a, e_w_fourier[j], B, S)
            attn = _attention(qt, k2, vt, B, S)
            h = _even_out(a_out, attn, e_w_out[j], h, ln_mix_g[i], ln_mix_b[i])
        else:
            h = _odd_mixer(h, o_w_in[j], o_b_in[j], o_v_ln_g[j], o_v_ln_b[j], o_w_spatial[j],
                           o_b_spatial[j], o_w_out[j], ln_mix_g[i], ln_mix_b[i])
        h = _moe(h, router_w[i], router_b[i], exp_w_gate, exp_w_up, exp_w_down, i,
                 sh_w_gate[i], sh_w_up[i], sh_w_down[i], ln_ffn_g[i], ln_ffn_b[i])
    return h.reshape(B, S, D)
```

```python
import functools
import math

import numpy as np
import jax
import jax.numpy as jnp
from jax import lax
from jax.experimental import pallas as pl
from jax.experimental.pallas import tpu as pltpu
from jax.experimental.pallas import tpu_sc as plsc

F32 = jnp.float32
BF16 = jnp.bfloat16
I32 = jnp.int32

D_MODEL = 1024
DEPTH = 4
GRID_W = 64
N_FGROUPS = 4
FGROUP_DIM = 128
F_WIDTH = N_FGROUPS * FGROUP_DIM
N_HEADS = 8
N_KV_HEADS = 2
HEAD_DIM = 64
Q_GROUP = N_HEADS // N_KV_HEADS
Q_WIDTH = N_HEADS * HEAD_DIM
KV_WIDTH = N_KV_HEADS * HEAD_DIM
ROPE_THETA = 10000.0
ROPE_PAIRS = HEAD_DIM // 4
EVEN_IN_WIDTH = F_WIDTH + Q_WIDTH + 2 * KV_WIDTH
CHUNK = 128
N_CGROUPS = 8
CGROUP_DIM = D_MODEL // N_CGROUPS
C_WIDTH = N_CGROUPS * CGROUP_DIM
N_EXPERTS = 64
EXPERT_DIM = 256
TOP_K = 8
N_EXPERT_GROUPS = 8
GROUP_SIZE = N_EXPERTS // N_EXPERT_GROUPS
TOPK_GROUPS = 4
ROUTE_SCALE = 2.5
LN_EPS = 1e-5
QK_EPS = 1e-6
DEEPNORM_ALPHA = (2 * DEPTH) ** 0.25

VMEM_LIMIT_BYTES = 56 * 1024 * 1024
ROW_TILE = 512
DFT_N1 = 64
DFT_KRON = 4
DFT_PITCH_PAD = 8
ROUTER_TILE = 1024
EXPERT_ROWS = 2048
EXPERT_XS_SLOTS = 3
EXPERT_SUB_ROWS = 512
HALF = D_MODEL // 2
SC_CORES = 2
SC_SUBCORES = 16
SC_WORKERS = SC_CORES * SC_SUBCORES
SC_CHUNK = 128
MOE_TOKEN_GROUPS = 2
ATT_TQ = 256
ATT_TK = 512
ATT_V_ROWS = 80
ATT_BOUND_SLACK = 1.0 + 2.0 ** -7
ATT_MIN_ROW_SUM = 2.0 ** -80
NEG_INF = float("-inf")


def _cparams(*sem):
    return pltpu.CompilerParams(dimension_semantics=sem, vmem_limit_bytes=VMEM_LIMIT_BYTES)


def _ln(x, g, b):
    mu = jnp.mean(x, axis=-1, keepdims=True)
    xc = x - mu
    var = jnp.mean(xc * xc, axis=-1, keepdims=True)
    return xc * lax.rsqrt(var + LN_EPS) * g + b


def _dot(a, b):
    return jnp.dot(a, b, preferred_element_type=F32)


def _pack_halves(y):
    lo = lax.bitcast_convert_type(y[:, :HALF].astype(BF16).astype(F32), I32)
    hi = lax.bitcast_convert_type(y[:, HALF:].astype(BF16).astype(F32), I32)
    return lax.shift_right_logical(lo, 16) | (hi & jnp.int32(-65536))


def _unpack_lo(w):
    return lax.bitcast_convert_type(lax.shift_left(w, 16), F32)


def _unpack_hi(w):
    return lax.bitcast_convert_type(w & jnp.int32(-65536), F32)


def _ln_kernel(x_ref, g_ref, b_ref, o_ref):
    o_ref[...] = _ln(x_ref[...], g_ref[...], b_ref[...])


def _layer_norm(x, g, b):
    T, D = x.shape
    row = pl.BlockSpec((ROW_TILE, D), lambda i: (i, 0))
    vec = pl.BlockSpec((1, D), lambda i: (0, 0))
    return pl.pallas_call(
        _ln_kernel, grid=(T // ROW_TILE,), in_specs=[row, vec, vec], out_specs=row,
        out_shape=jax.ShapeDtypeStruct((T, D), F32), compiler_params=_cparams("parallel"),
        name="ln_in")(x, g.reshape(1, D), b.reshape(1, D))


def _even_in_kernel(x_ref, w_ref, qm_ref, km_ref, qg_ref, kg_ref, cos_ref, sin_ref,
                    a_ref, qt_ref, k_ref, vt_ref):
    tm = x_ref.shape[0]
    h = _dot(x_ref[...].astype(BF16), w_ref[...])
    a_ref[...] = h[:, :F_WIDTH].astype(BF16)
    q = h[:, F_WIDTH:F_WIDTH + Q_WIDTH]
    k = h[:, F_WIDTH + Q_WIDTH:F_WIDTH + Q_WIDTH + KV_WIDTH]
    v = h[:, F_WIDTH + Q_WIDTH + KV_WIDTH:]
    cos = cos_ref[...]
    sin = sin_ref[...]
    lane = lax.broadcasted_iota(I32, (tm, 128), 1)
    first_of_pair = (lane & ROPE_PAIRS) == 0

    def mean_sq(xf, m_ref):
        sq = xf * xf
        hi = sq.astype(BF16)
        lo = (sq - hi.astype(F32)).astype(BF16)
        return _dot(hi, m_ref[...]) + _dot(lo, m_ref[...])

    def rope(xn):
        sw = jnp.where(first_of_pair, pltpu.roll(xn, 128 - ROPE_PAIRS, 1), pltpu.roll(xn, ROPE_PAIRS, 1))
        return xn * cos + sw * sin

    qn = q * lax.rsqrt(mean_sq(q, qm_ref) + QK_EPS) * qg_ref[...]
    scale = math.log2(math.e) / math.sqrt(HEAD_DIM)
    for c in range(Q_WIDTH // 128):
        qt_ref[c * 128:(c + 1) * 128, :] = (rope(qn[:, c * 128:(c + 1) * 128]) * scale).T.astype(BF16)
    kn = rope(k * lax.rsqrt(mean_sq(k, km_ref) + QK_EPS) * kg_ref[...])
    low = lane < HEAD_DIM
    k_ref[0] = jnp.where(low, kn, 0.0).astype(BF16)
    k_ref[1] = jnp.where(low, pltpu.roll(kn, HEAD_DIM, 1), 0.0).astype(BF16)
    ones_col = jnp.where(lane == HEAD_DIM, 1.0, 0.0)
    vt_ref[0:128, :] = jnp.where(low, v, ones_col).T.astype(BF16)
    vt_ref[128:256, :] = jnp.where(low, pltpu.roll(v, HEAD_DIM, 1), ones_col).T.astype(BF16)


def _rope_tables(S):
    t = np.arange(S)
    inv = ROPE_THETA ** (-np.arange(ROPE_PAIRS, dtype=np.float64) / ROPE_PAIRS)
    ang_r = (t // GRID_W)[:, None] * inv
    ang_c = (t % GRID_W)[:, None] * inv
    cos = np.concatenate([np.cos(ang_r), np.cos(ang_r), np.cos(ang_c), np.cos(ang_c)], axis=1)
    sin = np.concatenate([-np.sin(ang_r), np.sin(ang_r), -np.sin(ang_c), np.sin(ang_c)], axis=1)
    return (jnp.asarray(np.tile(cos, (1, 2)), F32), jnp.asarray(np.tile(sin, (1, 2)), F32))


def _head_mean_matrix(width):
    m = np.kron(np.eye(width // HEAD_DIM), np.full((HEAD_DIM, HEAD_DIM), 1.0 / HEAD_DIM))
    return jnp.asarray(m, BF16)


def _even_in(x, w_in, q_gain, k_gain, B, S):
    T, D = x.shape
    tm = ROW_TILE
    ns = S // tm
    cos, sin = _rope_tables(S)
    row = lambda w: pl.BlockSpec((tm, w), lambda i: (i, 0))
    full = lambda a: pl.BlockSpec(a.shape, lambda i: (0,) * a.ndim)
    tab = pl.BlockSpec((tm, 128), lambda i: (i % ns, 0))
    w = w_in.astype(BF16)
    qm = _head_mean_matrix(Q_WIDTH)
    km = _head_mean_matrix(KV_WIDTH)
    qg = jnp.tile(q_gain.astype(F32), N_HEADS).reshape(1, Q_WIDTH)
    kg = jnp.tile(k_gain.astype(F32), N_KV_HEADS).reshape(1, KV_WIDTH)
    return pl.pallas_call(
        _even_in_kernel, grid=(T // tm,),
        in_specs=[row(D), full(w), full(qm), full(km), full(qg), full(kg), tab, tab],
        out_specs=[row(F_WIDTH),
                   pl.BlockSpec((None, Q_WIDTH, tm), lambda i: (i // ns, 0, i % ns)),
                   pl.BlockSpec((N_KV_HEADS, tm, 128), lambda i: (0, i, 0)),
                   pl.BlockSpec((None, N_KV_HEADS * 128, tm), lambda i: (i // ns, 0, i % ns))],
        out_shape=[jax.ShapeDtypeStruct((T, F_WIDTH), BF16),
                   jax.ShapeDtypeStruct((B, Q_WIDTH, S), BF16),
                   jax.ShapeDtypeStruct((N_KV_HEADS, T, 128), BF16),
                   jax.ShapeDtypeStruct((B, N_KV_HEADS * 128, S), BF16)],
        compiler_params=_cparams("parallel"), name="even_in")(x, w, qm, km, qg, kg, cos, sin)


def _fourier_kernel(a_ref, dftc_ref, taba_ref, kc_ref, ks_ref, wf_ref, o_ref,
                    zr_ref, zi_ref, ur_ref, ui_ref, y_ref):
    S = a_ref.shape[0]
    n1_count = DFT_N1
    n2_count = S // DFT_N1
    pz = n1_count + DFT_PITCH_PAD
    pu = n2_count + DFT_PITCH_PAD
    blk = DFT_KRON * DFT_N1
    scale = 1.0 / math.sqrt(S * FGROUP_DIM)

    def channel_dft(j, carry):
        zz = _dot(a_ref[pl.ds(pl.multiple_of(j * blk, blk), blk), :], dftc_ref[...])
        for q in range(DFT_KRON):
            dst = pl.ds(pl.multiple_of((j * DFT_KRON + q) * pz, 8), n1_count)
            zr_ref[dst, :] = zz[q * n1_count:(q + 1) * n1_count, :FGROUP_DIM]
            zi_ref[dst, :] = zz[q * n1_count:(q + 1) * n1_count, FGROUP_DIM:]
        return carry

    lax.fori_loop(0, S // blk, channel_dft, 0, unroll=2)

    def stage_a(n1, carry):
        src = pl.ds(n1, n2_count, stride=pz)
        zn = jnp.concatenate([zr_ref[src, :], zi_ref[src, :]], axis=1).astype(BF16)
        r = _dot(taba_ref[n1], zn)
        dst = pl.ds(pl.multiple_of(n1 * pu, 8), n2_count)
        ur_ref[dst, :] = r[:n2_count, :FGROUP_DIM] + r[n2_count:, FGROUP_DIM:]
        ui_ref[dst, :] = r[:n2_count, FGROUP_DIM:] - r[n2_count:, :FGROUP_DIM]
        return carry

    lax.fori_loop(0, n1_count, stage_a, 0, unroll=4)

    def stage_b(j, carry):
        srcs = [pl.ds(j * DFT_KRON + q, n1_count, stride=pu) for q in range(DFT_KRON)]
        ur = jnp.concatenate([ur_ref[s, :] for s in srcs], axis=0).astype(BF16)
        ui = jnp.concatenate([ui_ref[s, :] for s in srcs], axis=0).astype(BF16)
        re = _dot(kc_ref[...], ur) + _dot(ks_ref[...], ui)
        out = _dot((re * scale).astype(BF16), wf_ref[...])
        for q in range(DFT_KRON):
            y_ref[srcs[q], :] = out[q * n1_count:(q + 1) * n1_count]
        return carry

    lax.fori_loop(0, S // blk, stage_b, 0, unroll=4)

    def compact(k1, carry):
        o_ref[pl.ds(pl.multiple_of(k1 * n2_count, n2_count), n2_count), :] = (
            y_ref[pl.ds(pl.multiple_of(k1 * pu, 8), n2_count), :].astype(BF16))
        return carry

    lax.fori_loop(0, n1_count, compact, 0)


def _dft_tables(S):
    n1c, n2c = DFT_N1, S // DFT_N1
    c = np.arange(FGROUP_DIM)
    ang = 2 * np.pi * np.outer(c, c) / FGROUP_DIM
    dftc = np.concatenate([np.cos(ang), -np.sin(ang)], axis=1)
    n1 = np.arange(n1c)[:, None, None]
    k2 = np.arange(n2c)[None, :, None]
    n2 = np.arange(n2c)[None, None, :]
    th = 2 * np.pi * (n2 * k2 / n2c + n1 * k2 / S)
    taba = np.concatenate([np.cos(th), np.sin(th)], axis=1)
    k1 = np.arange(n1c)
    g = 2 * np.pi * np.outer(k1, k1) / n1c
    eye = np.eye(DFT_KRON)
    kc = np.kron(eye, np.cos(g))
    ks = np.kron(eye, np.sin(g))
    return tuple(jnp.asarray(t, BF16) for t in (dftc, taba, kc, ks))


def _fourier(a, w_fourier, B, S):
    T = a.shape[0]
    dftc, taba, kc, ks = _dft_tables(S)
    full = lambda t: pl.BlockSpec(t.shape, lambda b, g: (0,) * t.ndim)
    blk = pl.BlockSpec((S, FGROUP_DIM), lambda b, g: (b, g))
    return pl.pallas_call(
        _fourier_kernel, grid=(B, N_FGROUPS),
        in_specs=[blk, full(dftc), full(taba), full(kc), full(ks),
                  pl.BlockSpec((None, FGROUP_DIM, FGROUP_DIM), lambda b, g: (g, 0, 0))],
        out_specs=blk,
        out_shape=jax.ShapeDtypeStruct((T, F_WIDTH), BF16),
        scratch_shapes=(
            [pltpu.VMEM((S // DFT_N1 * (DFT_N1 + DFT_PITCH_PAD), FGROUP_DIM), F32)] * 2
            + [pltpu.VMEM((DFT_N1 * (S // DFT_N1 + DFT_PITCH_PAD), FGROUP_DIM), F32)] * 3),
        compiler_params=_cparams("parallel", "parallel"), name="fourier")(
            a, dftc, taba, kc, ks, w_fourier.astype(BF16))


def _attn_kernel(qt_ref, k_ref, vt_ref, o_ref, qs_ref, kmax_ref, acc_ref, m_ref, s0_ref, s1_ref,
                 p0_ref, p1_ref):
    tq = qt_ref.shape[1]
    n_keys = k_ref.shape[0]
    tk = min(ATT_TK, n_keys)
    n_chunks = n_keys // tk
    assert n_chunks % 2 == 0 and n_chunks * tk == n_keys

    def keys(c):
        return k_ref[pl.ds(pl.multiple_of(c * tk, tk), tk), :]

    @pl.when(pl.program_id(2) == 0)
    def _():
        def body(c, best):
            k = keys(c).astype(F32)
            return jnp.maximum(best, jnp.sum(k * k, axis=1, keepdims=True))
        best = lax.fori_loop(0, n_chunks, body, jnp.zeros((tk, 1), F32))
        kmax_ref[...] = jnp.broadcast_to(jnp.sqrt(jnp.max(best, axis=0, keepdims=True)), kmax_ref.shape)

    qs_ref[HEAD_DIM:, :] = jnp.zeros((128 - HEAD_DIM, Q_GROUP * tq), BF16)
    for g in range(Q_GROUP):
        qs_ref[:HEAD_DIM, g * tq:(g + 1) * tq] = qt_ref[g * HEAD_DIM:(g + 1) * HEAD_DIM, :]
    qf = qs_ref[...].astype(F32)
    bound = jnp.sqrt(jnp.sum(qf * qf, axis=0, keepdims=True)) * kmax_ref[0:1, 0:1] * ATT_BOUND_SLACK

    def scores(c):
        return _dot(keys(c), qs_ref[...])

    def values(c):
        return vt_ref[:ATT_V_ROWS, pl.ds(pl.multiple_of(c * tk, tk), tk)]

    def weights(s_buf):
        return jnp.exp2(s_buf[...] - bound).astype(BF16)

    def accumulate(p_buf, c):
        acc_ref[...] += _dot(values(c), p_buf[...])

    last = n_chunks - 1

    def fast(c2, carry):
        c = 2 * c2
        s0_ref[...] = scores(jnp.minimum(c + 2, last))
        p1_ref[...] = weights(s1_ref)
        accumulate(p0_ref, c)
        s1_ref[...] = scores(jnp.minimum(c + 3, last))
        p0_ref[...] = weights(s0_ref)
        accumulate(p1_ref, c + 1)
        return carry

    acc_ref[...] = jnp.zeros(acc_ref.shape, F32)
    s0_ref[...] = scores(0)
    p0_ref[...] = weights(s0_ref)
    s1_ref[...] = scores(1)
    lax.fori_loop(0, n_chunks // 2, fast, 0, unroll=8)
    underflow = jnp.min(acc_ref[HEAD_DIM:HEAD_DIM + 1, :]) < ATT_MIN_ROW_SUM

    @pl.when(underflow)
    def _():
        def safe(c, carry):
            s = scores(c)
            m_old = m_ref[...]
            m_new = jnp.maximum(m_old, jnp.max(s, axis=0, keepdims=True))
            acc_ref[...] = (jnp.exp2(m_old - m_new) * acc_ref[...]
                            + _dot(values(c), jnp.exp2(s - m_new).astype(BF16)))
            m_ref[...] = m_new
            return carry

        m_ref[...] = jnp.full(m_ref.shape, NEG_INF, F32)
        acc_ref[...] = jnp.zeros(acc_ref.shape, F32)
        lax.fori_loop(0, n_chunks, safe, 0)

    acc = acc_ref[...]
    ot = acc[:HEAD_DIM, :] / acc[HEAD_DIM:HEAD_DIM + 1, :]
    ot = jnp.concatenate([ot, jnp.zeros((128 - HEAD_DIM, Q_GROUP * tq), F32)], axis=0)
    o = ot.T
    o_ref[...] = jnp.concatenate([o[g * tq:(g + 1) * tq, :HEAD_DIM] for g in range(Q_GROUP)],
                                 axis=1).astype(BF16)


def _attention(qt, k2, vt, B, S):
    T = B * S
    tq = ATT_TQ
    nq = S // tq
    gw = Q_GROUP * HEAD_DIM
    cols = Q_GROUP * tq
    tk = min(ATT_TK, S)
    return pl.pallas_call(
        _attn_kernel, grid=(B, N_KV_HEADS, nq),
        in_specs=[pl.BlockSpec((None, gw, tq), lambda b, h, i: (b, h, i)),
                  pl.BlockSpec((None, None, S, 128), lambda b, h, i: (h, b, 0, 0)),
                  pl.BlockSpec((None, 128, S), lambda b, h, i: (b, h, 0))],
        out_specs=pl.BlockSpec((tq, gw), lambda b, h, i: (b * nq + i, h)),
        out_shape=jax.ShapeDtypeStruct((T, Q_WIDTH), BF16),
        scratch_shapes=[pltpu.VMEM((128, cols), BF16), pltpu.VMEM((8, 128), F32),
                        pltpu.VMEM((ATT_V_ROWS, cols), F32), pltpu.VMEM((1, cols), F32),
                        pltpu.VMEM((tk, cols), F32), pltpu.VMEM((tk, cols), F32),
                        pltpu.VMEM((tk, cols), BF16), pltpu.VMEM((tk, cols), BF16)],
        compiler_params=_cparams("parallel", "parallel", "arbitrary"),
        name="attention")(qt, k2.reshape(N_KV_HEADS, B, S, 128), vt)


def _even_out_kernel(a_ref, t_ref, wa_ref, wt_ref, x_ref, g_ref, b_ref, o_ref):
    mix = _dot(a_ref[...], wa_ref[...]) + _dot(t_ref[...], wt_ref[...])
    o_ref[...] = _ln(DEEPNORM_ALPHA * x_ref[...] + mix, g_ref[...], b_ref[...])


def _even_out(a_out, attn, w_out, x, g, b):
    T, D = x.shape
    tm = ROW_TILE
    row = lambda w: pl.BlockSpec((tm, w), lambda i: (i, 0))
    full = lambda a: pl.BlockSpec(a.shape, lambda i: (0,) * a.ndim)
    wa = w_out[:F_WIDTH].astype(BF16)
    wt = w_out[F_WIDTH:].astype(BF16)
    g = g.reshape(1, D)
    b = b.reshape(1, D)
    return pl.pallas_call(
        _even_out_kernel, grid=(T // tm,),
        in_specs=[row(F_WIDTH), row(Q_WIDTH), full(wa), full(wt), row(D), full(g), full(b)],
        out_specs=row(D), out_shape=jax.ShapeDtypeStruct((T, D), F32),
        compiler_params=_cparams("parallel"), name="even_out")(a_out, attn, wa, wt, x, g, b)


def _odd_kernel(x_ref, wi_ref, bi_ref, vg_ref, vb_ref, ws_ref, bs_ref, wo_ref, g_ref, b_ref, o_ref,
                gate_ref):
    tm = x_ref.shape[0]
    x = x_ref[...]
    h = _dot(x.astype(BF16), wi_ref[...]) + bi_ref[...]
    h = 0.5 * h * (1.0 + lax.erf(h * (1.0 / math.sqrt(2.0))))
    u = h[:, :C_WIDTH]
    v = _ln(h[:, C_WIDTH:], vg_ref[...], vb_ref[...]).astype(BF16)
    for c in range(tm // CHUNK):
        r0 = c * CHUNK
        for gi in range(N_CGROUPS):
            l0 = gi * CGROUP_DIM
            sv = _dot(ws_ref[gi], v[r0:r0 + CHUNK, l0:l0 + CGROUP_DIM]) + bs_ref[gi]
            gate_ref[r0:r0 + CHUNK, l0:l0 + CGROUP_DIM] = (
                u[r0:r0 + CHUNK, l0:l0 + CGROUP_DIM] * sv).astype(BF16)
    mix = _dot(gate_ref[...], wo_ref[...])
    o_ref[...] = _ln(DEEPNORM_ALPHA * x + mix, g_ref[...], b_ref[...])


def _odd_mixer(x, w_in, b_in, v_g, v_b, w_s, b_s, w_out, g, b):
    T, D = x.shape
    tm = ROW_TILE
    row = pl.BlockSpec((tm, D), lambda i: (i, 0))
    full = lambda a: pl.BlockSpec(a.shape, lambda i: (0,) * a.ndim)
    args = [w_in.astype(BF16), b_in.reshape(1, 2 * C_WIDTH), v_g.reshape(1, C_WIDTH),
            v_b.reshape(1, C_WIDTH), w_s.astype(BF16),
            jnp.broadcast_to(b_s[:, :, None], (N_CGROUPS, CHUNK, CGROUP_DIM)).astype(F32),
            w_out.astype(BF16), g.reshape(1, D), b.reshape(1, D)]
    return pl.pallas_call(
        _odd_kernel, grid=(T // tm,),
        in_specs=[row] + [full(a) for a in args],
        out_specs=row, out_shape=jax.ShapeDtypeStruct((T, D), F32),
        scratch_shapes=[pltpu.VMEM((tm, C_WIDTH), BF16)],
        compiler_params=_cparams("parallel"), name="odd_mixer")(x, *args)


def _router_kernel(x_ref, w_ref, rb_ref, tri_ref, eidx_ref, wts_ref, pos_ref, cnt_ref, xp_ref, run_ref):
    tm = x_ref.shape[0]
    i = pl.program_id(0)

    @pl.when(i == 0)
    def _():
        run_ref[...] = jnp.zeros(run_ref.shape, F32)

    x = x_ref[...]
    xp_ref[...] = _pack_halves(x)
    xh = x.astype(BF16)
    xl = (x - xh.astype(F32)).astype(BF16)
    nt = (((1,), (1,)), ((), ()))
    dg = lambda a, c: lax.dot_general(a, c, nt, preferred_element_type=F32)
    logits = dg(w_ref[0], xh) + dg(w_ref[0], xl) + dg(w_ref[1], xh)
    scores = jax.nn.sigmoid(logits)
    sel = scores + rb_ref[...]

    i8 = lax.broadcasted_iota(I32, (GROUP_SIZE, tm), 0)
    gsc_rows = []
    for gidx in range(N_EXPERT_GROUPS):
        sg = sel[gidx * GROUP_SIZE:(gidx + 1) * GROUP_SIZE, :]
        m1 = jnp.max(sg, axis=0, keepdims=True)
        f1 = jnp.min(jnp.where(sg == m1, i8, GROUP_SIZE), axis=0, keepdims=True)
        m2 = jnp.max(jnp.where(i8 == f1, NEG_INF, sg), axis=0, keepdims=True)
        gsc_rows.append(m1 + m2)
    gsc = jnp.concatenate(gsc_rows, axis=0)

    gsel = jnp.zeros(gsc.shape, F32)
    for _ in range(TOPK_GROUPS):
        m = jnp.max(gsc, axis=0, keepdims=True)
        f = jnp.min(jnp.where(gsc == m, i8, N_EXPERT_GROUPS), axis=0, keepdims=True)
        pick = i8 == f
        gsel = jnp.where(pick, 1.0, gsel)
        gsc = jnp.where(pick, NEG_INF, gsc)
    esel = jnp.concatenate(
        [jnp.broadcast_to(gsel[gidx:gidx + 1, :], (GROUP_SIZE, tm)) for gidx in range(N_EXPERT_GROUPS)],
        axis=0)

    cur = jnp.where(esel > 0.0, sel, NEG_INF)
    ei = lax.broadcasted_iota(I32, cur.shape, 0)
    idx_rows, sc_rows = [], []
    chosen = jnp.zeros(cur.shape, F32)
    for _ in range(TOP_K):
        m = jnp.max(cur, axis=0, keepdims=True)
        f = jnp.min(jnp.where(cur == m, ei, N_EXPERTS), axis=0, keepdims=True)
        pick = ei == f
        idx_rows.append(f)
        sc_rows.append(jnp.sum(jnp.where(pick, scores, 0.0), axis=0, keepdims=True))
        chosen = jnp.where(pick, 1.0, chosen)
        cur = jnp.where(pick, NEG_INF, cur)
    eidx = jnp.concatenate(idx_rows, axis=0)
    sc = jnp.concatenate(sc_rows, axis=0)
    eidx_ref[...] = eidx
    wts_ref[...] = sc / jnp.sum(sc, axis=0, keepdims=True) * ROUTE_SCALE

    before = _dot(chosen.astype(BF16), tri_ref[...]) + run_ref[...]
    pos_rows = [jnp.sum(jnp.where(ei == idx_rows[k], before, 0.0), axis=0, keepdims=True)
                for k in range(TOP_K)]
    pos_ref[...] = jnp.concatenate(pos_rows, axis=0).astype(I32)
    run_new = run_ref[...] + jnp.sum(chosen, axis=1, keepdims=True)
    run_ref[...] = run_new
    cnt_ref[...] = jnp.broadcast_to(run_new, cnt_ref.shape).astype(I32)


def _router(x, router_w, router_b, first_row, T):
    D = x.shape[1]
    tm = min(ROUTER_TILE, T)
    n_tiles = T // tm
    first_tile = first_row // tm
    assert n_tiles * tm == T and first_tile * tm == first_row
    wt = router_w.T.astype(F32)
    wh = wt.astype(BF16)
    wl = (wt - wh.astype(F32)).astype(BF16)
    w2 = jnp.stack([wh, wl])
    rb = router_b.astype(F32).reshape(N_EXPERTS, 1)
    tri = jnp.asarray(np.triu(np.ones((tm, tm)), 1), BF16)
    full = lambda a: pl.BlockSpec(a.shape, lambda i: (0,) * a.ndim)
    col = pl.BlockSpec((TOP_K, tm), lambda i: (0, i))
    return pl.pallas_call(
        _router_kernel, grid=(n_tiles,),
        in_specs=[pl.BlockSpec((tm, D), lambda i: (i + first_tile, 0)), full(w2), full(rb), full(tri)],
        out_specs=[col, col, col, pl.BlockSpec((N_EXPERTS, 128), lambda i: (0, 0)),
                   pl.BlockSpec((tm, HALF), lambda i: (i, 0))],
        out_shape=[jax.ShapeDtypeStruct((TOP_K, T), I32), jax.ShapeDtypeStruct((TOP_K, T), F32),
                   jax.ShapeDtypeStruct((TOP_K, T), I32), jax.ShapeDtypeStruct((N_EXPERTS, 128), I32),
                   jax.ShapeDtypeStruct((T, HALF), I32)],
        scratch_shapes=[pltpu.VMEM((N_EXPERTS, 1), F32)],
        compiler_params=_cparams("arbitrary"), name="router")(x, w2, rb, tri)


def _dest_kernel(start_ref, eidx_ref, pos_ref, o_ref):
    e = eidx_ref[...]
    acc = pos_ref[...]
    for j in range(N_EXPERTS):
        acc = acc + jnp.where(e == j, start_ref[j], 0)
    o_ref[...] = acc


def _dest_rows(eidx, pos, seg_start):
    K, T = eidx.shape
    tl = min(T, 2048)
    blk = pl.BlockSpec((K, tl), lambda i, s: (0, i))
    grid_spec = pltpu.PrefetchScalarGridSpec(
        num_scalar_prefetch=1, grid=(T // tl,), in_specs=[blk, blk], out_specs=blk)
    return pl.pallas_call(
        _dest_kernel, grid_spec=grid_spec, out_shape=jax.ShapeDtypeStruct((K, T), I32),
        compiler_params=_cparams("parallel"), name="dest_rows")(seg_start, eidx, pos)


def _gather_rows(table, idx):
    n_rows = idx.shape[0]
    width = table.shape[1]
    per_worker = n_rows // SC_WORKERS
    n_chunks = per_worker // SC_CHUNK
    assert per_worker * SC_WORKERS == n_rows and n_chunks * SC_CHUNK == per_worker
    mesh = plsc.VectorSubcoreMesh(core_axis_name="c", subcore_axis_name="s")

    @functools.partial(
        pl.kernel, mesh=mesh,
        out_type=jax.ShapeDtypeStruct((n_rows, width), table.dtype),
        scratch_types=[pltpu.VMEM((SC_CHUNK,), I32), pltpu.VMEM((SC_CHUNK, width), table.dtype),
                       pltpu.SemaphoreType.DMA])
    def gather(table_hbm, idx_hbm, out_hbm, idx_v, rows_v, sem):
        wid = lax.axis_index("s") * SC_CORES + lax.axis_index("c")
        base = wid * per_worker

        @pl.loop(0, n_chunks)
        def _(j):
            off = base + j * SC_CHUNK
            pltpu.sync_copy(idx_hbm.at[pl.ds(off, SC_CHUNK)], idx_v)
            pltpu.async_copy(table_hbm.at[idx_v], rows_v, sem).wait()
            pltpu.sync_copy(rows_v, out_hbm.at[pl.ds(off, SC_CHUNK)])

    return gather(table, idx)


def _scatter_rows(rows, dest):
    n_tok, width = rows.shape
    n_dst = dest.shape[0]
    per_worker = n_tok // SC_WORKERS
    n_chunks = per_worker // SC_CHUNK
    assert per_worker * SC_WORKERS == n_tok and n_chunks * SC_CHUNK == per_worker
    mesh = plsc.VectorSubcoreMesh(core_axis_name="c", subcore_axis_name="s")

    @functools.partial(
        pl.kernel, mesh=mesh,
        out_type=jax.ShapeDtypeStruct((n_dst * n_tok, width), rows.dtype),
        scratch_types=[pltpu.VMEM((n_dst, SC_CHUNK), I32), pltpu.VMEM((SC_CHUNK, width), rows.dtype),
                       pltpu.SemaphoreType.DMA])
    def scatter(rows_hbm, dest_hbm, out_hbm, idx_v, rows_v, sem):
        wid = lax.axis_index("s") * SC_CORES + lax.axis_index("c")
        base = wid * per_worker

        @pl.loop(0, n_chunks)
        def _(j):
            off = base + j * SC_CHUNK
            pltpu.sync_copy(dest_hbm.at[:, pl.ds(off, SC_CHUNK)], idx_v)
            pltpu.sync_copy(rows_hbm.at[pl.ds(off, SC_CHUNK)], rows_v)
            copies = [pltpu.async_copy(rows_v, out_hbm.at[idx_v.at[k]], sem) for k in range(n_dst)]
            for c in copies:
                c.wait()

    return scatter(rows, dest)


def _expert_kernel(blk_ref, exp_ref, lo_ref, hi_ref, slot_ref, nxt_ref, xs_hbm, wg_hbm, wu_hbm, wd_hbm,
                   ys_hbm, wg_buf, wu_buf, wd_buf, wgu_s, wd_s, sem, xs_buf, xs_sem, ys_buf, ys_sem,
                   *, layer):
    i = pl.program_id(0)
    prev = jnp.maximum(i - 1, 0)

    def weight_copies(expert, slot):
        return [pltpu.make_async_copy(src.at[layer, expert], dst.at[slot], sem.at[slot])
                for src, dst in ((wg_hbm, wg_buf), (wu_hbm, wu_buf), (wd_hbm, wd_buf))]

    @pl.when(i == 0)
    def _():
        for c in weight_copies(exp_ref[0], slot_ref[0]):
            c.start()

    @pl.when(jnp.logical_or(i == 0, exp_ref[i] != exp_ref[prev]))
    def _():
        slot = slot_ref[i]
        for c in weight_copies(exp_ref[i], slot):
            c.wait()
        wgu_s[:, :EXPERT_DIM] = wg_buf[slot].astype(BF16)
        wgu_s[:, EXPERT_DIM:] = wu_buf[slot].astype(BF16)
        wd_s[...] = wd_buf[slot].astype(BF16)

        @pl.when(nxt_ref[i] >= 0)
        def _():
            for c in weight_copies(nxt_ref[i], 1 - slot):
                c.start()

    n_blocks = xs_hbm.shape[0] // EXPERT_ROWS
    blk = blk_ref[i]
    first = jnp.logical_or(i == 0, blk != blk_ref[prev])

    def rows_copy(block):
        slot = block % EXPERT_XS_SLOTS
        src = xs_hbm.at[pl.ds(pl.multiple_of(block * EXPERT_ROWS, EXPERT_ROWS), EXPERT_ROWS)]
        return pltpu.make_async_copy(src, xs_buf.at[slot], xs_sem.at[slot])

    @pl.when(i == 0)
    def _():
        for b0 in range(min(EXPERT_XS_SLOTS - 1, n_blocks)):
            rows_copy(b0).start()

    def result_copy(block):
        slot = block % EXPERT_XS_SLOTS
        dst = ys_hbm.at[pl.ds(pl.multiple_of(block * EXPERT_ROWS, EXPERT_ROWS), EXPERT_ROWS)]
        return pltpu.make_async_copy(ys_buf.at[slot], dst, ys_sem.at[slot])

    @pl.when(first)
    def _():
        rows_copy(blk).wait()

        @pl.when(blk + EXPERT_XS_SLOTS - 1 < n_blocks)
        def _():
            rows_copy(blk + EXPERT_XS_SLOTS - 1).start()

        @pl.when(blk >= EXPERT_XS_SLOTS)
        def _():
            result_copy(blk - EXPERT_XS_SLOTS).wait()

    xs_ref = xs_buf.at[blk % EXPERT_XS_SLOTS]
    ys_ref = ys_buf.at[blk % EXPERT_XS_SLOTS]
    lo = lo_ref[i]
    hi = hi_ref[i]

    def sub_block(r0):
        rows = slice(r0, r0 + EXPERT_SUB_ROWS)
        w = xs_ref[rows, :]
        xlo = lax.bitcast_convert_type(w.astype(jnp.int16), BF16)
        xhi = lax.bitcast_convert_type(lax.shift_right_logical(w, 16).astype(jnp.int16), BF16)
        gu = _dot(xlo, wgu_s[:HALF, :]) + _dot(xhi, wgu_s[HALF:, :])
        g = gu[:, :EXPERT_DIM]
        hb = (g * jax.nn.sigmoid(g) * gu[:, EXPERT_DIM:]).astype(BF16)
        y = _dot(hb, wd_s[...])
        packed = pltpu.pack_elementwise([y[:, :HALF], y[:, HALF:]], packed_dtype=BF16)
        row = r0 + lax.broadcasted_iota(I32, (EXPERT_SUB_ROWS, 1), 0)
        mine = jnp.logical_and(row >= lo, row < hi)
        kept = jnp.where(first, 0, ys_ref[rows, :])
        ys_ref[rows, :] = jnp.where(mine, packed, kept)

    def touched(r0):
        return jnp.logical_and(lo < r0 + EXPERT_SUB_ROWS, hi > r0)

    for r0 in range(0, EXPERT_ROWS, 2 * EXPERT_SUB_ROWS):
        r1 = r0 + EXPERT_SUB_ROWS
        t0, t1 = touched(r0), touched(r1)

        @pl.when(jnp.logical_and(t0, t1))
        def _():
            sub_block(r0)
            sub_block(r1)

        pl.when(jnp.logical_and(t0, jnp.logical_not(t1)))(functools.partial(sub_block, r0))
        pl.when(jnp.logical_and(jnp.logical_not(t0), t1))(functools.partial(sub_block, r1))

    n_items = pl.num_programs(0)
    final = i == n_items - 1
    block_done = jnp.logical_or(final, blk_ref[jnp.minimum(i + 1, n_items - 1)] != blk)

    @pl.when(block_done)
    def _():
        result_copy(blk).start()

    @pl.when(final)
    def _():
        for b0 in range(max(n_blocks - EXPERT_XS_SLOTS, 0), n_blocks):
            result_copy(b0).wait()


def _expert_items(counts, n_rows):
    n_blocks = n_rows // EXPERT_ROWS
    n_items = n_blocks + N_EXPERTS - 1
    end = jnp.cumsum(counts)
    start = end - counts
    first_blk = start // EXPERT_ROWS
    n_blk = jnp.where(counts > 0, (end - 1) // EXPERT_ROWS - first_blk + 1, 0)
    item_end = jnp.cumsum(n_blk)
    item_start = item_end - n_blk
    slot = jnp.arange(n_items, dtype=I32)
    e = jnp.minimum(jnp.sum((item_end[None, :] <= slot[:, None]).astype(I32), axis=1), N_EXPERTS - 1)
    onehot = (e[:, None] == jnp.arange(N_EXPERTS, dtype=I32)[None, :]).astype(I32)
    pick = lambda v: jnp.sum(onehot * v[None, :], axis=1)
    valid = slot < item_end[-1]
    blk = jnp.where(valid, pick(first_blk) + slot - pick(item_start), n_blocks - 1)
    lo = jnp.clip(pick(start) - blk * EXPERT_ROWS, 0, EXPERT_ROWS)
    hi = jnp.clip(pick(end) - blk * EXPERT_ROWS, 0, EXPERT_ROWS)
    last_e = jnp.max(jnp.where(counts > 0, jnp.arange(N_EXPERTS, dtype=I32), 0))
    e = jnp.where(valid, e, last_e)
    hi = jnp.where(valid, hi, 0)
    lo = jnp.where(valid, lo, 0)
    change = jnp.concatenate([jnp.ones((1,), I32), (e[1:] != e[:-1]).astype(I32)])
    slot = (jnp.cumsum(change) - 1) % 2
    later = jnp.where(jnp.arange(N_EXPERTS, dtype=I32)[None, :] > e[:, None], counts[None, :] > 0, False)
    nxt = jnp.where(jnp.any(later, axis=1), jnp.argmax(later, axis=1), -1)
    return tuple(a.astype(I32) for a in (blk, e, lo, hi, slot, nxt))


def _experts(xs, items, w_gate, w_up, w_down, layer):
    n_rows = xs.shape[0]
    n_items = items[0].shape[0]
    hbm = pl.BlockSpec(memory_space=pl.ANY)
    grid_spec = pltpu.PrefetchScalarGridSpec(
        num_scalar_prefetch=len(items), grid=(n_items,),
        in_specs=[hbm, hbm, hbm, hbm],
        out_specs=hbm,
        scratch_shapes=[pltpu.VMEM((2, D_MODEL, EXPERT_DIM), F32),
                        pltpu.VMEM((2, D_MODEL, EXPERT_DIM), F32),
                        pltpu.VMEM((2, EXPERT_DIM, D_MODEL), F32),
                        pltpu.VMEM((D_MODEL, 2 * EXPERT_DIM), BF16),
                        pltpu.VMEM((EXPERT_DIM, D_MODEL), BF16),
                        pltpu.SemaphoreType.DMA((2,)),
                        pltpu.VMEM((EXPERT_XS_SLOTS, EXPERT_ROWS, HALF), I32),
                        pltpu.SemaphoreType.DMA((EXPERT_XS_SLOTS,)),
                        pltpu.VMEM((EXPERT_XS_SLOTS, EXPERT_ROWS, HALF), I32),
                        pltpu.SemaphoreType.DMA((EXPERT_XS_SLOTS,))])
    return pl.pallas_call(
        functools.partial(_expert_kernel, layer=layer), grid_spec=grid_spec,
        out_shape=jax.ShapeDtypeStruct((n_rows, HALF), I32),
        compiler_params=_cparams("arbitrary"), name="experts")(*items, xs, w_gate, w_up, w_down)


def _moe_out_kernel(x_ref, yg_ref, wt_ref, sgu_ref, sd_ref, g_ref, b_ref, *rest):
    o_ref = rest[-1]
    x = x_ref[...]
    wt = wt_ref[...]
    lo = jnp.zeros((x.shape[0], HALF), F32)
    hi = jnp.zeros((x.shape[0], HALF), F32)
    for k in range(TOP_K):
        w = yg_ref[k]
        wk = wt[:, k:k + 1]
        lo = lo + wk * _unpack_lo(w)
        hi = hi + wk * _unpack_hi(w)
    gu = _dot(x.astype(BF16), sgu_ref[...])
    g = gu[:, :EXPERT_DIM]
    hs = (g * jax.nn.sigmoid(g) * gu[:, EXPERT_DIM:]).astype(BF16)
    ffn = jnp.concatenate([lo, hi], axis=1) + _dot(hs, sd_ref[...])
    o_ref[...] = _ln(DEEPNORM_ALPHA * x + ffn, g_ref[...], b_ref[...])


def _moe_out(x, yg, wts, sh_gate, sh_up, sh_down, g, b, first_tile, partial_out):
    T, D = x.shape
    tm = ROW_TILE
    n_tiles = yg.shape[1] // tm
    xrow = pl.BlockSpec((tm, D), lambda i: (i + first_tile, 0))
    full = lambda a: pl.BlockSpec(a.shape, lambda i: (0,) * a.ndim)
    sgu = jnp.concatenate([sh_gate, sh_up], axis=1).astype(BF16)
    sd = sh_down.astype(BF16)
    g = g.reshape(1, D)
    b = b.reshape(1, D)
    args = [x, yg, wts, sgu, sd, g, b]
    in_specs = [xrow, pl.BlockSpec((TOP_K, tm, HALF), lambda i: (0, i, 0)),
                pl.BlockSpec((tm, TOP_K), lambda i: (i, 0)), full(sgu), full(sd), full(g), full(b)]
    aliases = {}
    if partial_out is not None:
        args.append(partial_out)
        in_specs.append(pl.BlockSpec(memory_space=pl.ANY))
        aliases = {len(args) - 1: 0}
    return pl.pallas_call(
        _moe_out_kernel, grid=(n_tiles,), in_specs=in_specs,
        out_specs=xrow, out_shape=jax.ShapeDtypeStruct((T, D), F32),
        input_output_aliases=aliases,
        compiler_params=_cparams("parallel"), name="moe_out")(*args)


def _moe(x, router_w, router_b, w_gate, w_up, w_down, layer, sh_gate, sh_up, sh_down, g, b):
    T = x.shape[0]
    tiles = T // ROW_TILE // MOE_TOKEN_GROUPS
    tg = tiles * ROW_TILE
    out = None
    for grp in range(MOE_TOKEN_GROUPS):
        eidx, wts, pos, cnt, xp = _router(x, router_w, router_b, grp * tg, tg)
        counts = cnt[:, 0]
        seg_start = (jnp.cumsum(counts) - counts).astype(I32)
        dest = _dest_rows(eidx, pos, seg_start)
        xs = _scatter_rows(xp, dest)
        ys = _experts(xs, _expert_items(counts, tg * TOP_K), w_gate, w_up, w_down, layer)
        yg = _gather_rows(ys, dest.reshape(tg * TOP_K)).reshape(TOP_K, tg, HALF)
        out = _moe_out(x, yg, wts.T, sh_gate, sh_up, sh_down, g, b, grp * tiles, out)
    return out


def kernel(x, ln_in_g, ln_in_b, e_w_in, e_w_fourier, e_q_gain, e_k_gain, e_w_out, o_w_in, o_b_in, o_v_ln_g, o_v_ln_b, o_w_spatial, o_b_spatial, o_w_out, ln_mix_g, ln_mix_b, ln_ffn_g, ln_ffn_b, router_w, router_b, exp_w_gate, exp_w_up, exp_w_down, sh_w_gate, sh_w_up, sh_w_down):
    B, S, D = x.shape
    T = B * S
    h = _layer_norm(x.reshape(T, D), ln_in_g, ln_in_b)
    for i in range(DEPTH):
        j = i // 2
        if i % 2 == 0:
            a, qt, k2, vt = _even_in(h, e_w_in[j], e_q_gain[j], e_k_gain[j], B, S)
            a_out = _fourier(a, e_w_fourier[j], B, S)
            attn = _attention(qt, k2, vt, B, S)
            h = _even_out(a_out, attn, e_w_out[j], h, ln_mix_g[i], ln_mix_b[i])
        else:
            h = _odd_mixer(h, o_w_in[j], o_b_in[j], o_v_ln_g[j], o_v_ln_b[j], o_w_spatial[j],
                           o_b_spatial[j], o_w_out[j], ln_mix_g[i], ln_mix_b[i])
        h = _moe(h, router_w[i], router_b[i], exp_w_gate, exp_w_up, exp_w_down, i,
                 sh_w_gate[i], sh_w_up[i], sh_w_down[i], ln_ffn_g[i], ln_ffn_b[i])
    return h.reshape(B, S, D)
```

```python
import functools
import math

import numpy as np
import jax
import jax.numpy as jnp
from jax import lax
from jax.experimental import pallas as pl
from jax.experimental.pallas import tpu as pltpu
from jax.experimental.pallas import tpu_sc as plsc

F32 = jnp.float32
BF16 = jnp.bfloat16
I32 = jnp.int32

D_MODEL = 1024
DEPTH = 4
GRID_W = 64
N_FGROUPS = 4
FGROUP_DIM = 128
F_WIDTH = N_FGROUPS * FGROUP_DIM
N_HEADS = 8
N_KV_HEADS = 2
HEAD_DIM = 64
Q_GROUP = N_HEADS // N_KV_HEADS
Q_WIDTH = N_HEADS * HEAD_DIM
KV_WIDTH = N_KV_HEADS * HEAD_DIM
ROPE_THETA = 10000.0
ROPE_PAIRS = HEAD_DIM // 4
EVEN_IN_WIDTH = F_WIDTH + Q_WIDTH + 2 * KV_WIDTH
CHUNK = 128
N_CGROUPS = 8
CGROUP_DIM = D_MODEL // N_CGROUPS
C_WIDTH = N_CGROUPS * CGROUP_DIM
N_EXPERTS = 64
EXPERT_DIM = 256
TOP_K = 8
N_EXPERT_GROUPS = 8
GROUP_SIZE = N_EXPERTS // N_EXPERT_GROUPS
TOPK_GROUPS = 4
ROUTE_SCALE = 2.5
LN_EPS = 1e-5
QK_EPS = 1e-6
DEEPNORM_ALPHA = (2 * DEPTH) ** 0.25

VMEM_LIMIT_BYTES = 56 * 1024 * 1024
ROW_TILE = 512
DFT_N1 = 64
DFT_KRON = 4
DFT_PITCH_PAD = 8
ROUTER_TILE = 1024
EXPERT_ROWS = 2048
EXPERT_XS_SLOTS = 3
EXPERT_SUB_ROWS = 512
HALF = D_MODEL // 2
SC_CORES = 2
SC_SUBCORES = 16
SC_WORKERS = SC_CORES * SC_SUBCORES
SC_CHUNK = 128
MOE_TOKEN_GROUPS = 2
ATT_TQ = 256
ATT_TK = 256
ATT_V_ROWS = 80
ATT_BOUND_SLACK = 1.0 + 2.0 ** -7
ATT_MIN_ROW_SUM = 2.0 ** -80
NEG_INF = float("-inf")


def _cparams(*sem):
    return pltpu.CompilerParams(dimension_semantics=sem, vmem_limit_bytes=VMEM_LIMIT_BYTES)


def _ln(x, g, b):
    mu = jnp.mean(x, axis=-1, keepdims=True)
    xc = x - mu
    var = jnp.mean(xc * xc, axis=-1, keepdims=True)
    return xc * lax.rsqrt(var + LN_EPS) * g + b


def _dot(a, b):
    return jnp.dot(a, b, preferred_element_type=F32)


def _pack_halves(y):
    lo = lax.bitcast_convert_type(y[:, :HALF].astype(BF16).astype(F32), I32)
    hi = lax.bitcast_convert_type(y[:, HALF:].astype(BF16).astype(F32), I32)
    return lax.shift_right_logical(lo, 16) | (hi & jnp.int32(-65536))


def _unpack_lo(w):
    return lax.bitcast_convert_type(lax.shift_left(w, 16), F32)


def _unpack_hi(w):
    return lax.bitcast_convert_type(w & jnp.int32(-65536), F32)


def _ln_kernel(x_ref, g_ref, b_ref, o_ref):
    o_ref[...] = _ln(x_ref[...], g_ref[...], b_ref[...])


def _layer_norm(x, g, b):
    T, D = x.shape
    row = pl.BlockSpec((ROW_TILE, D), lambda i: (i, 0))
    vec = pl.BlockSpec((1, D), lambda i: (0, 0))
    return pl.pallas_call(
        _ln_kernel, grid=(T // ROW_TILE,), in_specs=[row, vec, vec], out_specs=row,
        out_shape=jax.ShapeDtypeStruct((T, D), F32), compiler_params=_cparams("parallel"),
        name="ln_in")(x, g.reshape(1, D), b.reshape(1, D))


def _even_in_kernel(x_ref, w_ref, qm_ref, km_ref, qg_ref, kg_ref, cos_ref, sin_ref,
                    a_ref, qt_ref, k_ref, vt_ref):
    tm = x_ref.shape[0]
    h = _dot(x_ref[...].astype(BF16), w_ref[...])
    a_ref[...] = h[:, :F_WIDTH].astype(BF16)
    q = h[:, F_WIDTH:F_WIDTH + Q_WIDTH]
    k = h[:, F_WIDTH + Q_WIDTH:F_WIDTH + Q_WIDTH + KV_WIDTH]
    v = h[:, F_WIDTH + Q_WIDTH + KV_WIDTH:]
    cos = cos_ref[...]
    sin = sin_ref[...]
    lane = lax.broadcasted_iota(I32, (tm, 128), 1)
    first_of_pair = (lane & ROPE_PAIRS) == 0

    def mean_sq(xf, m_ref):
        sq = xf * xf
        hi = sq.astype(BF16)
        lo = (sq - hi.astype(F32)).astype(BF16)
        return _dot(hi, m_ref[...]) + _dot(lo, m_ref[...])

    def rope(xn):
        sw = jnp.where(first_of_pair, pltpu.roll(xn, 128 - ROPE_PAIRS, 1), pltpu.roll(xn, ROPE_PAIRS, 1))
        return xn * cos + sw * sin

    qn = q * lax.rsqrt(mean_sq(q, qm_ref) + QK_EPS) * qg_ref[...]
    scale = math.log2(math.e) / math.sqrt(HEAD_DIM)
    for c in range(Q_WIDTH // 128):
        qt_ref[c * 128:(c + 1) * 128, :] = (rope(qn[:, c * 128:(c + 1) * 128]) * scale).T.astype(BF16)
    kn = rope(k * lax.rsqrt(mean_sq(k, km_ref) + QK_EPS) * kg_ref[...])
    low = lane < HEAD_DIM
    k_ref[0] = jnp.where(low, kn, 0.0).astype(BF16)
    k_ref[1] = jnp.where(low, pltpu.roll(kn, HEAD_DIM, 1), 0.0).astype(BF16)
    ones_col = jnp.where(lane == HEAD_DIM, 1.0, 0.0)
    vt_ref[0:128, :] = jnp.where(low, v, ones_col).T.astype(BF16)
    vt_ref[128:256, :] = jnp.where(low, pltpu.roll(v, HEAD_DIM, 1), ones_col).T.astype(BF16)


def _rope_tables(S):
    t = np.arange(S)
    inv = ROPE_THETA ** (-np.arange(ROPE_PAIRS, dtype=np.float64) / ROPE_PAIRS)
    ang_r = (t // GRID_W)[:, None] * inv
    ang_c = (t % GRID_W)[:, None] * inv
    cos = np.concatenate([np.cos(ang_r), np.cos(ang_r), np.cos(ang_c), np.cos(ang_c)], axis=1)
    sin = np.concatenate([-np.sin(ang_r), np.sin(ang_r), -np.sin(ang_c), np.sin(ang_c)], axis=1)
    return (jnp.asarray(np.tile(cos, (1, 2)), F32), jnp.asarray(np.tile(sin, (1, 2)), F32))


def _head_mean_matrix(width):
    m = np.kron(np.eye(width // HEAD_DIM), np.full((HEAD_DIM, HEAD_DIM), 1.0 / HEAD_DIM))
    return jnp.asarray(m, BF16)


def _even_in(x, w_in, q_gain, k_gain, B, S):
    T, D = x.shape
    tm = ROW_TILE
    ns = S // tm
    cos, sin = _rope_tables(S)
    row = lambda w: pl.BlockSpec((tm, w), lambda i: (i, 0))
    full = lambda a: pl.BlockSpec(a.shape, lambda i: (0,) * a.ndim)
    tab = pl.BlockSpec((tm, 128), lambda i: (i % ns, 0))
    w = w_in.astype(BF16)
    qm = _head_mean_matrix(Q_WIDTH)
    km = _head_mean_matrix(KV_WIDTH)
    qg = jnp.tile(q_gain.astype(F32), N_HEADS).reshape(1, Q_WIDTH)
    kg = jnp.tile(k_gain.astype(F32), N_KV_HEADS).reshape(1, KV_WIDTH)
    return pl.pallas_call(
        _even_in_kernel, grid=(T // tm,),
        in_specs=[row(D), full(w), full(qm), full(km), full(qg), full(kg), tab, tab],
        out_specs=[row(F_WIDTH),
                   pl.BlockSpec((None, Q_WIDTH, tm), lambda i: (i // ns, 0, i % ns)),
                   pl.BlockSpec((N_KV_HEADS, tm, 128), lambda i: (0, i, 0)),
                   pl.BlockSpec((None, N_KV_HEADS * 128, tm), lambda i: (i // ns, 0, i % ns))],
        out_shape=[jax.ShapeDtypeStruct((T, F_WIDTH), BF16),
                   jax.ShapeDtypeStruct((B, Q_WIDTH, S), BF16),
                   jax.ShapeDtypeStruct((N_KV_HEADS, T, 128), BF16),
                   jax.ShapeDtypeStruct((B, N_KV_HEADS * 128, S), BF16)],
        compiler_params=_cparams("parallel"), name="even_in")(x, w, qm, km, qg, kg, cos, sin)


def _fourier_kernel(a_ref, dftc_ref, taba_ref, kc_ref, ks_ref, wf_ref, o_ref,
                    zr_ref, zi_ref, ur_ref, ui_ref, y_ref):
    S = a_ref.shape[0]
    n1_count = DFT_N1
    n2_count = S // DFT_N1
    pz = n1_count + DFT_PITCH_PAD
    pu = n2_count + DFT_PITCH_PAD
    blk = DFT_KRON * DFT_N1
    scale = 1.0 / math.sqrt(S * FGROUP_DIM)

    def channel_dft(j, carry):
        zz = _dot(a_ref[pl.ds(pl.multiple_of(j * blk, blk), blk), :], dftc_ref[...])
        for q in range(DFT_KRON):
            dst = pl.ds(pl.multiple_of((j * DFT_KRON + q) * pz, 8), n1_count)
            zr_ref[dst, :] = zz[q * n1_count:(q + 1) * n1_count, :FGROUP_DIM]
            zi_ref[dst, :] = zz[q * n1_count:(q + 1) * n1_count, FGROUP_DIM:]
        return carry

    lax.fori_loop(0, S // blk, channel_dft, 0, unroll=2)

    def stage_a(n1, carry):
        src = pl.ds(n1, n2_count, stride=pz)
        zn = jnp.concatenate([zr_ref[src, :], zi_ref[src, :]], axis=1).astype(BF16)
        r = _dot(taba_ref[n1], zn)
        dst = pl.ds(pl.multiple_of(n1 * pu, 8), n2_count)
        ur_ref[dst, :] = r[:n2_count, :FGROUP_DIM] + r[n2_count:, FGROUP_DIM:]
        ui_ref[dst, :] = r[:n2_count, FGROUP_DIM:] - r[n2_count:, :FGROUP_DIM]
        return carry

    lax.fori_loop(0, n1_count, stage_a, 0, unroll=4)

    def stage_b(j, carry):
        srcs = [pl.ds(j * DFT_KRON + q, n1_count, stride=pu) for q in range(DFT_KRON)]
        ur = jnp.concatenate([ur_ref[s, :] for s in srcs], axis=0).astype(BF16)
        ui = jnp.concatenate([ui_ref[s, :] for s in srcs], axis=0).astype(BF16)
        re = _dot(kc_ref[...], ur) + _dot(ks_ref[...], ui)
        out = _dot((re * scale).astype(BF16), wf_ref[...])
        for q in range(DFT_KRON):
            y_ref[srcs[q], :] = out[q * n1_count:(q + 1) * n1_count]
        return carry

    lax.fori_loop(0, S // blk, stage_b, 0, unroll=4)

    def compact(k1, carry):
        o_ref[pl.ds(pl.multiple_of(k1 * n2_count, n2_count), n2_count), :] = (
            y_ref[pl.ds(pl.multiple_of(k1 * pu, 8), n2_count), :].astype(BF16))
        return carry

    lax.fori_loop(0, n1_count, compact, 0)


def _dft_tables(S):
    n1c, n2c = DFT_N1, S // DFT_N1
    c = np.arange(FGROUP_DIM)
    ang = 2 * np.pi * np.outer(c, c) / FGROUP_DIM
    dftc = np.concatenate([np.cos(ang), -np.sin(ang)], axis=1)
    n1 = np.arange(n1c)[:, None, None]
    k2 = np.arange(n2c)[None, :, None]
    n2 = np.arange(n2c)[None, None, :]
    th = 2 * np.pi * (n2 * k2 / n2c + n1 * k2 / S)
    taba = np.concatenate([np.cos(th), np.sin(th)], axis=1)
    k1 = np.arange(n1c)
    g = 2 * np.pi * np.outer(k1, k1) / n1c
    eye = np.eye(DFT_KRON)
    kc = np.kron(eye, np.cos(g))
    ks = np.kron(eye, np.sin(g))
    return tuple(jnp.asarray(t, BF16) for t in (dftc, taba, kc, ks))


def _fourier(a, w_fourier, B, S):
    T = a.shape[0]
    dftc, taba, kc, ks = _dft_tables(S)
    full = lambda t: pl.BlockSpec(t.shape, lambda b, g: (0,) * t.ndim)
    blk = pl.BlockSpec((S, FGROUP_DIM), lambda b, g: (b, g))
    return pl.pallas_call(
        _fourier_kernel, grid=(B, N_FGROUPS),
        in_specs=[blk, full(dftc), full(taba), full(kc), full(ks),
                  pl.BlockSpec((None, FGROUP_DIM, FGROUP_DIM), lambda b, g: (g, 0, 0))],
        out_specs=blk,
        out_shape=jax.ShapeDtypeStruct((T, F_WIDTH), BF16),
        scratch_shapes=(
            [pltpu.VMEM((S // DFT_N1 * (DFT_N1 + DFT_PITCH_PAD), FGROUP_DIM), F32)] * 2
            + [pltpu.VMEM((DFT_N1 * (S // DFT_N1 + DFT_PITCH_PAD), FGROUP_DIM), F32)] * 3),
        compiler_params=_cparams("parallel", "parallel"), name="fourier")(
            a, dftc, taba, kc, ks, w_fourier.astype(BF16))


def _attn_kernel(qt_ref, k_ref, vt_ref, o_ref, qs_ref, kmax_ref, acc_ref, m_ref, s0_ref, s1_ref,
                 p0_ref, p1_ref):
    tq = qt_ref.shape[1]
    n_keys = k_ref.shape[0]
    tk = min(ATT_TK, n_keys)
    n_chunks = n_keys // tk
    assert n_chunks % 2 == 0 and n_chunks * tk == n_keys

    def keys(c):
        return k_ref[pl.ds(pl.multiple_of(c * tk, tk), tk), :]

    @pl.when(pl.program_id(2) == 0)
    def _():
        def body(c, best):
            k = keys(c).astype(F32)
            return jnp.maximum(best, jnp.sum(k * k, axis=1, keepdims=True))
        best = lax.fori_loop(0, n_chunks, body, jnp.zeros((tk, 1), F32))
        kmax_ref[...] = jnp.broadcast_to(jnp.sqrt(jnp.max(best, axis=0, keepdims=True)), kmax_ref.shape)

    qs_ref[HEAD_DIM:, :] = jnp.zeros((128 - HEAD_DIM, Q_GROUP * tq), BF16)
    for g in range(Q_GROUP):
        qs_ref[:HEAD_DIM, g * tq:(g + 1) * tq] = qt_ref[g * HEAD_DIM:(g + 1) * HEAD_DIM, :]
    qf = qs_ref[...].astype(F32)
    bound = jnp.sqrt(jnp.sum(qf * qf, axis=0, keepdims=True)) * kmax_ref[0:1, 0:1] * ATT_BOUND_SLACK

    def scores(c):
        return _dot(keys(c), qs_ref[...])

    def values(c):
        return vt_ref[:ATT_V_ROWS, pl.ds(pl.multiple_of(c * tk, tk), tk)]

    def weights(s_buf):
        return jnp.exp2(s_buf[...] - bound).astype(BF16)

    def accumulate(p_buf, c):
        acc_ref[...] += _dot(values(c), p_buf[...])

    last = n_chunks - 1

    def fast(c2, carry):
        c = 2 * c2
        s0_ref[...] = scores(jnp.minimum(c + 2, last))
        p1_ref[...] = weights(s1_ref)
        accumulate(p0_ref, c)
        s1_ref[...] = scores(jnp.minimum(c + 3, last))
        p0_ref[...] = weights(s0_ref)
        accumulate(p1_ref, c + 1)
        return carry

    acc_ref[...] = jnp.zeros(acc_ref.shape, F32)
    s0_ref[...] = scores(0)
    p0_ref[...] = weights(s0_ref)
    s1_ref[...] = scores(1)
    lax.fori_loop(0, n_chunks // 2, fast, 0, unroll=True)
    underflow = jnp.min(acc_ref[HEAD_DIM:HEAD_DIM + 1, :]) < ATT_MIN_ROW_SUM

    @pl.when(underflow)
    def _():
        def safe(c, carry):
            s = scores(c)
            m_old = m_ref[...]
            m_new = jnp.maximum(m_old, jnp.max(s, axis=0, keepdims=True))
            acc_ref[...] = (jnp.exp2(m_old - m_new) * acc_ref[...]
                            + _dot(values(c), jnp.exp2(s - m_new).astype(BF16)))
            m_ref[...] = m_new
            return carry

        m_ref[...] = jnp.full(m_ref.shape, NEG_INF, F32)
        acc_ref[...] = jnp.zeros(acc_ref.shape, F32)
        lax.fori_loop(0, n_chunks, safe, 0)

    acc = acc_ref[...]
    ot = acc[:HEAD_DIM, :] / acc[HEAD_DIM:HEAD_DIM + 1, :]
    ot = jnp.concatenate([ot, jnp.zeros((128 - HEAD_DIM, Q_GROUP * tq), F32)], axis=0)
    o = ot.T
    o_ref[...] = jnp.concatenate([o[g * tq:(g + 1) * tq, :HEAD_DIM] for g in range(Q_GROUP)],
                                 axis=1).astype(BF16)


def _attention(qt, k2, vt, B, S):
    T = B * S
    tq = ATT_TQ
    nq = S // tq
    gw = Q_GROUP * HEAD_DIM
    cols = Q_GROUP * tq
    tk = min(ATT_TK, S)
    return pl.pallas_call(
        _attn_kernel, grid=(B, N_KV_HEADS, nq),
        in_specs=[pl.BlockSpec((None, gw, tq), lambda b, h, i: (b, h, i)),
                  pl.BlockSpec((None, None, S, 128), lambda b, h, i: (h, b, 0, 0)),
                  pl.BlockSpec((None, 128, S), lambda b, h, i: (b, h, 0))],
        out_specs=pl.BlockSpec((tq, gw), lambda b, h, i: (b * nq + i, h)),
        out_shape=jax.ShapeDtypeStruct((T, Q_WIDTH), BF16),
        scratch_shapes=[pltpu.VMEM((128, cols), BF16), pltpu.VMEM((8, 128), F32),
                        pltpu.VMEM((ATT_V_ROWS, cols), F32), pltpu.VMEM((1, cols), F32),
                        pltpu.VMEM((tk, cols), F32), pltpu.VMEM((tk, cols), F32),
                        pltpu.VMEM((tk, cols), BF16), pltpu.VMEM((tk, cols), BF16)],
        compiler_params=_cparams("parallel", "parallel", "arbitrary"),
        name="attention")(qt, k2.reshape(N_KV_HEADS, B, S, 128), vt)


def _even_out_kernel(a_ref, t_ref, wa_ref, wt_ref, x_ref, g_ref, b_ref, o_ref):
    mix = _dot(a_ref[...], wa_ref[...]) + _dot(t_ref[...], wt_ref[...])
    o_ref[...] = _ln(DEEPNORM_ALPHA * x_ref[...] + mix, g_ref[...], b_ref[...])


def _even_out(a_out, attn, w_out, x, g, b):
    T, D = x.shape
    tm = ROW_TILE
    row = lambda w: pl.BlockSpec((tm, w), lambda i: (i, 0))
    full = lambda a: pl.BlockSpec(a.shape, lambda i: (0,) * a.ndim)
    wa = w_out[:F_WIDTH].astype(BF16)
    wt = w_out[F_WIDTH:].astype(BF16)
    g = g.reshape(1, D)
    b = b.reshape(1, D)
    return pl.pallas_call(
        _even_out_kernel, grid=(T // tm,),
        in_specs=[row(F_WIDTH), row(Q_WIDTH), full(wa), full(wt), row(D), full(g), full(b)],
        out_specs=row(D), out_shape=jax.ShapeDtypeStruct((T, D), F32),
        compiler_params=_cparams("parallel"), name="even_out")(a_out, attn, wa, wt, x, g, b)


def _odd_kernel(x_ref, wi_ref, bi_ref, vg_ref, vb_ref, ws_ref, bs_ref, wo_ref, g_ref, b_ref, o_ref,
                gate_ref):
    tm = x_ref.shape[0]
    x = x_ref[...]
    h = _dot(x.astype(BF16), wi_ref[...]) + bi_ref[...]
    h = 0.5 * h * (1.0 + lax.erf(h * (1.0 / math.sqrt(2.0))))
    u = h[:, :C_WIDTH]
    v = _ln(h[:, C_WIDTH:], vg_ref[...], vb_ref[...]).astype(BF16)
    for c in range(tm // CHUNK):
        r0 = c * CHUNK
        for gi in range(N_CGROUPS):
            l0 = gi * CGROUP_DIM
            sv = _dot(ws_ref[gi], v[r0:r0 + CHUNK, l0:l0 + CGROUP_DIM]) + bs_ref[gi]
            gate_ref[r0:r0 + CHUNK, l0:l0 + CGROUP_DIM] = (
                u[r0:r0 + CHUNK, l0:l0 + CGROUP_DIM] * sv).astype(BF16)
    mix = _dot(gate_ref[...], wo_ref[...])
    o_ref[...] = _ln(DEEPNORM_ALPHA * x + mix, g_ref[...], b_ref[...])


def _odd_mixer(x, w_in, b_in, v_g, v_b, w_s, b_s, w_out, g, b):
    T, D = x.shape
    tm = ROW_TILE
    row = pl.BlockSpec((tm, D), lambda i: (i, 0))
    full = lambda a: pl.BlockSpec(a.shape, lambda i: (0,) * a.ndim)
    args = [w_in.astype(BF16), b_in.reshape(1, 2 * C_WIDTH), v_g.reshape(1, C_WIDTH),
            v_b.reshape(1, C_WIDTH), w_s.astype(BF16),
            jnp.broadcast_to(b_s[:, :, None], (N_CGROUPS, CHUNK, CGROUP_DIM)).astype(F32),
            w_out.astype(BF16), g.reshape(1, D), b.reshape(1, D)]
    return pl.pallas_call(
        _odd_kernel, grid=(T // tm,),
        in_specs=[row] + [full(a) for a in args],
        out_specs=row, out_shape=jax.ShapeDtypeStruct((T, D), F32),
        scratch_shapes=[pltpu.VMEM((tm, C_WIDTH), BF16)],
        compiler_params=_cparams("parallel"), name="odd_mixer")(x, *args)


def _router_kernel(x_ref, w_ref, rb_ref, tri_ref, eidx_ref, wts_ref, pos_ref, cnt_ref, xp_ref, run_ref):
    tm = x_ref.shape[0]
    i = pl.program_id(0)

    @pl.when(i == 0)
    def _():
        run_ref[...] = jnp.zeros(run_ref.shape, F32)

    x = x_ref[...]
    xp_ref[...] = _pack_halves(x)
    xh = x.astype(BF16)
    xl = (x - xh.astype(F32)).astype(BF16)
    nt = (((1,), (1,)), ((), ()))
    dg = lambda a, c: lax.dot_general(a, c, nt, preferred_element_type=F32)
    logits = dg(w_ref[0], xh) + dg(w_ref[0], xl) + dg(w_ref[1], xh)
    scores = jax.nn.sigmoid(logits)
    sel = scores + rb_ref[...]

    i8 = lax.broadcasted_iota(I32, (GROUP_SIZE, tm), 0)
    gsc_rows = []
    for gidx in range(N_EXPERT_GROUPS):
        sg = sel[gidx * GROUP_SIZE:(gidx + 1) * GROUP_SIZE, :]
        m1 = jnp.max(sg, axis=0, keepdims=True)
        f1 = jnp.min(jnp.where(sg == m1, i8, GROUP_SIZE), axis=0, keepdims=True)
        m2 = jnp.max(jnp.where(i8 == f1, NEG_INF, sg), axis=0, keepdims=True)
        gsc_rows.append(m1 + m2)
    gsc = jnp.concatenate(gsc_rows, axis=0)

    gsel = jnp.zeros(gsc.shape, F32)
    for _ in range(TOPK_GROUPS):
        m = jnp.max(gsc, axis=0, keepdims=True)
        f = jnp.min(jnp.where(gsc == m, i8, N_EXPERT_GROUPS), axis=0, keepdims=True)
        pick = i8 == f
        gsel = jnp.where(pick, 1.0, gsel)
        gsc = jnp.where(pick, NEG_INF, gsc)
    esel = jnp.concatenate(
        [jnp.broadcast_to(gsel[gidx:gidx + 1, :], (GROUP_SIZE, tm)) for gidx in range(N_EXPERT_GROUPS)],
        axis=0)

    cur = jnp.where(esel > 0.0, sel, NEG_INF)
    ei = lax.broadcasted_iota(I32, cur.shape, 0)
    idx_rows, sc_rows = [], []
    chosen = jnp.zeros(cur.shape, F32)
    for _ in range(TOP_K):
        m = jnp.max(cur, axis=0, keepdims=True)
        f = jnp.min(jnp.where(cur == m, ei, N_EXPERTS), axis=0, keepdims=True)
        pick = ei == f
        idx_rows.append(f)
        sc_rows.append(jnp.sum(jnp.where(pick, scores, 0.0), axis=0, keepdims=True))
        chosen = jnp.where(pick, 1.0, chosen)
        cur = jnp.where(pick, NEG_INF, cur)
    eidx = jnp.concatenate(idx_rows, axis=0)
    sc = jnp.concatenate(sc_rows, axis=0)
    eidx_ref[...] = eidx
    wts_ref[...] = sc / jnp.sum(sc, axis=0, keepdims=True) * ROUTE_SCALE

    before = _dot(chosen.astype(BF16), tri_ref[...]) + run_ref[...]
    pos_rows = [jnp.sum(jnp.where(ei == idx_rows[k], before, 0.0), axis=0, keepdims=True)
                for k in range(TOP_K)]
    pos_ref[...] = jnp.concatenate(pos_rows, axis=0).astype(I32)
    run_new = run_ref[...] + jnp.sum(chosen, axis=1, keepdims=True)
    run_ref[...] = run_new
    cnt_ref[...] = jnp.broadcast_to(run_new, cnt_ref.shape).astype(I32)


def _router(x, router_w, router_b, first_row, T):
    D = x.shape[1]
    tm = min(ROUTER_TILE, T)
    n_tiles = T // tm
    first_tile = first_row // tm
    assert n_tiles * tm == T and first_tile * tm == first_row
    wt = router_w.T.astype(F32)
    wh = wt.astype(BF16)
    wl = (wt - wh.astype(F32)).astype(BF16)
    w2 = jnp.stack([wh, wl])
    rb = router_b.astype(F32).reshape(N_EXPERTS, 1)
    tri = jnp.asarray(np.triu(np.ones((tm, tm)), 1), BF16)
    full = lambda a: pl.BlockSpec(a.shape, lambda i: (0,) * a.ndim)
    col = pl.BlockSpec((TOP_K, tm), lambda i: (0, i))
    return pl.pallas_call(
        _router_kernel, grid=(n_tiles,),
        in_specs=[pl.BlockSpec((tm, D), lambda i: (i + first_tile, 0)), full(w2), full(rb), full(tri)],
        out_specs=[col, col, col, pl.BlockSpec((N_EXPERTS, 128), lambda i: (0, 0)),
                   pl.BlockSpec((tm, HALF), lambda i: (i, 0))],
        out_shape=[jax.ShapeDtypeStruct((TOP_K, T), I32), jax.ShapeDtypeStruct((TOP_K, T), F32),
                   jax.ShapeDtypeStruct((TOP_K, T), I32), jax.ShapeDtypeStruct((N_EXPERTS, 128), I32),
                   jax.ShapeDtypeStruct((T, HALF), I32)],
        scratch_shapes=[pltpu.VMEM((N_EXPERTS, 1), F32)],
        compiler_params=_cparams("arbitrary"), name="router")(x, w2, rb, tri)


def _dest_kernel(start_ref, eidx_ref, pos_ref, o_ref):
    e = eidx_ref[...]
    acc = pos_ref[...]
    for j in range(N_EXPERTS):
        acc = acc + jnp.where(e == j, start_ref[j], 0)
    o_ref[...] = acc


def _dest_rows(eidx, pos, seg_start):
    K, T = eidx.shape
    tl = min(T, 2048)
    blk = pl.BlockSpec((K, tl), lambda i, s: (0, i))
    grid_spec = pltpu.PrefetchScalarGridSpec(
        num_scalar_prefetch=1, grid=(T // tl,), in_specs=[blk, blk], out_specs=blk)
    return pl.pallas_call(
        _dest_kernel, grid_spec=grid_spec, out_shape=jax.ShapeDtypeStruct((K, T), I32),
        compiler_params=_cparams("parallel"), name="dest_rows")(seg_start, eidx, pos)


def _gather_rows(table, idx):
    n_rows = idx.shape[0]
    width = table.shape[1]
    per_worker = n_rows // SC_WORKERS
    n_chunks = per_worker // SC_CHUNK
    assert per_worker * SC_WORKERS == n_rows and n_chunks * SC_CHUNK == per_worker
    mesh = plsc.VectorSubcoreMesh(core_axis_name="c", subcore_axis_name="s")

    @functools.partial(
        pl.kernel, mesh=mesh,
        out_type=jax.ShapeDtypeStruct((n_rows, width), table.dtype),
        scratch_types=[pltpu.VMEM((SC_CHUNK,), I32), pltpu.VMEM((SC_CHUNK, width), table.dtype),
                       pltpu.SemaphoreType.DMA])
    def gather(table_hbm, idx_hbm, out_hbm, idx_v, rows_v, sem):
        wid = lax.axis_index("s") * SC_CORES + lax.axis_index("c")
        base = wid * per_worker

        @pl.loop(0, n_chunks)
        def _(j):
            off = base + j * SC_CHUNK
            pltpu.sync_copy(idx_hbm.at[pl.ds(off, SC_CHUNK)], idx_v)
            pltpu.async_copy(table_hbm.at[idx_v], rows_v, sem).wait()
            pltpu.sync_copy(rows_v, out_hbm.at[pl.ds(off, SC_CHUNK)])

    return gather(table, idx)


def _scatter_rows(rows, dest):
    n_tok, width = rows.shape
    n_dst = dest.shape[0]
    per_worker = n_tok // SC_WORKERS
    n_chunks = per_worker // SC_CHUNK
    assert per_worker * SC_WORKERS == n_tok and n_chunks * SC_CHUNK == per_worker
    mesh = plsc.VectorSubcoreMesh(core_axis_name="c", subcore_axis_name="s")

    @functools.partial(
        pl.kernel, mesh=mesh,
        out_type=jax.ShapeDtypeStruct((n_dst * n_tok, width), rows.dtype),
        scratch_types=[pltpu.VMEM((n_dst, SC_CHUNK), I32), pltpu.VMEM((SC_CHUNK, width), rows.dtype),
                       pltpu.SemaphoreType.DMA])
    def scatter(rows_hbm, dest_hbm, out_hbm, idx_v, rows_v, sem):
        wid = lax.axis_index("s") * SC_CORES + lax.axis_index("c")
        base = wid * per_worker

        @pl.loop(0, n_chunks)
        def _(j):
            off = base + j * SC_CHUNK
            pltpu.sync_copy(dest_hbm.at[:, pl.ds(off, SC_CHUNK)], idx_v)
            pltpu.sync_copy(rows_hbm.at[pl.ds(off, SC_CHUNK)], rows_v)
            copies = [pltpu.async_copy(rows_v, out_hbm.at[idx_v.at[k]], sem) for k in range(n_dst)]
            for c in copies:
                c.wait()

    return scatter(rows, dest)


def _expert_kernel(blk_ref, exp_ref, lo_ref, hi_ref, slot_ref, nxt_ref, xs_hbm, wg_hbm, wu_hbm, wd_hbm,
                   ys_hbm, wg_buf, wu_buf, wd_buf, wgu_s, wd_s, sem, xs_buf, xs_sem, ys_buf, ys_sem,
                   *, layer):
    i = pl.program_id(0)
    prev = jnp.maximum(i - 1, 0)

    def weight_copies(expert, slot):
        return [pltpu.make_async_copy(src.at[layer, expert], dst.at[slot], sem.at[slot])
                for src, dst in ((wg_hbm, wg_buf), (wu_hbm, wu_buf), (wd_hbm, wd_buf))]

    @pl.when(i == 0)
    def _():
        for c in weight_copies(exp_ref[0], slot_ref[0]):
            c.start()

    @pl.when(jnp.logical_or(i == 0, exp_ref[i] != exp_ref[prev]))
    def _():
        slot = slot_ref[i]
        for c in weight_copies(exp_ref[i], slot):
            c.wait()
        wgu_s[:, :EXPERT_DIM] = wg_buf[slot].astype(BF16)
        wgu_s[:, EXPERT_DIM:] = wu_buf[slot].astype(BF16)
        wd_s[...] = wd_buf[slot].astype(BF16)

        @pl.when(nxt_ref[i] >= 0)
        def _():
            for c in weight_copies(nxt_ref[i], 1 - slot):
                c.start()

    n_blocks = xs_hbm.shape[0] // EXPERT_ROWS
    blk = blk_ref[i]
    first = jnp.logical_or(i == 0, blk != blk_ref[prev])

    def rows_copy(block):
        slot = block % EXPERT_XS_SLOTS
        src = xs_hbm.at[pl.ds(pl.multiple_of(block * EXPERT_ROWS, EXPERT_ROWS), EXPERT_ROWS)]
        return pltpu.make_async_copy(src, xs_buf.at[slot], xs_sem.at[slot])

    @pl.when(i == 0)
    def _():
        for b0 in range(min(EXPERT_XS_SLOTS - 1, n_blocks)):
            rows_copy(b0).start()

    def result_copy(block):
        slot = block % EXPERT_XS_SLOTS
        dst = ys_hbm.at[pl.ds(pl.multiple_of(block * EXPERT_ROWS, EXPERT_ROWS), EXPERT_ROWS)]
        return pltpu.make_async_copy(ys_buf.at[slot], dst, ys_sem.at[slot])

    @pl.when(first)
    def _():
        rows_copy(blk).wait()

        @pl.when(blk + EXPERT_XS_SLOTS - 1 < n_blocks)
        def _():
            rows_copy(blk + EXPERT_XS_SLOTS - 1).start()

        @pl.when(blk >= EXPERT_XS_SLOTS)
        def _():
            result_copy(blk - EXPERT_XS_SLOTS).wait()

    xs_ref = xs_buf.at[blk % EXPERT_XS_SLOTS]
    ys_ref = ys_buf.at[blk % EXPERT_XS_SLOTS]
    lo = lo_ref[i]
    hi = hi_ref[i]

    def sub_block(r0):
        rows = slice(r0, r0 + EXPERT_SUB_ROWS)
        w = xs_ref[rows, :]
        xlo = lax.bitcast_convert_type(w.astype(jnp.int16), BF16)
        xhi = lax.bitcast_convert_type(lax.shift_right_logical(w, 16).astype(jnp.int16), BF16)
        gu = _dot(xlo, wgu_s[:HALF, :]) + _dot(xhi, wgu_s[HALF:, :])
        g = gu[:, :EXPERT_DIM]
        hb = (g * jax.nn.sigmoid(g) * gu[:, EXPERT_DIM:]).astype(BF16)
        y = _dot(hb, wd_s[...])
        packed = pltpu.pack_elementwise([y[:, :HALF], y[:, HALF:]], packed_dtype=BF16)
        row = r0 + lax.broadcasted_iota(I32, (EXPERT_SUB_ROWS, 1), 0)
        mine = jnp.logical_and(row >= lo, row < hi)
        kept = jnp.where(first, 0, ys_ref[rows, :])
        ys_ref[rows, :] = jnp.where(mine, packed, kept)

    def touched(r0):
        return jnp.logical_and(lo < r0 + EXPERT_SUB_ROWS, hi > r0)

    for r0 in range(0, EXPERT_ROWS, 2 * EXPERT_SUB_ROWS):
        r1 = r0 + EXPERT_SUB_ROWS
        t0, t1 = touched(r0), touched(r1)

        @pl.when(jnp.logical_and(t0, t1))
        def _():
            sub_block(r0)
            sub_block(r1)

        pl.when(jnp.logical_and(t0, jnp.logical_not(t1)))(functools.partial(sub_block, r0))
        pl.when(jnp.logical_and(jnp.logical_not(t0), t1))(functools.partial(sub_block, r1))

    n_items = pl.num_programs(0)
    final = i == n_items - 1
    block_done = jnp.logical_or(final, blk_ref[jnp.minimum(i + 1, n_items - 1)] != blk)

    @pl.when(block_done)
    def _():
        result_copy(blk).start()

    @pl.when(final)
    def _():
        for b0 in range(max(n_blocks - EXPERT_XS_SLOTS, 0), n_blocks):
            result_copy(b0).wait()


def _expert_items(counts, n_rows):
    n_blocks = n_rows // EXPERT_ROWS
    n_items = n_blocks + N_EXPERTS - 1
    end = jnp.cumsum(counts)
    start = end - counts
    first_blk = start // EXPERT_ROWS
    n_blk = jnp.where(counts > 0, (end - 1) // EXPERT_ROWS - first_blk + 1, 0)
    item_end = jnp.cumsum(n_blk)
    item_start = item_end - n_blk
    slot = jnp.arange(n_items, dtype=I32)
    e = jnp.minimum(jnp.sum((item_end[None, :] <= slot[:, None]).astype(I32), axis=1), N_EXPERTS - 1)
    onehot = (e[:, None] == jnp.arange(N_EXPERTS, dtype=I32)[None, :]).astype(I32)
    pick = lambda v: jnp.sum(onehot * v[None, :], axis=1)
    valid = slot < item_end[-1]
    blk = jnp.where(valid, pick(first_blk) + slot - pick(item_start), n_blocks - 1)
    lo = jnp.clip(pick(start) - blk * EXPERT_ROWS, 0, EXPERT_ROWS)
    hi = jnp.clip(pick(end) - blk * EXPERT_ROWS, 0, EXPERT_ROWS)
    last_e = jnp.max(jnp.where(counts > 0, jnp.arange(N_EXPERTS, dtype=I32), 0))
    e = jnp.where(valid, e, last_e)
    hi = jnp.where(valid, hi, 0)
    lo = jnp.where(valid, lo, 0)
    change = jnp.concatenate([jnp.ones((1,), I32), (e[1:] != e[:-1]).astype(I32)])
    slot = (jnp.cumsum(change) - 1) % 2
    later = jnp.where(jnp.arange(N_EXPERTS, dtype=I32)[None, :] > e[:, None], counts[None, :] > 0, False)
    nxt = jnp.where(jnp.any(later, axis=1), jnp.argmax(later, axis=1), -1)
    return tuple(a.astype(I32) for a in (blk, e, lo, hi, slot, nxt))


def _experts(xs, items, w_gate, w_up, w_down, layer):
    n_rows = xs.shape[0]
    n_items = items[0].shape[0]
    hbm = pl.BlockSpec(memory_space=pl.ANY)
    grid_spec = pltpu.PrefetchScalarGridSpec(
        num_scalar_prefetch=len(items), grid=(n_items,),
        in_specs=[hbm, hbm, hbm, hbm],
        out_specs=hbm,
        scratch_shapes=[pltpu.VMEM((2, D_MODEL, EXPERT_DIM), F32),
                        pltpu.VMEM((2, D_MODEL, EXPERT_DIM), F32),
                        pltpu.VMEM((2, EXPERT_DIM, D_MODEL), F32),
                        pltpu.VMEM((D_MODEL, 2 * EXPERT_DIM), BF16),
                        pltpu.VMEM((EXPERT_DIM, D_MODEL), BF16),
                        pltpu.SemaphoreType.DMA((2,)),
                        pltpu.VMEM((EXPERT_XS_SLOTS, EXPERT_ROWS, HALF), I32),
                        pltpu.SemaphoreType.DMA((EXPERT_XS_SLOTS,)),
                        pltpu.VMEM((EXPERT_XS_SLOTS, EXPERT_ROWS, HALF), I32),
                        pltpu.SemaphoreType.DMA((EXPERT_XS_SLOTS,))])
    return pl.pallas_call(
        functools.partial(_expert_kernel, layer=layer), grid_spec=grid_spec,
        out_shape=jax.ShapeDtypeStruct((n_rows, HALF), I32),
        compiler_params=_cparams("arbitrary"), name="experts")(*items, xs, w_gate, w_up, w_down)


def _moe_out_kernel(x_ref, yg_ref, wt_ref, sgu_ref, sd_ref, g_ref, b_ref, *rest):
    o_ref = rest[-1]
    x = x_ref[...]
    wt = wt_ref[...]
    lo = jnp.zeros((x.shape[0], HALF), F32)
    hi = jnp.zeros((x.shape[0], HALF), F32)
    for k in range(TOP_K):
        w = yg_ref[k]
        wk = wt[:, k:k + 1]
        lo = lo + wk * _unpack_lo(w)
        hi = hi + wk * _unpack_hi(w)
    gu = _dot(x.astype(BF16), sgu_ref[...])
    g = gu[:, :EXPERT_DIM]
    hs = (g * jax.nn.sigmoid(g) * gu[:, EXPERT_DIM:]).astype(BF16)
    ffn = jnp.concatenate([lo, hi], axis=1) + _dot(hs, sd_ref[...])
    o_ref[...] = _ln(DEEPNORM_ALPHA * x + ffn, g_ref[...], b_ref[...])


def _moe_out(x, yg, wts, sh_gate, sh_up, sh_down, g, b, first_tile, partial_out):
    T, D = x.shape
    tm = ROW_TILE
    n_tiles = yg.shape[1] // tm
    xrow = pl.BlockSpec((tm, D), lambda i: (i + first_tile, 0))
    full = lambda a: pl.BlockSpec(a.shape, lambda i: (0,) * a.ndim)
    sgu = jnp.concatenate([sh_gate, sh_up], axis=1).astype(BF16)
    sd = sh_down.astype(BF16)
    g = g.reshape(1, D)
    b = b.reshape(1, D)
    args = [x, yg, wts, sgu, sd, g, b]
    in_specs = [xrow, pl.BlockSpec((TOP_K, tm, HALF), lambda i: (0, i, 0)),
                pl.BlockSpec((tm, TOP_K), lambda i: (i, 0)), full(sgu), full(sd), full(g), full(b)]
    aliases = {}
    if partial_out is not None:
        args.append(partial_out)
        in_specs.append(pl.BlockSpec(memory_space=pl.ANY))
        aliases = {len(args) - 1: 0}
    return pl.pallas_call(
        _moe_out_kernel, grid=(n_tiles,), in_specs=in_specs,
        out_specs=xrow, out_shape=jax.ShapeDtypeStruct((T, D), F32),
        input_output_aliases=aliases,
        compiler_params=_cparams("parallel"), name="moe_out")(*args)


def _moe(x, router_w, router_b, w_gate, w_up, w_down, layer, sh_gate, sh_up, sh_down, g, b):
    T = x.shape[0]
    tiles = T // ROW_TILE // MOE_TOKEN_GROUPS
    tg = tiles * ROW_TILE
    out = None
    for grp in range(MOE_TOKEN_GROUPS):
        eidx, wts, pos, cnt, xp = _router(x, router_w, router_b, grp * tg, tg)
        counts = cnt[:, 0]
        seg_start = (jnp.cumsum(counts) - counts).astype(I32)
        dest = _dest_rows(eidx, pos, seg_start)
        xs = _scatter_rows(xp, dest)
        ys = _experts(xs, _expert_items(counts, tg * TOP_K), w_gate, w_up, w_down, layer)
        yg = _gather_rows(ys, dest.reshape(tg * TOP_K)).reshape(TOP_K, tg, HALF)
        out = _moe_out(x, yg, wts.T, sh_gate, sh_up, sh_down, g, b, grp * tiles, out)
    return out


def kernel(x, ln_in_g, ln_in_b, e_w_in, e_w_fourier, e_q_gain, e_k_gain, e_w_out, o_w_in, o_b_in, o_v_ln_g, o_v_ln_b, o_w_spatial, o_b_spatial, o_w_out, ln_mix_g, ln_mix_b, ln_ffn_g, ln_ffn_b, router_w, router_b, exp_w_gate, exp_w_up, exp_w_down, sh_w_gate, sh_w_up, sh_w_down):
    B, S, D = x.shape
    T = B * S
    h = _layer_norm(x.reshape(T, D), ln_in_g, ln_in_b)
    for i in range(DEPTH):
        j = i // 2
        if i % 2 == 0:
            a, qt, k2, vt = _even_in(h, e_w_in[j], e_q_gain[j], e_k_gain[j], B, S)
            a_out = _fourier(a, e_w_fourier[j], B, S)
            attn = _attention(qt, k2, vt, B, S)
            h = _even_out(a_out, attn, e_w_out[j], h, ln_mix_g[i], ln_mix_b[i])
        else:
            h = _odd_mixer(h, o_w_in[j], o_b_in[j], o_v_ln_g[j], o_v_ln_b[j], o_w_spatial[j],
                           o_b_spatial[j], o_w_out[j], ln_mix_g[i], ln_mix_b[i])
        h = _moe(h, router_w[i], router_b[i], exp_w_gate, exp_w_up, exp_w_down, i,
                 sh_w_gate[i], sh_w_up[i], sh_w_down[i], ln_ffn_g[i], ln_ffn_b[i])
    return h.reshape(B, S, D)
```

```python
import functools
import math

import numpy as np
import jax
import jax.numpy as jnp
from jax import lax
from jax.experimental import pallas as pl
from jax.experimental.pallas import tpu as pltpu
from jax.experimental.pallas import tpu_sc as plsc

F32 = jnp.float32
BF16 = jnp.bfloat16
I32 = jnp.int32

D_MODEL = 1024
DEPTH = 4
GRID_W = 64
N_FGROUPS = 4
FGROUP_DIM = 128
F_WIDTH = N_FGROUPS * FGROUP_DIM
N_HEADS = 8
N_KV_HEADS = 2
HEAD_DIM = 64
Q_GROUP = N_HEADS // N_KV_HEADS
Q_WIDTH = N_HEADS * HEAD_DIM
KV_WIDTH = N_KV_HEADS * HEAD_DIM
ROPE_THETA = 10000.0
ROPE_PAIRS = HEAD_DIM // 4
EVEN_IN_WIDTH = F_WIDTH + Q_WIDTH + 2 * KV_WIDTH
CHUNK = 128
N_CGROUPS = 8
CGROUP_DIM = D_MODEL // N_CGROUPS
C_WIDTH = N_CGROUPS * CGROUP_DIM
N_EXPERTS = 64
EXPERT_DIM = 256
TOP_K = 8
N_EXPERT_GROUPS = 8
GROUP_SIZE = N_EXPERTS // N_EXPERT_GROUPS
TOPK_GROUPS = 4
ROUTE_SCALE = 2.5
LN_EPS = 1e-5
QK_EPS = 1e-6
DEEPNORM_ALPHA = (2 * DEPTH) ** 0.25

VMEM_LIMIT_BYTES = 56 * 1024 * 1024
ROW_TILE = 512
DFT_N1 = 64
DFT_KRON = 4
DFT_PITCH_PAD = 8
ROUTER_TILE = 1024
EXPERT_ROWS = 2048
EXPERT_XS_SLOTS = 3
EXPERT_SUB_ROWS = 512
HALF = D_MODEL // 2
SC_CORES = 2
SC_SUBCORES = 16
SC_WORKERS = SC_CORES * SC_SUBCORES
SC_CHUNK = 128
MOE_TOKEN_GROUPS = 2
ATT_TQ = 256
ATT_TK = 256
ATT_V_ROWS = 80
ATT_BOUND_SLACK = 1.0 + 2.0 ** -7
ATT_MIN_ROW_SUM = 2.0 ** -80
NEG_INF = float("-inf")


def _cparams(*sem):
    return pltpu.CompilerParams(dimension_semantics=sem, vmem_limit_bytes=VMEM_LIMIT_BYTES)


def _ln(x, g, b):
    mu = jnp.mean(x, axis=-1, keepdims=True)
    xc = x - mu
    var = jnp.mean(xc * xc, axis=-1, keepdims=True)
    return xc * lax.rsqrt(var + LN_EPS) * g + b


def _dot(a, b):
    return jnp.dot(a, b, preferred_element_type=F32)


def _pack_halves(y):
    lo = lax.bitcast_convert_type(y[:, :HALF].astype(BF16).astype(F32), I32)
    hi = lax.bitcast_convert_type(y[:, HALF:].astype(BF16).astype(F32), I32)
    return lax.shift_right_logical(lo, 16) | (hi & jnp.int32(-65536))


def _unpack_lo(w):
    return lax.bitcast_convert_type(lax.shift_left(w, 16), F32)


def _unpack_hi(w):
    return lax.bitcast_convert_type(w & jnp.int32(-65536), F32)


def _ln_kernel(x_ref, g_ref, b_ref, o_ref):
    o_ref[...] = _ln(x_ref[...], g_ref[...], b_ref[...])


def _layer_norm(x, g, b):
    T, D = x.shape
    row = pl.BlockSpec((ROW_TILE, D), lambda i: (i, 0))
    vec = pl.BlockSpec((1, D), lambda i: (0, 0))
    return pl.pallas_call(
        _ln_kernel, grid=(T // ROW_TILE,), in_specs=[row, vec, vec], out_specs=row,
        out_shape=jax.ShapeDtypeStruct((T, D), F32), compiler_params=_cparams("parallel"),
        name="ln_in")(x, g.reshape(1, D), b.reshape(1, D))


def _even_in_kernel(x_ref, w_ref, qm_ref, km_ref, qg_ref, kg_ref, cos_ref, sin_ref,
                    a_ref, qt_ref, k_ref, vt_ref):
    tm = x_ref.shape[0]
    h = _dot(x_ref[...].astype(BF16), w_ref[...])
    a_ref[...] = h[:, :F_WIDTH].astype(BF16)
    q = h[:, F_WIDTH:F_WIDTH + Q_WIDTH]
    k = h[:, F_WIDTH + Q_WIDTH:F_WIDTH + Q_WIDTH + KV_WIDTH]
    v = h[:, F_WIDTH + Q_WIDTH + KV_WIDTH:]
    cos = cos_ref[...]
    sin = sin_ref[...]
    lane = lax.broadcasted_iota(I32, (tm, 128), 1)
    first_of_pair = (lane & ROPE_PAIRS) == 0

    def mean_sq(xf, m_ref):
        sq = xf * xf
        hi = sq.astype(BF16)
        lo = (sq - hi.astype(F32)).astype(BF16)
        return _dot(hi, m_ref[...]) + _dot(lo, m_ref[...])

    def rope(xn):
        sw = jnp.where(first_of_pair, pltpu.roll(xn, 128 - ROPE_PAIRS, 1), pltpu.roll(xn, ROPE_PAIRS, 1))
        return xn * cos + sw * sin

    qn = q * lax.rsqrt(mean_sq(q, qm_ref) + QK_EPS) * qg_ref[...]
    scale = math.log2(math.e) / math.sqrt(HEAD_DIM)
    for c in range(Q_WIDTH // 128):
        qt_ref[c * 128:(c + 1) * 128, :] = (rope(qn[:, c * 128:(c + 1) * 128]) * scale).T.astype(BF16)
    kn = rope(k * lax.rsqrt(mean_sq(k, km_ref) + QK_EPS) * kg_ref[...])
    low = lane < HEAD_DIM
    k_ref[0] = jnp.where(low, kn, 0.0).astype(BF16)
    k_ref[1] = jnp.where(low, pltpu.roll(kn, HEAD_DIM, 1), 0.0).astype(BF16)
    ones_col = jnp.where(lane == HEAD_DIM, 1.0, 0.0)
    vt_ref[0:128, :] = jnp.where(low, v, ones_col).T.astype(BF16)
    vt_ref[128:256, :] = jnp.where(low, pltpu.roll(v, HEAD_DIM, 1), ones_col).T.astype(BF16)


def _rope_tables(S):
    t = np.arange(S)
    inv = ROPE_THETA ** (-np.arange(ROPE_PAIRS, dtype=np.float64) / ROPE_PAIRS)
    ang_r = (t // GRID_W)[:, None] * inv
    ang_c = (t % GRID_W)[:, None] * inv
    cos = np.concatenate([np.cos(ang_r), np.cos(ang_r), np.cos(ang_c), np.cos(ang_c)], axis=1)
    sin = np.concatenate([-np.sin(ang_r), np.sin(ang_r), -np.sin(ang_c), np.sin(ang_c)], axis=1)
    return (jnp.asarray(np.tile(cos, (1, 2)), F32), jnp.asarray(np.tile(sin, (1, 2)), F32))


def _head_mean_matrix(width):
    m = np.kron(np.eye(width // HEAD_DIM), np.full((HEAD_DIM, HEAD_DIM), 1.0 / HEAD_DIM))
    return jnp.asarray(m, BF16)


def _even_in(x, w_in, q_gain, k_gain, B, S):
    T, D = x.shape
    tm = ROW_TILE
    ns = S // tm
    cos, sin = _rope_tables(S)
    row = lambda w: pl.BlockSpec((tm, w), lambda i: (i, 0))
    full = lambda a: pl.BlockSpec(a.shape, lambda i: (0,) * a.ndim)
    tab = pl.BlockSpec((tm, 128), lambda i: (i % ns, 0))
    w = w_in.astype(BF16)
    qm = _head_mean_matrix(Q_WIDTH)
    km = _head_mean_matrix(KV_WIDTH)
    qg = jnp.tile(q_gain.astype(F32), N_HEADS).reshape(1, Q_WIDTH)
    kg = jnp.tile(k_gain.astype(F32), N_KV_HEADS).reshape(1, KV_WIDTH)
    return pl.pallas_call(
        _even_in_kernel, grid=(T // tm,),
        in_specs=[row(D), full(w), full(qm), full(km), full(qg), full(kg), tab, tab],
        out_specs=[row(F_WIDTH),
                   pl.BlockSpec((None, Q_WIDTH, tm), lambda i: (i // ns, 0, i % ns)),
                   pl.BlockSpec((N_KV_HEADS, tm, 128), lambda i: (0, i, 0)),
                   pl.BlockSpec((None, N_KV_HEADS * 128, tm), lambda i: (i // ns, 0, i % ns))],
        out_shape=[jax.ShapeDtypeStruct((T, F_WIDTH), BF16),
                   jax.ShapeDtypeStruct((B, Q_WIDTH, S), BF16),
                   jax.ShapeDtypeStruct((N_KV_HEADS, T, 128), BF16),
                   jax.ShapeDtypeStruct((B, N_KV_HEADS * 128, S), BF16)],
        compiler_params=_cparams("parallel"), name="even_in")(x, w, qm, km, qg, kg, cos, sin)


def _fourier_kernel(a_ref, dftc_ref, taba_ref, kc_ref, ks_ref, wf_ref, o_ref,
                    zr_ref, zi_ref, ur_ref, ui_ref, y_ref):
    S = a_ref.shape[0]
    n1_count = DFT_N1
    n2_count = S // DFT_N1
    pz = n1_count + DFT_PITCH_PAD
    pu = n2_count + DFT_PITCH_PAD
    blk = DFT_KRON * DFT_N1
    scale = 1.0 / math.sqrt(S * FGROUP_DIM)

    def channel_dft(j, carry):
        zz = _dot(a_ref[pl.ds(pl.multiple_of(j * blk, blk), blk), :], dftc_ref[...])
        for q in range(DFT_KRON):
            dst = pl.ds(pl.multiple_of((j * DFT_KRON + q) * pz, 8), n1_count)
            zr_ref[dst, :] = zz[q * n1_count:(q + 1) * n1_count, :FGROUP_DIM]
            zi_ref[dst, :] = zz[q * n1_count:(q + 1) * n1_count, FGROUP_DIM:]
        return carry

    lax.fori_loop(0, S // blk, channel_dft, 0, unroll=4)

    def stage_a(n1, carry):
        src = pl.ds(n1, n2_count, stride=pz)
        zn = jnp.concatenate([zr_ref[src, :], zi_ref[src, :]], axis=1).astype(BF16)
        r = _dot(taba_ref[n1], zn)
        dst = pl.ds(pl.multiple_of(n1 * pu, 8), n2_count)
        ur_ref[dst, :] = r[:n2_count, :FGROUP_DIM] + r[n2_count:, FGROUP_DIM:]
        ui_ref[dst, :] = r[:n2_count, FGROUP_DIM:] - r[n2_count:, :FGROUP_DIM]
        return carry

    lax.fori_loop(0, n1_count, stage_a, 0, unroll=8)

    def stage_b(j, carry):
        srcs = [pl.ds(j * DFT_KRON + q, n1_count, stride=pu) for q in range(DFT_KRON)]
        ur = jnp.concatenate([ur_ref[s, :] for s in srcs], axis=0).astype(BF16)
        ui = jnp.concatenate([ui_ref[s, :] for s in srcs], axis=0).astype(BF16)
        re = _dot(kc_ref[...], ur) + _dot(ks_ref[...], ui)
        out = _dot((re * scale).astype(BF16), wf_ref[...])
        for q in range(DFT_KRON):
            y_ref[srcs[q], :] = out[q * n1_count:(q + 1) * n1_count]
        return carry

    lax.fori_loop(0, S // blk, stage_b, 0, unroll=8)

    def compact(k1, carry):
        o_ref[pl.ds(pl.multiple_of(k1 * n2_count, n2_count), n2_count), :] = (
            y_ref[pl.ds(pl.multiple_of(k1 * pu, 8), n2_count), :].astype(BF16))
        return carry

    lax.fori_loop(0, n1_count, compact, 0)


def _dft_tables(S):
    n1c, n2c = DFT_N1, S // DFT_N1
    c = np.arange(FGROUP_DIM)
    ang = 2 * np.pi * np.outer(c, c) / FGROUP_DIM
    dftc = np.concatenate([np.cos(ang), -np.sin(ang)], axis=1)
    n1 = np.arange(n1c)[:, None, None]
    k2 = np.arange(n2c)[None, :, None]
    n2 = np.arange(n2c)[None, None, :]
    th = 2 * np.pi * (n2 * k2 / n2c + n1 * k2 / S)
    taba = np.concatenate([np.cos(th), np.sin(th)], axis=1)
    k1 = np.arange(n1c)
    g = 2 * np.pi * np.outer(k1, k1) / n1c
    eye = np.eye(DFT_KRON)
    kc = np.kron(eye, np.cos(g))
    ks = np.kron(eye, np.sin(g))
    return tuple(jnp.asarray(t, BF16) for t in (dftc, taba, kc, ks))


def _fourier(a, w_fourier, B, S):
    T = a.shape[0]
    dftc, taba, kc, ks = _dft_tables(S)
    full = lambda t: pl.BlockSpec(t.shape, lambda b, g: (0,) * t.ndim)
    blk = pl.BlockSpec((S, FGROUP_DIM), lambda b, g: (b, g))
    return pl.pallas_call(
        _fourier_kernel, grid=(B, N_FGROUPS),
        in_specs=[blk, full(dftc), full(taba), full(kc), full(ks),
                  pl.BlockSpec((None, FGROUP_DIM, FGROUP_DIM), lambda b, g: (g, 0, 0))],
        out_specs=blk,
        out_shape=jax.ShapeDtypeStruct((T, F_WIDTH), BF16),
        scratch_shapes=(
            [pltpu.VMEM((S // DFT_N1 * (DFT_N1 + DFT_PITCH_PAD), FGROUP_DIM), F32)] * 2
            + [pltpu.VMEM((DFT_N1 * (S // DFT_N1 + DFT_PITCH_PAD), FGROUP_DIM), F32)] * 3),
        compiler_params=_cparams("parallel", "parallel"), name="fourier")(
            a, dftc, taba, kc, ks, w_fourier.astype(BF16))


def _attn_kernel(qt_ref, k_ref, vt_ref, o_ref, qs_ref, kmax_ref, acc_ref, m_ref, s0_ref, s1_ref,
                 p0_ref, p1_ref):
    tq = qt_ref.shape[1]
    n_keys = k_ref.shape[0]
    tk = min(ATT_TK, n_keys)
    n_chunks = n_keys // tk
    assert n_chunks % 2 == 0 and n_chunks * tk == n_keys

    def keys(c):
        return k_ref[pl.ds(pl.multiple_of(c * tk, tk), tk), :]

    @pl.when(pl.program_id(2) == 0)
    def _():
        def body(c, best):
            k = keys(c).astype(F32)
            return jnp.maximum(best, jnp.sum(k * k, axis=1, keepdims=True))
        best = lax.fori_loop(0, n_chunks, body, jnp.zeros((tk, 1), F32))
        kmax_ref[...] = jnp.broadcast_to(jnp.sqrt(jnp.max(best, axis=0, keepdims=True)), kmax_ref.shape)

    qs_ref[HEAD_DIM:, :] = jnp.zeros((128 - HEAD_DIM, Q_GROUP * tq), BF16)
    for g in range(Q_GROUP):
        qs_ref[:HEAD_DIM, g * tq:(g + 1) * tq] = qt_ref[g * HEAD_DIM:(g + 1) * HEAD_DIM, :]
    qf = qs_ref[...].astype(F32)
    bound = jnp.sqrt(jnp.sum(qf * qf, axis=0, keepdims=True)) * kmax_ref[0:1, 0:1] * ATT_BOUND_SLACK

    def scores(c):
        return _dot(keys(c), qs_ref[...])

    def values(c):
        return vt_ref[:ATT_V_ROWS, pl.ds(pl.multiple_of(c * tk, tk), tk)]

    def weights(s_buf):
        return jnp.exp2(s_buf[...] - bound).astype(BF16)

    def accumulate(p_buf, c):
        acc_ref[...] += _dot(values(c), p_buf[...])

    last = n_chunks - 1

    def fast(c2, carry):
        c = 2 * c2
        s0_ref[...] = scores(jnp.minimum(c + 2, last))
        p1_ref[...] = weights(s1_ref)
        accumulate(p0_ref, c)
        s1_ref[...] = scores(jnp.minimum(c + 3, last))
        p0_ref[...] = weights(s0_ref)
        accumulate(p1_ref, c + 1)
        return carry

    acc_ref[...] = jnp.zeros(acc_ref.shape, F32)
    s0_ref[...] = scores(0)
    p0_ref[...] = weights(s0_ref)
    s1_ref[...] = scores(1)
    lax.fori_loop(0, n_chunks // 2, fast, 0, unroll=True)
    underflow = jnp.min(acc_ref[HEAD_DIM:HEAD_DIM + 1, :]) < ATT_MIN_ROW_SUM

    @pl.when(underflow)
    def _():
        def safe(c, carry):
            s = scores(c)
            m_old = m_ref[...]
            m_new = jnp.maximum(m_old, jnp.max(s, axis=0, keepdims=True))
            acc_ref[...] = (jnp.exp2(m_old - m_new) * acc_ref[...]
                            + _dot(values(c), jnp.exp2(s - m_new).astype(BF16)))
            m_ref[...] = m_new
            return carry

        m_ref[...] = jnp.full(m_ref.shape, NEG_INF, F32)
        acc_ref[...] = jnp.zeros(acc_ref.shape, F32)
        lax.fori_loop(0, n_chunks, safe, 0)

    acc = acc_ref[...]
    ot = acc[:HEAD_DIM, :] / acc[HEAD_DIM:HEAD_DIM + 1, :]
    ot = jnp.concatenate([ot, jnp.zeros((128 - HEAD_DIM, Q_GROUP * tq), F32)], axis=0)
    o = ot.T
    o_ref[...] = jnp.concatenate([o[g * tq:(g + 1) * tq, :HEAD_DIM] for g in range(Q_GROUP)],
                                 axis=1).astype(BF16)


def _attention(qt, k2, vt, B, S):
    T = B * S
    tq = ATT_TQ
    nq = S // tq
    gw = Q_GROUP * HEAD_DIM
    cols = Q_GROUP * tq
    tk = min(ATT_TK, S)
    return pl.pallas_call(
        _attn_kernel, grid=(B, N_KV_HEADS, nq),
        in_specs=[pl.BlockSpec((None, gw, tq), lambda b, h, i: (b, h, i)),
                  pl.BlockSpec((None, None, S, 128), lambda b, h, i: (h, b, 0, 0)),
                  pl.BlockSpec((None, 128, S), lambda b, h, i: (b, h, 0))],
        out_specs=pl.BlockSpec((tq, gw), lambda b, h, i: (b * nq + i, h)),
        out_shape=jax.ShapeDtypeStruct((T, Q_WIDTH), BF16),
        scratch_shapes=[pltpu.VMEM((128, cols), BF16), pltpu.VMEM((8, 128), F32),
                        pltpu.VMEM((ATT_V_ROWS, cols), F32), pltpu.VMEM((1, cols), F32),
                        pltpu.VMEM((tk, cols), F32), pltpu.VMEM((tk, cols), F32),
                        pltpu.VMEM((tk, cols), BF16), pltpu.VMEM((tk, cols), BF16)],
        compiler_params=_cparams("parallel", "parallel", "arbitrary"),
        name="attention")(qt, k2.reshape(N_KV_HEADS, B, S, 128), vt)


def _even_out_kernel(a_ref, t_ref, wa_ref, wt_ref, x_ref, g_ref, b_ref, o_ref):
    mix = _dot(a_ref[...], wa_ref[...]) + _dot(t_ref[...], wt_ref[...])
    o_ref[...] = _ln(DEEPNORM_ALPHA * x_ref[...] + mix, g_ref[...], b_ref[...])


def _even_out(a_out, attn, w_out, x, g, b):
    T, D = x.shape
    tm = ROW_TILE
    row = lambda w: pl.BlockSpec((tm, w), lambda i: (i, 0))
    full = lambda a: pl.BlockSpec(a.shape, lambda i: (0,) * a.ndim)
    wa = w_out[:F_WIDTH].astype(BF16)
    wt = w_out[F_WIDTH:].astype(BF16)
    g = g.reshape(1, D)
    b = b.reshape(1, D)
    return pl.pallas_call(
        _even_out_kernel, grid=(T // tm,),
        in_specs=[row(F_WIDTH), row(Q_WIDTH), full(wa), full(wt), row(D), full(g), full(b)],
        out_specs=row(D), out_shape=jax.ShapeDtypeStruct((T, D), F32),
        compiler_params=_cparams("parallel"), name="even_out")(a_out, attn, wa, wt, x, g, b)


def _odd_kernel(x_ref, wi_ref, bi_ref, vg_ref, vb_ref, ws_ref, bs_ref, wo_ref, g_ref, b_ref, o_ref,
                gate_ref):
    tm = x_ref.shape[0]
    x = x_ref[...]
    h = _dot(x.astype(BF16), wi_ref[...]) + bi_ref[...]
    h = 0.5 * h * (1.0 + lax.erf(h * (1.0 / math.sqrt(2.0))))
    u = h[:, :C_WIDTH]
    v = _ln(h[:, C_WIDTH:], vg_ref[...], vb_ref[...]).astype(BF16)
    for c in range(tm // CHUNK):
        r0 = c * CHUNK
        for gi in range(N_CGROUPS):
            l0 = gi * CGROUP_DIM
            sv = _dot(ws_ref[gi], v[r0:r0 + CHUNK, l0:l0 + CGROUP_DIM]) + bs_ref[gi]
            gate_ref[r0:r0 + CHUNK, l0:l0 + CGROUP_DIM] = (
                u[r0:r0 + CHUNK, l0:l0 + CGROUP_DIM] * sv).astype(BF16)
    mix = _dot(gate_ref[...], wo_ref[...])
    o_ref[...] = _ln(DEEPNORM_ALPHA * x + mix, g_ref[...], b_ref[...])


def _odd_mixer(x, w_in, b_in, v_g, v_b, w_s, b_s, w_out, g, b):
    T, D = x.shape
    tm = ROW_TILE
    row = pl.BlockSpec((tm, D), lambda i: (i, 0))
    full = lambda a: pl.BlockSpec(a.shape, lambda i: (0,) * a.ndim)
    args = [w_in.astype(BF16), b_in.reshape(1, 2 * C_WIDTH), v_g.reshape(1, C_WIDTH),
            v_b.reshape(1, C_WIDTH), w_s.astype(BF16),
            jnp.broadcast_to(b_s[:, :, None], (N_CGROUPS, CHUNK, CGROUP_DIM)).astype(F32),
            w_out.astype(BF16), g.reshape(1, D), b.reshape(1, D)]
    return pl.pallas_call(
        _odd_kernel, grid=(T // tm,),
        in_specs=[row] + [full(a) for a in args],
        out_specs=row, out_shape=jax.ShapeDtypeStruct((T, D), F32),
        scratch_shapes=[pltpu.VMEM((tm, C_WIDTH), BF16)],
        compiler_params=_cparams("parallel"), name="odd_mixer")(x, *args)


def _router_kernel(x_ref, w_ref, rb_ref, tri_ref, eidx_ref, wts_ref, pos_ref, cnt_ref, xp_ref, run_ref):
    tm = x_ref.shape[0]
    i = pl.program_id(0)

    @pl.when(i == 0)
    def _():
        run_ref[...] = jnp.zeros(run_ref.shape, F32)

    x = x_ref[...]
    xp_ref[...] = _pack_halves(x)
    xh = x.astype(BF16)
    xl = (x - xh.astype(F32)).astype(BF16)
    nt = (((1,), (1,)), ((), ()))
    dg = lambda a, c: lax.dot_general(a, c, nt, preferred_element_type=F32)
    logits = dg(w_ref[0], xh) + dg(w_ref[0], xl) + dg(w_ref[1], xh)
    scores = jax.nn.sigmoid(logits)
    sel = scores + rb_ref[...]

    i8 = lax.broadcasted_iota(I32, (GROUP_SIZE, tm), 0)
    gsc_rows = []
    for gidx in range(N_EXPERT_GROUPS):
        sg = sel[gidx * GROUP_SIZE:(gidx + 1) * GROUP_SIZE, :]
        m1 = jnp.max(sg, axis=0, keepdims=True)
        f1 = jnp.min(jnp.where(sg == m1, i8, GROUP_SIZE), axis=0, keepdims=True)
        m2 = jnp.max(jnp.where(i8 == f1, NEG_INF, sg), axis=0, keepdims=True)
        gsc_rows.append(m1 + m2)
    gsc = jnp.concatenate(gsc_rows, axis=0)

    gsel = jnp.zeros(gsc.shape, F32)
    for _ in range(TOPK_GROUPS):
        m = jnp.max(gsc, axis=0, keepdims=True)
        f = jnp.min(jnp.where(gsc == m, i8, N_EXPERT_GROUPS), axis=0, keepdims=True)
        pick = i8 == f
        gsel = jnp.where(pick, 1.0, gsel)
        gsc = jnp.where(pick, NEG_INF, gsc)
    esel = jnp.concatenate(
        [jnp.broadcast_to(gsel[gidx:gidx + 1, :], (GROUP_SIZE, tm)) for gidx in range(N_EXPERT_GROUPS)],
        axis=0)

    cur = jnp.where(esel > 0.0, sel, NEG_INF)
    ei = lax.broadcasted_iota(I32, cur.shape, 0)
    idx_rows, sc_rows = [], []
    chosen = jnp.zeros(cur.shape, F32)
    for _ in range(TOP_K):
        m = jnp.max(cur, axis=0, keepdims=True)
        f = jnp.min(jnp.where(cur == m, ei, N_EXPERTS), axis=0, keepdims=True)
        pick = ei == f
        idx_rows.append(f)
        sc_rows.append(jnp.sum(jnp.where(pick, scores, 0.0), axis=0, keepdims=True))
        chosen = jnp.where(pick, 1.0, chosen)
        cur = jnp.where(pick, NEG_INF, cur)
    eidx = jnp.concatenate(idx_rows, axis=0)
    sc = jnp.concatenate(sc_rows, axis=0)
    eidx_ref[...] = eidx
    wts_ref[...] = sc / jnp.sum(sc, axis=0, keepdims=True) * ROUTE_SCALE

    before = _dot(chosen.astype(BF16), tri_ref[...]) + run_ref[...]
    pos_rows = [jnp.sum(jnp.where(ei == idx_rows[k], before, 0.0), axis=0, keepdims=True)
                for k in range(TOP_K)]
    pos_ref[...] = jnp.concatenate(pos_rows, axis=0).astype(I32)
    run_new = run_ref[...] + jnp.sum(chosen, axis=1, keepdims=True)
    run_ref[...] = run_new
    cnt_ref[...] = jnp.broadcast_to(run_new, cnt_ref.shape).astype(I32)


def _router(x, router_w, router_b, first_row, T):
    D = x.shape[1]
    tm = min(ROUTER_TILE, T)
    n_tiles = T // tm
    first_tile = first_row // tm
    assert n_tiles * tm == T and first_tile * tm == first_row
    wt = router_w.T.astype(F32)
    wh = wt.astype(BF16)
    wl = (wt - wh.astype(F32)).astype(BF16)
    w2 = jnp.stack([wh, wl])
    rb = router_b.astype(F32).reshape(N_EXPERTS, 1)
    tri = jnp.asarray(np.triu(np.ones((tm, tm)), 1), BF16)
    full = lambda a: pl.BlockSpec(a.shape, lambda i: (0,) * a.ndim)
    col = pl.BlockSpec((TOP_K, tm), lambda i: (0, i))
    return pl.pallas_call(
        _router_kernel, grid=(n_tiles,),
        in_specs=[pl.BlockSpec((tm, D), lambda i: (i + first_tile, 0)), full(w2), full(rb), full(tri)],
        out_specs=[col, col, col, pl.BlockSpec((N_EXPERTS, 128), lambda i: (0, 0)),
                   pl.BlockSpec((tm, HALF), lambda i: (i, 0))],
        out_shape=[jax.ShapeDtypeStruct((TOP_K, T), I32), jax.ShapeDtypeStruct((TOP_K, T), F32),
                   jax.ShapeDtypeStruct((TOP_K, T), I32), jax.ShapeDtypeStruct((N_EXPERTS, 128), I32),
                   jax.ShapeDtypeStruct((T, HALF), I32)],
        scratch_shapes=[pltpu.VMEM((N_EXPERTS, 1), F32)],
        compiler_params=_cparams("arbitrary"), name="router")(x, w2, rb, tri)


def _dest_kernel(start_ref, eidx_ref, pos_ref, o_ref):
    e = eidx_ref[...]
    acc = pos_ref[...]
    for j in range(N_EXPERTS):
        acc = acc + jnp.where(e == j, start_ref[j], 0)
    o_ref[...] = acc


def _dest_rows(eidx, pos, seg_start):
    K, T = eidx.shape
    tl = min(T, 2048)
    blk = pl.BlockSpec((K, tl), lambda i, s: (0, i))
    grid_spec = pltpu.PrefetchScalarGridSpec(
        num_scalar_prefetch=1, grid=(T // tl,), in_specs=[blk, blk], out_specs=blk)
    return pl.pallas_call(
        _dest_kernel, grid_spec=grid_spec, out_shape=jax.ShapeDtypeStruct((K, T), I32),
        compiler_params=_cparams("parallel"), name="dest_rows")(seg_start, eidx, pos)


def _gather_rows(table, idx):
    n_rows = idx.shape[0]
    width = table.shape[1]
    per_worker = n_rows // SC_WORKERS
    n_chunks = per_worker // SC_CHUNK
    assert per_worker * SC_WORKERS == n_rows and n_chunks * SC_CHUNK == per_worker
    mesh = plsc.VectorSubcoreMesh(core_axis_name="c", subcore_axis_name="s")

    @functools.partial(
        pl.kernel, mesh=mesh,
        out_type=jax.ShapeDtypeStruct((n_rows, width), table.dtype),
        scratch_types=[pltpu.VMEM((SC_CHUNK,), I32), pltpu.VMEM((SC_CHUNK, width), table.dtype),
                       pltpu.SemaphoreType.DMA])
    def gather(table_hbm, idx_hbm, out_hbm, idx_v, rows_v, sem):
        wid = lax.axis_index("s") * SC_CORES + lax.axis_index("c")
        base = wid * per_worker

        @pl.loop(0, n_chunks)
        def _(j):
            off = base + j * SC_CHUNK
            pltpu.sync_copy(idx_hbm.at[pl.ds(off, SC_CHUNK)], idx_v)
            pltpu.async_copy(table_hbm.at[idx_v], rows_v, sem).wait()
            pltpu.sync_copy(rows_v, out_hbm.at[pl.ds(off, SC_CHUNK)])

    return gather(table, idx)


def _scatter_rows(rows, dest):
    n_tok, width = rows.shape
    n_dst = dest.shape[0]
    per_worker = n_tok // SC_WORKERS
    n_chunks = per_worker // SC_CHUNK
    assert per_worker * SC_WORKERS == n_tok and n_chunks * SC_CHUNK == per_worker
    mesh = plsc.VectorSubcoreMesh(core_axis_name="c", subcore_axis_name="s")

    @functools.partial(
        pl.kernel, mesh=mesh,
        out_type=jax.ShapeDtypeStruct((n_dst * n_tok, width), rows.dtype),
        scratch_types=[pltpu.VMEM((n_dst, SC_CHUNK), I32), pltpu.VMEM((SC_CHUNK, width), rows.dtype),
                       pltpu.SemaphoreType.DMA])
    def scatter(rows_hbm, dest_hbm, out_hbm, idx_v, rows_v, sem):
        wid = lax.axis_index("s") * SC_CORES + lax.axis_index("c")
        base = wid * per_worker

        @pl.loop(0, n_chunks)
        def _(j):
            off = base + j * SC_CHUNK
            pltpu.sync_copy(dest_hbm.at[:, pl.ds(off, SC_CHUNK)], idx_v)
            pltpu.sync_copy(rows_hbm.at[pl.ds(off, SC_CHUNK)], rows_v)
            copies = [pltpu.async_copy(rows_v, out_hbm.at[idx_v.at[k]], sem) for k in range(n_dst)]
            for c in copies:
                c.wait()

    return scatter(rows, dest)


def _expert_kernel(blk_ref, exp_ref, lo_ref, hi_ref, slot_ref, nxt_ref, xs_hbm, wg_hbm, wu_hbm, wd_hbm,
                   ys_hbm, wg_buf, wu_buf, wd_buf, wgu_s, wd_s, sem, xs_buf, xs_sem, ys_buf, ys_sem,
                   *, layer):
    i = pl.program_id(0)
    prev = jnp.maximum(i - 1, 0)

    def weight_copies(expert, slot):
        return [pltpu.make_async_copy(src.at[layer, expert], dst.at[slot], sem.at[slot])
                for src, dst in ((wg_hbm, wg_buf), (wu_hbm, wu_buf), (wd_hbm, wd_buf))]

    @pl.when(i == 0)
    def _():
        for c in weight_copies(exp_ref[0], slot_ref[0]):
            c.start()

    @pl.when(jnp.logical_or(i == 0, exp_ref[i] != exp_ref[prev]))
    def _():
        slot = slot_ref[i]
        for c in weight_copies(exp_ref[i], slot):
            c.wait()
        wgu_s[:, :EXPERT_DIM] = wg_buf[slot].astype(BF16)
        wgu_s[:, EXPERT_DIM:] = wu_buf[slot].astype(BF16)
        wd_s[...] = wd_buf[slot].astype(BF16)

        @pl.when(nxt_ref[i] >= 0)
        def _():
            for c in weight_copies(nxt_ref[i], 1 - slot):
                c.start()

    n_blocks = xs_hbm.shape[0] // EXPERT_ROWS
    blk = blk_ref[i]
    first = jnp.logical_or(i == 0, blk != blk_ref[prev])

    def rows_copy(block):
        slot = block % EXPERT_XS_SLOTS
        src = xs_hbm.at[pl.ds(pl.multiple_of(block * EXPERT_ROWS, EXPERT_ROWS), EXPERT_ROWS)]
        return pltpu.make_async_copy(src, xs_buf.at[slot], xs_sem.at[slot])

    @pl.when(i == 0)
    def _():
        for b0 in range(min(EXPERT_XS_SLOTS - 1, n_blocks)):
            rows_copy(b0).start()

    def result_copy(block):
        slot = block % EXPERT_XS_SLOTS
        dst = ys_hbm.at[pl.ds(pl.multiple_of(block * EXPERT_ROWS, EXPERT_ROWS), EXPERT_ROWS)]
        return pltpu.make_async_copy(ys_buf.at[slot], dst, ys_sem.at[slot])

    @pl.when(first)
    def _():
        rows_copy(blk).wait()

        @pl.when(blk + EXPERT_XS_SLOTS - 1 < n_blocks)
        def _():
            rows_copy(blk + EXPERT_XS_SLOTS - 1).start()

        @pl.when(blk >= EXPERT_XS_SLOTS)
        def _():
            result_copy(blk - EXPERT_XS_SLOTS).wait()

    xs_ref = xs_buf.at[blk % EXPERT_XS_SLOTS]
    ys_ref = ys_buf.at[blk % EXPERT_XS_SLOTS]
    lo = lo_ref[i]
    hi = hi_ref[i]

    def sub_block(r0):
        rows = slice(r0, r0 + EXPERT_SUB_ROWS)
        w = xs_ref[rows, :]
        xlo = lax.bitcast_convert_type(w.astype(jnp.int16), BF16)
        xhi = lax.bitcast_convert_type(lax.shift_right_logical(w, 16).astype(jnp.int16), BF16)
        gu = _dot(xlo, wgu_s[:HALF, :]) + _dot(xhi, wgu_s[HALF:, :])
        g = gu[:, :EXPERT_DIM]
        hb = (g * jax.nn.sigmoid(g) * gu[:, EXPERT_DIM:]).astype(BF16)
        y = _dot(hb, wd_s[...])
        packed = pltpu.pack_elementwise([y[:, :HALF], y[:, HALF:]], packed_dtype=BF16)
        row = r0 + lax.broadcasted_iota(I32, (EXPERT_SUB_ROWS, 1), 0)
        mine = jnp.logical_and(row >= lo, row < hi)
        kept = jnp.where(first, 0, ys_ref[rows, :])
        ys_ref[rows, :] = jnp.where(mine, packed, kept)

    def touched(r0):
        return jnp.logical_and(lo < r0 + EXPERT_SUB_ROWS, hi > r0)

    for r0 in range(0, EXPERT_ROWS, 2 * EXPERT_SUB_ROWS):
        r1 = r0 + EXPERT_SUB_ROWS
        t0, t1 = touched(r0), touched(r1)

        @pl.when(jnp.logical_and(t0, t1))
        def _():
            sub_block(r0)
            sub_block(r1)

        pl.when(jnp.logical_and(t0, jnp.logical_not(t1)))(functools.partial(sub_block, r0))
        pl.when(jnp.logical_and(jnp.logical_not(t0), t1))(functools.partial(sub_block, r1))

    n_items = pl.num_programs(0)
    final = i == n_items - 1
    block_done = jnp.logical_or(final, blk_ref[jnp.minimum(i + 1, n_items - 1)] != blk)

    @pl.when(block_done)
    def _():
        result_copy(blk).start()

    @pl.when(final)
    def _():
        for b0 in range(max(n_blocks - EXPERT_XS_SLOTS, 0), n_blocks):
            result_copy(b0).wait()


def _expert_items(counts, n_rows):
    n_blocks = n_rows // EXPERT_ROWS
    n_items = n_blocks + N_EXPERTS - 1
    end = jnp.cumsum(counts)
    start = end - counts
    first_blk = start // EXPERT_ROWS
    n_blk = jnp.where(counts > 0, (end - 1) // EXPERT_ROWS - first_blk + 1, 0)
    item_end = jnp.cumsum(n_blk)
    item_start = item_end - n_blk
    slot = jnp.arange(n_items, dtype=I32)
    e = jnp.minimum(jnp.sum((item_end[None, :] <= slot[:, None]).astype(I32), axis=1), N_EXPERTS - 1)
    onehot = (e[:, None] == jnp.arange(N_EXPERTS, dtype=I32)[None, :]).astype(I32)
    pick = lambda v: jnp.sum(onehot * v[None, :], axis=1)
    valid = slot < item_end[-1]
    blk = jnp.where(valid, pick(first_blk) + slot - pick(item_start), n_blocks - 1)
    lo = jnp.clip(pick(start) - blk * EXPERT_ROWS, 0, EXPERT_ROWS)
    hi = jnp.clip(pick(end) - blk * EXPERT_ROWS, 0, EXPERT_ROWS)
    last_e = jnp.max(jnp.where(counts > 0, jnp.arange(N_EXPERTS, dtype=I32), 0))
    e = jnp.where(valid, e, last_e)
    hi = jnp.where(valid, hi, 0)
    lo = jnp.where(valid, lo, 0)
    change = jnp.concatenate([jnp.ones((1,), I32), (e[1:] != e[:-1]).astype(I32)])
    slot = (jnp.cumsum(change) - 1) % 2
    later = jnp.where(jnp.arange(N_EXPERTS, dtype=I32)[None, :] > e[:, None], counts[None, :] > 0, False)
    nxt = jnp.where(jnp.any(later, axis=1), jnp.argmax(later, axis=1), -1)
    return tuple(a.astype(I32) for a in (blk, e, lo, hi, slot, nxt))


def _experts(xs, items, w_gate, w_up, w_down, layer):
    n_rows = xs.shape[0]
    n_items = items[0].shape[0]
    hbm = pl.BlockSpec(memory_space=pl.ANY)
    grid_spec = pltpu.PrefetchScalarGridSpec(
        num_scalar_prefetch=len(items), grid=(n_items,),
        in_specs=[hbm, hbm, hbm, hbm],
        out_specs=hbm,
        scratch_shapes=[pltpu.VMEM((2, D_MODEL, EXPERT_DIM), F32),
                        pltpu.VMEM((2, D_MODEL, EXPERT_DIM), F32),
                        pltpu.VMEM((2, EXPERT_DIM, D_MODEL), F32),
                        pltpu.VMEM((D_MODEL, 2 * EXPERT_DIM), BF16),
                        pltpu.VMEM((EXPERT_DIM, D_MODEL), BF16),
                        pltpu.SemaphoreType.DMA((2,)),
                        pltpu.VMEM((EXPERT_XS_SLOTS, EXPERT_ROWS, HALF), I32),
                        pltpu.SemaphoreType.DMA((EXPERT_XS_SLOTS,)),
                        pltpu.VMEM((EXPERT_XS_SLOTS, EXPERT_ROWS, HALF), I32),
                        pltpu.SemaphoreType.DMA((EXPERT_XS_SLOTS,))])
    return pl.pallas_call(
        functools.partial(_expert_kernel, layer=layer), grid_spec=grid_spec,
        out_shape=jax.ShapeDtypeStruct((n_rows, HALF), I32),
        compiler_params=_cparams("arbitrary"), name="experts")(*items, xs, w_gate, w_up, w_down)


def _moe_out_kernel(x_ref, yg_ref, wt_ref, sgu_ref, sd_ref, g_ref, b_ref, *rest):
    o_ref = rest[-1]
    x = x_ref[...]
    wt = wt_ref[...]
    lo = jnp.zeros((x.shape[0], HALF), F32)
    hi = jnp.zeros((x.shape[0], HALF), F32)
    for k in range(TOP_K):
        w = yg_ref[k]
        wk = wt[:, k:k + 1]
        lo = lo + wk * _unpack_lo(w)
        hi = hi + wk * _unpack_hi(w)
    gu = _dot(x.astype(BF16), sgu_ref[...])
    g = gu[:, :EXPERT_DIM]
    hs = (g * jax.nn.sigmoid(g) * gu[:, EXPERT_DIM:]).astype(BF16)
    ffn = jnp.concatenate([lo, hi], axis=1) + _dot(hs, sd_ref[...])
    o_ref[...] = _ln(DEEPNORM_ALPHA * x + ffn, g_ref[...], b_ref[...])


def _moe_out(x, yg, wts, sh_gate, sh_up, sh_down, g, b, first_tile, partial_out):
    T, D = x.shape
    tm = ROW_TILE
    n_tiles = yg.shape[1] // tm
    xrow = pl.BlockSpec((tm, D), lambda i: (i + first_tile, 0))
    full = lambda a: pl.BlockSpec(a.shape, lambda i: (0,) * a.ndim)
    sgu = jnp.concatenate([sh_gate, sh_up], axis=1).astype(BF16)
    sd = sh_down.astype(BF16)
    g = g.reshape(1, D)
    b = b.reshape(1, D)
    args = [x, yg, wts, sgu, sd, g, b]
    in_specs = [xrow, pl.BlockSpec((TOP_K, tm, HALF), lambda i: (0, i, 0)),
                pl.BlockSpec((tm, TOP_K), lambda i: (i, 0)), full(sgu), full(sd), full(g), full(b)]
    aliases = {}
    if partial_out is not None:
        args.append(partial_out)
        in_specs.append(pl.BlockSpec(memory_space=pl.ANY))
        aliases = {len(args) - 1: 0}
    return pl.pallas_call(
        _moe_out_kernel, grid=(n_tiles,), in_specs=in_specs,
        out_specs=xrow, out_shape=jax.ShapeDtypeStruct((T, D), F32),
        input_output_aliases=aliases,
        compiler_params=_cparams("parallel"), name="moe_out")(*args)


def _moe(x, router_w, router_b, w_gate, w_up, w_down, layer, sh_gate, sh_up, sh_down, g, b):
    T = x.shape[0]
    tiles = T // ROW_TILE // MOE_TOKEN_GROUPS
    tg = tiles * ROW_TILE
    out = None
    for grp in range(MOE_TOKEN_GROUPS):
        eidx, wts, pos, cnt, xp = _router(x, router_w, router_b, grp * tg, tg)
        counts = cnt[:, 0]
        seg_start = (jnp.cumsum(counts) - counts).astype(I32)
        dest = _dest_rows(eidx, pos, seg_start)
        xs = _scatter_rows(xp, dest)
        ys = _experts(xs, _expert_items(counts, tg * TOP_K), w_gate, w_up, w_down, layer)
        yg = _gather_rows(ys, dest.reshape(tg * TOP_K)).reshape(TOP_K, tg, HALF)
        out = _moe_out(x, yg, wts.T, sh_gate, sh_up, sh_down, g, b, grp * tiles, out)
    return out


def kernel(x, ln_in_g, ln_in_b, e_w_in, e_w_fourier, e_q_gain, e_k_gain, e_w_out, o_w_in, o_b_in, o_v_ln_g, o_v_ln_b, o_w_spatial, o_b_spatial, o_w_out, ln_mix_g, ln_mix_b, ln_ffn_g, ln_ffn_b, router_w, router_b, exp_w_gate, exp_w_up, exp_w_down, sh_w_gate, sh_w_up, sh_w_down):
    B, S, D = x.shape
    T = B * S
    h = _layer_norm(x.reshape(T, D), ln_in_g, ln_in_b)
    for i in range(DEPTH):
        j = i // 2
        if i % 2 == 0:
            a, qt, k2, vt = _even_in(h, e_w_in[j], e_q_gain[j], e_k_gain[j], B, S)
            a_out = _fourier(a, e_w_fourier[j], B, S)
            attn = _attention(qt, k2, vt, B, S)
            h = _even_out(a_out, attn, e_w_out[j], h, ln_mix_g[i], ln_mix_b[i])
        else:
            h = _odd_mixer(h, o_w_in[j], o_b_in[j], o_v_ln_g[j], o_v_ln_b[j], o_w_spatial[j],
                           o_b_spatial[j], o_w_out[j], ln_mix_g[i], ln_mix_b[i])
        h = _moe(h, router_w[i], router_b[i], exp_w_gate, exp_w_up, exp_w_down, i,
                 sh_w_gate[i], sh_w_up[i], sh_w_down[i], ln_ffn_g[i], ln_ffn_b[i])
    return h.reshape(B, S, D)
```

```python
import functools
import math

import numpy as np
import jax
import jax.numpy as jnp
from jax import lax
from jax.experimental import pallas as pl
from jax.experimental.pallas import tpu as pltpu
from jax.experimental.pallas import tpu_sc as plsc

F32 = jnp.float32
BF16 = jnp.bfloat16
I32 = jnp.int32

D_MODEL = 1024
DEPTH = 4
GRID_W = 64
N_FGROUPS = 4
FGROUP_DIM = 128
F_WIDTH = N_FGROUPS * FGROUP_DIM
N_HEADS = 8
N_KV_HEADS = 2
HEAD_DIM = 64
Q_GROUP = N_HEADS // N_KV_HEADS
Q_WIDTH = N_HEADS * HEAD_DIM
KV_WIDTH = N_KV_HEADS * HEAD_DIM
ROPE_THETA = 10000.0
ROPE_PAIRS = HEAD_DIM // 4
EVEN_IN_WIDTH = F_WIDTH + Q_WIDTH + 2 * KV_WIDTH
CHUNK = 128
N_CGROUPS = 8
CGROUP_DIM = D_MODEL // N_CGROUPS
C_WIDTH = N_CGROUPS * CGROUP_DIM
N_EXPERTS = 64
EXPERT_DIM = 256
TOP_K = 8
N_EXPERT_GROUPS = 8
GROUP_SIZE = N_EXPERTS // N_EXPERT_GROUPS
TOPK_GROUPS = 4
ROUTE_SCALE = 2.5
LN_EPS = 1e-5
QK_EPS = 1e-6
DEEPNORM_ALPHA = (2 * DEPTH) ** 0.25

VMEM_LIMIT_BYTES = 56 * 1024 * 1024
ROW_TILE = 512
DFT_N1 = 64
DFT_KRON = 4
DFT_PITCH_PAD = 8
ROUTER_TILE = 1024
EXPERT_ROWS = 2048
EXPERT_XS_SLOTS = 3
EXPERT_SUB_ROWS = 512
HALF = D_MODEL // 2
SC_CORES = 2
SC_SUBCORES = 16
SC_WORKERS = SC_CORES * SC_SUBCORES
SC_CHUNK = 128
MOE_TOKEN_GROUPS = 2
ATT_TQ = 256
ATT_TK = 256
ATT_V_ROWS = 80
ATT_BOUND_SLACK = 1.0 + 2.0 ** -7
ATT_MIN_ROW_SUM = 2.0 ** -80
NEG_INF = float("-inf")


def _cparams(*sem):
    return pltpu.CompilerParams(dimension_semantics=sem, vmem_limit_bytes=VMEM_LIMIT_BYTES)


def _ln(x, g, b):
    mu = jnp.mean(x, axis=-1, keepdims=True)
    xc = x - mu
    var = jnp.mean(xc * xc, axis=-1, keepdims=True)
    return xc * lax.rsqrt(var + LN_EPS) * g + b


def _dot(a, b):
    return jnp.dot(a, b, preferred_element_type=F32)


def _unpack_lo(w):
    return lax.bitcast_convert_type(lax.shift_left(w, 16), F32)


def _unpack_hi(w):
    return lax.bitcast_convert_type(w & jnp.int32(-65536), F32)


def _ln_kernel(x_ref, g_ref, b_ref, o_ref):
    o_ref[...] = _ln(x_ref[...], g_ref[...], b_ref[...])


def _layer_norm(x, g, b):
    T, D = x.shape
    row = pl.BlockSpec((ROW_TILE, D), lambda i: (i, 0))
    vec = pl.BlockSpec((1, D), lambda i: (0, 0))
    return pl.pallas_call(
        _ln_kernel, grid=(T // ROW_TILE,), in_specs=[row, vec, vec], out_specs=row,
        out_shape=jax.ShapeDtypeStruct((T, D), F32), compiler_params=_cparams("parallel"),
        name="ln_in")(x, g.reshape(1, D), b.reshape(1, D))


def _even_in_kernel(x_ref, w_ref, qm_ref, km_ref, qg_ref, kg_ref, cos_ref, sin_ref,
                    a_ref, qt_ref, k_ref, vt_ref):
    tm = x_ref.shape[0]
    h = _dot(x_ref[...].astype(BF16), w_ref[...])
    a_ref[...] = h[:, :F_WIDTH].astype(BF16)
    q = h[:, F_WIDTH:F_WIDTH + Q_WIDTH]
    k = h[:, F_WIDTH + Q_WIDTH:F_WIDTH + Q_WIDTH + KV_WIDTH]
    v = h[:, F_WIDTH + Q_WIDTH + KV_WIDTH:]
    cos = cos_ref[...]
    sin = sin_ref[...]
    lane = lax.broadcasted_iota(I32, (tm, 128), 1)
    first_of_pair = (lane & ROPE_PAIRS) == 0

    def mean_sq(xf, m_ref):
        sq = xf * xf
        hi = sq.astype(BF16)
        lo = (sq - hi.astype(F32)).astype(BF16)
        return _dot(hi, m_ref[...]) + _dot(lo, m_ref[...])

    def rope(xn):
        sw = jnp.where(first_of_pair, pltpu.roll(xn, 128 - ROPE_PAIRS, 1), pltpu.roll(xn, ROPE_PAIRS, 1))
        return xn * cos + sw * sin

    qn = q * lax.rsqrt(mean_sq(q, qm_ref) + QK_EPS) * qg_ref[...]
    scale = math.log2(math.e) / math.sqrt(HEAD_DIM)
    for c in range(Q_WIDTH // 128):
        qt_ref[c * 128:(c + 1) * 128, :] = (rope(qn[:, c * 128:(c + 1) * 128]) * scale).T.astype(BF16)
    kn = rope(k * lax.rsqrt(mean_sq(k, km_ref) + QK_EPS) * kg_ref[...])
    low = lane < HEAD_DIM
    k_ref[0] = jnp.where(low, kn, 0.0).astype(BF16)
    k_ref[1] = jnp.where(low, pltpu.roll(kn, HEAD_DIM, 1), 0.0).astype(BF16)
    ones_col = jnp.where(lane == HEAD_DIM, 1.0, 0.0)
    vt_ref[0:128, :] = jnp.where(low, v, ones_col).T.astype(BF16)
    vt_ref[128:256, :] = jnp.where(low, pltpu.roll(v, HEAD_DIM, 1), ones_col).T.astype(BF16)


def _rope_tables(S):
    t = np.arange(S)
    inv = ROPE_THETA ** (-np.arange(ROPE_PAIRS, dtype=np.float64) / ROPE_PAIRS)
    ang_r = (t // GRID_W)[:, None] * inv
    ang_c = (t % GRID_W)[:, None] * inv
    cos = np.concatenate([np.cos(ang_r), np.cos(ang_r), np.cos(ang_c), np.cos(ang_c)], axis=1)
    sin = np.concatenate([-np.sin(ang_r), np.sin(ang_r), -np.sin(ang_c), np.sin(ang_c)], axis=1)
    return (jnp.asarray(np.tile(cos, (1, 2)), F32), jnp.asarray(np.tile(sin, (1, 2)), F32))


def _head_mean_matrix(width):
    m = np.kron(np.eye(width // HEAD_DIM), np.full((HEAD_DIM, HEAD_DIM), 1.0 / HEAD_DIM))
    return jnp.asarray(m, BF16)


def _even_in(x, w_in, q_gain, k_gain, B, S):
    T, D = x.shape
    tm = ROW_TILE
    ns = S // tm
    cos, sin = _rope_tables(S)
    row = lambda w: pl.BlockSpec((tm, w), lambda i: (i, 0))
    full = lambda a: pl.BlockSpec(a.shape, lambda i: (0,) * a.ndim)
    tab = pl.BlockSpec((tm, 128), lambda i: (i % ns, 0))
    w = w_in.astype(BF16)
    qm = _head_mean_matrix(Q_WIDTH)
    km = _head_mean_matrix(KV_WIDTH)
    qg = jnp.tile(q_gain.astype(F32), N_HEADS).reshape(1, Q_WIDTH)
    kg = jnp.tile(k_gain.astype(F32), N_KV_HEADS).reshape(1, KV_WIDTH)
    return pl.pallas_call(
        _even_in_kernel, grid=(T // tm,),
        in_specs=[row(D), full(w), full(qm), full(km), full(qg), full(kg), tab, tab],
        out_specs=[row(F_WIDTH),
                   pl.BlockSpec((None, Q_WIDTH, tm), lambda i: (i // ns, 0, i % ns)),
                   pl.BlockSpec((N_KV_HEADS, tm, 128), lambda i: (0, i, 0)),
                   pl.BlockSpec((None, N_KV_HEADS * 128, tm), lambda i: (i // ns, 0, i % ns))],
        out_shape=[jax.ShapeDtypeStruct((T, F_WIDTH), BF16),
                   jax.ShapeDtypeStruct((B, Q_WIDTH, S), BF16),
                   jax.ShapeDtypeStruct((N_KV_HEADS, T, 128), BF16),
                   jax.ShapeDtypeStruct((B, N_KV_HEADS * 128, S), BF16)],
        compiler_params=_cparams("parallel"), name="even_in")(x, w, qm, km, qg, kg, cos, sin)


def _fourier_kernel(a_ref, dftc_ref, taba_ref, kc_ref, ks_ref, wf_ref, o_ref,
                    zr_ref, zi_ref, ur_ref, ui_ref, y_ref):
    S = a_ref.shape[0]
    n1_count = DFT_N1
    n2_count = S // DFT_N1
    pz = n1_count + DFT_PITCH_PAD
    pu = n2_count + DFT_PITCH_PAD
    blk = DFT_KRON * DFT_N1
    scale = 1.0 / math.sqrt(S * FGROUP_DIM)

    def channel_dft(j, carry):
        zz = _dot(a_ref[pl.ds(pl.multiple_of(j * blk, blk), blk), :], dftc_ref[...])
        for q in range(DFT_KRON):
            dst = pl.ds(pl.multiple_of((j * DFT_KRON + q) * pz, 8), n1_count)
            zr_ref[dst, :] = zz[q * n1_count:(q + 1) * n1_count, :FGROUP_DIM]
            zi_ref[dst, :] = zz[q * n1_count:(q + 1) * n1_count, FGROUP_DIM:]
        return carry

    lax.fori_loop(0, S // blk, channel_dft, 0, unroll=4)

    def stage_a(n1, carry):
        src = pl.ds(n1, n2_count, stride=pz)
        zn = jnp.concatenate([zr_ref[src, :], zi_ref[src, :]], axis=1).astype(BF16)
        r = _dot(taba_ref[n1], zn)
        dst = pl.ds(pl.multiple_of(n1 * pu, 8), n2_count)
        ur_ref[dst, :] = r[:n2_count, :FGROUP_DIM] + r[n2_count:, FGROUP_DIM:]
        ui_ref[dst, :] = r[:n2_count, FGROUP_DIM:] - r[n2_count:, :FGROUP_DIM]
        return carry

    lax.fori_loop(0, n1_count, stage_a, 0, unroll=8)

    def stage_b(j, carry):
        srcs = [pl.ds(j * DFT_KRON + q, n1_count, stride=pu) for q in range(DFT_KRON)]
        ur = jnp.concatenate([ur_ref[s, :] for s in srcs], axis=0).astype(BF16)
        ui = jnp.concatenate([ui_ref[s, :] for s in srcs], axis=0).astype(BF16)
        re = _dot(kc_ref[...], ur) + _dot(ks_ref[...], ui)
        out = _dot((re * scale).astype(BF16), wf_ref[...])
        for q in range(DFT_KRON):
            y_ref[srcs[q], :] = out[q * n1_count:(q + 1) * n1_count]
        return carry

    lax.fori_loop(0, S // blk, stage_b, 0, unroll=8)

    def compact(k1, carry):
        o_ref[pl.ds(pl.multiple_of(k1 * n2_count, n2_count), n2_count), :] = (
            y_ref[pl.ds(pl.multiple_of(k1 * pu, 8), n2_count), :].astype(BF16))
        return carry

    lax.fori_loop(0, n1_count, compact, 0)


def _dft_tables(S):
    n1c, n2c = DFT_N1, S // DFT_N1
    c = np.arange(FGROUP_DIM)
    ang = 2 * np.pi * np.outer(c, c) / FGROUP_DIM
    dftc = np.concatenate([np.cos(ang), -np.sin(ang)], axis=1)
    n1 = np.arange(n1c)[:, None, None]
    k2 = np.arange(n2c)[None, :, None]
    n2 = np.arange(n2c)[None, None, :]
    th = 2 * np.pi * (n2 * k2 / n2c + n1 * k2 / S)
    taba = np.concatenate([np.cos(th), np.sin(th)], axis=1)
    k1 = np.arange(n1c)
    g = 2 * np.pi * np.outer(k1, k1) / n1c
    eye = np.eye(DFT_KRON)
    kc = np.kron(eye, np.cos(g))
    ks = np.kron(eye, np.sin(g))
    return tuple(jnp.asarray(t, BF16) for t in (dftc, taba, kc, ks))


def _fourier(a, w_fourier, B, S):
    T = a.shape[0]
    dftc, taba, kc, ks = _dft_tables(S)
    full = lambda t: pl.BlockSpec(t.shape, lambda b, g: (0,) * t.ndim)
    blk = pl.BlockSpec((S, FGROUP_DIM), lambda b, g: (b, g))
    return pl.pallas_call(
        _fourier_kernel, grid=(B, N_FGROUPS),
        in_specs=[blk, full(dftc), full(taba), full(kc), full(ks),
                  pl.BlockSpec((None, FGROUP_DIM, FGROUP_DIM), lambda b, g: (g, 0, 0))],
        out_specs=blk,
        out_shape=jax.ShapeDtypeStruct((T, F_WIDTH), BF16),
        scratch_shapes=(
            [pltpu.VMEM((S // DFT_N1 * (DFT_N1 + DFT_PITCH_PAD), FGROUP_DIM), F32)] * 2
            + [pltpu.VMEM((DFT_N1 * (S // DFT_N1 + DFT_PITCH_PAD), FGROUP_DIM), F32)] * 3),
        compiler_params=_cparams("parallel", "parallel"), name="fourier")(
            a, dftc, taba, kc, ks, w_fourier.astype(BF16))


def _attn_kernel(qt_ref, k_ref, vt_ref, o_ref, qs_ref, kmax_ref, acc_ref, m_ref, s0_ref, s1_ref,
                 p0_ref, p1_ref):
    tq = qt_ref.shape[1]
    n_keys = k_ref.shape[0]
    tk = min(ATT_TK, n_keys)
    n_chunks = n_keys // tk
    assert n_chunks % 2 == 0 and n_chunks * tk == n_keys

    def keys(c):
        return k_ref[pl.ds(pl.multiple_of(c * tk, tk), tk), :]

    @pl.when(pl.program_id(2) == 0)
    def _():
        def body(c, best):
            k = keys(c).astype(F32)
            return jnp.maximum(best, jnp.sum(k * k, axis=1, keepdims=True))
        best = lax.fori_loop(0, n_chunks, body, jnp.zeros((tk, 1), F32))
        kmax_ref[...] = jnp.broadcast_to(jnp.sqrt(jnp.max(best, axis=0, keepdims=True)), kmax_ref.shape)

    qs_ref[HEAD_DIM:, :] = jnp.zeros((128 - HEAD_DIM, Q_GROUP * tq), BF16)
    for g in range(Q_GROUP):
        qs_ref[:HEAD_DIM, g * tq:(g + 1) * tq] = qt_ref[g * HEAD_DIM:(g + 1) * HEAD_DIM, :]
    qf = qs_ref[...].astype(F32)
    bound = jnp.sqrt(jnp.sum(qf * qf, axis=0, keepdims=True)) * kmax_ref[0:1, 0:1] * ATT_BOUND_SLACK

    def scores(c):
        return _dot(keys(c), qs_ref[...])

    def values(c):
        return vt_ref[:ATT_V_ROWS, pl.ds(pl.multiple_of(c * tk, tk), tk)]

    def weights(s_buf):
        return jnp.exp2(s_buf[...] - bound).astype(BF16)

    def accumulate(p_buf, c):
        acc_ref[...] += _dot(values(c), p_buf[...])

    last = n_chunks - 1

    def fast(c2, carry):
        c = 2 * c2
        s0_ref[...] = scores(jnp.minimum(c + 2, last))
        p1_ref[...] = weights(s1_ref)
        accumulate(p0_ref, c)
        s1_ref[...] = scores(jnp.minimum(c + 3, last))
        p0_ref[...] = weights(s0_ref)
        accumulate(p1_ref, c + 1)
        return carry

    acc_ref[...] = jnp.zeros(acc_ref.shape, F32)
    s0_ref[...] = scores(0)
    p0_ref[...] = weights(s0_ref)
    s1_ref[...] = scores(1)
    lax.fori_loop(0, n_chunks // 2, fast, 0, unroll=True)
    underflow = jnp.min(acc_ref[HEAD_DIM:HEAD_DIM + 1, :]) < ATT_MIN_ROW_SUM

    @pl.when(underflow)
    def _():
        def safe(c, carry):
            s = scores(c)
            m_old = m_ref[...]
            m_new = jnp.maximum(m_old, jnp.max(s, axis=0, keepdims=True))
            acc_ref[...] = (jnp.exp2(m_old - m_new) * acc_ref[...]
                            + _dot(values(c), jnp.exp2(s - m_new).astype(BF16)))
            m_ref[...] = m_new
            return carry

        m_ref[...] = jnp.full(m_ref.shape, NEG_INF, F32)
        acc_ref[...] = jnp.zeros(acc_ref.shape, F32)
        lax.fori_loop(0, n_chunks, safe, 0)

    acc = acc_ref[...]
    ot = acc[:HEAD_DIM, :] / acc[HEAD_DIM:HEAD_DIM + 1, :]
    ot = jnp.concatenate([ot, jnp.zeros((128 - HEAD_DIM, Q_GROUP * tq), F32)], axis=0)
    o = ot.T
    o_ref[...] = jnp.concatenate([o[g * tq:(g + 1) * tq, :HEAD_DIM] for g in range(Q_GROUP)],
                                 axis=1).astype(BF16)


def _attention(qt, k2, vt, B, S):
    T = B * S
    tq = ATT_TQ
    nq = S // tq
    gw = Q_GROUP * HEAD_DIM
    cols = Q_GROUP * tq
    tk = min(ATT_TK, S)
    return pl.pallas_call(
        _attn_kernel, grid=(B, N_KV_HEADS, nq),
        in_specs=[pl.BlockSpec((None, gw, tq), lambda b, h, i: (b, h, i)),
                  pl.BlockSpec((None, None, S, 128), lambda b, h, i: (h, b, 0, 0)),
                  pl.BlockSpec((None, 128, S), lambda b, h, i: (b, h, 0))],
        out_specs=pl.BlockSpec((tq, gw), lambda b, h, i: (b * nq + i, h)),
        out_shape=jax.ShapeDtypeStruct((T, Q_WIDTH), BF16),
        scratch_shapes=[pltpu.VMEM((128, cols), BF16), pltpu.VMEM((8, 128), F32),
                        pltpu.VMEM((ATT_V_ROWS, cols), F32), pltpu.VMEM((1, cols), F32),
                        pltpu.VMEM((tk, cols), F32), pltpu.VMEM((tk, cols), F32),
                        pltpu.VMEM((tk, cols), BF16), pltpu.VMEM((tk, cols), BF16)],
        compiler_params=_cparams("parallel", "parallel", "arbitrary"),
        name="attention")(qt, k2.reshape(N_KV_HEADS, B, S, 128), vt)


def _even_out_kernel(a_ref, t_ref, wa_ref, wt_ref, x_ref, g_ref, b_ref, o_ref):
    mix = _dot(a_ref[...], wa_ref[...]) + _dot(t_ref[...], wt_ref[...])
    o_ref[...] = _ln(DEEPNORM_ALPHA * x_ref[...] + mix, g_ref[...], b_ref[...])


def _even_out(a_out, attn, w_out, x, g, b):
    T, D = x.shape
    tm = ROW_TILE
    row = lambda w: pl.BlockSpec((tm, w), lambda i: (i, 0))
    full = lambda a: pl.BlockSpec(a.shape, lambda i: (0,) * a.ndim)
    wa = w_out[:F_WIDTH].astype(BF16)
    wt = w_out[F_WIDTH:].astype(BF16)
    g = g.reshape(1, D)
    b = b.reshape(1, D)
    return pl.pallas_call(
        _even_out_kernel, grid=(T // tm,),
        in_specs=[row(F_WIDTH), row(Q_WIDTH), full(wa), full(wt), row(D), full(g), full(b)],
        out_specs=row(D), out_shape=jax.ShapeDtypeStruct((T, D), F32),
        compiler_params=_cparams("parallel"), name="even_out")(a_out, attn, wa, wt, x, g, b)


def _odd_kernel(x_ref, wi_ref, bi_ref, vg_ref, vb_ref, ws_ref, bs_ref, wo_ref, g_ref, b_ref, o_ref,
                gate_ref):
    tm = x_ref.shape[0]
    x = x_ref[...]
    h = _dot(x.astype(BF16), wi_ref[...]) + bi_ref[...]
    h = 0.5 * h * (1.0 + lax.erf(h * (1.0 / math.sqrt(2.0))))
    u = h[:, :C_WIDTH]
    v = _ln(h[:, C_WIDTH:], vg_ref[...], vb_ref[...]).astype(BF16)
    for c in range(tm // CHUNK):
        r0 = c * CHUNK
        for gi in range(N_CGROUPS):
            l0 = gi * CGROUP_DIM
            sv = _dot(ws_ref[gi], v[r0:r0 + CHUNK, l0:l0 + CGROUP_DIM]) + bs_ref[gi]
            gate_ref[r0:r0 + CHUNK, l0:l0 + CGROUP_DIM] = (
                u[r0:r0 + CHUNK, l0:l0 + CGROUP_DIM] * sv).astype(BF16)
    mix = _dot(gate_ref[...], wo_ref[...])
    o_ref[...] = _ln(DEEPNORM_ALPHA * x + mix, g_ref[...], b_ref[...])


def _odd_mixer(x, w_in, b_in, v_g, v_b, w_s, b_s, w_out, g, b):
    T, D = x.shape
    tm = ROW_TILE
    row = pl.BlockSpec((tm, D), lambda i: (i, 0))
    full = lambda a: pl.BlockSpec(a.shape, lambda i: (0,) * a.ndim)
    args = [w_in.astype(BF16), b_in.reshape(1, 2 * C_WIDTH), v_g.reshape(1, C_WIDTH),
            v_b.reshape(1, C_WIDTH), w_s.astype(BF16),
            jnp.broadcast_to(b_s[:, :, None], (N_CGROUPS, CHUNK, CGROUP_DIM)).astype(F32),
            w_out.astype(BF16), g.reshape(1, D), b.reshape(1, D)]
    return pl.pallas_call(
        _odd_kernel, grid=(T // tm,),
        in_specs=[row] + [full(a) for a in args],
        out_specs=row, out_shape=jax.ShapeDtypeStruct((T, D), F32),
        scratch_shapes=[pltpu.VMEM((tm, C_WIDTH), BF16)],
        compiler_params=_cparams("parallel"), name="odd_mixer")(x, *args)


def _router_kernel(x_ref, w_ref, rb_ref, tri_ref, eidx_ref, wts_ref, pos_ref, cnt_ref, xp_ref, run_ref):
    tm = x_ref.shape[0]
    i = pl.program_id(0)

    @pl.when(i == 0)
    def _():
        run_ref[...] = jnp.zeros(run_ref.shape, F32)

    x = x_ref[...]
    xp_ref[...] = pltpu.pack_elementwise([x[:, :HALF], x[:, HALF:]], packed_dtype=BF16)
    xh = x.astype(BF16)
    xl = (x - xh.astype(F32)).astype(BF16)
    nt = (((1,), (1,)), ((), ()))
    dg = lambda a, c: lax.dot_general(a, c, nt, preferred_element_type=F32)
    logits = dg(w_ref[0], xh) + dg(w_ref[0], xl) + dg(w_ref[1], xh)
    scores = jax.nn.sigmoid(logits)
    sel = scores + rb_ref[...]

    i8 = lax.broadcasted_iota(I32, (GROUP_SIZE, tm), 0)
    gsc_rows = []
    for gidx in range(N_EXPERT_GROUPS):
        sg = sel[gidx * GROUP_SIZE:(gidx + 1) * GROUP_SIZE, :]
        m1 = jnp.max(sg, axis=0, keepdims=True)
        f1 = jnp.min(jnp.where(sg == m1, i8, GROUP_SIZE), axis=0, keepdims=True)
        m2 = jnp.max(jnp.where(i8 == f1, NEG_INF, sg), axis=0, keepdims=True)
        gsc_rows.append(m1 + m2)
    gsc = jnp.concatenate(gsc_rows, axis=0)

    gsel = jnp.zeros(gsc.shape, F32)
    for _ in range(TOPK_GROUPS):
        m = jnp.max(gsc, axis=0, keepdims=True)
        f = jnp.min(jnp.where(gsc == m, i8, N_EXPERT_GROUPS), axis=0, keepdims=True)
        pick = i8 == f
        gsel = jnp.where(pick, 1.0, gsel)
        gsc = jnp.where(pick, NEG_INF, gsc)
    esel = jnp.concatenate(
        [jnp.broadcast_to(gsel[gidx:gidx + 1, :], (GROUP_SIZE, tm)) for gidx in range(N_EXPERT_GROUPS)],
        axis=0)

    cur = jnp.where(esel > 0.0, sel, NEG_INF)
    ei = lax.broadcasted_iota(I32, cur.shape, 0)
    idx_rows, sc_rows = [], []
    chosen = jnp.zeros(cur.shape, F32)
    for _ in range(TOP_K):
        m = jnp.max(cur, axis=0, keepdims=True)
        f = jnp.min(jnp.where(cur == m, ei, N_EXPERTS), axis=0, keepdims=True)
        pick = ei == f
        idx_rows.append(f)
        sc_rows.append(jnp.sum(jnp.where(pick, scores, 0.0), axis=0, keepdims=True))
        chosen = jnp.where(pick, 1.0, chosen)
        cur = jnp.where(pick, NEG_INF, cur)
    eidx = jnp.concatenate(idx_rows, axis=0)
    sc = jnp.concatenate(sc_rows, axis=0)
    eidx_ref[...] = eidx
    wts_ref[...] = sc / jnp.sum(sc, axis=0, keepdims=True) * ROUTE_SCALE

    before = _dot(chosen.astype(BF16), tri_ref[...]) + run_ref[...]
    pos_rows = [jnp.sum(jnp.where(ei == idx_rows[k], before, 0.0), axis=0, keepdims=True)
                for k in range(TOP_K)]
    pos_ref[...] = jnp.concatenate(pos_rows, axis=0).astype(I32)
    run_new = run_ref[...] + jnp.sum(chosen, axis=1, keepdims=True)
    run_ref[...] = run_new
    cnt_ref[...] = jnp.broadcast_to(run_new, cnt_ref.shape).astype(I32)


def _router(x, router_w, router_b, first_row, T):
    D = x.shape[1]
    tm = min(ROUTER_TILE, T)
    n_tiles = T // tm
    first_tile = first_row // tm
    assert n_tiles * tm == T and first_tile * tm == first_row
    wt = router_w.T.astype(F32)
    wh = wt.astype(BF16)
    wl = (wt - wh.astype(F32)).astype(BF16)
    w2 = jnp.stack([wh, wl])
    rb = router_b.astype(F32).reshape(N_EXPERTS, 1)
    tri = jnp.asarray(np.triu(np.ones((tm, tm)), 1), BF16)
    full = lambda a: pl.BlockSpec(a.shape, lambda i: (0,) * a.ndim)
    col = pl.BlockSpec((TOP_K, tm), lambda i: (0, i))
    return pl.pallas_call(
        _router_kernel, grid=(n_tiles,),
        in_specs=[pl.BlockSpec((tm, D), lambda i: (i + first_tile, 0)), full(w2), full(rb), full(tri)],
        out_specs=[col, col, col, pl.BlockSpec((N_EXPERTS, 128), lambda i: (0, 0)),
                   pl.BlockSpec((tm, HALF), lambda i: (i, 0))],
        out_shape=[jax.ShapeDtypeStruct((TOP_K, T), I32), jax.ShapeDtypeStruct((TOP_K, T), F32),
                   jax.ShapeDtypeStruct((TOP_K, T), I32), jax.ShapeDtypeStruct((N_EXPERTS, 128), I32),
                   jax.ShapeDtypeStruct((T, HALF), I32)],
        scratch_shapes=[pltpu.VMEM((N_EXPERTS, 1), F32)],
        compiler_params=_cparams("arbitrary"), name="router")(x, w2, rb, tri)


def _dest_kernel(start_ref, eidx_ref, pos_ref, o_ref):
    e = eidx_ref[...]
    acc = pos_ref[...]
    for j in range(N_EXPERTS):
        acc = acc + jnp.where(e == j, start_ref[j], 0)
    o_ref[...] = acc


def _dest_rows(eidx, pos, seg_start):
    K, T = eidx.shape
    tl = min(T, 2048)
    blk = pl.BlockSpec((K, tl), lambda i, s: (0, i))
    grid_spec = pltpu.PrefetchScalarGridSpec(
        num_scalar_prefetch=1, grid=(T // tl,), in_specs=[blk, blk], out_specs=blk)
    return pl.pallas_call(
        _dest_kernel, grid_spec=grid_spec, out_shape=jax.ShapeDtypeStruct((K, T), I32),
        compiler_params=_cparams("parallel"), name="dest_rows")(seg_start, eidx, pos)


def _gather_rows(table, idx):
    n_rows = idx.shape[0]
    width = table.shape[1]
    per_worker = n_rows // SC_WORKERS
    n_chunks = per_worker // SC_CHUNK
    assert per_worker * SC_WORKERS == n_rows and n_chunks * SC_CHUNK == per_worker
    mesh = plsc.VectorSubcoreMesh(core_axis_name="c", subcore_axis_name="s")

    @functools.partial(
        pl.kernel, mesh=mesh,
        out_type=jax.ShapeDtypeStruct((n_rows, width), table.dtype),
        scratch_types=[pltpu.VMEM((SC_CHUNK,), I32), pltpu.VMEM((SC_CHUNK, width), table.dtype),
                       pltpu.SemaphoreType.DMA])
    def gather(table_hbm, idx_hbm, out_hbm, idx_v, rows_v, sem):
        wid = lax.axis_index("s") * SC_CORES + lax.axis_index("c")
        base = wid * per_worker

        @pl.loop(0, n_chunks)
        def _(j):
            off = base + j * SC_CHUNK
            pltpu.sync_copy(idx_hbm.at[pl.ds(off, SC_CHUNK)], idx_v)
            pltpu.async_copy(table_hbm.at[idx_v], rows_v, sem).wait()
            pltpu.sync_copy(rows_v, out_hbm.at[pl.ds(off, SC_CHUNK)])

    return gather(table, idx)


def _scatter_rows(rows, dest):
    n_tok, width = rows.shape
    n_dst = dest.shape[0]
    per_worker = n_tok // SC_WORKERS
    n_chunks = per_worker // SC_CHUNK
    assert per_worker * SC_WORKERS == n_tok and n_chunks * SC_CHUNK == per_worker
    mesh = plsc.VectorSubcoreMesh(core_axis_name="c", subcore_axis_name="s")

    @functools.partial(
        pl.kernel, mesh=mesh,
        out_type=jax.ShapeDtypeStruct((n_dst * n_tok, width), rows.dtype),
        scratch_types=[pltpu.VMEM((n_dst, SC_CHUNK), I32), pltpu.VMEM((SC_CHUNK, width), rows.dtype),
                       pltpu.SemaphoreType.DMA])
    def scatter(rows_hbm, dest_hbm, out_hbm, idx_v, rows_v, sem):
        wid = lax.axis_index("s") * SC_CORES + lax.axis_index("c")
        base = wid * per_worker

        @pl.loop(0, n_chunks)
        def _(j):
            off = base + j * SC_CHUNK
            pltpu.sync_copy(dest_hbm.at[:, pl.ds(off, SC_CHUNK)], idx_v)
            pltpu.sync_copy(rows_hbm.at[pl.ds(off, SC_CHUNK)], rows_v)
            copies = [pltpu.async_copy(rows_v, out_hbm.at[idx_v.at[k]], sem) for k in range(n_dst)]
            for c in copies:
                c.wait()

    return scatter(rows, dest)


def _expert_kernel(blk_ref, exp_ref, lo_ref, hi_ref, slot_ref, nxt_ref, xs_hbm, wg_hbm, wu_hbm, wd_hbm,
                   ys_hbm, wg_buf, wu_buf, wd_buf, wgu_s, wd_s, sem, xs_buf, xs_sem, ys_buf, ys_sem,
                   *, layer):
    i = pl.program_id(0)
    prev = jnp.maximum(i - 1, 0)

    def weight_copies(expert, slot):
        return [pltpu.make_async_copy(src.at[layer, expert], dst.at[slot], sem.at[slot])
                for src, dst in ((wg_hbm, wg_buf), (wu_hbm, wu_buf), (wd_hbm, wd_buf))]

    @pl.when(i == 0)
    def _():
        for c in weight_copies(exp_ref[0], slot_ref[0]):
            c.start()

    @pl.when(jnp.logical_or(i == 0, exp_ref[i] != exp_ref[prev]))
    def _():
        slot = slot_ref[i]
        for c in weight_copies(exp_ref[i], slot):
            c.wait()
        wgu_s[:, :EXPERT_DIM] = wg_buf[slot].astype(BF16)
        wgu_s[:, EXPERT_DIM:] = wu_buf[slot].astype(BF16)
        wd_s[...] = wd_buf[slot].astype(BF16)

        @pl.when(nxt_ref[i] >= 0)
        def _():
            for c in weight_copies(nxt_ref[i], 1 - slot):
                c.start()

    n_blocks = xs_hbm.shape[0] // EXPERT_ROWS
    blk = blk_ref[i]
    first = jnp.logical_or(i == 0, blk != blk_ref[prev])

    def rows_copy(block):
        slot = block % EXPERT_XS_SLOTS
        src = xs_hbm.at[pl.ds(pl.multiple_of(block * EXPERT_ROWS, EXPERT_ROWS), EXPERT_ROWS)]
        return pltpu.make_async_copy(src, xs_buf.at[slot], xs_sem.at[slot])

    @pl.when(i == 0)
    def _():
        for b0 in range(min(EXPERT_XS_SLOTS - 1, n_blocks)):
            rows_copy(b0).start()

    def result_copy(block):
        slot = block % EXPERT_XS_SLOTS
        dst = ys_hbm.at[pl.ds(pl.multiple_of(block * EXPERT_ROWS, EXPERT_ROWS), EXPERT_ROWS)]
        return pltpu.make_async_copy(ys_buf.at[slot], dst, ys_sem.at[slot])

    @pl.when(first)
    def _():
        rows_copy(blk).wait()

        @pl.when(blk + EXPERT_XS_SLOTS - 1 < n_blocks)
        def _():
            rows_copy(blk + EXPERT_XS_SLOTS - 1).start()

        @pl.when(blk >= EXPERT_XS_SLOTS)
        def _():
            result_copy(blk - EXPERT_XS_SLOTS).wait()

    xs_ref = xs_buf.at[blk % EXPERT_XS_SLOTS]
    ys_ref = ys_buf.at[blk % EXPERT_XS_SLOTS]
    lo = lo_ref[i]
    hi = hi_ref[i]

    def sub_block(r0):
        rows = slice(r0, r0 + EXPERT_SUB_ROWS)
        w = xs_ref[rows, :]
        xlo = lax.bitcast_convert_type(w.astype(jnp.int16), BF16)
        xhi = lax.bitcast_convert_type(lax.shift_right_logical(w, 16).astype(jnp.int16), BF16)
        gu = _dot(xlo, wgu_s[:HALF, :]) + _dot(xhi, wgu_s[HALF:, :])
        g = gu[:, :EXPERT_DIM]
        hb = (g * jax.nn.sigmoid(g) * gu[:, EXPERT_DIM:]).astype(BF16)
        y = _dot(hb, wd_s[...])
        packed = pltpu.pack_elementwise([y[:, :HALF], y[:, HALF:]], packed_dtype=BF16)
        row = r0 + lax.broadcasted_iota(I32, (EXPERT_SUB_ROWS, 1), 0)
        mine = jnp.logical_and(row >= lo, row < hi)
        kept = jnp.where(first, 0, ys_ref[rows, :])
        ys_ref[rows, :] = jnp.where(mine, packed, kept)

    def touched(r0):
        return jnp.logical_and(lo < r0 + EXPERT_SUB_ROWS, hi > r0)

    for r0 in range(0, EXPERT_ROWS, 2 * EXPERT_SUB_ROWS):
        r1 = r0 + EXPERT_SUB_ROWS
        t0, t1 = touched(r0), touched(r1)

        @pl.when(jnp.logical_and(t0, t1))
        def _():
            sub_block(r0)
            sub_block(r1)

        pl.when(jnp.logical_and(t0, jnp.logical_not(t1)))(functools.partial(sub_block, r0))
        pl.when(jnp.logical_and(jnp.logical_not(t0), t1))(functools.partial(sub_block, r1))

    n_items = pl.num_programs(0)
    final = i == n_items - 1
    block_done = jnp.logical_or(final, blk_ref[jnp.minimum(i + 1, n_items - 1)] != blk)

    @pl.when(block_done)
    def _():
        result_copy(blk).start()

    @pl.when(final)
    def _():
        for b0 in range(max(n_blocks - EXPERT_XS_SLOTS, 0), n_blocks):
            result_copy(b0).wait()


def _expert_items(counts, n_rows):
    n_blocks = n_rows // EXPERT_ROWS
    n_items = n_blocks + N_EXPERTS - 1
    end = jnp.cumsum(counts)
    start = end - counts
    first_blk = start // EXPERT_ROWS
    n_blk = jnp.where(counts > 0, (end - 1) // EXPERT_ROWS - first_blk + 1, 0)
    item_end = jnp.cumsum(n_blk)
    item_start = item_end - n_blk
    slot = jnp.arange(n_items, dtype=I32)
    e = jnp.minimum(jnp.sum((item_end[None, :] <= slot[:, None]).astype(I32), axis=1), N_EXPERTS - 1)
    onehot = (e[:, None] == jnp.arange(N_EXPERTS, dtype=I32)[None, :]).astype(I32)
    pick = lambda v: jnp.sum(onehot * v[None, :], axis=1)
    valid = slot < item_end[-1]
    blk = jnp.where(valid, pick(first_blk) + slot - pick(item_start), n_blocks - 1)
    lo = jnp.clip(pick(start) - blk * EXPERT_ROWS, 0, EXPERT_ROWS)
    hi = jnp.clip(pick(end) - blk * EXPERT_ROWS, 0, EXPERT_ROWS)
    last_e = jnp.max(jnp.where(counts > 0, jnp.arange(N_EXPERTS, dtype=I32), 0))
    e = jnp.where(valid, e, last_e)
    hi = jnp.where(valid, hi, 0)
    lo = jnp.where(valid, lo, 0)
    change = jnp.concatenate([jnp.ones((1,), I32), (e[1:] != e[:-1]).astype(I32)])
    slot = (jnp.cumsum(change) - 1) % 2
    later = jnp.where(jnp.arange(N_EXPERTS, dtype=I32)[None, :] > e[:, None], counts[None, :] > 0, False)
    nxt = jnp.where(jnp.any(later, axis=1), jnp.argmax(later, axis=1), -1)
    return tuple(a.astype(I32) for a in (blk, e, lo, hi, slot, nxt))


def _experts(xs, items, w_gate, w_up, w_down, layer):
    n_rows = xs.shape[0]
    n_items = items[0].shape[0]
    hbm = pl.BlockSpec(memory_space=pl.ANY)
    grid_spec = pltpu.PrefetchScalarGridSpec(
        num_scalar_prefetch=len(items), grid=(n_items,),
        in_specs=[hbm, hbm, hbm, hbm],
        out_specs=hbm,
        scratch_shapes=[pltpu.VMEM((2, D_MODEL, EXPERT_DIM), F32),
                        pltpu.VMEM((2, D_MODEL, EXPERT_DIM), F32),
                        pltpu.VMEM((2, EXPERT_DIM, D_MODEL), F32),
                        pltpu.VMEM((D_MODEL, 2 * EXPERT_DIM), BF16),
                        pltpu.VMEM((EXPERT_DIM, D_MODEL), BF16),
                        pltpu.SemaphoreType.DMA((2,)),
                        pltpu.VMEM((EXPERT_XS_SLOTS, EXPERT_ROWS, HALF), I32),
                        pltpu.SemaphoreType.DMA((EXPERT_XS_SLOTS,)),
                        pltpu.VMEM((EXPERT_XS_SLOTS, EXPERT_ROWS, HALF), I32),
                        pltpu.SemaphoreType.DMA((EXPERT_XS_SLOTS,))])
    return pl.pallas_call(
        functools.partial(_expert_kernel, layer=layer), grid_spec=grid_spec,
        out_shape=jax.ShapeDtypeStruct((n_rows, HALF), I32),
        compiler_params=_cparams("arbitrary"), name="experts")(*items, xs, w_gate, w_up, w_down)


def _moe_out_kernel(x_ref, yg_ref, wt_ref, sgu_ref, sd_ref, g_ref, b_ref, *rest):
    o_ref = rest[-1]
    x = x_ref[...]
    wt = wt_ref[...]
    lo = jnp.zeros((x.shape[0], HALF), F32)
    hi = jnp.zeros((x.shape[0], HALF), F32)
    for k in range(TOP_K):
        w = yg_ref[k]
        wk = wt[:, k:k + 1]
        lo = lo + wk * _unpack_lo(w)
        hi = hi + wk * _unpack_hi(w)
    gu = _dot(x.astype(BF16), sgu_ref[...])
    g = gu[:, :EXPERT_DIM]
    hs = (g * jax.nn.sigmoid(g) * gu[:, EXPERT_DIM:]).astype(BF16)
    ffn = jnp.concatenate([lo, hi], axis=1) + _dot(hs, sd_ref[...])
    o_ref[...] = _ln(DEEPNORM_ALPHA * x + ffn, g_ref[...], b_ref[...])


def _moe_out(x, yg, wts, sh_gate, sh_up, sh_down, g, b, first_tile, partial_out):
    T, D = x.shape
    tm = ROW_TILE
    n_tiles = yg.shape[1] // tm
    xrow = pl.BlockSpec((tm, D), lambda i: (i + first_tile, 0))
    full = lambda a: pl.BlockSpec(a.shape, lambda i: (0,) * a.ndim)
    sgu = jnp.concatenate([sh_gate, sh_up], axis=1).astype(BF16)
    sd = sh_down.astype(BF16)
    g = g.reshape(1, D)
    b = b.reshape(1, D)
    args = [x, yg, wts, sgu, sd, g, b]
    in_specs = [xrow, pl.BlockSpec((TOP_K, tm, HALF), lambda i: (0, i, 0)),
                pl.BlockSpec((tm, TOP_K), lambda i: (i, 0)), full(sgu), full(sd), full(g), full(b)]
    aliases = {}
    if partial_out is not None:
        args.append(partial_out)
        in_specs.append(pl.BlockSpec(memory_space=pl.ANY))
        aliases = {len(args) - 1: 0}
    return pl.pallas_call(
        _moe_out_kernel, grid=(n_tiles,), in_specs=in_specs,
        out_specs=xrow, out_shape=jax.ShapeDtypeStruct((T, D), F32),
        input_output_aliases=aliases,
        compiler_params=_cparams("parallel"), name="moe_out")(*args)


def _moe(x, router_w, router_b, w_gate, w_up, w_down, layer, sh_gate, sh_up, sh_down, g, b):
    T = x.shape[0]
    tiles = T // ROW_TILE // MOE_TOKEN_GROUPS
    tg = tiles * ROW_TILE
    out = None
    for grp in range(MOE_TOKEN_GROUPS):
        eidx, wts, pos, cnt, xp = _router(x, router_w, router_b, grp * tg, tg)
        counts = cnt[:, 0]
        seg_start = (jnp.cumsum(counts) - counts).astype(I32)
        dest = _dest_rows(eidx, pos, seg_start)
        xs = _scatter_rows(xp, dest)
        ys = _experts(xs, _expert_items(counts, tg * TOP_K), w_gate, w_up, w_down, layer)
        yg = _gather_rows(ys, dest.reshape(tg * TOP_K)).reshape(TOP_K, tg, HALF)
        out = _moe_out(x, yg, wts.T, sh_gate, sh_up, sh_down, g, b, grp * tiles, out)
    return out


def kernel(x, ln_in_g, ln_in_b, e_w_in, e_w_fourier, e_q_gain, e_k_gain, e_w_out, o_w_in, o_b_in, o_v_ln_g, o_v_ln_b, o_w_spatial, o_b_spatial, o_w_out, ln_mix_g, ln_mix_b, ln_ffn_g, ln_ffn_b, router_w, router_b, exp_w_gate, exp_w_up, exp_w_down, sh_w_gate, sh_w_up, sh_w_down):
    B, S, D = x.shape
    T = B * S
    h = _layer_norm(x.reshape(T, D), ln_in_g, ln_in_b)
    for i in range(DEPTH):
        j = i // 2
        if i % 2 == 0:
            a, qt, k2, vt = _even_in(h, e_w_in[j], e_q_gain[j], e_k_gain[j], B, S)
            a_out = _fourier(a, e_w_fourier[j], B, S)
            attn = _attention(qt, k2, vt, B, S)
            h = _even_out(a_out, attn, e_w_out[j], h, ln_mix_g[i], ln_mix_b[i])
        else:
            h = _odd_mixer(h, o_w_in[j], o_b_in[j], o_v_ln_g[j], o_v_ln_b[j], o_w_spatial[j],
                           o_b_spatial[j], o_w_out[j], ln_mix_g[i], ln_mix_b[i])
        h = _moe(h, router_w[i], router_b[i], exp_w_gate, exp_w_up, exp_w_down, i,
                 sh_w_gate[i], sh_w_up[i], sh_w_down[i], ln_ffn_g[i], ln_ffn_b[i])
    return h.reshape(B, S, D)
```

```python
import functools
import math

import numpy as np
import jax
import jax.numpy as jnp
from jax import lax
from jax.experimental import pallas as pl
from jax.experimental.pallas import tpu as pltpu
from jax.experimental.pallas import tpu_sc as plsc

F32 = jnp.float32
BF16 = jnp.bfloat16
I32 = jnp.int32

D_MODEL = 1024
DEPTH = 4
GRID_W = 64
N_FGROUPS = 4
FGROUP_DIM = 128
F_WIDTH = N_FGROUPS * FGROUP_DIM
N_HEADS = 8
N_KV_HEADS = 2
HEAD_DIM = 64
Q_GROUP = N_HEADS // N_KV_HEADS
Q_WIDTH = N_HEADS * HEAD_DIM
KV_WIDTH = N_KV_HEADS * HEAD_DIM
ROPE_THETA = 10000.0
ROPE_PAIRS = HEAD_DIM // 4
CHUNK = 128
N_CGROUPS = 8
CGROUP_DIM = D_MODEL // N_CGROUPS
C_WIDTH = N_CGROUPS * CGROUP_DIM
N_EXPERTS = 64
EXPERT_DIM = 256
TOP_K = 8
N_EXPERT_GROUPS = 8
GROUP_SIZE = N_EXPERTS // N_EXPERT_GROUPS
TOPK_GROUPS = 4
ROUTE_SCALE = 2.5
LN_EPS = 1e-5
QK_EPS = 1e-6
DEEPNORM_ALPHA = (2 * DEPTH) ** 0.25

LANES = 128
SUBLANES = 8
VMEM_LIMIT_BYTES = 56 * 1024 * 1024
ROW_TILE = 512
DFT_N1 = 64
DFT_KRON = 4
DFT_PITCH_PAD = 8
ROUTER_TILE = 1024
EXPERT_ROWS = 2048
EXPERT_XS_SLOTS = 3
EXPERT_SUB_ROWS = 512
HALF = D_MODEL // 2
SC_CORES = 2
SC_SUBCORES = 16
SC_WORKERS = SC_CORES * SC_SUBCORES
SC_CHUNK = 128
MOE_TOKEN_GROUPS = 2
ATT_TQ = 256
ATT_TK = 256
ATT_V_ROWS = -(-(HEAD_DIM + 1) // 16) * 16
ATT_BOUND_SLACK = 1.0 + 2.0 ** -7
ATT_MIN_ROW_SUM = 2.0 ** -80
NEG_INF = float("-inf")


def _cparams(*sem):
    return pltpu.CompilerParams(dimension_semantics=sem, vmem_limit_bytes=VMEM_LIMIT_BYTES)


def _ln(x, g, b):
    mu = jnp.mean(x, axis=-1, keepdims=True)
    xc = x - mu
    var = jnp.mean(xc * xc, axis=-1, keepdims=True)
    return xc * lax.rsqrt(var + LN_EPS) * g + b


def _dot(a, b):
    return jnp.dot(a, b, preferred_element_type=F32)


def _unpack_lo(w):
    return lax.bitcast_convert_type(lax.shift_left(w, 16), F32)


def _unpack_hi(w):
    return lax.bitcast_convert_type(w & jnp.int32(-65536), F32)


def _ln_kernel(x_ref, g_ref, b_ref, o_ref):
    o_ref[...] = _ln(x_ref[...], g_ref[...], b_ref[...])


def _layer_norm(x, g, b):
    T, D = x.shape
    row = pl.BlockSpec((ROW_TILE, D), lambda i: (i, 0))
    vec = pl.BlockSpec((1, D), lambda i: (0, 0))
    return pl.pallas_call(
        _ln_kernel, grid=(T // ROW_TILE,), in_specs=[row, vec, vec], out_specs=row,
        out_shape=jax.ShapeDtypeStruct((T, D), F32), compiler_params=_cparams("parallel"),
        name="ln_in")(x, g.reshape(1, D), b.reshape(1, D))


def _even_in_kernel(x_ref, w_ref, qm_ref, km_ref, qg_ref, kg_ref, cos_ref, sin_ref,
                    a_ref, qt_ref, k_ref, vt_ref):
    tm = x_ref.shape[0]
    h = _dot(x_ref[...].astype(BF16), w_ref[...])
    a_ref[...] = h[:, :F_WIDTH].astype(BF16)
    q = h[:, F_WIDTH:F_WIDTH + Q_WIDTH]
    k = h[:, F_WIDTH + Q_WIDTH:F_WIDTH + Q_WIDTH + KV_WIDTH]
    v = h[:, F_WIDTH + Q_WIDTH + KV_WIDTH:]
    cos = cos_ref[...]
    sin = sin_ref[...]
    lane = lax.broadcasted_iota(I32, (tm, LANES), 1)
    first_of_pair = (lane & ROPE_PAIRS) == 0

    def mean_sq(xf, m_ref):
        sq = xf * xf
        hi = sq.astype(BF16)
        lo = (sq - hi.astype(F32)).astype(BF16)
        return _dot(hi, m_ref[...]) + _dot(lo, m_ref[...])

    def rope(xn):
        sw = jnp.where(first_of_pair, pltpu.roll(xn, LANES - ROPE_PAIRS, 1), pltpu.roll(xn, ROPE_PAIRS, 1))
        return xn * cos + sw * sin

    qn = q * lax.rsqrt(mean_sq(q, qm_ref) + QK_EPS) * qg_ref[...]
    scale = math.log2(math.e) / math.sqrt(HEAD_DIM)
    for c in range(Q_WIDTH // LANES):
        qt_ref[c * LANES:(c + 1) * LANES, :] = (rope(qn[:, c * LANES:(c + 1) * LANES]) * scale).T.astype(BF16)
    kn = rope(k * lax.rsqrt(mean_sq(k, km_ref) + QK_EPS) * kg_ref[...])
    low = lane < HEAD_DIM
    k_ref[0] = jnp.where(low, kn, 0.0).astype(BF16)
    k_ref[1] = jnp.where(low, pltpu.roll(kn, HEAD_DIM, 1), 0.0).astype(BF16)
    ones_col = jnp.where(lane == HEAD_DIM, 1.0, 0.0)
    vt_ref[0:LANES, :] = jnp.where(low, v, ones_col).T.astype(BF16)
    vt_ref[LANES:2 * LANES, :] = jnp.where(low, pltpu.roll(v, HEAD_DIM, 1), ones_col).T.astype(BF16)


def _rope_tables(S):
    t = np.arange(S)
    inv = ROPE_THETA ** (-np.arange(ROPE_PAIRS, dtype=np.float64) / ROPE_PAIRS)
    ang_r = (t // GRID_W)[:, None] * inv
    ang_c = (t % GRID_W)[:, None] * inv
    cos = np.concatenate([np.cos(ang_r), np.cos(ang_r), np.cos(ang_c), np.cos(ang_c)], axis=1)
    sin = np.concatenate([-np.sin(ang_r), np.sin(ang_r), -np.sin(ang_c), np.sin(ang_c)], axis=1)
    return (jnp.asarray(np.tile(cos, (1, 2)), F32), jnp.asarray(np.tile(sin, (1, 2)), F32))


def _head_mean_matrix(width):
    m = np.kron(np.eye(width // HEAD_DIM), np.full((HEAD_DIM, HEAD_DIM), 1.0 / HEAD_DIM))
    return jnp.asarray(m, BF16)


def _even_in(x, w_in, q_gain, k_gain, B, S):
    T, D = x.shape
    tm = ROW_TILE
    ns = S // tm
    cos, sin = _rope_tables(S)
    row = lambda w: pl.BlockSpec((tm, w), lambda i: (i, 0))
    full = lambda a: pl.BlockSpec(a.shape, lambda i: (0,) * a.ndim)
    tab = pl.BlockSpec((tm, LANES), lambda i: (i % ns, 0))
    w = w_in.astype(BF16)
    qm = _head_mean_matrix(Q_WIDTH)
    km = _head_mean_matrix(KV_WIDTH)
    qg = jnp.tile(q_gain.astype(F32), N_HEADS).reshape(1, Q_WIDTH)
    kg = jnp.tile(k_gain.astype(F32), N_KV_HEADS).reshape(1, KV_WIDTH)
    return pl.pallas_call(
        _even_in_kernel, grid=(T // tm,),
        in_specs=[row(D), full(w), full(qm), full(km), full(qg), full(kg), tab, tab],
        out_specs=[row(F_WIDTH),
                   pl.BlockSpec((None, Q_WIDTH, tm), lambda i: (i // ns, 0, i % ns)),
                   pl.BlockSpec((N_KV_HEADS, tm, LANES), lambda i: (0, i, 0)),
                   pl.BlockSpec((None, N_KV_HEADS * LANES, tm), lambda i: (i // ns, 0, i % ns))],
        out_shape=[jax.ShapeDtypeStruct((T, F_WIDTH), BF16),
                   jax.ShapeDtypeStruct((B, Q_WIDTH, S), BF16),
                   jax.ShapeDtypeStruct((N_KV_HEADS, T, LANES), BF16),
                   jax.ShapeDtypeStruct((B, N_KV_HEADS * LANES, S), BF16)],
        compiler_params=_cparams("parallel"), name="even_in")(x, w, qm, km, qg, kg, cos, sin)


def _fourier_kernel(a_ref, dftc_ref, taba_ref, kc_ref, ks_ref, wf_ref, o_ref,
                    zr_ref, zi_ref, ur_ref, ui_ref, y_ref):
    S = a_ref.shape[0]
    n1_count = DFT_N1
    n2_count = S // DFT_N1
    pz = n1_count + DFT_PITCH_PAD
    pu = n2_count + DFT_PITCH_PAD
    blk = DFT_KRON * DFT_N1
    scale = 1.0 / math.sqrt(S * FGROUP_DIM)

    def channel_dft(j, carry):
        zz = _dot(a_ref[pl.ds(pl.multiple_of(j * blk, blk), blk), :], dftc_ref[...])
        for q in range(DFT_KRON):
            dst = pl.ds(pl.multiple_of((j * DFT_KRON + q) * pz, 8), n1_count)
            zr_ref[dst, :] = zz[q * n1_count:(q + 1) * n1_count, :FGROUP_DIM]
            zi_ref[dst, :] = zz[q * n1_count:(q + 1) * n1_count, FGROUP_DIM:]
        return carry

    lax.fori_loop(0, S // blk, channel_dft, 0, unroll=4)

    def stage_a(n1, carry):
        src = pl.ds(n1, n2_count, stride=pz)
        zn = jnp.concatenate([zr_ref[src, :], zi_ref[src, :]], axis=1).astype(BF16)
        r = _dot(taba_ref[n1], zn)
        dst = pl.ds(pl.multiple_of(n1 * pu, 8), n2_count)
        ur_ref[dst, :] = r[:n2_count, :FGROUP_DIM] + r[n2_count:, FGROUP_DIM:]
        ui_ref[dst, :] = r[:n2_count, FGROUP_DIM:] - r[n2_count:, :FGROUP_DIM]
        return carry

    lax.fori_loop(0, n1_count, stage_a, 0, unroll=8)

    def stage_b(j, carry):
        srcs = [pl.ds(j * DFT_KRON + q, n1_count, stride=pu) for q in range(DFT_KRON)]
        ur = jnp.concatenate([ur_ref[s, :] for s in srcs], axis=0).astype(BF16)
        ui = jnp.concatenate([ui_ref[s, :] for s in srcs], axis=0).astype(BF16)
        re = _dot(kc_ref[...], ur) + _dot(ks_ref[...], ui)
        out = _dot((re * scale).astype(BF16), wf_ref[...])
        for q in range(DFT_KRON):
            y_ref[srcs[q], :] = out[q * n1_count:(q + 1) * n1_count]
        return carry

    lax.fori_loop(0, S // blk, stage_b, 0, unroll=8)

    def compact(k1, carry):
        o_ref[pl.ds(pl.multiple_of(k1 * n2_count, n2_count), n2_count), :] = (
            y_ref[pl.ds(pl.multiple_of(k1 * pu, 8), n2_count), :].astype(BF16))
        return carry

    lax.fori_loop(0, n1_count, compact, 0)


def _dft_tables(S):
    n1c, n2c = DFT_N1, S // DFT_N1
    c = np.arange(FGROUP_DIM)
    ang = 2 * np.pi * np.outer(c, c) / FGROUP_DIM
    dftc = np.concatenate([np.cos(ang), -np.sin(ang)], axis=1)
    n1 = np.arange(n1c)[:, None, None]
    k2 = np.arange(n2c)[None, :, None]
    n2 = np.arange(n2c)[None, None, :]
    th = 2 * np.pi * (n2 * k2 / n2c + n1 * k2 / S)
    taba = np.concatenate([np.cos(th), np.sin(th)], axis=1)
    k1 = np.arange(n1c)
    g = 2 * np.pi * np.outer(k1, k1) / n1c
    eye = np.eye(DFT_KRON)
    kc = np.kron(eye, np.cos(g))
    ks = np.kron(eye, np.sin(g))
    return tuple(jnp.asarray(t, BF16) for t in (dftc, taba, kc, ks))


def _fourier(a, w_fourier, B, S):
    T = a.shape[0]
    dftc, taba, kc, ks = _dft_tables(S)
    full = lambda t: pl.BlockSpec(t.shape, lambda b, g: (0,) * t.ndim)
    blk = pl.BlockSpec((S, FGROUP_DIM), lambda b, g: (b, g))
    return pl.pallas_call(
        _fourier_kernel, grid=(B, N_FGROUPS),
        in_specs=[blk, full(dftc), full(taba), full(kc), full(ks),
                  pl.BlockSpec((None, FGROUP_DIM, FGROUP_DIM), lambda b, g: (g, 0, 0))],
        out_specs=blk,
        out_shape=jax.ShapeDtypeStruct((T, F_WIDTH), BF16),
        scratch_shapes=(
            [pltpu.VMEM((S // DFT_N1 * (DFT_N1 + DFT_PITCH_PAD), FGROUP_DIM), F32)] * 2
            + [pltpu.VMEM((DFT_N1 * (S // DFT_N1 + DFT_PITCH_PAD), FGROUP_DIM), F32)] * 3),
        compiler_params=_cparams("parallel", "parallel"), name="fourier")(
            a, dftc, taba, kc, ks, w_fourier.astype(BF16))


def _attn_kernel(qt_ref, k_ref, vt_ref, o_ref, qs_ref, kmax_ref, acc_ref, m_ref, s0_ref, s1_ref,
                 p0_ref, p1_ref):
    tq = qt_ref.shape[1]
    n_keys = k_ref.shape[0]
    tk = min(ATT_TK, n_keys)
    n_chunks = n_keys // tk
    assert n_chunks % 2 == 0 and n_chunks * tk == n_keys

    def keys(c):
        return k_ref[pl.ds(pl.multiple_of(c * tk, tk), tk), :]

    @pl.when(pl.program_id(2) == 0)
    def _():
        def body(c, best):
            k = keys(c).astype(F32)
            return jnp.maximum(best, jnp.sum(k * k, axis=1, keepdims=True))
        best = lax.fori_loop(0, n_chunks, body, jnp.zeros((tk, 1), F32))
        kmax_ref[...] = jnp.broadcast_to(jnp.sqrt(jnp.max(best, axis=0, keepdims=True)), kmax_ref.shape)

    qs_ref[HEAD_DIM:, :] = jnp.zeros((LANES - HEAD_DIM, Q_GROUP * tq), BF16)
    for g in range(Q_GROUP):
        qs_ref[:HEAD_DIM, g * tq:(g + 1) * tq] = qt_ref[g * HEAD_DIM:(g + 1) * HEAD_DIM, :]
    qf = qs_ref[...].astype(F32)
    bound = jnp.sqrt(jnp.sum(qf * qf, axis=0, keepdims=True)) * kmax_ref[0:1, 0:1] * ATT_BOUND_SLACK

    def scores(c):
        return _dot(keys(c), qs_ref[...])

    def values(c):
        return vt_ref[:ATT_V_ROWS, pl.ds(pl.multiple_of(c * tk, tk), tk)]

    def weights(s_buf):
        return jnp.exp2(s_buf[...] - bound).astype(BF16)

    def accumulate(p_buf, c):
        acc_ref[...] += _dot(values(c), p_buf[...])

    last = n_chunks - 1

    def fast(c2, carry):
        c = 2 * c2
        s0_ref[...] = scores(jnp.minimum(c + 2, last))
        p1_ref[...] = weights(s1_ref)
        accumulate(p0_ref, c)
        s1_ref[...] = scores(jnp.minimum(c + 3, last))
        p0_ref[...] = weights(s0_ref)
        accumulate(p1_ref, c + 1)
        return carry

    acc_ref[...] = jnp.zeros(acc_ref.shape, F32)
    s0_ref[...] = scores(0)
    p0_ref[...] = weights(s0_ref)
    s1_ref[...] = scores(1)
    lax.fori_loop(0, n_chunks // 2, fast, 0, unroll=True)
    underflow = jnp.min(acc_ref[HEAD_DIM:HEAD_DIM + 1, :]) < ATT_MIN_ROW_SUM

    @pl.when(underflow)
    def _():
        def safe(c, carry):
            s = scores(c)
            m_old = m_ref[...]
            m_new = jnp.maximum(m_old, jnp.max(s, axis=0, keepdims=True))
            acc_ref[...] = (jnp.exp2(m_old - m_new) * acc_ref[...]
                            + _dot(values(c), jnp.exp2(s - m_new).astype(BF16)))
            m_ref[...] = m_new
            return carry

        m_ref[...] = jnp.full(m_ref.shape, NEG_INF, F32)
        acc_ref[...] = jnp.zeros(acc_ref.shape, F32)
        lax.fori_loop(0, n_chunks, safe, 0)

    acc = acc_ref[...]
    ot = acc[:HEAD_DIM, :] / acc[HEAD_DIM:HEAD_DIM + 1, :]
    ot = jnp.concatenate([ot, jnp.zeros((LANES - HEAD_DIM, Q_GROUP * tq), F32)], axis=0)
    o = ot.T
    o_ref[...] = jnp.concatenate([o[g * tq:(g + 1) * tq, :HEAD_DIM] for g in range(Q_GROUP)],
                                 axis=1).astype(BF16)


def _attention(qt, k2, vt, B, S):
    T = B * S
    tq = ATT_TQ
    nq = S // tq
    gw = Q_GROUP * HEAD_DIM
    cols = Q_GROUP * tq
    tk = min(ATT_TK, S)
    return pl.pallas_call(
        _attn_kernel, grid=(B, N_KV_HEADS, nq),
        in_specs=[pl.BlockSpec((None, gw, tq), lambda b, h, i: (b, h, i)),
                  pl.BlockSpec((None, None, S, LANES), lambda b, h, i: (h, b, 0, 0)),
                  pl.BlockSpec((None, LANES, S), lambda b, h, i: (b, h, 0))],
        out_specs=pl.BlockSpec((tq, gw), lambda b, h, i: (b * nq + i, h)),
        out_shape=jax.ShapeDtypeStruct((T, Q_WIDTH), BF16),
        scratch_shapes=[pltpu.VMEM((LANES, cols), BF16), pltpu.VMEM((SUBLANES, LANES), F32),
                        pltpu.VMEM((ATT_V_ROWS, cols), F32), pltpu.VMEM((1, cols), F32),
                        pltpu.VMEM((tk, cols), F32), pltpu.VMEM((tk, cols), F32),
                        pltpu.VMEM((tk, cols), BF16), pltpu.VMEM((tk, cols), BF16)],
        compiler_params=_cparams("parallel", "parallel", "arbitrary"),
        name="attention")(qt, k2.reshape(N_KV_HEADS, B, S, LANES), vt)


def _even_out_kernel(a_ref, t_ref, wa_ref, wt_ref, x_ref, g_ref, b_ref, o_ref):
    mix = _dot(a_ref[...], wa_ref[...]) + _dot(t_ref[...], wt_ref[...])
    o_ref[...] = _ln(DEEPNORM_ALPHA * x_ref[...] + mix, g_ref[...], b_ref[...])


def _even_out(a_out, attn, w_out, x, g, b):
    T, D = x.shape
    tm = ROW_TILE
    row = lambda w: pl.BlockSpec((tm, w), lambda i: (i, 0))
    full = lambda a: pl.BlockSpec(a.shape, lambda i: (0,) * a.ndim)
    wa = w_out[:F_WIDTH].astype(BF16)
    wt = w_out[F_WIDTH:].astype(BF16)
    g = g.reshape(1, D)
    b = b.reshape(1, D)
    return pl.pallas_call(
        _even_out_kernel, grid=(T // tm,),
        in_specs=[row(F_WIDTH), row(Q_WIDTH), full(wa), full(wt), row(D), full(g), full(b)],
        out_specs=row(D), out_shape=jax.ShapeDtypeStruct((T, D), F32),
        compiler_params=_cparams("parallel"), name="even_out")(a_out, attn, wa, wt, x, g, b)


def _odd_kernel(x_ref, wi_ref, bi_ref, vg_ref, vb_ref, ws_ref, bs_ref, wo_ref, g_ref, b_ref, o_ref,
                gate_ref):
    tm = x_ref.shape[0]
    x = x_ref[...]
    h = _dot(x.astype(BF16), wi_ref[...]) + bi_ref[...]
    h = 0.5 * h * (1.0 + lax.erf(h * (1.0 / math.sqrt(2.0))))
    u = h[:, :C_WIDTH]
    v = _ln(h[:, C_WIDTH:], vg_ref[...], vb_ref[...]).astype(BF16)
    for c in range(tm // CHUNK):
        r0 = c * CHUNK
        for gi in range(N_CGROUPS):
            l0 = gi * CGROUP_DIM
            sv = _dot(ws_ref[gi], v[r0:r0 + CHUNK, l0:l0 + CGROUP_DIM]) + bs_ref[gi]
            gate_ref[r0:r0 + CHUNK, l0:l0 + CGROUP_DIM] = (
                u[r0:r0 + CHUNK, l0:l0 + CGROUP_DIM] * sv).astype(BF16)
    mix = _dot(gate_ref[...], wo_ref[...])
    o_ref[...] = _ln(DEEPNORM_ALPHA * x + mix, g_ref[...], b_ref[...])


def _odd_mixer(x, w_in, b_in, v_g, v_b, w_s, b_s, w_out, g, b):
    T, D = x.shape
    tm = ROW_TILE
    row = pl.BlockSpec((tm, D), lambda i: (i, 0))
    full = lambda a: pl.BlockSpec(a.shape, lambda i: (0,) * a.ndim)
    args = [w_in.astype(BF16), b_in.reshape(1, 2 * C_WIDTH), v_g.reshape(1, C_WIDTH),
            v_b.reshape(1, C_WIDTH), w_s.astype(BF16),
            jnp.broadcast_to(b_s[:, :, None], (N_CGROUPS, CHUNK, CGROUP_DIM)).astype(F32),
            w_out.astype(BF16), g.reshape(1, D), b.reshape(1, D)]
    return pl.pallas_call(
        _odd_kernel, grid=(T // tm,),
        in_specs=[row] + [full(a) for a in args],
        out_specs=row, out_shape=jax.ShapeDtypeStruct((T, D), F32),
        scratch_shapes=[pltpu.VMEM((tm, C_WIDTH), BF16)],
        compiler_params=_cparams("parallel"), name="odd_mixer")(x, *args)


def _router_kernel(x_ref, w_ref, rb_ref, tri_ref, eidx_ref, wts_ref, pos_ref, cnt_ref, xp_ref, run_ref):
    tm = x_ref.shape[0]
    i = pl.program_id(0)

    @pl.when(i == 0)
    def _():
        run_ref[...] = jnp.zeros(run_ref.shape, F32)

    x = x_ref[...]
    xp_ref[...] = pltpu.pack_elementwise([x[:, :HALF], x[:, HALF:]], packed_dtype=BF16)
    xh = x.astype(BF16)
    xl = (x - xh.astype(F32)).astype(BF16)
    nt = (((1,), (1,)), ((), ()))
    dg = lambda a, c: lax.dot_general(a, c, nt, preferred_element_type=F32)
    logits = dg(w_ref[0], xh) + dg(w_ref[0], xl) + dg(w_ref[1], xh)
    scores = jax.nn.sigmoid(logits)
    sel = scores + rb_ref[...]

    i8 = lax.broadcasted_iota(I32, (GROUP_SIZE, tm), 0)
    gsc_rows = []
    for gidx in range(N_EXPERT_GROUPS):
        sg = sel[gidx * GROUP_SIZE:(gidx + 1) * GROUP_SIZE, :]
        m1 = jnp.max(sg, axis=0, keepdims=True)
        f1 = jnp.min(jnp.where(sg == m1, i8, GROUP_SIZE), axis=0, keepdims=True)
        m2 = jnp.max(jnp.where(i8 == f1, NEG_INF, sg), axis=0, keepdims=True)
        gsc_rows.append(m1 + m2)
    gsc = jnp.concatenate(gsc_rows, axis=0)

    gsel = jnp.zeros(gsc.shape, F32)
    for _ in range(TOPK_GROUPS):
        m = jnp.max(gsc, axis=0, keepdims=True)
        f = jnp.min(jnp.where(gsc == m, i8, N_EXPERT_GROUPS), axis=0, keepdims=True)
        pick = i8 == f
        gsel = jnp.where(pick, 1.0, gsel)
        gsc = jnp.where(pick, NEG_INF, gsc)
    esel = jnp.concatenate(
        [jnp.broadcast_to(gsel[gidx:gidx + 1, :], (GROUP_SIZE, tm)) for gidx in range(N_EXPERT_GROUPS)],
        axis=0)

    cur = jnp.where(esel > 0.0, sel, NEG_INF)
    ei = lax.broadcasted_iota(I32, cur.shape, 0)
    idx_rows, sc_rows = [], []
    chosen = jnp.zeros(cur.shape, F32)
    for _ in range(TOP_K):
        m = jnp.max(cur, axis=0, keepdims=True)
        f = jnp.min(jnp.where(cur == m, ei, N_EXPERTS), axis=0, keepdims=True)
        pick = ei == f
        idx_rows.append(f)
        sc_rows.append(jnp.sum(jnp.where(pick, scores, 0.0), axis=0, keepdims=True))
        chosen = jnp.where(pick, 1.0, chosen)
        cur = jnp.where(pick, NEG_INF, cur)
    eidx = jnp.concatenate(idx_rows, axis=0)
    sc = jnp.concatenate(sc_rows, axis=0)
    eidx_ref[...] = eidx
    wts_ref[...] = sc / jnp.sum(sc, axis=0, keepdims=True) * ROUTE_SCALE

    before = _dot(chosen.astype(BF16), tri_ref[...]) + run_ref[...]
    pos_rows = [jnp.sum(jnp.where(ei == idx_rows[k], before, 0.0), axis=0, keepdims=True)
                for k in range(TOP_K)]
    pos_ref[...] = jnp.concatenate(pos_rows, axis=0).astype(I32)
    run_new = run_ref[...] + jnp.sum(chosen, axis=1, keepdims=True)
    run_ref[...] = run_new
    cnt_ref[...] = jnp.broadcast_to(run_new, cnt_ref.shape).astype(I32)


def _router(x, router_w, router_b, first_row, T):
    D = x.shape[1]
    tm = min(ROUTER_TILE, T)
    n_tiles = T // tm
    first_tile = first_row // tm
    assert n_tiles * tm == T and first_tile * tm == first_row
    wt = router_w.T.astype(F32)
    wh = wt.astype(BF16)
    wl = (wt - wh.astype(F32)).astype(BF16)
    w2 = jnp.stack([wh, wl])
    rb = router_b.astype(F32).reshape(N_EXPERTS, 1)
    tri = jnp.asarray(np.triu(np.ones((tm, tm)), 1), BF16)
    full = lambda a: pl.BlockSpec(a.shape, lambda i: (0,) * a.ndim)
    col = pl.BlockSpec((TOP_K, tm), lambda i: (0, i))
    return pl.pallas_call(
        _router_kernel, grid=(n_tiles,),
        in_specs=[pl.BlockSpec((tm, D), lambda i: (i + first_tile, 0)), full(w2), full(rb), full(tri)],
        out_specs=[col, col, col, pl.BlockSpec((N_EXPERTS, LANES), lambda i: (0, 0)),
                   pl.BlockSpec((tm, HALF), lambda i: (i, 0))],
        out_shape=[jax.ShapeDtypeStruct((TOP_K, T), I32), jax.ShapeDtypeStruct((TOP_K, T), F32),
                   jax.ShapeDtypeStruct((TOP_K, T), I32), jax.ShapeDtypeStruct((N_EXPERTS, LANES), I32),
                   jax.ShapeDtypeStruct((T, HALF), I32)],
        scratch_shapes=[pltpu.VMEM((N_EXPERTS, 1), F32)],
        compiler_params=_cparams("arbitrary"), name="router")(x, w2, rb, tri)


def _dest_kernel(start_ref, eidx_ref, pos_ref, o_ref):
    e = eidx_ref[...]
    acc = pos_ref[...]
    for j in range(N_EXPERTS):
        acc = acc + jnp.where(e == j, start_ref[j], 0)
    o_ref[...] = acc


def _dest_rows(eidx, pos, seg_start):
    K, T = eidx.shape
    tl = min(T, 2048)
    blk = pl.BlockSpec((K, tl), lambda i, s: (0, i))
    grid_spec = pltpu.PrefetchScalarGridSpec(
        num_scalar_prefetch=1, grid=(T // tl,), in_specs=[blk, blk], out_specs=blk)
    return pl.pallas_call(
        _dest_kernel, grid_spec=grid_spec, out_shape=jax.ShapeDtypeStruct((K, T), I32),
        compiler_params=_cparams("parallel"), name="dest_rows")(seg_start, eidx, pos)


def _gather_rows(table, idx):
    n_rows = idx.shape[0]
    width = table.shape[1]
    per_worker = n_rows // SC_WORKERS
    n_chunks = per_worker // SC_CHUNK
    assert per_worker * SC_WORKERS == n_rows and n_chunks * SC_CHUNK == per_worker
    mesh = plsc.VectorSubcoreMesh(core_axis_name="c", subcore_axis_name="s")

    @functools.partial(
        pl.kernel, mesh=mesh,
        out_type=jax.ShapeDtypeStruct((n_rows, width), table.dtype),
        scratch_types=[pltpu.VMEM((SC_CHUNK,), I32), pltpu.VMEM((SC_CHUNK, width), table.dtype),
                       pltpu.SemaphoreType.DMA])
    def gather(table_hbm, idx_hbm, out_hbm, idx_v, rows_v, sem):
        wid = lax.axis_index("s") * SC_CORES + lax.axis_index("c")
        base = wid * per_worker

        @pl.loop(0, n_chunks)
        def _(j):
            off = base + j * SC_CHUNK
            pltpu.sync_copy(idx_hbm.at[pl.ds(off, SC_CHUNK)], idx_v)
            pltpu.async_copy(table_hbm.at[idx_v], rows_v, sem).wait()
            pltpu.sync_copy(rows_v, out_hbm.at[pl.ds(off, SC_CHUNK)])

    return gather(table, idx)


def _scatter_rows(rows, dest):
    n_tok, width = rows.shape
    n_dst = dest.shape[0]
    per_worker = n_tok // SC_WORKERS
    n_chunks = per_worker // SC_CHUNK
    assert per_worker * SC_WORKERS == n_tok and n_chunks * SC_CHUNK == per_worker
    mesh = plsc.VectorSubcoreMesh(core_axis_name="c", subcore_axis_name="s")

    @functools.partial(
        pl.kernel, mesh=mesh,
        out_type=jax.ShapeDtypeStruct((n_dst * n_tok, width), rows.dtype),
        scratch_types=[pltpu.VMEM((n_dst, SC_CHUNK), I32), pltpu.VMEM((SC_CHUNK, width), rows.dtype),
                       pltpu.SemaphoreType.DMA])
    def scatter(rows_hbm, dest_hbm, out_hbm, idx_v, rows_v, sem):
        wid = lax.axis_index("s") * SC_CORES + lax.axis_index("c")
        base = wid * per_worker

        @pl.loop(0, n_chunks)
        def _(j):
            off = base + j * SC_CHUNK
            pltpu.sync_copy(dest_hbm.at[:, pl.ds(off, SC_CHUNK)], idx_v)
            pltpu.sync_copy(rows_hbm.at[pl.ds(off, SC_CHUNK)], rows_v)
            copies = [pltpu.async_copy(rows_v, out_hbm.at[idx_v.at[k]], sem) for k in range(n_dst)]
            for c in copies:
                c.wait()

    return scatter(rows, dest)


def _expert_kernel(blk_ref, exp_ref, lo_ref, hi_ref, slot_ref, nxt_ref, xs_hbm, wg_hbm, wu_hbm, wd_hbm,
                   ys_hbm, wg_buf, wu_buf, wd_buf, wgu_s, wd_s, sem, xs_buf, xs_sem, ys_buf, ys_sem,
                   *, layer):
    i = pl.program_id(0)
    prev = jnp.maximum(i - 1, 0)

    def weight_copies(expert, slot):
        return [pltpu.make_async_copy(src.at[layer, expert], dst.at[slot], sem.at[slot])
                for src, dst in ((wg_hbm, wg_buf), (wu_hbm, wu_buf), (wd_hbm, wd_buf))]

    @pl.when(i == 0)
    def _():
        for c in weight_copies(exp_ref[0], slot_ref[0]):
            c.start()

    @pl.when(jnp.logical_or(i == 0, exp_ref[i] != exp_ref[prev]))
    def _():
        slot = slot_ref[i]
        for c in weight_copies(exp_ref[i], slot):
            c.wait()
        wgu_s[:, :EXPERT_DIM] = wg_buf[slot].astype(BF16)
        wgu_s[:, EXPERT_DIM:] = wu_buf[slot].astype(BF16)
        wd_s[...] = wd_buf[slot].astype(BF16)

        @pl.when(nxt_ref[i] >= 0)
        def _():
            for c in weight_copies(nxt_ref[i], 1 - slot):
                c.start()

    n_blocks = xs_hbm.shape[0] // EXPERT_ROWS
    blk = blk_ref[i]
    first = jnp.logical_or(i == 0, blk != blk_ref[prev])

    def rows_copy(block):
        slot = block % EXPERT_XS_SLOTS
        src = xs_hbm.at[pl.ds(pl.multiple_of(block * EXPERT_ROWS, EXPERT_ROWS), EXPERT_ROWS)]
        return pltpu.make_async_copy(src, xs_buf.at[slot], xs_sem.at[slot])

    @pl.when(i == 0)
    def _():
        for b0 in range(min(EXPERT_XS_SLOTS - 1, n_blocks)):
            rows_copy(b0).start()

    def result_copy(block):
        slot = block % EXPERT_XS_SLOTS
        dst = ys_hbm.at[pl.ds(pl.multiple_of(block * EXPERT_ROWS, EXPERT_ROWS), EXPERT_ROWS)]
        return pltpu.make_async_copy(ys_buf.at[slot], dst, ys_sem.at[slot])

    @pl.when(first)
    def _():
        rows_copy(blk).wait()

        @pl.when(blk + EXPERT_XS_SLOTS - 1 < n_blocks)
        def _():
            rows_copy(blk + EXPERT_XS_SLOTS - 1).start()

        @pl.when(blk >= EXPERT_XS_SLOTS)
        def _():
            result_copy(blk - EXPERT_XS_SLOTS).wait()

    xs_ref = xs_buf.at[blk % EXPERT_XS_SLOTS]
    ys_ref = ys_buf.at[blk % EXPERT_XS_SLOTS]
    lo = lo_ref[i]
    hi = hi_ref[i]

    def sub_block(r0):
        rows = slice(r0, r0 + EXPERT_SUB_ROWS)
        w = xs_ref[rows, :]
        xlo = lax.bitcast_convert_type(w.astype(jnp.int16), BF16)
        xhi = lax.bitcast_convert_type(lax.shift_right_logical(w, 16).astype(jnp.int16), BF16)
        gu = _dot(xlo, wgu_s[:HALF, :]) + _dot(xhi, wgu_s[HALF:, :])
        g = gu[:, :EXPERT_DIM]
        hb = (g * jax.nn.sigmoid(g) * gu[:, EXPERT_DIM:]).astype(BF16)
        y = _dot(hb, wd_s[...])
        packed = pltpu.pack_elementwise([y[:, :HALF], y[:, HALF:]], packed_dtype=BF16)
        row = r0 + lax.broadcasted_iota(I32, (EXPERT_SUB_ROWS, 1), 0)
        mine = jnp.logical_and(row >= lo, row < hi)
        kept = jnp.where(first, 0, ys_ref[rows, :])
        ys_ref[rows, :] = jnp.where(mine, packed, kept)

    def touched(r0):
        return jnp.logical_and(lo < r0 + EXPERT_SUB_ROWS, hi > r0)

    for r0 in range(0, EXPERT_ROWS, 2 * EXPERT_SUB_ROWS):
        r1 = r0 + EXPERT_SUB_ROWS
        t0, t1 = touched(r0), touched(r1)

        @pl.when(jnp.logical_and(t0, t1))
        def _():
            sub_block(r0)
            sub_block(r1)

        pl.when(jnp.logical_and(t0, jnp.logical_not(t1)))(functools.partial(sub_block, r0))
        pl.when(jnp.logical_and(jnp.logical_not(t0), t1))(functools.partial(sub_block, r1))

    n_items = pl.num_programs(0)
    final = i == n_items - 1
    block_done = jnp.logical_or(final, blk_ref[jnp.minimum(i + 1, n_items - 1)] != blk)

    @pl.when(block_done)
    def _():
        result_copy(blk).start()

    @pl.when(final)
    def _():
        for b0 in range(max(n_blocks - EXPERT_XS_SLOTS, 0), n_blocks):
            result_copy(b0).wait()


def _expert_items(counts, n_rows):
    n_blocks = n_rows // EXPERT_ROWS
    n_items = n_blocks + N_EXPERTS - 1
    end = jnp.cumsum(counts)
    start = end - counts
    first_blk = start // EXPERT_ROWS
    n_blk = jnp.where(counts > 0, (end - 1) // EXPERT_ROWS - first_blk + 1, 0)
    item_end = jnp.cumsum(n_blk)
    item_start = item_end - n_blk
    slot = jnp.arange(n_items, dtype=I32)
    e = jnp.minimum(jnp.sum((item_end[None, :] <= slot[:, None]).astype(I32), axis=1), N_EXPERTS - 1)
    onehot = (e[:, None] == jnp.arange(N_EXPERTS, dtype=I32)[None, :]).astype(I32)
    pick = lambda v: jnp.sum(onehot * v[None, :], axis=1)
    valid = slot < item_end[-1]
    blk = jnp.where(valid, pick(first_blk) + slot - pick(item_start), n_blocks - 1)
    lo = jnp.clip(pick(start) - blk * EXPERT_ROWS, 0, EXPERT_ROWS)
    hi = jnp.clip(pick(end) - blk * EXPERT_ROWS, 0, EXPERT_ROWS)
    last_e = jnp.max(jnp.where(counts > 0, jnp.arange(N_EXPERTS, dtype=I32), 0))
    e = jnp.where(valid, e, last_e)
    hi = jnp.where(valid, hi, 0)
    lo = jnp.where(valid, lo, 0)
    change = jnp.concatenate([jnp.ones((1,), I32), (e[1:] != e[:-1]).astype(I32)])
    slot = (jnp.cumsum(change) - 1) % 2
    later = jnp.where(jnp.arange(N_EXPERTS, dtype=I32)[None, :] > e[:, None], counts[None, :] > 0, False)
    nxt = jnp.where(jnp.any(later, axis=1), jnp.argmax(later, axis=1), -1)
    return tuple(a.astype(I32) for a in (blk, e, lo, hi, slot, nxt))


def _experts(xs, items, w_gate, w_up, w_down, layer):
    n_rows = xs.shape[0]
    n_items = items[0].shape[0]
    hbm = pl.BlockSpec(memory_space=pl.ANY)
    grid_spec = pltpu.PrefetchScalarGridSpec(
        num_scalar_prefetch=len(items), grid=(n_items,),
        in_specs=[hbm, hbm, hbm, hbm],
        out_specs=hbm,
        scratch_shapes=[pltpu.VMEM((2, D_MODEL, EXPERT_DIM), F32),
                        pltpu.VMEM((2, D_MODEL, EXPERT_DIM), F32),
                        pltpu.VMEM((2, EXPERT_DIM, D_MODEL), F32),
                        pltpu.VMEM((D_MODEL, 2 * EXPERT_DIM), BF16),
                        pltpu.VMEM((EXPERT_DIM, D_MODEL), BF16),
                        pltpu.SemaphoreType.DMA((2,)),
                        pltpu.VMEM((EXPERT_XS_SLOTS, EXPERT_ROWS, HALF), I32),
                        pltpu.SemaphoreType.DMA((EXPERT_XS_SLOTS,)),
                        pltpu.VMEM((EXPERT_XS_SLOTS, EXPERT_ROWS, HALF), I32),
                        pltpu.SemaphoreType.DMA((EXPERT_XS_SLOTS,))])
    return pl.pallas_call(
        functools.partial(_expert_kernel, layer=layer), grid_spec=grid_spec,
        out_shape=jax.ShapeDtypeStruct((n_rows, HALF), I32),
        compiler_params=_cparams("arbitrary"), name="experts")(*items, xs, w_gate, w_up, w_down)


def _moe_out_kernel(x_ref, yg_ref, wt_ref, sgu_ref, sd_ref, g_ref, b_ref, *rest):
    o_ref = rest[-1]
    x = x_ref[...]
    wt = wt_ref[...]
    lo = jnp.zeros((x.shape[0], HALF), F32)
    hi = jnp.zeros((x.shape[0], HALF), F32)
    for k in range(TOP_K):
        w = yg_ref[k]
        wk = wt[:, k:k + 1]
        lo = lo + wk * _unpack_lo(w)
        hi = hi + wk * _unpack_hi(w)
    gu = _dot(x.astype(BF16), sgu_ref[...])
    g = gu[:, :EXPERT_DIM]
    hs = (g * jax.nn.sigmoid(g) * gu[:, EXPERT_DIM:]).astype(BF16)
    ffn = jnp.concatenate([lo, hi], axis=1) + _dot(hs, sd_ref[...])
    o_ref[...] = _ln(DEEPNORM_ALPHA * x + ffn, g_ref[...], b_ref[...])


def _moe_out(x, yg, wts, sh_gate, sh_up, sh_down, g, b, first_tile, partial_out):
    T, D = x.shape
    tm = ROW_TILE
    n_tiles = yg.shape[1] // tm
    xrow = pl.BlockSpec((tm, D), lambda i: (i + first_tile, 0))
    full = lambda a: pl.BlockSpec(a.shape, lambda i: (0,) * a.ndim)
    sgu = jnp.concatenate([sh_gate, sh_up], axis=1).astype(BF16)
    sd = sh_down.astype(BF16)
    g = g.reshape(1, D)
    b = b.reshape(1, D)
    args = [x, yg, wts, sgu, sd, g, b]
    in_specs = [xrow, pl.BlockSpec((TOP_K, tm, HALF), lambda i: (0, i, 0)),
                pl.BlockSpec((tm, TOP_K), lambda i: (i, 0)), full(sgu), full(sd), full(g), full(b)]
    aliases = {}
    if partial_out is not None:
        args.append(partial_out)
        in_specs.append(pl.BlockSpec(memory_space=pl.ANY))
        aliases = {len(args) - 1: 0}
    return pl.pallas_call(
        _moe_out_kernel, grid=(n_tiles,), in_specs=in_specs,
        out_specs=xrow, out_shape=jax.ShapeDtypeStruct((T, D), F32),
        input_output_aliases=aliases,
        compiler_params=_cparams("parallel"), name="moe_out")(*args)


def _moe(x, router_w, router_b, w_gate, w_up, w_down, layer, sh_gate, sh_up, sh_down, g, b):
    T = x.shape[0]
    tiles = T // ROW_TILE // MOE_TOKEN_GROUPS
    tg = tiles * ROW_TILE
    out = None
    for grp in range(MOE_TOKEN_GROUPS):
        eidx, wts, pos, cnt, xp = _router(x, router_w, router_b, grp * tg, tg)
        counts = cnt[:, 0]
        seg_start = (jnp.cumsum(counts) - counts).astype(I32)
        dest = _dest_rows(eidx, pos, seg_start)
        xs = _scatter_rows(xp, dest)
        ys = _experts(xs, _expert_items(counts, tg * TOP_K), w_gate, w_up, w_down, layer)
        yg = _gather_rows(ys, dest.reshape(tg * TOP_K)).reshape(TOP_K, tg, HALF)
        out = _moe_out(x, yg, wts.T, sh_gate, sh_up, sh_down, g, b, grp * tiles, out)
    return out


def kernel(x, ln_in_g, ln_in_b, e_w_in, e_w_fourier, e_q_gain, e_k_gain, e_w_out, o_w_in, o_b_in, o_v_ln_g, o_v_ln_b, o_w_spatial, o_b_spatial, o_w_out, ln_mix_g, ln_mix_b, ln_ffn_g, ln_ffn_b, router_w, router_b, exp_w_gate, exp_w_up, exp_w_down, sh_w_gate, sh_w_up, sh_w_down):
    B, S, D = x.shape
    T = B * S
    h = _layer_norm(x.reshape(T, D), ln_in_g, ln_in_b)
    for i in range(DEPTH):
        j = i // 2
        if i % 2 == 0:
            a, qt, k2, vt = _even_in(h, e_w_in[j], e_q_gain[j], e_k_gain[j], B, S)
            a_out = _fourier(a, e_w_fourier[j], B, S)
            attn = _attention(qt, k2, vt, B, S)
            h = _even_out(a_out, attn, e_w_out[j], h, ln_mix_g[i], ln_mix_b[i])
        else:
            h = _odd_mixer(h, o_w_in[j], o_b_in[j], o_v_ln_g[j], o_v_ln_b[j], o_w_spatial[j],
                           o_b_spatial[j], o_w_out[j], ln_mix_g[i], ln_mix_b[i])
        h = _moe(h, router_w[i], router_b[i], exp_w_gate, exp_w_up, exp_w_down, i,
                 sh_w_gate[i], sh_w_up[i], sh_w_down[i], ln_ffn_g[i], ln_ffn_b[i])
    return h.reshape(B, S, D)
```

```python
import functools
import math

import numpy as np
import jax
import jax.numpy as jnp
from jax import lax
from jax.experimental import pallas as pl
from jax.experimental.pallas import tpu as pltpu
from jax.experimental.pallas import tpu_sc as plsc

F32 = jnp.float32
BF16 = jnp.bfloat16
I32 = jnp.int32

D_MODEL = 1024
DEPTH = 4
GRID_W = 64
N_FGROUPS = 4
FGROUP_DIM = 128
F_WIDTH = N_FGROUPS * FGROUP_DIM
N_HEADS = 8
N_KV_HEADS = 2
HEAD_DIM = 64
Q_GROUP = N_HEADS // N_KV_HEADS
Q_WIDTH = N_HEADS * HEAD_DIM
KV_WIDTH = N_KV_HEADS * HEAD_DIM
ROPE_THETA = 10000.0
ROPE_PAIRS = HEAD_DIM // 4
CHUNK = 128
N_CGROUPS = 8
CGROUP_DIM = D_MODEL // N_CGROUPS
C_WIDTH = N_CGROUPS * CGROUP_DIM
N_EXPERTS = 64
EXPERT_DIM = 256
TOP_K = 8
N_EXPERT_GROUPS = 8
GROUP_SIZE = N_EXPERTS // N_EXPERT_GROUPS
TOPK_GROUPS = 4
ROUTE_SCALE = 2.5
LN_EPS = 1e-5
QK_EPS = 1e-6
DEEPNORM_ALPHA = (2 * DEPTH) ** 0.25

LANES = 128
SUBLANES = 8
VMEM_LIMIT_BYTES = 56 * 1024 * 1024
ROW_TILE = 512
DFT_N1 = 64
DFT_KRON = 4
DFT_PITCH_PAD = 8
ROUTER_TILE = 1024
EXPERT_ROWS = 2048
EXPERT_XS_SLOTS = 3
EXPERT_SUB_ROWS = 512
HALF = D_MODEL // 2
SC_CORES = 2
SC_SUBCORES = 16
SC_WORKERS = SC_CORES * SC_SUBCORES
SC_CHUNK = 128
MOE_TOKEN_GROUPS = 2
ATT_TQ = 256
ATT_TK = 256
ATT_V_ROWS = -(-(HEAD_DIM + 1) // 16) * 16
ATT_BOUND_SLACK = 1.0 + 2.0 ** -7
ATT_MIN_ROW_SUM = 2.0 ** -80
NEG_INF = float("-inf")


def _cparams(*sem):
    return pltpu.CompilerParams(dimension_semantics=sem, vmem_limit_bytes=VMEM_LIMIT_BYTES)


def _ln(x, g, b):
    mu = jnp.mean(x, axis=-1, keepdims=True)
    xc = x - mu
    var = jnp.mean(xc * xc, axis=-1, keepdims=True)
    return xc * lax.rsqrt(var + LN_EPS) * g + b


def _dot(a, b):
    return jnp.dot(a, b, preferred_element_type=F32)


def _unpack_lo(w):
    return lax.bitcast_convert_type(lax.shift_left(w, 16), F32)


def _unpack_hi(w):
    return lax.bitcast_convert_type(w & jnp.int32(-65536), F32)


def _even_in_kernel(x_ref, w_ref, qm_ref, km_ref, qg_ref, kg_ref, cos_ref, sin_ref,
                    *rest, input_ln):
    if input_ln:
        lg_ref, lb_ref, a_ref, qt_ref, k_ref, vt_ref, xn_ref = rest
        x = _ln(x_ref[...], lg_ref[...], lb_ref[...])
        xn_ref[...] = x
    else:
        a_ref, qt_ref, k_ref, vt_ref = rest
        x = x_ref[...]
    tm = x_ref.shape[0]
    h = _dot(x.astype(BF16), w_ref[...])
    a_ref[...] = h[:, :F_WIDTH].astype(BF16)
    q = h[:, F_WIDTH:F_WIDTH + Q_WIDTH]
    k = h[:, F_WIDTH + Q_WIDTH:F_WIDTH + Q_WIDTH + KV_WIDTH]
    v = h[:, F_WIDTH + Q_WIDTH + KV_WIDTH:]
    cos = cos_ref[...]
    sin = sin_ref[...]
    lane = lax.broadcasted_iota(I32, (tm, LANES), 1)
    first_of_pair = (lane & ROPE_PAIRS) == 0

    def mean_sq(xf, m_ref):
        sq = xf * xf
        hi = sq.astype(BF16)
        lo = (sq - hi.astype(F32)).astype(BF16)
        return _dot(hi, m_ref[...]) + _dot(lo, m_ref[...])

    def rope(xn):
        sw = jnp.where(first_of_pair, pltpu.roll(xn, LANES - ROPE_PAIRS, 1), pltpu.roll(xn, ROPE_PAIRS, 1))
        return xn * cos + sw * sin

    qn = q * lax.rsqrt(mean_sq(q, qm_ref) + QK_EPS) * qg_ref[...]
    scale = math.log2(math.e) / math.sqrt(HEAD_DIM)
    for c in range(Q_WIDTH // LANES):
        qt_ref[c * LANES:(c + 1) * LANES, :] = (rope(qn[:, c * LANES:(c + 1) * LANES]) * scale).T.astype(BF16)
    kn = rope(k * lax.rsqrt(mean_sq(k, km_ref) + QK_EPS) * kg_ref[...])
    low = lane < HEAD_DIM
    k_ref[0] = jnp.where(low, kn, 0.0).astype(BF16)
    k_ref[1] = jnp.where(low, pltpu.roll(kn, HEAD_DIM, 1), 0.0).astype(BF16)
    ones_col = jnp.where(lane == HEAD_DIM, 1.0, 0.0)
    vt_ref[0:LANES, :] = jnp.where(low, v, ones_col).T.astype(BF16)
    vt_ref[LANES:2 * LANES, :] = jnp.where(low, pltpu.roll(v, HEAD_DIM, 1), ones_col).T.astype(BF16)


def _rope_tables(S):
    t = np.arange(S)
    inv = ROPE_THETA ** (-np.arange(ROPE_PAIRS, dtype=np.float64) / ROPE_PAIRS)
    ang_r = (t // GRID_W)[:, None] * inv
    ang_c = (t % GRID_W)[:, None] * inv
    cos = np.concatenate([np.cos(ang_r), np.cos(ang_r), np.cos(ang_c), np.cos(ang_c)], axis=1)
    sin = np.concatenate([-np.sin(ang_r), np.sin(ang_r), -np.sin(ang_c), np.sin(ang_c)], axis=1)
    return (jnp.asarray(np.tile(cos, (1, 2)), F32), jnp.asarray(np.tile(sin, (1, 2)), F32))


def _head_mean_matrix(width):
    m = np.kron(np.eye(width // HEAD_DIM), np.full((HEAD_DIM, HEAD_DIM), 1.0 / HEAD_DIM))
    return jnp.asarray(m, BF16)


def _even_in(x, w_in, q_gain, k_gain, B, S, input_ln=None):
    T, D = x.shape
    tm = ROW_TILE
    ns = S // tm
    cos, sin = _rope_tables(S)
    row = lambda w: pl.BlockSpec((tm, w), lambda i: (i, 0))
    full = lambda a: pl.BlockSpec(a.shape, lambda i: (0,) * a.ndim)
    tab = pl.BlockSpec((tm, LANES), lambda i: (i % ns, 0))
    w = w_in.astype(BF16)
    qm = _head_mean_matrix(Q_WIDTH)
    km = _head_mean_matrix(KV_WIDTH)
    qg = jnp.tile(q_gain.astype(F32), N_HEADS).reshape(1, Q_WIDTH)
    kg = jnp.tile(k_gain.astype(F32), N_KV_HEADS).reshape(1, KV_WIDTH)
    args = [x, w, qm, km, qg, kg, cos, sin]
    in_specs = [row(D), full(w), full(qm), full(km), full(qg), full(kg), tab, tab]
    out_specs = [row(F_WIDTH),
                 pl.BlockSpec((None, Q_WIDTH, tm), lambda i: (i // ns, 0, i % ns)),
                 pl.BlockSpec((N_KV_HEADS, tm, LANES), lambda i: (0, i, 0)),
                 pl.BlockSpec((None, N_KV_HEADS * LANES, tm), lambda i: (i // ns, 0, i % ns))]
    out_shape = [jax.ShapeDtypeStruct((T, F_WIDTH), BF16),
                 jax.ShapeDtypeStruct((B, Q_WIDTH, S), BF16),
                 jax.ShapeDtypeStruct((N_KV_HEADS, T, LANES), BF16),
                 jax.ShapeDtypeStruct((B, N_KV_HEADS * LANES, S), BF16)]
    if input_ln is not None:
        ln = [p.reshape(1, D) for p in input_ln]
        args += ln
        in_specs += [full(p) for p in ln]
        out_specs.append(row(D))
        out_shape.append(jax.ShapeDtypeStruct((T, D), F32))
    return pl.pallas_call(
        functools.partial(_even_in_kernel, input_ln=input_ln is not None), grid=(T // tm,),
        in_specs=in_specs, out_specs=out_specs, out_shape=out_shape,
        compiler_params=_cparams("parallel"), name="even_in")(*args)


def _fourier_kernel(a_ref, dftc_ref, taba_ref, kc_ref, ks_ref, wf_ref, o_ref,
                    zr_ref, zi_ref, ur_ref, ui_ref, y_ref):
    S = a_ref.shape[0]
    n1_count = DFT_N1
    n2_count = S // DFT_N1
    pz = n1_count + DFT_PITCH_PAD
    pu = n2_count + DFT_PITCH_PAD
    blk = DFT_KRON * DFT_N1
    scale = 1.0 / math.sqrt(S * FGROUP_DIM)

    def channel_dft(j, carry):
        zz = _dot(a_ref[pl.ds(pl.multiple_of(j * blk, blk), blk), :], dftc_ref[...])
        for q in range(DFT_KRON):
            dst = pl.ds(pl.multiple_of((j * DFT_KRON + q) * pz, 8), n1_count)
            zr_ref[dst, :] = zz[q * n1_count:(q + 1) * n1_count, :FGROUP_DIM]
            zi_ref[dst, :] = zz[q * n1_count:(q + 1) * n1_count, FGROUP_DIM:]
        return carry

    lax.fori_loop(0, S // blk, channel_dft, 0, unroll=4)

    def stage_a(n1, carry):
        src = pl.ds(n1, n2_count, stride=pz)
        zn = jnp.concatenate([zr_ref[src, :], zi_ref[src, :]], axis=1).astype(BF16)
        r = _dot(taba_ref[n1], zn)
        dst = pl.ds(pl.multiple_of(n1 * pu, 8), n2_count)
        ur_ref[dst, :] = r[:n2_count, :FGROUP_DIM] + r[n2_count:, FGROUP_DIM:]
        ui_ref[dst, :] = r[:n2_count, FGROUP_DIM:] - r[n2_count:, :FGROUP_DIM]
        return carry

    lax.fori_loop(0, n1_count, stage_a, 0, unroll=8)

    def stage_b(j, carry):
        srcs = [pl.ds(j * DFT_KRON + q, n1_count, stride=pu) for q in range(DFT_KRON)]
        ur = jnp.concatenate([ur_ref[s, :] for s in srcs], axis=0).astype(BF16)
        ui = jnp.concatenate([ui_ref[s, :] for s in srcs], axis=0).astype(BF16)
        re = _dot(kc_ref[...], ur) + _dot(ks_ref[...], ui)
        out = _dot((re * scale).astype(BF16), wf_ref[...])
        for q in range(DFT_KRON):
            y_ref[srcs[q], :] = out[q * n1_count:(q + 1) * n1_count]
        return carry

    lax.fori_loop(0, S // blk, stage_b, 0, unroll=8)

    def compact(k1, carry):
        o_ref[pl.ds(pl.multiple_of(k1 * n2_count, n2_count), n2_count), :] = (
            y_ref[pl.ds(pl.multiple_of(k1 * pu, 8), n2_count), :].astype(BF16))
        return carry

    lax.fori_loop(0, n1_count, compact, 0)


def _dft_tables(S):
    n1c, n2c = DFT_N1, S // DFT_N1
    c = np.arange(FGROUP_DIM)
    ang = 2 * np.pi * np.outer(c, c) / FGROUP_DIM
    dftc = np.concatenate([np.cos(ang), -np.sin(ang)], axis=1)
    n1 = np.arange(n1c)[:, None, None]
    k2 = np.arange(n2c)[None, :, None]
    n2 = np.arange(n2c)[None, None, :]
    th = 2 * np.pi * (n2 * k2 / n2c + n1 * k2 / S)
    taba = np.concatenate([np.cos(th), np.sin(th)], axis=1)
    k1 = np.arange(n1c)
    g = 2 * np.pi * np.outer(k1, k1) / n1c
    eye = np.eye(DFT_KRON)
    kc = np.kron(eye, np.cos(g))
    ks = np.kron(eye, np.sin(g))
    return tuple(jnp.asarray(t, BF16) for t in (dftc, taba, kc, ks))


def _fourier(a, w_fourier, B, S):
    T = a.shape[0]
    dftc, taba, kc, ks = _dft_tables(S)
    full = lambda t: pl.BlockSpec(t.shape, lambda b, g: (0,) * t.ndim)
    blk = pl.BlockSpec((S, FGROUP_DIM), lambda b, g: (b, g))
    return pl.pallas_call(
        _fourier_kernel, grid=(B, N_FGROUPS),
        in_specs=[blk, full(dftc), full(taba), full(kc), full(ks),
                  pl.BlockSpec((None, FGROUP_DIM, FGROUP_DIM), lambda b, g: (g, 0, 0))],
        out_specs=blk,
        out_shape=jax.ShapeDtypeStruct((T, F_WIDTH), BF16),
        scratch_shapes=(
            [pltpu.VMEM((S // DFT_N1 * (DFT_N1 + DFT_PITCH_PAD), FGROUP_DIM), F32)] * 2
            + [pltpu.VMEM((DFT_N1 * (S // DFT_N1 + DFT_PITCH_PAD), FGROUP_DIM), F32)] * 3),
        compiler_params=_cparams("parallel", "parallel"), name="fourier")(
            a, dftc, taba, kc, ks, w_fourier.astype(BF16))


def _attn_kernel(qt_ref, k_ref, vt_ref, o_ref, qs_ref, kmax_ref, acc_ref, m_ref, s0_ref, s1_ref,
                 p0_ref, p1_ref):
    tq = qt_ref.shape[1]
    n_keys = k_ref.shape[0]
    tk = min(ATT_TK, n_keys)
    n_chunks = n_keys // tk
    assert n_chunks % 2 == 0 and n_chunks * tk == n_keys

    def keys(c):
        return k_ref[pl.ds(pl.multiple_of(c * tk, tk), tk), :]

    @pl.when(pl.program_id(2) == 0)
    def _():
        def body(c, best):
            k = keys(c).astype(F32)
            return jnp.maximum(best, jnp.sum(k * k, axis=1, keepdims=True))
        best = lax.fori_loop(0, n_chunks, body, jnp.zeros((tk, 1), F32))
        kmax_ref[...] = jnp.broadcast_to(jnp.sqrt(jnp.max(best, axis=0, keepdims=True)), kmax_ref.shape)

    qs_ref[HEAD_DIM:, :] = jnp.zeros((LANES - HEAD_DIM, Q_GROUP * tq), BF16)
    for g in range(Q_GROUP):
        qs_ref[:HEAD_DIM, g * tq:(g + 1) * tq] = qt_ref[g * HEAD_DIM:(g + 1) * HEAD_DIM, :]
    qf = qs_ref[...].astype(F32)
    bound = jnp.sqrt(jnp.sum(qf * qf, axis=0, keepdims=True)) * kmax_ref[0:1, 0:1] * ATT_BOUND_SLACK

    def scores(c):
        return _dot(keys(c), qs_ref[...])

    def values(c):
        return vt_ref[:ATT_V_ROWS, pl.ds(pl.multiple_of(c * tk, tk), tk)]

    def weights(s_buf):
        return jnp.exp2(s_buf[...] - bound).astype(BF16)

    def accumulate(p_buf, c):
        acc_ref[...] += _dot(values(c), p_buf[...])

    last = n_chunks - 1

    def fast(c2, carry):
        c = 2 * c2
        s0_ref[...] = scores(jnp.minimum(c + 2, last))
        p1_ref[...] = weights(s1_ref)
        accumulate(p0_ref, c)
        s1_ref[...] = scores(jnp.minimum(c + 3, last))
        p0_ref[...] = weights(s0_ref)
        accumulate(p1_ref, c + 1)
        return carry

    acc_ref[...] = jnp.zeros(acc_ref.shape, F32)
    s0_ref[...] = scores(0)
    p0_ref[...] = weights(s0_ref)
    s1_ref[...] = scores(1)
    lax.fori_loop(0, n_chunks // 2, fast, 0, unroll=True)
    underflow = jnp.min(acc_ref[HEAD_DIM:HEAD_DIM + 1, :]) < ATT_MIN_ROW_SUM

    @pl.when(underflow)
    def _():
        def safe(c, carry):
            s = scores(c)
            m_old = m_ref[...]
            m_new = jnp.maximum(m_old, jnp.max(s, axis=0, keepdims=True))
            acc_ref[...] = (jnp.exp2(m_old - m_new) * acc_ref[...]
                            + _dot(values(c), jnp.exp2(s - m_new).astype(BF16)))
            m_ref[...] = m_new
            return carry

        m_ref[...] = jnp.full(m_ref.shape, NEG_INF, F32)
        acc_ref[...] = jnp.zeros(acc_ref.shape, F32)
        lax.fori_loop(0, n_chunks, safe, 0)

    acc = acc_ref[...]
    ot = acc[:HEAD_DIM, :] / acc[HEAD_DIM:HEAD_DIM + 1, :]
    ot = jnp.concatenate([ot, jnp.zeros((LANES - HEAD_DIM, Q_GROUP * tq), F32)], axis=0)
    o = ot.T
    o_ref[...] = jnp.concatenate([o[g * tq:(g + 1) * tq, :HEAD_DIM] for g in range(Q_GROUP)],
                                 axis=1).astype(BF16)


def _attention(qt, k2, vt, B, S):
    T = B * S
    tq = ATT_TQ
    nq = S // tq
    gw = Q_GROUP * HEAD_DIM
    cols = Q_GROUP * tq
    tk = min(ATT_TK, S)
    return pl.pallas_call(
        _attn_kernel, grid=(B, N_KV_HEADS, nq),
        in_specs=[pl.BlockSpec((None, gw, tq), lambda b, h, i: (b, h, i)),
                  pl.BlockSpec((None, None, S, LANES), lambda b, h, i: (h, b, 0, 0)),
                  pl.BlockSpec((None, LANES, S), lambda b, h, i: (b, h, 0))],
        out_specs=pl.BlockSpec((tq, gw), lambda b, h, i: (b * nq + i, h)),
        out_shape=jax.ShapeDtypeStruct((T, Q_WIDTH), BF16),
        scratch_shapes=[pltpu.VMEM((LANES, cols), BF16), pltpu.VMEM((SUBLANES, LANES), F32),
                        pltpu.VMEM((ATT_V_ROWS, cols), F32), pltpu.VMEM((1, cols), F32),
                        pltpu.VMEM((tk, cols), F32), pltpu.VMEM((tk, cols), F32),
                        pltpu.VMEM((tk, cols), BF16), pltpu.VMEM((tk, cols), BF16)],
        compiler_params=_cparams("parallel", "parallel", "arbitrary"),
        name="attention")(qt, k2.reshape(N_KV_HEADS, B, S, LANES), vt)


def _even_out_kernel(a_ref, t_ref, wa_ref, wt_ref, x_ref, g_ref, b_ref, o_ref):
    mix = _dot(a_ref[...], wa_ref[...]) + _dot(t_ref[...], wt_ref[...])
    o_ref[...] = _ln(DEEPNORM_ALPHA * x_ref[...] + mix, g_ref[...], b_ref[...])


def _even_out(a_out, attn, w_out, x, g, b):
    T, D = x.shape
    tm = ROW_TILE
    row = lambda w: pl.BlockSpec((tm, w), lambda i: (i, 0))
    full = lambda a: pl.BlockSpec(a.shape, lambda i: (0,) * a.ndim)
    wa = w_out[:F_WIDTH].astype(BF16)
    wt = w_out[F_WIDTH:].astype(BF16)
    g = g.reshape(1, D)
    b = b.reshape(1, D)
    return pl.pallas_call(
        _even_out_kernel, grid=(T // tm,),
        in_specs=[row(F_WIDTH), row(Q_WIDTH), full(wa), full(wt), row(D), full(g), full(b)],
        out_specs=row(D), out_shape=jax.ShapeDtypeStruct((T, D), F32),
        compiler_params=_cparams("parallel"), name="even_out")(a_out, attn, wa, wt, x, g, b)


def _odd_kernel(x_ref, wi_ref, bi_ref, vg_ref, vb_ref, ws_ref, bs_ref, wo_ref, g_ref, b_ref, o_ref,
                gate_ref):
    tm = x_ref.shape[0]
    x = x_ref[...]
    h = _dot(x.astype(BF16), wi_ref[...]) + bi_ref[...]
    h = 0.5 * h * (1.0 + lax.erf(h * (1.0 / math.sqrt(2.0))))
    u = h[:, :C_WIDTH]
    v = _ln(h[:, C_WIDTH:], vg_ref[...], vb_ref[...]).astype(BF16)
    for c in range(tm // CHUNK):
        r0 = c * CHUNK
        for gi in range(N_CGROUPS):
            l0 = gi * CGROUP_DIM
            sv = _dot(ws_ref[gi], v[r0:r0 + CHUNK, l0:l0 + CGROUP_DIM]) + bs_ref[gi]
            gate_ref[r0:r0 + CHUNK, l0:l0 + CGROUP_DIM] = (
                u[r0:r0 + CHUNK, l0:l0 + CGROUP_DIM] * sv).astype(BF16)
    mix = _dot(gate_ref[...], wo_ref[...])
    o_ref[...] = _ln(DEEPNORM_ALPHA * x + mix, g_ref[...], b_ref[...])


def _odd_mixer(x, w_in, b_in, v_g, v_b, w_s, b_s, w_out, g, b):
    T, D = x.shape
    tm = ROW_TILE
    row = pl.BlockSpec((tm, D), lambda i: (i, 0))
    full = lambda a: pl.BlockSpec(a.shape, lambda i: (0,) * a.ndim)
    args = [w_in.astype(BF16), b_in.reshape(1, 2 * C_WIDTH), v_g.reshape(1, C_WIDTH),
            v_b.reshape(1, C_WIDTH), w_s.astype(BF16),
            jnp.broadcast_to(b_s[:, :, None], (N_CGROUPS, CHUNK, CGROUP_DIM)).astype(F32),
            w_out.astype(BF16), g.reshape(1, D), b.reshape(1, D)]
    return pl.pallas_call(
        _odd_kernel, grid=(T // tm,),
        in_specs=[row] + [full(a) for a in args],
        out_specs=row, out_shape=jax.ShapeDtypeStruct((T, D), F32),
        scratch_shapes=[pltpu.VMEM((tm, C_WIDTH), BF16)],
        compiler_params=_cparams("parallel"), name="odd_mixer")(x, *args)


def _router_kernel(x_ref, w_ref, rb_ref, tri_ref, eidx_ref, wts_ref, pos_ref, cnt_ref, xp_ref, run_ref):
    tm = x_ref.shape[0]
    i = pl.program_id(0)

    @pl.when(i == 0)
    def _():
        run_ref[...] = jnp.zeros(run_ref.shape, F32)

    x = x_ref[...]
    xp_ref[...] = pltpu.pack_elementwise([x[:, :HALF], x[:, HALF:]], packed_dtype=BF16)
    xh = x.astype(BF16)
    xl = (x - xh.astype(F32)).astype(BF16)
    nt = (((1,), (1,)), ((), ()))
    dg = lambda a, c: lax.dot_general(a, c, nt, preferred_element_type=F32)
    logits = dg(w_ref[0], xh) + dg(w_ref[0], xl) + dg(w_ref[1], xh)
    scores = jax.nn.sigmoid(logits)
    sel = scores + rb_ref[...]

    i8 = lax.broadcasted_iota(I32, (GROUP_SIZE, tm), 0)
    gsc_rows = []
    for gidx in range(N_EXPERT_GROUPS):
        sg = sel[gidx * GROUP_SIZE:(gidx + 1) * GROUP_SIZE, :]
        m1 = jnp.max(sg, axis=0, keepdims=True)
        f1 = jnp.min(jnp.where(sg == m1, i8, GROUP_SIZE), axis=0, keepdims=True)
        m2 = jnp.max(jnp.where(i8 == f1, NEG_INF, sg), axis=0, keepdims=True)
        gsc_rows.append(m1 + m2)
    gsc = jnp.concatenate(gsc_rows, axis=0)

    gsel = jnp.zeros(gsc.shape, F32)
    for _ in range(TOPK_GROUPS):
        m = jnp.max(gsc, axis=0, keepdims=True)
        f = jnp.min(jnp.where(gsc == m, i8, N_EXPERT_GROUPS), axis=0, keepdims=True)
        pick = i8 == f
        gsel = jnp.where(pick, 1.0, gsel)
        gsc = jnp.where(pick, NEG_INF, gsc)
    esel = jnp.concatenate(
        [jnp.broadcast_to(gsel[gidx:gidx + 1, :], (GROUP_SIZE, tm)) for gidx in range(N_EXPERT_GROUPS)],
        axis=0)

    cur = jnp.where(esel > 0.0, sel, NEG_INF)
    ei = lax.broadcasted_iota(I32, cur.shape, 0)
    idx_rows, sc_rows = [], []
    chosen = jnp.zeros(cur.shape, F32)
    for _ in range(TOP_K):
        m = jnp.max(cur, axis=0, keepdims=True)
        f = jnp.min(jnp.where(cur == m, ei, N_EXPERTS), axis=0, keepdims=True)
        pick = ei == f
        idx_rows.append(f)
        sc_rows.append(jnp.sum(jnp.where(pick, scores, 0.0), axis=0, keepdims=True))
        chosen = jnp.where(pick, 1.0, chosen)
        cur = jnp.where(pick, NEG_INF, cur)
    eidx = jnp.concatenate(idx_rows, axis=0)
    sc = jnp.concatenate(sc_rows, axis=0)
    eidx_ref[...] = eidx
    wts_ref[...] = sc / jnp.sum(sc, axis=0, keepdims=True) * ROUTE_SCALE

    before = _dot(chosen.astype(BF16), tri_ref[...]) + run_ref[...]
    pos_rows = [jnp.sum(jnp.where(ei == idx_rows[k], before, 0.0), axis=0, keepdims=True)
                for k in range(TOP_K)]
    pos_ref[...] = jnp.concatenate(pos_rows, axis=0).astype(I32)
    run_new = run_ref[...] + jnp.sum(chosen, axis=1, keepdims=True)
    run_ref[...] = run_new
    cnt_ref[...] = jnp.broadcast_to(run_new, cnt_ref.shape).astype(I32)


def _router(x, router_w, router_b, first_row, T):
    D = x.shape[1]
    tm = min(ROUTER_TILE, T)
    n_tiles = T // tm
    first_tile = first_row // tm
    assert n_tiles * tm == T and first_tile * tm == first_row
    wt = router_w.T.astype(F32)
    wh = wt.astype(BF16)
    wl = (wt - wh.astype(F32)).astype(BF16)
    w2 = jnp.stack([wh, wl])
    rb = router_b.astype(F32).reshape(N_EXPERTS, 1)
    tri = jnp.asarray(np.triu(np.ones((tm, tm)), 1), BF16)
    full = lambda a: pl.BlockSpec(a.shape, lambda i: (0,) * a.ndim)
    col = pl.BlockSpec((TOP_K, tm), lambda i: (0, i))
    return pl.pallas_call(
        _router_kernel, grid=(n_tiles,),
        in_specs=[pl.BlockSpec((tm, D), lambda i: (i + first_tile, 0)), full(w2), full(rb), full(tri)],
        out_specs=[col, col, col, pl.BlockSpec((N_EXPERTS, LANES), lambda i: (0, 0)),
                   pl.BlockSpec((tm, HALF), lambda i: (i, 0))],
        out_shape=[jax.ShapeDtypeStruct((TOP_K, T), I32), jax.ShapeDtypeStruct((TOP_K, T), F32),
                   jax.ShapeDtypeStruct((TOP_K, T), I32), jax.ShapeDtypeStruct((N_EXPERTS, LANES), I32),
                   jax.ShapeDtypeStruct((T, HALF), I32)],
        scratch_shapes=[pltpu.VMEM((N_EXPERTS, 1), F32)],
        compiler_params=_cparams("arbitrary"), name="router")(x, w2, rb, tri)


def _dest_kernel(start_ref, eidx_ref, pos_ref, o_ref):
    e = eidx_ref[...]
    acc = pos_ref[...]
    for j in range(N_EXPERTS):
        acc = acc + jnp.where(e == j, start_ref[j], 0)
    o_ref[...] = acc


def _dest_rows(eidx, pos, seg_start):
    K, T = eidx.shape
    tl = min(T, 2048)
    blk = pl.BlockSpec((K, tl), lambda i, s: (0, i))
    grid_spec = pltpu.PrefetchScalarGridSpec(
        num_scalar_prefetch=1, grid=(T // tl,), in_specs=[blk, blk], out_specs=blk)
    return pl.pallas_call(
        _dest_kernel, grid_spec=grid_spec, out_shape=jax.ShapeDtypeStruct((K, T), I32),
        compiler_params=_cparams("parallel"), name="dest_rows")(seg_start, eidx, pos)


def _gather_rows(table, idx):
    n_rows = idx.shape[0]
    width = table.shape[1]
    per_worker = n_rows // SC_WORKERS
    n_chunks = per_worker // SC_CHUNK
    assert per_worker * SC_WORKERS == n_rows and n_chunks * SC_CHUNK == per_worker
    mesh = plsc.VectorSubcoreMesh(core_axis_name="c", subcore_axis_name="s")

    @functools.partial(
        pl.kernel, mesh=mesh,
        out_type=jax.ShapeDtypeStruct((n_rows, width), table.dtype),
        scratch_types=[pltpu.VMEM((SC_CHUNK,), I32), pltpu.VMEM((SC_CHUNK, width), table.dtype),
                       pltpu.SemaphoreType.DMA])
    def gather(table_hbm, idx_hbm, out_hbm, idx_v, rows_v, sem):
        wid = lax.axis_index("s") * SC_CORES + lax.axis_index("c")
        base = wid * per_worker

        @pl.loop(0, n_chunks)
        def _(j):
            off = base + j * SC_CHUNK
            pltpu.sync_copy(idx_hbm.at[pl.ds(off, SC_CHUNK)], idx_v)
            pltpu.async_copy(table_hbm.at[idx_v], rows_v, sem).wait()
            pltpu.sync_copy(rows_v, out_hbm.at[pl.ds(off, SC_CHUNK)])

    return gather(table, idx)


def _scatter_rows(rows, dest):
    n_tok, width = rows.shape
    n_dst = dest.shape[0]
    per_worker = n_tok // SC_WORKERS
    n_chunks = per_worker // SC_CHUNK
    assert per_worker * SC_WORKERS == n_tok and n_chunks * SC_CHUNK == per_worker
    mesh = plsc.VectorSubcoreMesh(core_axis_name="c", subcore_axis_name="s")

    @functools.partial(
        pl.kernel, mesh=mesh,
        out_type=jax.ShapeDtypeStruct((n_dst * n_tok, width), rows.dtype),
        scratch_types=[pltpu.VMEM((n_dst, SC_CHUNK), I32), pltpu.VMEM((SC_CHUNK, width), rows.dtype),
                       pltpu.SemaphoreType.DMA])
    def scatter(rows_hbm, dest_hbm, out_hbm, idx_v, rows_v, sem):
        wid = lax.axis_index("s") * SC_CORES + lax.axis_index("c")
        base = wid * per_worker

        @pl.loop(0, n_chunks)
        def _(j):
            off = base + j * SC_CHUNK
            pltpu.sync_copy(dest_hbm.at[:, pl.ds(off, SC_CHUNK)], idx_v)
            pltpu.sync_copy(rows_hbm.at[pl.ds(off, SC_CHUNK)], rows_v)
            copies = [pltpu.async_copy(rows_v, out_hbm.at[idx_v.at[k]], sem) for k in range(n_dst)]
            for c in copies:
                c.wait()

    return scatter(rows, dest)


def _expert_kernel(blk_ref, exp_ref, lo_ref, hi_ref, slot_ref, nxt_ref, xs_hbm, wg_hbm, wu_hbm, wd_hbm,
                   ys_hbm, wg_buf, wu_buf, wd_buf, wgu_s, wd_s, sem, xs_buf, xs_sem, ys_buf, ys_sem,
                   *, layer):
    i = pl.program_id(0)
    prev = jnp.maximum(i - 1, 0)

    def weight_copies(expert, slot):
        return [pltpu.make_async_copy(src.at[layer, expert], dst.at[slot], sem.at[slot])
                for src, dst in ((wg_hbm, wg_buf), (wu_hbm, wu_buf), (wd_hbm, wd_buf))]

    @pl.when(i == 0)
    def _():
        for c in weight_copies(exp_ref[0], slot_ref[0]):
            c.start()

    @pl.when(jnp.logical_or(i == 0, exp_ref[i] != exp_ref[prev]))
    def _():
        slot = slot_ref[i]
        for c in weight_copies(exp_ref[i], slot):
            c.wait()
        wgu_s[:, :EXPERT_DIM] = wg_buf[slot].astype(BF16)
        wgu_s[:, EXPERT_DIM:] = wu_buf[slot].astype(BF16)
        wd_s[...] = wd_buf[slot].astype(BF16)

        @pl.when(nxt_ref[i] >= 0)
        def _():
            for c in weight_copies(nxt_ref[i], 1 - slot):
                c.start()

    n_blocks = xs_hbm.shape[0] // EXPERT_ROWS
    blk = blk_ref[i]
    first = jnp.logical_or(i == 0, blk != blk_ref[prev])

    def rows_copy(block):
        slot = block % EXPERT_XS_SLOTS
        src = xs_hbm.at[pl.ds(pl.multiple_of(block * EXPERT_ROWS, EXPERT_ROWS), EXPERT_ROWS)]
        return pltpu.make_async_copy(src, xs_buf.at[slot], xs_sem.at[slot])

    @pl.when(i == 0)
    def _():
        for b0 in range(min(EXPERT_XS_SLOTS - 1, n_blocks)):
            rows_copy(b0).start()

    def result_copy(block):
        slot = block % EXPERT_XS_SLOTS
        dst = ys_hbm.at[pl.ds(pl.multiple_of(block * EXPERT_ROWS, EXPERT_ROWS), EXPERT_ROWS)]
        return pltpu.make_async_copy(ys_buf.at[slot], dst, ys_sem.at[slot])

    @pl.when(first)
    def _():
        rows_copy(blk).wait()

        @pl.when(blk + EXPERT_XS_SLOTS - 1 < n_blocks)
        def _():
            rows_copy(blk + EXPERT_XS_SLOTS - 1).start()

        @pl.when(blk >= EXPERT_XS_SLOTS)
        def _():
            result_copy(blk - EXPERT_XS_SLOTS).wait()

    xs_ref = xs_buf.at[blk % EXPERT_XS_SLOTS]
    ys_ref = ys_buf.at[blk % EXPERT_XS_SLOTS]
    lo = lo_ref[i]
    hi = hi_ref[i]

    def sub_block(r0):
        rows = slice(r0, r0 + EXPERT_SUB_ROWS)
        w = xs_ref[rows, :]
        xlo = lax.bitcast_convert_type(w.astype(jnp.int16), BF16)
        xhi = lax.bitcast_convert_type(lax.shift_right_logical(w, 16).astype(jnp.int16), BF16)
        gu = _dot(xlo, wgu_s[:HALF, :]) + _dot(xhi, wgu_s[HALF:, :])
        g = gu[:, :EXPERT_DIM]
        hb = (g * jax.nn.sigmoid(g) * gu[:, EXPERT_DIM:]).astype(BF16)
        y = _dot(hb, wd_s[...])
        packed = pltpu.pack_elementwise([y[:, :HALF], y[:, HALF:]], packed_dtype=BF16)
        row = r0 + lax.broadcasted_iota(I32, (EXPERT_SUB_ROWS, 1), 0)
        mine = jnp.logical_and(row >= lo, row < hi)
        kept = jnp.where(first, 0, ys_ref[rows, :])
        ys_ref[rows, :] = jnp.where(mine, packed, kept)

    def touched(r0):
        return jnp.logical_and(lo < r0 + EXPERT_SUB_ROWS, hi > r0)

    for r0 in range(0, EXPERT_ROWS, 2 * EXPERT_SUB_ROWS):
        r1 = r0 + EXPERT_SUB_ROWS
        t0, t1 = touched(r0), touched(r1)

        @pl.when(jnp.logical_and(t0, t1))
        def _():
            sub_block(r0)
            sub_block(r1)

        pl.when(jnp.logical_and(t0, jnp.logical_not(t1)))(functools.partial(sub_block, r0))
        pl.when(jnp.logical_and(jnp.logical_not(t0), t1))(functools.partial(sub_block, r1))

    n_items = pl.num_programs(0)
    final = i == n_items - 1
    block_done = jnp.logical_or(final, blk_ref[jnp.minimum(i + 1, n_items - 1)] != blk)

    @pl.when(block_done)
    def _():
        result_copy(blk).start()

    @pl.when(final)
    def _():
        for b0 in range(max(n_blocks - EXPERT_XS_SLOTS, 0), n_blocks):
            result_copy(b0).wait()


def _expert_items(counts, n_rows):
    n_blocks = n_rows // EXPERT_ROWS
    n_items = n_blocks + N_EXPERTS - 1
    end = jnp.cumsum(counts)
    start = end - counts
    first_blk = start // EXPERT_ROWS
    n_blk = jnp.where(counts > 0, (end - 1) // EXPERT_ROWS - first_blk + 1, 0)
    item_end = jnp.cumsum(n_blk)
    item_start = item_end - n_blk
    slot = jnp.arange(n_items, dtype=I32)
    e = jnp.minimum(jnp.sum((item_end[None, :] <= slot[:, None]).astype(I32), axis=1), N_EXPERTS - 1)
    onehot = (e[:, None] == jnp.arange(N_EXPERTS, dtype=I32)[None, :]).astype(I32)
    pick = lambda v: jnp.sum(onehot * v[None, :], axis=1)
    valid = slot < item_end[-1]
    blk = jnp.where(valid, pick(first_blk) + slot - pick(item_start), n_blocks - 1)
    lo = jnp.clip(pick(start) - blk * EXPERT_ROWS, 0, EXPERT_ROWS)
    hi = jnp.clip(pick(end) - blk * EXPERT_ROWS, 0, EXPERT_ROWS)
    last_e = jnp.max(jnp.where(counts > 0, jnp.arange(N_EXPERTS, dtype=I32), 0))
    e = jnp.where(valid, e, last_e)
    hi = jnp.where(valid, hi, 0)
    lo = jnp.where(valid, lo, 0)
    change = jnp.concatenate([jnp.ones((1,), I32), (e[1:] != e[:-1]).astype(I32)])
    slot = (jnp.cumsum(change) - 1) % 2
    later = jnp.where(jnp.arange(N_EXPERTS, dtype=I32)[None, :] > e[:, None], counts[None, :] > 0, False)
    nxt = jnp.where(jnp.any(later, axis=1), jnp.argmax(later, axis=1), -1)
    return tuple(a.astype(I32) for a in (blk, e, lo, hi, slot, nxt))


def _experts(xs, items, w_gate, w_up, w_down, layer):
    n_rows = xs.shape[0]
    n_items = items[0].shape[0]
    hbm = pl.BlockSpec(memory_space=pl.ANY)
    grid_spec = pltpu.PrefetchScalarGridSpec(
        num_scalar_prefetch=len(items), grid=(n_items,),
        in_specs=[hbm, hbm, hbm, hbm],
        out_specs=hbm,
        scratch_shapes=[pltpu.VMEM((2, D_MODEL, EXPERT_DIM), F32),
                        pltpu.VMEM((2, D_MODEL, EXPERT_DIM), F32),
                        pltpu.VMEM((2, EXPERT_DIM, D_MODEL), F32),
                        pltpu.VMEM((D_MODEL, 2 * EXPERT_DIM), BF16),
                        pltpu.VMEM((EXPERT_DIM, D_MODEL), BF16),
                        pltpu.SemaphoreType.DMA((2,)),
                        pltpu.VMEM((EXPERT_XS_SLOTS, EXPERT_ROWS, HALF), I32),
                        pltpu.SemaphoreType.DMA((EXPERT_XS_SLOTS,)),
                        pltpu.VMEM((EXPERT_XS_SLOTS, EXPERT_ROWS, HALF), I32),
                        pltpu.SemaphoreType.DMA((EXPERT_XS_SLOTS,))])
    return pl.pallas_call(
        functools.partial(_expert_kernel, layer=layer), grid_spec=grid_spec,
        out_shape=jax.ShapeDtypeStruct((n_rows, HALF), I32),
        compiler_params=_cparams("arbitrary"), name="experts")(*items, xs, w_gate, w_up, w_down)


def _moe_out_kernel(x_ref, yg_ref, wt_ref, sgu_ref, sd_ref, g_ref, b_ref, *rest):
    o_ref = rest[-1]
    x = x_ref[...]
    wt = wt_ref[...]
    lo = jnp.zeros((x.shape[0], HALF), F32)
    hi = jnp.zeros((x.shape[0], HALF), F32)
    for k in range(TOP_K):
        w = yg_ref[k]
        wk = wt[:, k:k + 1]
        lo = lo + wk * _unpack_lo(w)
        hi = hi + wk * _unpack_hi(w)
    gu = _dot(x.astype(BF16), sgu_ref[...])
    g = gu[:, :EXPERT_DIM]
    hs = (g * jax.nn.sigmoid(g) * gu[:, EXPERT_DIM:]).astype(BF16)
    ffn = jnp.concatenate([lo, hi], axis=1) + _dot(hs, sd_ref[...])
    o_ref[...] = _ln(DEEPNORM_ALPHA * x + ffn, g_ref[...], b_ref[...])


def _moe_out(x, yg, wts, sh_gate, sh_up, sh_down, g, b, first_tile, partial_out):
    T, D = x.shape
    tm = ROW_TILE
    n_tiles = yg.shape[1] // tm
    xrow = pl.BlockSpec((tm, D), lambda i: (i + first_tile, 0))
    full = lambda a: pl.BlockSpec(a.shape, lambda i: (0,) * a.ndim)
    sgu = jnp.concatenate([sh_gate, sh_up], axis=1).astype(BF16)
    sd = sh_down.astype(BF16)
    g = g.reshape(1, D)
    b = b.reshape(1, D)
    args = [x, yg, wts, sgu, sd, g, b]
    in_specs = [xrow, pl.BlockSpec((TOP_K, tm, HALF), lambda i: (0, i, 0)),
                pl.BlockSpec((tm, TOP_K), lambda i: (i, 0)), full(sgu), full(sd), full(g), full(b)]
    aliases = {}
    if partial_out is not None:
        args.append(partial_out)
        in_specs.append(pl.BlockSpec(memory_space=pl.ANY))
        aliases = {len(args) - 1: 0}
    return pl.pallas_call(
        _moe_out_kernel, grid=(n_tiles,), in_specs=in_specs,
        out_specs=xrow, out_shape=jax.ShapeDtypeStruct((T, D), F32),
        input_output_aliases=aliases,
        compiler_params=_cparams("parallel"), name="moe_out")(*args)


def _moe(x, router_w, router_b, w_gate, w_up, w_down, layer, sh_gate, sh_up, sh_down, g, b):
    T = x.shape[0]
    tiles = T // ROW_TILE // MOE_TOKEN_GROUPS
    tg = tiles * ROW_TILE
    out = None
    for grp in range(MOE_TOKEN_GROUPS):
        eidx, wts, pos, cnt, xp = _router(x, router_w, router_b, grp * tg, tg)
        counts = cnt[:, 0]
        seg_start = (jnp.cumsum(counts) - counts).astype(I32)
        dest = _dest_rows(eidx, pos, seg_start)
        xs = _scatter_rows(xp, dest)
        ys = _experts(xs, _expert_items(counts, tg * TOP_K), w_gate, w_up, w_down, layer)
        yg = _gather_rows(ys, dest.reshape(tg * TOP_K)).reshape(TOP_K, tg, HALF)
        out = _moe_out(x, yg, wts.T, sh_gate, sh_up, sh_down, g, b, grp * tiles, out)
    return out


def kernel(x, ln_in_g, ln_in_b, e_w_in, e_w_fourier, e_q_gain, e_k_gain, e_w_out, o_w_in, o_b_in, o_v_ln_g, o_v_ln_b, o_w_spatial, o_b_spatial, o_w_out, ln_mix_g, ln_mix_b, ln_ffn_g, ln_ffn_b, router_w, router_b, exp_w_gate, exp_w_up, exp_w_down, sh_w_gate, sh_w_up, sh_w_down):
    B, S, D = x.shape
    T = B * S
    h = x.reshape(T, D)
    for i in range(DEPTH):
        j = i // 2
        if i == 0:
            a, qt, k2, vt, h = _even_in(h, e_w_in[j], e_q_gain[j], e_k_gain[j], B, S, (ln_in_g, ln_in_b))
        elif i % 2 == 0:
            a, qt, k2, vt = _even_in(h, e_w_in[j], e_q_gain[j], e_k_gain[j], B, S)
        if i % 2 == 0:
            a_out = _fourier(a, e_w_fourier[j], B, S)
            attn = _attention(qt, k2, vt, B, S)
            h = _even_out(a_out, attn, e_w_out[j], h, ln_mix_g[i], ln_mix_b[i])
        else:
            h = _odd_mixer(h, o_w_in[j], o_b_in[j], o_v_ln_g[j], o_v_ln_b[j], o_w_spatial[j],
                           o_b_spatial[j], o_w_out[j], ln_mix_g[i], ln_mix_b[i])
        h = _moe(h, router_w[i], router_b[i], exp_w_gate, exp_w_up, exp_w_down, i,
                 sh_w_gate[i], sh_w_up[i], sh_w_down[i], ln_ffn_g[i], ln_ffn_b[i])
    return h.reshape(B, S, D)
```

```python
import functools
import math

import numpy as np
import jax
import jax.numpy as jnp
from jax import lax
from jax.experimental import pallas as pl
from jax.experimental.pallas import tpu as pltpu
from jax.experimental.pallas import tpu_sc as plsc

F32 = jnp.float32
BF16 = jnp.bfloat16
I32 = jnp.int32

D_MODEL = 1024
DEPTH = 4
GRID_W = 64
N_FGROUPS = 4
FGROUP_DIM = 128
F_WIDTH = N_FGROUPS * FGROUP_DIM
N_HEADS = 8
N_KV_HEADS = 2
HEAD_DIM = 64
Q_GROUP = N_HEADS // N_KV_HEADS
Q_WIDTH = N_HEADS * HEAD_DIM
KV_WIDTH = N_KV_HEADS * HEAD_DIM
ROPE_THETA = 10000.0
ROPE_PAIRS = HEAD_DIM // 4
CHUNK = 128
N_CGROUPS = 8
CGROUP_DIM = D_MODEL // N_CGROUPS
C_WIDTH = N_CGROUPS * CGROUP_DIM
N_EXPERTS = 64
EXPERT_DIM = 256
TOP_K = 8
N_EXPERT_GROUPS = 8
GROUP_SIZE = N_EXPERTS // N_EXPERT_GROUPS
TOPK_GROUPS = 4
ROUTE_SCALE = 2.5
LN_EPS = 1e-5
QK_EPS = 1e-6
DEEPNORM_ALPHA = (2 * DEPTH) ** 0.25

LANES = 128
SUBLANES = 8
VMEM_LIMIT_BYTES = 56 * 1024 * 1024
ROW_TILE = 512
DFT_N1 = 64
DFT_KRON = 4
DFT_PITCH_PAD = 8
ROUTER_TILE = 1024
EXPERT_ROWS = 2048
EXPERT_XS_SLOTS = 3
EXPERT_SUB_ROWS = 512
HALF = D_MODEL // 2
SC_CORES = 2
SC_SUBCORES = 16
SC_WORKERS = SC_CORES * SC_SUBCORES
SC_CHUNK = 128
MOE_TOKEN_GROUPS = 2
ATT_TQ = 256
ATT_TK = 256
ATT_V_ROWS = -(-(HEAD_DIM + 1) // 16) * 16
ATT_BOUND_SLACK = 1.0 + 2.0 ** -7
ATT_MIN_ROW_SUM = 2.0 ** -80
NEG_INF = float("-inf")


def _cparams(*sem):
    return pltpu.CompilerParams(dimension_semantics=sem, vmem_limit_bytes=VMEM_LIMIT_BYTES)


def _ln(x, g, b):
    mu = jnp.mean(x, axis=-1, keepdims=True)
    xc = x - mu
    var = jnp.mean(xc * xc, axis=-1, keepdims=True)
    return xc * lax.rsqrt(var + LN_EPS) * g + b


def _dot(a, b):
    return jnp.dot(a, b, preferred_element_type=F32)


def _unpack_lo(w):
    return lax.bitcast_convert_type(lax.shift_left(w, 16), F32)


def _unpack_hi(w):
    return lax.bitcast_convert_type(w & jnp.int32(-65536), F32)


def _even_in_kernel(x_ref, w_ref, qm_ref, km_ref, qg_ref, kg_ref, cos_ref, sin_ref,
                    *rest, input_ln):
    if input_ln:
        lg_ref, lb_ref, a_ref, qt_ref, k_ref, vt_ref, xn_ref = rest
        x = _ln(x_ref[...], lg_ref[...], lb_ref[...])
        xn_ref[...] = x
    else:
        a_ref, qt_ref, k_ref, vt_ref = rest
        x = x_ref[...]
    tm = x_ref.shape[0]
    h = _dot(x.astype(BF16), w_ref[...])
    a_ref[...] = h[:, :F_WIDTH].astype(BF16)
    q = h[:, F_WIDTH:F_WIDTH + Q_WIDTH]
    k = h[:, F_WIDTH + Q_WIDTH:F_WIDTH + Q_WIDTH + KV_WIDTH]
    v = h[:, F_WIDTH + Q_WIDTH + KV_WIDTH:]
    cos = cos_ref[...]
    sin = sin_ref[...]
    lane = lax.broadcasted_iota(I32, (tm, LANES), 1)
    first_of_pair = (lane & ROPE_PAIRS) == 0

    def mean_sq(xf, m_ref):
        sq = xf * xf
        hi = sq.astype(BF16)
        lo = (sq - hi.astype(F32)).astype(BF16)
        return _dot(hi, m_ref[...]) + _dot(lo, m_ref[...])

    def rope(xn):
        sw = jnp.where(first_of_pair, pltpu.roll(xn, LANES - ROPE_PAIRS, 1), pltpu.roll(xn, ROPE_PAIRS, 1))
        return xn * cos + sw * sin

    qn = q * lax.rsqrt(mean_sq(q, qm_ref) + QK_EPS) * qg_ref[...]
    scale = math.log2(math.e) / math.sqrt(HEAD_DIM)
    for c in range(Q_WIDTH // LANES):
        qt_ref[c * LANES:(c + 1) * LANES, :] = (rope(qn[:, c * LANES:(c + 1) * LANES]) * scale).T.astype(BF16)
    kn = rope(k * lax.rsqrt(mean_sq(k, km_ref) + QK_EPS) * kg_ref[...])
    low = lane < HEAD_DIM
    k_ref[0] = jnp.where(low, kn, 0.0).astype(BF16)
    k_ref[1] = jnp.where(low, pltpu.roll(kn, HEAD_DIM, 1), 0.0).astype(BF16)
    ones_col = jnp.where(lane == HEAD_DIM, 1.0, 0.0)
    vt_ref[0:LANES, :] = jnp.where(low, v, ones_col).T.astype(BF16)
    vt_ref[LANES:2 * LANES, :] = jnp.where(low, pltpu.roll(v, HEAD_DIM, 1), ones_col).T.astype(BF16)


def _rope_tables(S):
    t = np.arange(S)
    inv = ROPE_THETA ** (-np.arange(ROPE_PAIRS, dtype=np.float64) / ROPE_PAIRS)
    ang_r = (t // GRID_W)[:, None] * inv
    ang_c = (t % GRID_W)[:, None] * inv
    cos = np.concatenate([np.cos(ang_r), np.cos(ang_r), np.cos(ang_c), np.cos(ang_c)], axis=1)
    sin = np.concatenate([-np.sin(ang_r), np.sin(ang_r), -np.sin(ang_c), np.sin(ang_c)], axis=1)
    return (jnp.asarray(np.tile(cos, (1, 2)), F32), jnp.asarray(np.tile(sin, (1, 2)), F32))


def _head_mean_matrix(width):
    m = np.kron(np.eye(width // HEAD_DIM), np.full((HEAD_DIM, HEAD_DIM), 1.0 / HEAD_DIM))
    return jnp.asarray(m, BF16)


def _even_in(x, w_in, q_gain, k_gain, B, S, input_ln=None):
    T, D = x.shape
    tm = ROW_TILE
    ns = S // tm
    cos, sin = _rope_tables(S)
    row = lambda w: pl.BlockSpec((tm, w), lambda i: (i, 0))
    full = lambda a: pl.BlockSpec(a.shape, lambda i: (0,) * a.ndim)
    tab = pl.BlockSpec((tm, LANES), lambda i: (i % ns, 0))
    w = w_in.astype(BF16)
    qm = _head_mean_matrix(Q_WIDTH)
    km = _head_mean_matrix(KV_WIDTH)
    qg = jnp.tile(q_gain.astype(F32), N_HEADS).reshape(1, Q_WIDTH)
    kg = jnp.tile(k_gain.astype(F32), N_KV_HEADS).reshape(1, KV_WIDTH)
    args = [x, w, qm, km, qg, kg, cos, sin]
    in_specs = [row(D), full(w), full(qm), full(km), full(qg), full(kg), tab, tab]
    out_specs = [row(F_WIDTH),
                 pl.BlockSpec((None, Q_WIDTH, tm), lambda i: (i // ns, 0, i % ns)),
                 pl.BlockSpec((N_KV_HEADS, tm, LANES), lambda i: (0, i, 0)),
                 pl.BlockSpec((None, N_KV_HEADS * LANES, tm), lambda i: (i // ns, 0, i % ns))]
    out_shape = [jax.ShapeDtypeStruct((T, F_WIDTH), BF16),
                 jax.ShapeDtypeStruct((B, Q_WIDTH, S), BF16),
                 jax.ShapeDtypeStruct((N_KV_HEADS, T, LANES), BF16),
                 jax.ShapeDtypeStruct((B, N_KV_HEADS * LANES, S), BF16)]
    if input_ln is not None:
        ln = [p.reshape(1, D) for p in input_ln]
        args += ln
        in_specs += [full(p) for p in ln]
        out_specs.append(row(D))
        out_shape.append(jax.ShapeDtypeStruct((T, D), F32))
    return pl.pallas_call(
        functools.partial(_even_in_kernel, input_ln=input_ln is not None), grid=(T // tm,),
        in_specs=in_specs, out_specs=out_specs, out_shape=out_shape,
        compiler_params=_cparams("parallel"), name="even_in")(*args)


def _fourier_kernel(a_ref, dftc_ref, taba_ref, kc_ref, ks_ref, wf_ref, o_ref,
                    zr_ref, zi_ref, ur_ref, ui_ref, y_ref):
    S = a_ref.shape[0]
    n1_count = DFT_N1
    n2_count = S // DFT_N1
    pz = n1_count + DFT_PITCH_PAD
    pu = n2_count + DFT_PITCH_PAD
    blk = DFT_KRON * DFT_N1
    scale = 1.0 / math.sqrt(S * FGROUP_DIM)

    def channel_dft(j, carry):
        zz = _dot(a_ref[pl.ds(pl.multiple_of(j * blk, blk), blk), :], dftc_ref[...])
        for q in range(DFT_KRON):
            dst = pl.ds(pl.multiple_of((j * DFT_KRON + q) * pz, 8), n1_count)
            zr_ref[dst, :] = zz[q * n1_count:(q + 1) * n1_count, :FGROUP_DIM]
            zi_ref[dst, :] = zz[q * n1_count:(q + 1) * n1_count, FGROUP_DIM:]
        return carry

    lax.fori_loop(0, S // blk, channel_dft, 0, unroll=4)

    def stage_a(n1, carry):
        src = pl.ds(n1, n2_count, stride=pz)
        zn = jnp.concatenate([zr_ref[src, :], zi_ref[src, :]], axis=1).astype(BF16)
        r = _dot(taba_ref[n1], zn)
        dst = pl.ds(pl.multiple_of(n1 * pu, 8), n2_count)
        ur_ref[dst, :] = r[:n2_count, :FGROUP_DIM] + r[n2_count:, FGROUP_DIM:]
        ui_ref[dst, :] = r[:n2_count, FGROUP_DIM:] - r[n2_count:, :FGROUP_DIM]
        return carry

    lax.fori_loop(0, n1_count, stage_a, 0, unroll=8)

    def stage_b(j, carry):
        srcs = [pl.ds(j * DFT_KRON + q, n1_count, stride=pu) for q in range(DFT_KRON)]
        ur = jnp.concatenate([ur_ref[s, :] for s in srcs], axis=0).astype(BF16)
        ui = jnp.concatenate([ui_ref[s, :] for s in srcs], axis=0).astype(BF16)
        re = _dot(kc_ref[...], ur) + _dot(ks_ref[...], ui)
        out = _dot((re * scale).astype(BF16), wf_ref[...])
        for q in range(DFT_KRON):
            y_ref[srcs[q], :] = out[q * n1_count:(q + 1) * n1_count]
        return carry

    lax.fori_loop(0, S // blk, stage_b, 0, unroll=8)

    def compact(k1, carry):
        o_ref[pl.ds(pl.multiple_of(k1 * n2_count, n2_count), n2_count), :] = (
            y_ref[pl.ds(pl.multiple_of(k1 * pu, 8), n2_count), :].astype(BF16))
        return carry

    lax.fori_loop(0, n1_count, compact, 0)


def _dft_tables(S):
    n1c, n2c = DFT_N1, S // DFT_N1
    c = np.arange(FGROUP_DIM)
    ang = 2 * np.pi * np.outer(c, c) / FGROUP_DIM
    dftc = np.concatenate([np.cos(ang), -np.sin(ang)], axis=1)
    n1 = np.arange(n1c)[:, None, None]
    k2 = np.arange(n2c)[None, :, None]
    n2 = np.arange(n2c)[None, None, :]
    th = 2 * np.pi * (n2 * k2 / n2c + n1 * k2 / S)
    taba = np.concatenate([np.cos(th), np.sin(th)], axis=1)
    k1 = np.arange(n1c)
    g = 2 * np.pi * np.outer(k1, k1) / n1c
    eye = np.eye(DFT_KRON)
    kc = np.kron(eye, np.cos(g))
    ks = np.kron(eye, np.sin(g))
    return tuple(jnp.asarray(t, BF16) for t in (dftc, taba, kc, ks))


def _fourier(a, w_fourier, B, S):
    T = a.shape[0]
    dftc, taba, kc, ks = _dft_tables(S)
    full = lambda t: pl.BlockSpec(t.shape, lambda b, g: (0,) * t.ndim)
    blk = pl.BlockSpec((S, FGROUP_DIM), lambda b, g: (b, g))
    return pl.pallas_call(
        _fourier_kernel, grid=(B, N_FGROUPS),
        in_specs=[blk, full(dftc), full(taba), full(kc), full(ks),
                  pl.BlockSpec((None, FGROUP_DIM, FGROUP_DIM), lambda b, g: (g, 0, 0))],
        out_specs=blk,
        out_shape=jax.ShapeDtypeStruct((T, F_WIDTH), BF16),
        scratch_shapes=(
            [pltpu.VMEM((S // DFT_N1 * (DFT_N1 + DFT_PITCH_PAD), FGROUP_DIM), F32)] * 2
            + [pltpu.VMEM((DFT_N1 * (S // DFT_N1 + DFT_PITCH_PAD), FGROUP_DIM), F32)] * 3),
        compiler_params=_cparams("parallel", "parallel"), name="fourier")(
            a, dftc, taba, kc, ks, w_fourier.astype(BF16))


def _attn_kernel(qt_ref, k_ref, vt_ref, o_ref, qs_ref, kmax_ref, acc_ref, m_ref, s0_ref, s1_ref,
                 p0_ref, p1_ref):
    tq = qt_ref.shape[1]
    n_keys = k_ref.shape[0]
    tk = min(ATT_TK, n_keys)
    n_chunks = n_keys // tk
    assert n_chunks % 2 == 0 and n_chunks * tk == n_keys

    def keys(c):
        return k_ref[pl.ds(pl.multiple_of(c * tk, tk), tk), :]

    @pl.when(pl.program_id(2) == 0)
    def _():
        def body(c, best):
            k = keys(c).astype(F32)
            return jnp.maximum(best, jnp.sum(k * k, axis=1, keepdims=True))
        best = lax.fori_loop(0, n_chunks, body, jnp.zeros((tk, 1), F32))
        kmax_ref[...] = jnp.broadcast_to(jnp.sqrt(jnp.max(best, axis=0, keepdims=True)), kmax_ref.shape)

    qs_ref[HEAD_DIM:, :] = jnp.zeros((LANES - HEAD_DIM, Q_GROUP * tq), BF16)
    for g in range(Q_GROUP):
        qs_ref[:HEAD_DIM, g * tq:(g + 1) * tq] = qt_ref[g * HEAD_DIM:(g + 1) * HEAD_DIM, :]
    qf = qs_ref[...].astype(F32)
    bound = jnp.sqrt(jnp.sum(qf * qf, axis=0, keepdims=True)) * kmax_ref[0:1, 0:1] * ATT_BOUND_SLACK

    def scores(c):
        return _dot(keys(c), qs_ref[...])

    def values(c):
        return vt_ref[:ATT_V_ROWS, pl.ds(pl.multiple_of(c * tk, tk), tk)]

    def weights(s_buf):
        return jnp.exp2(s_buf[...] - bound).astype(BF16)

    def accumulate(p_buf, c):
        acc_ref[...] += _dot(values(c), p_buf[...])

    last = n_chunks - 1

    def fast(c2, carry):
        c = 2 * c2
        s0_ref[...] = scores(jnp.minimum(c + 2, last))
        p1_ref[...] = weights(s1_ref)
        accumulate(p0_ref, c)
        s1_ref[...] = scores(jnp.minimum(c + 3, last))
        p0_ref[...] = weights(s0_ref)
        accumulate(p1_ref, c + 1)
        return carry

    acc_ref[...] = jnp.zeros(acc_ref.shape, F32)
    s0_ref[...] = scores(0)
    p0_ref[...] = weights(s0_ref)
    s1_ref[...] = scores(1)
    lax.fori_loop(0, n_chunks // 2, fast, 0, unroll=True)
    underflow = jnp.min(acc_ref[HEAD_DIM:HEAD_DIM + 1, :]) < ATT_MIN_ROW_SUM

    @pl.when(underflow)
    def _():
        def safe(c, carry):
            s = scores(c)
            m_old = m_ref[...]
            m_new = jnp.maximum(m_old, jnp.max(s, axis=0, keepdims=True))
            acc_ref[...] = (jnp.exp2(m_old - m_new) * acc_ref[...]
                            + _dot(values(c), jnp.exp2(s - m_new).astype(BF16)))
            m_ref[...] = m_new
            return carry

        m_ref[...] = jnp.full(m_ref.shape, NEG_INF, F32)
        acc_ref[...] = jnp.zeros(acc_ref.shape, F32)
        lax.fori_loop(0, n_chunks, safe, 0)

    acc = acc_ref[...]
    ot = acc[:HEAD_DIM, :] / acc[HEAD_DIM:HEAD_DIM + 1, :]
    ot = jnp.concatenate([ot, jnp.zeros((LANES - HEAD_DIM, Q_GROUP * tq), F32)], axis=0)
    o = ot.T
    o_ref[...] = jnp.concatenate([o[g * tq:(g + 1) * tq, :HEAD_DIM] for g in range(Q_GROUP)],
                                 axis=1).astype(BF16)


def _attention(qt, k2, vt, B, S):
    T = B * S
    tq = ATT_TQ
    nq = S // tq
    gw = Q_GROUP * HEAD_DIM
    cols = Q_GROUP * tq
    tk = min(ATT_TK, S)
    return pl.pallas_call(
        _attn_kernel, grid=(B, N_KV_HEADS, nq),
        in_specs=[pl.BlockSpec((None, gw, tq), lambda b, h, i: (b, h, i)),
                  pl.BlockSpec((None, None, S, LANES), lambda b, h, i: (h, b, 0, 0)),
                  pl.BlockSpec((None, LANES, S), lambda b, h, i: (b, h, 0))],
        out_specs=pl.BlockSpec((tq, gw), lambda b, h, i: (b * nq + i, h)),
        out_shape=jax.ShapeDtypeStruct((T, Q_WIDTH), BF16),
        scratch_shapes=[pltpu.VMEM((LANES, cols), BF16), pltpu.VMEM((SUBLANES, LANES), F32),
                        pltpu.VMEM((ATT_V_ROWS, cols), F32), pltpu.VMEM((1, cols), F32),
                        pltpu.VMEM((tk, cols), F32), pltpu.VMEM((tk, cols), F32),
                        pltpu.VMEM((tk, cols), BF16), pltpu.VMEM((tk, cols), BF16)],
        compiler_params=_cparams("parallel", "parallel", "arbitrary"),
        name="attention")(qt, k2.reshape(N_KV_HEADS, B, S, LANES), vt)


def _even_out_kernel(a_ref, t_ref, wa_ref, wt_ref, x_ref, g_ref, b_ref, o_ref):
    mix = _dot(a_ref[...], wa_ref[...]) + _dot(t_ref[...], wt_ref[...])
    o_ref[...] = _ln(DEEPNORM_ALPHA * x_ref[...] + mix, g_ref[...], b_ref[...])


def _even_out(a_out, attn, w_out, x, g, b):
    T, D = x.shape
    tm = ROW_TILE
    row = lambda w: pl.BlockSpec((tm, w), lambda i: (i, 0))
    full = lambda a: pl.BlockSpec(a.shape, lambda i: (0,) * a.ndim)
    wa = w_out[:F_WIDTH].astype(BF16)
    wt = w_out[F_WIDTH:].astype(BF16)
    g = g.reshape(1, D)
    b = b.reshape(1, D)
    return pl.pallas_call(
        _even_out_kernel, grid=(T // tm,),
        in_specs=[row(F_WIDTH), row(Q_WIDTH), full(wa), full(wt), row(D), full(g), full(b)],
        out_specs=row(D), out_shape=jax.ShapeDtypeStruct((T, D), F32),
        compiler_params=_cparams("parallel"), name="even_out")(a_out, attn, wa, wt, x, g, b)


def _odd_kernel(x_ref, wi_ref, bi_ref, vg_ref, vb_ref, ws_ref, bs_ref, wo_ref, g_ref, b_ref, o_ref,
                gate_ref):
    tm = x_ref.shape[0]
    x = x_ref[...]
    xb = x.astype(BF16)

    def gelu_half(lo):
        h = _dot(xb, wi_ref[:, lo:lo + C_WIDTH]) + bi_ref[:, lo:lo + C_WIDTH]
        return 0.5 * h * (1.0 + lax.erf(h * (1.0 / math.sqrt(2.0))))

    v = _ln(gelu_half(C_WIDTH), vg_ref[...], vb_ref[...]).astype(BF16)
    u = gelu_half(0)
    for c in range(tm // CHUNK):
        r0 = c * CHUNK
        for gi in range(N_CGROUPS):
            l0 = gi * CGROUP_DIM
            sv = _dot(ws_ref[gi], v[r0:r0 + CHUNK, l0:l0 + CGROUP_DIM]) + bs_ref[gi]
            gate_ref[r0:r0 + CHUNK, l0:l0 + CGROUP_DIM] = (
                u[r0:r0 + CHUNK, l0:l0 + CGROUP_DIM] * sv).astype(BF16)
    mix = _dot(gate_ref[...], wo_ref[...])
    o_ref[...] = _ln(DEEPNORM_ALPHA * x + mix, g_ref[...], b_ref[...])


def _odd_mixer(x, w_in, b_in, v_g, v_b, w_s, b_s, w_out, g, b):
    T, D = x.shape
    tm = ROW_TILE
    row = pl.BlockSpec((tm, D), lambda i: (i, 0))
    full = lambda a: pl.BlockSpec(a.shape, lambda i: (0,) * a.ndim)
    args = [w_in.astype(BF16), b_in.reshape(1, 2 * C_WIDTH), v_g.reshape(1, C_WIDTH),
            v_b.reshape(1, C_WIDTH), w_s.astype(BF16),
            jnp.broadcast_to(b_s[:, :, None], (N_CGROUPS, CHUNK, CGROUP_DIM)).astype(F32),
            w_out.astype(BF16), g.reshape(1, D), b.reshape(1, D)]
    return pl.pallas_call(
        _odd_kernel, grid=(T // tm,),
        in_specs=[row] + [full(a) for a in args],
        out_specs=row, out_shape=jax.ShapeDtypeStruct((T, D), F32),
        scratch_shapes=[pltpu.VMEM((tm, C_WIDTH), BF16)],
        compiler_params=_cparams("parallel"), name="odd_mixer")(x, *args)


def _router_kernel(x_ref, w_ref, rb_ref, tri_ref, eidx_ref, wts_ref, pos_ref, cnt_ref, xp_ref, run_ref):
    tm = x_ref.shape[0]
    i = pl.program_id(0)

    @pl.when(i == 0)
    def _():
        run_ref[...] = jnp.zeros(run_ref.shape, F32)

    x = x_ref[...]
    xp_ref[...] = pltpu.pack_elementwise([x[:, :HALF], x[:, HALF:]], packed_dtype=BF16)
    xh = x.astype(BF16)
    xl = (x - xh.astype(F32)).astype(BF16)
    nt = (((1,), (1,)), ((), ()))
    dg = lambda a, c: lax.dot_general(a, c, nt, preferred_element_type=F32)
    logits = dg(w_ref[0], xh) + dg(w_ref[0], xl) + dg(w_ref[1], xh)
    scores = jax.nn.sigmoid(logits)
    sel = scores + rb_ref[...]

    i8 = lax.broadcasted_iota(I32, (GROUP_SIZE, tm), 0)
    gsc_rows = []
    for gidx in range(N_EXPERT_GROUPS):
        sg = sel[gidx * GROUP_SIZE:(gidx + 1) * GROUP_SIZE, :]
        m1 = jnp.max(sg, axis=0, keepdims=True)
        f1 = jnp.min(jnp.where(sg == m1, i8, GROUP_SIZE), axis=0, keepdims=True)
        m2 = jnp.max(jnp.where(i8 == f1, NEG_INF, sg), axis=0, keepdims=True)
        gsc_rows.append(m1 + m2)
    gsc = jnp.concatenate(gsc_rows, axis=0)

    gsel = jnp.zeros(gsc.shape, F32)
    for _ in range(TOPK_GROUPS):
        m = jnp.max(gsc, axis=0, keepdims=True)
        f = jnp.min(jnp.where(gsc == m, i8, N_EXPERT_GROUPS), axis=0, keepdims=True)
        pick = i8 == f
        gsel = jnp.where(pick, 1.0, gsel)
        gsc = jnp.where(pick, NEG_INF, gsc)
    esel = jnp.concatenate(
        [jnp.broadcast_to(gsel[gidx:gidx + 1, :], (GROUP_SIZE, tm)) for gidx in range(N_EXPERT_GROUPS)],
        axis=0)

    cur = jnp.where(esel > 0.0, sel, NEG_INF)
    ei = lax.broadcasted_iota(I32, cur.shape, 0)
    idx_rows, sc_rows = [], []
    chosen = jnp.zeros(cur.shape, F32)
    for _ in range(TOP_K):
        m = jnp.max(cur, axis=0, keepdims=True)
        f = jnp.min(jnp.where(cur == m, ei, N_EXPERTS), axis=0, keepdims=True)
        pick = ei == f
        idx_rows.append(f)
        sc_rows.append(jnp.sum(jnp.where(pick, scores, 0.0), axis=0, keepdims=True))
        chosen = jnp.where(pick, 1.0, chosen)
        cur = jnp.where(pick, NEG_INF, cur)
    eidx = jnp.concatenate(idx_rows, axis=0)
    sc = jnp.concatenate(sc_rows, axis=0)
    eidx_ref[...] = eidx
    wts_ref[...] = sc / jnp.sum(sc, axis=0, keepdims=True) * ROUTE_SCALE

    before = _dot(chosen.astype(BF16), tri_ref[...]) + run_ref[...]
    pos_rows = [jnp.sum(jnp.where(ei == idx_rows[k], before, 0.0), axis=0, keepdims=True)
                for k in range(TOP_K)]
    pos_ref[...] = jnp.concatenate(pos_rows, axis=0).astype(I32)
    run_new = run_ref[...] + jnp.sum(chosen, axis=1, keepdims=True)
    run_ref[...] = run_new
    cnt_ref[...] = jnp.broadcast_to(run_new, cnt_ref.shape).astype(I32)


def _router(x, router_w, router_b, first_row, T):
    D = x.shape[1]
    tm = min(ROUTER_TILE, T)
    n_tiles = T // tm
    first_tile = first_row // tm
    assert n_tiles * tm == T and first_tile * tm == first_row
    wt = router_w.T.astype(F32)
    wh = wt.astype(BF16)
    wl = (wt - wh.astype(F32)).astype(BF16)
    w2 = jnp.stack([wh, wl])
    rb = router_b.astype(F32).reshape(N_EXPERTS, 1)
    tri = jnp.asarray(np.triu(np.ones((tm, tm)), 1), BF16)
    full = lambda a: pl.BlockSpec(a.shape, lambda i: (0,) * a.ndim)
    col = pl.BlockSpec((TOP_K, tm), lambda i: (0, i))
    return pl.pallas_call(
        _router_kernel, grid=(n_tiles,),
        in_specs=[pl.BlockSpec((tm, D), lambda i: (i + first_tile, 0)), full(w2), full(rb), full(tri)],
        out_specs=[col, col, col, pl.BlockSpec((N_EXPERTS, LANES), lambda i: (0, 0)),
                   pl.BlockSpec((tm, HALF), lambda i: (i, 0))],
        out_shape=[jax.ShapeDtypeStruct((TOP_K, T), I32), jax.ShapeDtypeStruct((TOP_K, T), F32),
                   jax.ShapeDtypeStruct((TOP_K, T), I32), jax.ShapeDtypeStruct((N_EXPERTS, LANES), I32),
                   jax.ShapeDtypeStruct((T, HALF), I32)],
        scratch_shapes=[pltpu.VMEM((N_EXPERTS, 1), F32)],
        compiler_params=_cparams("arbitrary"), name="router")(x, w2, rb, tri)


def _dest_kernel(start_ref, eidx_ref, pos_ref, o_ref):
    e = eidx_ref[...]
    acc = pos_ref[...]
    for j in range(N_EXPERTS):
        acc = acc + jnp.where(e == j, start_ref[j], 0)
    o_ref[...] = acc


def _dest_rows(eidx, pos, seg_start):
    K, T = eidx.shape
    tl = min(T, 2048)
    blk = pl.BlockSpec((K, tl), lambda i, s: (0, i))
    grid_spec = pltpu.PrefetchScalarGridSpec(
        num_scalar_prefetch=1, grid=(T // tl,), in_specs=[blk, blk], out_specs=blk)
    return pl.pallas_call(
        _dest_kernel, grid_spec=grid_spec, out_shape=jax.ShapeDtypeStruct((K, T), I32),
        compiler_params=_cparams("parallel"), name="dest_rows")(seg_start, eidx, pos)


def _gather_rows(table, idx):
    n_rows = idx.shape[0]
    width = table.shape[1]
    per_worker = n_rows // SC_WORKERS
    n_chunks = per_worker // SC_CHUNK
    assert per_worker * SC_WORKERS == n_rows and n_chunks * SC_CHUNK == per_worker
    mesh = plsc.VectorSubcoreMesh(core_axis_name="c", subcore_axis_name="s")

    @functools.partial(
        pl.kernel, mesh=mesh,
        out_type=jax.ShapeDtypeStruct((n_rows, width), table.dtype),
        scratch_types=[pltpu.VMEM((SC_CHUNK,), I32), pltpu.VMEM((SC_CHUNK, width), table.dtype),
                       pltpu.SemaphoreType.DMA])
    def gather(table_hbm, idx_hbm, out_hbm, idx_v, rows_v, sem):
        wid = lax.axis_index("s") * SC_CORES + lax.axis_index("c")
        base = wid * per_worker

        @pl.loop(0, n_chunks)
        def _(j):
            off = base + j * SC_CHUNK
            pltpu.sync_copy(idx_hbm.at[pl.ds(off, SC_CHUNK)], idx_v)
            pltpu.async_copy(table_hbm.at[idx_v], rows_v, sem).wait()
            pltpu.sync_copy(rows_v, out_hbm.at[pl.ds(off, SC_CHUNK)])

    return gather(table, idx)


def _scatter_rows(rows, dest):
    n_tok, width = rows.shape
    n_dst = dest.shape[0]
    per_worker = n_tok // SC_WORKERS
    n_chunks = per_worker // SC_CHUNK
    assert per_worker * SC_WORKERS == n_tok and n_chunks * SC_CHUNK == per_worker
    mesh = plsc.VectorSubcoreMesh(core_axis_name="c", subcore_axis_name="s")

    @functools.partial(
        pl.kernel, mesh=mesh,
        out_type=jax.ShapeDtypeStruct((n_dst * n_tok, width), rows.dtype),
        scratch_types=[pltpu.VMEM((n_dst, SC_CHUNK), I32), pltpu.VMEM((SC_CHUNK, width), rows.dtype),
                       pltpu.SemaphoreType.DMA])
    def scatter(rows_hbm, dest_hbm, out_hbm, idx_v, rows_v, sem):
        wid = lax.axis_index("s") * SC_CORES + lax.axis_index("c")
        base = wid * per_worker

        @pl.loop(0, n_chunks)
        def _(j):
            off = base + j * SC_CHUNK
            pltpu.sync_copy(dest_hbm.at[:, pl.ds(off, SC_CHUNK)], idx_v)
            pltpu.sync_copy(rows_hbm.at[pl.ds(off, SC_CHUNK)], rows_v)
            copies = [pltpu.async_copy(rows_v, out_hbm.at[idx_v.at[k]], sem) for k in range(n_dst)]
            for c in copies:
                c.wait()

    return scatter(rows, dest)


def _expert_kernel(blk_ref, exp_ref, lo_ref, hi_ref, slot_ref, nxt_ref, xs_hbm, wg_hbm, wu_hbm, wd_hbm,
                   ys_hbm, wg_buf, wu_buf, wd_buf, wgu_s, wd_s, sem, xs_buf, xs_sem, ys_buf, ys_sem,
                   *, layer):
    i = pl.program_id(0)
    prev = jnp.maximum(i - 1, 0)

    def weight_copies(expert, slot):
        return [pltpu.make_async_copy(src.at[layer, expert], dst.at[slot], sem.at[slot])
                for src, dst in ((wg_hbm, wg_buf), (wu_hbm, wu_buf), (wd_hbm, wd_buf))]

    @pl.when(i == 0)
    def _():
        for c in weight_copies(exp_ref[0], slot_ref[0]):
            c.start()

    @pl.when(jnp.logical_or(i == 0, exp_ref[i] != exp_ref[prev]))
    def _():
        slot = slot_ref[i]
        for c in weight_copies(exp_ref[i], slot):
            c.wait()
        wgu_s[:, :EXPERT_DIM] = wg_buf[slot].astype(BF16)
        wgu_s[:, EXPERT_DIM:] = wu_buf[slot].astype(BF16)
        wd_s[...] = wd_buf[slot].astype(BF16)

        @pl.when(nxt_ref[i] >= 0)
        def _():
            for c in weight_copies(nxt_ref[i], 1 - slot):
                c.start()

    n_blocks = xs_hbm.shape[0] // EXPERT_ROWS
    blk = blk_ref[i]
    first = jnp.logical_or(i == 0, blk != blk_ref[prev])

    def rows_copy(block):
        slot = block % EXPERT_XS_SLOTS
        src = xs_hbm.at[pl.ds(pl.multiple_of(block * EXPERT_ROWS, EXPERT_ROWS), EXPERT_ROWS)]
        return pltpu.make_async_copy(src, xs_buf.at[slot], xs_sem.at[slot])

    @pl.when(i == 0)
    def _():
        for b0 in range(min(EXPERT_XS_SLOTS - 1, n_blocks)):
            rows_copy(b0).start()

    def result_copy(block):
        slot = block % EXPERT_XS_SLOTS
        dst = ys_hbm.at[pl.ds(pl.multiple_of(block * EXPERT_ROWS, EXPERT_ROWS), EXPERT_ROWS)]
        return pltpu.make_async_copy(ys_buf.at[slot], dst, ys_sem.at[slot])

    @pl.when(first)
    def _():
        rows_copy(blk).wait()

        @pl.when(blk + EXPERT_XS_SLOTS - 1 < n_blocks)
        def _():
            rows_copy(blk + EXPERT_XS_SLOTS - 1).start()

        @pl.when(blk >= EXPERT_XS_SLOTS)
        def _():
            result_copy(blk - EXPERT_XS_SLOTS).wait()

    xs_ref = xs_buf.at[blk % EXPERT_XS_SLOTS]
    ys_ref = ys_buf.at[blk % EXPERT_XS_SLOTS]
    lo = lo_ref[i]
    hi = hi_ref[i]

    def sub_block(r0):
        rows = slice(r0, r0 + EXPERT_SUB_ROWS)
        w = xs_ref[rows, :]
        xlo = lax.bitcast_convert_type(w.astype(jnp.int16), BF16)
        xhi = lax.bitcast_convert_type(lax.shift_right_logical(w, 16).astype(jnp.int16), BF16)
        gu = _dot(xlo, wgu_s[:HALF, :]) + _dot(xhi, wgu_s[HALF:, :])
        g = gu[:, :EXPERT_DIM]
        hb = (g * jax.nn.sigmoid(g) * gu[:, EXPERT_DIM:]).astype(BF16)
        y = _dot(hb, wd_s[...])
        packed = pltpu.pack_elementwise([y[:, :HALF], y[:, HALF:]], packed_dtype=BF16)
        row = r0 + lax.broadcasted_iota(I32, (EXPERT_SUB_ROWS, 1), 0)
        mine = jnp.logical_and(row >= lo, row < hi)
        kept = jnp.where(first, 0, ys_ref[rows, :])
        ys_ref[rows, :] = jnp.where(mine, packed, kept)

    def touched(r0):
        return jnp.logical_and(lo < r0 + EXPERT_SUB_ROWS, hi > r0)

    for r0 in range(0, EXPERT_ROWS, 2 * EXPERT_SUB_ROWS):
        r1 = r0 + EXPERT_SUB_ROWS
        t0, t1 = touched(r0), touched(r1)

        @pl.when(jnp.logical_and(t0, t1))
        def _():
            sub_block(r0)
            sub_block(r1)

        pl.when(jnp.logical_and(t0, jnp.logical_not(t1)))(functools.partial(sub_block, r0))
        pl.when(jnp.logical_and(jnp.logical_not(t0), t1))(functools.partial(sub_block, r1))

    n_items = pl.num_programs(0)
    final = i == n_items - 1
    block_done = jnp.logical_or(final, blk_ref[jnp.minimum(i + 1, n_items - 1)] != blk)

    @pl.when(block_done)
    def _():
        result_copy(blk).start()

    @pl.when(final)
    def _():
        for b0 in range(max(n_blocks - EXPERT_XS_SLOTS, 0), n_blocks):
            result_copy(b0).wait()


def _expert_items(counts, n_rows):
    n_blocks = n_rows // EXPERT_ROWS
    n_items = n_blocks + N_EXPERTS - 1
    end = jnp.cumsum(counts)
    start = end - counts
    first_blk = start // EXPERT_ROWS
    n_blk = jnp.where(counts > 0, (end - 1) // EXPERT_ROWS - first_blk + 1, 0)
    item_end = jnp.cumsum(n_blk)
    item_start = item_end - n_blk
    slot = jnp.arange(n_items, dtype=I32)
    e = jnp.minimum(jnp.sum((item_end[None, :] <= slot[:, None]).astype(I32), axis=1), N_EXPERTS - 1)
    onehot = (e[:, None] == jnp.arange(N_EXPERTS, dtype=I32)[None, :]).astype(I32)
    pick = lambda v: jnp.sum(onehot * v[None, :], axis=1)
    valid = slot < item_end[-1]
    blk = jnp.where(valid, pick(first_blk) + slot - pick(item_start), n_blocks - 1)
    lo = jnp.clip(pick(start) - blk * EXPERT_ROWS, 0, EXPERT_ROWS)
    hi = jnp.clip(pick(end) - blk * EXPERT_ROWS, 0, EXPERT_ROWS)
    last_e = jnp.max(jnp.where(counts > 0, jnp.arange(N_EXPERTS, dtype=I32), 0))
    e = jnp.where(valid, e, last_e)
    hi = jnp.where(valid, hi, 0)
    lo = jnp.where(valid, lo, 0)
    change = jnp.concatenate([jnp.ones((1,), I32), (e[1:] != e[:-1]).astype(I32)])
    slot = (jnp.cumsum(change) - 1) % 2
    later = jnp.where(jnp.arange(N_EXPERTS, dtype=I32)[None, :] > e[:, None], counts[None, :] > 0, False)
    nxt = jnp.where(jnp.any(later, axis=1), jnp.argmax(later, axis=1), -1)
    return tuple(a.astype(I32) for a in (blk, e, lo, hi, slot, nxt))


def _experts(xs, items, w_gate, w_up, w_down, layer):
    n_rows = xs.shape[0]
    n_items = items[0].shape[0]
    hbm = pl.BlockSpec(memory_space=pl.ANY)
    grid_spec = pltpu.PrefetchScalarGridSpec(
        num_scalar_prefetch=len(items), grid=(n_items,),
        in_specs=[hbm, hbm, hbm, hbm],
        out_specs=hbm,
        scratch_shapes=[pltpu.VMEM((2, D_MODEL, EXPERT_DIM), F32),
                        pltpu.VMEM((2, D_MODEL, EXPERT_DIM), F32),
                        pltpu.VMEM((2, EXPERT_DIM, D_MODEL), F32),
                        pltpu.VMEM((D_MODEL, 2 * EXPERT_DIM), BF16),
                        pltpu.VMEM((EXPERT_DIM, D_MODEL), BF16),
                        pltpu.SemaphoreType.DMA((2,)),
                        pltpu.VMEM((EXPERT_XS_SLOTS, EXPERT_ROWS, HALF), I32),
                        pltpu.SemaphoreType.DMA((EXPERT_XS_SLOTS,)),
                        pltpu.VMEM((EXPERT_XS_SLOTS, EXPERT_ROWS, HALF), I32),
                        pltpu.SemaphoreType.DMA((EXPERT_XS_SLOTS,))])
    return pl.pallas_call(
        functools.partial(_expert_kernel, layer=layer), grid_spec=grid_spec,
        out_shape=jax.ShapeDtypeStruct((n_rows, HALF), I32),
        compiler_params=_cparams("arbitrary"), name="experts")(*items, xs, w_gate, w_up, w_down)


def _moe_out_kernel(x_ref, yg_ref, wt_ref, sgu_ref, sd_ref, g_ref, b_ref, *rest):
    o_ref = rest[-1]
    x = x_ref[...]
    wt = wt_ref[...]
    lo = jnp.zeros((x.shape[0], HALF), F32)
    hi = jnp.zeros((x.shape[0], HALF), F32)
    for k in range(TOP_K):
        w = yg_ref[k]
        wk = wt[:, k:k + 1]
        lo = lo + wk * _unpack_lo(w)
        hi = hi + wk * _unpack_hi(w)
    gu = _dot(x.astype(BF16), sgu_ref[...])
    g = gu[:, :EXPERT_DIM]
    hs = (g * jax.nn.sigmoid(g) * gu[:, EXPERT_DIM:]).astype(BF16)
    ffn = jnp.concatenate([lo, hi], axis=1) + _dot(hs, sd_ref[...])
    o_ref[...] = _ln(DEEPNORM_ALPHA * x + ffn, g_ref[...], b_ref[...])


def _moe_out(x, yg, wts, sh_gate, sh_up, sh_down, g, b, first_tile, partial_out):
    T, D = x.shape
    tm = ROW_TILE
    n_tiles = yg.shape[1] // tm
    xrow = pl.BlockSpec((tm, D), lambda i: (i + first_tile, 0))
    full = lambda a: pl.BlockSpec(a.shape, lambda i: (0,) * a.ndim)
    sgu = jnp.concatenate([sh_gate, sh_up], axis=1).astype(BF16)
    sd = sh_down.astype(BF16)
    g = g.reshape(1, D)
    b = b.reshape(1, D)
    args = [x, yg, wts, sgu, sd, g, b]
    in_specs = [xrow, pl.BlockSpec((TOP_K, tm, HALF), lambda i: (0, i, 0)),
                pl.BlockSpec((tm, TOP_K), lambda i: (i, 0)), full(sgu), full(sd), full(g), full(b)]
    aliases = {}
    if partial_out is not None:
        args.append(partial_out)
        in_specs.append(pl.BlockSpec(memory_space=pl.ANY))
        aliases = {len(args) - 1: 0}
    return pl.pallas_call(
        _moe_out_kernel, grid=(n_tiles,), in_specs=in_specs,
        out_specs=xrow, out_shape=jax.ShapeDtypeStruct((T, D), F32),
        input_output_aliases=aliases,
        compiler_params=_cparams("parallel"), name="moe_out")(*args)


def _moe(x, router_w, router_b, w_gate, w_up, w_down, layer, sh_gate, sh_up, sh_down, g, b):
    T = x.shape[0]
    tiles = T // ROW_TILE // MOE_TOKEN_GROUPS
    tg = tiles * ROW_TILE
    out = None
    for grp in range(MOE_TOKEN_GROUPS):
        eidx, wts, pos, cnt, xp = _router(x, router_w, router_b, grp * tg, tg)
        counts = cnt[:, 0]
        seg_start = (jnp.cumsum(counts) - counts).astype(I32)
        dest = _dest_rows(eidx, pos, seg_start)
        xs = _scatter_rows(xp, dest)
        ys = _experts(xs, _expert_items(counts, tg * TOP_K), w_gate, w_up, w_down, layer)
        yg = _gather_rows(ys, dest.reshape(tg * TOP_K)).reshape(TOP_K, tg, HALF)
        out = _moe_out(x, yg, wts.T, sh_gate, sh_up, sh_down, g, b, grp * tiles, out)
    return out


def kernel(x, ln_in_g, ln_in_b, e_w_in, e_w_fourier, e_q_gain, e_k_gain, e_w_out, o_w_in, o_b_in, o_v_ln_g, o_v_ln_b, o_w_spatial, o_b_spatial, o_w_out, ln_mix_g, ln_mix_b, ln_ffn_g, ln_ffn_b, router_w, router_b, exp_w_gate, exp_w_up, exp_w_down, sh_w_gate, sh_w_up, sh_w_down):
    B, S, D = x.shape
    T = B * S
    h = x.reshape(T, D)
    for i in range(DEPTH):
        j = i // 2
        if i == 0:
            a, qt, k2, vt, h = _even_in(h, e_w_in[j], e_q_gain[j], e_k_gain[j], B, S, (ln_in_g, ln_in_b))
        elif i % 2 == 0:
            a, qt, k2, vt = _even_in(h, e_w_in[j], e_q_gain[j], e_k_gain[j], B, S)
        if i % 2 == 0:
            a_out = _fourier(a, e_w_fourier[j], B, S)
            attn = _attention(qt, k2, vt, B, S)
            h = _even_out(a_out, attn, e_w_out[j], h, ln_mix_g[i], ln_mix_b[i])
        else:
            h = _odd_mixer(h, o_w_in[j], o_b_in[j], o_v_ln_g[j], o_v_ln_b[j], o_w_spatial[j],
                           o_b_spatial[j], o_w_out[j], ln_mix_g[i], ln_mix_b[i])
        h = _moe(h, router_w[i], router_b[i], exp_w_gate, exp_w_up, exp_w_down, i,
                 sh_w_gate[i], sh_w_up[i], sh_w_down[i], ln_ffn_g[i], ln_ffn_b[i])
    return h.reshape(B, S, D)
```
